```python
import math
import jax, jax.numpy as jnp
from jax import lax
import numpy as np

D_MODEL = 1024
BATCH = 32
SEQ = 256
DEPTH = 2
DEC_BATCH = 2
DEC_SEQ = 4096
PAST_LEN = 512

GRID_W = 64
D_MIX = D_MODEL
SSD_WIDTH = D_MIX // 2
SSD_HEAD_DIM = 64
SSD_HEADS = SSD_WIDTH // SSD_HEAD_DIM
SSD_GROUPS = 2
SSD_STATE = 64
SSD_CONV = 5
SSD_CHUNK = 128
SSD_XBC = SSD_WIDTH + 2 * SSD_GROUPS * SSD_STATE
MLA_WIDTH = D_MIX - SSD_WIDTH
MLA_V_DIM = 64
MLA_HEADS = MLA_WIDTH // MLA_V_DIM
MLA_NOPE_DIM = 64
MLA_ROPE_DIM = 32
MLA_Q_RANK = 384
MLA_KV_RANK = 256
ROPE_THETA = 10000.0
Q_BLOCK = 128
D_FF = 2816
N_EXPERTS = 8
TOP_K = 2
D_FF_EXPERT = 1408
N_DENSE_LAYERS = (DEPTH + 1) // 2
N_MOE_LAYERS = DEPTH // 2
N_MOD = 6
EPS = 1e-6
IN_COLS = SSD_WIDTH + SSD_XBC + 2 * SSD_HEADS + MLA_Q_RANK + MLA_KV_RANK + MLA_ROPE_DIM

kernel_name = "hybrid_ssd_mla_prefix_diffusion_step"


def rms_norm(x, g):
    xf = x.astype(jnp.float32)
    y = xf * lax.rsqrt(jnp.mean(xf * xf, axis=-1, keepdims=True) + EPS)
    return (y * g.astype(jnp.float32)).astype(x.dtype)


def centred_depthwise_conv(u, w, b):
    pad = SSD_CONV // 2
    length = u.shape[1]
    up = jnp.pad(u, ((0, 0), (pad, pad), (0, 0)))
    out = b
    for k in range(SSD_CONV):
        out = out + up[:, k:k + length] * w[k]
    return out


def ssd_chunked(x, dt, a, bm, cm, h0):
    b, length, nh, p = x.shape
    n = bm.shape[-1]
    q = SSD_CHUNK
    nc = length // q
    xc = x.reshape(b, nc, q, nh, p)
    bc = bm.reshape(b, nc, q, nh, n)
    cc = cm.reshape(b, nc, q, nh, n)
    dtc = dt.reshape(b, nc, q, nh)
    la = jnp.cumsum((dtc * a).astype(jnp.float32), axis=2)
    lower = jnp.tril(jnp.ones((q, q), dtype=bool))[:, :, None]
    seg = la[:, :, :, None, :] - la[:, :, None, :, :]
    decay = jnp.exp(jnp.where(lower, seg, -jnp.inf)).astype(x.dtype)
    scores = jnp.einsum('bcthn,bcshn->bctsh', cc, bc) * decay * dtc[:, :, None, :, :]
    y_intra = jnp.einsum('bctsh,bcshp->bcthp', scores, xc)
    to_end = jnp.exp(la[:, :, -1:, :] - la).astype(x.dtype)
    chunk_states = jnp.einsum('bcshn,bcsh,bcshp->bchpn', bc, to_end * dtc, xc)
    chunk_decay = jnp.exp(la[:, :, -1, :]).astype(x.dtype)

    def step(h, inp):
        st, dec = inp
        return h * dec[:, :, None, None] + st, h

    h_final, h_start = lax.scan(step, h0, (jnp.moveaxis(chunk_states, 1, 0), jnp.moveaxis(chunk_decay, 1, 0)))
    h_start = jnp.moveaxis(h_start, 0, 1)
    y_inter = jnp.einsum('bcthn,bchpn->bcthp', cc, h_start) * jnp.exp(la).astype(x.dtype)[..., None]
    return (y_intra + y_inter).reshape(b, length, nh, p), h_final


def axial_rope_tables(n_tokens, dtype):
    rows = n_tokens // GRID_W
    row = jnp.repeat(jnp.arange(rows, dtype=jnp.float32), GRID_W)
    col = jnp.tile(jnp.arange(GRID_W, dtype=jnp.float32), rows)
    half = MLA_ROPE_DIM // 2
    inv = ROPE_THETA ** (-jnp.arange(0, half, 2, dtype=jnp.float32) / half)
    ar = row[:, None] * inv[None, :]
    ac = col[:, None] * inv[None, :]
    ang = jnp.concatenate([ar, ar, ac, ac], axis=-1)
    return jnp.cos(ang).astype(dtype), jnp.sin(ang).astype(dtype)


def apply_axial_rope(u, cos, sin):
    r1, r2, c1, c2 = jnp.split(u, 4, axis=-1)
    rot = jnp.concatenate([-r2, r1, -c2, c1], axis=-1)
    return u * cos + rot * sin


def blocked_attention(q, k, v):
    b, lq, nh, dqk = q.shape
    scale = dqk ** -0.5
    qb = jnp.moveaxis(q.reshape(b, lq // Q_BLOCK, Q_BLOCK, nh, dqk), 1, 0)

    def one_block(qblk):
        s = jnp.einsum('bqhd,bkhd->bhqk', qblk, k).astype(jnp.float32) * scale
        pr = jax.nn.softmax(s, axis=-1).astype(v.dtype)
        return jnp.einsum('bhqk,bkhd->bqhd', pr, v)

    out = lax.map(one_block, qb)
    return jnp.moveaxis(out, 0, 1).reshape(b, lq, nh, v.shape[-1])


def token_mixers(h, w_in, conv_w, conv_b, dt_bias, a_log, d_skip, ssd_norm, q_norm, w_uq, kv_norm, w_ukv,
                 mla_norm, w_out, ctx):
    b, length, _ = h.shape
    proj = h @ w_in
    s1 = SSD_WIDTH
    s2 = s1 + SSD_XBC
    s3 = s2 + 2 * SSD_HEADS
    s4 = s3 + MLA_Q_RANK
    s5 = s4 + MLA_KV_RANK
    z, xbc, dt_raw, cq, ckv, k_rope = jnp.split(proj, [s1, s2, s3, s4, s5], axis=-1)

    xbc = jax.nn.silu(centred_depthwise_conv(xbc, conv_w, conv_b))
    xs, bm, cm = jnp.split(xbc, [SSD_WIDTH, SSD_WIDTH + SSD_GROUPS * SSD_STATE], axis=-1)
    xs = xs.reshape(b, length, SSD_HEADS, SSD_HEAD_DIM)
    rep = SSD_HEADS // SSD_GROUPS
    bm = jnp.repeat(bm.reshape(b, length, SSD_GROUPS, SSD_STATE), rep, axis=2)
    cm = jnp.repeat(cm.reshape(b, length, SSD_GROUPS, SSD_STATE), rep, axis=2)
    dt = jax.nn.softplus(dt_raw.reshape(b, length, 2, SSD_HEADS) + dt_bias)
    a = -jnp.exp(a_log.astype(jnp.float32))
    if ctx is None:
        h0 = jnp.zeros((b, 2, SSD_HEADS, SSD_HEAD_DIM, SSD_STATE), h.dtype)
    else:
        h0 = ctx[2]
    y_f, h_f = ssd_chunked(xs, dt[:, :, 0], a[0], bm, cm, h0[:, 0])
    flip = lambda u: jnp.flip(u, axis=1)
    y_b, h_b = ssd_chunked(flip(xs), flip(dt[:, :, 1]), a[1], flip(bm), flip(cm), h0[:, 1])
    y_ssd = y_f + flip(y_b) + d_skip[:, None] * xs
    y_ssd = y_ssd * jax.nn.silu(z.reshape(b, length, SSD_HEADS, SSD_HEAD_DIM))
    y_ssd = rms_norm(y_ssd.reshape(b, length, SSD_WIDTH), ssd_norm)

    q = (rms_norm(cq, q_norm) @ w_uq).reshape(b, length, MLA_HEADS, MLA_NOPE_DIM + MLA_ROPE_DIM)
    q_nope, q_rope = jnp.split(q, [MLA_NOPE_DIM], axis=-1)
    ckv_n = rms_norm(ckv, kv_norm)
    if ctx is None:
        ckv_all, kr_all = ckv_n, k_rope
    else:
        cos, sin = axial_rope_tables(length, h.dtype)
        q_rope = apply_axial_rope(q_rope, cos[:, None, :], sin[:, None, :])
        kr_lat = apply_axial_rope(k_rope, cos, sin)
        ckv_all = jnp.concatenate([ctx[0], ckv_n], axis=1)
        kr_all = jnp.concatenate([ctx[1], kr_lat], axis=1)
    n_keys = ckv_all.shape[1]
    kv = (ckv_all @ w_ukv).reshape(b, n_keys, MLA_HEADS, MLA_NOPE_DIM + MLA_V_DIM)
    k_nope, v = jnp.split(kv, [MLA_NOPE_DIM], axis=-1)
    k_full = jnp.concatenate(
        [k_nope, jnp.broadcast_to(kr_all[:, :, None, :], (b, n_keys, MLA_HEADS, MLA_ROPE_DIM))], axis=-1)
    q_full = jnp.concatenate([q_nope, q_rope], axis=-1)
    o = blocked_attention(q_full, k_full, v).reshape(b, length, MLA_WIDTH)
    o = rms_norm(o, mla_norm)

    y = jnp.concatenate([y_ssd, o], axis=-1) @ w_out
    if ctx is None:
        return y, (ckv_n, k_rope, jnp.stack([h_f, h_b], axis=1))
    return y, None


def swiglu(u, w_gate, w_up, w_down):
    return (jax.nn.silu(u @ w_gate) * (u @ w_up)) @ w_down


def moe_swiglu(u, router, w_gate, w_up, w_down):
    logits = (u @ router).astype(jnp.float32)
    top_v, top_i = lax.top_k(logits, TOP_K)
    top_w = jax.nn.softmax(top_v, axis=-1)
    combine = jnp.sum(jax.nn.one_hot(top_i, N_EXPERTS, dtype=jnp.float32) * top_w[..., None], axis=-2)
    hidden = jax.nn.silu(jnp.einsum('bld,edf->blef', u, w_gate)) * jnp.einsum('bld,edf->blef', u, w_up)
    hidden = hidden * combine.astype(u.dtype)[..., None]
    return jnp.einsum('blef,efd->bld', hidden, w_down)


def trunk_layer(x, mod, norms, mix_params, ffn, ctx):
    shift1, scale1, gate1, shift2, scale2, gate2 = jnp.split(mod, N_MOD, axis=-1)
    g_pre1, g_post1, g_pre2, g_post2 = norms
    h = rms_norm(x, g_pre1) * (1 + scale1) + shift1
    y, ctx_out = token_mixers(h, *mix_params, ctx=ctx)
    x = x + gate1 * rms_norm(y, g_post1)
    h = rms_norm(x, g_pre2) * (1 + scale2) + shift2
    x = x + gate2 * rms_norm(ffn(h), g_post2)
    return x, ctx_out


def setup_inputs(seed: int = 0) -> dict:
    key = jax.random.key(seed)
    ks = jax.random.split(key, 48)
    cnt = [0]

    def nk():
        cnt[0] += 1
        return ks[cnt[0]]

    def nrm(shape, scale=1.0):
        return jax.random.normal(nk(), shape, jnp.float32) * scale

    def gain(shape):
        return 1.0 + nrm(shape, 0.05)

    dt0 = jnp.exp(jax.random.uniform(nk(), (DEPTH, 2, SSD_HEADS), jnp.float32,
                                     minval=math.log(1e-3), maxval=math.log(1e-1)))
    dt_bias = dt0 + jnp.log(-jnp.expm1(-dt0))
    a_log = jnp.log(jax.random.uniform(nk(), (DEPTH, 2, SSD_HEADS), jnp.float32, minval=1.0, maxval=16.0))
    return {
        "x_prompt": nrm((BATCH, SEQ, D_MODEL)),
        "x_sample": nrm((DEC_BATCH, DEC_SEQ, D_MODEL)),
        "cache_ckv": nrm((DEC_BATCH, DEPTH, PAST_LEN, MLA_KV_RANK)),
        "cache_krope": nrm((DEC_BATCH, DEPTH, PAST_LEN, MLA_ROPE_DIM)),
        "state_ssm": nrm((DEC_BATCH, DEPTH, 2, SSD_HEADS, SSD_HEAD_DIM, SSD_STATE), 0.5),
        "c": nrm((DEC_BATCH, D_MODEL)),
        "c_ctx": nrm((D_MODEL,)),
        "w_mod": nrm((DEPTH, D_MODEL, N_MOD * D_MODEL), 0.5 * D_MODEL ** -0.5),
        "b_mod": nrm((DEPTH, N_MOD * D_MODEL), 0.02),
        "norm_pre_mix": gain((DEPTH, D_MODEL)),
        "norm_post_mix": gain((DEPTH, D_MODEL)),
        "norm_pre_ffn": gain((DEPTH, D_MODEL)),
        "norm_post_ffn": gain((DEPTH, D_MODEL)),
        "w_in": nrm((DEPTH, D_MODEL, IN_COLS), D_MODEL ** -0.5),
        "conv_w": nrm((DEPTH, SSD_CONV, SSD_XBC), SSD_CONV ** -0.5),
        "conv_b": nrm((DEPTH, SSD_XBC), 0.02),
        "dt_bias": dt_bias,
        "a_log": a_log,
        "d_skip": gain((DEPTH, SSD_HEADS)),
        "ssd_norm": gain((DEPTH, SSD_WIDTH)),
        "q_norm": gain((DEPTH, MLA_Q_RANK)),
        "w_uq": nrm((DEPTH, MLA_Q_RANK, MLA_HEADS * (MLA_NOPE_DIM + MLA_ROPE_DIM)), MLA_Q_RANK ** -0.5),
        "kv_norm": gain((DEPTH, MLA_KV_RANK)),
        "w_ukv": nrm((DEPTH, MLA_KV_RANK, MLA_HEADS * (MLA_NOPE_DIM + MLA_V_DIM)), MLA_KV_RANK ** -0.5),
        "mla_norm": gain((DEPTH, MLA_WIDTH)),
        "w_out": nrm((DEPTH, D_MIX, D_MODEL), D_MIX ** -0.5),
        "ffn_w_gate": nrm((N_DENSE_LAYERS, D_MODEL, D_FF), D_MODEL ** -0.5),
        "ffn_w_up": nrm((N_DENSE_LAYERS, D_MODEL, D_FF), D_MODEL ** -0.5),
        "ffn_w_down": nrm((N_DENSE_LAYERS, D_FF, D_MODEL), D_FF ** -0.5),
        "moe_router": nrm((N_MOE_LAYERS, D_MODEL, N_EXPERTS), D_MODEL ** -0.5),
        "moe_w_gate": nrm((N_MOE_LAYERS, N_EXPERTS, D_MODEL, D_FF_EXPERT), D_MODEL ** -0.5),
        "moe_w_up": nrm((N_MOE_LAYERS, N_EXPERTS, D_MODEL, D_FF_EXPERT), D_MODEL ** -0.5),
        "moe_w_down": nrm((N_MOE_LAYERS, N_EXPERTS, D_FF_EXPERT, D_MODEL), D_FF_EXPERT ** -0.5),
    }


def reference(x_prompt, x_sample, cache_ckv, cache_krope, state_ssm, c, c_ctx, w_mod, b_mod,
              norm_pre_mix, norm_post_mix, norm_pre_ffn, norm_post_ffn, w_in, conv_w, conv_b, dt_bias, a_log,
              d_skip, ssd_norm, q_norm, w_uq, kv_norm, w_ukv, mla_norm, w_out, ffn_w_gate, ffn_w_up, ffn_w_down,
              moe_router, moe_w_gate, moe_w_up, moe_w_down):
    xp = x_prompt
    xs = x_sample
    silu_c = jax.nn.silu(c)
    silu_ctx = jax.nn.silu(c_ctx)
    new_ckv, new_kr, new_ssm = [], [], []
    for i in range(DEPTH):
        norms = (norm_pre_mix[i], norm_post_mix[i], norm_pre_ffn[i], norm_post_ffn[i])
        mix_params = (w_in[i], conv_w[i], conv_b[i], dt_bias[i], a_log[i], d_skip[i], ssd_norm[i], q_norm[i],
                      w_uq[i], kv_norm[i], w_ukv[i], mla_norm[i], w_out[i])
        j = i // 2
        if i % 2 == 0:
            ffn = lambda u, j=j: swiglu(u, ffn_w_gate[j], ffn_w_up[j], ffn_w_down[j])
        else:
            ffn = lambda u, j=j: moe_swiglu(u, moe_router[j], moe_w_gate[j], moe_w_up[j], moe_w_down[j])
        mod_ctx = (silu_ctx @ w_mod[i] + b_mod[i])[None, None, :]
        xp, (ckv_i, kr_i, ssm_i) = trunk_layer(xp, mod_ctx, norms, mix_params, ffn, None)
        new_ckv.append(ckv_i)
        new_kr.append(kr_i)
        new_ssm.append(ssm_i)
        mod_lat = (silu_c @ w_mod[i] + b_mod[i])[:, None, :]
        xs, _ = trunk_layer(xs, mod_lat, norms, mix_params, ffn,
                            (cache_ckv[:, i], cache_krope[:, i], state_ssm[:, i]))
    return (xp, xs, jnp.stack(new_ckv, axis=1), jnp.stack(new_kr, axis=1), jnp.stack(new_ssm, axis=1))
```

```python
import functools
import math

import jax
import jax.numpy as jnp
from jax import lax
from jax.experimental import pallas as pl
from jax.experimental.pallas import tpu as pltpu

F32 = jnp.float32
BF16 = jnp.bfloat16

D_MODEL = 1024
GRID_W = 64
SSD_WIDTH = 512
SSD_HEAD_DIM = 64
SSD_HEADS = 8
SSD_GROUPS = 2
SSD_STATE = 64
SSD_CONV = 5
SSD_CHUNK = 128
SSD_XBC = SSD_WIDTH + 2 * SSD_GROUPS * SSD_STATE
MLA_WIDTH = 512
MLA_V_DIM = 64
MLA_HEADS = 8
MLA_NOPE_DIM = 64
MLA_ROPE_DIM = 32
MLA_Q_RANK = 384
MLA_KV_RANK = 256
ROPE_THETA = 10000.0
N_EXPERTS = 8
N_MOD = 6
EPS = 1e-6

LANE = 128
SUBLANE = 8
HEAD_PAD = 128
ROPE_LANE0 = MLA_NOPE_DIM
C_Z = 0
C_XBC = C_Z + SSD_WIDTH
C_CQ = C_XBC + SSD_XBC
C_CKV = C_CQ + MLA_Q_RANK
C_TA = C_CKV + MLA_KV_RANK
C_TB = C_TA + LANE
IN_PAD = C_TB + LANE

VMEM_LIMIT = 56 * 1024 * 1024

ROW_TILE = 512
ATTN_Q_TILE = 256
FFN_ROW_TILE = 512

NT_DIMS = (((1,), (1,)), ((), ()))
TN_DIMS = (((0,), (0,)), ((), ()))


def _cparams(*sem):
    return pltpu.CompilerParams(dimension_semantics=sem, vmem_limit_bytes=VMEM_LIMIT)


def _silu(x):
    return x / (1.0 + jnp.exp(-x))


def _softplus(x):
    return jnp.maximum(x, 0.0) + jnp.log(1.0 + jnp.exp(-jnp.abs(x)))


def _rms_rows(x, g):
    ms = jnp.mean(x * x, axis=-1, keepdims=True)
    return x * lax.rsqrt(ms + EPS) * g


def _dot(a, b):
    return jnp.dot(a, b, preferred_element_type=F32)


def _dot_nt(a, b):
    return lax.dot_general(a, b, NT_DIMS, preferred_element_type=F32)


def _dot_tn(a, b):
    return lax.dot_general(a, b, TN_DIMS, preferred_element_type=F32)


def _split3(x):
    hi = x.astype(BF16)
    r1 = x - hi.astype(F32)
    mid = r1.astype(BF16)
    lo = (r1 - mid.astype(F32)).astype(BF16)
    return hi, mid, lo


def _mod_kernel(c_ref, w_ref, b_ref, o_ref):
    s = _silu(c_ref[...]).astype(BF16)
    o_ref[0] = _dot(s, w_ref[0].astype(BF16)) + b_ref[0]


def _modulation(cvec, w_mod, b_mod):
    depth, d, n = w_mod.shape
    tn = 1536
    return pl.pallas_call(
        _mod_kernel,
        grid=(depth, n // tn),
        in_specs=[
            pl.BlockSpec((SUBLANE, d), lambda l, j: (0, 0)),
            pl.BlockSpec((1, d, tn), lambda l, j: (l, 0, j)),
            pl.BlockSpec((1, 1, tn), lambda l, j: (l, 0, j)),
        ],
        out_specs=pl.BlockSpec((1, SUBLANE, tn), lambda l, j: (l, 0, j)),
        out_shape=jax.ShapeDtypeStruct((depth, SUBLANE, n), F32),
        compiler_params=_cparams("arbitrary", "arbitrary"),
        name="modulation",
    )(cvec, w_mod, b_mod.reshape(depth, 1, n))


def _inproj_kernel(x_ref, mod_ref, gpre_ref, win_ref, wdt_ref, dtb_row_ref, dtb_col_ref,
                   qn_ref, wq_ref, wqrot_ref, kvn_ref, wuk_ref, wuv_ref,
                   cosq_ref, sinq_ref, cosk_ref, sink_ref,
                   z_ref, xbc_ref, dt_ref, dtt_ref, qt_ref, k_ref, vt_ref, ckvn_ref, kr_ref):
    mod = mod_ref[0]
    shift = mod[:, 0:D_MODEL]
    scale = mod[:, D_MODEL:2 * D_MODEL]
    h = _rms_rows(x_ref[...], gpre_ref[...]) * (1.0 + scale) + shift
    hb = h.astype(BF16)
    proj = _dot(hb, win_ref[...])
    z_ref[...] = proj[:, C_Z:C_XBC]
    xbc_ref[...] = proj[:, C_XBC:C_CQ]
    cqn = _rms_rows(proj[:, C_CQ:C_CKV], qn_ref[...]).astype(BF16)
    ckvn = _rms_rows(proj[:, C_CKV:C_TA], kvn_ref[...])
    ckvn_ref[...] = ckvn
    ckvb = ckvn.astype(BF16)
    ta = proj[:, C_TA:C_TB]
    tb = proj[:, C_TB:IN_PAD]
    dt_ref[...] = _softplus(ta + dtb_row_ref[...])
    kr_ref[...] = ta[:, ROPE_LANE0:ROPE_LANE0 + MLA_ROPE_DIM]
    kr_rot = ta * cosk_ref[...] + tb * sink_ref[...]
    knp = _dot(ckvb, wuk_ref[...])
    for hd in range(MLA_HEADS):
        sl = slice(hd * HEAD_PAD, (hd + 1) * HEAD_PAD)
        k_ref[:, sl] = (knp[:, sl] + kr_rot).astype(BF16)
    vt_ref[...] = _dot_nt(wuv_ref[...], ckvb).astype(BF16)
    qt = _dot_nt(wq_ref[...], cqn)
    qrt = _dot_nt(wqrot_ref[...], cqn)
    cosq = cosq_ref[...]
    sinq = sinq_ref[...]
    for hd in range(MLA_HEADS):
        sl = slice(hd * HEAD_PAD, (hd + 1) * HEAD_PAD)
        qt_ref[sl, :] = (qt[sl, :] * cosq + qrt[sl, :] * sinq).astype(BF16)
    dtt_ref[...] = _softplus(_dot_nt(wdt_ref[...], hb) + dtb_col_ref[...])


def _inproj(x, mod3, mod_row_fn, lw, tabs, tab_blocks):
    t = x.shape[0]
    tm = ROW_TILE
    nb = t // tm
    cosq, sinq, cosk, sink = tabs
    ntab = tab_blocks
    const = lambda i: (0, 0)
    row = lambda i: (i, 0)
    col = lambda i: (0, i)
    in_specs = [
        pl.BlockSpec((tm, D_MODEL), row),
        pl.BlockSpec((1, 1, N_MOD * D_MODEL), lambda i: (mod_row_fn(i), 0, 0)),
        pl.BlockSpec((1, D_MODEL), const),
        pl.BlockSpec((D_MODEL, IN_PAD), const),
        pl.BlockSpec((2 * SSD_HEADS, D_MODEL), const),
        pl.BlockSpec((1, LANE), const),
        pl.BlockSpec((2 * SSD_HEADS, 1), const),
        pl.BlockSpec((1, MLA_Q_RANK), const),
        pl.BlockSpec((MLA_HEADS * HEAD_PAD, MLA_Q_RANK), const),
        pl.BlockSpec((MLA_HEADS * HEAD_PAD, MLA_Q_RANK), const),
        pl.BlockSpec((1, MLA_KV_RANK), const),
        pl.BlockSpec((MLA_KV_RANK, MLA_HEADS * HEAD_PAD), const),
        pl.BlockSpec((MLA_WIDTH, MLA_KV_RANK), const),
        pl.BlockSpec((HEAD_PAD, tm), lambda i: (0, i % ntab)),
        pl.BlockSpec((HEAD_PAD, tm), lambda i: (0, i % ntab)),
        pl.BlockSpec((tm, LANE), lambda i: (i % ntab, 0)),
        pl.BlockSpec((tm, LANE), lambda i: (i % ntab, 0)),
    ]
    out_specs = [
        pl.BlockSpec((tm, SSD_WIDTH), row),
        pl.BlockSpec((tm, SSD_XBC), row),
        pl.BlockSpec((tm, LANE), row),
        pl.BlockSpec((2 * SSD_HEADS, tm), col),
        pl.BlockSpec((MLA_HEADS * HEAD_PAD, tm), col),
        pl.BlockSpec((tm, MLA_HEADS * HEAD_PAD), row),
        pl.BlockSpec((MLA_WIDTH, tm), col),
        pl.BlockSpec((tm, MLA_KV_RANK), row),
        pl.BlockSpec((tm, MLA_ROPE_DIM), row),
    ]
    out_shape = [
        jax.ShapeDtypeStruct((t, SSD_WIDTH), F32),
        jax.ShapeDtypeStruct((t, SSD_XBC), F32),
        jax.ShapeDtypeStruct((t, LANE), F32),
        jax.ShapeDtypeStruct((2 * SSD_HEADS, t), F32),
        jax.ShapeDtypeStruct((MLA_HEADS * HEAD_PAD, t), BF16),
        jax.ShapeDtypeStruct((t, MLA_HEADS * HEAD_PAD), BF16),
        jax.ShapeDtypeStruct((MLA_WIDTH, t), BF16),
        jax.ShapeDtypeStruct((t, MLA_KV_RANK), F32),
        jax.ShapeDtypeStruct((t, MLA_ROPE_DIM), F32),
    ]
    return pl.pallas_call(
        _inproj_kernel,
        grid=(nb,),
        in_specs=in_specs,
        out_specs=out_specs,
        out_shape=out_shape,
        compiler_params=_cparams("arbitrary"),
        name="inproj",
    )(x, mod3, lw["g_pre1"], lw["w_in"], lw["w_dt"], lw["dtb_row"], lw["dtb_col"],
      lw["q_norm"], lw["w_q"], lw["w_qrot"], lw["kv_norm"], lw["w_uk"], lw["w_uv"],
      cosq, sinq, cosk, sink)


def _kvcache_kernel(ckv_ref, kr_ref, wuk_ref, wuv_ref, k_ref, vt_ref):
    ckvb = ckv_ref[...].astype(BF16)
    knp = _dot(ckvb, wuk_ref[...])
    kr = kr_ref[...]
    for hd in range(MLA_HEADS):
        sl = slice(hd * HEAD_PAD, (hd + 1) * HEAD_PAD)
        k_ref[:, sl] = (knp[:, sl] + kr).astype(BF16)
    vt_ref[...] = _dot_nt(wuv_ref[...], ckvb).astype(BF16)


def _kvcache(ckv, kr_tile, lw):
    n = ckv.shape[0]
    tm = 512
    return pl.pallas_call(
        _kvcache_kernel,
        grid=(n // tm,),
        in_specs=[
            pl.BlockSpec((tm, MLA_KV_RANK), lambda i: (i, 0)),
            pl.BlockSpec((tm, LANE), lambda i: (i, 0)),
            pl.BlockSpec((MLA_KV_RANK, MLA_HEADS * HEAD_PAD), lambda i: (0, 0)),
            pl.BlockSpec((MLA_WIDTH, MLA_KV_RANK), lambda i: (0, 0)),
        ],
        out_specs=[
            pl.BlockSpec((tm, MLA_HEADS * HEAD_PAD), lambda i: (i, 0)),
            pl.BlockSpec((MLA_WIDTH, tm), lambda i: (0, i)),
        ],
        out_shape=[
            jax.ShapeDtypeStruct((n, MLA_HEADS * HEAD_PAD), BF16),
            jax.ShapeDtypeStruct((MLA_WIDTH, n), BF16),
        ],
        compiler_params=_cparams("arbitrary"),
        name="kvcache",
    )(ckv, kr_tile, lw["w_uk"], lw["w_uv"])


def _ssd_kernel(*refs, backward, cps, has_h0, nchunks):
    it = iter(refs)
    xbc_ref, prev_ref, next_ref, dt_ref, dtt_ref = (next(it) for _ in range(5))
    cw_ref, cb_ref, alog_row_ref, alog_col_ref = (next(it) for _ in range(4))
    h0_ref = next(it) if has_h0 else None
    if backward:
        z_ref, yf_ref, dskip_ref, gn_ref = (next(it) for _ in range(4))
    y_ref, hout_ref, st_ref = next(it), next(it), next(it)

    q = SSD_CHUNK
    i = pl.program_id(0)
    g = (nchunks - 1 - i) if backward else i
    pos = g % cps
    seq_first = pos == 0
    seq_last = pos == cps - 1
    scan_first = seq_last if backward else seq_first

    prev = jnp.where(seq_first, 0.0, prev_ref[...])
    nxt = jnp.where(seq_last, 0.0, next_ref[...])
    ext = jnp.concatenate([prev, xbc_ref[...], nxt], axis=0)
    cw = cw_ref[...]
    acc = cb_ref[...] + ext[SUBLANE - 2:SUBLANE - 2 + q] * cw[0:1]
    for k in range(1, SSD_CONV):
        o = SUBLANE - 2 + k
        acc = acc + ext[o:o + q] * cw[k:k + 1]
    xc = _silu(acc)
    xs = xc[:, :SSD_WIDTH]
    xsb = xs.astype(BF16)
    bmb = xc[:, SSD_WIDTH:SSD_WIDTH + SSD_GROUPS * SSD_STATE].astype(BF16)
    cmb = xc[:, SSD_WIDTH + SSD_GROUPS * SSD_STATE:].astype(BF16)

    dt = dt_ref[...]
    dtt = dtt_ref[...]
    a_row = -jnp.exp(alog_row_ref[...])
    a_col = -jnp.exp(alog_col_ref[...])
    dta = dt * a_row
    dtat = dtt * a_col
    r_i = lax.broadcasted_iota(jnp.int32, (q, q), 0)
    c_i = lax.broadcasted_iota(jnp.int32, (q, q), 1)
    lower = r_i >= c_i
    upper = r_i <= c_i
    tril = jnp.where(lower, 1.0, 0.0).astype(BF16)
    triu = jnp.where(upper, 1.0, 0.0).astype(BF16)
    m_col, m_row = (triu, tril) if backward else (tril, triu)
    la_col = sum(_dot(m_col, p) for p in _split3(dta))
    la_row = sum(_dot(p, m_row) for p in _split3(dtat))
    end = 0 if backward else q - 1
    tot_row = la_col[end:end + 1, :]
    w_end = jnp.exp(tot_row - la_col) * dt
    e_col = jnp.exp(la_col)
    e_tot = jnp.exp(tot_row)
    mask = upper if backward else lower

    @pl.when(scan_first)
    def _():
        if has_h0:
            st_ref[...] = h0_ref[0]
        else:
            st_ref[...] = jnp.zeros_like(st_ref)

    doff = SSD_HEADS if backward else 0
    rep = SSD_HEADS // SSD_GROUPS
    ys = []
    for hd in range(SSD_HEADS):
        j = doff + hd
        grp = hd // rep
        cg = cmb[:, grp * SSD_STATE:(grp + 1) * SSD_STATE]
        bg = bmb[:, grp * SSD_STATE:(grp + 1) * SSD_STATE]
        xh = xsb[:, hd * SSD_HEAD_DIM:(hd + 1) * SSD_HEAD_DIM]
        cbm = _dot_nt(cg, bg)
        seg = la_col[:, j:j + 1] - la_row[j:j + 1, :]
        decay = jnp.exp(jnp.where(mask, seg, -jnp.inf))
        scores = cbm * decay * dtt[j:j + 1, :]
        y_intra = _dot(scores.astype(BF16), xh)
        hs = st_ref[hd]
        y_inter = _dot_nt(cg, hs.astype(BF16)) * e_col[:, j:j + 1]
        bw = (bg.astype(F32) * w_end[:, j:j + 1]).astype(BF16)
        cst = _dot_tn(xh, bw)
        st_ref[hd] = hs * e_tot[:, j:j + 1] + cst
        ys.append(y_intra + y_inter)
    y = jnp.concatenate(ys, axis=-1)
    hout_ref[0] = st_ref[...]
    if backward:
        y = y + yf_ref[...] + dskip_ref[...] * xs
        y = y * _silu(z_ref[...])
        y_ref[...] = _rms_rows(y, gn_ref[...]).astype(BF16)
    else:
        y_ref[...] = y


def _ssd(xbc, dt, dtt, lw, cps, direction, h0=None, z=None, yf=None):
    t = xbc.shape[0]
    q = SSD_CHUNK
    nchunks = t // q
    nseq = nchunks // cps
    backward = direction == 1
    hb = q // SUBLANE
    n8 = t // SUBLANE
    gi = (lambda i: nchunks - 1 - i) if backward else (lambda i: i)
    const = lambda i: (0, 0)
    in_specs = [
        pl.BlockSpec((q, SSD_XBC), lambda i: (gi(i), 0)),
        pl.BlockSpec((SUBLANE, SSD_XBC), lambda i: (jnp.maximum(gi(i) * hb - 1, 0), 0)),
        pl.BlockSpec((SUBLANE, SSD_XBC), lambda i: (jnp.minimum((gi(i) + 1) * hb, n8 - 1), 0)),
        pl.BlockSpec((q, LANE), lambda i: (gi(i), 0)),
        pl.BlockSpec((2 * SSD_HEADS, q), lambda i: (0, gi(i))),
        pl.BlockSpec((SSD_CONV, SSD_XBC), const),
        pl.BlockSpec((1, SSD_XBC), const),
        pl.BlockSpec((1, LANE), const),
        pl.BlockSpec((2 * SSD_HEADS, 1), const),
    ]
    args = [xbc, xbc, xbc, dt, dtt, lw["conv_w"], lw["conv_b"], lw["alog_row"], lw["alog_col"]]
    st_block = (1, SSD_HEADS, SSD_HEAD_DIM, SSD_STATE)
    if h0 is not None:
        in_specs.append(pl.BlockSpec(st_block, lambda i: (gi(i) // cps, 0, 0, 0)))
        args.append(h0)
    if backward:
        in_specs += [
            pl.BlockSpec((q, SSD_WIDTH), lambda i: (gi(i), 0)),
            pl.BlockSpec((q, SSD_WIDTH), lambda i: (gi(i), 0)),
            pl.BlockSpec((1, SSD_WIDTH), const),
            pl.BlockSpec((1, SSD_WIDTH), const),
        ]
        args += [z, yf, lw["dskip_row"], lw["ssd_norm"]]
    out_specs = [
        pl.BlockSpec((q, SSD_WIDTH), lambda i: (gi(i), 0)),
        pl.BlockSpec(st_block, lambda i: (gi(i) // cps, 0, 0, 0)),
    ]
    out_shape = [
        jax.ShapeDtypeStruct((t, SSD_WIDTH), BF16 if backward else F32),
        jax.ShapeDtypeStruct((nseq, SSD_HEADS, SSD_HEAD_DIM, SSD_STATE), F32),
    ]
    kern = functools.partial(_ssd_kernel, backward=backward, cps=cps,
                             has_h0=h0 is not None, nchunks=nchunks)
    return pl.pallas_call(
        kern,
        grid=(nchunks,),
        in_specs=in_specs,
        out_specs=out_specs,
        out_shape=out_shape,
        scratch_shapes=[pltpu.VMEM((SSD_HEADS, SSD_HEAD_DIM, SSD_STATE), F32)],
        compiler_params=_cparams("arbitrary"),
        name="ssd_bwd" if backward else "ssd_fwd",
    )(*args)


def _attn_kernel(*refs, heads, has_cache):
    if has_cache:
        qt_ref, k_ref, vt_ref, kc_ref, vct_ref, o_ref = refs
    else:
        qt_ref, k_ref, vt_ref, o_ref = refs
    for hd in range(heads):
        q = qt_ref[hd * HEAD_PAD:(hd + 1) * HEAD_PAD, :]
        s = _dot(k_ref[:, hd * HEAD_PAD:(hd + 1) * HEAD_PAD], q)
        m = jnp.max(s, axis=0, keepdims=True)
        if has_cache:
            sc = _dot(kc_ref[:, hd * HEAD_PAD:(hd + 1) * HEAD_PAD], q)
            m = jnp.maximum(m, jnp.max(sc, axis=0, keepdims=True))
        p = jnp.exp(s - m)
        l = jnp.sum(p, axis=0, keepdims=True)
        vs = slice(hd * MLA_V_DIM, (hd + 1) * MLA_V_DIM)
        o = _dot(vt_ref[vs, :], p.astype(BF16))
        if has_cache:
            pc = jnp.exp(sc - m)
            l = l + jnp.sum(pc, axis=0, keepdims=True)
            o = o + _dot(vct_ref[vs, :], pc.astype(BF16))
        o_ref[vs, :] = o / l


def _attention(qt, k, vt, seq_len, heads_per_step, cache=None):
    t = k.shape[0]
    nseq = t // seq_len
    tq = min(ATTN_Q_TILE, seq_len)
    nq = seq_len // tq
    g = heads_per_step
    in_specs = [
        pl.BlockSpec((g * HEAD_PAD, tq), lambda s, h, j: (h, s * nq + j)),
        pl.BlockSpec((seq_len, g * HEAD_PAD), lambda s, h, j: (s, h)),
        pl.BlockSpec((g * MLA_V_DIM, seq_len), lambda s, h, j: (h, s)),
    ]
    args = [qt, k, vt]
    if cache is not None:
        kc, vct = cache
        past = kc.shape[0] // nseq
        in_specs += [
            pl.BlockSpec((past, g * HEAD_PAD), lambda s, h, j: (s, h)),
            pl.BlockSpec((g * MLA_V_DIM, past), lambda s, h, j: (h, s)),
        ]
        args += [kc, vct]
    kern = functools.partial(_attn_kernel, heads=g, has_cache=cache is not None)
    return pl.pallas_call(
        kern,
        grid=(nseq, MLA_HEADS // g, nq),
        in_specs=in_specs,
        out_specs=pl.BlockSpec((g * MLA_V_DIM, tq), lambda s, h, j: (h, s * nq + j)),
        out_shape=jax.ShapeDtypeStruct((MLA_WIDTH, t), F32),
        compiler_params=_cparams("arbitrary", "arbitrary", "arbitrary"),
        name="attention",
    )(*args)


def _outproj_kernel(*refs, has_router):
    if has_router:
        (y_ref, ot_ref, x_ref, mod_ref, wa_ref, wb_ref, gm_ref, gpost_ref, gpre2_ref, rt_ref,
         x1_ref, h2_ref, comb_ref) = refs
    else:
        (y_ref, ot_ref, x_ref, mod_ref, wa_ref, wb_ref, gm_ref, gpost_ref, gpre2_ref,
         x1_ref, h2_ref) = refs
    mod = mod_ref[0]
    gate1 = mod[:, 2 * D_MODEL:3 * D_MODEL]
    shift2 = mod[:, 3 * D_MODEL:4 * D_MODEL]
    scale2 = mod[:, 4 * D_MODEL:5 * D_MODEL]
    ot = ot_ref[...]
    ms = jnp.mean(ot * ot, axis=0, keepdims=True)
    on = (ot * lax.rsqrt(ms + EPS) * gm_ref[...]).astype(BF16)
    y = _dot(y_ref[...], wa_ref[...]) + _dot_tn(on, wb_ref[...])
    x1 = x_ref[...] + gate1 * _rms_rows(y, gpost_ref[...])
    x1_ref[...] = x1
    h2 = _rms_rows(x1, gpre2_ref[...]) * (1.0 + scale2) + shift2
    h2_ref[...] = h2.astype(BF16)
    if has_router:
        hh, hm, _ = _split3(h2)
        rh, rm, _ = _split3(rt_ref[...])
        logits = _dot(hh, rh) + (_dot(hm, rh) + _dot(hh, rm))
        lane = lax.broadcasted_iota(jnp.int32, logits.shape, 1).astype(F32)
        neg = jnp.float32(-jnp.inf)
        lg = jnp.where(lane < N_EXPERTS, logits, neg)
        m1 = jnp.max(lg, axis=-1, keepdims=True)
        i1 = jnp.min(jnp.where(lg == m1, lane, float(LANE)), axis=-1, keepdims=True)
        lg2 = jnp.where(lane == i1, neg, lg)
        m2 = jnp.max(lg2, axis=-1, keepdims=True)
        i2 = jnp.min(jnp.where(lg2 == m2, lane, float(LANE)), axis=-1, keepdims=True)
        e2 = jnp.exp(m2 - m1)
        w1 = 1.0 / (1.0 + e2)
        w2 = e2 / (1.0 + e2)
        comb_ref[...] = jnp.where(lane == i1, w1, 0.0) + jnp.where(lane == i2, w2, 0.0)


def _outproj(yssd, ot, x, mod3, mod_row_fn, lw, router=None):
    t = x.shape[0]
    tm = ROW_TILE
    const = lambda i: (0, 0)
    row = lambda i: (i, 0)
    in_specs = [
        pl.BlockSpec((tm, SSD_WIDTH), row),
        pl.BlockSpec((MLA_WIDTH, tm), lambda i: (0, i)),
        pl.BlockSpec((tm, D_MODEL), row),
        pl.BlockSpec((1, 1, N_MOD * D_MODEL), lambda i: (mod_row_fn(i), 0, 0)),
        pl.BlockSpec((SSD_WIDTH, D_MODEL), const),
        pl.BlockSpec((MLA_WIDTH, D_MODEL), const),
        pl.BlockSpec((MLA_WIDTH, 1), const),
        pl.BlockSpec((1, D_MODEL), const),
        pl.BlockSpec((1, D_MODEL), const),
    ]
    args = [yssd, ot, x, mod3, lw["w_out_a"], lw["w_out_b"], lw["mla_norm_col"],
            lw["g_post1"], lw["g_pre2"]]
    out_specs = [pl.BlockSpec((tm, D_MODEL), row), pl.BlockSpec((tm, D_MODEL), row)]
    out_shape = [jax.ShapeDtypeStruct((t, D_MODEL), F32), jax.ShapeDtypeStruct((t, D_MODEL), BF16)]
    if router is not None:
        in_specs.append(pl.BlockSpec((D_MODEL, LANE), const))
        args.append(router)
        out_specs.append(pl.BlockSpec((tm, LANE), row))
        out_shape.append(jax.ShapeDtypeStruct((t, LANE), F32))
    return pl.pallas_call(
        functools.partial(_outproj_kernel, has_router=router is not None),
        grid=(t // tm,),
        in_specs=in_specs,
        out_specs=out_specs,
        out_shape=out_shape,
        compiler_params=_cparams("arbitrary"),
        name="outproj",
    )(*args)


def _ffn_kernel(*refs, has_comb, nslab):
    if has_comb:
        h_ref, x_ref, comb_ref, mod_ref, wg_ref, wu_ref, wd_ref, gpost_ref, o_ref, acc_ref = refs
    else:
        h_ref, x_ref, mod_ref, wg_ref, wu_ref, wd_ref, gpost_ref, o_ref, acc_ref = refs
    e = pl.program_id(1)
    h = h_ref[...]
    hid = _silu(_dot(h, wg_ref[0])) * _dot(h, wu_ref[0])
    if has_comb:
        comb = comb_ref[...]
        lane = lax.broadcasted_iota(jnp.int32, comb.shape, 1)
        wcol = jnp.sum(jnp.where(lane == e, comb, 0.0), axis=-1, keepdims=True)
        hid = hid * wcol
    part = _dot(hid.astype(BF16), wd_ref[0])

    @pl.when(e == 0)
    def _():
        acc_ref[...] = part

    @pl.when(e > 0)
    def _():
        acc_ref[...] += part

    @pl.when(e == nslab - 1)
    def _():
        gate2 = mod_ref[0][:, 5 * D_MODEL:6 * D_MODEL]
        o_ref[...] = x_ref[...] + gate2 * _rms_rows(acc_ref[...], gpost_ref[...])


def _ffn(h2, x1, mod3, mod_row_fn, wg, wu, wd, gpost, comb=None):
    t = x1.shape[0]
    tm = FFN_ROW_TILE
    nslab, _, f = wg.shape
    row = lambda i, e: (i, 0)
    in_specs = [pl.BlockSpec((tm, D_MODEL), row), pl.BlockSpec((tm, D_MODEL), row)]
    args = [h2, x1]
    if comb is not None:
        in_specs.append(pl.BlockSpec((tm, LANE), row))
        args.append(comb)
    in_specs += [
        pl.BlockSpec((1, 1, N_MOD * D_MODEL), lambda i, e: (mod_row_fn(i), 0, 0)),
        pl.BlockSpec((1, D_MODEL, f), lambda i, e: (e, 0, 0)),
        pl.BlockSpec((1, D_MODEL, f), lambda i, e: (e, 0, 0)),
        pl.BlockSpec((1, f, D_MODEL), lambda i, e: (e, 0, 0)),
        pl.BlockSpec((1, D_MODEL), lambda i, e: (0, 0)),
    ]
    args += [mod3, wg, wu, wd, gpost]
    return pl.pallas_call(
        functools.partial(_ffn_kernel, has_comb=comb is not None, nslab=nslab),
        grid=(t // tm, nslab),
        in_specs=in_specs,
        out_specs=pl.BlockSpec((tm, D_MODEL), row),
        out_shape=jax.ShapeDtypeStruct((t, D_MODEL), F32),
        scratch_shapes=[pltpu.VMEM((tm, D_MODEL), F32)],
        compiler_params=_cparams("arbitrary", "arbitrary"),
        name="ffn",
    )(*args)


_ROT_PERM = tuple(list(range(8, 16)) + list(range(0, 8)) + list(range(24, 32)) + list(range(16, 24)))
_ROT_SIGN = tuple([-1.0] * 8 + [1.0] * 8 + [-1.0] * 8 + [1.0] * 8)


def _rot_cols(w):
    return w[..., jnp.array(_ROT_PERM)] * jnp.array(_ROT_SIGN, F32)


def _prep_layer(i, p):
    w_in = p["w_in"][i]
    s1 = SSD_WIDTH
    s2 = s1 + SSD_XBC
    s3 = s2 + 2 * SSD_HEADS
    s4 = s3 + MLA_Q_RANK
    s5 = s4 + MLA_KV_RANK
    w_z, w_xbc, w_dt, w_cq, w_ckv, w_kr = (w_in[:, :s1], w_in[:, s1:s2], w_in[:, s2:s3],
                                             w_in[:, s3:s4], w_in[:, s4:s5], w_in[:, s5:])
    zc = lambda n: jnp.zeros((D_MODEL, n), F32)
    tile_a = jnp.concatenate([w_dt, zc(ROPE_LANE0 - 2 * SSD_HEADS), w_kr,
                              zc(LANE - ROPE_LANE0 - MLA_ROPE_DIM)], axis=1)
    tile_b = jnp.concatenate([zc(ROPE_LANE0), _rot_cols(w_kr),
                              zc(LANE - ROPE_LANE0 - MLA_ROPE_DIM)], axis=1)
    w_in_pad = jnp.concatenate([w_z, w_xbc, w_cq, w_ckv, tile_a, tile_b], axis=1).astype(BF16)

    w_uq = p["w_uq"][i].reshape(MLA_Q_RANK, MLA_HEADS, MLA_NOPE_DIM + MLA_ROPE_DIM)
    q_nope, q_rope = w_uq[..., :MLA_NOPE_DIM], w_uq[..., MLA_NOPE_DIM:]
    zq = lambda n: jnp.zeros((MLA_Q_RANK, MLA_HEADS, n), F32)
    pad = HEAD_PAD - MLA_NOPE_DIM - MLA_ROPE_DIM
    w_q = jnp.concatenate([q_nope, q_rope, zq(pad)], axis=-1).reshape(MLA_Q_RANK, -1)
    w_qrot = jnp.concatenate([zq(MLA_NOPE_DIM), _rot_cols(q_rope), zq(pad)], axis=-1).reshape(MLA_Q_RANK, -1)

    w_ukv = p["w_ukv"][i].reshape(MLA_KV_RANK, MLA_HEADS, MLA_NOPE_DIM + MLA_V_DIM)
    k_nope, v_w = w_ukv[..., :MLA_NOPE_DIM], w_ukv[..., MLA_NOPE_DIM:]
    w_uk = jnp.concatenate([k_nope, jnp.zeros((MLA_KV_RANK, MLA_HEADS, HEAD_PAD - MLA_NOPE_DIM), F32)],
                           axis=-1).reshape(MLA_KV_RANK, -1)
    w_uv = v_w.reshape(MLA_KV_RANK, MLA_WIDTH)

    dtb = p["dt_bias"][i].reshape(2 * SSD_HEADS)
    alog = p["a_log"][i].reshape(2 * SSD_HEADS)
    padl = lambda v: jnp.pad(v, (0, LANE - v.shape[0])).reshape(1, LANE)
    w_out = p["w_out"][i]
    return {
        "g_pre1": p["norm_pre_mix"][i].reshape(1, D_MODEL),
        "g_post1": p["norm_post_mix"][i].reshape(1, D_MODEL),
        "g_pre2": p["norm_pre_ffn"][i].reshape(1, D_MODEL),
        "g_post2": p["norm_post_ffn"][i].reshape(1, D_MODEL),
        "w_in": w_in_pad,
        "w_dt": w_dt.T.astype(BF16),
        "dtb_row": padl(dtb),
        "dtb_col": dtb.reshape(-1, 1),
        "alog_row": padl(alog),
        "alog_col": alog.reshape(-1, 1),
        "q_norm": p["q_norm"][i].reshape(1, -1),
        "w_q": w_q.T.astype(BF16),
        "w_qrot": w_qrot.T.astype(BF16),
        "kv_norm": p["kv_norm"][i].reshape(1, -1),
        "w_uk": w_uk.astype(BF16),
        "w_uv": w_uv.T.astype(BF16),
        "conv_w": p["conv_w"][i],
        "conv_b": p["conv_b"][i].reshape(1, -1),
        "dskip_row": jnp.repeat(p["d_skip"][i], SSD_HEAD_DIM).reshape(1, -1),
        "ssd_norm": p["ssd_norm"][i].reshape(1, -1),
        "mla_norm_col": p["mla_norm"][i].reshape(-1, 1),
        "w_out_a": w_out[:SSD_WIDTH].astype(BF16),
        "w_out_b": w_out[SSD_WIDTH:].astype(BF16),
    }


def _rope_tables(n_tokens):
    rows = n_tokens // GRID_W
    row = jnp.repeat(jnp.arange(rows, dtype=F32), GRID_W)
    col = jnp.tile(jnp.arange(GRID_W, dtype=F32), rows)
    half = MLA_ROPE_DIM // 2
    inv = ROPE_THETA ** (-jnp.arange(0, half, 2, dtype=F32) / half)
    ar = row[:, None] * inv[None, :]
    ac = col[:, None] * inv[None, :]
    ang = jnp.concatenate([ar, ar, ac, ac], axis=-1)
    return jnp.cos(ang), jnp.sin(ang)


def _attn_tables(cos, sin, n):
    scale = (MLA_NOPE_DIM + MLA_ROPE_DIM) ** -0.5
    pad = HEAD_PAD - ROPE_LANE0 - MLA_ROPE_DIM
    cosk = jnp.concatenate([jnp.zeros((n, ROPE_LANE0), F32), cos, jnp.zeros((n, pad), F32)], axis=1)
    sink = jnp.concatenate([jnp.zeros((n, ROPE_LANE0), F32), sin, jnp.zeros((n, pad), F32)], axis=1)
    cosq = jnp.concatenate([jnp.ones((n, ROPE_LANE0), F32), cos, jnp.zeros((n, pad), F32)], axis=1)
    return (cosq * scale).T, (sink * scale).T, cosk, sink


def kernel(x_prompt, x_sample, cache_ckv, cache_krope, state_ssm, c, c_ctx, w_mod, b_mod, norm_pre_mix, norm_post_mix, norm_pre_ffn, norm_post_ffn, w_in, conv_w, conv_b, dt_bias, a_log, d_skip, ssd_norm, q_norm, w_uq, kv_norm, w_ukv, mla_norm, w_out, ffn_w_gate, ffn_w_up, ffn_w_down, moe_router, moe_w_gate, moe_w_up, moe_w_down):
    params = dict(w_in=w_in, conv_w=conv_w, conv_b=conv_b, dt_bias=dt_bias, a_log=a_log, d_skip=d_skip,
                  ssd_norm=ssd_norm, q_norm=q_norm, w_uq=w_uq, kv_norm=kv_norm, w_ukv=w_ukv,
                  mla_norm=mla_norm, w_out=w_out, norm_pre_mix=norm_pre_mix, norm_post_mix=norm_post_mix,
                  norm_pre_ffn=norm_pre_ffn, norm_post_ffn=norm_post_ffn)
    batch, seq, d = x_prompt.shape
    dec_batch, dec_seq, _ = x_sample.shape
    depth = w_in.shape[0]
    past = cache_ckv.shape[2]
    tm = ROW_TILE

    cvec = jnp.concatenate([c_ctx[None, :], c, jnp.zeros((SUBLANE - 1 - dec_batch, d), F32)], axis=0)
    mod = _modulation(cvec, w_mod, b_mod)

    ones = jnp.ones((tm, MLA_ROPE_DIM), F32)
    tabs_ctx = _attn_tables(ones, jnp.zeros_like(ones), tm)
    cos, sin = _rope_tables(dec_seq)
    tabs_lat = _attn_tables(cos, sin, dec_seq)
    lat_blocks = dec_seq // tm

    xp = x_prompt.reshape(batch * seq, d)
    xs = x_sample.reshape(dec_batch * dec_seq, d)
    new_ckv, new_kr, new_ssm = [], [], []
    for i in range(depth):
        lw = _prep_layer(i, params)
        mod3 = mod[i].reshape(SUBLANE, 1, N_MOD * d)
        j = i // 2
        if i % 2 == 0:
            f = ffn_w_gate.shape[2] // 2
            split = lambda w: jnp.stack([w[:, :f], w[:, f:]], axis=0).astype(BF16)
            wg, wu = split(ffn_w_gate[j]), split(ffn_w_up[j])
            wd = ffn_w_down[j].reshape(2, f, d).astype(BF16)
            router = None
        else:
            wg, wu, wd = moe_w_gate[j].astype(BF16), moe_w_up[j].astype(BF16), moe_w_down[j].astype(BF16)
            router = jnp.pad(moe_router[j], ((0, 0), (0, LANE - N_EXPERTS)))

        def run(x, row_fn, tabs, tab_blocks, seq_len, heads_per_step, ctx):
            z, xbc, dt, dtt, qt, k, vt, ckvn, kr = _inproj(x, mod3, row_fn, lw, tabs, tab_blocks)
            cps = seq_len // SSD_CHUNK
            h0f = None if ctx is None else ctx[2][:, 0]
            h0b = None if ctx is None else ctx[2][:, 1]
            yf, hf = _ssd(xbc, dt, dtt, lw, cps, 0, h0=h0f)
            yssd, hb = _ssd(xbc, dt, dtt, lw, cps, 1, h0=h0b, z=z, yf=yf)
            cache = None
            if ctx is not None:
                kr_tile = jnp.pad(ctx[1].reshape(-1, MLA_ROPE_DIM),
                                  ((0, 0), (ROPE_LANE0, HEAD_PAD - ROPE_LANE0 - MLA_ROPE_DIM)))
                cache = _kvcache(ctx[0].reshape(-1, MLA_KV_RANK), kr_tile, lw)
            ot = _attention(qt, k, vt, seq_len, heads_per_step, cache=cache)
            outs = _outproj(yssd, ot, x, mod3, row_fn, lw, router=router)
            x1, h2 = outs[0], outs[1]
            comb = outs[2] if router is not None else None
            x2 = _ffn(h2, x1, mod3, row_fn, wg, wu, wd, lw["g_post2"], comb=comb)
            return x2, ckvn, kr, hf, hb

        xp, ckvn, kr, hf, hb = run(xp, lambda b: 0, tabs_ctx, 1, seq, MLA_HEADS, None)
        new_ckv.append(ckvn.reshape(batch, seq, MLA_KV_RANK))
        new_kr.append(kr.reshape(batch, seq, MLA_ROPE_DIM))
        new_ssm.append(jnp.stack([hf, hb], axis=1))
        xs, _, _, _, _ = run(xs, lambda b: 1 + b // lat_blocks, tabs_lat, lat_blocks, dec_seq, 1,
                             (cache_ckv[:, i], cache_krope[:, i], state_ssm[:, i]))
    return (xp.reshape(batch, seq, d), xs.reshape(dec_batch, dec_seq, d),
            jnp.stack(new_ckv, axis=1), jnp.stack(new_kr, axis=1), jnp.stack(new_ssm, axis=1))
```

```python
import functools
import math

import jax
import jax.numpy as jnp
from jax import lax
from jax.experimental import pallas as pl
from jax.experimental.pallas import tpu as pltpu

F32 = jnp.float32
BF16 = jnp.bfloat16

D_MODEL = 1024
GRID_W = 64
SSD_WIDTH = 512
SSD_HEAD_DIM = 64
SSD_HEADS = 8
SSD_GROUPS = 2
SSD_STATE = 64
SSD_CONV = 5
SSD_CHUNK = 128
SSD_XBC = SSD_WIDTH + 2 * SSD_GROUPS * SSD_STATE
MLA_WIDTH = 512
MLA_V_DIM = 64
MLA_HEADS = 8
MLA_NOPE_DIM = 64
MLA_ROPE_DIM = 32
MLA_Q_RANK = 384
MLA_KV_RANK = 256
ROPE_THETA = 10000.0
N_EXPERTS = 8
N_MOD = 6
EPS = 1e-6

LANE = 128
SUBLANE = 8
HEAD_PAD = 128
ONES_ROWS = 16
ROPE_LANE0 = MLA_NOPE_DIM
C_Z = 0
C_XBC = C_Z + SSD_WIDTH
C_CQ = C_XBC + SSD_XBC
C_CKV = C_CQ + MLA_Q_RANK
C_TA = C_CKV + MLA_KV_RANK
C_TB = C_TA + LANE
IN_PAD = C_TB + LANE

VMEM_LIMIT = 56 * 1024 * 1024

ROW_TILE = 512
ATTN_Q_TILE = 256
ATTN_KEY_BLOCK = 512
FFN_ROW_TILE = 512

NT_DIMS = (((1,), (1,)), ((), ()))
TN_DIMS = (((0,), (0,)), ((), ()))


def _cparams(*sem):
    return pltpu.CompilerParams(dimension_semantics=sem, vmem_limit_bytes=VMEM_LIMIT)


def _silu(x):
    return x / (1.0 + jnp.exp(-x))


def _softplus(x):
    return jnp.maximum(x, 0.0) + jnp.log(1.0 + jnp.exp(-jnp.abs(x)))


def _rms_rows(x, g):
    ms = jnp.mean(x * x, axis=-1, keepdims=True)
    return x * lax.rsqrt(ms + EPS) * g


def _dot(a, b):
    return jnp.dot(a, b, preferred_element_type=F32)


def _dot_nt(a, b):
    return lax.dot_general(a, b, NT_DIMS, preferred_element_type=F32)


def _dot_tn(a, b):
    return lax.dot_general(a, b, TN_DIMS, preferred_element_type=F32)


def _split3(x):
    hi = x.astype(BF16)
    r1 = x - hi.astype(F32)
    mid = r1.astype(BF16)
    lo = (r1 - mid.astype(F32)).astype(BF16)
    return hi, mid, lo


def _mod_kernel(c_ref, w_ref, b_ref, o_ref):
    s = _silu(c_ref[...]).astype(BF16)
    o_ref[0] = _dot(s, w_ref[0].astype(BF16)) + b_ref[0]


def _modulation(cvec, w_mod, b_mod):
    depth, d, n = w_mod.shape
    tn = 1536
    return pl.pallas_call(
        _mod_kernel,
        grid=(depth, n // tn),
        in_specs=[
            pl.BlockSpec((SUBLANE, d), lambda l, j: (0, 0)),
            pl.BlockSpec((1, d, tn), lambda l, j: (l, 0, j)),
            pl.BlockSpec((1, 1, tn), lambda l, j: (l, 0, j)),
        ],
        out_specs=pl.BlockSpec((1, SUBLANE, tn), lambda l, j: (l, 0, j)),
        out_shape=jax.ShapeDtypeStruct((depth, SUBLANE, n), F32),
        compiler_params=_cparams("arbitrary", "arbitrary"),
        name="modulation",
    )(cvec, w_mod, b_mod.reshape(depth, 1, n))


def _inproj_kernel(x_ref, mod_ref, gpre_ref, win_ref, wdt_ref, dtb_row_ref, dtb_col_ref,
                   qn_ref, wq_ref, wqrot_ref, kvn_ref, wuk_ref, wuv_ref,
                   cosq_ref, sinq_ref, cosk_ref, sink_ref,
                   z_ref, xbc_ref, dt_ref, dtt_ref, qt_ref, k_ref, vt_ref, ckvn_ref, kr_ref):
    mod = mod_ref[0]
    shift = mod[:, 0:D_MODEL]
    scale = mod[:, D_MODEL:2 * D_MODEL]
    h = _rms_rows(x_ref[...], gpre_ref[...]) * (1.0 + scale) + shift
    hb = h.astype(BF16)
    proj = _dot(hb, win_ref[...])
    z_ref[...] = proj[:, C_Z:C_XBC]
    xbc_ref[...] = proj[:, C_XBC:C_CQ]
    cqn = _rms_rows(proj[:, C_CQ:C_CKV], qn_ref[...]).astype(BF16)
    ckvn = _rms_rows(proj[:, C_CKV:C_TA], kvn_ref[...])
    ckvn_ref[...] = ckvn
    ckvb = ckvn.astype(BF16)
    ta = proj[:, C_TA:C_TB]
    tb = proj[:, C_TB:IN_PAD]
    dt_ref[...] = _softplus(ta + dtb_row_ref[...])
    kr_ref[...] = ta[:, ROPE_LANE0:ROPE_LANE0 + MLA_ROPE_DIM]
    kr_rot = ta * cosk_ref[...] + tb * sink_ref[...]
    knp = _dot(ckvb, wuk_ref[...])
    for hd in range(MLA_HEADS):
        sl = slice(hd * HEAD_PAD, (hd + 1) * HEAD_PAD)
        k_ref[:, sl] = (knp[:, sl] + kr_rot).astype(BF16)
    vt_ref[...] = _dot_nt(wuv_ref[...], ckvb).astype(BF16)
    qt = _dot_nt(wq_ref[...], cqn)
    qrt = _dot_nt(wqrot_ref[...], cqn)
    cosq = cosq_ref[...]
    sinq = sinq_ref[...]
    for hd in range(MLA_HEADS):
        sl = slice(hd * HEAD_PAD, (hd + 1) * HEAD_PAD)
        qt_ref[sl, :] = (qt[sl, :] * cosq + qrt[sl, :] * sinq).astype(BF16)
    dtt_ref[...] = _softplus(_dot_nt(wdt_ref[...], hb) + dtb_col_ref[...])


def _inproj(x, mod3, mod_row_fn, lw, tabs, tab_blocks):
    t = x.shape[0]
    tm = ROW_TILE
    nb = t // tm
    cosq, sinq, cosk, sink = tabs
    ntab = tab_blocks
    const = lambda i: (0, 0)
    row = lambda i: (i, 0)
    col = lambda i: (0, i)
    in_specs = [
        pl.BlockSpec((tm, D_MODEL), row),
        pl.BlockSpec((1, 1, N_MOD * D_MODEL), lambda i: (mod_row_fn(i), 0, 0)),
        pl.BlockSpec((1, D_MODEL), const),
        pl.BlockSpec((D_MODEL, IN_PAD), const),
        pl.BlockSpec((2 * SSD_HEADS, D_MODEL), const),
        pl.BlockSpec((1, LANE), const),
        pl.BlockSpec((2 * SSD_HEADS, 1), const),
        pl.BlockSpec((1, MLA_Q_RANK), const),
        pl.BlockSpec((MLA_HEADS * HEAD_PAD, MLA_Q_RANK), const),
        pl.BlockSpec((MLA_HEADS * HEAD_PAD, MLA_Q_RANK), const),
        pl.BlockSpec((1, MLA_KV_RANK), const),
        pl.BlockSpec((MLA_KV_RANK, MLA_HEADS * HEAD_PAD), const),
        pl.BlockSpec((MLA_WIDTH, MLA_KV_RANK), const),
        pl.BlockSpec((HEAD_PAD, tm), lambda i: (0, i % ntab)),
        pl.BlockSpec((HEAD_PAD, tm), lambda i: (0, i % ntab)),
        pl.BlockSpec((tm, LANE), lambda i: (i % ntab, 0)),
        pl.BlockSpec((tm, LANE), lambda i: (i % ntab, 0)),
    ]
    out_specs = [
        pl.BlockSpec((tm, SSD_WIDTH), row),
        pl.BlockSpec((tm, SSD_XBC), row),
        pl.BlockSpec((tm, LANE), row),
        pl.BlockSpec((2 * SSD_HEADS, tm), col),
        pl.BlockSpec((MLA_HEADS * HEAD_PAD, tm), col),
        pl.BlockSpec((tm, MLA_HEADS * HEAD_PAD), row),
        pl.BlockSpec((MLA_WIDTH, tm), col),
        pl.BlockSpec((tm, MLA_KV_RANK), row),
        pl.BlockSpec((tm, MLA_ROPE_DIM), row),
    ]
    out_shape = [
        jax.ShapeDtypeStruct((t, SSD_WIDTH), F32),
        jax.ShapeDtypeStruct((t, SSD_XBC), F32),
        jax.ShapeDtypeStruct((t, LANE), F32),
        jax.ShapeDtypeStruct((2 * SSD_HEADS, t), F32),
        jax.ShapeDtypeStruct((MLA_HEADS * HEAD_PAD, t), BF16),
        jax.ShapeDtypeStruct((t, MLA_HEADS * HEAD_PAD), BF16),
        jax.ShapeDtypeStruct((MLA_WIDTH, t), BF16),
        jax.ShapeDtypeStruct((t, MLA_KV_RANK), F32),
        jax.ShapeDtypeStruct((t, MLA_ROPE_DIM), F32),
    ]
    return pl.pallas_call(
        _inproj_kernel,
        grid=(nb,),
        in_specs=in_specs,
        out_specs=out_specs,
        out_shape=out_shape,
        compiler_params=_cparams("arbitrary"),
        name="inproj",
    )(x, mod3, lw["g_pre1"], lw["w_in"], lw["w_dt"], lw["dtb_row"], lw["dtb_col"],
      lw["q_norm"], lw["w_q"], lw["w_qrot"], lw["kv_norm"], lw["w_uk"], lw["w_uv"],
      cosq, sinq, cosk, sink)


def _kvcache_kernel(ckv_ref, kr_ref, wuk_ref, wuv_ref, k_ref, vt_ref):
    ckvb = ckv_ref[...].astype(BF16)
    knp = _dot(ckvb, wuk_ref[...])
    kr = kr_ref[...]
    for hd in range(MLA_HEADS):
        sl = slice(hd * HEAD_PAD, (hd + 1) * HEAD_PAD)
        k_ref[:, sl] = (knp[:, sl] + kr).astype(BF16)
    vt_ref[...] = _dot_nt(wuv_ref[...], ckvb).astype(BF16)


def _kvcache(ckv, kr_tile, lw):
    n = ckv.shape[0]
    tm = 512
    return pl.pallas_call(
        _kvcache_kernel,
        grid=(n // tm,),
        in_specs=[
            pl.BlockSpec((tm, MLA_KV_RANK), lambda i: (i, 0)),
            pl.BlockSpec((tm, LANE), lambda i: (i, 0)),
            pl.BlockSpec((MLA_KV_RANK, MLA_HEADS * HEAD_PAD), lambda i: (0, 0)),
            pl.BlockSpec((MLA_WIDTH, MLA_KV_RANK), lambda i: (0, 0)),
        ],
        out_specs=[
            pl.BlockSpec((tm, MLA_HEADS * HEAD_PAD), lambda i: (i, 0)),
            pl.BlockSpec((MLA_WIDTH, tm), lambda i: (0, i)),
        ],
        out_shape=[
            jax.ShapeDtypeStruct((n, MLA_HEADS * HEAD_PAD), BF16),
            jax.ShapeDtypeStruct((MLA_WIDTH, n), BF16),
        ],
        compiler_params=_cparams("arbitrary"),
        name="kvcache",
    )(ckv, kr_tile, lw["w_uk"], lw["w_uv"])


def _ssd_kernel(*refs, backward, cps, has_h0, nchunks):
    it = iter(refs)
    xbc_ref, prev_ref, next_ref, dt_ref, dtt_ref = (next(it) for _ in range(5))
    cw_ref, cb_ref, alog_row_ref, alog_col_ref = (next(it) for _ in range(4))
    h0_ref = next(it) if has_h0 else None
    if backward:
        z_ref, yf_ref, dskip_ref, gn_ref = (next(it) for _ in range(4))
    y_ref, hout_ref, st_ref = next(it), next(it), next(it)

    q = SSD_CHUNK
    i = pl.program_id(0)
    g = (nchunks - 1 - i) if backward else i
    pos = g % cps
    seq_first = pos == 0
    seq_last = pos == cps - 1
    scan_first = seq_last if backward else seq_first

    prev = jnp.where(seq_first, 0.0, prev_ref[...])
    nxt = jnp.where(seq_last, 0.0, next_ref[...])
    ext = jnp.concatenate([prev, xbc_ref[...], nxt], axis=0)
    cw = cw_ref[...]
    acc = cb_ref[...] + ext[SUBLANE - 2:SUBLANE - 2 + q] * cw[0:1]
    for k in range(1, SSD_CONV):
        o = SUBLANE - 2 + k
        acc = acc + ext[o:o + q] * cw[k:k + 1]
    xc = _silu(acc)
    xs = xc[:, :SSD_WIDTH]
    xsb = xs.astype(BF16)
    bmb = xc[:, SSD_WIDTH:SSD_WIDTH + SSD_GROUPS * SSD_STATE].astype(BF16)
    cmb = xc[:, SSD_WIDTH + SSD_GROUPS * SSD_STATE:].astype(BF16)

    dt = dt_ref[...]
    dtt = dtt_ref[...]
    a_row = -jnp.exp(alog_row_ref[...])
    a_col = -jnp.exp(alog_col_ref[...])
    dta = dt * a_row
    dtat = dtt * a_col
    r_i = lax.broadcasted_iota(jnp.int32, (q, q), 0)
    c_i = lax.broadcasted_iota(jnp.int32, (q, q), 1)
    lower = r_i >= c_i
    upper = r_i <= c_i
    tril = jnp.where(lower, 1.0, 0.0).astype(BF16)
    triu = jnp.where(upper, 1.0, 0.0).astype(BF16)
    m_col, m_row = (triu, tril) if backward else (tril, triu)
    la_col = sum(_dot(m_col, p) for p in _split3(dta))
    la_row = sum(_dot(p, m_row) for p in _split3(dtat))
    end = 0 if backward else q - 1
    tot_row = la_col[end:end + 1, :]
    w_end = jnp.exp(tot_row - la_col) * dt
    e_col = jnp.exp(la_col)
    e_tot = jnp.exp(tot_row)
    mask = upper if backward else lower

    @pl.when(scan_first)
    def _():
        if has_h0:
            st_ref[...] = h0_ref[0]
        else:
            st_ref[...] = jnp.zeros_like(st_ref)

    doff = SSD_HEADS if backward else 0
    rep = SSD_HEADS // SSD_GROUPS
    ys = []
    for hd in range(SSD_HEADS):
        j = doff + hd
        grp = hd // rep
        cg = cmb[:, grp * SSD_STATE:(grp + 1) * SSD_STATE]
        bg = bmb[:, grp * SSD_STATE:(grp + 1) * SSD_STATE]
        xh = xsb[:, hd * SSD_HEAD_DIM:(hd + 1) * SSD_HEAD_DIM]
        cbm = _dot_nt(cg, bg)
        seg = la_col[:, j:j + 1] - la_row[j:j + 1, :]
        decay = jnp.exp(jnp.where(mask, seg, -jnp.inf))
        scores = cbm * decay * dtt[j:j + 1, :]
        y_intra = _dot(scores.astype(BF16), xh)
        hs = st_ref[hd]
        y_inter = _dot_nt(cg, hs.astype(BF16)) * e_col[:, j:j + 1]
        bw = (bg.astype(F32) * w_end[:, j:j + 1]).astype(BF16)
        cst = _dot_tn(xh, bw)
        st_ref[hd] = hs * e_tot[:, j:j + 1] + cst
        ys.append(y_intra + y_inter)
    y = jnp.concatenate(ys, axis=-1)
    hout_ref[0] = st_ref[...]
    if backward:
        y = y + yf_ref[...] + dskip_ref[...] * xs
        y = y * _silu(z_ref[...])
        y_ref[...] = _rms_rows(y, gn_ref[...]).astype(BF16)
    else:
        y_ref[...] = y


def _ssd(xbc, dt, dtt, lw, cps, direction, h0=None, z=None, yf=None):
    t = xbc.shape[0]
    q = SSD_CHUNK
    nchunks = t // q
    nseq = nchunks // cps
    backward = direction == 1
    hb = q // SUBLANE
    n8 = t // SUBLANE
    gi = (lambda i: nchunks - 1 - i) if backward else (lambda i: i)
    const = lambda i: (0, 0)
    in_specs = [
        pl.BlockSpec((q, SSD_XBC), lambda i: (gi(i), 0)),
        pl.BlockSpec((SUBLANE, SSD_XBC), lambda i: (jnp.maximum(gi(i) * hb - 1, 0), 0)),
        pl.BlockSpec((SUBLANE, SSD_XBC), lambda i: (jnp.minimum((gi(i) + 1) * hb, n8 - 1), 0)),
        pl.BlockSpec((q, LANE), lambda i: (gi(i), 0)),
        pl.BlockSpec((2 * SSD_HEADS, q), lambda i: (0, gi(i))),
        pl.BlockSpec((SSD_CONV, SSD_XBC), const),
        pl.BlockSpec((1, SSD_XBC), const),
        pl.BlockSpec((1, LANE), const),
        pl.BlockSpec((2 * SSD_HEADS, 1), const),
    ]
    args = [xbc, xbc, xbc, dt, dtt, lw["conv_w"], lw["conv_b"], lw["alog_row"], lw["alog_col"]]
    st_block = (1, SSD_HEADS, SSD_HEAD_DIM, SSD_STATE)
    if h0 is not None:
        in_specs.append(pl.BlockSpec(st_block, lambda i: (gi(i) // cps, 0, 0, 0)))
        args.append(h0)
    if backward:
        in_specs += [
            pl.BlockSpec((q, SSD_WIDTH), lambda i: (gi(i), 0)),
            pl.BlockSpec((q, SSD_WIDTH), lambda i: (gi(i), 0)),
            pl.BlockSpec((1, SSD_WIDTH), const),
            pl.BlockSpec((1, SSD_WIDTH), const),
        ]
        args += [z, yf, lw["dskip_row"], lw["ssd_norm"]]
    out_specs = [
        pl.BlockSpec((q, SSD_WIDTH), lambda i: (gi(i), 0)),
        pl.BlockSpec(st_block, lambda i: (gi(i) // cps, 0, 0, 0)),
    ]
    out_shape = [
        jax.ShapeDtypeStruct((t, SSD_WIDTH), BF16 if backward else F32),
        jax.ShapeDtypeStruct((nseq, SSD_HEADS, SSD_HEAD_DIM, SSD_STATE), F32),
    ]
    kern = functools.partial(_ssd_kernel, backward=backward, cps=cps,
                             has_h0=h0 is not None, nchunks=nchunks)
    return pl.pallas_call(
        kern,
        grid=(nchunks,),
        in_specs=in_specs,
        out_specs=out_specs,
        out_shape=out_shape,
        scratch_shapes=[pltpu.VMEM((SSD_HEADS, SSD_HEAD_DIM, SSD_STATE), F32)],
        compiler_params=_cparams("arbitrary"),
        name="ssd_bwd" if backward else "ssd_fwd",
    )(*args)


def _attn_kernel(*refs, heads, has_cache):
    if has_cache:
        qt_ref, k_ref, vt_ref, kc_ref, vct_ref = refs[:5]
    else:
        qt_ref, k_ref, vt_ref = refs[:3]
    s_ref = refs[-1]
    o_ref = refs[-2]
    lk = k_ref.shape[0]
    kb = min(ATTN_KEY_BLOCK, lk)
    blocks = [(k_ref, vt_ref, i * kb) for i in range(lk // kb)]
    if has_cache:
        lc = kc_ref.shape[0]
        kbc = min(ATTN_KEY_BLOCK, lc)
        blocks += [(kc_ref, vct_ref, i * kbc) for i in range(lc // kbc)]
        assert kbc == kb
    nblk = len(blocks)
    ones = jnp.ones((ONES_ROWS, kb), BF16)

    def score_block(hd, i, m):
        kr, _, off = blocks[i]
        q = qt_ref[hd * HEAD_PAD:(hd + 1) * HEAD_PAD, :]
        s = _dot(kr[off:off + kb, hd * HEAD_PAD:(hd + 1) * HEAD_PAD], q)
        s_ref[hd % 2, i * kb:(i + 1) * kb, :] = s
        bm = jnp.max(s, axis=0, keepdims=True)
        return bm if m is None else jnp.maximum(m, bm)

    def value_block(hd, i, m, acc):
        _, vr, off = blocks[i]
        p = jnp.exp2(s_ref[hd % 2, i * kb:(i + 1) * kb, :] - m).astype(BF16)
        v = vr[hd * MLA_V_DIM:(hd + 1) * MLA_V_DIM, off:off + kb]
        part = _dot(jnp.concatenate([v, ones], axis=0), p)
        return part if acc is None else acc + part

    m_cur = None
    for i in range(nblk):
        m_cur = score_block(0, i, m_cur)
    for hd in range(heads):
        m_next, acc = None, None
        for i in range(nblk):
            if hd + 1 < heads:
                m_next = score_block(hd + 1, i, m_next)
            acc = value_block(hd, i, m_cur, acc)
        vs = slice(hd * MLA_V_DIM, (hd + 1) * MLA_V_DIM)
        o_ref[vs, :] = acc[:MLA_V_DIM] / acc[MLA_V_DIM:MLA_V_DIM + 1]
        m_cur = m_next


def _attention(qt, k, vt, seq_len, heads_per_step, cache=None):
    t = k.shape[0]
    nseq = t // seq_len
    tq = min(ATTN_Q_TILE, seq_len)
    nq = seq_len // tq
    g = heads_per_step
    in_specs = [
        pl.BlockSpec((g * HEAD_PAD, tq), lambda s, h, j: (h, s * nq + j)),
        pl.BlockSpec((seq_len, g * HEAD_PAD), lambda s, h, j: (s, h)),
        pl.BlockSpec((g * MLA_V_DIM, seq_len), lambda s, h, j: (h, s)),
    ]
    args = [qt, k, vt]
    n_keys = seq_len
    if cache is not None:
        kc, vct = cache
        past = kc.shape[0] // nseq
        n_keys += past
        in_specs += [
            pl.BlockSpec((past, g * HEAD_PAD), lambda s, h, j: (s, h)),
            pl.BlockSpec((g * MLA_V_DIM, past), lambda s, h, j: (h, s)),
        ]
        args += [kc, vct]
    kern = functools.partial(_attn_kernel, heads=g, has_cache=cache is not None)
    return pl.pallas_call(
        kern,
        grid=(nseq, MLA_HEADS // g, nq),
        in_specs=in_specs,
        out_specs=pl.BlockSpec((g * MLA_V_DIM, tq), lambda s, h, j: (h, s * nq + j)),
        out_shape=jax.ShapeDtypeStruct((MLA_WIDTH, t), F32),
        scratch_shapes=[pltpu.VMEM((2, n_keys, tq), F32)],
        compiler_params=_cparams("arbitrary", "arbitrary", "arbitrary"),
        name="attention",
    )(*args)


def _outproj_kernel(*refs, has_router):
    if has_router:
        (y_ref, ot_ref, x_ref, mod_ref, wa_ref, wb_ref, gm_ref, gpost_ref, gpre2_ref, rt_ref,
         x1_ref, h2_ref, comb_ref) = refs
    else:
        (y_ref, ot_ref, x_ref, mod_ref, wa_ref, wb_ref, gm_ref, gpost_ref, gpre2_ref,
         x1_ref, h2_ref) = refs
    mod = mod_ref[0]
    gate1 = mod[:, 2 * D_MODEL:3 * D_MODEL]
    shift2 = mod[:, 3 * D_MODEL:4 * D_MODEL]
    scale2 = mod[:, 4 * D_MODEL:5 * D_MODEL]
    ot = ot_ref[...]
    ms = jnp.mean(ot * ot, axis=0, keepdims=True)
    on = (ot * lax.rsqrt(ms + EPS) * gm_ref[...]).astype(BF16)
    y = _dot(y_ref[...], wa_ref[...]) + _dot_tn(on, wb_ref[...])
    x1 = x_ref[...] + gate1 * _rms_rows(y, gpost_ref[...])
    x1_ref[...] = x1
    h2 = _rms_rows(x1, gpre2_ref[...]) * (1.0 + scale2) + shift2
    h2_ref[...] = h2.astype(BF16)
    if has_router:
        hh, hm, _ = _split3(h2)
        rh, rm, _ = _split3(rt_ref[...])
        logits = _dot(hh, rh) + (_dot(hm, rh) + _dot(hh, rm))
        lane = lax.broadcasted_iota(jnp.int32, logits.shape, 1).astype(F32)
        neg = jnp.float32(-jnp.inf)
        lg = jnp.where(lane < N_EXPERTS, logits, neg)
        m1 = jnp.max(lg, axis=-1, keepdims=True)
        i1 = jnp.min(jnp.where(lg == m1, lane, float(LANE)), axis=-1, keepdims=True)
        lg2 = jnp.where(lane == i1, neg, lg)
        m2 = jnp.max(lg2, axis=-1, keepdims=True)
        i2 = jnp.min(jnp.where(lg2 == m2, lane, float(LANE)), axis=-1, keepdims=True)
        e2 = jnp.exp(m2 - m1)
        w1 = 1.0 / (1.0 + e2)
        w2 = e2 / (1.0 + e2)
        comb_ref[...] = jnp.where(lane == i1, w1, 0.0) + jnp.where(lane == i2, w2, 0.0)


def _outproj(yssd, ot, x, mod3, mod_row_fn, lw, router=None):
    t = x.shape[0]
    tm = ROW_TILE
    const = lambda i: (0, 0)
    row = lambda i: (i, 0)
    in_specs = [
        pl.BlockSpec((tm, SSD_WIDTH), row),
        pl.BlockSpec((MLA_WIDTH, tm), lambda i: (0, i)),
        pl.BlockSpec((tm, D_MODEL), row),
        pl.BlockSpec((1, 1, N_MOD * D_MODEL), lambda i: (mod_row_fn(i), 0, 0)),
        pl.BlockSpec((SSD_WIDTH, D_MODEL), const),
        pl.BlockSpec((MLA_WIDTH, D_MODEL), const),
        pl.BlockSpec((MLA_WIDTH, 1), const),
        pl.BlockSpec((1, D_MODEL), const),
        pl.BlockSpec((1, D_MODEL), const),
    ]
    args = [yssd, ot, x, mod3, lw["w_out_a"], lw["w_out_b"], lw["mla_norm_col"],
            lw["g_post1"], lw["g_pre2"]]
    out_specs = [pl.BlockSpec((tm, D_MODEL), row), pl.BlockSpec((tm, D_MODEL), row)]
    out_shape = [jax.ShapeDtypeStruct((t, D_MODEL), F32), jax.ShapeDtypeStruct((t, D_MODEL), BF16)]
    if router is not None:
        in_specs.append(pl.BlockSpec((D_MODEL, LANE), const))
        args.append(router)
        out_specs.append(pl.BlockSpec((tm, LANE), row))
        out_shape.append(jax.ShapeDtypeStruct((t, LANE), F32))
    return pl.pallas_call(
        functools.partial(_outproj_kernel, has_router=router is not None),
        grid=(t // tm,),
        in_specs=in_specs,
        out_specs=out_specs,
        out_shape=out_shape,
        compiler_params=_cparams("arbitrary"),
        name="outproj",
    )(*args)


def _ffn_kernel(*refs, has_comb, nslab):
    if has_comb:
        h_ref, x_ref, comb_ref, mod_ref, wg_ref, wu_ref, wd_ref, gpost_ref, o_ref, acc_ref = refs
    else:
        h_ref, x_ref, mod_ref, wg_ref, wu_ref, wd_ref, gpost_ref, o_ref, acc_ref = refs
    e = pl.program_id(1)
    h = h_ref[...]
    hid = _silu(_dot(h, wg_ref[0])) * _dot(h, wu_ref[0])
    if has_comb:
        comb = comb_ref[...]
        lane = lax.broadcasted_iota(jnp.int32, comb.shape, 1)
        wcol = jnp.sum(jnp.where(lane == e, comb, 0.0), axis=-1, keepdims=True)
        hid = hid * wcol
    part = _dot(hid.astype(BF16), wd_ref[0])

    @pl.when(e == 0)
    def _():
        acc_ref[...] = part

    @pl.when(e > 0)
    def _():
        acc_ref[...] += part

    @pl.when(e == nslab - 1)
    def _():
        gate2 = mod_ref[0][:, 5 * D_MODEL:6 * D_MODEL]
        o_ref[...] = x_ref[...] + gate2 * _rms_rows(acc_ref[...], gpost_ref[...])


def _ffn(h2, x1, mod3, mod_row_fn, wg, wu, wd, gpost, comb=None):
    t = x1.shape[0]
    tm = FFN_ROW_TILE
    nslab, _, f = wg.shape
    row = lambda i, e: (i, 0)
    in_specs = [pl.BlockSpec((tm, D_MODEL), row), pl.BlockSpec((tm, D_MODEL), row)]
    args = [h2, x1]
    if comb is not None:
        in_specs.append(pl.BlockSpec((tm, LANE), row))
        args.append(comb)
    in_specs += [
        pl.BlockSpec((1, 1, N_MOD * D_MODEL), lambda i, e: (mod_row_fn(i), 0, 0)),
        pl.BlockSpec((1, D_MODEL, f), lambda i, e: (e, 0, 0)),
        pl.BlockSpec((1, D_MODEL, f), lambda i, e: (e, 0, 0)),
        pl.BlockSpec((1, f, D_MODEL), lambda i, e: (e, 0, 0)),
        pl.BlockSpec((1, D_MODEL), lambda i, e: (0, 0)),
    ]
    args += [mod3, wg, wu, wd, gpost]
    return pl.pallas_call(
        functools.partial(_ffn_kernel, has_comb=comb is not None, nslab=nslab),
        grid=(t // tm, nslab),
        in_specs=in_specs,
        out_specs=pl.BlockSpec((tm, D_MODEL), row),
        out_shape=jax.ShapeDtypeStruct((t, D_MODEL), F32),
        scratch_shapes=[pltpu.VMEM((tm, D_MODEL), F32)],
        compiler_params=_cparams("arbitrary", "arbitrary"),
        name="ffn",
    )(*args)


_ROT_PERM = tuple(list(range(8, 16)) + list(range(0, 8)) + list(range(24, 32)) + list(range(16, 24)))
_ROT_SIGN = tuple([-1.0] * 8 + [1.0] * 8 + [-1.0] * 8 + [1.0] * 8)


def _rot_cols(w):
    return w[..., jnp.array(_ROT_PERM)] * jnp.array(_ROT_SIGN, F32)


def _prep_layer(i, p):
    w_in = p["w_in"][i]
    s1 = SSD_WIDTH
    s2 = s1 + SSD_XBC
    s3 = s2 + 2 * SSD_HEADS
    s4 = s3 + MLA_Q_RANK
    s5 = s4 + MLA_KV_RANK
    w_z, w_xbc, w_dt, w_cq, w_ckv, w_kr = (w_in[:, :s1], w_in[:, s1:s2], w_in[:, s2:s3],
                                             w_in[:, s3:s4], w_in[:, s4:s5], w_in[:, s5:])
    zc = lambda n: jnp.zeros((D_MODEL, n), F32)
    tile_a = jnp.concatenate([w_dt, zc(ROPE_LANE0 - 2 * SSD_HEADS), w_kr,
                              zc(LANE - ROPE_LANE0 - MLA_ROPE_DIM)], axis=1)
    tile_b = jnp.concatenate([zc(ROPE_LANE0), _rot_cols(w_kr),
                              zc(LANE - ROPE_LANE0 - MLA_ROPE_DIM)], axis=1)
    w_in_pad = jnp.concatenate([w_z, w_xbc, w_cq, w_ckv, tile_a, tile_b], axis=1).astype(BF16)

    w_uq = p["w_uq"][i].reshape(MLA_Q_RANK, MLA_HEADS, MLA_NOPE_DIM + MLA_ROPE_DIM)
    q_nope, q_rope = w_uq[..., :MLA_NOPE_DIM], w_uq[..., MLA_NOPE_DIM:]
    zq = lambda n: jnp.zeros((MLA_Q_RANK, MLA_HEADS, n), F32)
    pad = HEAD_PAD - MLA_NOPE_DIM - MLA_ROPE_DIM
    w_q = jnp.concatenate([q_nope, q_rope, zq(pad)], axis=-1).reshape(MLA_Q_RANK, -1)
    w_qrot = jnp.concatenate([zq(MLA_NOPE_DIM), _rot_cols(q_rope), zq(pad)], axis=-1).reshape(MLA_Q_RANK, -1)

    w_ukv = p["w_ukv"][i].reshape(MLA_KV_RANK, MLA_HEADS, MLA_NOPE_DIM + MLA_V_DIM)
    k_nope, v_w = w_ukv[..., :MLA_NOPE_DIM], w_ukv[..., MLA_NOPE_DIM:]
    w_uk = jnp.concatenate([k_nope, jnp.zeros((MLA_KV_RANK, MLA_HEADS, HEAD_PAD - MLA_NOPE_DIM), F32)],
                           axis=-1).reshape(MLA_KV_RANK, -1)
    w_uv = v_w.reshape(MLA_KV_RANK, MLA_WIDTH)

    dtb = p["dt_bias"][i].reshape(2 * SSD_HEADS)
    alog = p["a_log"][i].reshape(2 * SSD_HEADS)
    padl = lambda v: jnp.pad(v, (0, LANE - v.shape[0])).reshape(1, LANE)
    w_out = p["w_out"][i]
    return {
        "g_pre1": p["norm_pre_mix"][i].reshape(1, D_MODEL),
        "g_post1": p["norm_post_mix"][i].reshape(1, D_MODEL),
        "g_pre2": p["norm_pre_ffn"][i].reshape(1, D_MODEL),
        "g_post2": p["norm_post_ffn"][i].reshape(1, D_MODEL),
        "w_in": w_in_pad,
        "w_dt": w_dt.T.astype(BF16),
        "dtb_row": padl(dtb),
        "dtb_col": dtb.reshape(-1, 1),
        "alog_row": padl(alog),
        "alog_col": alog.reshape(-1, 1),
        "q_norm": p["q_norm"][i].reshape(1, -1),
        "w_q": w_q.T.astype(BF16),
        "w_qrot": w_qrot.T.astype(BF16),
        "kv_norm": p["kv_norm"][i].reshape(1, -1),
        "w_uk": w_uk.astype(BF16),
        "w_uv": w_uv.T.astype(BF16),
        "conv_w": p["conv_w"][i],
        "conv_b": p["conv_b"][i].reshape(1, -1),
        "dskip_row": jnp.repeat(p["d_skip"][i], SSD_HEAD_DIM).reshape(1, -1),
        "ssd_norm": p["ssd_norm"][i].reshape(1, -1),
        "mla_norm_col": p["mla_norm"][i].reshape(-1, 1),
        "w_out_a": w_out[:SSD_WIDTH].astype(BF16),
        "w_out_b": w_out[SSD_WIDTH:].astype(BF16),
    }


def _rope_tables(n_tokens):
    rows = n_tokens // GRID_W
    row = jnp.repeat(jnp.arange(rows, dtype=F32), GRID_W)
    col = jnp.tile(jnp.arange(GRID_W, dtype=F32), rows)
    half = MLA_ROPE_DIM // 2
    inv = ROPE_THETA ** (-jnp.arange(0, half, 2, dtype=F32) / half)
    ar = row[:, None] * inv[None, :]
    ac = col[:, None] * inv[None, :]
    ang = jnp.concatenate([ar, ar, ac, ac], axis=-1)
    return jnp.cos(ang), jnp.sin(ang)


def _attn_tables(cos, sin, n):
    scale = (MLA_NOPE_DIM + MLA_ROPE_DIM) ** -0.5 * math.log2(math.e)
    pad = HEAD_PAD - ROPE_LANE0 - MLA_ROPE_DIM
    cosk = jnp.concatenate([jnp.zeros((n, ROPE_LANE0), F32), cos, jnp.zeros((n, pad), F32)], axis=1)
    sink = jnp.concatenate([jnp.zeros((n, ROPE_LANE0), F32), sin, jnp.zeros((n, pad), F32)], axis=1)
    cosq = jnp.concatenate([jnp.ones((n, ROPE_LANE0), F32), cos, jnp.zeros((n, pad), F32)], axis=1)
    return (cosq * scale).T, (sink * scale).T, cosk, sink


def kernel(x_prompt, x_sample, cache_ckv, cache_krope, state_ssm, c, c_ctx, w_mod, b_mod, norm_pre_mix, norm_post_mix, norm_pre_ffn, norm_post_ffn, w_in, conv_w, conv_b, dt_bias, a_log, d_skip, ssd_norm, q_norm, w_uq, kv_norm, w_ukv, mla_norm, w_out, ffn_w_gate, ffn_w_up, ffn_w_down, moe_router, moe_w_gate, moe_w_up, moe_w_down):
    params = dict(w_in=w_in, conv_w=conv_w, conv_b=conv_b, dt_bias=dt_bias, a_log=a_log, d_skip=d_skip,
                  ssd_norm=ssd_norm, q_norm=q_norm, w_uq=w_uq, kv_norm=kv_norm, w_ukv=w_ukv,
                  mla_norm=mla_norm, w_out=w_out, norm_pre_mix=norm_pre_mix, norm_post_mix=norm_post_mix,
                  norm_pre_ffn=norm_pre_ffn, norm_post_ffn=norm_post_ffn)
    batch, seq, d = x_prompt.shape
    dec_batch, dec_seq, _ = x_sample.shape
    depth = w_in.shape[0]
    past = cache_ckv.shape[2]
    tm = ROW_TILE

    cvec = jnp.concatenate([c_ctx[None, :], c, jnp.zeros((SUBLANE - 1 - dec_batch, d), F32)], axis=0)
    mod = _modulation(cvec, w_mod, b_mod)

    ones = jnp.ones((tm, MLA_ROPE_DIM), F32)
    tabs_ctx = _attn_tables(ones, jnp.zeros_like(ones), tm)
    cos, sin = _rope_tables(dec_seq)
    tabs_lat = _attn_tables(cos, sin, dec_seq)
    lat_blocks = dec_seq // tm

    xp = x_prompt.reshape(batch * seq, d)
    xs = x_sample.reshape(dec_batch * dec_seq, d)
    new_ckv, new_kr, new_ssm = [], [], []
    for i in range(depth):
        lw = _prep_layer(i, params)
        mod3 = mod[i].reshape(SUBLANE, 1, N_MOD * d)
        j = i // 2
        if i % 2 == 0:
            f = ffn_w_gate.shape[2] // 2
            split = lambda w: jnp.stack([w[:, :f], w[:, f:]], axis=0).astype(BF16)
            wg, wu = split(ffn_w_gate[j]), split(ffn_w_up[j])
            wd = ffn_w_down[j].reshape(2, f, d).astype(BF16)
            router = None
        else:
            wg, wu, wd = moe_w_gate[j].astype(BF16), moe_w_up[j].astype(BF16), moe_w_down[j].astype(BF16)
            router = jnp.pad(moe_router[j], ((0, 0), (0, LANE - N_EXPERTS)))

        def run(x, row_fn, tabs, tab_blocks, seq_len, heads_per_step, ctx):
            z, xbc, dt, dtt, qt, k, vt, ckvn, kr = _inproj(x, mod3, row_fn, lw, tabs, tab_blocks)
            cps = seq_len // SSD_CHUNK
            h0f = None if ctx is None else ctx[2][:, 0]
            h0b = None if ctx is None else ctx[2][:, 1]
            yf, hf = _ssd(xbc, dt, dtt, lw, cps, 0, h0=h0f)
            yssd, hb = _ssd(xbc, dt, dtt, lw, cps, 1, h0=h0b, z=z, yf=yf)
            cache = None
            if ctx is not None:
                kr_tile = jnp.pad(ctx[1].reshape(-1, MLA_ROPE_DIM),
                                  ((0, 0), (ROPE_LANE0, HEAD_PAD - ROPE_LANE0 - MLA_ROPE_DIM)))
                cache = _kvcache(ctx[0].reshape(-1, MLA_KV_RANK), kr_tile, lw)
            ot = _attention(qt, k, vt, seq_len, heads_per_step, cache=cache)
            outs = _outproj(yssd, ot, x, mod3, row_fn, lw, router=router)
            x1, h2 = outs[0], outs[1]
            comb = outs[2] if router is not None else None
            x2 = _ffn(h2, x1, mod3, row_fn, wg, wu, wd, lw["g_post2"], comb=comb)
            return x2, ckvn, kr, hf, hb

        xp, ckvn, kr, hf, hb = run(xp, lambda b: 0, tabs_ctx, 1, seq, MLA_HEADS, None)
        new_ckv.append(ckvn.reshape(batch, seq, MLA_KV_RANK))
        new_kr.append(kr.reshape(batch, seq, MLA_ROPE_DIM))
        new_ssm.append(jnp.stack([hf, hb], axis=1))
        xs, _, _, _, _ = run(xs, lambda b: 1 + b // lat_blocks, tabs_lat, lat_blocks, dec_seq, MLA_HEADS,
                             (cache_ckv[:, i], cache_krope[:, i], state_ssm[:, i]))
    return (xp.reshape(batch, seq, d), xs.reshape(dec_batch, dec_seq, d),
            jnp.stack(new_ckv, axis=1), jnp.stack(new_kr, axis=1), jnp.stack(new_ssm, axis=1))
```

```python
import functools
import math

import jax
import jax.numpy as jnp
from jax import lax
from jax.experimental import pallas as pl
from jax.experimental.pallas import tpu as pltpu

F32 = jnp.float32
BF16 = jnp.bfloat16

D_MODEL = 1024
GRID_W = 64
SSD_WIDTH = 512
SSD_HEAD_DIM = 64
SSD_HEADS = 8
SSD_GROUPS = 2
SSD_STATE = 64
SSD_CONV = 5
SSD_CHUNK = 128
SSD_XBC = SSD_WIDTH + 2 * SSD_GROUPS * SSD_STATE
MLA_WIDTH = 512
MLA_V_DIM = 64
MLA_HEADS = 8
MLA_NOPE_DIM = 64
MLA_ROPE_DIM = 32
MLA_Q_RANK = 384
MLA_KV_RANK = 256
ROPE_THETA = 10000.0
N_EXPERTS = 8
N_MOD = 6
EPS = 1e-6

LANE = 128
SUBLANE = 8
HEAD_PAD = 128
ONES_ROWS = 16
ROPE_LANE0 = MLA_NOPE_DIM
C_Z = 0
C_XBC = C_Z + SSD_WIDTH
C_CQ = C_XBC + SSD_XBC
C_CKV = C_CQ + MLA_Q_RANK
C_TA = C_CKV + MLA_KV_RANK
C_TB = C_TA + LANE
IN_PAD = C_TB + LANE

VMEM_LIMIT = 56 * 1024 * 1024

ROW_TILE = 512
ATTN_Q_TILE = 256
ATTN_KEY_BLOCK = 512
FFN_ROW_TILE = 512

NT_DIMS = (((1,), (1,)), ((), ()))
TN_DIMS = (((0,), (0,)), ((), ()))


def _cparams(*sem):
    return pltpu.CompilerParams(dimension_semantics=sem, vmem_limit_bytes=VMEM_LIMIT)


def _silu(x):
    return x / (1.0 + jnp.exp(-x))


def _softplus(x):
    return jnp.maximum(x, 0.0) + jnp.log(1.0 + jnp.exp(-jnp.abs(x)))


def _rms_rows(x, g):
    ms = jnp.mean(x * x, axis=-1, keepdims=True)
    return x * lax.rsqrt(ms + EPS) * g


def _dot(a, b):
    return jnp.dot(a, b, preferred_element_type=F32)


def _dot_nt(a, b):
    return lax.dot_general(a, b, NT_DIMS, preferred_element_type=F32)


def _dot_tn(a, b):
    return lax.dot_general(a, b, TN_DIMS, preferred_element_type=F32)


def _split3(x):
    hi = x.astype(BF16)
    r1 = x - hi.astype(F32)
    mid = r1.astype(BF16)
    lo = (r1 - mid.astype(F32)).astype(BF16)
    return hi, mid, lo


def _mod_kernel(c_ref, w_ref, b_ref, o_ref):
    s = _silu(c_ref[...]).astype(BF16)
    o_ref[0] = _dot(s, w_ref[0].astype(BF16)) + b_ref[0]


def _modulation(cvec, w_mod, b_mod):
    depth, d, n = w_mod.shape
    tn = 1536
    return pl.pallas_call(
        _mod_kernel,
        grid=(depth, n // tn),
        in_specs=[
            pl.BlockSpec((SUBLANE, d), lambda l, j: (0, 0)),
            pl.BlockSpec((1, d, tn), lambda l, j: (l, 0, j)),
            pl.BlockSpec((1, 1, tn), lambda l, j: (l, 0, j)),
        ],
        out_specs=pl.BlockSpec((1, SUBLANE, tn), lambda l, j: (l, 0, j)),
        out_shape=jax.ShapeDtypeStruct((depth, SUBLANE, n), F32),
        compiler_params=_cparams("arbitrary", "arbitrary"),
        name="modulation",
    )(cvec, w_mod, b_mod.reshape(depth, 1, n))


def _inproj_kernel(x_ref, mod_ref, gpre_ref, win_ref, wdt_ref, dtb_row_ref, dtb_col_ref,
                   qn_ref, wq_ref, wqrot_ref, kvn_ref, wuk_ref, wuv_ref,
                   cosq_ref, sinq_ref, cosk_ref, sink_ref,
                   z_ref, xbc_ref, dt_ref, dtt_ref, qt_ref, k_ref, vt_ref, ckvn_ref, kr_ref):
    mod = mod_ref[0]
    shift = mod[:, 0:D_MODEL]
    scale = mod[:, D_MODEL:2 * D_MODEL]
    h = _rms_rows(x_ref[...], gpre_ref[...]) * (1.0 + scale) + shift
    hb = h.astype(BF16)
    proj = _dot(hb, win_ref[...])
    z_ref[...] = proj[:, C_Z:C_XBC]
    xbc_ref[...] = proj[:, C_XBC:C_CQ]
    cqn = _rms_rows(proj[:, C_CQ:C_CKV], qn_ref[...]).astype(BF16)
    ckvn = _rms_rows(proj[:, C_CKV:C_TA], kvn_ref[...])
    ckvn_ref[...] = ckvn
    ckvb = ckvn.astype(BF16)
    ta = proj[:, C_TA:C_TB]
    tb = proj[:, C_TB:IN_PAD]
    dt_ref[...] = _softplus(ta + dtb_row_ref[...])
    kr_ref[...] = ta[:, ROPE_LANE0:ROPE_LANE0 + MLA_ROPE_DIM]
    kr_rot = ta * cosk_ref[...] + tb * sink_ref[...]
    knp = _dot(ckvb, wuk_ref[...])
    for hd in range(MLA_HEADS):
        sl = slice(hd * HEAD_PAD, (hd + 1) * HEAD_PAD)
        k_ref[:, sl] = (knp[:, sl] + kr_rot).astype(BF16)
    vt_ref[...] = _dot_nt(wuv_ref[...], ckvb).astype(BF16)
    qt = _dot_nt(wq_ref[...], cqn)
    qrt = _dot_nt(wqrot_ref[...], cqn)
    cosq = cosq_ref[...]
    sinq = sinq_ref[...]
    for hd in range(MLA_HEADS):
        sl = slice(hd * HEAD_PAD, (hd + 1) * HEAD_PAD)
        qt_ref[sl, :] = (qt[sl, :] * cosq + qrt[sl, :] * sinq).astype(BF16)
    dtt_ref[...] = _softplus(_dot_nt(wdt_ref[...], hb) + dtb_col_ref[...])


def _inproj(x, mod3, mod_row_fn, lw, tabs, tab_blocks):
    t = x.shape[0]
    tm = ROW_TILE
    nb = t // tm
    cosq, sinq, cosk, sink = tabs
    ntab = tab_blocks
    const = lambda i: (0, 0)
    row = lambda i: (i, 0)
    col = lambda i: (0, i)
    in_specs = [
        pl.BlockSpec((tm, D_MODEL), row),
        pl.BlockSpec((1, 1, N_MOD * D_MODEL), lambda i: (mod_row_fn(i), 0, 0)),
        pl.BlockSpec((1, D_MODEL), const),
        pl.BlockSpec((D_MODEL, IN_PAD), const),
        pl.BlockSpec((2 * SSD_HEADS, D_MODEL), const),
        pl.BlockSpec((1, LANE), const),
        pl.BlockSpec((2 * SSD_HEADS, 1), const),
        pl.BlockSpec((1, MLA_Q_RANK), const),
        pl.BlockSpec((MLA_HEADS * HEAD_PAD, MLA_Q_RANK), const),
        pl.BlockSpec((MLA_HEADS * HEAD_PAD, MLA_Q_RANK), const),
        pl.BlockSpec((1, MLA_KV_RANK), const),
        pl.BlockSpec((MLA_KV_RANK, MLA_HEADS * HEAD_PAD), const),
        pl.BlockSpec((MLA_WIDTH, MLA_KV_RANK), const),
        pl.BlockSpec((HEAD_PAD, tm), lambda i: (0, i % ntab)),
        pl.BlockSpec((HEAD_PAD, tm), lambda i: (0, i % ntab)),
        pl.BlockSpec((tm, LANE), lambda i: (i % ntab, 0)),
        pl.BlockSpec((tm, LANE), lambda i: (i % ntab, 0)),
    ]
    out_specs = [
        pl.BlockSpec((tm, SSD_WIDTH), row),
        pl.BlockSpec((tm, SSD_XBC), row),
        pl.BlockSpec((tm, LANE), row),
        pl.BlockSpec((2 * SSD_HEADS, tm), col),
        pl.BlockSpec((MLA_HEADS * HEAD_PAD, tm), col),
        pl.BlockSpec((tm, MLA_HEADS * HEAD_PAD), row),
        pl.BlockSpec((MLA_WIDTH, tm), col),
        pl.BlockSpec((tm, MLA_KV_RANK), row),
        pl.BlockSpec((tm, MLA_ROPE_DIM), row),
    ]
    out_shape = [
        jax.ShapeDtypeStruct((t, SSD_WIDTH), F32),
        jax.ShapeDtypeStruct((t, SSD_XBC), F32),
        jax.ShapeDtypeStruct((t, LANE), F32),
        jax.ShapeDtypeStruct((2 * SSD_HEADS, t), F32),
        jax.ShapeDtypeStruct((MLA_HEADS * HEAD_PAD, t), BF16),
        jax.ShapeDtypeStruct((t, MLA_HEADS * HEAD_PAD), BF16),
        jax.ShapeDtypeStruct((MLA_WIDTH, t), BF16),
        jax.ShapeDtypeStruct((t, MLA_KV_RANK), F32),
        jax.ShapeDtypeStruct((t, MLA_ROPE_DIM), F32),
    ]
    return pl.pallas_call(
        _inproj_kernel,
        grid=(nb,),
        in_specs=in_specs,
        out_specs=out_specs,
        out_shape=out_shape,
        compiler_params=_cparams("arbitrary"),
        name="inproj",
    )(x, mod3, lw["g_pre1"], lw["w_in"], lw["w_dt"], lw["dtb_row"], lw["dtb_col"],
      lw["q_norm"], lw["w_q"], lw["w_qrot"], lw["kv_norm"], lw["w_uk"], lw["w_uv"],
      cosq, sinq, cosk, sink)


def _kvcache_kernel(ckv_ref, kr_ref, wuk_ref, wuv_ref, k_ref, vt_ref):
    ckvb = ckv_ref[...].astype(BF16)
    knp = _dot(ckvb, wuk_ref[...])
    kr = kr_ref[...]
    for hd in range(MLA_HEADS):
        sl = slice(hd * HEAD_PAD, (hd + 1) * HEAD_PAD)
        k_ref[:, sl] = (knp[:, sl] + kr).astype(BF16)
    vt_ref[...] = _dot_nt(wuv_ref[...], ckvb).astype(BF16)


def _kvcache(ckv, kr_tile, lw):
    n = ckv.shape[0]
    tm = 512
    return pl.pallas_call(
        _kvcache_kernel,
        grid=(n // tm,),
        in_specs=[
            pl.BlockSpec((tm, MLA_KV_RANK), lambda i: (i, 0)),
            pl.BlockSpec((tm, LANE), lambda i: (i, 0)),
            pl.BlockSpec((MLA_KV_RANK, MLA_HEADS * HEAD_PAD), lambda i: (0, 0)),
            pl.BlockSpec((MLA_WIDTH, MLA_KV_RANK), lambda i: (0, 0)),
        ],
        out_specs=[
            pl.BlockSpec((tm, MLA_HEADS * HEAD_PAD), lambda i: (i, 0)),
            pl.BlockSpec((MLA_WIDTH, tm), lambda i: (0, i)),
        ],
        out_shape=[
            jax.ShapeDtypeStruct((n, MLA_HEADS * HEAD_PAD), BF16),
            jax.ShapeDtypeStruct((MLA_WIDTH, n), BF16),
        ],
        compiler_params=_cparams("arbitrary"),
        name="kvcache",
    )(ckv, kr_tile, lw["w_uk"], lw["w_uv"])


def _head_expand_matrix():
    r = lax.broadcasted_iota(jnp.int32, (LANE, 2 * SSD_WIDTH), 0)
    c = lax.broadcasted_iota(jnp.int32, (LANE, 2 * SSD_WIDTH), 1)
    return jnp.where(c // SSD_HEAD_DIM == r, 1.0, 0.0).astype(BF16)


def _expand_heads(v, emat):
    hi = v.astype(BF16)
    mid = (v - hi.astype(F32)).astype(BF16)
    return _dot(hi, emat) + _dot(mid, emat)


def _prefix_rows(dta, tril):
    return sum(_dot(tril, p) for p in _split3(dta))


def _chunk_masks(q):
    r_i = lax.broadcasted_iota(jnp.int32, (q, q), 0)
    c_i = lax.broadcasted_iota(jnp.int32, (q, q), 1)
    return r_i >= c_i, r_i <= c_i


def _ssd_state_kernel(*refs, cps, has_h0):
    it = iter(refs)
    xbc_ref, prev_ref, next_ref, dt_ref, cw_ref, cb_ref, alog_row_ref = (next(it) for _ in range(7))
    h0_ref = next(it) if has_h0 else None
    xcb_ref, hsf_ref, sb_ref, dec_ref, hfin_ref, st_ref = (next(it) for _ in range(6))

    q = SSD_CHUNK
    g = pl.program_id(0)
    pos = g % cps
    seq_first = pos == 0
    seq_last = pos == cps - 1

    prev = jnp.where(seq_first, 0.0, prev_ref[...])
    nxt = jnp.where(seq_last, 0.0, next_ref[...])
    ext = jnp.concatenate([prev, xbc_ref[...], nxt], axis=0)
    cw = cw_ref[...]
    acc = cb_ref[...] + ext[SUBLANE - 2:SUBLANE - 2 + q] * cw[0:1]
    for k in range(1, SSD_CONV):
        o = SUBLANE - 2 + k
        acc = acc + ext[o:o + q] * cw[k:k + 1]
    xc = _silu(acc)
    xcb = xc.astype(BF16)
    xcb_ref[...] = xcb
    xs = xcb[:, :SSD_WIDTH].astype(F32)

    lower, _ = _chunk_masks(q)
    tril = jnp.where(lower, 1.0, 0.0).astype(BF16)
    dt = dt_ref[...]
    a_row = -jnp.exp(alog_row_ref[...])
    dta = dt * a_row
    la = _prefix_rows(dta, tril)
    tot = la[q - 1:q, :]
    lane = lax.broadcasted_iota(jnp.int32, (q, LANE), 1)
    w = jnp.exp(jnp.where(lane < SSD_HEADS, tot - la, la - dta)) * dt
    w = jnp.where(lane < 2 * SSD_HEADS, w, 0.0)
    emat = _head_expand_matrix()
    wexp = _expand_heads(w, emat)
    lane_t = lax.broadcasted_iota(jnp.int32, (2 * SUBLANE, LANE), 1)
    etot = jnp.where(lane_t < 2 * SSD_HEADS, jnp.exp(jnp.broadcast_to(tot, (2 * SUBLANE, LANE))), 0.0)
    dec = _expand_heads(etot, emat)[:SUBLANE]
    dec_ref[0] = dec

    gw = SSD_WIDTH // SSD_GROUPS
    bmb = xcb[:, SSD_WIDTH:SSD_WIDTH + SSD_GROUPS * SSD_STATE]
    states = []
    for d in range(2):
        xw = (xs * wexp[:, d * SSD_WIDTH:(d + 1) * SSD_WIDTH]).astype(BF16)
        parts = [_dot_tn(bmb[:, grp * SSD_STATE:(grp + 1) * SSD_STATE], xw[:, grp * gw:(grp + 1) * gw])
                 for grp in range(SSD_GROUPS)]
        states.append(jnp.concatenate(parts, axis=1))
    sb_ref[0] = states[1]

    @pl.when(seq_first)
    def _():
        if has_h0:
            st_ref[...] = h0_ref[0]
        else:
            st_ref[...] = jnp.zeros_like(st_ref)

    hs = st_ref[...]
    hsf_ref[0] = hs.astype(BF16)
    hnew = hs * dec[0:1, :SSD_WIDTH] + states[0]
    st_ref[...] = hnew
    hfin_ref[0] = hnew


def _ssd_out_kernel(*refs, cps, has_h0, nchunks):
    it = iter(refs)
    (xcb_ref, dt_ref, dtt_ref, z_ref, hsf_ref, sb_ref, dec_ref,
     alog_row_ref, alog_col_ref, dskip_ref, gn_ref) = (next(it) for _ in range(11))
    h0_ref = next(it) if has_h0 else None
    y_ref, hfin_ref, st_ref = next(it), next(it), next(it)

    q = SSD_CHUNK
    g = nchunks - 1 - pl.program_id(0)
    pos = g % cps
    seq_last = pos == cps - 1
    log2e = math.log2(math.e)

    xcb = xcb_ref[...]
    xsb = xcb[:, :SSD_WIDTH]
    bmb = xcb[:, SSD_WIDTH:SSD_WIDTH + SSD_GROUPS * SSD_STATE]
    cmb = xcb[:, SSD_WIDTH + SSD_GROUPS * SSD_STATE:]

    lower, upper = _chunk_masks(q)
    tril = jnp.where(lower, 1.0, 0.0).astype(BF16)
    triu = jnp.where(upper, 1.0, 0.0).astype(BF16)
    dt = dt_ref[...]
    dtt = dtt_ref[...]
    dta = dt * (-jnp.exp(alog_row_ref[...]))
    dtat = dtt * (-jnp.exp(alog_col_ref[...]))
    la = _prefix_rows(dta, tril)
    tot = la[q - 1:q, :]
    lane = lax.broadcasted_iota(jnp.int32, (q, LANE), 1)
    lcol = jnp.where(lane < SSD_HEADS, la, tot - la + dta)
    lat = sum(_dot(p, triu) for p in _split3(dtat))
    tott = lat[:, q - 1:q]
    rowi = lax.broadcasted_iota(jnp.int32, (2 * SSD_HEADS, q), 0)
    lrow = jnp.where(rowi < SSD_HEADS, lat, tott - lat + dtat)
    lcol2 = lcol * log2e
    lrow2 = (lrow - jnp.log(dtt)) * log2e
    ecol = jnp.where(lane < 2 * SSD_HEADS, jnp.exp(lcol), 0.0)
    emat = _head_expand_matrix()
    eexp = _expand_heads(ecol, emat)

    @pl.when(seq_last)
    def _():
        if has_h0:
            st_ref[...] = h0_ref[0]
        else:
            st_ref[...] = jnp.zeros_like(st_ref)

    hb = st_ref[...]
    hsb = hb.astype(BF16)
    hsf = hsf_ref[0]

    rep = SSD_HEADS // SSD_GROUPS
    neg = jnp.float32(-jnp.inf)
    lane_q = lax.broadcasted_iota(jnp.int32, (q, LANE), 1)
    cbs = []
    for grp in range(SSD_GROUPS):
        cg = cmb[:, grp * SSD_STATE:(grp + 1) * SSD_STATE]
        bg = bmb[:, grp * SSD_STATE:(grp + 1) * SSD_STATE]
        cbs.append(_dot_nt(cg, bg))
    tiles = []
    for pair in range(SSD_HEADS // 2):
        xpair = xsb[:, pair * LANE:(pair + 1) * LANE]
        res = []
        for hd in (2 * pair, 2 * pair + 1):
            jf, jb = hd, SSD_HEADS + hd
            ef = jnp.exp2(jnp.where(lower, lcol2[:, jf:jf + 1] - lrow2[jf:jf + 1, :], neg))
            eb = jnp.exp2(jnp.where(upper, lcol2[:, jb:jb + 1] - lrow2[jb:jb + 1, :], neg))
            mm = (cbs[hd // rep] * (ef + eb)).astype(BF16)
            res.append(_dot(mm, xpair))
        tiles.append(jnp.where(lane_q < SSD_HEAD_DIM, res[0], res[1]))
    y = jnp.concatenate(tiles, axis=1)

    gw = SSD_WIDTH // SSD_GROUPS
    for d, hst in ((0, hsf), (1, hsb)):
        parts = [_dot(cmb[:, grp * SSD_STATE:(grp + 1) * SSD_STATE], hst[:, grp * gw:(grp + 1) * gw])
                 for grp in range(SSD_GROUPS)]
        y = y + jnp.concatenate(parts, axis=1) * eexp[:, d * SSD_WIDTH:(d + 1) * SSD_WIDTH]

    y = y + dskip_ref[...] * xsb.astype(F32)
    y = y * _silu(z_ref[...])
    y_ref[...] = _rms_rows(y, gn_ref[...]).astype(BF16)

    hnew = hb * dec_ref[0][0:1, SSD_WIDTH:] + sb_ref[0]
    st_ref[...] = hnew
    hfin_ref[0] = hnew


def _ssd(xbc, dt, dtt, z, lw, cps, h0=None):
    t = xbc.shape[0]
    q = SSD_CHUNK
    nchunks = t // q
    nseq = nchunks // cps
    hb = q // SUBLANE
    n8 = t // SUBLANE
    has_h0 = h0 is not None
    const = lambda i: (0, 0)
    st_block = (1, SSD_STATE, SSD_WIDTH)
    dec_block = (1, SUBLANE, 2 * SSD_WIDTH)

    in_specs = [
        pl.BlockSpec((q, SSD_XBC), lambda i: (i, 0)),
        pl.BlockSpec((SUBLANE, SSD_XBC), lambda i: (jnp.maximum(i * hb - 1, 0), 0)),
        pl.BlockSpec((SUBLANE, SSD_XBC), lambda i: (jnp.minimum((i + 1) * hb, n8 - 1), 0)),
        pl.BlockSpec((q, LANE), lambda i: (i, 0)),
        pl.BlockSpec((SSD_CONV, SSD_XBC), const),
        pl.BlockSpec((1, SSD_XBC), const),
        pl.BlockSpec((1, LANE), const),
    ]
    args = [xbc, xbc, xbc, dt, lw["conv_w"], lw["conv_b"], lw["alog_row"]]
    if has_h0:
        in_specs.append(pl.BlockSpec(st_block, lambda i: (i // cps, 0, 0)))
        args.append(h0[0])
    xcb, hsf, sb, dec, hfin_f = pl.pallas_call(
        functools.partial(_ssd_state_kernel, cps=cps, has_h0=has_h0),
        grid=(nchunks,),
        in_specs=in_specs,
        out_specs=[
            pl.BlockSpec((q, SSD_XBC), lambda i: (i, 0)),
            pl.BlockSpec(st_block, lambda i: (i, 0, 0)),
            pl.BlockSpec(st_block, lambda i: (i, 0, 0)),
            pl.BlockSpec(dec_block, lambda i: (i, 0, 0)),
            pl.BlockSpec(st_block, lambda i: (i // cps, 0, 0)),
        ],
        out_shape=[
            jax.ShapeDtypeStruct((t, SSD_XBC), BF16),
            jax.ShapeDtypeStruct((nchunks, SSD_STATE, SSD_WIDTH), BF16),
            jax.ShapeDtypeStruct((nchunks, SSD_STATE, SSD_WIDTH), F32),
            jax.ShapeDtypeStruct((nchunks, SUBLANE, 2 * SSD_WIDTH), F32),
            jax.ShapeDtypeStruct((nseq, SSD_STATE, SSD_WIDTH), F32),
        ],
        scratch_shapes=[pltpu.VMEM((SSD_STATE, SSD_WIDTH), F32)],
        compiler_params=_cparams("arbitrary"),
        name="ssd_state",
    )(*args)

    gi = lambda i: nchunks - 1 - i
    in_specs = [
        pl.BlockSpec((q, SSD_XBC), lambda i: (gi(i), 0)),
        pl.BlockSpec((q, LANE), lambda i: (gi(i), 0)),
        pl.BlockSpec((2 * SSD_HEADS, q), lambda i: (0, gi(i))),
        pl.BlockSpec((q, SSD_WIDTH), lambda i: (gi(i), 0)),
        pl.BlockSpec(st_block, lambda i: (gi(i), 0, 0)),
        pl.BlockSpec(st_block, lambda i: (gi(i), 0, 0)),
        pl.BlockSpec(dec_block, lambda i: (gi(i), 0, 0)),
        pl.BlockSpec((1, LANE), const),
        pl.BlockSpec((2 * SSD_HEADS, 1), const),
        pl.BlockSpec((1, SSD_WIDTH), const),
        pl.BlockSpec((1, SSD_WIDTH), const),
    ]
    args = [xcb, dt, dtt, z, hsf, sb, dec, lw["alog_row"], lw["alog_col"], lw["dskip_row"], lw["ssd_norm"]]
    if has_h0:
        in_specs.append(pl.BlockSpec(st_block, lambda i: (gi(i) // cps, 0, 0)))
        args.append(h0[1])
    y, hfin_b = pl.pallas_call(
        functools.partial(_ssd_out_kernel, cps=cps, has_h0=has_h0, nchunks=nchunks),
        grid=(nchunks,),
        in_specs=in_specs,
        out_specs=[
            pl.BlockSpec((q, SSD_WIDTH), lambda i: (gi(i), 0)),
            pl.BlockSpec(st_block, lambda i: (gi(i) // cps, 0, 0)),
        ],
        out_shape=[
            jax.ShapeDtypeStruct((t, SSD_WIDTH), BF16),
            jax.ShapeDtypeStruct((nseq, SSD_STATE, SSD_WIDTH), F32),
        ],
        scratch_shapes=[pltpu.VMEM((SSD_STATE, SSD_WIDTH), F32)],
        compiler_params=_cparams("arbitrary"),
        name="ssd_out",
    )(*args)
    return y, hfin_f, hfin_b


def _state_to_kernel_layout(h):
    n = h.shape[0]
    return h.transpose(0, 3, 1, 2).reshape(n, SSD_STATE, SSD_WIDTH)


def _state_from_kernel_layout(h):
    n = h.shape[0]
    return h.reshape(n, SSD_STATE, SSD_HEADS, SSD_HEAD_DIM).transpose(0, 2, 3, 1)


def _attn_kernel(*refs, heads, has_cache):
    if has_cache:
        qt_ref, k_ref, vt_ref, kc_ref, vct_ref = refs[:5]
    else:
        qt_ref, k_ref, vt_ref = refs[:3]
    s_ref = refs[-1]
    o_ref = refs[-2]
    lk = k_ref.shape[0]
    kb = min(ATTN_KEY_BLOCK, lk)
    blocks = [(k_ref, vt_ref, i * kb) for i in range(lk // kb)]
    if has_cache:
        lc = kc_ref.shape[0]
        kbc = min(ATTN_KEY_BLOCK, lc)
        blocks += [(kc_ref, vct_ref, i * kbc) for i in range(lc // kbc)]
        assert kbc == kb
    nblk = len(blocks)
    ones = jnp.ones((ONES_ROWS, kb), BF16)

    def score_block(hd, i, m):
        kr, _, off = blocks[i]
        q = qt_ref[hd * HEAD_PAD:(hd + 1) * HEAD_PAD, :]
        s = _dot(kr[off:off + kb, hd * HEAD_PAD:(hd + 1) * HEAD_PAD], q)
        s_ref[hd % 2, i * kb:(i + 1) * kb, :] = s
        bm = jnp.max(s, axis=0, keepdims=True)
        return bm if m is None else jnp.maximum(m, bm)

    def value_block(hd, i, m, acc):
        _, vr, off = blocks[i]
        p = jnp.exp2(s_ref[hd % 2, i * kb:(i + 1) * kb, :] - m).astype(BF16)
        v = vr[hd * MLA_V_DIM:(hd + 1) * MLA_V_DIM, off:off + kb]
        part = _dot(jnp.concatenate([v, ones], axis=0), p)
        return part if acc is None else acc + part

    m_cur = None
    for i in range(nblk):
        m_cur = score_block(0, i, m_cur)
    for hd in range(heads):
        m_next, acc = None, None
        for i in range(nblk):
            if hd + 1 < heads:
                m_next = score_block(hd + 1, i, m_next)
            acc = value_block(hd, i, m_cur, acc)
        vs = slice(hd * MLA_V_DIM, (hd + 1) * MLA_V_DIM)
        o_ref[vs, :] = acc[:MLA_V_DIM] / acc[MLA_V_DIM:MLA_V_DIM + 1]
        m_cur = m_next


def _attention(qt, k, vt, seq_len, heads_per_step, cache=None):
    t = k.shape[0]
    nseq = t // seq_len
    tq = min(ATTN_Q_TILE, seq_len)
    nq = seq_len // tq
    g = heads_per_step
    in_specs = [
        pl.BlockSpec((g * HEAD_PAD, tq), lambda s, h, j: (h, s * nq + j)),
        pl.BlockSpec((seq_len, g * HEAD_PAD), lambda s, h, j: (s, h)),
        pl.BlockSpec((g * MLA_V_DIM, seq_len), lambda s, h, j: (h, s)),
    ]
    args = [qt, k, vt]
    n_keys = seq_len
    if cache is not None:
        kc, vct = cache
        past = kc.shape[0] // nseq
        n_keys += past
        in_specs += [
            pl.BlockSpec((past, g * HEAD_PAD), lambda s, h, j: (s, h)),
            pl.BlockSpec((g * MLA_V_DIM, past), lambda s, h, j: (h, s)),
        ]
        args += [kc, vct]
    kern = functools.partial(_attn_kernel, heads=g, has_cache=cache is not None)
    return pl.pallas_call(
        kern,
        grid=(nseq, MLA_HEADS // g, nq),
        in_specs=in_specs,
        out_specs=pl.BlockSpec((g * MLA_V_DIM, tq), lambda s, h, j: (h, s * nq + j)),
        out_shape=jax.ShapeDtypeStruct((MLA_WIDTH, t), F32),
        scratch_shapes=[pltpu.VMEM((2, n_keys, tq), F32)],
        compiler_params=_cparams("arbitrary", "arbitrary", "arbitrary"),
        name="attention",
    )(*args)


def _outproj_kernel(*refs, has_router):
    if has_router:
        (y_ref, ot_ref, x_ref, mod_ref, wa_ref, wb_ref, gm_ref, gpost_ref, gpre2_ref, rt_ref,
         x1_ref, h2_ref, comb_ref) = refs
    else:
        (y_ref, ot_ref, x_ref, mod_ref, wa_ref, wb_ref, gm_ref, gpost_ref, gpre2_ref,
         x1_ref, h2_ref) = refs
    mod = mod_ref[0]
    gate1 = mod[:, 2 * D_MODEL:3 * D_MODEL]
    shift2 = mod[:, 3 * D_MODEL:4 * D_MODEL]
    scale2 = mod[:, 4 * D_MODEL:5 * D_MODEL]
    ot = ot_ref[...]
    ms = jnp.mean(ot * ot, axis=0, keepdims=True)
    on = (ot * lax.rsqrt(ms + EPS) * gm_ref[...]).astype(BF16)
    y = _dot(y_ref[...], wa_ref[...]) + _dot_tn(on, wb_ref[...])
    x1 = x_ref[...] + gate1 * _rms_rows(y, gpost_ref[...])
    x1_ref[...] = x1
    h2 = _rms_rows(x1, gpre2_ref[...]) * (1.0 + scale2) + shift2
    h2_ref[...] = h2.astype(BF16)
    if has_router:
        hh, hm, _ = _split3(h2)
        rh, rm, _ = _split3(rt_ref[...])
        logits = _dot(hh, rh) + (_dot(hm, rh) + _dot(hh, rm))
        lane = lax.broadcasted_iota(jnp.int32, logits.shape, 1).astype(F32)
        neg = jnp.float32(-jnp.inf)
        lg = jnp.where(lane < N_EXPERTS, logits, neg)
        m1 = jnp.max(lg, axis=-1, keepdims=True)
        i1 = jnp.min(jnp.where(lg == m1, lane, float(LANE)), axis=-1, keepdims=True)
        lg2 = jnp.where(lane == i1, neg, lg)
        m2 = jnp.max(lg2, axis=-1, keepdims=True)
        i2 = jnp.min(jnp.where(lg2 == m2, lane, float(LANE)), axis=-1, keepdims=True)
        e2 = jnp.exp(m2 - m1)
        w1 = 1.0 / (1.0 + e2)
        w2 = e2 / (1.0 + e2)
        comb_ref[...] = jnp.where(lane == i1, w1, 0.0) + jnp.where(lane == i2, w2, 0.0)


def _outproj(yssd, ot, x, mod3, mod_row_fn, lw, router=None):
    t = x.shape[0]
    tm = ROW_TILE
    const = lambda i: (0, 0)
    row = lambda i: (i, 0)
    in_specs = [
        pl.BlockSpec((tm, SSD_WIDTH), row),
        pl.BlockSpec((MLA_WIDTH, tm), lambda i: (0, i)),
        pl.BlockSpec((tm, D_MODEL), row),
        pl.BlockSpec((1, 1, N_MOD * D_MODEL), lambda i: (mod_row_fn(i), 0, 0)),
        pl.BlockSpec((SSD_WIDTH, D_MODEL), const),
        pl.BlockSpec((MLA_WIDTH, D_MODEL), const),
        pl.BlockSpec((MLA_WIDTH, 1), const),
        pl.BlockSpec((1, D_MODEL), const),
        pl.BlockSpec((1, D_MODEL), const),
    ]
    args = [yssd, ot, x, mod3, lw["w_out_a"], lw["w_out_b"], lw["mla_norm_col"],
            lw["g_post1"], lw["g_pre2"]]
    out_specs = [pl.BlockSpec((tm, D_MODEL), row), pl.BlockSpec((tm, D_MODEL), row)]
    out_shape = [jax.ShapeDtypeStruct((t, D_MODEL), F32), jax.ShapeDtypeStruct((t, D_MODEL), BF16)]
    if router is not None:
        in_specs.append(pl.BlockSpec((D_MODEL, LANE), const))
        args.append(router)
        out_specs.append(pl.BlockSpec((tm, LANE), row))
        out_shape.append(jax.ShapeDtypeStruct((t, LANE), F32))
    return pl.pallas_call(
        functools.partial(_outproj_kernel, has_router=router is not None),
        grid=(t // tm,),
        in_specs=in_specs,
        out_specs=out_specs,
        out_shape=out_shape,
        compiler_params=_cparams("arbitrary"),
        name="outproj",
    )(*args)


def _ffn_kernel(*refs, has_comb, nslab):
    if has_comb:
        h_ref, x_ref, comb_ref, mod_ref, wg_ref, wu_ref, wd_ref, gpost_ref, o_ref, acc_ref = refs
    else:
        h_ref, x_ref, mod_ref, wg_ref, wu_ref, wd_ref, gpost_ref, o_ref, acc_ref = refs
    e = pl.program_id(1)
    h = h_ref[...]
    hid = _silu(_dot(h, wg_ref[0])) * _dot(h, wu_ref[0])
    if has_comb:
        comb = comb_ref[...]
        lane = lax.broadcasted_iota(jnp.int32, comb.shape, 1)
        wcol = jnp.sum(jnp.where(lane == e, comb, 0.0), axis=-1, keepdims=True)
        hid = hid * wcol
    part = _dot(hid.astype(BF16), wd_ref[0])

    @pl.when(e == 0)
    def _():
        acc_ref[...] = part

    @pl.when(e > 0)
    def _():
        acc_ref[...] += part

    @pl.when(e == nslab - 1)
    def _():
        gate2 = mod_ref[0][:, 5 * D_MODEL:6 * D_MODEL]
        o_ref[...] = x_ref[...] + gate2 * _rms_rows(acc_ref[...], gpost_ref[...])


def _ffn(h2, x1, mod3, mod_row_fn, wg, wu, wd, gpost, comb=None):
    t = x1.shape[0]
    tm = FFN_ROW_TILE
    nslab, _, f = wg.shape
    row = lambda i, e: (i, 0)
    in_specs = [pl.BlockSpec((tm, D_MODEL), row), pl.BlockSpec((tm, D_MODEL), row)]
    args = [h2, x1]
    if comb is not None:
        in_specs.append(pl.BlockSpec((tm, LANE), row))
        args.append(comb)
    in_specs += [
        pl.BlockSpec((1, 1, N_MOD * D_MODEL), lambda i, e: (mod_row_fn(i), 0, 0)),
        pl.BlockSpec((1, D_MODEL, f), lambda i, e: (e, 0, 0)),
        pl.BlockSpec((1, D_MODEL, f), lambda i, e: (e, 0, 0)),
        pl.BlockSpec((1, f, D_MODEL), lambda i, e: (e, 0, 0)),
        pl.BlockSpec((1, D_MODEL), lambda i, e: (0, 0)),
    ]
    args += [mod3, wg, wu, wd, gpost]
    return pl.pallas_call(
        functools.partial(_ffn_kernel, has_comb=comb is not None, nslab=nslab),
        grid=(t // tm, nslab),
        in_specs=in_specs,
        out_specs=pl.BlockSpec((tm, D_MODEL), row),
        out_shape=jax.ShapeDtypeStruct((t, D_MODEL), F32),
        scratch_shapes=[pltpu.VMEM((tm, D_MODEL), F32)],
        compiler_params=_cparams("arbitrary", "arbitrary"),
        name="ffn",
    )(*args)


_ROT_PERM = tuple(list(range(8, 16)) + list(range(0, 8)) + list(range(24, 32)) + list(range(16, 24)))
_ROT_SIGN = tuple([-1.0] * 8 + [1.0] * 8 + [-1.0] * 8 + [1.0] * 8)


def _rot_cols(w):
    return w[..., jnp.array(_ROT_PERM)] * jnp.array(_ROT_SIGN, F32)


def _prep_layer(i, p):
    w_in = p["w_in"][i]
    s1 = SSD_WIDTH
    s2 = s1 + SSD_XBC
    s3 = s2 + 2 * SSD_HEADS
    s4 = s3 + MLA_Q_RANK
    s5 = s4 + MLA_KV_RANK
    w_z, w_xbc, w_dt, w_cq, w_ckv, w_kr = (w_in[:, :s1], w_in[:, s1:s2], w_in[:, s2:s3],
                                             w_in[:, s3:s4], w_in[:, s4:s5], w_in[:, s5:])
    zc = lambda n: jnp.zeros((D_MODEL, n), F32)
    tile_a = jnp.concatenate([w_dt, zc(ROPE_LANE0 - 2 * SSD_HEADS), w_kr,
                              zc(LANE - ROPE_LANE0 - MLA_ROPE_DIM)], axis=1)
    tile_b = jnp.concatenate([zc(ROPE_LANE0), _rot_cols(w_kr),
                              zc(LANE - ROPE_LANE0 - MLA_ROPE_DIM)], axis=1)
    w_in_pad = jnp.concatenate([w_z, w_xbc, w_cq, w_ckv, tile_a, tile_b], axis=1).astype(BF16)

    w_uq = p["w_uq"][i].reshape(MLA_Q_RANK, MLA_HEADS, MLA_NOPE_DIM + MLA_ROPE_DIM)
    q_nope, q_rope = w_uq[..., :MLA_NOPE_DIM], w_uq[..., MLA_NOPE_DIM:]
    zq = lambda n: jnp.zeros((MLA_Q_RANK, MLA_HEADS, n), F32)
    pad = HEAD_PAD - MLA_NOPE_DIM - MLA_ROPE_DIM
    w_q = jnp.concatenate([q_nope, q_rope, zq(pad)], axis=-1).reshape(MLA_Q_RANK, -1)
    w_qrot = jnp.concatenate([zq(MLA_NOPE_DIM), _rot_cols(q_rope), zq(pad)], axis=-1).reshape(MLA_Q_RANK, -1)

    w_ukv = p["w_ukv"][i].reshape(MLA_KV_RANK, MLA_HEADS, MLA_NOPE_DIM + MLA_V_DIM)
    k_nope, v_w = w_ukv[..., :MLA_NOPE_DIM], w_ukv[..., MLA_NOPE_DIM:]
    w_uk = jnp.concatenate([k_nope, jnp.zeros((MLA_KV_RANK, MLA_HEADS, HEAD_PAD - MLA_NOPE_DIM), F32)],
                           axis=-1).reshape(MLA_KV_RANK, -1)
    w_uv = v_w.reshape(MLA_KV_RANK, MLA_WIDTH)

    dtb = p["dt_bias"][i].reshape(2 * SSD_HEADS)
    alog = p["a_log"][i].reshape(2 * SSD_HEADS)
    padl = lambda v: jnp.pad(v, (0, LANE - v.shape[0])).reshape(1, LANE)
    w_out = p["w_out"][i]
    return {
        "g_pre1": p["norm_pre_mix"][i].reshape(1, D_MODEL),
        "g_post1": p["norm_post_mix"][i].reshape(1, D_MODEL),
        "g_pre2": p["norm_pre_ffn"][i].reshape(1, D_MODEL),
        "g_post2": p["norm_post_ffn"][i].reshape(1, D_MODEL),
        "w_in": w_in_pad,
        "w_dt": w_dt.T.astype(BF16),
        "dtb_row": padl(dtb),
        "dtb_col": dtb.reshape(-1, 1),
        "alog_row": padl(alog),
        "alog_col": alog.reshape(-1, 1),
        "q_norm": p["q_norm"][i].reshape(1, -1),
        "w_q": w_q.T.astype(BF16),
        "w_qrot": w_qrot.T.astype(BF16),
        "kv_norm": p["kv_norm"][i].reshape(1, -1),
        "w_uk": w_uk.astype(BF16),
        "w_uv": w_uv.T.astype(BF16),
        "conv_w": p["conv_w"][i],
        "conv_b": p["conv_b"][i].reshape(1, -1),
        "dskip_row": jnp.repeat(p["d_skip"][i], SSD_HEAD_DIM).reshape(1, -1),
        "ssd_norm": p["ssd_norm"][i].reshape(1, -1),
        "mla_norm_col": p["mla_norm"][i].reshape(-1, 1),
        "w_out_a": w_out[:SSD_WIDTH].astype(BF16),
        "w_out_b": w_out[SSD_WIDTH:].astype(BF16),
    }


def _rope_tables(n_tokens):
    rows = n_tokens // GRID_W
    row = jnp.repeat(jnp.arange(rows, dtype=F32), GRID_W)
    col = jnp.tile(jnp.arange(GRID_W, dtype=F32), rows)
    half = MLA_ROPE_DIM // 2
    inv = ROPE_THETA ** (-jnp.arange(0, half, 2, dtype=F32) / half)
    ar = row[:, None] * inv[None, :]
    ac = col[:, None] * inv[None, :]
    ang = jnp.concatenate([ar, ar, ac, ac], axis=-1)
    return jnp.cos(ang), jnp.sin(ang)


def _attn_tables(cos, sin, n):
    scale = (MLA_NOPE_DIM + MLA_ROPE_DIM) ** -0.5 * math.log2(math.e)
    pad = HEAD_PAD - ROPE_LANE0 - MLA_ROPE_DIM
    cosk = jnp.concatenate([jnp.zeros((n, ROPE_LANE0), F32), cos, jnp.zeros((n, pad), F32)], axis=1)
    sink = jnp.concatenate([jnp.zeros((n, ROPE_LANE0), F32), sin, jnp.zeros((n, pad), F32)], axis=1)
    cosq = jnp.concatenate([jnp.ones((n, ROPE_LANE0), F32), cos, jnp.zeros((n, pad), F32)], axis=1)
    return (cosq * scale).T, (sink * scale).T, cosk, sink


def kernel(x_prompt, x_sample, cache_ckv, cache_krope, state_ssm, c, c_ctx, w_mod, b_mod, norm_pre_mix, norm_post_mix, norm_pre_ffn, norm_post_ffn, w_in, conv_w, conv_b, dt_bias, a_log, d_skip, ssd_norm, q_norm, w_uq, kv_norm, w_ukv, mla_norm, w_out, ffn_w_gate, ffn_w_up, ffn_w_down, moe_router, moe_w_gate, moe_w_up, moe_w_down):
    params = dict(w_in=w_in, conv_w=conv_w, conv_b=conv_b, dt_bias=dt_bias, a_log=a_log, d_skip=d_skip,
                  ssd_norm=ssd_norm, q_norm=q_norm, w_uq=w_uq, kv_norm=kv_norm, w_ukv=w_ukv,
                  mla_norm=mla_norm, w_out=w_out, norm_pre_mix=norm_pre_mix, norm_post_mix=norm_post_mix,
                  norm_pre_ffn=norm_pre_ffn, norm_post_ffn=norm_post_ffn)
    batch, seq, d = x_prompt.shape
    dec_batch, dec_seq, _ = x_sample.shape
    depth = w_in.shape[0]
    past = cache_ckv.shape[2]
    tm = ROW_TILE

    cvec = jnp.concatenate([c_ctx[None, :], c, jnp.zeros((SUBLANE - 1 - dec_batch, d), F32)], axis=0)
    mod = _modulation(cvec, w_mod, b_mod)

    ones = jnp.ones((tm, MLA_ROPE_DIM), F32)
    tabs_ctx = _attn_tables(ones, jnp.zeros_like(ones), tm)
    cos, sin = _rope_tables(dec_seq)
    tabs_lat = _attn_tables(cos, sin, dec_seq)
    lat_blocks = dec_seq // tm

    xp = x_prompt.reshape(batch * seq, d)
    xs = x_sample.reshape(dec_batch * dec_seq, d)
    new_ckv, new_kr, new_ssm = [], [], []
    for i in range(depth):
        lw = _prep_layer(i, params)
        mod3 = mod[i].reshape(SUBLANE, 1, N_MOD * d)
        j = i // 2
        if i % 2 == 0:
            f = ffn_w_gate.shape[2] // 2
            split = lambda w: jnp.stack([w[:, :f], w[:, f:]], axis=0).astype(BF16)
            wg, wu = split(ffn_w_gate[j]), split(ffn_w_up[j])
            wd = ffn_w_down[j].reshape(2, f, d).astype(BF16)
            router = None
        else:
            wg, wu, wd = moe_w_gate[j].astype(BF16), moe_w_up[j].astype(BF16), moe_w_down[j].astype(BF16)
            router = jnp.pad(moe_router[j], ((0, 0), (0, LANE - N_EXPERTS)))

        def run(x, row_fn, tabs, tab_blocks, seq_len, heads_per_step, ctx):
            z, xbc, dt, dtt, qt, k, vt, ckvn, kr = _inproj(x, mod3, row_fn, lw, tabs, tab_blocks)
            cps = seq_len // SSD_CHUNK
            h0 = None
            if ctx is not None:
                h0 = (_state_to_kernel_layout(ctx[2][:, 0]), _state_to_kernel_layout(ctx[2][:, 1]))
            yssd, hf, hb = _ssd(xbc, dt, dtt, z, lw, cps, h0=h0)
            hf, hb = _state_from_kernel_layout(hf), _state_from_kernel_layout(hb)
            cache = None
            if ctx is not None:
                kr_tile = jnp.pad(ctx[1].reshape(-1, MLA_ROPE_DIM),
                                  ((0, 0), (ROPE_LANE0, HEAD_PAD - ROPE_LANE0 - MLA_ROPE_DIM)))
                cache = _kvcache(ctx[0].reshape(-1, MLA_KV_RANK), kr_tile, lw)
            ot = _attention(qt, k, vt, seq_len, heads_per_step, cache=cache)
            outs = _outproj(yssd, ot, x, mod3, row_fn, lw, router=router)
            x1, h2 = outs[0], outs[1]
            comb = outs[2] if router is not None else None
            x2 = _ffn(h2, x1, mod3, row_fn, wg, wu, wd, lw["g_post2"], comb=comb)
            return x2, ckvn, kr, hf, hb

        xp, ckvn, kr, hf, hb = run(xp, lambda b: 0, tabs_ctx, 1, seq, MLA_HEADS, None)
        new_ckv.append(ckvn.reshape(batch, seq, MLA_KV_RANK))
        new_kr.append(kr.reshape(batch, seq, MLA_ROPE_DIM))
        new_ssm.append(jnp.stack([hf, hb], axis=1))
        xs, _, _, _, _ = run(xs, lambda b: 1 + b // lat_blocks, tabs_lat, lat_blocks, dec_seq, MLA_HEADS,
                             (cache_ckv[:, i], cache_krope[:, i], state_ssm[:, i]))
    return (xp.reshape(batch, seq, d), xs.reshape(dec_batch, dec_seq, d),
            jnp.stack(new_ckv, axis=1), jnp.stack(new_kr, axis=1), jnp.stack(new_ssm, axis=1))
```

```python
import functools
import math

import jax
import jax.numpy as jnp
from jax import lax
from jax.experimental import pallas as pl
from jax.experimental.pallas import tpu as pltpu

F32 = jnp.float32
BF16 = jnp.bfloat16

D_MODEL = 1024
GRID_W = 64
SSD_WIDTH = 512
SSD_HEAD_DIM = 64
SSD_HEADS = 8
SSD_GROUPS = 2
SSD_STATE = 64
SSD_CONV = 5
SSD_CHUNK = 128
SSD_XBC = SSD_WIDTH + 2 * SSD_GROUPS * SSD_STATE
MLA_WIDTH = 512
MLA_V_DIM = 64
MLA_HEADS = 8
MLA_NOPE_DIM = 64
MLA_ROPE_DIM = 32
MLA_Q_RANK = 384
MLA_KV_RANK = 256
ROPE_THETA = 10000.0
N_EXPERTS = 8
N_MOD = 6
EPS = 1e-6

LANE = 128
SUBLANE = 8
HEAD_PAD = 128
ONES_ROWS = 16
ROPE_LANE0 = MLA_NOPE_DIM
C_Z = 0
C_XBC = C_Z + SSD_WIDTH
C_CQ = C_XBC + SSD_XBC
C_CKV = C_CQ + MLA_Q_RANK
C_TA = C_CKV + MLA_KV_RANK
C_TB = C_TA + LANE
IN_PAD = C_TB + LANE

VMEM_LIMIT = 56 * 1024 * 1024

ROW_TILE = 512
ATTN_Q_TILE = 256
ATTN_KEY_BLOCK = 512
FFN_ROW_TILE = 512
MOE_ROW_TILE = 1024
MOE_SUB_TILE = 512
MOE_CAP = 160

NT_DIMS = (((1,), (1,)), ((), ()))
TN_DIMS = (((0,), (0,)), ((), ()))


def _cparams(*sem):
    return pltpu.CompilerParams(dimension_semantics=sem, vmem_limit_bytes=VMEM_LIMIT)


def _silu(x):
    return x / (1.0 + jnp.exp(-x))


def _softplus(x):
    return jnp.maximum(x, 0.0) + jnp.log(1.0 + jnp.exp(-jnp.abs(x)))


def _rms_rows(x, g):
    ms = jnp.mean(x * x, axis=-1, keepdims=True)
    return x * lax.rsqrt(ms + EPS) * g


def _dot(a, b):
    return jnp.dot(a, b, preferred_element_type=F32)


def _dot_nt(a, b):
    return lax.dot_general(a, b, NT_DIMS, preferred_element_type=F32)


def _dot_tn(a, b):
    return lax.dot_general(a, b, TN_DIMS, preferred_element_type=F32)


def _split3(x):
    hi = x.astype(BF16)
    r1 = x - hi.astype(F32)
    mid = r1.astype(BF16)
    lo = (r1 - mid.astype(F32)).astype(BF16)
    return hi, mid, lo


def _mod_kernel(c_ref, w_ref, b_ref, o_ref):
    s = _silu(c_ref[...]).astype(BF16)
    o_ref[0] = _dot(s, w_ref[0].astype(BF16)) + b_ref[0]


def _modulation(cvec, w_mod, b_mod):
    depth, d, n = w_mod.shape
    tn = 1536
    return pl.pallas_call(
        _mod_kernel,
        grid=(depth, n // tn),
        in_specs=[
            pl.BlockSpec((SUBLANE, d), lambda l, j: (0, 0)),
            pl.BlockSpec((1, d, tn), lambda l, j: (l, 0, j)),
            pl.BlockSpec((1, 1, tn), lambda l, j: (l, 0, j)),
        ],
        out_specs=pl.BlockSpec((1, SUBLANE, tn), lambda l, j: (l, 0, j)),
        out_shape=jax.ShapeDtypeStruct((depth, SUBLANE, n), F32),
        compiler_params=_cparams("arbitrary", "arbitrary"),
        name="modulation",
    )(cvec, w_mod, b_mod.reshape(depth, 1, n))


def _inproj_kernel(x_ref, mod_ref, gpre_ref, win_ref, wdt_ref, dtb_row_ref, dtb_col_ref,
                   qn_ref, wq_ref, wqrot_ref, kvn_ref, wuk_ref, wuv_ref,
                   cosq_ref, sinq_ref, cosk_ref, sink_ref,
                   z_ref, xbc_ref, dt_ref, dtt_ref, qt_ref, k_ref, vt_ref, ckvn_ref, kr_ref):
    mod = mod_ref[0]
    shift = mod[:, 0:D_MODEL]
    scale = mod[:, D_MODEL:2 * D_MODEL]
    h = _rms_rows(x_ref[...], gpre_ref[...]) * (1.0 + scale) + shift
    hb = h.astype(BF16)
    proj = _dot(hb, win_ref[...])
    z_ref[...] = proj[:, C_Z:C_XBC]
    xbc_ref[...] = proj[:, C_XBC:C_CQ]
    cqn = _rms_rows(proj[:, C_CQ:C_CKV], qn_ref[...]).astype(BF16)
    ckvn = _rms_rows(proj[:, C_CKV:C_TA], kvn_ref[...])
    ckvn_ref[...] = ckvn
    ckvb = ckvn.astype(BF16)
    ta = proj[:, C_TA:C_TB]
    tb = proj[:, C_TB:IN_PAD]
    dt_ref[...] = _softplus(ta + dtb_row_ref[...])
    kr_ref[...] = ta[:, ROPE_LANE0:ROPE_LANE0 + MLA_ROPE_DIM]
    kr_rot = ta * cosk_ref[...] + tb * sink_ref[...]
    knp = _dot(ckvb, wuk_ref[...])
    for hd in range(MLA_HEADS):
        sl = slice(hd * HEAD_PAD, (hd + 1) * HEAD_PAD)
        k_ref[:, sl] = (knp[:, sl] + kr_rot).astype(BF16)
    vt_ref[...] = _dot_nt(wuv_ref[...], ckvb).astype(BF16)
    qt = _dot_nt(wq_ref[...], cqn)
    qrt = _dot_nt(wqrot_ref[...], cqn)
    cosq = cosq_ref[...]
    sinq = sinq_ref[...]
    for hd in range(MLA_HEADS):
        sl = slice(hd * HEAD_PAD, (hd + 1) * HEAD_PAD)
        qt_ref[sl, :] = (qt[sl, :] * cosq + qrt[sl, :] * sinq).astype(BF16)
    dtt_ref[...] = _softplus(_dot_nt(wdt_ref[...], hb) + dtb_col_ref[...])


def _inproj(x, mod3, mod_row_fn, lw, tabs, tab_blocks):
    t = x.shape[0]
    tm = ROW_TILE
    nb = t // tm
    cosq, sinq, cosk, sink = tabs
    ntab = tab_blocks
    const = lambda i: (0, 0)
    row = lambda i: (i, 0)
    col = lambda i: (0, i)
    in_specs = [
        pl.BlockSpec((tm, D_MODEL), row),
        pl.BlockSpec((1, 1, N_MOD * D_MODEL), lambda i: (mod_row_fn(ROW_TILE)(i), 0, 0)),
        pl.BlockSpec((1, D_MODEL), const),
        pl.BlockSpec((D_MODEL, IN_PAD), const),
        pl.BlockSpec((2 * SSD_HEADS, D_MODEL), const),
        pl.BlockSpec((1, LANE), const),
        pl.BlockSpec((2 * SSD_HEADS, 1), const),
        pl.BlockSpec((1, MLA_Q_RANK), const),
        pl.BlockSpec((MLA_HEADS * HEAD_PAD, MLA_Q_RANK), const),
        pl.BlockSpec((MLA_HEADS * HEAD_PAD, MLA_Q_RANK), const),
        pl.BlockSpec((1, MLA_KV_RANK), const),
        pl.BlockSpec((MLA_KV_RANK, MLA_HEADS * HEAD_PAD), const),
        pl.BlockSpec((MLA_WIDTH, MLA_KV_RANK), const),
        pl.BlockSpec((HEAD_PAD, tm), lambda i: (0, i % ntab)),
        pl.BlockSpec((HEAD_PAD, tm), lambda i: (0, i % ntab)),
        pl.BlockSpec((tm, LANE), lambda i: (i % ntab, 0)),
        pl.BlockSpec((tm, LANE), lambda i: (i % ntab, 0)),
    ]
    out_specs = [
        pl.BlockSpec((tm, SSD_WIDTH), row),
        pl.BlockSpec((tm, SSD_XBC), row),
        pl.BlockSpec((tm, LANE), row),
        pl.BlockSpec((2 * SSD_HEADS, tm), col),
        pl.BlockSpec((MLA_HEADS * HEAD_PAD, tm), col),
        pl.BlockSpec((tm, MLA_HEADS * HEAD_PAD), row),
        pl.BlockSpec((MLA_WIDTH, tm), col),
        pl.BlockSpec((tm, MLA_KV_RANK), row),
        pl.BlockSpec((tm, MLA_ROPE_DIM), row),
    ]
    out_shape = [
        jax.ShapeDtypeStruct((t, SSD_WIDTH), F32),
        jax.ShapeDtypeStruct((t, SSD_XBC), F32),
        jax.ShapeDtypeStruct((t, LANE), F32),
        jax.ShapeDtypeStruct((2 * SSD_HEADS, t), F32),
        jax.ShapeDtypeStruct((MLA_HEADS * HEAD_PAD, t), BF16),
        jax.ShapeDtypeStruct((t, MLA_HEADS * HEAD_PAD), BF16),
        jax.ShapeDtypeStruct((MLA_WIDTH, t), BF16),
        jax.ShapeDtypeStruct((t, MLA_KV_RANK), F32),
        jax.ShapeDtypeStruct((t, MLA_ROPE_DIM), F32),
    ]
    return pl.pallas_call(
        _inproj_kernel,
        grid=(nb,),
        in_specs=in_specs,
        out_specs=out_specs,
        out_shape=out_shape,
        compiler_params=_cparams("arbitrary"),
        name="inproj",
    )(x, mod3, lw["g_pre1"], lw["w_in"], lw["w_dt"], lw["dtb_row"], lw["dtb_col"],
      lw["q_norm"], lw["w_q"], lw["w_qrot"], lw["kv_norm"], lw["w_uk"], lw["w_uv"],
      cosq, sinq, cosk, sink)


def _kvcache_kernel(ckv_ref, kr_ref, wuk_ref, wuv_ref, k_ref, vt_ref):
    ckvb = ckv_ref[...].astype(BF16)
    knp = _dot(ckvb, wuk_ref[...])
    kr = kr_ref[...]
    for hd in range(MLA_HEADS):
        sl = slice(hd * HEAD_PAD, (hd + 1) * HEAD_PAD)
        k_ref[:, sl] = (knp[:, sl] + kr).astype(BF16)
    vt_ref[...] = _dot_nt(wuv_ref[...], ckvb).astype(BF16)


def _kvcache(ckv, kr_tile, lw):
    n = ckv.shape[0]
    tm = 512
    return pl.pallas_call(
        _kvcache_kernel,
        grid=(n // tm,),
        in_specs=[
            pl.BlockSpec((tm, MLA_KV_RANK), lambda i: (i, 0)),
            pl.BlockSpec((tm, LANE), lambda i: (i, 0)),
            pl.BlockSpec((MLA_KV_RANK, MLA_HEADS * HEAD_PAD), lambda i: (0, 0)),
            pl.BlockSpec((MLA_WIDTH, MLA_KV_RANK), lambda i: (0, 0)),
        ],
        out_specs=[
            pl.BlockSpec((tm, MLA_HEADS * HEAD_PAD), lambda i: (i, 0)),
            pl.BlockSpec((MLA_WIDTH, tm), lambda i: (0, i)),
        ],
        out_shape=[
            jax.ShapeDtypeStruct((n, MLA_HEADS * HEAD_PAD), BF16),
            jax.ShapeDtypeStruct((MLA_WIDTH, n), BF16),
        ],
        compiler_params=_cparams("arbitrary"),
        name="kvcache",
    )(ckv, kr_tile, lw["w_uk"], lw["w_uv"])


def _head_expand_matrix():
    r = lax.broadcasted_iota(jnp.int32, (LANE, 2 * SSD_WIDTH), 0)
    c = lax.broadcasted_iota(jnp.int32, (LANE, 2 * SSD_WIDTH), 1)
    return jnp.where(c // SSD_HEAD_DIM == r, 1.0, 0.0).astype(BF16)


def _expand_heads(v, emat):
    hi = v.astype(BF16)
    mid = (v - hi.astype(F32)).astype(BF16)
    return _dot(hi, emat) + _dot(mid, emat)


def _prefix_rows(dta, tril):
    return sum(_dot(tril, p) for p in _split3(dta))


def _chunk_masks(q):
    r_i = lax.broadcasted_iota(jnp.int32, (q, q), 0)
    c_i = lax.broadcasted_iota(jnp.int32, (q, q), 1)
    return r_i >= c_i, r_i <= c_i


def _ssd_state_kernel(*refs, cps, has_h0):
    it = iter(refs)
    xbc_ref, prev_ref, next_ref, dt_ref, cw_ref, cb_ref, alog_row_ref = (next(it) for _ in range(7))
    h0_ref = next(it) if has_h0 else None
    xcb_ref, hsf_ref, sb_ref, dec_ref, hfin_ref, st_ref = (next(it) for _ in range(6))

    q = SSD_CHUNK
    g = pl.program_id(0)
    pos = g % cps
    seq_first = pos == 0
    seq_last = pos == cps - 1

    prev = jnp.where(seq_first, 0.0, prev_ref[...])
    nxt = jnp.where(seq_last, 0.0, next_ref[...])
    ext = jnp.concatenate([prev, xbc_ref[...], nxt], axis=0)
    cw = cw_ref[...]
    acc = cb_ref[...] + ext[SUBLANE - 2:SUBLANE - 2 + q] * cw[0:1]
    for k in range(1, SSD_CONV):
        o = SUBLANE - 2 + k
        acc = acc + ext[o:o + q] * cw[k:k + 1]
    xc = _silu(acc)
    xcb = xc.astype(BF16)
    xcb_ref[...] = xcb
    xs = xcb[:, :SSD_WIDTH].astype(F32)

    lower, _ = _chunk_masks(q)
    tril = jnp.where(lower, 1.0, 0.0).astype(BF16)
    dt = dt_ref[...]
    a_row = -jnp.exp(alog_row_ref[...])
    dta = dt * a_row
    la = _prefix_rows(dta, tril)
    tot = la[q - 1:q, :]
    lane = lax.broadcasted_iota(jnp.int32, (q, LANE), 1)
    w = jnp.exp(jnp.where(lane < SSD_HEADS, tot - la, la - dta)) * dt
    w = jnp.where(lane < 2 * SSD_HEADS, w, 0.0)
    emat = _head_expand_matrix()
    wexp = _expand_heads(w, emat)
    lane_t = lax.broadcasted_iota(jnp.int32, (2 * SUBLANE, LANE), 1)
    etot = jnp.where(lane_t < 2 * SSD_HEADS, jnp.exp(jnp.broadcast_to(tot, (2 * SUBLANE, LANE))), 0.0)
    dec = _expand_heads(etot, emat)[:SUBLANE]
    dec_ref[0] = dec

    gw = SSD_WIDTH // SSD_GROUPS
    bmb = xcb[:, SSD_WIDTH:SSD_WIDTH + SSD_GROUPS * SSD_STATE]
    states = []
    for d in range(2):
        xw = (xs * wexp[:, d * SSD_WIDTH:(d + 1) * SSD_WIDTH]).astype(BF16)
        parts = [_dot_tn(bmb[:, grp * SSD_STATE:(grp + 1) * SSD_STATE], xw[:, grp * gw:(grp + 1) * gw])
                 for grp in range(SSD_GROUPS)]
        states.append(jnp.concatenate(parts, axis=1))
    sb_ref[0] = states[1]

    @pl.when(seq_first)
    def _():
        if has_h0:
            st_ref[...] = h0_ref[0]
        else:
            st_ref[...] = jnp.zeros_like(st_ref)

    hs = st_ref[...]
    hsf_ref[0] = hs.astype(BF16)
    hnew = hs * dec[0:1, :SSD_WIDTH] + states[0]
    st_ref[...] = hnew
    hfin_ref[0] = hnew


def _ssd_out_kernel(*refs, cps, has_h0, nchunks):
    it = iter(refs)
    (xcb_ref, dt_ref, dtt_ref, z_ref, hsf_ref, sb_ref, dec_ref,
     alog_row_ref, alog_col_ref, dskip_ref, gn_ref) = (next(it) for _ in range(11))
    h0_ref = next(it) if has_h0 else None
    y_ref, hfin_ref, st_ref = next(it), next(it), next(it)

    q = SSD_CHUNK
    g = nchunks - 1 - pl.program_id(0)
    pos = g % cps
    seq_last = pos == cps - 1
    log2e = math.log2(math.e)

    xcb = xcb_ref[...]
    xsb = xcb[:, :SSD_WIDTH]
    bmb = xcb[:, SSD_WIDTH:SSD_WIDTH + SSD_GROUPS * SSD_STATE]
    cmb = xcb[:, SSD_WIDTH + SSD_GROUPS * SSD_STATE:]

    lower, upper = _chunk_masks(q)
    tril = jnp.where(lower, 1.0, 0.0).astype(BF16)
    triu = jnp.where(upper, 1.0, 0.0).astype(BF16)
    dt = dt_ref[...]
    dtt = dtt_ref[...]
    dta = dt * (-jnp.exp(alog_row_ref[...]))
    dtat = dtt * (-jnp.exp(alog_col_ref[...]))
    la = _prefix_rows(dta, tril)
    tot = la[q - 1:q, :]
    lane = lax.broadcasted_iota(jnp.int32, (q, LANE), 1)
    lcol = jnp.where(lane < SSD_HEADS, la, tot - la + dta)
    lat = sum(_dot(p, triu) for p in _split3(dtat))
    tott = lat[:, q - 1:q]
    rowi = lax.broadcasted_iota(jnp.int32, (2 * SSD_HEADS, q), 0)
    lrow = jnp.where(rowi < SSD_HEADS, lat, tott - lat + dtat)
    lcol2 = lcol * log2e
    lrow2 = (lrow - jnp.log(dtt)) * log2e
    ecol = jnp.where(lane < 2 * SSD_HEADS, jnp.exp(lcol), 0.0)
    emat = _head_expand_matrix()
    eexp = _expand_heads(ecol, emat)

    @pl.when(seq_last)
    def _():
        if has_h0:
            st_ref[...] = h0_ref[0]
        else:
            st_ref[...] = jnp.zeros_like(st_ref)

    hb = st_ref[...]
    hsb = hb.astype(BF16)
    hsf = hsf_ref[0]

    rep = SSD_HEADS // SSD_GROUPS
    neg = jnp.float32(-jnp.inf)
    lane_q = lax.broadcasted_iota(jnp.int32, (q, LANE), 1)
    cbs = []
    for grp in range(SSD_GROUPS):
        cg = cmb[:, grp * SSD_STATE:(grp + 1) * SSD_STATE]
        bg = bmb[:, grp * SSD_STATE:(grp + 1) * SSD_STATE]
        cbs.append(_dot_nt(cg, bg))
    tiles = []
    for pair in range(SSD_HEADS // 2):
        xpair = xsb[:, pair * LANE:(pair + 1) * LANE]
        res = []
        for hd in (2 * pair, 2 * pair + 1):
            jf, jb = hd, SSD_HEADS + hd
            ef = jnp.exp2(jnp.where(lower, lcol2[:, jf:jf + 1] - lrow2[jf:jf + 1, :], neg))
            eb = jnp.exp2(jnp.where(upper, lcol2[:, jb:jb + 1] - lrow2[jb:jb + 1, :], neg))
            mm = (cbs[hd // rep] * (ef + eb)).astype(BF16)
            res.append(_dot(mm, xpair))
        tiles.append(jnp.where(lane_q < SSD_HEAD_DIM, res[0], res[1]))
    y = jnp.concatenate(tiles, axis=1)

    gw = SSD_WIDTH // SSD_GROUPS
    for d, hst in ((0, hsf), (1, hsb)):
        parts = [_dot(cmb[:, grp * SSD_STATE:(grp + 1) * SSD_STATE], hst[:, grp * gw:(grp + 1) * gw])
                 for grp in range(SSD_GROUPS)]
        y = y + jnp.concatenate(parts, axis=1) * eexp[:, d * SSD_WIDTH:(d + 1) * SSD_WIDTH]

    y = y + dskip_ref[...] * xsb.astype(F32)
    y = y * _silu(z_ref[...])
    y_ref[...] = _rms_rows(y, gn_ref[...]).astype(BF16)

    hnew = hb * dec_ref[0][0:1, SSD_WIDTH:] + sb_ref[0]
    st_ref[...] = hnew
    hfin_ref[0] = hnew


def _ssd(xbc, dt, dtt, z, lw, cps, h0=None):
    t = xbc.shape[0]
    q = SSD_CHUNK
    nchunks = t // q
    nseq = nchunks // cps
    hb = q // SUBLANE
    n8 = t // SUBLANE
    has_h0 = h0 is not None
    const = lambda i: (0, 0)
    st_block = (1, SSD_STATE, SSD_WIDTH)
    dec_block = (1, SUBLANE, 2 * SSD_WIDTH)

    in_specs = [
        pl.BlockSpec((q, SSD_XBC), lambda i: (i, 0)),
        pl.BlockSpec((SUBLANE, SSD_XBC), lambda i: (jnp.maximum(i * hb - 1, 0), 0)),
        pl.BlockSpec((SUBLANE, SSD_XBC), lambda i: (jnp.minimum((i + 1) * hb, n8 - 1), 0)),
        pl.BlockSpec((q, LANE), lambda i: (i, 0)),
        pl.BlockSpec((SSD_CONV, SSD_XBC), const),
        pl.BlockSpec((1, SSD_XBC), const),
        pl.BlockSpec((1, LANE), const),
    ]
    args = [xbc, xbc, xbc, dt, lw["conv_w"], lw["conv_b"], lw["alog_row"]]
    if has_h0:
        in_specs.append(pl.BlockSpec(st_block, lambda i: (i // cps, 0, 0)))
        args.append(h0[0])
    xcb, hsf, sb, dec, hfin_f = pl.pallas_call(
        functools.partial(_ssd_state_kernel, cps=cps, has_h0=has_h0),
        grid=(nchunks,),
        in_specs=in_specs,
        out_specs=[
            pl.BlockSpec((q, SSD_XBC), lambda i: (i, 0)),
            pl.BlockSpec(st_block, lambda i: (i, 0, 0)),
            pl.BlockSpec(st_block, lambda i: (i, 0, 0)),
            pl.BlockSpec(dec_block, lambda i: (i, 0, 0)),
            pl.BlockSpec(st_block, lambda i: (i // cps, 0, 0)),
        ],
        out_shape=[
            jax.ShapeDtypeStruct((t, SSD_XBC), BF16),
            jax.ShapeDtypeStruct((nchunks, SSD_STATE, SSD_WIDTH), BF16),
            jax.ShapeDtypeStruct((nchunks, SSD_STATE, SSD_WIDTH), F32),
            jax.ShapeDtypeStruct((nchunks, SUBLANE, 2 * SSD_WIDTH), F32),
            jax.ShapeDtypeStruct((nseq, SSD_STATE, SSD_WIDTH), F32),
        ],
        scratch_shapes=[pltpu.VMEM((SSD_STATE, SSD_WIDTH), F32)],
        compiler_params=_cparams("arbitrary"),
        name="ssd_state",
    )(*args)

    gi = lambda i: nchunks - 1 - i
    in_specs = [
        pl.BlockSpec((q, SSD_XBC), lambda i: (gi(i), 0)),
        pl.BlockSpec((q, LANE), lambda i: (gi(i), 0)),
        pl.BlockSpec((2 * SSD_HEADS, q), lambda i: (0, gi(i))),
        pl.BlockSpec((q, SSD_WIDTH), lambda i: (gi(i), 0)),
        pl.BlockSpec(st_block, lambda i: (gi(i), 0, 0)),
        pl.BlockSpec(st_block, lambda i: (gi(i), 0, 0)),
        pl.BlockSpec(dec_block, lambda i: (gi(i), 0, 0)),
        pl.BlockSpec((1, LANE), const),
        pl.BlockSpec((2 * SSD_HEADS, 1), const),
        pl.BlockSpec((1, SSD_WIDTH), const),
        pl.BlockSpec((1, SSD_WIDTH), const),
    ]
    args = [xcb, dt, dtt, z, hsf, sb, dec, lw["alog_row"], lw["alog_col"], lw["dskip_row"], lw["ssd_norm"]]
    if has_h0:
        in_specs.append(pl.BlockSpec(st_block, lambda i: (gi(i) // cps, 0, 0)))
        args.append(h0[1])
    y, hfin_b = pl.pallas_call(
        functools.partial(_ssd_out_kernel, cps=cps, has_h0=has_h0, nchunks=nchunks),
        grid=(nchunks,),
        in_specs=in_specs,
        out_specs=[
            pl.BlockSpec((q, SSD_WIDTH), lambda i: (gi(i), 0)),
            pl.BlockSpec(st_block, lambda i: (gi(i) // cps, 0, 0)),
        ],
        out_shape=[
            jax.ShapeDtypeStruct((t, SSD_WIDTH), BF16),
            jax.ShapeDtypeStruct((nseq, SSD_STATE, SSD_WIDTH), F32),
        ],
        scratch_shapes=[pltpu.VMEM((SSD_STATE, SSD_WIDTH), F32)],
        compiler_params=_cparams("arbitrary"),
        name="ssd_out",
    )(*args)
    return y, hfin_f, hfin_b


def _state_to_kernel_layout(h):
    n = h.shape[0]
    return h.transpose(0, 3, 1, 2).reshape(n, SSD_STATE, SSD_WIDTH)


def _state_from_kernel_layout(h):
    n = h.shape[0]
    return h.reshape(n, SSD_STATE, SSD_HEADS, SSD_HEAD_DIM).transpose(0, 2, 3, 1)


def _attn_kernel(*refs, heads, has_cache):
    if has_cache:
        qt_ref, k_ref, vt_ref, kc_ref, vct_ref = refs[:5]
    else:
        qt_ref, k_ref, vt_ref = refs[:3]
    s_ref = refs[-1]
    o_ref = refs[-2]
    lk = k_ref.shape[0]
    kb = min(ATTN_KEY_BLOCK, lk)
    blocks = [(k_ref, vt_ref, i * kb) for i in range(lk // kb)]
    if has_cache:
        lc = kc_ref.shape[0]
        kbc = min(ATTN_KEY_BLOCK, lc)
        blocks += [(kc_ref, vct_ref, i * kbc) for i in range(lc // kbc)]
        assert kbc == kb
    nblk = len(blocks)
    ones = jnp.ones((ONES_ROWS, kb), BF16)

    def score_block(hd, i, m):
        kr, _, off = blocks[i]
        q = qt_ref[hd * HEAD_PAD:(hd + 1) * HEAD_PAD, :]
        s = _dot(kr[off:off + kb, hd * HEAD_PAD:(hd + 1) * HEAD_PAD], q)
        s_ref[hd % 2, i * kb:(i + 1) * kb, :] = s
        bm = jnp.max(s, axis=0, keepdims=True)
        return bm if m is None else jnp.maximum(m, bm)

    def value_block(hd, i, m, acc):
        _, vr, off = blocks[i]
        p = jnp.exp2(s_ref[hd % 2, i * kb:(i + 1) * kb, :] - m).astype(BF16)
        v = vr[hd * MLA_V_DIM:(hd + 1) * MLA_V_DIM, off:off + kb]
        part = _dot(jnp.concatenate([v, ones], axis=0), p)
        return part if acc is None else acc + part

    m_cur = None
    for i in range(nblk):
        m_cur = score_block(0, i, m_cur)
    for hd in range(heads):
        m_next, acc = None, None
        for i in range(nblk):
            if hd + 1 < heads:
                m_next = score_block(hd + 1, i, m_next)
            acc = value_block(hd, i, m_cur, acc)
        vs = slice(hd * MLA_V_DIM, (hd + 1) * MLA_V_DIM)
        o_ref[vs, :] = acc[:MLA_V_DIM] / acc[MLA_V_DIM:MLA_V_DIM + 1]
        m_cur = m_next


def _attention(qt, k, vt, seq_len, heads_per_step, cache=None):
    t = k.shape[0]
    nseq = t // seq_len
    tq = min(ATTN_Q_TILE, seq_len)
    nq = seq_len // tq
    g = heads_per_step
    in_specs = [
        pl.BlockSpec((g * HEAD_PAD, tq), lambda s, h, j: (h, s * nq + j)),
        pl.BlockSpec((seq_len, g * HEAD_PAD), lambda s, h, j: (s, h)),
        pl.BlockSpec((g * MLA_V_DIM, seq_len), lambda s, h, j: (h, s)),
    ]
    args = [qt, k, vt]
    n_keys = seq_len
    if cache is not None:
        kc, vct = cache
        past = kc.shape[0] // nseq
        n_keys += past
        in_specs += [
            pl.BlockSpec((past, g * HEAD_PAD), lambda s, h, j: (s, h)),
            pl.BlockSpec((g * MLA_V_DIM, past), lambda s, h, j: (h, s)),
        ]
        args += [kc, vct]
    kern = functools.partial(_attn_kernel, heads=g, has_cache=cache is not None)
    return pl.pallas_call(
        kern,
        grid=(nseq, MLA_HEADS // g, nq),
        in_specs=in_specs,
        out_specs=pl.BlockSpec((g * MLA_V_DIM, tq), lambda s, h, j: (h, s * nq + j)),
        out_shape=jax.ShapeDtypeStruct((MLA_WIDTH, t), F32),
        scratch_shapes=[pltpu.VMEM((2, n_keys, tq), F32)],
        compiler_params=_cparams("arbitrary", "arbitrary", "arbitrary"),
        name="attention",
    )(*args)


def _outproj_kernel(*refs, has_router):
    if has_router:
        (y_ref, ot_ref, x_ref, mod_ref, wa_ref, wb_ref, gm_ref, gpost_ref, gpre2_ref, rt_ref,
         x1_ref, h2_ref, comb_ref) = refs
    else:
        (y_ref, ot_ref, x_ref, mod_ref, wa_ref, wb_ref, gm_ref, gpost_ref, gpre2_ref,
         x1_ref, h2_ref) = refs
    mod = mod_ref[0]
    gate1 = mod[:, 2 * D_MODEL:3 * D_MODEL]
    shift2 = mod[:, 3 * D_MODEL:4 * D_MODEL]
    scale2 = mod[:, 4 * D_MODEL:5 * D_MODEL]
    ot = ot_ref[...]
    ms = jnp.mean(ot * ot, axis=0, keepdims=True)
    on = (ot * lax.rsqrt(ms + EPS) * gm_ref[...]).astype(BF16)
    y = _dot(y_ref[...], wa_ref[...]) + _dot_tn(on, wb_ref[...])
    x1 = x_ref[...] + gate1 * _rms_rows(y, gpost_ref[...])
    x1_ref[...] = x1
    h2 = _rms_rows(x1, gpre2_ref[...]) * (1.0 + scale2) + shift2
    h2_ref[...] = h2.astype(BF16)
    if has_router:
        hh, hm, _ = _split3(h2)
        rh, rm, _ = _split3(rt_ref[...])
        logits = _dot(hh, rh) + (_dot(hm, rh) + _dot(hh, rm))
        lane = lax.broadcasted_iota(jnp.int32, logits.shape, 1).astype(F32)
        neg = jnp.float32(-jnp.inf)
        lg = jnp.where(lane < N_EXPERTS, logits, neg)
        m1 = jnp.max(lg, axis=-1, keepdims=True)
        i1 = jnp.min(jnp.where(lg == m1, lane, float(LANE)), axis=-1, keepdims=True)
        lg2 = jnp.where(lane == i1, neg, lg)
        m2 = jnp.max(lg2, axis=-1, keepdims=True)
        i2 = jnp.min(jnp.where(lg2 == m2, lane, float(LANE)), axis=-1, keepdims=True)
        e2 = jnp.exp(m2 - m1)
        w1 = 1.0 / (1.0 + e2)
        w2 = e2 / (1.0 + e2)
        comb_ref[...] = jnp.where(lane == i1, w1, 0.0) + jnp.where(lane == i2, w2, 0.0)


def _outproj(yssd, ot, x, mod3, mod_row_fn, lw, router=None):
    t = x.shape[0]
    tm = ROW_TILE
    const = lambda i: (0, 0)
    row = lambda i: (i, 0)
    in_specs = [
        pl.BlockSpec((tm, SSD_WIDTH), row),
        pl.BlockSpec((MLA_WIDTH, tm), lambda i: (0, i)),
        pl.BlockSpec((tm, D_MODEL), row),
        pl.BlockSpec((1, 1, N_MOD * D_MODEL), lambda i: (mod_row_fn(ROW_TILE)(i), 0, 0)),
        pl.BlockSpec((SSD_WIDTH, D_MODEL), const),
        pl.BlockSpec((MLA_WIDTH, D_MODEL), const),
        pl.BlockSpec((MLA_WIDTH, 1), const),
        pl.BlockSpec((1, D_MODEL), const),
        pl.BlockSpec((1, D_MODEL), const),
    ]
    args = [yssd, ot, x, mod3, lw["w_out_a"], lw["w_out_b"], lw["mla_norm_col"],
            lw["g_post1"], lw["g_pre2"]]
    out_specs = [pl.BlockSpec((tm, D_MODEL), row), pl.BlockSpec((tm, D_MODEL), row)]
    out_shape = [jax.ShapeDtypeStruct((t, D_MODEL), F32), jax.ShapeDtypeStruct((t, D_MODEL), BF16)]
    if router is not None:
        in_specs.append(pl.BlockSpec((D_MODEL, LANE), const))
        args.append(router)
        out_specs.append(pl.BlockSpec((tm, LANE), row))
        out_shape.append(jax.ShapeDtypeStruct((t, LANE), F32))
    return pl.pallas_call(
        functools.partial(_outproj_kernel, has_router=router is not None),
        grid=(t // tm,),
        in_specs=in_specs,
        out_specs=out_specs,
        out_shape=out_shape,
        compiler_params=_cparams("arbitrary"),
        name="outproj",
    )(*args)


def _ffn_kernel(*refs, has_comb, nslab):
    if has_comb:
        h_ref, x_ref, comb_ref, mod_ref, wg_ref, wu_ref, wd_ref, gpost_ref, o_ref, acc_ref = refs
    else:
        h_ref, x_ref, mod_ref, wg_ref, wu_ref, wd_ref, gpost_ref, o_ref, acc_ref = refs
    e = pl.program_id(1)
    h = h_ref[...]
    hid = _silu(_dot(h, wg_ref[0])) * _dot(h, wu_ref[0])
    if has_comb:
        comb = comb_ref[...]
        lane = lax.broadcasted_iota(jnp.int32, comb.shape, 1)
        wcol = jnp.sum(jnp.where(lane == e, comb, 0.0), axis=-1, keepdims=True)
        hid = hid * wcol
    part = _dot(hid.astype(BF16), wd_ref[0])

    @pl.when(e == 0)
    def _():
        acc_ref[...] = part

    @pl.when(e > 0)
    def _():
        acc_ref[...] += part

    @pl.when(e == nslab - 1)
    def _():
        gate2 = mod_ref[0][:, 5 * D_MODEL:6 * D_MODEL]
        o_ref[...] = x_ref[...] + gate2 * _rms_rows(acc_ref[...], gpost_ref[...])


def _ffn(h2, x1, mod3, mod_row_fn, wg, wu, wd, gpost, comb=None):
    t = x1.shape[0]
    tm = FFN_ROW_TILE
    nslab, _, f = wg.shape
    row = lambda i, e: (i, 0)
    in_specs = [pl.BlockSpec((tm, D_MODEL), row), pl.BlockSpec((tm, D_MODEL), row)]
    args = [h2, x1]
    if comb is not None:
        in_specs.append(pl.BlockSpec((tm, LANE), row))
        args.append(comb)
    in_specs += [
        pl.BlockSpec((1, 1, N_MOD * D_MODEL), lambda i, e: (mod_row_fn(FFN_ROW_TILE)(i), 0, 0)),
        pl.BlockSpec((1, D_MODEL, f), lambda i, e: (e, 0, 0)),
        pl.BlockSpec((1, D_MODEL, f), lambda i, e: (e, 0, 0)),
        pl.BlockSpec((1, f, D_MODEL), lambda i, e: (e, 0, 0)),
        pl.BlockSpec((1, D_MODEL), lambda i, e: (0, 0)),
    ]
    args += [mod3, wg, wu, wd, gpost]
    return pl.pallas_call(
        functools.partial(_ffn_kernel, has_comb=comb is not None, nslab=nslab),
        grid=(t // tm, nslab),
        in_specs=in_specs,
        out_specs=pl.BlockSpec((tm, D_MODEL), row),
        out_shape=jax.ShapeDtypeStruct((t, D_MODEL), F32),
        scratch_shapes=[pltpu.VMEM((tm, D_MODEL), F32)],
        compiler_params=_cparams("arbitrary", "arbitrary"),
        name="ffn",
    )(*args)


def _moe_kernel(h_ref, x_ref, comb_ref, mod_ref, wg_ref, wu_ref, wd_ref, gpost_ref, o_ref,
                acc_ref, rank_ref, rank_t_ref, comb_t_ref, *, nexp, sub):
    e = pl.program_id(1)
    tm = h_ref.shape[0]
    cap = MOE_CAP
    capl = -(-cap // LANE) * LANE

    @pl.when(e == 0)
    def _():
        acc_ref[...] = jnp.zeros_like(acc_ref)
        r_i = lax.broadcasted_iota(jnp.int32, (sub, sub), 0)
        c_i = lax.broadcasted_iota(jnp.int32, (sub, sub), 1)
        strict = jnp.where(r_i > c_i, 1.0, 0.0).astype(BF16)
        for s in range(tm // sub):
            rows = slice(s * sub, (s + 1) * sub)
            comb = comb_ref[rows, :]
            rank = _dot(strict, jnp.where(comb > 0.0, 1.0, 0.0).astype(BF16))
            rank_ref[rows, :] = rank
            rank_t_ref[:, rows] = rank.T
            comb_t_ref[:, rows] = comb.T

    lane = lax.broadcasted_iota(jnp.int32, (sub, LANE), 1)
    slot_l = lax.broadcasted_iota(jnp.int32, (sub, capl), 1).astype(F32)
    slot_s = lax.broadcasted_iota(jnp.int32, (cap, sub), 0).astype(F32)
    for s in range(tm // sub):
        rows = slice(s * sub, (s + 1) * sub)
        pick = lane == e
        wcol = jnp.sum(jnp.where(pick, comb_ref[rows, :], 0.0), axis=-1, keepdims=True)
        rcol = jnp.sum(jnp.where(pick, rank_ref[rows, :], 0.0), axis=-1, keepdims=True)
        wrow = comb_t_ref[pl.ds(e, 1), rows]
        rrow = rank_t_ref[pl.ds(e, 1), rows]
        count = jnp.max(jnp.where(wrow > 0.0, rrow + 1.0, 0.0))
        nchunk = ((count + (cap - 1.0)) * (1.0 / cap)).astype(jnp.int32)
        hsub = h_ref[rows, :]

        def chunk(c, carry, rows=rows, wcol=wcol, rcol=rcol, wrow=wrow, rrow=rrow, hsub=hsub):
            base = (c * cap).astype(F32)
            gather = jnp.where(((rrow - base) == slot_s) & (wrow > 0.0), 1.0, 0.0).astype(BF16)
            scatter = jnp.where(((rcol - base) == slot_l) & (wcol > 0.0) & (slot_l < float(cap)),
                                1.0, 0.0).astype(BF16)
            xg = _dot(gather, hsub).astype(BF16)
            hid = _silu(_dot(xg, wg_ref[0])) * _dot(xg, wu_ref[0])
            y = _dot(hid.astype(BF16), wd_ref[0]).astype(BF16)
            if capl > cap:
                y = jnp.concatenate([y, jnp.zeros((capl - cap, y.shape[1]), BF16)], axis=0)
            acc_ref[rows, :] += wcol * _dot(scatter, y)
            return carry

        lax.fori_loop(0, nchunk, chunk, 0)

    @pl.when(e == nexp - 1)
    def _():
        gate2 = mod_ref[0][:, 5 * D_MODEL:6 * D_MODEL]
        o_ref[...] = x_ref[...] + gate2 * _rms_rows(acc_ref[...], gpost_ref[...])


def _moe(h2, x1, comb, mod3, mod_row_fn, wg, wu, wd, gpost):
    t = x1.shape[0]
    tm = min(MOE_ROW_TILE, t)
    sub = min(MOE_SUB_TILE, tm)
    nexp, _, f = wg.shape
    row = lambda i, e: (i, 0)
    return pl.pallas_call(
        functools.partial(_moe_kernel, nexp=nexp, sub=sub),
        grid=(t // tm, nexp),
        in_specs=[
            pl.BlockSpec((tm, D_MODEL), row),
            pl.BlockSpec((tm, D_MODEL), row),
            pl.BlockSpec((tm, LANE), row),
            pl.BlockSpec((1, 1, N_MOD * D_MODEL), lambda i, e: (mod_row_fn(tm)(i), 0, 0)),
            pl.BlockSpec((1, D_MODEL, f), lambda i, e: (e, 0, 0)),
            pl.BlockSpec((1, D_MODEL, f), lambda i, e: (e, 0, 0)),
            pl.BlockSpec((1, f, D_MODEL), lambda i, e: (e, 0, 0)),
            pl.BlockSpec((1, D_MODEL), lambda i, e: (0, 0)),
        ],
        out_specs=pl.BlockSpec((tm, D_MODEL), row),
        out_shape=jax.ShapeDtypeStruct((t, D_MODEL), F32),
        scratch_shapes=[pltpu.VMEM((tm, D_MODEL), F32), pltpu.VMEM((tm, LANE), F32),
                        pltpu.VMEM((LANE, tm), F32), pltpu.VMEM((LANE, tm), F32)],
        compiler_params=_cparams("arbitrary", "arbitrary"),
        name="moe",
    )(h2, x1, comb, mod3, wg, wu, wd, gpost)


_ROT_PERM = tuple(list(range(8, 16)) + list(range(0, 8)) + list(range(24, 32)) + list(range(16, 24)))
_ROT_SIGN = tuple([-1.0] * 8 + [1.0] * 8 + [-1.0] * 8 + [1.0] * 8)


def _rot_cols(w):
    return w[..., jnp.array(_ROT_PERM)] * jnp.array(_ROT_SIGN, F32)


def _prep_layer(i, p):
    w_in = p["w_in"][i]
    s1 = SSD_WIDTH
    s2 = s1 + SSD_XBC
    s3 = s2 + 2 * SSD_HEADS
    s4 = s3 + MLA_Q_RANK
    s5 = s4 + MLA_KV_RANK
    w_z, w_xbc, w_dt, w_cq, w_ckv, w_kr = (w_in[:, :s1], w_in[:, s1:s2], w_in[:, s2:s3],
                                             w_in[:, s3:s4], w_in[:, s4:s5], w_in[:, s5:])
    zc = lambda n: jnp.zeros((D_MODEL, n), F32)
    tile_a = jnp.concatenate([w_dt, zc(ROPE_LANE0 - 2 * SSD_HEADS), w_kr,
                              zc(LANE - ROPE_LANE0 - MLA_ROPE_DIM)], axis=1)
    tile_b = jnp.concatenate([zc(ROPE_LANE0), _rot_cols(w_kr),
                              zc(LANE - ROPE_LANE0 - MLA_ROPE_DIM)], axis=1)
    w_in_pad = jnp.concatenate([w_z, w_xbc, w_cq, w_ckv, tile_a, tile_b], axis=1).astype(BF16)

    w_uq = p["w_uq"][i].reshape(MLA_Q_RANK, MLA_HEADS, MLA_NOPE_DIM + MLA_ROPE_DIM)
    q_nope, q_rope = w_uq[..., :MLA_NOPE_DIM], w_uq[..., MLA_NOPE_DIM:]
    zq = lambda n: jnp.zeros((MLA_Q_RANK, MLA_HEADS, n), F32)
    pad = HEAD_PAD - MLA_NOPE_DIM - MLA_ROPE_DIM
    w_q = jnp.concatenate([q_nope, q_rope, zq(pad)], axis=-1).reshape(MLA_Q_RANK, -1)
    w_qrot = jnp.concatenate([zq(MLA_NOPE_DIM), _rot_cols(q_rope), zq(pad)], axis=-1).reshape(MLA_Q_RANK, -1)

    w_ukv = p["w_ukv"][i].reshape(MLA_KV_RANK, MLA_HEADS, MLA_NOPE_DIM + MLA_V_DIM)
    k_nope, v_w = w_ukv[..., :MLA_NOPE_DIM], w_ukv[..., MLA_NOPE_DIM:]
    w_uk = jnp.concatenate([k_nope, jnp.zeros((MLA_KV_RANK, MLA_HEADS, HEAD_PAD - MLA_NOPE_DIM), F32)],
                           axis=-1).reshape(MLA_KV_RANK, -1)
    w_uv = v_w.reshape(MLA_KV_RANK, MLA_WIDTH)

    dtb = p["dt_bias"][i].reshape(2 * SSD_HEADS)
    alog = p["a_log"][i].reshape(2 * SSD_HEADS)
    padl = lambda v: jnp.pad(v, (0, LANE - v.shape[0])).reshape(1, LANE)
    w_out = p["w_out"][i]
    return {
        "g_pre1": p["norm_pre_mix"][i].reshape(1, D_MODEL),
        "g_post1": p["norm_post_mix"][i].reshape(1, D_MODEL),
        "g_pre2": p["norm_pre_ffn"][i].reshape(1, D_MODEL),
        "g_post2": p["norm_post_ffn"][i].reshape(1, D_MODEL),
        "w_in": w_in_pad,
        "w_dt": w_dt.T.astype(BF16),
        "dtb_row": padl(dtb),
        "dtb_col": dtb.reshape(-1, 1),
        "alog_row": padl(alog),
        "alog_col": alog.reshape(-1, 1),
        "q_norm": p["q_norm"][i].reshape(1, -1),
        "w_q": w_q.T.astype(BF16),
        "w_qrot": w_qrot.T.astype(BF16),
        "kv_norm": p["kv_norm"][i].reshape(1, -1),
        "w_uk": w_uk.astype(BF16),
        "w_uv": w_uv.T.astype(BF16),
        "conv_w": p["conv_w"][i],
        "conv_b": p["conv_b"][i].reshape(1, -1),
        "dskip_row": jnp.repeat(p["d_skip"][i], SSD_HEAD_DIM).reshape(1, -1),
        "ssd_norm": p["ssd_norm"][i].reshape(1, -1),
        "mla_norm_col": p["mla_norm"][i].reshape(-1, 1),
        "w_out_a": w_out[:SSD_WIDTH].astype(BF16),
        "w_out_b": w_out[SSD_WIDTH:].astype(BF16),
    }


def _rope_tables(n_tokens):
    rows = n_tokens // GRID_W
    row = jnp.repeat(jnp.arange(rows, dtype=F32), GRID_W)
    col = jnp.tile(jnp.arange(GRID_W, dtype=F32), rows)
    half = MLA_ROPE_DIM // 2
    inv = ROPE_THETA ** (-jnp.arange(0, half, 2, dtype=F32) / half)
    ar = row[:, None] * inv[None, :]
    ac = col[:, None] * inv[None, :]
    ang = jnp.concatenate([ar, ar, ac, ac], axis=-1)
    return jnp.cos(ang), jnp.sin(ang)


def _attn_tables(cos, sin, n):
    scale = (MLA_NOPE_DIM + MLA_ROPE_DIM) ** -0.5 * math.log2(math.e)
    pad = HEAD_PAD - ROPE_LANE0 - MLA_ROPE_DIM
    cosk = jnp.concatenate([jnp.zeros((n, ROPE_LANE0), F32), cos, jnp.zeros((n, pad), F32)], axis=1)
    sink = jnp.concatenate([jnp.zeros((n, ROPE_LANE0), F32), sin, jnp.zeros((n, pad), F32)], axis=1)
    cosq = jnp.concatenate([jnp.ones((n, ROPE_LANE0), F32), cos, jnp.zeros((n, pad), F32)], axis=1)
    return (cosq * scale).T, (sink * scale).T, cosk, sink


def kernel(x_prompt, x_sample, cache_ckv, cache_krope, state_ssm, c, c_ctx, w_mod, b_mod, norm_pre_mix, norm_post_mix, norm_pre_ffn, norm_post_ffn, w_in, conv_w, conv_b, dt_bias, a_log, d_skip, ssd_norm, q_norm, w_uq, kv_norm, w_ukv, mla_norm, w_out, ffn_w_gate, ffn_w_up, ffn_w_down, moe_router, moe_w_gate, moe_w_up, moe_w_down):
    params = dict(w_in=w_in, conv_w=conv_w, conv_b=conv_b, dt_bias=dt_bias, a_log=a_log, d_skip=d_skip,
                  ssd_norm=ssd_norm, q_norm=q_norm, w_uq=w_uq, kv_norm=kv_norm, w_ukv=w_ukv,
                  mla_norm=mla_norm, w_out=w_out, norm_pre_mix=norm_pre_mix, norm_post_mix=norm_post_mix,
                  norm_pre_ffn=norm_pre_ffn, norm_post_ffn=norm_post_ffn)
    batch, seq, d = x_prompt.shape
    dec_batch, dec_seq, _ = x_sample.shape
    depth = w_in.shape[0]
    past = cache_ckv.shape[2]
    tm = ROW_TILE

    cvec = jnp.concatenate([c_ctx[None, :], c, jnp.zeros((SUBLANE - 1 - dec_batch, d), F32)], axis=0)
    mod = _modulation(cvec, w_mod, b_mod)

    ones = jnp.ones((tm, MLA_ROPE_DIM), F32)
    tabs_ctx = _attn_tables(ones, jnp.zeros_like(ones), tm)
    cos, sin = _rope_tables(dec_seq)
    tabs_lat = _attn_tables(cos, sin, dec_seq)
    lat_blocks = dec_seq // tm

    xp = x_prompt.reshape(batch * seq, d)
    xs = x_sample.reshape(dec_batch * dec_seq, d)
    new_ckv, new_kr, new_ssm = [], [], []
    for i in range(depth):
        lw = _prep_layer(i, params)
        mod3 = mod[i].reshape(SUBLANE, 1, N_MOD * d)
        j = i // 2
        if i % 2 == 0:
            f = ffn_w_gate.shape[2] // 2
            split = lambda w: jnp.stack([w[:, :f], w[:, f:]], axis=0).astype(BF16)
            wg, wu = split(ffn_w_gate[j]), split(ffn_w_up[j])
            wd = ffn_w_down[j].reshape(2, f, d).astype(BF16)
            router = None
        else:
            wg, wu, wd = moe_w_gate[j].astype(BF16), moe_w_up[j].astype(BF16), moe_w_down[j].astype(BF16)
            router = jnp.pad(moe_router[j], ((0, 0), (0, LANE - N_EXPERTS)))

        def run(x, row_fn, tabs, tab_blocks, seq_len, heads_per_step, ctx):
            z, xbc, dt, dtt, qt, k, vt, ckvn, kr = _inproj(x, mod3, row_fn, lw, tabs, tab_blocks)
            cps = seq_len // SSD_CHUNK
            h0 = None
            if ctx is not None:
                h0 = (_state_to_kernel_layout(ctx[2][:, 0]), _state_to_kernel_layout(ctx[2][:, 1]))
            yssd, hf, hb = _ssd(xbc, dt, dtt, z, lw, cps, h0=h0)
            hf, hb = _state_from_kernel_layout(hf), _state_from_kernel_layout(hb)
            cache = None
            if ctx is not None:
                kr_tile = jnp.pad(ctx[1].reshape(-1, MLA_ROPE_DIM),
                                  ((0, 0), (ROPE_LANE0, HEAD_PAD - ROPE_LANE0 - MLA_ROPE_DIM)))
                cache = _kvcache(ctx[0].reshape(-1, MLA_KV_RANK), kr_tile, lw)
            ot = _attention(qt, k, vt, seq_len, heads_per_step, cache=cache)
            outs = _outproj(yssd, ot, x, mod3, row_fn, lw, router=router)
            x1, h2 = outs[0], outs[1]
            comb = outs[2] if router is not None else None
            if comb is None:
                x2 = _ffn(h2, x1, mod3, row_fn, wg, wu, wd, lw["g_post2"])
            else:
                x2 = _moe(h2, x1, comb, mod3, row_fn, wg, wu, wd, lw["g_post2"])
            return x2, ckvn, kr, hf, hb

        xp, ckvn, kr, hf, hb = run(xp, lambda tile: (lambda b: 0), tabs_ctx, 1, seq, MLA_HEADS, None)
        new_ckv.append(ckvn.reshape(batch, seq, MLA_KV_RANK))
        new_kr.append(kr.reshape(batch, seq, MLA_ROPE_DIM))
        new_ssm.append(jnp.stack([hf, hb], axis=1))
        xs, _, _, _, _ = run(xs, lambda tile: (lambda b: 1 + (b * tile) // dec_seq), tabs_lat, lat_blocks, dec_seq, MLA_HEADS,
                             (cache_ckv[:, i], cache_krope[:, i], state_ssm[:, i]))
    return (xp.reshape(batch, seq, d), xs.reshape(dec_batch, dec_seq, d),
            jnp.stack(new_ckv, axis=1), jnp.stack(new_kr, axis=1), jnp.stack(new_ssm, axis=1))
```

```python
import functools
import math

import jax
import jax.numpy as jnp
import numpy as np
from jax import lax
from jax.experimental import pallas as pl
from jax.experimental.pallas import tpu as pltpu

F32 = jnp.float32
BF16 = jnp.bfloat16

D_MODEL = 1024
GRID_W = 64
SSD_WIDTH = 512
SSD_HEAD_DIM = 64
SSD_HEADS = 8
SSD_GROUPS = 2
SSD_STATE = 64
SSD_CONV = 5
SSD_CHUNK = 128
SSD_XBC = SSD_WIDTH + 2 * SSD_GROUPS * SSD_STATE
MLA_WIDTH = 512
MLA_V_DIM = 64
MLA_HEADS = 8
MLA_NOPE_DIM = 64
MLA_ROPE_DIM = 32
MLA_Q_RANK = 384
MLA_KV_RANK = 256
ROPE_THETA = 10000.0
N_EXPERTS = 8
N_MOD = 6
EPS = 1e-6

LANE = 128
SUBLANE = 8
HEAD_PAD = 128
ONES_ROWS = 16
ROPE_LANE0 = MLA_NOPE_DIM
C_Z = 0
C_XBC = C_Z + SSD_WIDTH
C_CQ = C_XBC + SSD_XBC
C_CKV = C_CQ + MLA_Q_RANK
C_TA = C_CKV + MLA_KV_RANK
C_TB = C_TA + LANE
IN_PAD = C_TB + LANE

VMEM_LIMIT = 56 * 1024 * 1024

ROW_TILE = 512
ATTN_Q_TILE = 256
ATTN_KEY_BLOCK = 512
FFN_ROW_TILE = 512
MOE_ROW_TILE = 1024
MOE_SUB_TILE = 512
MOE_CAP = 160

NT_DIMS = (((1,), (1,)), ((), ()))
TN_DIMS = (((0,), (0,)), ((), ()))


def _cparams(*sem):
    return pltpu.CompilerParams(dimension_semantics=sem, vmem_limit_bytes=VMEM_LIMIT)


def _silu(x):
    return x / (1.0 + jnp.exp(-x))


def _softplus(x):
    return jnp.maximum(x, 0.0) + jnp.log(1.0 + jnp.exp(-jnp.abs(x)))


def _rms_rows(x, g):
    ms = jnp.mean(x * x, axis=-1, keepdims=True)
    return x * lax.rsqrt(ms + EPS) * g


def _dot(a, b):
    return jnp.dot(a, b, preferred_element_type=F32)


def _dot_nt(a, b):
    return lax.dot_general(a, b, NT_DIMS, preferred_element_type=F32)


def _dot_tn(a, b):
    return lax.dot_general(a, b, TN_DIMS, preferred_element_type=F32)


def _split3(x):
    hi = x.astype(BF16)
    r1 = x - hi.astype(F32)
    mid = r1.astype(BF16)
    lo = (r1 - mid.astype(F32)).astype(BF16)
    return hi, mid, lo


def _mod_kernel(c_ref, w_ref, b_ref, o_ref):
    s = _silu(c_ref[...]).astype(BF16)
    o_ref[0] = _dot(s, w_ref[0].astype(BF16)) + b_ref[0]


def _modulation(cvec, w_mod, b_mod):
    depth, d, n = w_mod.shape
    tn = 1536
    return pl.pallas_call(
        _mod_kernel,
        grid=(depth, n // tn),
        in_specs=[
            pl.BlockSpec((SUBLANE, d), lambda l, j: (0, 0)),
            pl.BlockSpec((1, d, tn), lambda l, j: (l, 0, j)),
            pl.BlockSpec((1, 1, tn), lambda l, j: (l, 0, j)),
        ],
        out_specs=pl.BlockSpec((1, SUBLANE, tn), lambda l, j: (l, 0, j)),
        out_shape=jax.ShapeDtypeStruct((depth, SUBLANE, n), F32),
        compiler_params=_cparams("arbitrary", "arbitrary"),
        name="modulation",
    )(cvec, w_mod, b_mod.reshape(depth, 1, n))


def _inproj_kernel(x_ref, mod_ref, gpre_ref, win_ref, wdt_ref, dtb_row_ref, dtb_col_ref,
                   qn_ref, wq_ref, wqrot_ref, kvn_ref, wuk_ref, wuv_ref,
                   cosq_ref, sinq_ref, cosk_ref, sink_ref,
                   z_ref, xbc_ref, dt_ref, dtt_ref, qt_ref, k_ref, vt_ref, ckvn_ref, kr_ref):
    mod = mod_ref[0]
    shift = mod[:, 0:D_MODEL]
    scale = mod[:, D_MODEL:2 * D_MODEL]
    h = _rms_rows(x_ref[...], gpre_ref[...]) * (1.0 + scale) + shift
    hb = h.astype(BF16)
    proj = _dot(hb, win_ref[...])
    z_ref[...] = proj[:, C_Z:C_XBC]
    xbc_ref[...] = proj[:, C_XBC:C_CQ]
    cqn = _rms_rows(proj[:, C_CQ:C_CKV], qn_ref[...]).astype(BF16)
    ckvn = _rms_rows(proj[:, C_CKV:C_TA], kvn_ref[...])
    ckvn_ref[...] = ckvn
    ckvb = ckvn.astype(BF16)
    ta = proj[:, C_TA:C_TB]
    tb = proj[:, C_TB:IN_PAD]
    dt_ref[...] = _softplus(ta + dtb_row_ref[...])
    kr_ref[...] = ta[:, ROPE_LANE0:ROPE_LANE0 + MLA_ROPE_DIM]
    kr_rot = ta * cosk_ref[...] + tb * sink_ref[...]
    knp = _dot(ckvb, wuk_ref[...])
    for hd in range(MLA_HEADS):
        sl = slice(hd * HEAD_PAD, (hd + 1) * HEAD_PAD)
        k_ref[:, sl] = (knp[:, sl] + kr_rot).astype(BF16)
    vt_ref[...] = _dot_nt(wuv_ref[...], ckvb).astype(BF16)
    qt = _dot_nt(wq_ref[...], cqn)
    qrt = _dot_nt(wqrot_ref[...], cqn)
    cosq = cosq_ref[...]
    sinq = sinq_ref[...]
    for hd in range(MLA_HEADS):
        sl = slice(hd * HEAD_PAD, (hd + 1) * HEAD_PAD)
        qt_ref[sl, :] = (qt[sl, :] * cosq + qrt[sl, :] * sinq).astype(BF16)
    dtt_ref[...] = _softplus(_dot_nt(wdt_ref[...], hb) + dtb_col_ref[...])


def _inproj(x, mod3, mod_row_fn, lw, tabs, tab_blocks):
    t = x.shape[0]
    tm = ROW_TILE
    nb = t // tm
    cosq, sinq, cosk, sink = tabs
    ntab = tab_blocks
    const = lambda i: (0, 0)
    row = lambda i: (i, 0)
    col = lambda i: (0, i)
    in_specs = [
        pl.BlockSpec((tm, D_MODEL), row),
        pl.BlockSpec((1, 1, N_MOD * D_MODEL), lambda i: (mod_row_fn(ROW_TILE)(i), 0, 0)),
        pl.BlockSpec((1, D_MODEL), const),
        pl.BlockSpec((D_MODEL, IN_PAD), const),
        pl.BlockSpec((2 * SSD_HEADS, D_MODEL), const),
        pl.BlockSpec((1, LANE), const),
        pl.BlockSpec((2 * SSD_HEADS, 1), const),
        pl.BlockSpec((1, MLA_Q_RANK), const),
        pl.BlockSpec((MLA_HEADS * HEAD_PAD, MLA_Q_RANK), const),
        pl.BlockSpec((MLA_HEADS * HEAD_PAD, MLA_Q_RANK), const),
        pl.BlockSpec((1, MLA_KV_RANK), const),
        pl.BlockSpec((MLA_KV_RANK, MLA_HEADS * HEAD_PAD), const),
        pl.BlockSpec((MLA_WIDTH, MLA_KV_RANK), const),
        pl.BlockSpec((HEAD_PAD, tm), lambda i: (0, i % ntab)),
        pl.BlockSpec((HEAD_PAD, tm), lambda i: (0, i % ntab)),
        pl.BlockSpec((tm, LANE), lambda i: (i % ntab, 0)),
        pl.BlockSpec((tm, LANE), lambda i: (i % ntab, 0)),
    ]
    out_specs = [
        pl.BlockSpec((tm, SSD_WIDTH), row),
        pl.BlockSpec((tm, SSD_XBC), row),
        pl.BlockSpec((tm, LANE), row),
        pl.BlockSpec((2 * SSD_HEADS, tm), col),
        pl.BlockSpec((MLA_HEADS * HEAD_PAD, tm), col),
        pl.BlockSpec((tm, MLA_HEADS * HEAD_PAD), row),
        pl.BlockSpec((MLA_WIDTH, tm), col),
        pl.BlockSpec((tm, MLA_KV_RANK), row),
        pl.BlockSpec((tm, MLA_ROPE_DIM), row),
    ]
    out_shape = [
        jax.ShapeDtypeStruct((t, SSD_WIDTH), F32),
        jax.ShapeDtypeStruct((t, SSD_XBC), F32),
        jax.ShapeDtypeStruct((t, LANE), F32),
        jax.ShapeDtypeStruct((2 * SSD_HEADS, t), F32),
        jax.ShapeDtypeStruct((MLA_HEADS * HEAD_PAD, t), BF16),
        jax.ShapeDtypeStruct((t, MLA_HEADS * HEAD_PAD), BF16),
        jax.ShapeDtypeStruct((MLA_WIDTH, t), BF16),
        jax.ShapeDtypeStruct((t, MLA_KV_RANK), F32),
        jax.ShapeDtypeStruct((t, MLA_ROPE_DIM), F32),
    ]
    return pl.pallas_call(
        _inproj_kernel,
        grid=(nb,),
        in_specs=in_specs,
        out_specs=out_specs,
        out_shape=out_shape,
        compiler_params=_cparams("arbitrary"),
        name="inproj",
    )(x, mod3, lw["g_pre1"], lw["w_in"], lw["w_dt"], lw["dtb_row"], lw["dtb_col"],
      lw["q_norm"], lw["w_q"], lw["w_qrot"], lw["kv_norm"], lw["w_uk"], lw["w_uv"],
      cosq, sinq, cosk, sink)


def _kvcache_kernel(ckv_ref, kr_ref, wuk_ref, wuv_ref, k_ref, vt_ref):
    ckvb = ckv_ref[...].astype(BF16)
    knp = _dot(ckvb, wuk_ref[...])
    kr = kr_ref[...]
    for hd in range(MLA_HEADS):
        sl = slice(hd * HEAD_PAD, (hd + 1) * HEAD_PAD)
        k_ref[:, sl] = (knp[:, sl] + kr).astype(BF16)
    vt_ref[...] = _dot_nt(wuv_ref[...], ckvb).astype(BF16)


def _kvcache(ckv, kr_tile, lw):
    n = ckv.shape[0]
    tm = 512
    return pl.pallas_call(
        _kvcache_kernel,
        grid=(n // tm,),
        in_specs=[
            pl.BlockSpec((tm, MLA_KV_RANK), lambda i: (i, 0)),
            pl.BlockSpec((tm, LANE), lambda i: (i, 0)),
            pl.BlockSpec((MLA_KV_RANK, MLA_HEADS * HEAD_PAD), lambda i: (0, 0)),
            pl.BlockSpec((MLA_WIDTH, MLA_KV_RANK), lambda i: (0, 0)),
        ],
        out_specs=[
            pl.BlockSpec((tm, MLA_HEADS * HEAD_PAD), lambda i: (i, 0)),
            pl.BlockSpec((MLA_WIDTH, tm), lambda i: (0, i)),
        ],
        out_shape=[
            jax.ShapeDtypeStruct((n, MLA_HEADS * HEAD_PAD), BF16),
            jax.ShapeDtypeStruct((MLA_WIDTH, n), BF16),
        ],
        compiler_params=_cparams("arbitrary"),
        name="kvcache",
    )(ckv, kr_tile, lw["w_uk"], lw["w_uv"])


def _head_expand_matrix():
    r = lax.broadcasted_iota(jnp.int32, (LANE, 2 * SSD_WIDTH), 0)
    c = lax.broadcasted_iota(jnp.int32, (LANE, 2 * SSD_WIDTH), 1)
    return jnp.where(c // SSD_HEAD_DIM == r, 1.0, 0.0).astype(BF16)


def _expand_heads(v, emat):
    hi = v.astype(BF16)
    mid = (v - hi.astype(F32)).astype(BF16)
    return _dot(hi, emat) + _dot(mid, emat)


def _prefix_rows(dta, tril):
    return sum(_dot(tril, p) for p in _split3(dta))


def _chunk_masks(q):
    r_i = lax.broadcasted_iota(jnp.int32, (q, q), 0)
    c_i = lax.broadcasted_iota(jnp.int32, (q, q), 1)
    return r_i >= c_i, r_i <= c_i


def _ssd_state_kernel(*refs, cps, has_h0):
    it = iter(refs)
    xbc_ref, prev_ref, next_ref, dt_ref, cw_ref, cb_ref, alog_row_ref = (next(it) for _ in range(7))
    h0_ref = next(it) if has_h0 else None
    xcb_ref, hsf_ref, sb_ref, dec_ref, hfin_ref, st_ref = (next(it) for _ in range(6))

    q = SSD_CHUNK
    g = pl.program_id(0)
    pos = g % cps
    seq_first = pos == 0
    seq_last = pos == cps - 1

    prev = jnp.where(seq_first, 0.0, prev_ref[...])
    nxt = jnp.where(seq_last, 0.0, next_ref[...])
    ext = jnp.concatenate([prev, xbc_ref[...], nxt], axis=0)
    cw = cw_ref[...]
    acc = cb_ref[...] + ext[SUBLANE - 2:SUBLANE - 2 + q] * cw[0:1]
    for k in range(1, SSD_CONV):
        o = SUBLANE - 2 + k
        acc = acc + ext[o:o + q] * cw[k:k + 1]
    xc = _silu(acc)
    xcb = xc.astype(BF16)
    xcb_ref[...] = xcb
    xs = xcb[:, :SSD_WIDTH].astype(F32)

    lower, _ = _chunk_masks(q)
    tril = jnp.where(lower, 1.0, 0.0).astype(BF16)
    dt = dt_ref[...]
    a_row = -jnp.exp(alog_row_ref[...])
    dta = dt * a_row
    la = _prefix_rows(dta, tril)
    tot = la[q - 1:q, :]
    lane = lax.broadcasted_iota(jnp.int32, (q, LANE), 1)
    w = jnp.exp(jnp.where(lane < SSD_HEADS, tot - la, la - dta)) * dt
    w = jnp.where(lane < 2 * SSD_HEADS, w, 0.0)
    emat = _head_expand_matrix()
    wexp = _expand_heads(w, emat)
    lane_t = lax.broadcasted_iota(jnp.int32, (2 * SUBLANE, LANE), 1)
    etot = jnp.where(lane_t < 2 * SSD_HEADS, jnp.exp(jnp.broadcast_to(tot, (2 * SUBLANE, LANE))), 0.0)
    dec = _expand_heads(etot, emat)[:SUBLANE]
    dec_ref[0] = dec

    gw = SSD_WIDTH // SSD_GROUPS
    bmb = xcb[:, SSD_WIDTH:SSD_WIDTH + SSD_GROUPS * SSD_STATE]
    states = []
    for d in range(2):
        xw = (xs * wexp[:, d * SSD_WIDTH:(d + 1) * SSD_WIDTH]).astype(BF16)
        parts = [_dot_tn(bmb[:, grp * SSD_STATE:(grp + 1) * SSD_STATE], xw[:, grp * gw:(grp + 1) * gw])
                 for grp in range(SSD_GROUPS)]
        states.append(jnp.concatenate(parts, axis=1))
    sb_ref[0] = states[1]

    @pl.when(seq_first)
    def _():
        if has_h0:
            st_ref[...] = h0_ref[0]
        else:
            st_ref[...] = jnp.zeros_like(st_ref)

    hs = st_ref[...]
    hsf_ref[0] = hs.astype(BF16)
    hnew = hs * dec[0:1, :SSD_WIDTH] + states[0]
    st_ref[...] = hnew

    @pl.when(seq_last)
    def _():
        hfin_ref[0] = hnew.T


def _ssd_out_kernel(*refs, cps, has_h0, nchunks):
    it = iter(refs)
    (xcb_ref, dt_ref, dtt_ref, z_ref, hsf_ref, sb_ref, dec_ref,
     alog_row_ref, alog_col_ref, dskip_ref, gn_ref) = (next(it) for _ in range(11))
    h0_ref = next(it) if has_h0 else None
    y_ref, hfin_ref, st_ref = next(it), next(it), next(it)

    q = SSD_CHUNK
    g = nchunks - 1 - pl.program_id(0)
    pos = g % cps
    seq_last = pos == cps - 1
    log2e = math.log2(math.e)

    xcb = xcb_ref[...]
    xsb = xcb[:, :SSD_WIDTH]
    bmb = xcb[:, SSD_WIDTH:SSD_WIDTH + SSD_GROUPS * SSD_STATE]
    cmb = xcb[:, SSD_WIDTH + SSD_GROUPS * SSD_STATE:]

    lower, upper = _chunk_masks(q)
    tril = jnp.where(lower, 1.0, 0.0).astype(BF16)
    triu = jnp.where(upper, 1.0, 0.0).astype(BF16)
    dt = dt_ref[...]
    dtt = dtt_ref[...]
    dta = dt * (-jnp.exp(alog_row_ref[...]))
    dtat = dtt * (-jnp.exp(alog_col_ref[...]))
    la = _prefix_rows(dta, tril)
    tot = la[q - 1:q, :]
    lane = lax.broadcasted_iota(jnp.int32, (q, LANE), 1)
    lcol = jnp.where(lane < SSD_HEADS, la, tot - la + dta)
    lat = sum(_dot(p, triu) for p in _split3(dtat))
    tott = lat[:, q - 1:q]
    rowi = lax.broadcasted_iota(jnp.int32, (2 * SSD_HEADS, q), 0)
    lrow = jnp.where(rowi < SSD_HEADS, lat, tott - lat + dtat)
    lcol2 = lcol * log2e
    lrow2 = (lrow - jnp.log(dtt)) * log2e
    ecol = jnp.where(lane < 2 * SSD_HEADS, jnp.exp(lcol), 0.0)
    emat = _head_expand_matrix()
    eexp = _expand_heads(ecol, emat)

    @pl.when(seq_last)
    def _():
        if has_h0:
            st_ref[...] = h0_ref[0]
        else:
            st_ref[...] = jnp.zeros_like(st_ref)

    hb = st_ref[...]
    hsb = hb.astype(BF16)
    hsf = hsf_ref[0]

    rep = SSD_HEADS // SSD_GROUPS
    neg = jnp.float32(-jnp.inf)
    lane_q = lax.broadcasted_iota(jnp.int32, (q, LANE), 1)
    cbs = []
    for grp in range(SSD_GROUPS):
        cg = cmb[:, grp * SSD_STATE:(grp + 1) * SSD_STATE]
        bg = bmb[:, grp * SSD_STATE:(grp + 1) * SSD_STATE]
        cbs.append(_dot_nt(cg, bg))
    tiles = []
    for pair in range(SSD_HEADS // 2):
        xpair = xsb[:, pair * LANE:(pair + 1) * LANE]
        res = []
        for hd in (2 * pair, 2 * pair + 1):
            jf, jb = hd, SSD_HEADS + hd
            ef = jnp.exp2(jnp.where(lower, lcol2[:, jf:jf + 1] - lrow2[jf:jf + 1, :], neg))
            eb = jnp.exp2(jnp.where(upper, lcol2[:, jb:jb + 1] - lrow2[jb:jb + 1, :], neg))
            mm = (cbs[hd // rep] * (ef + eb)).astype(BF16)
            res.append(_dot(mm, xpair))
        tiles.append(jnp.where(lane_q < SSD_HEAD_DIM, res[0], res[1]))
    y = jnp.concatenate(tiles, axis=1)

    gw = SSD_WIDTH // SSD_GROUPS
    for d, hst in ((0, hsf), (1, hsb)):
        parts = [_dot(cmb[:, grp * SSD_STATE:(grp + 1) * SSD_STATE], hst[:, grp * gw:(grp + 1) * gw])
                 for grp in range(SSD_GROUPS)]
        y = y + jnp.concatenate(parts, axis=1) * eexp[:, d * SSD_WIDTH:(d + 1) * SSD_WIDTH]

    y = y + dskip_ref[...] * xsb.astype(F32)
    y = y * _silu(z_ref[...])
    y_ref[...] = _rms_rows(y, gn_ref[...]).astype(BF16)

    hnew = hb * dec_ref[0][0:1, SSD_WIDTH:] + sb_ref[0]
    st_ref[...] = hnew

    @pl.when(pos == 0)
    def _():
        hfin_ref[0] = hnew.T


def _ssd(xbc, dt, dtt, z, lw, cps, h0=None):
    t = xbc.shape[0]
    q = SSD_CHUNK
    nchunks = t // q
    nseq = nchunks // cps
    hb = q // SUBLANE
    n8 = t // SUBLANE
    has_h0 = h0 is not None
    const = lambda i: (0, 0)
    st_block = (1, SSD_STATE, SSD_WIDTH)
    fin_block = (1, SSD_WIDTH, SSD_STATE)
    dec_block = (1, SUBLANE, 2 * SSD_WIDTH)

    in_specs = [
        pl.BlockSpec((q, SSD_XBC), lambda i: (i, 0)),
        pl.BlockSpec((SUBLANE, SSD_XBC), lambda i: (jnp.maximum(i * hb - 1, 0), 0)),
        pl.BlockSpec((SUBLANE, SSD_XBC), lambda i: (jnp.minimum((i + 1) * hb, n8 - 1), 0)),
        pl.BlockSpec((q, LANE), lambda i: (i, 0)),
        pl.BlockSpec((SSD_CONV, SSD_XBC), const),
        pl.BlockSpec((1, SSD_XBC), const),
        pl.BlockSpec((1, LANE), const),
    ]
    args = [xbc, xbc, xbc, dt, lw["conv_w"], lw["conv_b"], lw["alog_row"]]
    if has_h0:
        in_specs.append(pl.BlockSpec(st_block, lambda i: (i // cps, 0, 0)))
        args.append(h0[0])
    xcb, hsf, sb, dec, hfin_f = pl.pallas_call(
        functools.partial(_ssd_state_kernel, cps=cps, has_h0=has_h0),
        grid=(nchunks,),
        in_specs=in_specs,
        out_specs=[
            pl.BlockSpec((q, SSD_XBC), lambda i: (i, 0)),
            pl.BlockSpec(st_block, lambda i: (i, 0, 0)),
            pl.BlockSpec(st_block, lambda i: (i, 0, 0)),
            pl.BlockSpec(dec_block, lambda i: (i, 0, 0)),
            pl.BlockSpec(fin_block, lambda i: (i // cps, 0, 0)),
        ],
        out_shape=[
            jax.ShapeDtypeStruct((t, SSD_XBC), BF16),
            jax.ShapeDtypeStruct((nchunks, SSD_STATE, SSD_WIDTH), BF16),
            jax.ShapeDtypeStruct((nchunks, SSD_STATE, SSD_WIDTH), F32),
            jax.ShapeDtypeStruct((nchunks, SUBLANE, 2 * SSD_WIDTH), F32),
            jax.ShapeDtypeStruct((nseq, SSD_WIDTH, SSD_STATE), F32),
        ],
        scratch_shapes=[pltpu.VMEM((SSD_STATE, SSD_WIDTH), F32)],
        compiler_params=_cparams("arbitrary"),
        name="ssd_state",
    )(*args)

    gi = lambda i: nchunks - 1 - i
    in_specs = [
        pl.BlockSpec((q, SSD_XBC), lambda i: (gi(i), 0)),
        pl.BlockSpec((q, LANE), lambda i: (gi(i), 0)),
        pl.BlockSpec((2 * SSD_HEADS, q), lambda i: (0, gi(i))),
        pl.BlockSpec((q, SSD_WIDTH), lambda i: (gi(i), 0)),
        pl.BlockSpec(st_block, lambda i: (gi(i), 0, 0)),
        pl.BlockSpec(st_block, lambda i: (gi(i), 0, 0)),
        pl.BlockSpec(dec_block, lambda i: (gi(i), 0, 0)),
        pl.BlockSpec((1, LANE), const),
        pl.BlockSpec((2 * SSD_HEADS, 1), const),
        pl.BlockSpec((1, SSD_WIDTH), const),
        pl.BlockSpec((1, SSD_WIDTH), const),
    ]
    args = [xcb, dt, dtt, z, hsf, sb, dec, lw["alog_row"], lw["alog_col"], lw["dskip_row"], lw["ssd_norm"]]
    if has_h0:
        in_specs.append(pl.BlockSpec(st_block, lambda i: (gi(i) // cps, 0, 0)))
        args.append(h0[1])
    y, hfin_b = pl.pallas_call(
        functools.partial(_ssd_out_kernel, cps=cps, has_h0=has_h0, nchunks=nchunks),
        grid=(nchunks,),
        in_specs=in_specs,
        out_specs=[
            pl.BlockSpec((q, SSD_WIDTH), lambda i: (gi(i), 0)),
            pl.BlockSpec(fin_block, lambda i: (gi(i) // cps, 0, 0)),
        ],
        out_shape=[
            jax.ShapeDtypeStruct((t, SSD_WIDTH), BF16),
            jax.ShapeDtypeStruct((nseq, SSD_WIDTH, SSD_STATE), F32),
        ],
        scratch_shapes=[pltpu.VMEM((SSD_STATE, SSD_WIDTH), F32)],
        compiler_params=_cparams("arbitrary"),
        name="ssd_out",
    )(*args)
    return y, hfin_f, hfin_b


def _state_to_kernel_layout(h):
    n = h.shape[0]
    return h.transpose(0, 3, 1, 2).reshape(n, SSD_STATE, SSD_WIDTH)


def _state_from_kernel_layout(h):
    return h.reshape(h.shape[0], SSD_HEADS, SSD_HEAD_DIM, SSD_STATE)


def _attn_kernel(*refs, heads, has_cache):
    if has_cache:
        qt_ref, k_ref, vt_ref, kc_ref, vct_ref = refs[:5]
    else:
        qt_ref, k_ref, vt_ref = refs[:3]
    s_refs = refs[-2:]
    o_ref = refs[-3]
    lk = k_ref.shape[0]
    kb = min(ATTN_KEY_BLOCK, lk)
    blocks = [(k_ref, vt_ref, i * kb) for i in range(lk // kb)]
    if has_cache:
        lc = kc_ref.shape[0]
        kbc = min(ATTN_KEY_BLOCK, lc)
        blocks += [(kc_ref, vct_ref, i * kbc) for i in range(lc // kbc)]
        assert kbc == kb
    nblk = len(blocks)
    ones = jnp.ones((ONES_ROWS, kb), BF16)

    def score_block(hd, i, m):
        kr, _, off = blocks[i]
        q = qt_ref[hd * HEAD_PAD:(hd + 1) * HEAD_PAD, :]
        s = _dot(kr[off:off + kb, hd * HEAD_PAD:(hd + 1) * HEAD_PAD], q)
        s_refs[hd % 2][i * kb:(i + 1) * kb, :] = s
        bm = jnp.max(s, axis=0, keepdims=True)
        return bm if m is None else jnp.maximum(m, bm)

    def value_block(hd, i, m, acc):
        _, vr, off = blocks[i]
        p = jnp.exp2((s_refs[hd % 2][i * kb:(i + 1) * kb, :] - m).astype(BF16))
        v = vr[hd * MLA_V_DIM:(hd + 1) * MLA_V_DIM, off:off + kb]
        part = _dot(jnp.concatenate([v, ones], axis=0), p)
        return part if acc is None else acc + part

    m_cur = None
    for i in range(nblk):
        m_cur = score_block(0, i, m_cur)
    for hd in range(heads):
        m_next, acc = None, None
        for i in range(nblk):
            if hd + 1 < heads:
                m_next = score_block(hd + 1, i, m_next)
            acc = value_block(hd, i, m_cur, acc)
        vs = slice(hd * MLA_V_DIM, (hd + 1) * MLA_V_DIM)
        o_ref[vs, :] = acc[:MLA_V_DIM] / acc[MLA_V_DIM:MLA_V_DIM + 1]
        m_cur = m_next


def _attention(qt, k, vt, seq_len, heads_per_step, cache=None):
    t = k.shape[0]
    nseq = t // seq_len
    tq = min(ATTN_Q_TILE, seq_len)
    nq = seq_len // tq
    g = heads_per_step
    in_specs = [
        pl.BlockSpec((g * HEAD_PAD, tq), lambda s, h, j: (h, s * nq + j)),
        pl.BlockSpec((seq_len, g * HEAD_PAD), lambda s, h, j: (s, h)),
        pl.BlockSpec((g * MLA_V_DIM, seq_len), lambda s, h, j: (h, s)),
    ]
    args = [qt, k, vt]
    n_keys = seq_len
    if cache is not None:
        kc, vct = cache
        past = kc.shape[0] // nseq
        n_keys += past
        in_specs += [
            pl.BlockSpec((past, g * HEAD_PAD), lambda s, h, j: (s, h)),
            pl.BlockSpec((g * MLA_V_DIM, past), lambda s, h, j: (h, s)),
        ]
        args += [kc, vct]
    kern = functools.partial(_attn_kernel, heads=g, has_cache=cache is not None)
    return pl.pallas_call(
        kern,
        grid=(nseq, MLA_HEADS // g, nq),
        in_specs=in_specs,
        out_specs=pl.BlockSpec((g * MLA_V_DIM, tq), lambda s, h, j: (h, s * nq + j)),
        out_shape=jax.ShapeDtypeStruct((MLA_WIDTH, t), F32),
        scratch_shapes=[pltpu.VMEM((n_keys, tq), F32), pltpu.VMEM((n_keys, tq), F32)],
        compiler_params=_cparams("arbitrary", "arbitrary", "arbitrary"),
        name="attention",
    )(*args)


def _outproj_kernel(*refs, has_router):
    if has_router:
        (y_ref, ot_ref, x_ref, mod_ref, wa_ref, wb_ref, gm_ref, gpost_ref, gpre2_ref, rt_ref,
         x1_ref, h2_ref, comb_ref) = refs
    else:
        (y_ref, ot_ref, x_ref, mod_ref, wa_ref, wb_ref, gm_ref, gpost_ref, gpre2_ref,
         x1_ref, h2_ref) = refs
    mod = mod_ref[0]
    gate1 = mod[:, 2 * D_MODEL:3 * D_MODEL]
    shift2 = mod[:, 3 * D_MODEL:4 * D_MODEL]
    scale2 = mod[:, 4 * D_MODEL:5 * D_MODEL]
    ot = ot_ref[...]
    ms = jnp.mean(ot * ot, axis=0, keepdims=True)
    on = (ot * lax.rsqrt(ms + EPS) * gm_ref[...]).astype(BF16)
    y = _dot(y_ref[...], wa_ref[...]) + _dot_tn(on, wb_ref[...])
    x1 = x_ref[...] + gate1 * _rms_rows(y, gpost_ref[...])
    x1_ref[...] = x1
    h2 = _rms_rows(x1, gpre2_ref[...]) * (1.0 + scale2) + shift2
    h2_ref[...] = h2.astype(BF16)
    if has_router:
        hh, hm, _ = _split3(h2)
        rh, rm, _ = _split3(rt_ref[...])
        logits = _dot(hh, rh) + (_dot(hm, rh) + _dot(hh, rm))
        lane = lax.broadcasted_iota(jnp.int32, logits.shape, 1).astype(F32)
        neg = jnp.float32(-jnp.inf)
        lg = jnp.where(lane < N_EXPERTS, logits, neg)
        m1 = jnp.max(lg, axis=-1, keepdims=True)
        i1 = jnp.min(jnp.where(lg == m1, lane, float(LANE)), axis=-1, keepdims=True)
        lg2 = jnp.where(lane == i1, neg, lg)
        m2 = jnp.max(lg2, axis=-1, keepdims=True)
        i2 = jnp.min(jnp.where(lg2 == m2, lane, float(LANE)), axis=-1, keepdims=True)
        e2 = jnp.exp(m2 - m1)
        w1 = 1.0 / (1.0 + e2)
        w2 = e2 / (1.0 + e2)
        comb_ref[...] = jnp.where(lane == i1, w1, 0.0) + jnp.where(lane == i2, w2, 0.0)


def _outproj(yssd, ot, x, mod3, mod_row_fn, lw, router=None):
    t = x.shape[0]
    tm = ROW_TILE
    const = lambda i: (0, 0)
    row = lambda i: (i, 0)
    in_specs = [
        pl.BlockSpec((tm, SSD_WIDTH), row),
        pl.BlockSpec((MLA_WIDTH, tm), lambda i: (0, i)),
        pl.BlockSpec((tm, D_MODEL), row),
        pl.BlockSpec((1, 1, N_MOD * D_MODEL), lambda i: (mod_row_fn(ROW_TILE)(i), 0, 0)),
        pl.BlockSpec((SSD_WIDTH, D_MODEL), const),
        pl.BlockSpec((MLA_WIDTH, D_MODEL), const),
        pl.BlockSpec((MLA_WIDTH, 1), const),
        pl.BlockSpec((1, D_MODEL), const),
        pl.BlockSpec((1, D_MODEL), const),
    ]
    args = [yssd, ot, x, mod3, lw["w_out_a"], lw["w_out_b"], lw["mla_norm_col"],
            lw["g_post1"], lw["g_pre2"]]
    out_specs = [pl.BlockSpec((tm, D_MODEL), row), pl.BlockSpec((tm, D_MODEL), row)]
    out_shape = [jax.ShapeDtypeStruct((t, D_MODEL), F32), jax.ShapeDtypeStruct((t, D_MODEL), BF16)]
    if router is not None:
        in_specs.append(pl.BlockSpec((D_MODEL, LANE), const))
        args.append(router)
        out_specs.append(pl.BlockSpec((tm, LANE), row))
        out_shape.append(jax.ShapeDtypeStruct((t, LANE), F32))
    return pl.pallas_call(
        functools.partial(_outproj_kernel, has_router=router is not None),
        grid=(t // tm,),
        in_specs=in_specs,
        out_specs=out_specs,
        out_shape=out_shape,
        compiler_params=_cparams("arbitrary"),
        name="outproj",
    )(*args)


def _ffn_kernel(*refs, has_comb, nslab):
    if has_comb:
        h_ref, x_ref, comb_ref, mod_ref, wgu_ref, wd_ref, gpost_ref, o_ref, acc_ref = refs
    else:
        h_ref, x_ref, mod_ref, wgu_ref, wd_ref, gpost_ref, o_ref, acc_ref = refs
    e = pl.program_id(1)
    h = h_ref[...]
    f = wd_ref.shape[1]
    gu = _dot(h, wgu_ref[0])
    hid = _silu(gu[:, :f]) * gu[:, f:]
    if has_comb:
        comb = comb_ref[...]
        lane = lax.broadcasted_iota(jnp.int32, comb.shape, 1)
        wcol = jnp.sum(jnp.where(lane == e, comb, 0.0), axis=-1, keepdims=True)
        hid = hid * wcol
    part = _dot(hid.astype(BF16), wd_ref[0])

    @pl.when(e == 0)
    def _():
        acc_ref[...] = part

    @pl.when(e > 0)
    def _():
        acc_ref[...] += part

    @pl.when(e == nslab - 1)
    def _():
        gate2 = mod_ref[0][:, 5 * D_MODEL:6 * D_MODEL]
        o_ref[...] = x_ref[...] + gate2 * _rms_rows(acc_ref[...], gpost_ref[...])


def _ffn(h2, x1, mod3, mod_row_fn, wgu, wd, gpost, comb=None):
    t = x1.shape[0]
    tm = FFN_ROW_TILE
    nslab, f, _ = wd.shape
    row = lambda i, e: (i, 0)
    in_specs = [pl.BlockSpec((tm, D_MODEL), row), pl.BlockSpec((tm, D_MODEL), row)]
    args = [h2, x1]
    if comb is not None:
        in_specs.append(pl.BlockSpec((tm, LANE), row))
        args.append(comb)
    in_specs += [
        pl.BlockSpec((1, 1, N_MOD * D_MODEL), lambda i, e: (mod_row_fn(FFN_ROW_TILE)(i), 0, 0)),
        pl.BlockSpec((1, D_MODEL, 2 * f), lambda i, e: (e, 0, 0)),
        pl.BlockSpec((1, f, D_MODEL), lambda i, e: (e, 0, 0)),
        pl.BlockSpec((1, D_MODEL), lambda i, e: (0, 0)),
    ]
    args += [mod3, wgu, wd, gpost]
    return pl.pallas_call(
        functools.partial(_ffn_kernel, has_comb=comb is not None, nslab=nslab),
        grid=(t // tm, nslab),
        in_specs=in_specs,
        out_specs=pl.BlockSpec((tm, D_MODEL), row),
        out_shape=jax.ShapeDtypeStruct((t, D_MODEL), F32),
        scratch_shapes=[pltpu.VMEM((tm, D_MODEL), F32)],
        compiler_params=_cparams("arbitrary", "arbitrary"),
        name="ffn",
    )(*args)


def _moe_kernel(h_ref, x_ref, comb_ref, mod_ref, wgu_ref, wd_ref, gpost_ref, o_ref,
                acc_ref, rank_ref, rank_t_ref, comb_t_ref, *, nexp, sub):
    e = pl.program_id(1)
    tm = h_ref.shape[0]
    cap = MOE_CAP
    f = wd_ref.shape[1]
    capl = -(-cap // LANE) * LANE

    @pl.when(e == 0)
    def _():
        acc_ref[...] = jnp.zeros_like(acc_ref)
        r_i = lax.broadcasted_iota(jnp.int32, (sub, sub), 0)
        c_i = lax.broadcasted_iota(jnp.int32, (sub, sub), 1)
        strict = jnp.where(r_i > c_i, 1.0, 0.0).astype(BF16)
        for s in range(tm // sub):
            rows = slice(s * sub, (s + 1) * sub)
            comb = comb_ref[rows, :]
            rank = _dot(strict, jnp.where(comb > 0.0, 1.0, 0.0).astype(BF16))
            rank_ref[rows, :] = rank
            rank_t_ref[:, rows] = rank.T
            comb_t_ref[:, rows] = comb.T

    lane = lax.broadcasted_iota(jnp.int32, (sub, LANE), 1)
    slot_l = lax.broadcasted_iota(jnp.int32, (sub, capl), 1).astype(F32)
    slot_s = lax.broadcasted_iota(jnp.int32, (cap, sub), 0).astype(F32)
    for s in range(tm // sub):
        rows = slice(s * sub, (s + 1) * sub)
        pick = lane == e
        wcol = jnp.sum(jnp.where(pick, comb_ref[rows, :], 0.0), axis=-1, keepdims=True)
        rcol = jnp.sum(jnp.where(pick, rank_ref[rows, :], 0.0), axis=-1, keepdims=True)
        wrow = comb_t_ref[pl.ds(e, 1), rows]
        rrow = rank_t_ref[pl.ds(e, 1), rows]
        count = jnp.max(jnp.where(wrow > 0.0, rrow + 1.0, 0.0))
        nchunk = ((count + (cap - 1.0)) * (1.0 / cap)).astype(jnp.int32)
        hsub = h_ref[rows, :]

        def chunk(c, carry, rows=rows, wcol=wcol, rcol=rcol, wrow=wrow, rrow=rrow, hsub=hsub):
            base = (c * cap).astype(F32)
            gather = jnp.where(((rrow - base) == slot_s) & (wrow > 0.0), 1.0, 0.0).astype(BF16)
            scatter = jnp.where(((rcol - base) == slot_l) & (wcol > 0.0) & (slot_l < float(cap)),
                                1.0, 0.0).astype(BF16)
            xg = _dot(gather, hsub).astype(BF16)
            gu = _dot(xg, wgu_ref[0])
            hid = _silu(gu[:, :f]) * gu[:, f:]
            y = _dot(hid.astype(BF16), wd_ref[0]).astype(BF16)
            if capl > cap:
                y = jnp.concatenate([y, jnp.zeros((capl - cap, y.shape[1]), BF16)], axis=0)
            acc_ref[rows, :] += wcol * _dot(scatter, y)
            return carry

        lax.fori_loop(0, nchunk, chunk, 0)

    @pl.when(e == nexp - 1)
    def _():
        gate2 = mod_ref[0][:, 5 * D_MODEL:6 * D_MODEL]
        o_ref[...] = x_ref[...] + gate2 * _rms_rows(acc_ref[...], gpost_ref[...])


def _moe(h2, x1, comb, mod3, mod_row_fn, wgu, wd, gpost):
    t = x1.shape[0]
    tm = min(MOE_ROW_TILE, t)
    sub = min(MOE_SUB_TILE, tm)
    nexp, f, _ = wd.shape
    row = lambda i, e: (i, 0)
    return pl.pallas_call(
        functools.partial(_moe_kernel, nexp=nexp, sub=sub),
        grid=(t // tm, nexp),
        in_specs=[
            pl.BlockSpec((tm, D_MODEL), row),
            pl.BlockSpec((tm, D_MODEL), row),
            pl.BlockSpec((tm, LANE), row),
            pl.BlockSpec((1, 1, N_MOD * D_MODEL), lambda i, e: (mod_row_fn(tm)(i), 0, 0)),
            pl.BlockSpec((1, D_MODEL, 2 * f), lambda i, e: (e, 0, 0)),
            pl.BlockSpec((1, f, D_MODEL), lambda i, e: (e, 0, 0)),
            pl.BlockSpec((1, D_MODEL), lambda i, e: (0, 0)),
        ],
        out_specs=pl.BlockSpec((tm, D_MODEL), row),
        out_shape=jax.ShapeDtypeStruct((t, D_MODEL), F32),
        scratch_shapes=[pltpu.VMEM((tm, D_MODEL), F32), pltpu.VMEM((tm, LANE), F32),
                        pltpu.VMEM((LANE, tm), F32), pltpu.VMEM((LANE, tm), F32)],
        compiler_params=_cparams("arbitrary", "arbitrary"),
        name="moe",
    )(h2, x1, comb, mod3, wgu, wd, gpost)


_ROT_PERM = tuple(list(range(8, 16)) + list(range(0, 8)) + list(range(24, 32)) + list(range(16, 24)))
_ROT_SIGN = tuple([-1.0] * 8 + [1.0] * 8 + [-1.0] * 8 + [1.0] * 8)


def _rot_cols(w):
    return w[..., jnp.array(_ROT_PERM)] * jnp.array(_ROT_SIGN, F32)


def _prep_layer(i, p):
    w_in = p["w_in"][i]
    s1 = SSD_WIDTH
    s2 = s1 + SSD_XBC
    s3 = s2 + 2 * SSD_HEADS
    s4 = s3 + MLA_Q_RANK
    s5 = s4 + MLA_KV_RANK
    w_z, w_xbc, w_dt, w_cq, w_ckv, w_kr = (w_in[:, :s1], w_in[:, s1:s2], w_in[:, s2:s3],
                                             w_in[:, s3:s4], w_in[:, s4:s5], w_in[:, s5:])
    zc = lambda n: jnp.zeros((D_MODEL, n), F32)
    tile_a = jnp.concatenate([w_dt, zc(ROPE_LANE0 - 2 * SSD_HEADS), w_kr,
                              zc(LANE - ROPE_LANE0 - MLA_ROPE_DIM)], axis=1)
    tile_b = jnp.concatenate([zc(ROPE_LANE0), _rot_cols(w_kr),
                              zc(LANE - ROPE_LANE0 - MLA_ROPE_DIM)], axis=1)
    w_in_pad = jnp.concatenate([w_z, w_xbc, w_cq, w_ckv, tile_a, tile_b], axis=1).astype(BF16)

    w_uq = p["w_uq"][i].reshape(MLA_Q_RANK, MLA_HEADS, MLA_NOPE_DIM + MLA_ROPE_DIM)
    q_nope, q_rope = w_uq[..., :MLA_NOPE_DIM], w_uq[..., MLA_NOPE_DIM:]
    zq = lambda n: jnp.zeros((MLA_Q_RANK, MLA_HEADS, n), F32)
    pad = HEAD_PAD - MLA_NOPE_DIM - MLA_ROPE_DIM
    w_q = jnp.concatenate([q_nope, q_rope, zq(pad)], axis=-1).reshape(MLA_Q_RANK, -1)
    w_qrot = jnp.concatenate([zq(MLA_NOPE_DIM), _rot_cols(q_rope), zq(pad)], axis=-1).reshape(MLA_Q_RANK, -1)

    w_ukv = p["w_ukv"][i].reshape(MLA_KV_RANK, MLA_HEADS, MLA_NOPE_DIM + MLA_V_DIM)
    k_nope, v_w = w_ukv[..., :MLA_NOPE_DIM], w_ukv[..., MLA_NOPE_DIM:]
    w_uk = jnp.concatenate([k_nope, jnp.zeros((MLA_KV_RANK, MLA_HEADS, HEAD_PAD - MLA_NOPE_DIM), F32)],
                           axis=-1).reshape(MLA_KV_RANK, -1)
    w_uv = v_w.reshape(MLA_KV_RANK, MLA_WIDTH)

    dtb = p["dt_bias"][i].reshape(2 * SSD_HEADS)
    alog = p["a_log"][i].reshape(2 * SSD_HEADS)
    padl = lambda v: jnp.pad(v, (0, LANE - v.shape[0])).reshape(1, LANE)
    w_out = p["w_out"][i]
    return {
        "g_pre1": p["norm_pre_mix"][i].reshape(1, D_MODEL),
        "g_post1": p["norm_post_mix"][i].reshape(1, D_MODEL),
        "g_pre2": p["norm_pre_ffn"][i].reshape(1, D_MODEL),
        "g_post2": p["norm_post_ffn"][i].reshape(1, D_MODEL),
        "w_in": w_in_pad,
        "w_dt": w_dt.T.astype(BF16),
        "dtb_row": padl(dtb),
        "dtb_col": dtb.reshape(-1, 1),
        "alog_row": padl(alog),
        "alog_col": alog.reshape(-1, 1),
        "q_norm": p["q_norm"][i].reshape(1, -1),
        "w_q": w_q.T.astype(BF16),
        "w_qrot": w_qrot.T.astype(BF16),
        "kv_norm": p["kv_norm"][i].reshape(1, -1),
        "w_uk": w_uk.astype(BF16),
        "w_uv": w_uv.T.astype(BF16),
        "conv_w": p["conv_w"][i],
        "conv_b": p["conv_b"][i].reshape(1, -1),
        "dskip_row": jnp.repeat(p["d_skip"][i], SSD_HEAD_DIM).reshape(1, -1),
        "ssd_norm": p["ssd_norm"][i].reshape(1, -1),
        "mla_norm_col": p["mla_norm"][i].reshape(-1, 1),
        "w_out_a": w_out[:SSD_WIDTH].astype(BF16),
        "w_out_b": w_out[SSD_WIDTH:].astype(BF16),
    }


def _rope_tables(n_tokens):
    rows = n_tokens // GRID_W
    row = np.repeat(np.arange(rows, dtype=np.float32), GRID_W)
    col = np.tile(np.arange(GRID_W, dtype=np.float32), rows)
    half = MLA_ROPE_DIM // 2
    inv = (np.float32(ROPE_THETA) ** (-np.arange(0, half, 2, dtype=np.float32) / np.float32(half))).astype(np.float32)
    ar = row[:, None] * inv[None, :]
    ac = col[:, None] * inv[None, :]
    ang = np.concatenate([ar, ar, ac, ac], axis=-1).astype(np.float32)
    return jnp.asarray(np.cos(ang), F32), jnp.asarray(np.sin(ang), F32)


def _attn_tables(cos, sin, n):
    scale = (MLA_NOPE_DIM + MLA_ROPE_DIM) ** -0.5 * math.log2(math.e)
    pad = HEAD_PAD - ROPE_LANE0 - MLA_ROPE_DIM
    cosk = jnp.concatenate([jnp.zeros((n, ROPE_LANE0), F32), cos, jnp.zeros((n, pad), F32)], axis=1)
    sink = jnp.concatenate([jnp.zeros((n, ROPE_LANE0), F32), sin, jnp.zeros((n, pad), F32)], axis=1)
    cosq = jnp.concatenate([jnp.ones((n, ROPE_LANE0), F32), cos, jnp.zeros((n, pad), F32)], axis=1)
    return (cosq * scale).T, (sink * scale).T, cosk, sink


def kernel(x_prompt, x_sample, cache_ckv, cache_krope, state_ssm, c, c_ctx, w_mod, b_mod, norm_pre_mix, norm_post_mix, norm_pre_ffn, norm_post_ffn, w_in, conv_w, conv_b, dt_bias, a_log, d_skip, ssd_norm, q_norm, w_uq, kv_norm, w_ukv, mla_norm, w_out, ffn_w_gate, ffn_w_up, ffn_w_down, moe_router, moe_w_gate, moe_w_up, moe_w_down):
    params = dict(w_in=w_in, conv_w=conv_w, conv_b=conv_b, dt_bias=dt_bias, a_log=a_log, d_skip=d_skip,
                  ssd_norm=ssd_norm, q_norm=q_norm, w_uq=w_uq, kv_norm=kv_norm, w_ukv=w_ukv,
                  mla_norm=mla_norm, w_out=w_out, norm_pre_mix=norm_pre_mix, norm_post_mix=norm_post_mix,
                  norm_pre_ffn=norm_pre_ffn, norm_post_ffn=norm_post_ffn)
    batch, seq, d = x_prompt.shape
    dec_batch, dec_seq, _ = x_sample.shape
    depth = w_in.shape[0]
    past = cache_ckv.shape[2]
    tm = ROW_TILE

    cvec = jnp.concatenate([c_ctx[None, :], c, jnp.zeros((SUBLANE - 1 - dec_batch, d), F32)], axis=0)
    mod = _modulation(cvec, w_mod, b_mod)

    ones = jnp.ones((tm, MLA_ROPE_DIM), F32)
    tabs_ctx = _attn_tables(ones, jnp.zeros_like(ones), tm)
    cos, sin = _rope_tables(dec_seq)
    tabs_lat = _attn_tables(cos, sin, dec_seq)
    lat_blocks = dec_seq // tm

    xp = x_prompt.reshape(batch * seq, d)
    xs = x_sample.reshape(dec_batch * dec_seq, d)
    new_ckv, new_kr, new_ssm = [], [], []
    for i in range(depth):
        lw = _prep_layer(i, params)
        mod3 = mod[i].reshape(SUBLANE, 1, N_MOD * d)
        j = i // 2
        if i % 2 == 0:
            f = ffn_w_gate.shape[2] // 2
            wgu = jnp.stack([jnp.concatenate([ffn_w_gate[j][:, s * f:(s + 1) * f], ffn_w_up[j][:, s * f:(s + 1) * f]],
                                             axis=1) for s in range(2)], axis=0).astype(BF16)
            wd = ffn_w_down[j].reshape(2, f, d).astype(BF16)
            router = None
        else:
            wgu = jnp.concatenate([moe_w_gate[j], moe_w_up[j]], axis=-1).astype(BF16)
            wd = moe_w_down[j].astype(BF16)
            router = jnp.pad(moe_router[j], ((0, 0), (0, LANE - N_EXPERTS)))

        def run(x, row_fn, tabs, tab_blocks, seq_len, heads_per_step, ctx):
            z, xbc, dt, dtt, qt, k, vt, ckvn, kr = _inproj(x, mod3, row_fn, lw, tabs, tab_blocks)
            cps = seq_len // SSD_CHUNK
            h0 = None
            if ctx is not None:
                h0 = (_state_to_kernel_layout(ctx[2][:, 0]), _state_to_kernel_layout(ctx[2][:, 1]))
            yssd, hf, hb = _ssd(xbc, dt, dtt, z, lw, cps, h0=h0)
            hf, hb = _state_from_kernel_layout(hf), _state_from_kernel_layout(hb)
            cache = None
            if ctx is not None:
                kr_tile = jnp.pad(ctx[1].reshape(-1, MLA_ROPE_DIM),
                                  ((0, 0), (ROPE_LANE0, HEAD_PAD - ROPE_LANE0 - MLA_ROPE_DIM)))
                cache = _kvcache(ctx[0].reshape(-1, MLA_KV_RANK), kr_tile, lw)
            ot = _attention(qt, k, vt, seq_len, heads_per_step, cache=cache)
            outs = _outproj(yssd, ot, x, mod3, row_fn, lw, router=router)
            x1, h2 = outs[0], outs[1]
            comb = outs[2] if router is not None else None
            if comb is None:
                x2 = _ffn(h2, x1, mod3, row_fn, wgu, wd, lw["g_post2"])
            else:
                x2 = _moe(h2, x1, comb, mod3, row_fn, wgu, wd, lw["g_post2"])
            return x2, ckvn, kr, hf, hb

        xp, ckvn, kr, hf, hb = run(xp, lambda tile: (lambda b: 0), tabs_ctx, 1, seq, MLA_HEADS, None)
        new_ckv.append(ckvn.reshape(batch, seq, MLA_KV_RANK))
        new_kr.append(kr.reshape(batch, seq, MLA_ROPE_DIM))
        new_ssm.append(jnp.stack([hf, hb], axis=1))
        xs, _, _, _, _ = run(xs, lambda tile: (lambda b: 1 + (b * tile) // dec_seq), tabs_lat, lat_blocks, dec_seq, MLA_HEADS,
                             (cache_ckv[:, i], cache_krope[:, i], state_ssm[:, i]))
    return (xp.reshape(batch, seq, d), xs.reshape(dec_batch, dec_seq, d),
            jnp.stack(new_ckv, axis=1), jnp.stack(new_kr, axis=1), jnp.stack(new_ssm, axis=1))
```

```python
import functools
import math

import jax
import jax.numpy as jnp
import numpy as np
from jax import lax
from jax.experimental import pallas as pl
from jax.experimental.pallas import tpu as pltpu

F32 = jnp.float32
BF16 = jnp.bfloat16

D_MODEL = 1024
GRID_W = 64
SSD_WIDTH = 512
SSD_HEAD_DIM = 64
SSD_HEADS = 8
SSD_GROUPS = 2
SSD_STATE = 64
SSD_CONV = 5
SSD_CHUNK = 128
SSD_STEP_CHUNKS = 4
SSD_XBC = SSD_WIDTH + 2 * SSD_GROUPS * SSD_STATE
MLA_WIDTH = 512
MLA_V_DIM = 64
MLA_HEADS = 8
MLA_NOPE_DIM = 64
MLA_ROPE_DIM = 32
MLA_Q_RANK = 384
MLA_KV_RANK = 256
ROPE_THETA = 10000.0
N_EXPERTS = 8
N_MOD = 6
EPS = 1e-6

LANE = 128
SUBLANE = 8
HEAD_PAD = 128
ONES_ROWS = 16
ROPE_LANE0 = MLA_NOPE_DIM
C_Z = 0
C_XBC = C_Z + SSD_WIDTH
C_CQ = C_XBC + SSD_XBC
C_CKV = C_CQ + MLA_Q_RANK
C_TA = C_CKV + MLA_KV_RANK
C_TB = C_TA + LANE
IN_PAD = C_TB + LANE

VMEM_LIMIT = 56 * 1024 * 1024

ROW_TILE = 512
ATTN_Q_TILE = 256
ATTN_KEY_BLOCK = 512
FFN_ROW_TILE = 512
MOE_ROW_TILE = 1024
MOE_SUB_TILE = 512
MOE_CAP = 160

NT_DIMS = (((1,), (1,)), ((), ()))
TN_DIMS = (((0,), (0,)), ((), ()))


def _cparams(*sem):
    return pltpu.CompilerParams(dimension_semantics=sem, vmem_limit_bytes=VMEM_LIMIT)


def _silu(x):
    return x / (1.0 + jnp.exp(-x))


def _softplus(x):
    return jnp.maximum(x, 0.0) + jnp.log(1.0 + jnp.exp(-jnp.abs(x)))


def _rms_rows(x, g):
    ms = jnp.mean(x * x, axis=-1, keepdims=True)
    return x * lax.rsqrt(ms + EPS) * g


def _dot(a, b):
    return jnp.dot(a, b, preferred_element_type=F32)


def _dot_nt(a, b):
    return lax.dot_general(a, b, NT_DIMS, preferred_element_type=F32)


def _dot_tn(a, b):
    return lax.dot_general(a, b, TN_DIMS, preferred_element_type=F32)


def _split3(x):
    hi = x.astype(BF16)
    r1 = x - hi.astype(F32)
    mid = r1.astype(BF16)
    lo = (r1 - mid.astype(F32)).astype(BF16)
    return hi, mid, lo


def _mod_kernel(c_ref, w_ref, b_ref, o_ref):
    s = _silu(c_ref[...]).astype(BF16)
    o_ref[0] = _dot(s, w_ref[0].astype(BF16)) + b_ref[0]


def _modulation(cvec, w_mod, b_mod):
    depth, d, n = w_mod.shape
    tn = 1536
    return pl.pallas_call(
        _mod_kernel,
        grid=(depth, n // tn),
        in_specs=[
            pl.BlockSpec((SUBLANE, d), lambda l, j: (0, 0)),
            pl.BlockSpec((1, d, tn), lambda l, j: (l, 0, j)),
            pl.BlockSpec((1, 1, tn), lambda l, j: (l, 0, j)),
        ],
        out_specs=pl.BlockSpec((1, SUBLANE, tn), lambda l, j: (l, 0, j)),
        out_shape=jax.ShapeDtypeStruct((depth, SUBLANE, n), F32),
        compiler_params=_cparams("arbitrary", "arbitrary"),
        name="modulation",
    )(cvec, w_mod, b_mod.reshape(depth, 1, n))


def _inproj_kernel(x_ref, mod_ref, gpre_ref, win_ref, wdt_ref, dtb_row_ref, dtb_col_ref,
                   qn_ref, wq_ref, wqrot_ref, kvn_ref, wuk_ref, wuv_ref,
                   cosq_ref, sinq_ref, cosk_ref, sink_ref,
                   z_ref, xbc_ref, dt_ref, dtt_ref, qt_ref, k_ref, vt_ref, ckvn_ref, kr_ref):
    mod = mod_ref[0]
    shift = mod[:, 0:D_MODEL]
    scale = mod[:, D_MODEL:2 * D_MODEL]
    h = _rms_rows(x_ref[...], gpre_ref[...]) * (1.0 + scale) + shift
    hb = h.astype(BF16)
    proj = _dot(hb, win_ref[...])
    z_ref[...] = proj[:, C_Z:C_XBC]
    xbc_ref[...] = proj[:, C_XBC:C_CQ]
    cqn = _rms_rows(proj[:, C_CQ:C_CKV], qn_ref[...]).astype(BF16)
    ckvn = _rms_rows(proj[:, C_CKV:C_TA], kvn_ref[...])
    ckvn_ref[...] = ckvn
    ckvb = ckvn.astype(BF16)
    ta = proj[:, C_TA:C_TB]
    tb = proj[:, C_TB:IN_PAD]
    dt_ref[...] = _softplus(ta + dtb_row_ref[...])
    kr_ref[...] = ta[:, ROPE_LANE0:ROPE_LANE0 + MLA_ROPE_DIM]
    kr_rot = ta * cosk_ref[...] + tb * sink_ref[...]
    knp = _dot(ckvb, wuk_ref[...])
    for hd in range(MLA_HEADS):
        sl = slice(hd * HEAD_PAD, (hd + 1) * HEAD_PAD)
        k_ref[:, sl] = (knp[:, sl] + kr_rot).astype(BF16)
    vt_ref[...] = _dot_nt(wuv_ref[...], ckvb).astype(BF16)
    qt = _dot_nt(wq_ref[...], cqn)
    qrt = _dot_nt(wqrot_ref[...], cqn)
    cosq = cosq_ref[...]
    sinq = sinq_ref[...]
    for hd in range(MLA_HEADS):
        sl = slice(hd * HEAD_PAD, (hd + 1) * HEAD_PAD)
        qt_ref[sl, :] = (qt[sl, :] * cosq + qrt[sl, :] * sinq).astype(BF16)
    dtt_ref[...] = _softplus(_dot_nt(wdt_ref[...], hb) + dtb_col_ref[...])


def _inproj(x, mod3, mod_row_fn, lw, tabs, tab_blocks):
    t = x.shape[0]
    tm = ROW_TILE
    nb = t // tm
    cosq, sinq, cosk, sink = tabs
    ntab = tab_blocks
    const = lambda i: (0, 0)
    row = lambda i: (i, 0)
    col = lambda i: (0, i)
    in_specs = [
        pl.BlockSpec((tm, D_MODEL), row),
        pl.BlockSpec((1, 1, N_MOD * D_MODEL), lambda i: (mod_row_fn(ROW_TILE)(i), 0, 0)),
        pl.BlockSpec((1, D_MODEL), const),
        pl.BlockSpec((D_MODEL, IN_PAD), const),
        pl.BlockSpec((2 * SSD_HEADS, D_MODEL), const),
        pl.BlockSpec((1, LANE), const),
        pl.BlockSpec((2 * SSD_HEADS, 1), const),
        pl.BlockSpec((1, MLA_Q_RANK), const),
        pl.BlockSpec((MLA_HEADS * HEAD_PAD, MLA_Q_RANK), const),
        pl.BlockSpec((MLA_HEADS * HEAD_PAD, MLA_Q_RANK), const),
        pl.BlockSpec((1, MLA_KV_RANK), const),
        pl.BlockSpec((MLA_KV_RANK, MLA_HEADS * HEAD_PAD), const),
        pl.BlockSpec((MLA_WIDTH, MLA_KV_RANK), const),
        pl.BlockSpec((HEAD_PAD, tm), lambda i: (0, i % ntab)),
        pl.BlockSpec((HEAD_PAD, tm), lambda i: (0, i % ntab)),
        pl.BlockSpec((tm, LANE), lambda i: (i % ntab, 0)),
        pl.BlockSpec((tm, LANE), lambda i: (i % ntab, 0)),
    ]
    out_specs = [
        pl.BlockSpec((tm, SSD_WIDTH), row),
        pl.BlockSpec((tm, SSD_XBC), row),
        pl.BlockSpec((tm, LANE), row),
        pl.BlockSpec((2 * SSD_HEADS, tm), col),
        pl.BlockSpec((MLA_HEADS * HEAD_PAD, tm), col),
        pl.BlockSpec((tm, MLA_HEADS * HEAD_PAD), row),
        pl.BlockSpec((MLA_WIDTH, tm), col),
        pl.BlockSpec((tm, MLA_KV_RANK), row),
        pl.BlockSpec((tm, MLA_ROPE_DIM), row),
    ]
    out_shape = [
        jax.ShapeDtypeStruct((t, SSD_WIDTH), F32),
        jax.ShapeDtypeStruct((t, SSD_XBC), F32),
        jax.ShapeDtypeStruct((t, LANE), F32),
        jax.ShapeDtypeStruct((2 * SSD_HEADS, t), F32),
        jax.ShapeDtypeStruct((MLA_HEADS * HEAD_PAD, t), BF16),
        jax.ShapeDtypeStruct((t, MLA_HEADS * HEAD_PAD), BF16),
        jax.ShapeDtypeStruct((MLA_WIDTH, t), BF16),
        jax.ShapeDtypeStruct((t, MLA_KV_RANK), F32),
        jax.ShapeDtypeStruct((t, MLA_ROPE_DIM), F32),
    ]
    return pl.pallas_call(
        _inproj_kernel,
        grid=(nb,),
        in_specs=in_specs,
        out_specs=out_specs,
        out_shape=out_shape,
        compiler_params=_cparams("arbitrary"),
        name="inproj",
    )(x, mod3, lw["g_pre1"], lw["w_in"], lw["w_dt"], lw["dtb_row"], lw["dtb_col"],
      lw["q_norm"], lw["w_q"], lw["w_qrot"], lw["kv_norm"], lw["w_uk"], lw["w_uv"],
      cosq, sinq, cosk, sink)


def _kvcache_kernel(ckv_ref, kr_ref, wuk_ref, wuv_ref, k_ref, vt_ref):
    ckvb = ckv_ref[...].astype(BF16)
    knp = _dot(ckvb, wuk_ref[...])
    kr = kr_ref[...]
    for hd in range(MLA_HEADS):
        sl = slice(hd * HEAD_PAD, (hd + 1) * HEAD_PAD)
        k_ref[:, sl] = (knp[:, sl] + kr).astype(BF16)
    vt_ref[...] = _dot_nt(wuv_ref[...], ckvb).astype(BF16)


def _kvcache(ckv, kr_tile, lw):
    n = ckv.shape[0]
    tm = 512
    return pl.pallas_call(
        _kvcache_kernel,
        grid=(n // tm,),
        in_specs=[
            pl.BlockSpec((tm, MLA_KV_RANK), lambda i: (i, 0)),
            pl.BlockSpec((tm, LANE), lambda i: (i, 0)),
            pl.BlockSpec((MLA_KV_RANK, MLA_HEADS * HEAD_PAD), lambda i: (0, 0)),
            pl.BlockSpec((MLA_WIDTH, MLA_KV_RANK), lambda i: (0, 0)),
        ],
        out_specs=[
            pl.BlockSpec((tm, MLA_HEADS * HEAD_PAD), lambda i: (i, 0)),
            pl.BlockSpec((MLA_WIDTH, tm), lambda i: (0, i)),
        ],
        out_shape=[
            jax.ShapeDtypeStruct((n, MLA_HEADS * HEAD_PAD), BF16),
            jax.ShapeDtypeStruct((MLA_WIDTH, n), BF16),
        ],
        compiler_params=_cparams("arbitrary"),
        name="kvcache",
    )(ckv, kr_tile, lw["w_uk"], lw["w_uv"])


def _head_expand_matrix():
    r = lax.broadcasted_iota(jnp.int32, (LANE, 2 * SSD_WIDTH), 0)
    c = lax.broadcasted_iota(jnp.int32, (LANE, 2 * SSD_WIDTH), 1)
    return jnp.where(c // SSD_HEAD_DIM == r, 1.0, 0.0).astype(BF16)


def _expand_heads(v, emat):
    hi = v.astype(BF16)
    mid = (v - hi.astype(F32)).astype(BF16)
    return _dot(hi, emat) + _dot(mid, emat)


def _prefix_rows(dta, tril):
    return sum(_dot(tril, p) for p in _split3(dta))


def _chunk_masks(q):
    r_i = lax.broadcasted_iota(jnp.int32, (q, q), 0)
    c_i = lax.broadcasted_iota(jnp.int32, (q, q), 1)
    return r_i >= c_i, r_i <= c_i


def _ssd_state_kernel(*refs, cps, nc, has_h0):
    it = iter(refs)
    xbc_ref, prev_ref, next_ref, dt_ref, cw_ref, cb_ref, alog_row_ref = (next(it) for _ in range(7))
    h0_ref = next(it) if has_h0 else None
    xcb_ref, hsf_ref, sb_ref, dec_ref, hfin_ref, st_ref = (next(it) for _ in range(6))

    q = SSD_CHUNK
    rows = nc * q
    pos = (pl.program_id(0) * nc) % cps
    seq_first = pos == 0
    seq_last = pos + nc == cps

    prev = jnp.where(seq_first, 0.0, prev_ref[...])
    nxt = jnp.where(seq_last, 0.0, next_ref[...])
    ext = jnp.concatenate([prev, xbc_ref[...], nxt], axis=0)
    cw = cw_ref[...]
    acc = cb_ref[...] + ext[SUBLANE - 2:SUBLANE - 2 + rows] * cw[0:1]
    for k in range(1, SSD_CONV):
        o = SUBLANE - 2 + k
        acc = acc + ext[o:o + rows] * cw[k:k + 1]
    xcb_all = _silu(acc).astype(BF16)
    xcb_ref[...] = xcb_all

    lower, _ = _chunk_masks(q)
    tril = jnp.where(lower, 1.0, 0.0).astype(BF16)
    a_row = -jnp.exp(alog_row_ref[...])
    lane = lax.broadcasted_iota(jnp.int32, (q, LANE), 1)
    lane_t = lax.broadcasted_iota(jnp.int32, (2 * SUBLANE, LANE), 1)
    emat = _head_expand_matrix()
    gw = SSD_WIDTH // SSD_GROUPS

    @pl.when(seq_first)
    def _():
        if has_h0:
            st_ref[...] = h0_ref[0]
        else:
            st_ref[...] = jnp.zeros_like(st_ref)

    chunk_dec, chunk_states = [], []
    for c in range(nc):
        sl = slice(c * q, (c + 1) * q)
        xcb = xcb_all[sl]
        xs = xcb[:, :SSD_WIDTH].astype(F32)
        dt = dt_ref[sl, :]
        dta = dt * a_row
        la = _prefix_rows(dta, tril)
        tot = la[q - 1:q, :]
        w = jnp.exp(jnp.where(lane < SSD_HEADS, tot - la, la - dta)) * dt
        w = jnp.where(lane < 2 * SSD_HEADS, w, 0.0)
        wexp = _expand_heads(w, emat)
        etot = jnp.where(lane_t < 2 * SSD_HEADS, jnp.exp(jnp.broadcast_to(tot, (2 * SUBLANE, LANE))), 0.0)
        dec = _expand_heads(etot, emat)[:SUBLANE]
        dec_ref[c] = dec
        bmb = xcb[:, SSD_WIDTH:SSD_WIDTH + SSD_GROUPS * SSD_STATE]
        states = []
        for d in range(2):
            xw = (xs * wexp[:, d * SSD_WIDTH:(d + 1) * SSD_WIDTH]).astype(BF16)
            parts = [_dot_tn(bmb[:, grp * SSD_STATE:(grp + 1) * SSD_STATE], xw[:, grp * gw:(grp + 1) * gw])
                     for grp in range(SSD_GROUPS)]
            states.append(jnp.concatenate(parts, axis=1))
        sb_ref[c] = states[1]
        chunk_dec.append(dec[0:1, :SSD_WIDTH])
        chunk_states.append(states[0])

    hs = st_ref[...]
    for c in range(nc):
        hsf_ref[c] = hs.astype(BF16)
        hs = hs * chunk_dec[c] + chunk_states[c]
    st_ref[...] = hs

    @pl.when(seq_last)
    def _():
        hfin_ref[0] = hs.T


def _ssd_out_kernel(*refs, cps, nc, has_h0, nsteps):
    it = iter(refs)
    (xcb_ref, dt_ref, dtt_ref, z_ref, hsf_ref, sb_ref, dec_ref,
     alog_row_ref, alog_col_ref, dskip_ref, gn_ref) = (next(it) for _ in range(11))
    h0_ref = next(it) if has_h0 else None
    y_ref, hfin_ref, st_ref = next(it), next(it), next(it)

    q = SSD_CHUNK
    pos = ((nsteps - 1 - pl.program_id(0)) * nc) % cps
    seq_first = pos == 0
    seq_last = pos + nc == cps
    log2e = math.log2(math.e)

    lower, upper = _chunk_masks(q)
    tril = jnp.where(lower, 1.0, 0.0).astype(BF16)
    triu = jnp.where(upper, 1.0, 0.0).astype(BF16)
    a_row = -jnp.exp(alog_row_ref[...])
    a_col = -jnp.exp(alog_col_ref[...])
    lane = lax.broadcasted_iota(jnp.int32, (q, LANE), 1)
    rowi = lax.broadcasted_iota(jnp.int32, (2 * SUBLANE, q), 0)
    emat = _head_expand_matrix()
    rep = SSD_HEADS // SSD_GROUPS
    gw = SSD_WIDTH // SSD_GROUPS
    neg = jnp.float32(-jnp.inf)

    @pl.when(seq_last)
    def _():
        if has_h0:
            st_ref[...] = h0_ref[0]
        else:
            st_ref[...] = jnp.zeros_like(st_ref)

    def chunk_terms(c):
        sl = slice(c * q, (c + 1) * q)
        xcb = xcb_ref[sl, :]
        xsb = xcb[:, :SSD_WIDTH]
        bmb = xcb[:, SSD_WIDTH:SSD_WIDTH + SSD_GROUPS * SSD_STATE]
        cmb = xcb[:, SSD_WIDTH + SSD_GROUPS * SSD_STATE:]
        dt = dt_ref[sl, :]
        dtt = dtt_ref[:, sl]
        dta = dt * a_row
        dtat = dtt * a_col
        la = _prefix_rows(dta, tril)
        tot = la[q - 1:q, :]
        lcol = jnp.where(lane < SSD_HEADS, la, tot - la + dta)
        lat = sum(_dot(p, triu) for p in _split3(dtat))
        tott = lat[:, q - 1:q]
        lrow = jnp.where(rowi < SSD_HEADS, lat, tott - lat + dtat)
        lcol2 = lcol * log2e
        lrow2 = (lrow - jnp.log(dtt)) * log2e
        ecol = jnp.where(lane < 2 * SSD_HEADS, jnp.exp(lcol), 0.0)
        eexp = _expand_heads(ecol, emat)

        cbs = []
        for grp in range(SSD_GROUPS):
            cg = cmb[:, grp * SSD_STATE:(grp + 1) * SSD_STATE]
            bg = bmb[:, grp * SSD_STATE:(grp + 1) * SSD_STATE]
            cbs.append(_dot_nt(cg, bg))
        tiles = []
        for pair in range(SSD_HEADS // 2):
            xpair = xsb[:, pair * LANE:(pair + 1) * LANE]
            res = []
            for hd in (2 * pair, 2 * pair + 1):
                jf, jb = hd, SSD_HEADS + hd
                ef = jnp.exp2(jnp.where(lower, lcol2[:, jf:jf + 1] - lrow2[jf:jf + 1, :], neg))
                eb = jnp.exp2(jnp.where(upper, lcol2[:, jb:jb + 1] - lrow2[jb:jb + 1, :], neg))
                mm = (cbs[hd // rep] * (ef + eb)).astype(BF16)
                res.append(_dot(mm, xpair))
            tiles.append(jnp.where(lane < SSD_HEAD_DIM, res[0], res[1]))
        y = jnp.concatenate(tiles, axis=1)
        hsf = hsf_ref[c]
        parts = [_dot(cmb[:, grp * SSD_STATE:(grp + 1) * SSD_STATE], hsf[:, grp * gw:(grp + 1) * gw])
                 for grp in range(SSD_GROUPS)]
        y = y + jnp.concatenate(parts, axis=1) * eexp[:, :SSD_WIDTH]
        y = y + dskip_ref[...] * xsb.astype(F32)
        return y, cmb, eexp[:, SSD_WIDTH:], _silu(z_ref[sl, :])

    terms = [chunk_terms(c) for c in range(nc)]

    hb = st_ref[...]
    for c in reversed(range(nc)):
        y, cmb, eexp_b, gate = terms[c]
        hsb = hb.astype(BF16)
        parts = [_dot(cmb[:, grp * SSD_STATE:(grp + 1) * SSD_STATE], hsb[:, grp * gw:(grp + 1) * gw])
                 for grp in range(SSD_GROUPS)]
        y = (y + jnp.concatenate(parts, axis=1) * eexp_b) * gate
        y_ref[c * q:(c + 1) * q, :] = _rms_rows(y, gn_ref[...]).astype(BF16)
        hb = hb * dec_ref[c][0:1, SSD_WIDTH:] + sb_ref[c]
    st_ref[...] = hb

    @pl.when(seq_first)
    def _():
        hfin_ref[0] = hb.T


def _ssd(xbc, dt, dtt, z, lw, cps, h0=None):
    t = xbc.shape[0]
    q = SSD_CHUNK
    nc = min(SSD_STEP_CHUNKS, cps)
    assert cps % nc == 0
    rows = nc * q
    nchunks = t // q
    nsteps = nchunks // nc
    spq = cps // nc
    nseq = nchunks // cps
    hb = rows // SUBLANE
    n8 = t // SUBLANE
    has_h0 = h0 is not None
    const = lambda i: (0, 0)
    st_block = (1, SSD_STATE, SSD_WIDTH)
    ch_block = (nc, SSD_STATE, SSD_WIDTH)
    fin_block = (1, SSD_WIDTH, SSD_STATE)
    dec_block = (nc, SUBLANE, 2 * SSD_WIDTH)

    in_specs = [
        pl.BlockSpec((rows, SSD_XBC), lambda i: (i, 0)),
        pl.BlockSpec((SUBLANE, SSD_XBC), lambda i: (jnp.maximum(i * hb - 1, 0), 0)),
        pl.BlockSpec((SUBLANE, SSD_XBC), lambda i: (jnp.minimum((i + 1) * hb, n8 - 1), 0)),
        pl.BlockSpec((rows, LANE), lambda i: (i, 0)),
        pl.BlockSpec((SSD_CONV, SSD_XBC), const),
        pl.BlockSpec((1, SSD_XBC), const),
        pl.BlockSpec((1, LANE), const),
    ]
    args = [xbc, xbc, xbc, dt, lw["conv_w"], lw["conv_b"], lw["alog_row"]]
    if has_h0:
        in_specs.append(pl.BlockSpec(st_block, lambda i: (i // spq, 0, 0)))
        args.append(h0[0])
    xcb, hsf, sb, dec, hfin_f = pl.pallas_call(
        functools.partial(_ssd_state_kernel, cps=cps, nc=nc, has_h0=has_h0),
        grid=(nsteps,),
        in_specs=in_specs,
        out_specs=[
            pl.BlockSpec((rows, SSD_XBC), lambda i: (i, 0)),
            pl.BlockSpec(ch_block, lambda i: (i, 0, 0)),
            pl.BlockSpec(ch_block, lambda i: (i, 0, 0)),
            pl.BlockSpec(dec_block, lambda i: (i, 0, 0)),
            pl.BlockSpec(fin_block, lambda i: (i // spq, 0, 0)),
        ],
        out_shape=[
            jax.ShapeDtypeStruct((t, SSD_XBC), BF16),
            jax.ShapeDtypeStruct((nchunks, SSD_STATE, SSD_WIDTH), BF16),
            jax.ShapeDtypeStruct((nchunks, SSD_STATE, SSD_WIDTH), F32),
            jax.ShapeDtypeStruct((nchunks, SUBLANE, 2 * SSD_WIDTH), F32),
            jax.ShapeDtypeStruct((nseq, SSD_WIDTH, SSD_STATE), F32),
        ],
        scratch_shapes=[pltpu.VMEM((SSD_STATE, SSD_WIDTH), F32)],
        compiler_params=_cparams("arbitrary"),
        name="ssd_state",
    )(*args)

    gi = lambda i: nsteps - 1 - i
    in_specs = [
        pl.BlockSpec((rows, SSD_XBC), lambda i: (gi(i), 0)),
        pl.BlockSpec((rows, LANE), lambda i: (gi(i), 0)),
        pl.BlockSpec((2 * SSD_HEADS, rows), lambda i: (0, gi(i))),
        pl.BlockSpec((rows, SSD_WIDTH), lambda i: (gi(i), 0)),
        pl.BlockSpec(ch_block, lambda i: (gi(i), 0, 0)),
        pl.BlockSpec(ch_block, lambda i: (gi(i), 0, 0)),
        pl.BlockSpec(dec_block, lambda i: (gi(i), 0, 0)),
        pl.BlockSpec((1, LANE), const),
        pl.BlockSpec((2 * SSD_HEADS, 1), const),
        pl.BlockSpec((1, SSD_WIDTH), const),
        pl.BlockSpec((1, SSD_WIDTH), const),
    ]
    args = [xcb, dt, dtt, z, hsf, sb, dec, lw["alog_row"], lw["alog_col"], lw["dskip_row"], lw["ssd_norm"]]
    if has_h0:
        in_specs.append(pl.BlockSpec(st_block, lambda i: (gi(i) // spq, 0, 0)))
        args.append(h0[1])
    y, hfin_b = pl.pallas_call(
        functools.partial(_ssd_out_kernel, cps=cps, nc=nc, has_h0=has_h0, nsteps=nsteps),
        grid=(nsteps,),
        in_specs=in_specs,
        out_specs=[
            pl.BlockSpec((rows, SSD_WIDTH), lambda i: (gi(i), 0)),
            pl.BlockSpec(fin_block, lambda i: (gi(i) // spq, 0, 0)),
        ],
        out_shape=[
            jax.ShapeDtypeStruct((t, SSD_WIDTH), BF16),
            jax.ShapeDtypeStruct((nseq, SSD_WIDTH, SSD_STATE), F32),
        ],
        scratch_shapes=[pltpu.VMEM((SSD_STATE, SSD_WIDTH), F32)],
        compiler_params=_cparams("arbitrary"),
        name="ssd_out",
    )(*args)
    return y, hfin_f, hfin_b


def _state_to_kernel_layout(h):
    n = h.shape[0]
    return h.transpose(0, 3, 1, 2).reshape(n, SSD_STATE, SSD_WIDTH)


def _state_from_kernel_layout(h):
    return h.reshape(h.shape[0], SSD_HEADS, SSD_HEAD_DIM, SSD_STATE)


def _attn_kernel(*refs, heads, has_cache):
    if has_cache:
        qt_ref, k_ref, vt_ref, kc_ref, vct_ref = refs[:5]
    else:
        qt_ref, k_ref, vt_ref = refs[:3]
    s_refs = refs[-2:]
    o_ref = refs[-3]
    lk = k_ref.shape[0]
    kb = min(ATTN_KEY_BLOCK, lk)
    blocks = [(k_ref, vt_ref, i * kb) for i in range(lk // kb)]
    if has_cache:
        lc = kc_ref.shape[0]
        kbc = min(ATTN_KEY_BLOCK, lc)
        blocks += [(kc_ref, vct_ref, i * kbc) for i in range(lc // kbc)]
        assert kbc == kb
    nblk = len(blocks)
    ones = jnp.ones((ONES_ROWS, kb), BF16)

    def score_block(hd, i, m):
        kr, _, off = blocks[i]
        q = qt_ref[hd * HEAD_PAD:(hd + 1) * HEAD_PAD, :]
        s = _dot(kr[off:off + kb, hd * HEAD_PAD:(hd + 1) * HEAD_PAD], q)
        s_refs[hd % 2][i * kb:(i + 1) * kb, :] = s
        bm = jnp.max(s, axis=0, keepdims=True)
        return bm if m is None else jnp.maximum(m, bm)

    def value_block(hd, i, m, acc):
        _, vr, off = blocks[i]
        p = jnp.exp2((s_refs[hd % 2][i * kb:(i + 1) * kb, :] - m).astype(BF16))
        v = vr[hd * MLA_V_DIM:(hd + 1) * MLA_V_DIM, off:off + kb]
        part = _dot(jnp.concatenate([v, ones], axis=0), p)
        return part if acc is None else acc + part

    m_cur = None
    for i in range(nblk):
        m_cur = score_block(0, i, m_cur)
    for hd in range(heads):
        m_next, acc = None, None
        for i in range(nblk):
            if hd + 1 < heads:
                m_next = score_block(hd + 1, i, m_next)
            acc = value_block(hd, i, m_cur, acc)
        vs = slice(hd * MLA_V_DIM, (hd + 1) * MLA_V_DIM)
        o_ref[vs, :] = acc[:MLA_V_DIM] / acc[MLA_V_DIM:MLA_V_DIM + 1]
        m_cur = m_next


def _attention(qt, k, vt, seq_len, heads_per_step, cache=None):
    t = k.shape[0]
    nseq = t // seq_len
    tq = min(ATTN_Q_TILE, seq_len)
    nq = seq_len // tq
    g = heads_per_step
    in_specs = [
        pl.BlockSpec((g * HEAD_PAD, tq), lambda s, h, j: (h, s * nq + j)),
        pl.BlockSpec((seq_len, g * HEAD_PAD), lambda s, h, j: (s, h)),
        pl.BlockSpec((g * MLA_V_DIM, seq_len), lambda s, h, j: (h, s)),
    ]
    args = [qt, k, vt]
    n_keys = seq_len
    if cache is not None:
        kc, vct = cache
        past = kc.shape[0] // nseq
        n_keys += past
        in_specs += [
            pl.BlockSpec((past, g * HEAD_PAD), lambda s, h, j: (s, h)),
            pl.BlockSpec((g * MLA_V_DIM, past), lambda s, h, j: (h, s)),
        ]
        args += [kc, vct]
    kern = functools.partial(_attn_kernel, heads=g, has_cache=cache is not None)
    return pl.pallas_call(
        kern,
        grid=(nseq, MLA_HEADS // g, nq),
        in_specs=in_specs,
        out_specs=pl.BlockSpec((g * MLA_V_DIM, tq), lambda s, h, j: (h, s * nq + j)),
        out_shape=jax.ShapeDtypeStruct((MLA_WIDTH, t), F32),
        scratch_shapes=[pltpu.VMEM((n_keys, tq), F32), pltpu.VMEM((n_keys, tq), F32)],
        compiler_params=_cparams("arbitrary", "arbitrary", "arbitrary"),
        name="attention",
    )(*args)


def _outproj_kernel(*refs, has_router):
    if has_router:
        (y_ref, ot_ref, x_ref, mod_ref, wa_ref, wb_ref, gm_ref, gpost_ref, gpre2_ref, rt_ref,
         x1_ref, h2_ref, comb_ref) = refs
    else:
        (y_ref, ot_ref, x_ref, mod_ref, wa_ref, wb_ref, gm_ref, gpost_ref, gpre2_ref,
         x1_ref, h2_ref) = refs
    mod = mod_ref[0]
    gate1 = mod[:, 2 * D_MODEL:3 * D_MODEL]
    shift2 = mod[:, 3 * D_MODEL:4 * D_MODEL]
    scale2 = mod[:, 4 * D_MODEL:5 * D_MODEL]
    ot = ot_ref[...]
    ms = jnp.mean(ot * ot, axis=0, keepdims=True)
    on = (ot * lax.rsqrt(ms + EPS) * gm_ref[...]).astype(BF16)
    y = _dot(y_ref[...], wa_ref[...]) + _dot_tn(on, wb_ref[...])
    x1 = x_ref[...] + gate1 * _rms_rows(y, gpost_ref[...])
    x1_ref[...] = x1
    h2 = _rms_rows(x1, gpre2_ref[...]) * (1.0 + scale2) + shift2
    h2_ref[...] = h2.astype(BF16)
    if has_router:
        hh, hm, _ = _split3(h2)
        rh, rm, _ = _split3(rt_ref[...])
        logits = _dot(hh, rh) + (_dot(hm, rh) + _dot(hh, rm))
        lane = lax.broadcasted_iota(jnp.int32, logits.shape, 1).astype(F32)
        neg = jnp.float32(-jnp.inf)
        lg = jnp.where(lane < N_EXPERTS, logits, neg)
        m1 = jnp.max(lg, axis=-1, keepdims=True)
        i1 = jnp.min(jnp.where(lg == m1, lane, float(LANE)), axis=-1, keepdims=True)
        lg2 = jnp.where(lane == i1, neg, lg)
        m2 = jnp.max(lg2, axis=-1, keepdims=True)
        i2 = jnp.min(jnp.where(lg2 == m2, lane, float(LANE)), axis=-1, keepdims=True)
        e2 = jnp.exp(m2 - m1)
        w1 = 1.0 / (1.0 + e2)
        w2 = e2 / (1.0 + e2)
        comb_ref[...] = jnp.where(lane == i1, w1, 0.0) + jnp.where(lane == i2, w2, 0.0)


def _outproj(yssd, ot, x, mod3, mod_row_fn, lw, router=None):
    t = x.shape[0]
    tm = ROW_TILE
    const = lambda i: (0, 0)
    row = lambda i: (i, 0)
    in_specs = [
        pl.BlockSpec((tm, SSD_WIDTH), row),
        pl.BlockSpec((MLA_WIDTH, tm), lambda i: (0, i)),
        pl.BlockSpec((tm, D_MODEL), row),
        pl.BlockSpec((1, 1, N_MOD * D_MODEL), lambda i: (mod_row_fn(ROW_TILE)(i), 0, 0)),
        pl.BlockSpec((SSD_WIDTH, D_MODEL), const),
        pl.BlockSpec((MLA_WIDTH, D_MODEL), const),
        pl.BlockSpec((MLA_WIDTH, 1), const),
        pl.BlockSpec((1, D_MODEL), const),
        pl.BlockSpec((1, D_MODEL), const),
    ]
    args = [yssd, ot, x, mod3, lw["w_out_a"], lw["w_out_b"], lw["mla_norm_col"],
            lw["g_post1"], lw["g_pre2"]]
    out_specs = [pl.BlockSpec((tm, D_MODEL), row), pl.BlockSpec((tm, D_MODEL), row)]
    out_shape = [jax.ShapeDtypeStruct((t, D_MODEL), F32), jax.ShapeDtypeStruct((t, D_MODEL), BF16)]
    if router is not None:
        in_specs.append(pl.BlockSpec((D_MODEL, LANE), const))
        args.append(router)
        out_specs.append(pl.BlockSpec((tm, LANE), row))
        out_shape.append(jax.ShapeDtypeStruct((t, LANE), F32))
    return pl.pallas_call(
        functools.partial(_outproj_kernel, has_router=router is not None),
        grid=(t // tm,),
        in_specs=in_specs,
        out_specs=out_specs,
        out_shape=out_shape,
        compiler_params=_cparams("arbitrary"),
        name="outproj",
    )(*args)


def _ffn_kernel(*refs, has_comb, nslab):
    if has_comb:
        h_ref, x_ref, comb_ref, mod_ref, wgu_ref, wd_ref, gpost_ref, o_ref, acc_ref = refs
    else:
        h_ref, x_ref, mod_ref, wgu_ref, wd_ref, gpost_ref, o_ref, acc_ref = refs
    e = pl.program_id(1)
    h = h_ref[...]
    f = wd_ref.shape[1]
    gu = _dot(h, wgu_ref[0])
    hid = _silu(gu[:, :f]) * gu[:, f:]
    if has_comb:
        comb = comb_ref[...]
        lane = lax.broadcasted_iota(jnp.int32, comb.shape, 1)
        wcol = jnp.sum(jnp.where(lane == e, comb, 0.0), axis=-1, keepdims=True)
        hid = hid * wcol
    part = _dot(hid.astype(BF16), wd_ref[0])

    @pl.when(e == 0)
    def _():
        acc_ref[...] = part

    @pl.when(e > 0)
    def _():
        acc_ref[...] += part

    @pl.when(e == nslab - 1)
    def _():
        gate2 = mod_ref[0][:, 5 * D_MODEL:6 * D_MODEL]
        o_ref[...] = x_ref[...] + gate2 * _rms_rows(acc_ref[...], gpost_ref[...])


def _ffn(h2, x1, mod3, mod_row_fn, wgu, wd, gpost, comb=None):
    t = x1.shape[0]
    tm = FFN_ROW_TILE
    nslab, f, _ = wd.shape
    row = lambda i, e: (i, 0)
    in_specs = [pl.BlockSpec((tm, D_MODEL), row), pl.BlockSpec((tm, D_MODEL), row)]
    args = [h2, x1]
    if comb is not None:
        in_specs.append(pl.BlockSpec((tm, LANE), row))
        args.append(comb)
    in_specs += [
        pl.BlockSpec((1, 1, N_MOD * D_MODEL), lambda i, e: (mod_row_fn(FFN_ROW_TILE)(i), 0, 0)),
        pl.BlockSpec((1, D_MODEL, 2 * f), lambda i, e: (e, 0, 0)),
        pl.BlockSpec((1, f, D_MODEL), lambda i, e: (e, 0, 0)),
        pl.BlockSpec((1, D_MODEL), lambda i, e: (0, 0)),
    ]
    args += [mod3, wgu, wd, gpost]
    return pl.pallas_call(
        functools.partial(_ffn_kernel, has_comb=comb is not None, nslab=nslab),
        grid=(t // tm, nslab),
        in_specs=in_specs,
        out_specs=pl.BlockSpec((tm, D_MODEL), row),
        out_shape=jax.ShapeDtypeStruct((t, D_MODEL), F32),
        scratch_shapes=[pltpu.VMEM((tm, D_MODEL), F32)],
        compiler_params=_cparams("arbitrary", "arbitrary"),
        name="ffn",
    )(*args)


def _moe_kernel(h_ref, x_ref, comb_ref, mod_ref, wgu_ref, wd_ref, gpost_ref, o_ref,
                acc_ref, rank_ref, rank_t_ref, comb_t_ref, *, nexp, sub):
    e = pl.program_id(1)
    tm = h_ref.shape[0]
    cap = MOE_CAP
    f = wd_ref.shape[1]
    capl = -(-cap // LANE) * LANE

    @pl.when(e == 0)
    def _():
        acc_ref[...] = jnp.zeros_like(acc_ref)
        r_i = lax.broadcasted_iota(jnp.int32, (sub, sub), 0)
        c_i = lax.broadcasted_iota(jnp.int32, (sub, sub), 1)
        strict = jnp.where(r_i > c_i, 1.0, 0.0).astype(BF16)
        for s in range(tm // sub):
            rows = slice(s * sub, (s + 1) * sub)
            comb = comb_ref[rows, :]
            rank = _dot(strict, jnp.where(comb > 0.0, 1.0, 0.0).astype(BF16))
            rank_ref[rows, :] = rank
            rank_t_ref[:, rows] = rank.T
            comb_t_ref[:, rows] = comb.T

    lane = lax.broadcasted_iota(jnp.int32, (sub, LANE), 1)
    slot_l = lax.broadcasted_iota(jnp.int32, (sub, capl), 1).astype(F32)
    slot_s = lax.broadcasted_iota(jnp.int32, (cap, sub), 0).astype(F32)
    for s in range(tm // sub):
        rows = slice(s * sub, (s + 1) * sub)
        pick = lane == e
        wcol = jnp.sum(jnp.where(pick, comb_ref[rows, :], 0.0), axis=-1, keepdims=True)
        rcol = jnp.sum(jnp.where(pick, rank_ref[rows, :], 0.0), axis=-1, keepdims=True)
        wrow = comb_t_ref[pl.ds(e, 1), rows]
        rrow = rank_t_ref[pl.ds(e, 1), rows]
        count = jnp.max(jnp.where(wrow > 0.0, rrow + 1.0, 0.0))
        nchunk = ((count + (cap - 1.0)) * (1.0 / cap)).astype(jnp.int32)
        hsub = h_ref[rows, :]

        def chunk(c, carry, rows=rows, wcol=wcol, rcol=rcol, wrow=wrow, rrow=rrow, hsub=hsub):
            base = (c * cap).astype(F32)
            gather = jnp.where(((rrow - base) == slot_s) & (wrow > 0.0), 1.0, 0.0).astype(BF16)
            scatter = jnp.where(((rcol - base) == slot_l) & (wcol > 0.0) & (slot_l < float(cap)),
                                1.0, 0.0).astype(BF16)
            xg = _dot(gather, hsub).astype(BF16)
            gu = _dot(xg, wgu_ref[0])
            hid = _silu(gu[:, :f]) * gu[:, f:]
            y = _dot(hid.astype(BF16), wd_ref[0]).astype(BF16)
            if capl > cap:
                y = jnp.concatenate([y, jnp.zeros((capl - cap, y.shape[1]), BF16)], axis=0)
            acc_ref[rows, :] += wcol * _dot(scatter, y)
            return carry

        lax.fori_loop(0, nchunk, chunk, 0)

    @pl.when(e == nexp - 1)
    def _():
        gate2 = mod_ref[0][:, 5 * D_MODEL:6 * D_MODEL]
        o_ref[...] = x_ref[...] + gate2 * _rms_rows(acc_ref[...], gpost_ref[...])


def _moe(h2, x1, comb, mod3, mod_row_fn, wgu, wd, gpost):
    t = x1.shape[0]
    tm = min(MOE_ROW_TILE, t)
    sub = min(MOE_SUB_TILE, tm)
    nexp, f, _ = wd.shape
    row = lambda i, e: (i, 0)
    return pl.pallas_call(
        functools.partial(_moe_kernel, nexp=nexp, sub=sub),
        grid=(t // tm, nexp),
        in_specs=[
            pl.BlockSpec((tm, D_MODEL), row),
            pl.BlockSpec((tm, D_MODEL), row),
            pl.BlockSpec((tm, LANE), row),
            pl.BlockSpec((1, 1, N_MOD * D_MODEL), lambda i, e: (mod_row_fn(tm)(i), 0, 0)),
            pl.BlockSpec((1, D_MODEL, 2 * f), lambda i, e: (e, 0, 0)),
            pl.BlockSpec((1, f, D_MODEL), lambda i, e: (e, 0, 0)),
            pl.BlockSpec((1, D_MODEL), lambda i, e: (0, 0)),
        ],
        out_specs=pl.BlockSpec((tm, D_MODEL), row),
        out_shape=jax.ShapeDtypeStruct((t, D_MODEL), F32),
        scratch_shapes=[pltpu.VMEM((tm, D_MODEL), F32), pltpu.VMEM((tm, LANE), F32),
                        pltpu.VMEM((LANE, tm), F32), pltpu.VMEM((LANE, tm), F32)],
        compiler_params=_cparams("arbitrary", "arbitrary"),
        name="moe",
    )(h2, x1, comb, mod3, wgu, wd, gpost)


_ROT_PERM = tuple(list(range(8, 16)) + list(range(0, 8)) + list(range(24, 32)) + list(range(16, 24)))
_ROT_SIGN = tuple([-1.0] * 8 + [1.0] * 8 + [-1.0] * 8 + [1.0] * 8)


def _rot_cols(w):
    return w[..., jnp.array(_ROT_PERM)] * jnp.array(_ROT_SIGN, F32)


def _prep_layer(i, p):
    w_in = p["w_in"][i]
    s1 = SSD_WIDTH
    s2 = s1 + SSD_XBC
    s3 = s2 + 2 * SSD_HEADS
    s4 = s3 + MLA_Q_RANK
    s5 = s4 + MLA_KV_RANK
    w_z, w_xbc, w_dt, w_cq, w_ckv, w_kr = (w_in[:, :s1], w_in[:, s1:s2], w_in[:, s2:s3],
                                             w_in[:, s3:s4], w_in[:, s4:s5], w_in[:, s5:])
    zc = lambda n: jnp.zeros((D_MODEL, n), F32)
    tile_a = jnp.concatenate([w_dt, zc(ROPE_LANE0 - 2 * SSD_HEADS), w_kr,
                              zc(LANE - ROPE_LANE0 - MLA_ROPE_DIM)], axis=1)
    tile_b = jnp.concatenate([zc(ROPE_LANE0), _rot_cols(w_kr),
                              zc(LANE - ROPE_LANE0 - MLA_ROPE_DIM)], axis=1)
    w_in_pad = jnp.concatenate([w_z, w_xbc, w_cq, w_ckv, tile_a, tile_b], axis=1).astype(BF16)

    w_uq = p["w_uq"][i].reshape(MLA_Q_RANK, MLA_HEADS, MLA_NOPE_DIM + MLA_ROPE_DIM)
    q_nope, q_rope = w_uq[..., :MLA_NOPE_DIM], w_uq[..., MLA_NOPE_DIM:]
    zq = lambda n: jnp.zeros((MLA_Q_RANK, MLA_HEADS, n), F32)
    pad = HEAD_PAD - MLA_NOPE_DIM - MLA_ROPE_DIM
    w_q = jnp.concatenate([q_nope, q_rope, zq(pad)], axis=-1).reshape(MLA_Q_RANK, -1)
    w_qrot = jnp.concatenate([zq(MLA_NOPE_DIM), _rot_cols(q_rope), zq(pad)], axis=-1).reshape(MLA_Q_RANK, -1)

    w_ukv = p["w_ukv"][i].reshape(MLA_KV_RANK, MLA_HEADS, MLA_NOPE_DIM + MLA_V_DIM)
    k_nope, v_w = w_ukv[..., :MLA_NOPE_DIM], w_ukv[..., MLA_NOPE_DIM:]
    w_uk = jnp.concatenate([k_nope, jnp.zeros((MLA_KV_RANK, MLA_HEADS, HEAD_PAD - MLA_NOPE_DIM), F32)],
                           axis=-1).reshape(MLA_KV_RANK, -1)
    w_uv = v_w.reshape(MLA_KV_RANK, MLA_WIDTH)

    dtb = p["dt_bias"][i].reshape(2 * SSD_HEADS)
    alog = p["a_log"][i].reshape(2 * SSD_HEADS)
    padl = lambda v: jnp.pad(v, (0, LANE - v.shape[0])).reshape(1, LANE)
    w_out = p["w_out"][i]
    return {
        "g_pre1": p["norm_pre_mix"][i].reshape(1, D_MODEL),
        "g_post1": p["norm_post_mix"][i].reshape(1, D_MODEL),
        "g_pre2": p["norm_pre_ffn"][i].reshape(1, D_MODEL),
        "g_post2": p["norm_post_ffn"][i].reshape(1, D_MODEL),
        "w_in": w_in_pad,
        "w_dt": w_dt.T.astype(BF16),
        "dtb_row": padl(dtb),
        "dtb_col": dtb.reshape(-1, 1),
        "alog_row": padl(alog),
        "alog_col": alog.reshape(-1, 1),
        "q_norm": p["q_norm"][i].reshape(1, -1),
        "w_q": w_q.T.astype(BF16),
        "w_qrot": w_qrot.T.astype(BF16),
        "kv_norm": p["kv_norm"][i].reshape(1, -1),
        "w_uk": w_uk.astype(BF16),
        "w_uv": w_uv.T.astype(BF16),
        "conv_w": p["conv_w"][i],
        "conv_b": p["conv_b"][i].reshape(1, -1),
        "dskip_row": jnp.repeat(p["d_skip"][i], SSD_HEAD_DIM).reshape(1, -1),
        "ssd_norm": p["ssd_norm"][i].reshape(1, -1),
        "mla_norm_col": p["mla_norm"][i].reshape(-1, 1),
        "w_out_a": w_out[:SSD_WIDTH].astype(BF16),
        "w_out_b": w_out[SSD_WIDTH:].astype(BF16),
    }


def _rope_tables(n_tokens):
    rows = n_tokens // GRID_W
    row = np.repeat(np.arange(rows, dtype=np.float32), GRID_W)
    col = np.tile(np.arange(GRID_W, dtype=np.float32), rows)
    half = MLA_ROPE_DIM // 2
    inv = (np.float32(ROPE_THETA) ** (-np.arange(0, half, 2, dtype=np.float32) / np.float32(half))).astype(np.float32)
    ar = row[:, None] * inv[None, :]
    ac = col[:, None] * inv[None, :]
    ang = np.concatenate([ar, ar, ac, ac], axis=-1).astype(np.float32)
    return jnp.asarray(np.cos(ang), F32), jnp.asarray(np.sin(ang), F32)


def _attn_tables(cos, sin, n):
    scale = (MLA_NOPE_DIM + MLA_ROPE_DIM) ** -0.5 * math.log2(math.e)
    pad = HEAD_PAD - ROPE_LANE0 - MLA_ROPE_DIM
    cosk = jnp.concatenate([jnp.zeros((n, ROPE_LANE0), F32), cos, jnp.zeros((n, pad), F32)], axis=1)
    sink = jnp.concatenate([jnp.zeros((n, ROPE_LANE0), F32), sin, jnp.zeros((n, pad), F32)], axis=1)
    cosq = jnp.concatenate([jnp.ones((n, ROPE_LANE0), F32), cos, jnp.zeros((n, pad), F32)], axis=1)
    return (cosq * scale).T, (sink * scale).T, cosk, sink


def kernel(x_prompt, x_sample, cache_ckv, cache_krope, state_ssm, c, c_ctx, w_mod, b_mod, norm_pre_mix, norm_post_mix, norm_pre_ffn, norm_post_ffn, w_in, conv_w, conv_b, dt_bias, a_log, d_skip, ssd_norm, q_norm, w_uq, kv_norm, w_ukv, mla_norm, w_out, ffn_w_gate, ffn_w_up, ffn_w_down, moe_router, moe_w_gate, moe_w_up, moe_w_down):
    params = dict(w_in=w_in, conv_w=conv_w, conv_b=conv_b, dt_bias=dt_bias, a_log=a_log, d_skip=d_skip,
                  ssd_norm=ssd_norm, q_norm=q_norm, w_uq=w_uq, kv_norm=kv_norm, w_ukv=w_ukv,
                  mla_norm=mla_norm, w_out=w_out, norm_pre_mix=norm_pre_mix, norm_post_mix=norm_post_mix,
                  norm_pre_ffn=norm_pre_ffn, norm_post_ffn=norm_post_ffn)
    batch, seq, d = x_prompt.shape
    dec_batch, dec_seq, _ = x_sample.shape
    depth = w_in.shape[0]
    past = cache_ckv.shape[2]
    tm = ROW_TILE

    cvec = jnp.concatenate([c_ctx[None, :], c, jnp.zeros((SUBLANE - 1 - dec_batch, d), F32)], axis=0)
    mod = _modulation(cvec, w_mod, b_mod)

    ones = jnp.ones((tm, MLA_ROPE_DIM), F32)
    tabs_ctx = _attn_tables(ones, jnp.zeros_like(ones), tm)
    cos, sin = _rope_tables(dec_seq)
    tabs_lat = _attn_tables(cos, sin, dec_seq)
    lat_blocks = dec_seq // tm

    xp = x_prompt.reshape(batch * seq, d)
    xs = x_sample.reshape(dec_batch * dec_seq, d)
    new_ckv, new_kr, new_ssm = [], [], []
    for i in range(depth):
        lw = _prep_layer(i, params)
        mod3 = mod[i].reshape(SUBLANE, 1, N_MOD * d)
        j = i // 2
        if i % 2 == 0:
            f = ffn_w_gate.shape[2] // 2
            wgu = jnp.stack([jnp.concatenate([ffn_w_gate[j][:, s * f:(s + 1) * f], ffn_w_up[j][:, s * f:(s + 1) * f]],
                                             axis=1) for s in range(2)], axis=0).astype(BF16)
            wd = ffn_w_down[j].reshape(2, f, d).astype(BF16)
            router = None
        else:
            wgu = jnp.concatenate([moe_w_gate[j], moe_w_up[j]], axis=-1).astype(BF16)
            wd = moe_w_down[j].astype(BF16)
            router = jnp.pad(moe_router[j], ((0, 0), (0, LANE - N_EXPERTS)))

        def run(x, row_fn, tabs, tab_blocks, seq_len, heads_per_step, ctx):
            z, xbc, dt, dtt, qt, k, vt, ckvn, kr = _inproj(x, mod3, row_fn, lw, tabs, tab_blocks)
            cps = seq_len // SSD_CHUNK
            h0 = None
            if ctx is not None:
                h0 = (_state_to_kernel_layout(ctx[2][:, 0]), _state_to_kernel_layout(ctx[2][:, 1]))
            yssd, hf, hb = _ssd(xbc, dt, dtt, z, lw, cps, h0=h0)
            hf, hb = _state_from_kernel_layout(hf), _state_from_kernel_layout(hb)
            cache = None
            if ctx is not None:
                kr_tile = jnp.pad(ctx[1].reshape(-1, MLA_ROPE_DIM),
                                  ((0, 0), (ROPE_LANE0, HEAD_PAD - ROPE_LANE0 - MLA_ROPE_DIM)))
                cache = _kvcache(ctx[0].reshape(-1, MLA_KV_RANK), kr_tile, lw)
            ot = _attention(qt, k, vt, seq_len, heads_per_step, cache=cache)
            outs = _outproj(yssd, ot, x, mod3, row_fn, lw, router=router)
            x1, h2 = outs[0], outs[1]
            comb = outs[2] if router is not None else None
            if comb is None:
                x2 = _ffn(h2, x1, mod3, row_fn, wgu, wd, lw["g_post2"])
            else:
                x2 = _moe(h2, x1, comb, mod3, row_fn, wgu, wd, lw["g_post2"])
            return x2, ckvn, kr, hf, hb

        xp, ckvn, kr, hf, hb = run(xp, lambda tile: (lambda b: 0), tabs_ctx, 1, seq, MLA_HEADS, None)
        new_ckv.append(ckvn.reshape(batch, seq, MLA_KV_RANK))
        new_kr.append(kr.reshape(batch, seq, MLA_ROPE_DIM))
        new_ssm.append(jnp.stack([hf, hb], axis=1))
        xs, _, _, _, _ = run(xs, lambda tile: (lambda b: 1 + (b * tile) // dec_seq), tabs_lat, lat_blocks, dec_seq, MLA_HEADS,
                             (cache_ckv[:, i], cache_krope[:, i], state_ssm[:, i]))
    return (xp.reshape(batch, seq, d), xs.reshape(dec_batch, dec_seq, d),
            jnp.stack(new_ckv, axis=1), jnp.stack(new_kr, axis=1), jnp.stack(new_ssm, axis=1))
```

```python
import functools
import math

import jax
import jax.numpy as jnp
import numpy as np
from jax import lax
from jax.experimental import pallas as pl
from jax.experimental.pallas import tpu as pltpu

F32 = jnp.float32
BF16 = jnp.bfloat16

D_MODEL = 1024
GRID_W = 64
SSD_WIDTH = 512
SSD_HEAD_DIM = 64
SSD_HEADS = 8
SSD_GROUPS = 2
SSD_STATE = 64
SSD_CONV = 5
SSD_CHUNK = 128
SSD_STEP_CHUNKS = 4
SSD_XBC = SSD_WIDTH + 2 * SSD_GROUPS * SSD_STATE
MLA_WIDTH = 512
MLA_V_DIM = 64
MLA_HEADS = 8
MLA_NOPE_DIM = 64
MLA_ROPE_DIM = 32
MLA_Q_RANK = 384
MLA_KV_RANK = 256
ROPE_THETA = 10000.0
N_EXPERTS = 8
N_MOD = 6
EPS = 1e-6

LANE = 128
SUBLANE = 8
HEAD_PAD = 128
ONES_ROWS = 16
ROPE_LANE0 = MLA_NOPE_DIM
C_Z = 0
C_XBC = C_Z + SSD_WIDTH
C_CQ = C_XBC + SSD_XBC
C_CKV = C_CQ + MLA_Q_RANK
C_TA = C_CKV + MLA_KV_RANK
IN_PAD = C_TA + LANE
ROT_GROUP = MLA_ROPE_DIM // 4

VMEM_LIMIT = 56 * 1024 * 1024

ROW_TILE = 512
ATTN_Q_TILE = 256
ATTN_KEY_BLOCK = 512
FFN_ROW_TILE = 512
MOE_ROW_TILE = 1024
MOE_SUB_TILE = 512
MOE_CAP = 160

NT_DIMS = (((1,), (1,)), ((), ()))
TN_DIMS = (((0,), (0,)), ((), ()))


def _cparams(*sem):
    return pltpu.CompilerParams(dimension_semantics=sem, vmem_limit_bytes=VMEM_LIMIT)


def _silu(x):
    return x / (1.0 + jnp.exp(-x))


def _softplus(x):
    return jnp.maximum(x, 0.0) + jnp.log(1.0 + jnp.exp(-jnp.abs(x)))


def _rms_rows(x, g):
    ms = jnp.mean(x * x, axis=-1, keepdims=True)
    return x * lax.rsqrt(ms + EPS) * g


def _dot(a, b):
    return jnp.dot(a, b, preferred_element_type=F32)


def _dot_nt(a, b):
    return lax.dot_general(a, b, NT_DIMS, preferred_element_type=F32)


def _dot_tn(a, b):
    return lax.dot_general(a, b, TN_DIMS, preferred_element_type=F32)


def _split3(x):
    hi = x.astype(BF16)
    r1 = x - hi.astype(F32)
    mid = r1.astype(BF16)
    lo = (r1 - mid.astype(F32)).astype(BF16)
    return hi, mid, lo


def _mod_kernel(c_ref, w_ref, b_ref, o_ref):
    s = _silu(c_ref[...]).astype(BF16)
    o_ref[0] = _dot(s, w_ref[0].astype(BF16)) + b_ref[0]


def _modulation(cvec, w_mod, b_mod):
    depth, d, n = w_mod.shape
    tn = 1536
    return pl.pallas_call(
        _mod_kernel,
        grid=(depth, n // tn),
        in_specs=[
            pl.BlockSpec((SUBLANE, d), lambda l, j: (0, 0)),
            pl.BlockSpec((1, d, tn), lambda l, j: (l, 0, j)),
            pl.BlockSpec((1, 1, tn), lambda l, j: (l, 0, j)),
        ],
        out_specs=pl.BlockSpec((1, SUBLANE, tn), lambda l, j: (l, 0, j)),
        out_shape=jax.ShapeDtypeStruct((depth, SUBLANE, n), F32),
        compiler_params=_cparams("arbitrary", "arbitrary"),
        name="modulation",
    )(cvec, w_mod, b_mod.reshape(depth, 1, n))


def _inproj_kernel(*refs, emit_cache, n_alias):
    (x_ref, mod_ref, gpre_ref, win_ref, wdt_ref, dtb_row_ref, dtb_col_ref, qn_ref, wq_ref, kvn_ref, wuk_ref,
     wuv_ref, cosq_ref, sinq_ref, cosk_ref, sink_ref) = refs[:16]
    outs = refs[16 + n_alias:]
    z_ref, xbc_ref, dt_ref, dtt_ref, qt_ref, k_ref, vt_ref = outs[:7]
    mod = mod_ref[0]
    shift = mod[:, 0:D_MODEL]
    scale = mod[:, D_MODEL:2 * D_MODEL]
    h = _rms_rows(x_ref[...], gpre_ref[...]) * (1.0 + scale) + shift
    hb = h.astype(BF16)
    proj = _dot(hb, win_ref[...])
    z_ref[...] = proj[:, C_Z:C_XBC]
    xbc_ref[...] = proj[:, C_XBC:C_CQ]
    cqn = _rms_rows(proj[:, C_CQ:C_CKV], qn_ref[...]).astype(BF16)
    ckvn = _rms_rows(proj[:, C_CKV:C_TA], kvn_ref[...])
    if emit_cache:
        ckvn_ref, kr_ref = outs[7:]
        nb, _, sq, _ = ckvn_ref.shape
        ckvn_ref[:, 0] = ckvn.reshape(nb, sq, MLA_KV_RANK)
    ckvb = ckvn.astype(BF16)
    ta = proj[:, C_TA:IN_PAD]
    dt_ref[...] = _softplus(ta + dtb_row_ref[...])
    if emit_cache:
        kr_ref[:, 0] = ta[:, ROPE_LANE0:ROPE_LANE0 + MLA_ROPE_DIM].reshape(nb, sq, MLA_ROPE_DIM)
    lane = lax.broadcasted_iota(jnp.int32, ta.shape, 1)
    first = (lane // ROT_GROUP) % 2 == 0
    rot = jnp.where(first, -pltpu.roll(ta, LANE - ROT_GROUP, 1), pltpu.roll(ta, ROT_GROUP, 1))
    kr_rot = ta * cosk_ref[...] + rot * sink_ref[...]
    knp = _dot(ckvb, wuk_ref[...])
    for hd in range(MLA_HEADS):
        sl = slice(hd * HEAD_PAD, (hd + 1) * HEAD_PAD)
        k_ref[:, sl] = (knp[:, sl] + kr_rot).astype(BF16)
    vt_ref[...] = _dot_nt(wuv_ref[...], ckvb).astype(BF16)
    qt = _dot_nt(wq_ref[...], cqn)
    cosq = cosq_ref[...]
    sinq = sinq_ref[...]
    g, r0 = ROT_GROUP, ROPE_LANE0
    for hd in range(MLA_HEADS):
        blk = qt[hd * HEAD_PAD:(hd + 1) * HEAD_PAD, :]
        rot = jnp.concatenate([blk[:r0], -blk[r0 + g:r0 + 2 * g], blk[r0:r0 + g], -blk[r0 + 3 * g:r0 + 4 * g],
                               blk[r0 + 2 * g:r0 + 3 * g], blk[r0 + 4 * g:]], axis=0)
        qt_ref[hd * HEAD_PAD:(hd + 1) * HEAD_PAD, :] = (blk * cosq + rot * sinq).astype(BF16)
    dtt_ref[...] = _softplus(_dot_nt(wdt_ref[...], hb) + dtb_col_ref[...])


def _inproj(x, mod3, mod_row_fn, lw, tabs, tab_blocks, cache_out=None):
    t = x.shape[0]
    tm = ROW_TILE
    nb = t // tm
    cosq, sinq, cosk, sink = tabs
    ntab = tab_blocks
    const = lambda i: (0, 0)
    row = lambda i: (i, 0)
    col = lambda i: (0, i)
    in_specs = [
        pl.BlockSpec((tm, D_MODEL), row),
        pl.BlockSpec((1, 1, N_MOD * D_MODEL), lambda i: (mod_row_fn(ROW_TILE)(i), 0, 0)),
        pl.BlockSpec((1, D_MODEL), const),
        pl.BlockSpec((D_MODEL, IN_PAD), const),
        pl.BlockSpec((2 * SSD_HEADS, D_MODEL), const),
        pl.BlockSpec((1, LANE), const),
        pl.BlockSpec((2 * SSD_HEADS, 1), const),
        pl.BlockSpec((1, MLA_Q_RANK), const),
        pl.BlockSpec((MLA_HEADS * HEAD_PAD, MLA_Q_RANK), const),
        pl.BlockSpec((1, MLA_KV_RANK), const),
        pl.BlockSpec((MLA_KV_RANK, MLA_HEADS * HEAD_PAD), const),
        pl.BlockSpec((MLA_WIDTH, MLA_KV_RANK), const),
        pl.BlockSpec((HEAD_PAD, tm), lambda i: (0, i % ntab)),
        pl.BlockSpec((HEAD_PAD, tm), lambda i: (0, i % ntab)),
        pl.BlockSpec((tm, LANE), lambda i: (i % ntab, 0)),
        pl.BlockSpec((tm, LANE), lambda i: (i % ntab, 0)),
    ]
    out_specs = [
        pl.BlockSpec((tm, SSD_WIDTH), row),
        pl.BlockSpec((tm, SSD_XBC), row),
        pl.BlockSpec((tm, LANE), row),
        pl.BlockSpec((2 * SSD_HEADS, tm), col),
        pl.BlockSpec((MLA_HEADS * HEAD_PAD, tm), col),
        pl.BlockSpec((tm, MLA_HEADS * HEAD_PAD), row),
        pl.BlockSpec((MLA_WIDTH, tm), col),
    ]
    out_shape = [
        jax.ShapeDtypeStruct((t, SSD_WIDTH), F32),
        jax.ShapeDtypeStruct((t, SSD_XBC), F32),
        jax.ShapeDtypeStruct((t, LANE), F32),
        jax.ShapeDtypeStruct((2 * SSD_HEADS, t), F32),
        jax.ShapeDtypeStruct((MLA_HEADS * HEAD_PAD, t), BF16),
        jax.ShapeDtypeStruct((t, MLA_HEADS * HEAD_PAD), BF16),
        jax.ShapeDtypeStruct((MLA_WIDTH, t), BF16),
    ]
    args = [x, mod3, lw["g_pre1"], lw["w_in"], lw["w_dt"], lw["dtb_row"], lw["dtb_col"],
            lw["q_norm"], lw["w_q"], lw["kv_norm"], lw["w_uk"], lw["w_uv"], cosq, sinq, cosk, sink]
    aliases = {}
    n_alias = 0
    if cache_out is not None:
        layer, depth, seq, bufs = cache_out
        nb_seq = tm // seq
        for rank in (MLA_KV_RANK, MLA_ROPE_DIM):
            out_specs.append(pl.BlockSpec((nb_seq, 1, seq, rank), lambda i: (i, layer, 0, 0)))
            out_shape.append(jax.ShapeDtypeStruct((t // seq, depth, seq, rank), F32))
        if bufs is not None:
            n_alias = len(bufs)
            for j, buf in enumerate(bufs):
                aliases[len(args)] = len(out_shape) - n_alias + j
                in_specs.append(pl.BlockSpec(memory_space=pl.ANY))
                args.append(buf)
    return pl.pallas_call(
        functools.partial(_inproj_kernel, emit_cache=cache_out is not None, n_alias=n_alias),
        grid=(nb,),
        in_specs=in_specs,
        out_specs=out_specs,
        out_shape=out_shape,
        input_output_aliases=aliases,
        compiler_params=_cparams("arbitrary"),
        name="inproj",
    )(*args)


def _kvcache_kernel(ckv_ref, kr_ref, wuk_ref, wuv_ref, k_ref, vt_ref):
    ckvb = ckv_ref[...].astype(BF16)
    knp = _dot(ckvb, wuk_ref[...])
    kr = kr_ref[...]
    for hd in range(MLA_HEADS):
        sl = slice(hd * HEAD_PAD, (hd + 1) * HEAD_PAD)
        k_ref[:, sl] = (knp[:, sl] + kr).astype(BF16)
    vt_ref[...] = _dot_nt(wuv_ref[...], ckvb).astype(BF16)


def _kvcache(ckv, kr_tile, lw):
    n = ckv.shape[0]
    tm = 512
    return pl.pallas_call(
        _kvcache_kernel,
        grid=(n // tm,),
        in_specs=[
            pl.BlockSpec((tm, MLA_KV_RANK), lambda i: (i, 0)),
            pl.BlockSpec((tm, LANE), lambda i: (i, 0)),
            pl.BlockSpec((MLA_KV_RANK, MLA_HEADS * HEAD_PAD), lambda i: (0, 0)),
            pl.BlockSpec((MLA_WIDTH, MLA_KV_RANK), lambda i: (0, 0)),
        ],
        out_specs=[
            pl.BlockSpec((tm, MLA_HEADS * HEAD_PAD), lambda i: (i, 0)),
            pl.BlockSpec((MLA_WIDTH, tm), lambda i: (0, i)),
        ],
        out_shape=[
            jax.ShapeDtypeStruct((n, MLA_HEADS * HEAD_PAD), BF16),
            jax.ShapeDtypeStruct((MLA_WIDTH, n), BF16),
        ],
        compiler_params=_cparams("arbitrary"),
        name="kvcache",
    )(ckv, kr_tile, lw["w_uk"], lw["w_uv"])


def _head_expand_matrix():
    r = lax.broadcasted_iota(jnp.int32, (LANE, 2 * SSD_WIDTH), 0)
    c = lax.broadcasted_iota(jnp.int32, (LANE, 2 * SSD_WIDTH), 1)
    return jnp.where(c // SSD_HEAD_DIM == r, 1.0, 0.0).astype(BF16)


def _expand_heads(v, emat):
    hi = v.astype(BF16)
    mid = (v - hi.astype(F32)).astype(BF16)
    return _dot(hi, emat) + _dot(mid, emat)


def _prefix_rows(dta, tril):
    return sum(_dot(tril, p) for p in _split3(dta))


def _chunk_masks(q):
    r_i = lax.broadcasted_iota(jnp.int32, (q, q), 0)
    c_i = lax.broadcasted_iota(jnp.int32, (q, q), 1)
    return r_i >= c_i, r_i <= c_i


def _ssd_state_kernel(*refs, cps, nc, has_h0, has_sink):
    it = iter(refs)
    xbc_ref, prev_ref, next_ref, dt_ref, cw_ref, cb_ref, alog_row_ref = (next(it) for _ in range(7))
    h0_ref = next(it) if has_h0 else None
    if has_sink:
        next(it)
    xcb_ref, hsf_ref, sb_ref, dec_ref, hfin_ref, st_ref = (next(it) for _ in range(6))

    q = SSD_CHUNK
    rows = nc * q
    pos = (pl.program_id(0) * nc) % cps
    seq_first = pos == 0
    seq_last = pos + nc == cps

    prev = jnp.where(seq_first, 0.0, prev_ref[...])
    nxt = jnp.where(seq_last, 0.0, next_ref[...])
    ext = jnp.concatenate([prev, xbc_ref[...], nxt], axis=0)
    cw = cw_ref[...]
    acc = cb_ref[...] + ext[SUBLANE - 2:SUBLANE - 2 + rows] * cw[0:1]
    for k in range(1, SSD_CONV):
        o = SUBLANE - 2 + k
        acc = acc + ext[o:o + rows] * cw[k:k + 1]
    xcb_all = _silu(acc).astype(BF16)
    xcb_ref[...] = xcb_all

    lower, _ = _chunk_masks(q)
    tril = jnp.where(lower, 1.0, 0.0).astype(BF16)
    a_row = -jnp.exp(alog_row_ref[...])
    lane = lax.broadcasted_iota(jnp.int32, (q, LANE), 1)
    lane_t = lax.broadcasted_iota(jnp.int32, (2 * SUBLANE, LANE), 1)
    emat = _head_expand_matrix()
    gw = SSD_WIDTH // SSD_GROUPS

    @pl.when(seq_first)
    def _():
        if has_h0:
            st_ref[...] = h0_ref[0]
        else:
            st_ref[...] = jnp.zeros_like(st_ref)

    chunk_dec, chunk_states = [], []
    for c in range(nc):
        sl = slice(c * q, (c + 1) * q)
        xcb = xcb_all[sl]
        xs = xcb[:, :SSD_WIDTH].astype(F32)
        dt = dt_ref[sl, :]
        dta = dt * a_row
        la = _prefix_rows(dta, tril)
        tot = la[q - 1:q, :]
        w = jnp.exp(jnp.where(lane < SSD_HEADS, tot - la, la - dta)) * dt
        w = jnp.where(lane < 2 * SSD_HEADS, w, 0.0)
        wexp = _expand_heads(w, emat)
        etot = jnp.where(lane_t < 2 * SSD_HEADS, jnp.exp(jnp.broadcast_to(tot, (2 * SUBLANE, LANE))), 0.0)
        dec = _expand_heads(etot, emat)[:SUBLANE]
        dec_ref[c] = dec
        bmb = xcb[:, SSD_WIDTH:SSD_WIDTH + SSD_GROUPS * SSD_STATE]
        states = []
        for d in range(2):
            xw = (xs * wexp[:, d * SSD_WIDTH:(d + 1) * SSD_WIDTH]).astype(BF16)
            parts = [_dot_tn(bmb[:, grp * SSD_STATE:(grp + 1) * SSD_STATE], xw[:, grp * gw:(grp + 1) * gw])
                     for grp in range(SSD_GROUPS)]
            states.append(jnp.concatenate(parts, axis=1))
        sb_ref[c] = states[1]
        chunk_dec.append(dec[0:1, :SSD_WIDTH])
        chunk_states.append(states[0])

    hs = st_ref[...]
    for c in range(nc):
        hsf_ref[c] = hs.astype(BF16)
        hs = hs * chunk_dec[c] + chunk_states[c]
    st_ref[...] = hs

    @pl.when(seq_last)
    def _():
        hfin_ref[0, 0] = hs.T


def _ssd_out_kernel(*refs, cps, nc, has_h0, has_sink, nsteps):
    it = iter(refs)
    (xcb_ref, dt_ref, dtt_ref, z_ref, hsf_ref, sb_ref, dec_ref,
     alog_row_ref, alog_col_ref, dskip_ref, gn_ref) = (next(it) for _ in range(11))
    h0_ref = next(it) if has_h0 else None
    if has_sink:
        next(it)
    y_ref, hfin_ref, st_ref = next(it), next(it), next(it)

    q = SSD_CHUNK
    pos = ((nsteps - 1 - pl.program_id(0)) * nc) % cps
    seq_first = pos == 0
    seq_last = pos + nc == cps
    log2e = math.log2(math.e)

    lower, upper = _chunk_masks(q)
    tril = jnp.where(lower, 1.0, 0.0).astype(BF16)
    triu = jnp.where(upper, 1.0, 0.0).astype(BF16)
    a_row = -jnp.exp(alog_row_ref[...])
    a_col = -jnp.exp(alog_col_ref[...])
    lane = lax.broadcasted_iota(jnp.int32, (q, LANE), 1)
    rowi = lax.broadcasted_iota(jnp.int32, (2 * SUBLANE, q), 0)
    emat = _head_expand_matrix()
    rep = SSD_HEADS // SSD_GROUPS
    gw = SSD_WIDTH // SSD_GROUPS
    neg = jnp.float32(-jnp.inf)

    @pl.when(seq_last)
    def _():
        if has_h0:
            st_ref[...] = h0_ref[0]
        else:
            st_ref[...] = jnp.zeros_like(st_ref)

    def chunk_terms(c):
        sl = slice(c * q, (c + 1) * q)
        xcb = xcb_ref[sl, :]
        xsb = xcb[:, :SSD_WIDTH]
        bmb = xcb[:, SSD_WIDTH:SSD_WIDTH + SSD_GROUPS * SSD_STATE]
        cmb = xcb[:, SSD_WIDTH + SSD_GROUPS * SSD_STATE:]
        dt = dt_ref[sl, :]
        dtt = dtt_ref[:, sl]
        dta = dt * a_row
        dtat = dtt * a_col
        la = _prefix_rows(dta, tril)
        tot = la[q - 1:q, :]
        lcol = jnp.where(lane < SSD_HEADS, la, tot - la + dta)
        lat = sum(_dot(p, triu) for p in _split3(dtat))
        tott = lat[:, q - 1:q]
        lrow = jnp.where(rowi < SSD_HEADS, lat, tott - lat + dtat)
        lcol2 = lcol * log2e
        lrow2 = (lrow - jnp.log(dtt)) * log2e
        ecol = jnp.where(lane < 2 * SSD_HEADS, jnp.exp(lcol), 0.0)
        eexp = _expand_heads(ecol, emat)

        cbs = []
        for grp in range(SSD_GROUPS):
            cg = cmb[:, grp * SSD_STATE:(grp + 1) * SSD_STATE]
            bg = bmb[:, grp * SSD_STATE:(grp + 1) * SSD_STATE]
            cbs.append(_dot_nt(cg, bg))
        tiles = []
        for pair in range(SSD_HEADS // 2):
            xpair = xsb[:, pair * LANE:(pair + 1) * LANE]
            res = []
            for hd in (2 * pair, 2 * pair + 1):
                jf, jb = hd, SSD_HEADS + hd
                ef = jnp.exp2(jnp.where(lower, lcol2[:, jf:jf + 1] - lrow2[jf:jf + 1, :], neg))
                eb = jnp.exp2(jnp.where(upper, lcol2[:, jb:jb + 1] - lrow2[jb:jb + 1, :], neg))
                mm = (cbs[hd // rep] * (ef + eb)).astype(BF16)
                res.append(_dot(mm, xpair))
            tiles.append(jnp.where(lane < SSD_HEAD_DIM, res[0], res[1]))
        y = jnp.concatenate(tiles, axis=1)
        hsf = hsf_ref[c]
        parts = [_dot(cmb[:, grp * SSD_STATE:(grp + 1) * SSD_STATE], hsf[:, grp * gw:(grp + 1) * gw])
                 for grp in range(SSD_GROUPS)]
        y = y + jnp.concatenate(parts, axis=1) * eexp[:, :SSD_WIDTH]
        y = y + dskip_ref[...] * xsb.astype(F32)
        return y, cmb, eexp[:, SSD_WIDTH:], _silu(z_ref[sl, :])

    terms = [chunk_terms(c) for c in range(nc)]

    hb = st_ref[...]
    for c in reversed(range(nc)):
        y, cmb, eexp_b, gate = terms[c]
        hsb = hb.astype(BF16)
        parts = [_dot(cmb[:, grp * SSD_STATE:(grp + 1) * SSD_STATE], hsb[:, grp * gw:(grp + 1) * gw])
                 for grp in range(SSD_GROUPS)]
        y = (y + jnp.concatenate(parts, axis=1) * eexp_b) * gate
        y_ref[c * q:(c + 1) * q, :] = _rms_rows(y, gn_ref[...]).astype(BF16)
        hb = hb * dec_ref[c][0:1, SSD_WIDTH:] + sb_ref[c]
    st_ref[...] = hb

    @pl.when(seq_first)
    def _():
        hfin_ref[0, 0] = hb.T


def _ssd(xbc, dt, dtt, z, lw, cps, h0=None, sink=(0, 1, None)):
    slot0, nslots, sink_buf = sink
    t = xbc.shape[0]
    q = SSD_CHUNK
    nc = min(SSD_STEP_CHUNKS, cps)
    assert cps % nc == 0
    rows = nc * q
    nchunks = t // q
    nsteps = nchunks // nc
    spq = cps // nc
    nseq = nchunks // cps
    hb = rows // SUBLANE
    n8 = t // SUBLANE
    has_h0 = h0 is not None
    const = lambda i: (0, 0)
    st_block = (1, SSD_STATE, SSD_WIDTH)
    ch_block = (nc, SSD_STATE, SSD_WIDTH)
    fin_block = (1, 1, SSD_WIDTH, SSD_STATE)
    fin_shape = jax.ShapeDtypeStruct((nseq, nslots, SSD_WIDTH, SSD_STATE), F32)
    any_spec = pl.BlockSpec(memory_space=pl.ANY)
    dec_block = (nc, SUBLANE, 2 * SSD_WIDTH)

    in_specs = [
        pl.BlockSpec((rows, SSD_XBC), lambda i: (i, 0)),
        pl.BlockSpec((SUBLANE, SSD_XBC), lambda i: (jnp.maximum(i * hb - 1, 0), 0)),
        pl.BlockSpec((SUBLANE, SSD_XBC), lambda i: (jnp.minimum((i + 1) * hb, n8 - 1), 0)),
        pl.BlockSpec((rows, LANE), lambda i: (i, 0)),
        pl.BlockSpec((SSD_CONV, SSD_XBC), const),
        pl.BlockSpec((1, SSD_XBC), const),
        pl.BlockSpec((1, LANE), const),
    ]
    args = [xbc, xbc, xbc, dt, lw["conv_w"], lw["conv_b"], lw["alog_row"]]
    if has_h0:
        in_specs.append(pl.BlockSpec(st_block, lambda i: (i // spq, 0, 0)))
        args.append(h0[0])
    aliases = {}
    if sink_buf is not None:
        aliases[len(args)] = 4
        in_specs.append(any_spec)
        args.append(sink_buf)
    xcb, hsf, sb, dec, fin = pl.pallas_call(
        functools.partial(_ssd_state_kernel, cps=cps, nc=nc, has_h0=has_h0, has_sink=sink_buf is not None),
        grid=(nsteps,),
        in_specs=in_specs,
        out_specs=[
            pl.BlockSpec((rows, SSD_XBC), lambda i: (i, 0)),
            pl.BlockSpec(ch_block, lambda i: (i, 0, 0)),
            pl.BlockSpec(ch_block, lambda i: (i, 0, 0)),
            pl.BlockSpec(dec_block, lambda i: (i, 0, 0)),
            pl.BlockSpec(fin_block, lambda i: (i // spq, slot0, 0, 0)),
        ],
        out_shape=[
            jax.ShapeDtypeStruct((t, SSD_XBC), BF16),
            jax.ShapeDtypeStruct((nchunks, SSD_STATE, SSD_WIDTH), BF16),
            jax.ShapeDtypeStruct((nchunks, SSD_STATE, SSD_WIDTH), F32),
            jax.ShapeDtypeStruct((nchunks, SUBLANE, 2 * SSD_WIDTH), F32),
            fin_shape,
        ],
        input_output_aliases=aliases,
        scratch_shapes=[pltpu.VMEM((SSD_STATE, SSD_WIDTH), F32)],
        compiler_params=_cparams("arbitrary"),
        name="ssd_state",
    )(*args)

    gi = lambda i: nsteps - 1 - i
    in_specs = [
        pl.BlockSpec((rows, SSD_XBC), lambda i: (gi(i), 0)),
        pl.BlockSpec((rows, LANE), lambda i: (gi(i), 0)),
        pl.BlockSpec((2 * SSD_HEADS, rows), lambda i: (0, gi(i))),
        pl.BlockSpec((rows, SSD_WIDTH), lambda i: (gi(i), 0)),
        pl.BlockSpec(ch_block, lambda i: (gi(i), 0, 0)),
        pl.BlockSpec(ch_block, lambda i: (gi(i), 0, 0)),
        pl.BlockSpec(dec_block, lambda i: (gi(i), 0, 0)),
        pl.BlockSpec((1, LANE), const),
        pl.BlockSpec((2 * SSD_HEADS, 1), const),
        pl.BlockSpec((1, SSD_WIDTH), const),
        pl.BlockSpec((1, SSD_WIDTH), const),
    ]
    args = [xcb, dt, dtt, z, hsf, sb, dec, lw["alog_row"], lw["alog_col"], lw["dskip_row"], lw["ssd_norm"]]
    if has_h0:
        in_specs.append(pl.BlockSpec(st_block, lambda i: (gi(i) // spq, 0, 0)))
        args.append(h0[1])
    in_specs.append(any_spec)
    args.append(fin)
    y, fin = pl.pallas_call(
        functools.partial(_ssd_out_kernel, cps=cps, nc=nc, has_h0=has_h0, has_sink=True, nsteps=nsteps),
        grid=(nsteps,),
        in_specs=in_specs,
        out_specs=[
            pl.BlockSpec((rows, SSD_WIDTH), lambda i: (gi(i), 0)),
            pl.BlockSpec(fin_block, lambda i: (gi(i) // spq, slot0 + 1, 0, 0)),
        ],
        out_shape=[jax.ShapeDtypeStruct((t, SSD_WIDTH), BF16), fin_shape],
        input_output_aliases={len(args) - 1: 1},
        scratch_shapes=[pltpu.VMEM((SSD_STATE, SSD_WIDTH), F32)],
        compiler_params=_cparams("arbitrary"),
        name="ssd_out",
    )(*args)
    return y, fin


def _state_to_kernel_layout(h):
    n = h.shape[0]
    return h.transpose(0, 3, 1, 2).reshape(n, SSD_STATE, SSD_WIDTH)


def _attn_kernel(*refs, heads, has_cache):
    if has_cache:
        qt_ref, k_ref, vt_ref, kc_ref, vct_ref = refs[:5]
    else:
        qt_ref, k_ref, vt_ref = refs[:3]
    s_refs = refs[-2:]
    o_ref = refs[-3]
    lk = k_ref.shape[0]
    kb = min(ATTN_KEY_BLOCK, lk)
    blocks = [(k_ref, vt_ref, i * kb) for i in range(lk // kb)]
    if has_cache:
        lc = kc_ref.shape[0]
        kbc = min(ATTN_KEY_BLOCK, lc)
        blocks += [(kc_ref, vct_ref, i * kbc) for i in range(lc // kbc)]
        assert kbc == kb
    nblk = len(blocks)
    ones = jnp.ones((ONES_ROWS, kb), BF16)

    def score_block(hd, i, m):
        kr, _, off = blocks[i]
        q = qt_ref[hd * HEAD_PAD:(hd + 1) * HEAD_PAD, :]
        s = _dot(kr[off:off + kb, hd * HEAD_PAD:(hd + 1) * HEAD_PAD], q)
        s_refs[hd % 2][i * kb:(i + 1) * kb, :] = s
        bm = jnp.max(s, axis=0, keepdims=True)
        return bm if m is None else jnp.maximum(m, bm)

    def value_block(hd, i, m, acc):
        _, vr, off = blocks[i]
        p = jnp.exp2((s_refs[hd % 2][i * kb:(i + 1) * kb, :] - m).astype(BF16))
        v = vr[hd * MLA_V_DIM:(hd + 1) * MLA_V_DIM, off:off + kb]
        part = _dot(jnp.concatenate([v, ones], axis=0), p)
        return part if acc is None else acc + part

    m_cur = None
    for i in range(nblk):
        m_cur = score_block(0, i, m_cur)
    for hd in range(heads):
        m_next, acc = None, None
        for i in range(nblk):
            if hd + 1 < heads:
                m_next = score_block(hd + 1, i, m_next)
            acc = value_block(hd, i, m_cur, acc)
        vs = slice(hd * MLA_V_DIM, (hd + 1) * MLA_V_DIM)
        o_ref[vs, :] = acc[:MLA_V_DIM] / acc[MLA_V_DIM:MLA_V_DIM + 1]
        m_cur = m_next


def _attention(qt, k, vt, seq_len, heads_per_step, cache=None):
    t = k.shape[0]
    nseq = t // seq_len
    tq = min(ATTN_Q_TILE, seq_len)
    nq = seq_len // tq
    g = heads_per_step
    in_specs = [
        pl.BlockSpec((g * HEAD_PAD, tq), lambda s, h, j: (h, s * nq + j)),
        pl.BlockSpec((seq_len, g * HEAD_PAD), lambda s, h, j: (s, h)),
        pl.BlockSpec((g * MLA_V_DIM, seq_len), lambda s, h, j: (h, s)),
    ]
    args = [qt, k, vt]
    n_keys = seq_len
    if cache is not None:
        kc, vct = cache
        past = kc.shape[0] // nseq
        n_keys += past
        in_specs += [
            pl.BlockSpec((past, g * HEAD_PAD), lambda s, h, j: (s, h)),
            pl.BlockSpec((g * MLA_V_DIM, past), lambda s, h, j: (h, s)),
        ]
        args += [kc, vct]
    kern = functools.partial(_attn_kernel, heads=g, has_cache=cache is not None)
    return pl.pallas_call(
        kern,
        grid=(nseq, MLA_HEADS // g, nq),
        in_specs=in_specs,
        out_specs=pl.BlockSpec((g * MLA_V_DIM, tq), lambda s, h, j: (h, s * nq + j)),
        out_shape=jax.ShapeDtypeStruct((MLA_WIDTH, t), F32),
        scratch_shapes=[pltpu.VMEM((n_keys, tq), F32), pltpu.VMEM((n_keys, tq), F32)],
        compiler_params=_cparams("arbitrary", "arbitrary", "arbitrary"),
        name="attention",
    )(*args)


def _outproj_kernel(*refs, has_router):
    if has_router:
        (y_ref, ot_ref, x_ref, mod_ref, wa_ref, wb_ref, gm_ref, gpost_ref, gpre2_ref, rt_ref,
         x1_ref, h2_ref, comb_ref) = refs
    else:
        (y_ref, ot_ref, x_ref, mod_ref, wa_ref, wb_ref, gm_ref, gpost_ref, gpre2_ref,
         x1_ref, h2_ref) = refs
    mod = mod_ref[0]
    gate1 = mod[:, 2 * D_MODEL:3 * D_MODEL]
    shift2 = mod[:, 3 * D_MODEL:4 * D_MODEL]
    scale2 = mod[:, 4 * D_MODEL:5 * D_MODEL]
    ot = ot_ref[...]
    ms = jnp.mean(ot * ot, axis=0, keepdims=True)
    on = (ot * lax.rsqrt(ms + EPS) * gm_ref[...]).astype(BF16)
    y = _dot(y_ref[...], wa_ref[...]) + _dot_tn(on, wb_ref[...])
    x1 = x_ref[...] + gate1 * _rms_rows(y, gpost_ref[...])
    x1_ref[...] = x1
    h2 = _rms_rows(x1, gpre2_ref[...]) * (1.0 + scale2) + shift2
    h2_ref[...] = h2.astype(BF16)
    if has_router:
        hh, hm, _ = _split3(h2)
        rh, rm, _ = _split3(rt_ref[...])
        logits = _dot(hh, rh) + (_dot(hm, rh) + _dot(hh, rm))
        lane = lax.broadcasted_iota(jnp.int32, logits.shape, 1).astype(F32)
        neg = jnp.float32(-jnp.inf)
        lg = jnp.where(lane < N_EXPERTS, logits, neg)
        m1 = jnp.max(lg, axis=-1, keepdims=True)
        i1 = jnp.min(jnp.where(lg == m1, lane, float(LANE)), axis=-1, keepdims=True)
        lg2 = jnp.where(lane == i1, neg, lg)
        m2 = jnp.max(lg2, axis=-1, keepdims=True)
        i2 = jnp.min(jnp.where(lg2 == m2, lane, float(LANE)), axis=-1, keepdims=True)
        e2 = jnp.exp(m2 - m1)
        w1 = 1.0 / (1.0 + e2)
        w2 = e2 / (1.0 + e2)
        comb_ref[...] = jnp.where(lane == i1, w1, 0.0) + jnp.where(lane == i2, w2, 0.0)


def _outproj(yssd, ot, x, mod3, mod_row_fn, lw, router=None):
    t = x.shape[0]
    tm = ROW_TILE
    const = lambda i: (0, 0)
    row = lambda i: (i, 0)
    in_specs = [
        pl.BlockSpec((tm, SSD_WIDTH), row),
        pl.BlockSpec((MLA_WIDTH, tm), lambda i: (0, i)),
        pl.BlockSpec((tm, D_MODEL), row),
        pl.BlockSpec((1, 1, N_MOD * D_MODEL), lambda i: (mod_row_fn(ROW_TILE)(i), 0, 0)),
        pl.BlockSpec((SSD_WIDTH, D_MODEL), const),
        pl.BlockSpec((MLA_WIDTH, D_MODEL), const),
        pl.BlockSpec((MLA_WIDTH, 1), const),
        pl.BlockSpec((1, D_MODEL), const),
        pl.BlockSpec((1, D_MODEL), const),
    ]
    args = [yssd, ot, x, mod3, lw["w_out_a"], lw["w_out_b"], lw["mla_norm_col"],
            lw["g_post1"], lw["g_pre2"]]
    out_specs = [pl.BlockSpec((tm, D_MODEL), row), pl.BlockSpec((tm, D_MODEL), row)]
    out_shape = [jax.ShapeDtypeStruct((t, D_MODEL), F32), jax.ShapeDtypeStruct((t, D_MODEL), BF16)]
    if router is not None:
        in_specs.append(pl.BlockSpec((D_MODEL, LANE), const))
        args.append(router)
        out_specs.append(pl.BlockSpec((tm, LANE), row))
        out_shape.append(jax.ShapeDtypeStruct((t, LANE), F32))
    return pl.pallas_call(
        functools.partial(_outproj_kernel, has_router=router is not None),
        grid=(t // tm,),
        in_specs=in_specs,
        out_specs=out_specs,
        out_shape=out_shape,
        compiler_params=_cparams("arbitrary"),
        name="outproj",
    )(*args)


def _ffn_kernel(*refs, has_comb, nslab):
    if has_comb:
        h_ref, x_ref, comb_ref, mod_ref, wgu_ref, wd_ref, gpost_ref, o_ref, acc_ref = refs
    else:
        h_ref, x_ref, mod_ref, wgu_ref, wd_ref, gpost_ref, o_ref, acc_ref = refs
    e = pl.program_id(1)
    h = h_ref[...]
    f = wd_ref.shape[1]
    gu = _dot(h, wgu_ref[0])
    hid = _silu(gu[:, :f]) * gu[:, f:]
    if has_comb:
        comb = comb_ref[...]
        lane = lax.broadcasted_iota(jnp.int32, comb.shape, 1)
        wcol = jnp.sum(jnp.where(lane == e, comb, 0.0), axis=-1, keepdims=True)
        hid = hid * wcol
    part = _dot(hid.astype(BF16), wd_ref[0])

    @pl.when(e == 0)
    def _():
        acc_ref[...] = part

    @pl.when(e > 0)
    def _():
        acc_ref[...] += part

    @pl.when(e == nslab - 1)
    def _():
        gate2 = mod_ref[0][:, 5 * D_MODEL:6 * D_MODEL]
        o_ref[...] = x_ref[...] + gate2 * _rms_rows(acc_ref[...], gpost_ref[...])


def _ffn(h2, x1, mod3, mod_row_fn, wgu, wd, gpost, comb=None):
    t = x1.shape[0]
    tm = FFN_ROW_TILE
    nslab, f, _ = wd.shape
    row = lambda i, e: (i, 0)
    in_specs = [pl.BlockSpec((tm, D_MODEL), row), pl.BlockSpec((tm, D_MODEL), row)]
    args = [h2, x1]
    if comb is not None:
        in_specs.append(pl.BlockSpec((tm, LANE), row))
        args.append(comb)
    in_specs += [
        pl.BlockSpec((1, 1, N_MOD * D_MODEL), lambda i, e: (mod_row_fn(FFN_ROW_TILE)(i), 0, 0)),
        pl.BlockSpec((1, D_MODEL, 2 * f), lambda i, e: (e, 0, 0)),
        pl.BlockSpec((1, f, D_MODEL), lambda i, e: (e, 0, 0)),
        pl.BlockSpec((1, D_MODEL), lambda i, e: (0, 0)),
    ]
    args += [mod3, wgu, wd, gpost]
    return pl.pallas_call(
        functools.partial(_ffn_kernel, has_comb=comb is not None, nslab=nslab),
        grid=(t // tm, nslab),
        in_specs=in_specs,
        out_specs=pl.BlockSpec((tm, D_MODEL), row),
        out_shape=jax.ShapeDtypeStruct((t, D_MODEL), F32),
        scratch_shapes=[pltpu.VMEM((tm, D_MODEL), F32)],
        compiler_params=_cparams("arbitrary", "arbitrary"),
        name="ffn",
    )(*args)


def _moe_kernel(h_ref, x_ref, comb_ref, mod_ref, wgu_ref, wd_ref, gpost_ref, o_ref,
                acc_ref, rank_ref, rank_t_ref, comb_t_ref, *, nexp, sub):
    e = pl.program_id(1)
    tm = h_ref.shape[0]
    cap = MOE_CAP
    f = wd_ref.shape[1]
    capl = -(-cap // LANE) * LANE

    @pl.when(e == 0)
    def _():
        acc_ref[...] = jnp.zeros_like(acc_ref)
        r_i = lax.broadcasted_iota(jnp.int32, (sub, sub), 0)
        c_i = lax.broadcasted_iota(jnp.int32, (sub, sub), 1)
        strict = jnp.where(r_i > c_i, 1.0, 0.0).astype(BF16)
        for s in range(tm // sub):
            rows = slice(s * sub, (s + 1) * sub)
            comb = comb_ref[rows, :]
            rank = _dot(strict, jnp.where(comb > 0.0, 1.0, 0.0).astype(BF16))
            rank_ref[rows, :] = rank
            rank_t_ref[:, rows] = rank.T
            comb_t_ref[:, rows] = comb.T

    lane = lax.broadcasted_iota(jnp.int32, (sub, LANE), 1)
    slot_l = lax.broadcasted_iota(jnp.int32, (sub, capl), 1).astype(F32)
    slot_s = lax.broadcasted_iota(jnp.int32, (cap, sub), 0).astype(F32)
    for s in range(tm // sub):
        rows = slice(s * sub, (s + 1) * sub)
        pick = lane == e
        wcol = jnp.sum(jnp.where(pick, comb_ref[rows, :], 0.0), axis=-1, keepdims=True)
        rcol = jnp.sum(jnp.where(pick, rank_ref[rows, :], 0.0), axis=-1, keepdims=True)
        wrow = comb_t_ref[pl.ds(e, 1), rows]
        rrow = rank_t_ref[pl.ds(e, 1), rows]
        count = jnp.max(jnp.where(wrow > 0.0, rrow + 1.0, 0.0))
        nchunk = ((count + (cap - 1.0)) * (1.0 / cap)).astype(jnp.int32)
        hsub = h_ref[rows, :]

        def chunk(c, carry, rows=rows, wcol=wcol, rcol=rcol, wrow=wrow, rrow=rrow, hsub=hsub):
            base = (c * cap).astype(F32)
            gather = jnp.where(((rrow - base) == slot_s) & (wrow > 0.0), 1.0, 0.0).astype(BF16)
            scatter = jnp.where(((rcol - base) == slot_l) & (wcol > 0.0) & (slot_l < float(cap)),
                                1.0, 0.0).astype(BF16)
            xg = _dot(gather, hsub).astype(BF16)
            gu = _dot(xg, wgu_ref[0])
            hid = _silu(gu[:, :f]) * gu[:, f:]
            y = _dot(hid.astype(BF16), wd_ref[0]).astype(BF16)
            if capl > cap:
                y = jnp.concatenate([y, jnp.zeros((capl - cap, y.shape[1]), BF16)], axis=0)
            acc_ref[rows, :] += wcol * _dot(scatter, y)
            return carry

        lax.fori_loop(0, nchunk, chunk, 0)

    @pl.when(e == nexp - 1)
    def _():
        gate2 = mod_ref[0][:, 5 * D_MODEL:6 * D_MODEL]
        o_ref[...] = x_ref[...] + gate2 * _rms_rows(acc_ref[...], gpost_ref[...])


def _moe(h2, x1, comb, mod3, mod_row_fn, wgu, wd, gpost):
    t = x1.shape[0]
    tm = min(MOE_ROW_TILE, t)
    sub = min(MOE_SUB_TILE, tm)
    nexp, f, _ = wd.shape
    row = lambda i, e: (i, 0)
    return pl.pallas_call(
        functools.partial(_moe_kernel, nexp=nexp, sub=sub),
        grid=(t // tm, nexp),
        in_specs=[
            pl.BlockSpec((tm, D_MODEL), row),
            pl.BlockSpec((tm, D_MODEL), row),
            pl.BlockSpec((tm, LANE), row),
            pl.BlockSpec((1, 1, N_MOD * D_MODEL), lambda i, e: (mod_row_fn(tm)(i), 0, 0)),
            pl.BlockSpec((1, D_MODEL, 2 * f), lambda i, e: (e, 0, 0)),
            pl.BlockSpec((1, f, D_MODEL), lambda i, e: (e, 0, 0)),
            pl.BlockSpec((1, D_MODEL), lambda i, e: (0, 0)),
        ],
        out_specs=pl.BlockSpec((tm, D_MODEL), row),
        out_shape=jax.ShapeDtypeStruct((t, D_MODEL), F32),
        scratch_shapes=[pltpu.VMEM((tm, D_MODEL), F32), pltpu.VMEM((tm, LANE), F32),
                        pltpu.VMEM((LANE, tm), F32), pltpu.VMEM((LANE, tm), F32)],
        compiler_params=_cparams("arbitrary", "arbitrary"),
        name="moe",
    )(h2, x1, comb, mod3, wgu, wd, gpost)


def _prep_layer(i, p):
    w_in = p["w_in"][i]
    s1 = SSD_WIDTH
    s2 = s1 + SSD_XBC
    s3 = s2 + 2 * SSD_HEADS
    s4 = s3 + MLA_Q_RANK
    s5 = s4 + MLA_KV_RANK
    w_z, w_xbc, w_dt, w_cq, w_ckv, w_kr = (w_in[:, :s1], w_in[:, s1:s2], w_in[:, s2:s3],
                                             w_in[:, s3:s4], w_in[:, s4:s5], w_in[:, s5:])
    zc = lambda n: jnp.zeros((D_MODEL, n), F32)
    tile_a = jnp.concatenate([w_dt, zc(ROPE_LANE0 - 2 * SSD_HEADS), w_kr,
                              zc(LANE - ROPE_LANE0 - MLA_ROPE_DIM)], axis=1)
    w_in_pad = jnp.concatenate([w_z, w_xbc, w_cq, w_ckv, tile_a], axis=1).astype(BF16)

    w_uq = p["w_uq"][i].reshape(MLA_Q_RANK, MLA_HEADS, MLA_NOPE_DIM + MLA_ROPE_DIM)
    q_nope, q_rope = w_uq[..., :MLA_NOPE_DIM], w_uq[..., MLA_NOPE_DIM:]
    zq = lambda n: jnp.zeros((MLA_Q_RANK, MLA_HEADS, n), F32)
    pad = HEAD_PAD - MLA_NOPE_DIM - MLA_ROPE_DIM
    w_q = jnp.concatenate([q_nope, q_rope, zq(pad)], axis=-1).reshape(MLA_Q_RANK, -1)

    w_ukv = p["w_ukv"][i].reshape(MLA_KV_RANK, MLA_HEADS, MLA_NOPE_DIM + MLA_V_DIM)
    k_nope, v_w = w_ukv[..., :MLA_NOPE_DIM], w_ukv[..., MLA_NOPE_DIM:]
    w_uk = jnp.concatenate([k_nope, jnp.zeros((MLA_KV_RANK, MLA_HEADS, HEAD_PAD - MLA_NOPE_DIM), F32)],
                           axis=-1).reshape(MLA_KV_RANK, -1)
    w_uv = v_w.reshape(MLA_KV_RANK, MLA_WIDTH)

    dtb = p["dt_bias"][i].reshape(2 * SSD_HEADS)
    alog = p["a_log"][i].reshape(2 * SSD_HEADS)
    padl = lambda v: jnp.pad(v, (0, LANE - v.shape[0])).reshape(1, LANE)
    w_out = p["w_out"][i]
    return {
        "g_pre1": p["norm_pre_mix"][i].reshape(1, D_MODEL),
        "g_post1": p["norm_post_mix"][i].reshape(1, D_MODEL),
        "g_pre2": p["norm_pre_ffn"][i].reshape(1, D_MODEL),
        "g_post2": p["norm_post_ffn"][i].reshape(1, D_MODEL),
        "w_in": w_in_pad,
        "w_dt": w_dt.T.astype(BF16),
        "dtb_row": padl(dtb),
        "dtb_col": dtb.reshape(-1, 1),
        "alog_row": padl(alog),
        "alog_col": alog.reshape(-1, 1),
        "q_norm": p["q_norm"][i].reshape(1, -1),
        "w_q": w_q.T.astype(BF16),
        "kv_norm": p["kv_norm"][i].reshape(1, -1),
        "w_uk": w_uk.astype(BF16),
        "w_uv": w_uv.T.astype(BF16),
        "conv_w": p["conv_w"][i],
        "conv_b": p["conv_b"][i].reshape(1, -1),
        "dskip_row": jnp.repeat(p["d_skip"][i], SSD_HEAD_DIM).reshape(1, -1),
        "ssd_norm": p["ssd_norm"][i].reshape(1, -1),
        "mla_norm_col": p["mla_norm"][i].reshape(-1, 1),
        "w_out_a": w_out[:SSD_WIDTH].astype(BF16),
        "w_out_b": w_out[SSD_WIDTH:].astype(BF16),
    }


def _rope_tables(n_tokens):
    rows = n_tokens // GRID_W
    row = np.repeat(np.arange(rows, dtype=np.float32), GRID_W)
    col = np.tile(np.arange(GRID_W, dtype=np.float32), rows)
    half = MLA_ROPE_DIM // 2
    inv = (np.float32(ROPE_THETA) ** (-np.arange(0, half, 2, dtype=np.float32) / np.float32(half))).astype(np.float32)
    ar = row[:, None] * inv[None, :]
    ac = col[:, None] * inv[None, :]
    ang = np.concatenate([ar, ar, ac, ac], axis=-1).astype(np.float32)
    return jnp.asarray(np.cos(ang), F32), jnp.asarray(np.sin(ang), F32)


def _attn_tables(cos, sin, n):
    scale = (MLA_NOPE_DIM + MLA_ROPE_DIM) ** -0.5 * math.log2(math.e)
    pad = HEAD_PAD - ROPE_LANE0 - MLA_ROPE_DIM
    cosk = jnp.concatenate([jnp.zeros((n, ROPE_LANE0), F32), cos, jnp.zeros((n, pad), F32)], axis=1)
    sink = jnp.concatenate([jnp.zeros((n, ROPE_LANE0), F32), sin, jnp.zeros((n, pad), F32)], axis=1)
    cosq = jnp.concatenate([jnp.ones((n, ROPE_LANE0), F32), cos, jnp.zeros((n, pad), F32)], axis=1)
    return (cosq * scale).T, (sink * scale).T, cosk, sink


def kernel(x_prompt, x_sample, cache_ckv, cache_krope, state_ssm, c, c_ctx, w_mod, b_mod, norm_pre_mix, norm_post_mix, norm_pre_ffn, norm_post_ffn, w_in, conv_w, conv_b, dt_bias, a_log, d_skip, ssd_norm, q_norm, w_uq, kv_norm, w_ukv, mla_norm, w_out, ffn_w_gate, ffn_w_up, ffn_w_down, moe_router, moe_w_gate, moe_w_up, moe_w_down):
    params = dict(w_in=w_in, conv_w=conv_w, conv_b=conv_b, dt_bias=dt_bias, a_log=a_log, d_skip=d_skip,
                  ssd_norm=ssd_norm, q_norm=q_norm, w_uq=w_uq, kv_norm=kv_norm, w_ukv=w_ukv,
                  mla_norm=mla_norm, w_out=w_out, norm_pre_mix=norm_pre_mix, norm_post_mix=norm_post_mix,
                  norm_pre_ffn=norm_pre_ffn, norm_post_ffn=norm_post_ffn)
    batch, seq, d = x_prompt.shape
    dec_batch, dec_seq, _ = x_sample.shape
    depth = w_in.shape[0]
    past = cache_ckv.shape[2]
    tm = ROW_TILE

    cvec = jnp.concatenate([c_ctx[None, :], c, jnp.zeros((SUBLANE - 1 - dec_batch, d), F32)], axis=0)
    mod = _modulation(cvec, w_mod, b_mod)

    ones = jnp.ones((tm, MLA_ROPE_DIM), F32)
    tabs_ctx = _attn_tables(ones, jnp.zeros_like(ones), tm)
    cos, sin = _rope_tables(dec_seq)
    tabs_lat = _attn_tables(cos, sin, dec_seq)
    lat_blocks = dec_seq // tm

    xp = x_prompt.reshape(batch * seq, d)
    xs = x_sample.reshape(dec_batch * dec_seq, d)
    cache_bufs, ssm_buf = None, None
    for i in range(depth):
        lw = _prep_layer(i, params)
        mod3 = mod[i].reshape(SUBLANE, 1, N_MOD * d)
        j = i // 2
        if i % 2 == 0:
            f = ffn_w_gate.shape[2] // 2
            wgu = jnp.stack([jnp.concatenate([ffn_w_gate[j][:, s * f:(s + 1) * f], ffn_w_up[j][:, s * f:(s + 1) * f]],
                                             axis=1) for s in range(2)], axis=0).astype(BF16)
            wd = ffn_w_down[j].reshape(2, f, d).astype(BF16)
            router = None
        else:
            wgu = jnp.concatenate([moe_w_gate[j], moe_w_up[j]], axis=-1).astype(BF16)
            wd = moe_w_down[j].astype(BF16)
            router = jnp.pad(moe_router[j], ((0, 0), (0, LANE - N_EXPERTS)))

        def run(x, row_fn, tabs, tab_blocks, seq_len, heads_per_step, ctx):
            cache_out = (i, depth, seq_len, cache_bufs) if ctx is None else None
            z, xbc, dt, dtt, qt, k, vt, *caches = _inproj(x, mod3, row_fn, lw, tabs, tab_blocks, cache_out)
            cps = seq_len // SSD_CHUNK
            h0 = None
            if ctx is not None:
                h0 = (_state_to_kernel_layout(ctx[2][:, 0]), _state_to_kernel_layout(ctx[2][:, 1]))
            sink = (2 * i, 2 * depth, ssm_buf) if ctx is None else (0, 2, None)
            yssd, fin = _ssd(xbc, dt, dtt, z, lw, cps, h0=h0, sink=sink)
            cache = None
            if ctx is not None:
                kr_tile = jnp.pad(ctx[1].reshape(-1, MLA_ROPE_DIM),
                                  ((0, 0), (ROPE_LANE0, HEAD_PAD - ROPE_LANE0 - MLA_ROPE_DIM)))
                cache = _kvcache(ctx[0].reshape(-1, MLA_KV_RANK), kr_tile, lw)
            ot = _attention(qt, k, vt, seq_len, heads_per_step, cache=cache)
            outs = _outproj(yssd, ot, x, mod3, row_fn, lw, router=router)
            x1, h2 = outs[0], outs[1]
            comb = outs[2] if router is not None else None
            if comb is None:
                x2 = _ffn(h2, x1, mod3, row_fn, wgu, wd, lw["g_post2"])
            else:
                x2 = _moe(h2, x1, comb, mod3, row_fn, wgu, wd, lw["g_post2"])
            return x2, caches, fin

        xp, cache_bufs, ssm_buf = run(xp, lambda tile: (lambda b: 0), tabs_ctx, 1, seq, MLA_HEADS, None)
        xs, _, _ = run(xs, lambda tile: (lambda b: 1 + (b * tile) // dec_seq), tabs_lat, lat_blocks, dec_seq, MLA_HEADS,
                             (cache_ckv[:, i], cache_krope[:, i], state_ssm[:, i]))
    return (xp.reshape(batch, seq, d), xs.reshape(dec_batch, dec_seq, d),
            cache_bufs[0], cache_bufs[1],
            ssm_buf.reshape(batch, depth, 2, SSD_HEADS, SSD_HEAD_DIM, SSD_STATE))
```

```python
import functools
import math

import jax
import jax.numpy as jnp
import numpy as np
from jax import lax
from jax.experimental import pallas as pl
from jax.experimental.pallas import tpu as pltpu

F32 = jnp.float32
BF16 = jnp.bfloat16

D_MODEL = 1024
GRID_W = 64
SSD_WIDTH = 512
SSD_HEAD_DIM = 64
SSD_HEADS = 8
SSD_GROUPS = 2
SSD_STATE = 64
SSD_CONV = 5
SSD_CHUNK = 128
SSD_STEP_CHUNKS = 4
SSD_XBC = SSD_WIDTH + 2 * SSD_GROUPS * SSD_STATE
MLA_WIDTH = 512
MLA_V_DIM = 64
MLA_HEADS = 8
MLA_NOPE_DIM = 64
MLA_ROPE_DIM = 32
MLA_Q_RANK = 384
MLA_KV_RANK = 256
ROPE_THETA = 10000.0
N_EXPERTS = 8
N_MOD = 6
EPS = 1e-6

LANE = 128
SUBLANE = 8
HEAD_PAD = 128
ONES_ROWS = 16
ROPE_LANE0 = MLA_NOPE_DIM
C_Z = 0
C_XBC = C_Z + SSD_WIDTH
C_CQ = C_XBC + SSD_XBC
C_CKV = C_CQ + MLA_Q_RANK
C_TA = C_CKV + MLA_KV_RANK
IN_PAD = C_TA + LANE
ROT_GROUP = MLA_ROPE_DIM // 4

VMEM_LIMIT = 56 * 1024 * 1024

ROW_TILE = 512
ATTN_Q_TILE = 256
ATTN_KEY_BLOCK = 512
FFN_ROW_TILE = 512
MOE_ROW_TILE = 1024
MOE_SUB_TILE = 512
MOE_CAPS = (128, 160, 192, 224, 256)

NT_DIMS = (((1,), (1,)), ((), ()))
TN_DIMS = (((0,), (0,)), ((), ()))


def _cparams(*sem):
    return pltpu.CompilerParams(dimension_semantics=sem, vmem_limit_bytes=VMEM_LIMIT)


def _silu(x):
    return x / (1.0 + jnp.exp(-x))


def _softplus(x):
    return jnp.maximum(x, 0.0) + jnp.log(1.0 + jnp.exp(-jnp.abs(x)))


def _rms_rows(x, g):
    ms = jnp.mean(x * x, axis=-1, keepdims=True)
    return x * lax.rsqrt(ms + EPS) * g


def _dot(a, b):
    return jnp.dot(a, b, preferred_element_type=F32)


def _dot_nt(a, b):
    return lax.dot_general(a, b, NT_DIMS, preferred_element_type=F32)


def _dot_tn(a, b):
    return lax.dot_general(a, b, TN_DIMS, preferred_element_type=F32)


def _split3(x):
    hi = x.astype(BF16)
    r1 = x - hi.astype(F32)
    mid = r1.astype(BF16)
    lo = (r1 - mid.astype(F32)).astype(BF16)
    return hi, mid, lo


def _mod_kernel(c_ref, w_ref, b_ref, o_ref):
    s = _silu(c_ref[...]).astype(BF16)
    o_ref[0] = _dot(s, w_ref[0].astype(BF16)) + b_ref[0]


def _modulation(cvec, w_mod, b_mod):
    depth, d, n = w_mod.shape
    tn = 1536
    return pl.pallas_call(
        _mod_kernel,
        grid=(depth, n // tn),
        in_specs=[
            pl.BlockSpec((SUBLANE, d), lambda l, j: (0, 0)),
            pl.BlockSpec((1, d, tn), lambda l, j: (l, 0, j)),
            pl.BlockSpec((1, 1, tn), lambda l, j: (l, 0, j)),
        ],
        out_specs=pl.BlockSpec((1, SUBLANE, tn), lambda l, j: (l, 0, j)),
        out_shape=jax.ShapeDtypeStruct((depth, SUBLANE, n), F32),
        compiler_params=_cparams("arbitrary", "arbitrary"),
        name="modulation",
    )(cvec, w_mod, b_mod.reshape(depth, 1, n))


def _inproj_kernel(*refs, emit_cache, n_alias):
    (x_ref, mod_ref, gpre_ref, win_ref, wdt_ref, dtb_row_ref, dtb_col_ref, qn_ref, wq_ref, kvn_ref, wuk_ref,
     wuv_ref, cosq_ref, sinq_ref, cosk_ref, sink_ref) = refs[:16]
    outs = refs[16 + n_alias:]
    z_ref, xbc_ref, dt_ref, dtt_ref, qt_ref, k_ref, vt_ref = outs[:7]
    mod = mod_ref[0]
    shift = mod[:, 0:D_MODEL]
    scale = mod[:, D_MODEL:2 * D_MODEL]
    h = _rms_rows(x_ref[...], gpre_ref[...]) * (1.0 + scale) + shift
    hb = h.astype(BF16)
    proj = _dot(hb, win_ref[...])
    z_ref[...] = proj[:, C_Z:C_XBC]
    xbc_ref[...] = proj[:, C_XBC:C_CQ]
    cqn = _rms_rows(proj[:, C_CQ:C_CKV], qn_ref[...]).astype(BF16)
    ckvn = _rms_rows(proj[:, C_CKV:C_TA], kvn_ref[...])
    if emit_cache:
        ckvn_ref, kr_ref = outs[7:]
        nb, _, sq, _ = ckvn_ref.shape
        ckvn_ref[:, 0] = ckvn.reshape(nb, sq, MLA_KV_RANK)
    ckvb = ckvn.astype(BF16)
    ta = proj[:, C_TA:IN_PAD]
    dt_ref[...] = _softplus(ta + dtb_row_ref[...])
    if emit_cache:
        kr_ref[:, 0] = ta[:, ROPE_LANE0:ROPE_LANE0 + MLA_ROPE_DIM].reshape(nb, sq, MLA_ROPE_DIM)
    lane = lax.broadcasted_iota(jnp.int32, ta.shape, 1)
    first = (lane // ROT_GROUP) % 2 == 0
    rot = jnp.where(first, -pltpu.roll(ta, LANE - ROT_GROUP, 1), pltpu.roll(ta, ROT_GROUP, 1))
    kr_rot = ta * cosk_ref[...] + rot * sink_ref[...]
    knp = _dot(ckvb, wuk_ref[...])
    for hd in range(MLA_HEADS):
        sl = slice(hd * HEAD_PAD, (hd + 1) * HEAD_PAD)
        k_ref[:, sl] = (knp[:, sl] + kr_rot).astype(BF16)
    vt_ref[...] = _dot_nt(wuv_ref[...], ckvb).astype(BF16)
    qt = _dot_nt(wq_ref[...], cqn)
    cosq = cosq_ref[...]
    sinq = sinq_ref[...]
    g, r0 = ROT_GROUP, ROPE_LANE0
    for hd in range(MLA_HEADS):
        blk = qt[hd * HEAD_PAD:(hd + 1) * HEAD_PAD, :]
        rot = jnp.concatenate([blk[:r0], -blk[r0 + g:r0 + 2 * g], blk[r0:r0 + g], -blk[r0 + 3 * g:r0 + 4 * g],
                               blk[r0 + 2 * g:r0 + 3 * g], blk[r0 + 4 * g:]], axis=0)
        qt_ref[hd * HEAD_PAD:(hd + 1) * HEAD_PAD, :] = (blk * cosq + rot * sinq).astype(BF16)
    dtt_ref[...] = _softplus(_dot_nt(wdt_ref[...], hb) + dtb_col_ref[...])


def _inproj(x, mod3, mod_row_fn, lw, tabs, tab_blocks, cache_out=None):
    t = x.shape[0]
    tm = ROW_TILE
    nb = t // tm
    cosq, sinq, cosk, sink = tabs
    ntab = tab_blocks
    const = lambda i: (0, 0)
    row = lambda i: (i, 0)
    col = lambda i: (0, i)
    in_specs = [
        pl.BlockSpec((tm, D_MODEL), row),
        pl.BlockSpec((1, 1, N_MOD * D_MODEL), lambda i: (mod_row_fn(ROW_TILE)(i), 0, 0)),
        pl.BlockSpec((1, D_MODEL), const),
        pl.BlockSpec((D_MODEL, IN_PAD), const),
        pl.BlockSpec((2 * SSD_HEADS, D_MODEL), const),
        pl.BlockSpec((1, LANE), const),
        pl.BlockSpec((2 * SSD_HEADS, 1), const),
        pl.BlockSpec((1, MLA_Q_RANK), const),
        pl.BlockSpec((MLA_HEADS * HEAD_PAD, MLA_Q_RANK), const),
        pl.BlockSpec((1, MLA_KV_RANK), const),
        pl.BlockSpec((MLA_KV_RANK, MLA_HEADS * HEAD_PAD), const),
        pl.BlockSpec((MLA_WIDTH, MLA_KV_RANK), const),
        pl.BlockSpec((HEAD_PAD, tm), lambda i: (0, i % ntab)),
        pl.BlockSpec((HEAD_PAD, tm), lambda i: (0, i % ntab)),
        pl.BlockSpec((tm, LANE), lambda i: (i % ntab, 0)),
        pl.BlockSpec((tm, LANE), lambda i: (i % ntab, 0)),
    ]
    out_specs = [
        pl.BlockSpec((tm, SSD_WIDTH), row),
        pl.BlockSpec((tm, SSD_XBC), row),
        pl.BlockSpec((tm, LANE), row),
        pl.BlockSpec((2 * SSD_HEADS, tm), col),
        pl.BlockSpec((MLA_HEADS * HEAD_PAD, tm), col),
        pl.BlockSpec((tm, MLA_HEADS * HEAD_PAD), row),
        pl.BlockSpec((MLA_WIDTH, tm), col),
    ]
    out_shape = [
        jax.ShapeDtypeStruct((t, SSD_WIDTH), F32),
        jax.ShapeDtypeStruct((t, SSD_XBC), F32),
        jax.ShapeDtypeStruct((t, LANE), F32),
        jax.ShapeDtypeStruct((2 * SSD_HEADS, t), F32),
        jax.ShapeDtypeStruct((MLA_HEADS * HEAD_PAD, t), BF16),
        jax.ShapeDtypeStruct((t, MLA_HEADS * HEAD_PAD), BF16),
        jax.ShapeDtypeStruct((MLA_WIDTH, t), BF16),
    ]
    args = [x, mod3, lw["g_pre1"], lw["w_in"], lw["w_dt"], lw["dtb_row"], lw["dtb_col"],
            lw["q_norm"], lw["w_q"], lw["kv_norm"], lw["w_uk"], lw["w_uv"], cosq, sinq, cosk, sink]
    aliases = {}
    n_alias = 0
    if cache_out is not None:
        layer, depth, seq, bufs = cache_out
        nb_seq = tm // seq
        for rank in (MLA_KV_RANK, MLA_ROPE_DIM):
            out_specs.append(pl.BlockSpec((nb_seq, 1, seq, rank), lambda i: (i, layer, 0, 0)))
            out_shape.append(jax.ShapeDtypeStruct((t // seq, depth, seq, rank), F32))
        if bufs is not None:
            n_alias = len(bufs)
            for j, buf in enumerate(bufs):
                aliases[len(args)] = len(out_shape) - n_alias + j
                in_specs.append(pl.BlockSpec(memory_space=pl.ANY))
                args.append(buf)
    return pl.pallas_call(
        functools.partial(_inproj_kernel, emit_cache=cache_out is not None, n_alias=n_alias),
        grid=(nb,),
        in_specs=in_specs,
        out_specs=out_specs,
        out_shape=out_shape,
        input_output_aliases=aliases,
        compiler_params=_cparams("arbitrary"),
        name="inproj",
    )(*args)


def _kvcache_kernel(ckv_ref, kr_ref, wuk_ref, wuv_ref, k_ref, vt_ref):
    ckvb = ckv_ref[...].astype(BF16)
    knp = _dot(ckvb, wuk_ref[...])
    kr = kr_ref[...]
    for hd in range(MLA_HEADS):
        sl = slice(hd * HEAD_PAD, (hd + 1) * HEAD_PAD)
        k_ref[:, sl] = (knp[:, sl] + kr).astype(BF16)
    vt_ref[...] = _dot_nt(wuv_ref[...], ckvb).astype(BF16)


def _kvcache(ckv, kr_tile, lw):
    n = ckv.shape[0]
    tm = 512
    return pl.pallas_call(
        _kvcache_kernel,
        grid=(n // tm,),
        in_specs=[
            pl.BlockSpec((tm, MLA_KV_RANK), lambda i: (i, 0)),
            pl.BlockSpec((tm, LANE), lambda i: (i, 0)),
            pl.BlockSpec((MLA_KV_RANK, MLA_HEADS * HEAD_PAD), lambda i: (0, 0)),
            pl.BlockSpec((MLA_WIDTH, MLA_KV_RANK), lambda i: (0, 0)),
        ],
        out_specs=[
            pl.BlockSpec((tm, MLA_HEADS * HEAD_PAD), lambda i: (i, 0)),
            pl.BlockSpec((MLA_WIDTH, tm), lambda i: (0, i)),
        ],
        out_shape=[
            jax.ShapeDtypeStruct((n, MLA_HEADS * HEAD_PAD), BF16),
            jax.ShapeDtypeStruct((MLA_WIDTH, n), BF16),
        ],
        compiler_params=_cparams("arbitrary"),
        name="kvcache",
    )(ckv, kr_tile, lw["w_uk"], lw["w_uv"])


def _head_expand_matrix():
    r = lax.broadcasted_iota(jnp.int32, (LANE, 2 * SSD_WIDTH), 0)
    c = lax.broadcasted_iota(jnp.int32, (LANE, 2 * SSD_WIDTH), 1)
    return jnp.where(c // SSD_HEAD_DIM == r, 1.0, 0.0).astype(BF16)


def _expand_heads(v, emat):
    hi = v.astype(BF16)
    mid = (v - hi.astype(F32)).astype(BF16)
    return _dot(hi, emat) + _dot(mid, emat)


def _prefix_rows(dta, tril):
    return sum(_dot(tril, p) for p in _split3(dta))


def _chunk_masks(q):
    r_i = lax.broadcasted_iota(jnp.int32, (q, q), 0)
    c_i = lax.broadcasted_iota(jnp.int32, (q, q), 1)
    return r_i >= c_i, r_i <= c_i


def _ssd_state_kernel(*refs, cps, nc, has_h0, has_sink):
    it = iter(refs)
    xbc_ref, prev_ref, next_ref, dt_ref, cw_ref, cb_ref, alog_row_ref = (next(it) for _ in range(7))
    h0_ref = next(it) if has_h0 else None
    if has_sink:
        next(it)
    xcb_ref, hsf_ref, sb_ref, dec_ref, hfin_ref, st_ref = (next(it) for _ in range(6))

    q = SSD_CHUNK
    rows = nc * q
    pos = (pl.program_id(0) * nc) % cps
    seq_first = pos == 0
    seq_last = pos + nc == cps

    prev = jnp.where(seq_first, 0.0, prev_ref[...])
    nxt = jnp.where(seq_last, 0.0, next_ref[...])
    ext = jnp.concatenate([prev, xbc_ref[...], nxt], axis=0)
    cw = cw_ref[...]
    acc = cb_ref[...] + ext[SUBLANE - 2:SUBLANE - 2 + rows] * cw[0:1]
    for k in range(1, SSD_CONV):
        o = SUBLANE - 2 + k
        acc = acc + ext[o:o + rows] * cw[k:k + 1]
    xcb_all = _silu(acc).astype(BF16)
    xcb_ref[...] = xcb_all

    lower, _ = _chunk_masks(q)
    tril = jnp.where(lower, 1.0, 0.0).astype(BF16)
    a_row = -jnp.exp(alog_row_ref[...])
    lane = lax.broadcasted_iota(jnp.int32, (q, LANE), 1)
    lane_t = lax.broadcasted_iota(jnp.int32, (2 * SUBLANE, LANE), 1)
    emat = _head_expand_matrix()
    gw = SSD_WIDTH // SSD_GROUPS

    @pl.when(seq_first)
    def _():
        if has_h0:
            st_ref[...] = h0_ref[0]
        else:
            st_ref[...] = jnp.zeros_like(st_ref)

    chunk_dec, chunk_states = [], []
    for c in range(nc):
        sl = slice(c * q, (c + 1) * q)
        xcb = xcb_all[sl]
        xs = xcb[:, :SSD_WIDTH].astype(F32)
        dt = dt_ref[sl, :]
        dta = dt * a_row
        la = _prefix_rows(dta, tril)
        tot = la[q - 1:q, :]
        w = jnp.exp(jnp.where(lane < SSD_HEADS, tot - la, la - dta)) * dt
        w = jnp.where(lane < 2 * SSD_HEADS, w, 0.0)
        wexp = _expand_heads(w, emat)
        etot = jnp.where(lane_t < 2 * SSD_HEADS, jnp.exp(jnp.broadcast_to(tot, (2 * SUBLANE, LANE))), 0.0)
        dec = _expand_heads(etot, emat)[:SUBLANE]
        dec_ref[c] = dec
        bmb = xcb[:, SSD_WIDTH:SSD_WIDTH + SSD_GROUPS * SSD_STATE]
        states = []
        for d in range(2):
            xw = (xs * wexp[:, d * SSD_WIDTH:(d + 1) * SSD_WIDTH]).astype(BF16)
            parts = [_dot_tn(bmb[:, grp * SSD_STATE:(grp + 1) * SSD_STATE], xw[:, grp * gw:(grp + 1) * gw])
                     for grp in range(SSD_GROUPS)]
            states.append(jnp.concatenate(parts, axis=1))
        sb_ref[c] = states[1]
        chunk_dec.append(dec[0:1, :SSD_WIDTH])
        chunk_states.append(states[0])

    hs = st_ref[...]
    for c in range(nc):
        hsf_ref[c] = hs.astype(BF16)
        hs = hs * chunk_dec[c] + chunk_states[c]
    st_ref[...] = hs

    @pl.when(seq_last)
    def _():
        hfin_ref[0, 0] = hs.T


def _ssd_out_kernel(*refs, cps, nc, has_h0, has_sink, nsteps):
    it = iter(refs)
    (xcb_ref, dt_ref, dtt_ref, z_ref, hsf_ref, sb_ref, dec_ref,
     alog_row_ref, alog_col_ref, dskip_ref, gn_ref) = (next(it) for _ in range(11))
    h0_ref = next(it) if has_h0 else None
    if has_sink:
        next(it)
    y_ref, hfin_ref, st_ref = next(it), next(it), next(it)

    q = SSD_CHUNK
    pos = ((nsteps - 1 - pl.program_id(0)) * nc) % cps
    seq_first = pos == 0
    seq_last = pos + nc == cps
    log2e = math.log2(math.e)

    lower, upper = _chunk_masks(q)
    tril = jnp.where(lower, 1.0, 0.0).astype(BF16)
    triu = jnp.where(upper, 1.0, 0.0).astype(BF16)
    a_row = -jnp.exp(alog_row_ref[...])
    a_col = -jnp.exp(alog_col_ref[...])
    lane = lax.broadcasted_iota(jnp.int32, (q, LANE), 1)
    rowi = lax.broadcasted_iota(jnp.int32, (2 * SUBLANE, q), 0)
    emat = _head_expand_matrix()
    rep = SSD_HEADS // SSD_GROUPS
    gw = SSD_WIDTH // SSD_GROUPS
    neg = jnp.float32(-jnp.inf)

    @pl.when(seq_last)
    def _():
        if has_h0:
            st_ref[...] = h0_ref[0]
        else:
            st_ref[...] = jnp.zeros_like(st_ref)

    def chunk_terms(c):
        sl = slice(c * q, (c + 1) * q)
        xcb = xcb_ref[sl, :]
        xsb = xcb[:, :SSD_WIDTH]
        bmb = xcb[:, SSD_WIDTH:SSD_WIDTH + SSD_GROUPS * SSD_STATE]
        cmb = xcb[:, SSD_WIDTH + SSD_GROUPS * SSD_STATE:]
        dt = dt_ref[sl, :]
        dtt = dtt_ref[:, sl]
        dta = dt * a_row
        dtat = dtt * a_col
        la = _prefix_rows(dta, tril)
        tot = la[q - 1:q, :]
        lcol = jnp.where(lane < SSD_HEADS, la, tot - la + dta)
        lat = sum(_dot(p, triu) for p in _split3(dtat))
        tott = lat[:, q - 1:q]
        lrow = jnp.where(rowi < SSD_HEADS, lat, tott - lat + dtat)
        lcol2 = lcol * log2e
        lrow2 = (lrow - jnp.log(dtt)) * log2e
        ecol = jnp.where(lane < 2 * SSD_HEADS, jnp.exp(lcol), 0.0)
        eexp = _expand_heads(ecol, emat)

        cbs = []
        for grp in range(SSD_GROUPS):
            cg = cmb[:, grp * SSD_STATE:(grp + 1) * SSD_STATE]
            bg = bmb[:, grp * SSD_STATE:(grp + 1) * SSD_STATE]
            cbs.append(_dot_nt(cg, bg))
        tiles = []
        for pair in range(SSD_HEADS // 2):
            xpair = xsb[:, pair * LANE:(pair + 1) * LANE]
            res = []
            for hd in (2 * pair, 2 * pair + 1):
                jf, jb = hd, SSD_HEADS + hd
                ef = jnp.exp2(jnp.where(lower, lcol2[:, jf:jf + 1] - lrow2[jf:jf + 1, :], neg))
                eb = jnp.exp2(jnp.where(upper, lcol2[:, jb:jb + 1] - lrow2[jb:jb + 1, :], neg))
                mm = (cbs[hd // rep] * (ef + eb)).astype(BF16)
                res.append(_dot(mm, xpair))
            tiles.append(jnp.where(lane < SSD_HEAD_DIM, res[0], res[1]))
        y = jnp.concatenate(tiles, axis=1)
        hsf = hsf_ref[c]
        parts = [_dot(cmb[:, grp * SSD_STATE:(grp + 1) * SSD_STATE], hsf[:, grp * gw:(grp + 1) * gw])
                 for grp in range(SSD_GROUPS)]
        y = y + jnp.concatenate(parts, axis=1) * eexp[:, :SSD_WIDTH]
        y = y + dskip_ref[...] * xsb.astype(F32)
        return y, cmb, eexp[:, SSD_WIDTH:], _silu(z_ref[sl, :])

    terms = [chunk_terms(c) for c in range(nc)]

    hb = st_ref[...]
    for c in reversed(range(nc)):
        y, cmb, eexp_b, gate = terms[c]
        hsb = hb.astype(BF16)
        parts = [_dot(cmb[:, grp * SSD_STATE:(grp + 1) * SSD_STATE], hsb[:, grp * gw:(grp + 1) * gw])
                 for grp in range(SSD_GROUPS)]
        y = (y + jnp.concatenate(parts, axis=1) * eexp_b) * gate
        y_ref[c * q:(c + 1) * q, :] = _rms_rows(y, gn_ref[...]).astype(BF16)
        hb = hb * dec_ref[c][0:1, SSD_WIDTH:] + sb_ref[c]
    st_ref[...] = hb

    @pl.when(seq_first)
    def _():
        hfin_ref[0, 0] = hb.T


def _ssd(xbc, dt, dtt, z, lw, cps, h0=None, sink=(0, 1, None)):
    slot0, nslots, sink_buf = sink
    t = xbc.shape[0]
    q = SSD_CHUNK
    nc = min(SSD_STEP_CHUNKS, cps)
    assert cps % nc == 0
    rows = nc * q
    nchunks = t // q
    nsteps = nchunks // nc
    spq = cps // nc
    nseq = nchunks // cps
    hb = rows // SUBLANE
    n8 = t // SUBLANE
    has_h0 = h0 is not None
    const = lambda i: (0, 0)
    st_block = (1, SSD_STATE, SSD_WIDTH)
    ch_block = (nc, SSD_STATE, SSD_WIDTH)
    fin_block = (1, 1, SSD_WIDTH, SSD_STATE)
    fin_shape = jax.ShapeDtypeStruct((nseq, nslots, SSD_WIDTH, SSD_STATE), F32)
    any_spec = pl.BlockSpec(memory_space=pl.ANY)
    dec_block = (nc, SUBLANE, 2 * SSD_WIDTH)

    in_specs = [
        pl.BlockSpec((rows, SSD_XBC), lambda i: (i, 0)),
        pl.BlockSpec((SUBLANE, SSD_XBC), lambda i: (jnp.maximum(i * hb - 1, 0), 0)),
        pl.BlockSpec((SUBLANE, SSD_XBC), lambda i: (jnp.minimum((i + 1) * hb, n8 - 1), 0)),
        pl.BlockSpec((rows, LANE), lambda i: (i, 0)),
        pl.BlockSpec((SSD_CONV, SSD_XBC), const),
        pl.BlockSpec((1, SSD_XBC), const),
        pl.BlockSpec((1, LANE), const),
    ]
    args = [xbc, xbc, xbc, dt, lw["conv_w"], lw["conv_b"], lw["alog_row"]]
    if has_h0:
        in_specs.append(pl.BlockSpec(st_block, lambda i: (i // spq, 0, 0)))
        args.append(h0[0])
    aliases = {}
    if sink_buf is not None:
        aliases[len(args)] = 4
        in_specs.append(any_spec)
        args.append(sink_buf)
    xcb, hsf, sb, dec, fin = pl.pallas_call(
        functools.partial(_ssd_state_kernel, cps=cps, nc=nc, has_h0=has_h0, has_sink=sink_buf is not None),
        grid=(nsteps,),
        in_specs=in_specs,
        out_specs=[
            pl.BlockSpec((rows, SSD_XBC), lambda i: (i, 0)),
            pl.BlockSpec(ch_block, lambda i: (i, 0, 0)),
            pl.BlockSpec(ch_block, lambda i: (i, 0, 0)),
            pl.BlockSpec(dec_block, lambda i: (i, 0, 0)),
            pl.BlockSpec(fin_block, lambda i: (i // spq, slot0, 0, 0)),
        ],
        out_shape=[
            jax.ShapeDtypeStruct((t, SSD_XBC), BF16),
            jax.ShapeDtypeStruct((nchunks, SSD_STATE, SSD_WIDTH), BF16),
            jax.ShapeDtypeStruct((nchunks, SSD_STATE, SSD_WIDTH), F32),
            jax.ShapeDtypeStruct((nchunks, SUBLANE, 2 * SSD_WIDTH), F32),
            fin_shape,
        ],
        input_output_aliases=aliases,
        scratch_shapes=[pltpu.VMEM((SSD_STATE, SSD_WIDTH), F32)],
        compiler_params=_cparams("arbitrary"),
        name="ssd_state",
    )(*args)

    gi = lambda i: nsteps - 1 - i
    in_specs = [
        pl.BlockSpec((rows, SSD_XBC), lambda i: (gi(i), 0)),
        pl.BlockSpec((rows, LANE), lambda i: (gi(i), 0)),
        pl.BlockSpec((2 * SSD_HEADS, rows), lambda i: (0, gi(i))),
        pl.BlockSpec((rows, SSD_WIDTH), lambda i: (gi(i), 0)),
        pl.BlockSpec(ch_block, lambda i: (gi(i), 0, 0)),
        pl.BlockSpec(ch_block, lambda i: (gi(i), 0, 0)),
        pl.BlockSpec(dec_block, lambda i: (gi(i), 0, 0)),
        pl.BlockSpec((1, LANE), const),
        pl.BlockSpec((2 * SSD_HEADS, 1), const),
        pl.BlockSpec((1, SSD_WIDTH), const),
        pl.BlockSpec((1, SSD_WIDTH), const),
    ]
    args = [xcb, dt, dtt, z, hsf, sb, dec, lw["alog_row"], lw["alog_col"], lw["dskip_row"], lw["ssd_norm"]]
    if has_h0:
        in_specs.append(pl.BlockSpec(st_block, lambda i: (gi(i) // spq, 0, 0)))
        args.append(h0[1])
    in_specs.append(any_spec)
    args.append(fin)
    y, fin = pl.pallas_call(
        functools.partial(_ssd_out_kernel, cps=cps, nc=nc, has_h0=has_h0, has_sink=True, nsteps=nsteps),
        grid=(nsteps,),
        in_specs=in_specs,
        out_specs=[
            pl.BlockSpec((rows, SSD_WIDTH), lambda i: (gi(i), 0)),
            pl.BlockSpec(fin_block, lambda i: (gi(i) // spq, slot0 + 1, 0, 0)),
        ],
        out_shape=[jax.ShapeDtypeStruct((t, SSD_WIDTH), BF16), fin_shape],
        input_output_aliases={len(args) - 1: 1},
        scratch_shapes=[pltpu.VMEM((SSD_STATE, SSD_WIDTH), F32)],
        compiler_params=_cparams("arbitrary"),
        name="ssd_out",
    )(*args)
    return y, fin


def _state_to_kernel_layout(h):
    n = h.shape[0]
    return h.transpose(0, 3, 1, 2).reshape(n, SSD_STATE, SSD_WIDTH)


def _attn_kernel(*refs, heads, has_cache):
    if has_cache:
        qt_ref, k_ref, vt_ref, kc_ref, vct_ref = refs[:5]
    else:
        qt_ref, k_ref, vt_ref = refs[:3]
    s_refs = refs[-2:]
    o_ref = refs[-3]
    lk = k_ref.shape[0]
    kb = min(ATTN_KEY_BLOCK, lk)
    blocks = [(k_ref, vt_ref, i * kb) for i in range(lk // kb)]
    if has_cache:
        lc = kc_ref.shape[0]
        kbc = min(ATTN_KEY_BLOCK, lc)
        blocks += [(kc_ref, vct_ref, i * kbc) for i in range(lc // kbc)]
        assert kbc == kb
    nblk = len(blocks)
    ones = jnp.ones((ONES_ROWS, kb), BF16)

    def score_block(hd, i, m):
        kr, _, off = blocks[i]
        q = qt_ref[hd * HEAD_PAD:(hd + 1) * HEAD_PAD, :]
        s = _dot(kr[off:off + kb, hd * HEAD_PAD:(hd + 1) * HEAD_PAD], q)
        s_refs[hd % 2][i * kb:(i + 1) * kb, :] = s
        bm = jnp.max(s, axis=0, keepdims=True)
        return bm if m is None else jnp.maximum(m, bm)

    def value_block(hd, i, m, acc):
        _, vr, off = blocks[i]
        p = jnp.exp2((s_refs[hd % 2][i * kb:(i + 1) * kb, :] - m).astype(BF16))
        v = vr[hd * MLA_V_DIM:(hd + 1) * MLA_V_DIM, off:off + kb]
        part = _dot(jnp.concatenate([v, ones], axis=0), p)
        return part if acc is None else acc + part

    m_cur = None
    for i in range(nblk):
        m_cur = score_block(0, i, m_cur)
    for hd in range(heads):
        m_next, acc = None, None
        for i in range(nblk):
            if hd + 1 < heads:
                m_next = score_block(hd + 1, i, m_next)
            acc = value_block(hd, i, m_cur, acc)
        vs = slice(hd * MLA_V_DIM, (hd + 1) * MLA_V_DIM)
        o_ref[vs, :] = acc[:MLA_V_DIM] / acc[MLA_V_DIM:MLA_V_DIM + 1]
        m_cur = m_next


def _attention(qt, k, vt, seq_len, heads_per_step, cache=None):
    t = k.shape[0]
    nseq = t // seq_len
    tq = min(ATTN_Q_TILE, seq_len)
    nq = seq_len // tq
    g = heads_per_step
    in_specs = [
        pl.BlockSpec((g * HEAD_PAD, tq), lambda s, h, j: (h, s * nq + j)),
        pl.BlockSpec((seq_len, g * HEAD_PAD), lambda s, h, j: (s, h)),
        pl.BlockSpec((g * MLA_V_DIM, seq_len), lambda s, h, j: (h, s)),
    ]
    args = [qt, k, vt]
    n_keys = seq_len
    if cache is not None:
        kc, vct = cache
        past = kc.shape[0] // nseq
        n_keys += past
        in_specs += [
            pl.BlockSpec((past, g * HEAD_PAD), lambda s, h, j: (s, h)),
            pl.BlockSpec((g * MLA_V_DIM, past), lambda s, h, j: (h, s)),
        ]
        args += [kc, vct]
    kern = functools.partial(_attn_kernel, heads=g, has_cache=cache is not None)
    return pl.pallas_call(
        kern,
        grid=(nseq, MLA_HEADS // g, nq),
        in_specs=in_specs,
        out_specs=pl.BlockSpec((g * MLA_V_DIM, tq), lambda s, h, j: (h, s * nq + j)),
        out_shape=jax.ShapeDtypeStruct((MLA_WIDTH, t), F32),
        scratch_shapes=[pltpu.VMEM((n_keys, tq), F32), pltpu.VMEM((n_keys, tq), F32)],
        compiler_params=_cparams("arbitrary", "arbitrary", "arbitrary"),
        name="attention",
    )(*args)


def _outproj_kernel(*refs, has_router):
    if has_router:
        (y_ref, ot_ref, x_ref, mod_ref, wa_ref, wb_ref, gm_ref, gpost_ref, gpre2_ref, rt_ref,
         x1_ref, h2_ref, comb_ref) = refs
    else:
        (y_ref, ot_ref, x_ref, mod_ref, wa_ref, wb_ref, gm_ref, gpost_ref, gpre2_ref,
         x1_ref, h2_ref) = refs
    mod = mod_ref[0]
    gate1 = mod[:, 2 * D_MODEL:3 * D_MODEL]
    shift2 = mod[:, 3 * D_MODEL:4 * D_MODEL]
    scale2 = mod[:, 4 * D_MODEL:5 * D_MODEL]
    ot = ot_ref[...]
    ms = jnp.mean(ot * ot, axis=0, keepdims=True)
    on = (ot * lax.rsqrt(ms + EPS) * gm_ref[...]).astype(BF16)
    y = _dot(y_ref[...], wa_ref[...]) + _dot_tn(on, wb_ref[...])
    x1 = x_ref[...] + gate1 * _rms_rows(y, gpost_ref[...])
    x1_ref[...] = x1
    h2 = _rms_rows(x1, gpre2_ref[...]) * (1.0 + scale2) + shift2
    h2_ref[...] = h2.astype(BF16)
    if has_router:
        hh, hm, _ = _split3(h2)
        rh, rm, _ = _split3(rt_ref[...])
        logits = _dot(hh, rh) + (_dot(hm, rh) + _dot(hh, rm))
        lane = lax.broadcasted_iota(jnp.int32, logits.shape, 1).astype(F32)
        neg = jnp.float32(-jnp.inf)
        lg = jnp.where(lane < N_EXPERTS, logits, neg)
        m1 = jnp.max(lg, axis=-1, keepdims=True)
        i1 = jnp.min(jnp.where(lg == m1, lane, float(LANE)), axis=-1, keepdims=True)
        lg2 = jnp.where(lane == i1, neg, lg)
        m2 = jnp.max(lg2, axis=-1, keepdims=True)
        i2 = jnp.min(jnp.where(lg2 == m2, lane, float(LANE)), axis=-1, keepdims=True)
        e2 = jnp.exp(m2 - m1)
        w1 = 1.0 / (1.0 + e2)
        w2 = e2 / (1.0 + e2)
        comb_ref[...] = jnp.where(lane == i1, w1, 0.0) + jnp.where(lane == i2, w2, 0.0)


def _outproj(yssd, ot, x, mod3, mod_row_fn, lw, router=None):
    t = x.shape[0]
    tm = ROW_TILE
    const = lambda i: (0, 0)
    row = lambda i: (i, 0)
    in_specs = [
        pl.BlockSpec((tm, SSD_WIDTH), row),
        pl.BlockSpec((MLA_WIDTH, tm), lambda i: (0, i)),
        pl.BlockSpec((tm, D_MODEL), row),
        pl.BlockSpec((1, 1, N_MOD * D_MODEL), lambda i: (mod_row_fn(ROW_TILE)(i), 0, 0)),
        pl.BlockSpec((SSD_WIDTH, D_MODEL), const),
        pl.BlockSpec((MLA_WIDTH, D_MODEL), const),
        pl.BlockSpec((MLA_WIDTH, 1), const),
        pl.BlockSpec((1, D_MODEL), const),
        pl.BlockSpec((1, D_MODEL), const),
    ]
    args = [yssd, ot, x, mod3, lw["w_out_a"], lw["w_out_b"], lw["mla_norm_col"],
            lw["g_post1"], lw["g_pre2"]]
    out_specs = [pl.BlockSpec((tm, D_MODEL), row), pl.BlockSpec((tm, D_MODEL), row)]
    out_shape = [jax.ShapeDtypeStruct((t, D_MODEL), F32), jax.ShapeDtypeStruct((t, D_MODEL), BF16)]
    if router is not None:
        in_specs.append(pl.BlockSpec((D_MODEL, LANE), const))
        args.append(router)
        out_specs.append(pl.BlockSpec((tm, LANE), row))
        out_shape.append(jax.ShapeDtypeStruct((t, LANE), F32))
    return pl.pallas_call(
        functools.partial(_outproj_kernel, has_router=router is not None),
        grid=(t // tm,),
        in_specs=in_specs,
        out_specs=out_specs,
        out_shape=out_shape,
        compiler_params=_cparams("arbitrary"),
        name="outproj",
    )(*args)


def _ffn_kernel(*refs, has_comb, nslab):
    if has_comb:
        h_ref, x_ref, comb_ref, mod_ref, wgu_ref, wd_ref, gpost_ref, o_ref, acc_ref = refs
    else:
        h_ref, x_ref, mod_ref, wgu_ref, wd_ref, gpost_ref, o_ref, acc_ref = refs
    e = pl.program_id(1)
    h = h_ref[...]
    f = wd_ref.shape[1]
    gu = _dot(h, wgu_ref[0])
    hid = _silu(gu[:, :f]) * gu[:, f:]
    if has_comb:
        comb = comb_ref[...]
        lane = lax.broadcasted_iota(jnp.int32, comb.shape, 1)
        wcol = jnp.sum(jnp.where(lane == e, comb, 0.0), axis=-1, keepdims=True)
        hid = hid * wcol
    part = _dot(hid.astype(BF16), wd_ref[0])

    @pl.when(e == 0)
    def _():
        acc_ref[...] = part

    @pl.when(e > 0)
    def _():
        acc_ref[...] += part

    @pl.when(e == nslab - 1)
    def _():
        gate2 = mod_ref[0][:, 5 * D_MODEL:6 * D_MODEL]
        o_ref[...] = x_ref[...] + gate2 * _rms_rows(acc_ref[...], gpost_ref[...])


def _ffn(h2, x1, mod3, mod_row_fn, wgu, wd, gpost, comb=None):
    t = x1.shape[0]
    tm = FFN_ROW_TILE
    nslab, f, _ = wd.shape
    row = lambda i, e: (i, 0)
    in_specs = [pl.BlockSpec((tm, D_MODEL), row), pl.BlockSpec((tm, D_MODEL), row)]
    args = [h2, x1]
    if comb is not None:
        in_specs.append(pl.BlockSpec((tm, LANE), row))
        args.append(comb)
    in_specs += [
        pl.BlockSpec((1, 1, N_MOD * D_MODEL), lambda i, e: (mod_row_fn(FFN_ROW_TILE)(i), 0, 0)),
        pl.BlockSpec((1, D_MODEL, 2 * f), lambda i, e: (e, 0, 0)),
        pl.BlockSpec((1, f, D_MODEL), lambda i, e: (e, 0, 0)),
        pl.BlockSpec((1, D_MODEL), lambda i, e: (0, 0)),
    ]
    args += [mod3, wgu, wd, gpost]
    return pl.pallas_call(
        functools.partial(_ffn_kernel, has_comb=comb is not None, nslab=nslab),
        grid=(t // tm, nslab),
        in_specs=in_specs,
        out_specs=pl.BlockSpec((tm, D_MODEL), row),
        out_shape=jax.ShapeDtypeStruct((t, D_MODEL), F32),
        scratch_shapes=[pltpu.VMEM((tm, D_MODEL), F32)],
        compiler_params=_cparams("arbitrary", "arbitrary"),
        name="ffn",
    )(*args)


def _moe_kernel(h_ref, x_ref, comb_ref, mod_ref, wg_ref, wu_ref, wd_ref, gpost_ref, o_ref,
                acc_ref, rank_ref, rank_t_ref, comb_t_ref, *, nexp, sub):
    e = pl.program_id(1)
    tm = h_ref.shape[0]
    caps = MOE_CAPS
    cmax = caps[-1]

    @pl.when(e == 0)
    def _():
        acc_ref[...] = jnp.zeros_like(acc_ref)
        r_i = lax.broadcasted_iota(jnp.int32, (sub, sub), 0)
        c_i = lax.broadcasted_iota(jnp.int32, (sub, sub), 1)
        strict = jnp.where(r_i > c_i, 1.0, 0.0).astype(BF16)
        for s in range(tm // sub):
            rows = slice(s * sub, (s + 1) * sub)
            comb = comb_ref[rows, :]
            rank = _dot(strict, jnp.where(comb > 0.0, 1.0, 0.0).astype(BF16))
            rank_ref[rows, :] = rank
            rank_t_ref[:, rows] = rank.T
            comb_t_ref[:, rows] = comb.T

    def expert_pass(rows, wcol, rcol, wrow, rrow, base, cap):
        capl = -(-cap // LANE) * LANE
        slot_l = lax.broadcasted_iota(jnp.int32, (sub, capl), 1).astype(F32)
        slot_s = lax.broadcasted_iota(jnp.int32, (cap, sub), 0).astype(F32)
        gather = jnp.where(((rrow - base) == slot_s) & (wrow > 0.0), 1.0, 0.0).astype(BF16)
        scatter = jnp.where(((rcol - base) == slot_l) & (wcol > 0.0) & (slot_l < float(cap)),
                            1.0, 0.0).astype(BF16)
        xg = _dot(gather, h_ref[rows, :]).astype(BF16)
        hid = _silu(_dot(xg, wg_ref[0])) * _dot(xg, wu_ref[0])
        y = _dot(hid.astype(BF16), wd_ref[0]).astype(BF16)
        if capl > cap:
            y = jnp.concatenate([y, jnp.zeros((capl - cap, y.shape[1]), BF16)], axis=0)
        acc_ref[rows, :] += wcol * _dot(scatter, y)

    lane = lax.broadcasted_iota(jnp.int32, (sub, LANE), 1)

    def sub_tile(s, carry):
        rows = pl.ds(pl.multiple_of(s * sub, sub), sub)
        pick = lane == e
        wcol = jnp.sum(jnp.where(pick, comb_ref[rows, :], 0.0), axis=-1, keepdims=True)
        rcol = jnp.sum(jnp.where(pick, rank_ref[rows, :], 0.0), axis=-1, keepdims=True)
        wrow = comb_t_ref[pl.ds(e, 1), rows]
        rrow = rank_t_ref[pl.ds(e, 1), rows]
        count = jnp.max(jnp.where(wrow > 0.0, rrow + 1.0, 0.0))
        npass = ((count + (cmax - 1.0)) * (1.0 / cmax)).astype(jnp.int32)
        nfull = jnp.maximum(npass - 1, 0)

        def full_pass(k, c):
            expert_pass(rows, wcol, rcol, wrow, rrow, (k * cmax).astype(F32), cmax)
            return c

        lax.fori_loop(0, nfull, full_pass, 0)
        base = (nfull * cmax).astype(F32)
        left = count - base
        lo = 0
        for cap in caps:
            @pl.when(jnp.logical_and(left > float(lo), left <= float(cap)))
            def _(cap=cap):
                expert_pass(rows, wcol, rcol, wrow, rrow, base, cap)
            lo = cap
        return carry

    lax.fori_loop(0, tm // sub, sub_tile, 0)

    @pl.when(e == nexp - 1)
    def _():
        gate2 = mod_ref[0][:, 5 * D_MODEL:6 * D_MODEL]
        o_ref[...] = x_ref[...] + gate2 * _rms_rows(acc_ref[...], gpost_ref[...])


def _moe(h2, x1, comb, mod3, mod_row_fn, wg, wu, wd, gpost):
    t = x1.shape[0]
    tm = min(MOE_ROW_TILE, t)
    sub = min(MOE_SUB_TILE, tm)
    nexp, f, _ = wd.shape
    row = lambda i, e: (i, 0)
    return pl.pallas_call(
        functools.partial(_moe_kernel, nexp=nexp, sub=sub),
        grid=(t // tm, nexp),
        in_specs=[
            pl.BlockSpec((tm, D_MODEL), row),
            pl.BlockSpec((tm, D_MODEL), row),
            pl.BlockSpec((tm, LANE), row),
            pl.BlockSpec((1, 1, N_MOD * D_MODEL), lambda i, e: (mod_row_fn(tm)(i), 0, 0)),
            pl.BlockSpec((1, D_MODEL, f), lambda i, e: (e, 0, 0)),
            pl.BlockSpec((1, D_MODEL, f), lambda i, e: (e, 0, 0)),
            pl.BlockSpec((1, f, D_MODEL), lambda i, e: (e, 0, 0)),
            pl.BlockSpec((1, D_MODEL), lambda i, e: (0, 0)),
        ],
        out_specs=pl.BlockSpec((tm, D_MODEL), row),
        out_shape=jax.ShapeDtypeStruct((t, D_MODEL), F32),
        scratch_shapes=[pltpu.VMEM((tm, D_MODEL), F32), pltpu.VMEM((tm, LANE), F32),
                        pltpu.VMEM((LANE, tm), F32), pltpu.VMEM((LANE, tm), F32)],
        compiler_params=_cparams("arbitrary", "arbitrary"),
        name="moe",
    )(h2, x1, comb, mod3, wg, wu, wd, gpost)


def _prep_layer(i, p):
    w_in = p["w_in"][i]
    s1 = SSD_WIDTH
    s2 = s1 + SSD_XBC
    s3 = s2 + 2 * SSD_HEADS
    s4 = s3 + MLA_Q_RANK
    s5 = s4 + MLA_KV_RANK
    w_z, w_xbc, w_dt, w_cq, w_ckv, w_kr = (w_in[:, :s1], w_in[:, s1:s2], w_in[:, s2:s3],
                                             w_in[:, s3:s4], w_in[:, s4:s5], w_in[:, s5:])
    zc = lambda n: jnp.zeros((D_MODEL, n), F32)
    tile_a = jnp.concatenate([w_dt, zc(ROPE_LANE0 - 2 * SSD_HEADS), w_kr,
                              zc(LANE - ROPE_LANE0 - MLA_ROPE_DIM)], axis=1)
    w_in_pad = jnp.concatenate([w_z, w_xbc, w_cq, w_ckv, tile_a], axis=1).astype(BF16)

    w_uq = p["w_uq"][i].reshape(MLA_Q_RANK, MLA_HEADS, MLA_NOPE_DIM + MLA_ROPE_DIM)
    q_nope, q_rope = w_uq[..., :MLA_NOPE_DIM], w_uq[..., MLA_NOPE_DIM:]
    zq = lambda n: jnp.zeros((MLA_Q_RANK, MLA_HEADS, n), F32)
    pad = HEAD_PAD - MLA_NOPE_DIM - MLA_ROPE_DIM
    w_q = jnp.concatenate([q_nope, q_rope, zq(pad)], axis=-1).reshape(MLA_Q_RANK, -1)

    w_ukv = p["w_ukv"][i].reshape(MLA_KV_RANK, MLA_HEADS, MLA_NOPE_DIM + MLA_V_DIM)
    k_nope, v_w = w_ukv[..., :MLA_NOPE_DIM], w_ukv[..., MLA_NOPE_DIM:]
    w_uk = jnp.concatenate([k_nope, jnp.zeros((MLA_KV_RANK, MLA_HEADS, HEAD_PAD - MLA_NOPE_DIM), F32)],
                           axis=-1).reshape(MLA_KV_RANK, -1)
    w_uv = v_w.reshape(MLA_KV_RANK, MLA_WIDTH)

    dtb = p["dt_bias"][i].reshape(2 * SSD_HEADS)
    alog = p["a_log"][i].reshape(2 * SSD_HEADS)
    padl = lambda v: jnp.pad(v, (0, LANE - v.shape[0])).reshape(1, LANE)
    w_out = p["w_out"][i]
    return {
        "g_pre1": p["norm_pre_mix"][i].reshape(1, D_MODEL),
        "g_post1": p["norm_post_mix"][i].reshape(1, D_MODEL),
        "g_pre2": p["norm_pre_ffn"][i].reshape(1, D_MODEL),
        "g_post2": p["norm_post_ffn"][i].reshape(1, D_MODEL),
        "w_in": w_in_pad,
        "w_dt": w_dt.T.astype(BF16),
        "dtb_row": padl(dtb),
        "dtb_col": dtb.reshape(-1, 1),
        "alog_row": padl(alog),
        "alog_col": alog.reshape(-1, 1),
        "q_norm": p["q_norm"][i].reshape(1, -1),
        "w_q": w_q.T.astype(BF16),
        "kv_norm": p["kv_norm"][i].reshape(1, -1),
        "w_uk": w_uk.astype(BF16),
        "w_uv": w_uv.T.astype(BF16),
        "conv_w": p["conv_w"][i],
        "conv_b": p["conv_b"][i].reshape(1, -1),
        "dskip_row": jnp.repeat(p["d_skip"][i], SSD_HEAD_DIM).reshape(1, -1),
        "ssd_norm": p["ssd_norm"][i].reshape(1, -1),
        "mla_norm_col": p["mla_norm"][i].reshape(-1, 1),
        "w_out_a": w_out[:SSD_WIDTH].astype(BF16),
        "w_out_b": w_out[SSD_WIDTH:].astype(BF16),
    }


def _rope_tables(n_tokens):
    rows = n_tokens // GRID_W
    row = np.repeat(np.arange(rows, dtype=np.float32), GRID_W)
    col = np.tile(np.arange(GRID_W, dtype=np.float32), rows)
    half = MLA_ROPE_DIM // 2
    inv = (np.float32(ROPE_THETA) ** (-np.arange(0, half, 2, dtype=np.float32) / np.float32(half))).astype(np.float32)
    ar = row[:, None] * inv[None, :]
    ac = col[:, None] * inv[None, :]
    ang = np.concatenate([ar, ar, ac, ac], axis=-1).astype(np.float32)
    return jnp.asarray(np.cos(ang), F32), jnp.asarray(np.sin(ang), F32)


def _attn_tables(cos, sin, n):
    scale = (MLA_NOPE_DIM + MLA_ROPE_DIM) ** -0.5 * math.log2(math.e)
    pad = HEAD_PAD - ROPE_LANE0 - MLA_ROPE_DIM
    cosk = jnp.concatenate([jnp.zeros((n, ROPE_LANE0), F32), cos, jnp.zeros((n, pad), F32)], axis=1)
    sink = jnp.concatenate([jnp.zeros((n, ROPE_LANE0), F32), sin, jnp.zeros((n, pad), F32)], axis=1)
    cosq = jnp.concatenate([jnp.ones((n, ROPE_LANE0), F32), cos, jnp.zeros((n, pad), F32)], axis=1)
    return (cosq * scale).T, (sink * scale).T, cosk, sink


def kernel(x_prompt, x_sample, cache_ckv, cache_krope, state_ssm, c, c_ctx, w_mod, b_mod, norm_pre_mix, norm_post_mix, norm_pre_ffn, norm_post_ffn, w_in, conv_w, conv_b, dt_bias, a_log, d_skip, ssd_norm, q_norm, w_uq, kv_norm, w_ukv, mla_norm, w_out, ffn_w_gate, ffn_w_up, ffn_w_down, moe_router, moe_w_gate, moe_w_up, moe_w_down):
    params = dict(w_in=w_in, conv_w=conv_w, conv_b=conv_b, dt_bias=dt_bias, a_log=a_log, d_skip=d_skip,
                  ssd_norm=ssd_norm, q_norm=q_norm, w_uq=w_uq, kv_norm=kv_norm, w_ukv=w_ukv,
                  mla_norm=mla_norm, w_out=w_out, norm_pre_mix=norm_pre_mix, norm_post_mix=norm_post_mix,
                  norm_pre_ffn=norm_pre_ffn, norm_post_ffn=norm_post_ffn)
    batch, seq, d = x_prompt.shape
    dec_batch, dec_seq, _ = x_sample.shape
    depth = w_in.shape[0]
    past = cache_ckv.shape[2]
    tm = ROW_TILE

    cvec = jnp.concatenate([c_ctx[None, :], c, jnp.zeros((SUBLANE - 1 - dec_batch, d), F32)], axis=0)
    mod = _modulation(cvec, w_mod, b_mod)

    ones = jnp.ones((tm, MLA_ROPE_DIM), F32)
    tabs_ctx = _attn_tables(ones, jnp.zeros_like(ones), tm)
    cos, sin = _rope_tables(dec_seq)
    tabs_lat = _attn_tables(cos, sin, dec_seq)
    lat_blocks = dec_seq // tm

    xp = x_prompt.reshape(batch * seq, d)
    xs = x_sample.reshape(dec_batch * dec_seq, d)
    cache_bufs, ssm_buf = None, None
    for i in range(depth):
        lw = _prep_layer(i, params)
        mod3 = mod[i].reshape(SUBLANE, 1, N_MOD * d)
        j = i // 2
        if i % 2 == 0:
            f = ffn_w_gate.shape[2] // 2
            wgu = jnp.stack([jnp.concatenate([ffn_w_gate[j][:, s * f:(s + 1) * f], ffn_w_up[j][:, s * f:(s + 1) * f]],
                                             axis=1) for s in range(2)], axis=0).astype(BF16)
            wd = ffn_w_down[j].reshape(2, f, d).astype(BF16)
            router = None
        else:
            wgu = (moe_w_gate[j].astype(BF16), moe_w_up[j].astype(BF16))
            wd = moe_w_down[j].astype(BF16)
            router = jnp.pad(moe_router[j], ((0, 0), (0, LANE - N_EXPERTS)))

        def run(x, row_fn, tabs, tab_blocks, seq_len, heads_per_step, ctx):
            cache_out = (i, depth, seq_len, cache_bufs) if ctx is None else None
            z, xbc, dt, dtt, qt, k, vt, *caches = _inproj(x, mod3, row_fn, lw, tabs, tab_blocks, cache_out)
            cps = seq_len // SSD_CHUNK
            h0 = None
            if ctx is not None:
                h0 = (_state_to_kernel_layout(ctx[2][:, 0]), _state_to_kernel_layout(ctx[2][:, 1]))
            sink = (2 * i, 2 * depth, ssm_buf) if ctx is None else (0, 2, None)
            yssd, fin = _ssd(xbc, dt, dtt, z, lw, cps, h0=h0, sink=sink)
            cache = None
            if ctx is not None:
                kr_tile = jnp.pad(ctx[1].reshape(-1, MLA_ROPE_DIM),
                                  ((0, 0), (ROPE_LANE0, HEAD_PAD - ROPE_LANE0 - MLA_ROPE_DIM)))
                cache = _kvcache(ctx[0].reshape(-1, MLA_KV_RANK), kr_tile, lw)
            ot = _attention(qt, k, vt, seq_len, heads_per_step, cache=cache)
            outs = _outproj(yssd, ot, x, mod3, row_fn, lw, router=router)
            x1, h2 = outs[0], outs[1]
            comb = outs[2] if router is not None else None
            if comb is None:
                x2 = _ffn(h2, x1, mod3, row_fn, wgu, wd, lw["g_post2"])
            else:
                x2 = _moe(h2, x1, comb, mod3, row_fn, *wgu, wd, lw["g_post2"])
            return x2, caches, fin

        xp, cache_bufs, ssm_buf = run(xp, lambda tile: (lambda b: 0), tabs_ctx, 1, seq, MLA_HEADS, None)
        xs, _, _ = run(xs, lambda tile: (lambda b: 1 + (b * tile) // dec_seq), tabs_lat, lat_blocks, dec_seq, MLA_HEADS,
                             (cache_ckv[:, i], cache_krope[:, i], state_ssm[:, i]))
    return (xp.reshape(batch, seq, d), xs.reshape(dec_batch, dec_seq, d),
            cache_bufs[0], cache_bufs[1],
            ssm_buf.reshape(batch, depth, 2, SSD_HEADS, SSD_HEAD_DIM, SSD_STATE))
```

```python
import functools
import math

import jax
import jax.numpy as jnp
import numpy as np
from jax import lax
from jax.experimental import pallas as pl
from jax.experimental.pallas import tpu as pltpu

F32 = jnp.float32
BF16 = jnp.bfloat16

D_MODEL = 1024
GRID_W = 64
SSD_WIDTH = 512
SSD_HEAD_DIM = 64
SSD_HEADS = 8
SSD_GROUPS = 2
SSD_STATE = 64
SSD_CONV = 5
SSD_CHUNK = 128
SSD_STEP_CHUNKS = 4
SSD_XBC = SSD_WIDTH + 2 * SSD_GROUPS * SSD_STATE
MLA_WIDTH = 512
MLA_V_DIM = 64
MLA_HEADS = 8
MLA_NOPE_DIM = 64
MLA_ROPE_DIM = 32
MLA_Q_RANK = 384
MLA_KV_RANK = 256
ROPE_THETA = 10000.0
N_EXPERTS = 8
N_MOD = 6
EPS = 1e-6

LANE = 128
SUBLANE = 8
HEAD_PAD = 128
ONES_ROWS = 16
ROPE_LANE0 = MLA_NOPE_DIM
C_Z = 0
C_XBC = C_Z + SSD_WIDTH
C_CQ = C_XBC + SSD_XBC
C_CKV = C_CQ + MLA_Q_RANK
C_TA = C_CKV + MLA_KV_RANK
IN_PAD = C_TA + LANE
ROT_GROUP = MLA_ROPE_DIM // 4

VMEM_LIMIT = 56 * 1024 * 1024

ROW_TILE = 512
ATTN_Q_TILE = 256
ATTN_KEY_BLOCK = 512
ATTN_SEQS_PER_STEP = 2
FFN_ROW_TILE = 512
MOE_ROW_TILE = 1024
MOE_SUB_TILE = 512
MOE_CAPS = (64, 96, 128, 160, 192, 224, 256)

NT_DIMS = (((1,), (1,)), ((), ()))
TN_DIMS = (((0,), (0,)), ((), ()))


def _cparams(*sem):
    return pltpu.CompilerParams(dimension_semantics=sem, vmem_limit_bytes=VMEM_LIMIT)


def _silu(x):
    return x / (1.0 + jnp.exp(-x))


def _softplus(x):
    return jnp.maximum(x, 0.0) + jnp.log(1.0 + jnp.exp(-jnp.abs(x)))


def _rms_rows(x, g):
    ms = jnp.mean(x * x, axis=-1, keepdims=True)
    return x * lax.rsqrt(ms + EPS) * g


def _dot(a, b):
    return jnp.dot(a, b, preferred_element_type=F32)


def _dot_nt(a, b):
    return lax.dot_general(a, b, NT_DIMS, preferred_element_type=F32)


def _dot_tn(a, b):
    return lax.dot_general(a, b, TN_DIMS, preferred_element_type=F32)


def _split3(x):
    hi = x.astype(BF16)
    r1 = x - hi.astype(F32)
    mid = r1.astype(BF16)
    lo = (r1 - mid.astype(F32)).astype(BF16)
    return hi, mid, lo


def _mod_kernel(c_ref, w_ref, b_ref, o_ref):
    s = _silu(c_ref[...]).astype(BF16)
    o_ref[0] = _dot(s, w_ref[0].astype(BF16)) + b_ref[0]


def _modulation(cvec, w_mod, b_mod):
    depth, d, n = w_mod.shape
    tn = 1536
    return pl.pallas_call(
        _mod_kernel,
        grid=(depth, n // tn),
        in_specs=[
            pl.BlockSpec((SUBLANE, d), lambda l, j: (0, 0)),
            pl.BlockSpec((1, d, tn), lambda l, j: (l, 0, j)),
            pl.BlockSpec((1, 1, tn), lambda l, j: (l, 0, j)),
        ],
        out_specs=pl.BlockSpec((1, SUBLANE, tn), lambda l, j: (l, 0, j)),
        out_shape=jax.ShapeDtypeStruct((depth, SUBLANE, n), F32),
        compiler_params=_cparams("arbitrary", "arbitrary"),
        name="modulation",
    )(cvec, w_mod, b_mod.reshape(depth, 1, n))


def _inproj_kernel(*refs, emit_cache, n_alias):
    (x_ref, mod_ref, gpre_ref, win_ref, wdt_ref, dtb_row_ref, dtb_col_ref, qn_ref, wq_ref, kvn_ref, wuk_ref,
     wuv_ref, cosq_ref, sinq_ref, cosk_ref, sink_ref) = refs[:16]
    outs = refs[16 + n_alias:]
    z_ref, xbc_ref, dt_ref, dtt_ref, qt_ref, k_ref, vt_ref = outs[:7]
    mod = mod_ref[0]
    shift = mod[:, 0:D_MODEL]
    scale = mod[:, D_MODEL:2 * D_MODEL]
    h = _rms_rows(x_ref[...], gpre_ref[...]) * (1.0 + scale) + shift
    hb = h.astype(BF16)
    proj = _dot(hb, win_ref[...])
    z_ref[...] = proj[:, C_Z:C_XBC]
    xbc_ref[...] = proj[:, C_XBC:C_CQ]
    cqn = _rms_rows(proj[:, C_CQ:C_CKV], qn_ref[...]).astype(BF16)
    ckvn = _rms_rows(proj[:, C_CKV:C_TA], kvn_ref[...])
    if emit_cache:
        ckvn_ref, kr_ref = outs[7:]
        nb, _, sq, _ = ckvn_ref.shape
        ckvn_ref[:, 0] = ckvn.reshape(nb, sq, MLA_KV_RANK)
    ckvb = ckvn.astype(BF16)
    ta = proj[:, C_TA:IN_PAD]
    dt_ref[...] = _softplus(ta + dtb_row_ref[...])
    if emit_cache:
        kr_ref[:, 0] = ta[:, ROPE_LANE0:ROPE_LANE0 + MLA_ROPE_DIM].reshape(nb, sq, MLA_ROPE_DIM)
    lane = lax.broadcasted_iota(jnp.int32, ta.shape, 1)
    first = (lane // ROT_GROUP) % 2 == 0
    rot = jnp.where(first, -pltpu.roll(ta, LANE - ROT_GROUP, 1), pltpu.roll(ta, ROT_GROUP, 1))
    kr_rot = ta * cosk_ref[...] + rot * sink_ref[...]
    knp = _dot(ckvb, wuk_ref[...])
    for hd in range(MLA_HEADS):
        sl = slice(hd * HEAD_PAD, (hd + 1) * HEAD_PAD)
        k_ref[:, sl] = (knp[:, sl] + kr_rot).astype(BF16)
    vt_ref[...] = _dot_nt(wuv_ref[...], ckvb).astype(BF16)
    qt = _dot_nt(wq_ref[...], cqn)
    cosq = cosq_ref[...]
    sinq = sinq_ref[...]
    g, r0 = ROT_GROUP, ROPE_LANE0
    for hd in range(MLA_HEADS):
        blk = qt[hd * HEAD_PAD:(hd + 1) * HEAD_PAD, :]
        rot = jnp.concatenate([blk[:r0], -blk[r0 + g:r0 + 2 * g], blk[r0:r0 + g], -blk[r0 + 3 * g:r0 + 4 * g],
                               blk[r0 + 2 * g:r0 + 3 * g], blk[r0 + 4 * g:]], axis=0)
        qt_ref[hd * HEAD_PAD:(hd + 1) * HEAD_PAD, :] = (blk * cosq + rot * sinq).astype(BF16)
    dtt_ref[...] = _softplus(_dot_nt(wdt_ref[...], hb) + dtb_col_ref[...])


def _inproj(x, mod3, mod_row_fn, lw, tabs, tab_blocks, cache_out=None):
    t = x.shape[0]
    tm = ROW_TILE
    nb = t // tm
    cosq, sinq, cosk, sink = tabs
    ntab = tab_blocks
    const = lambda i: (0, 0)
    row = lambda i: (i, 0)
    col = lambda i: (0, i)
    in_specs = [
        pl.BlockSpec((tm, D_MODEL), row),
        pl.BlockSpec((1, 1, N_MOD * D_MODEL), lambda i: (mod_row_fn(ROW_TILE)(i), 0, 0)),
        pl.BlockSpec((1, D_MODEL), const),
        pl.BlockSpec((D_MODEL, IN_PAD), const),
        pl.BlockSpec((2 * SSD_HEADS, D_MODEL), const),
        pl.BlockSpec((1, LANE), const),
        pl.BlockSpec((2 * SSD_HEADS, 1), const),
        pl.BlockSpec((1, MLA_Q_RANK), const),
        pl.BlockSpec((MLA_HEADS * HEAD_PAD, MLA_Q_RANK), const),
        pl.BlockSpec((1, MLA_KV_RANK), const),
        pl.BlockSpec((MLA_KV_RANK, MLA_HEADS * HEAD_PAD), const),
        pl.BlockSpec((MLA_WIDTH, MLA_KV_RANK), const),
        pl.BlockSpec((HEAD_PAD, tm), lambda i: (0, i % ntab)),
        pl.BlockSpec((HEAD_PAD, tm), lambda i: (0, i % ntab)),
        pl.BlockSpec((tm, LANE), lambda i: (i % ntab, 0)),
        pl.BlockSpec((tm, LANE), lambda i: (i % ntab, 0)),
    ]
    out_specs = [
        pl.BlockSpec((tm, SSD_WIDTH), row),
        pl.BlockSpec((tm, SSD_XBC), row),
        pl.BlockSpec((tm, LANE), row),
        pl.BlockSpec((2 * SSD_HEADS, tm), col),
        pl.BlockSpec((MLA_HEADS * HEAD_PAD, tm), col),
        pl.BlockSpec((tm, MLA_HEADS * HEAD_PAD), row),
        pl.BlockSpec((MLA_WIDTH, tm), col),
    ]
    out_shape = [
        jax.ShapeDtypeStruct((t, SSD_WIDTH), F32),
        jax.ShapeDtypeStruct((t, SSD_XBC), F32),
        jax.ShapeDtypeStruct((t, LANE), F32),
        jax.ShapeDtypeStruct((2 * SSD_HEADS, t), F32),
        jax.ShapeDtypeStruct((MLA_HEADS * HEAD_PAD, t), BF16),
        jax.ShapeDtypeStruct((t, MLA_HEADS * HEAD_PAD), BF16),
        jax.ShapeDtypeStruct((MLA_WIDTH, t), BF16),
    ]
    args = [x, mod3, lw["g_pre1"], lw["w_in"], lw["w_dt"], lw["dtb_row"], lw["dtb_col"],
            lw["q_norm"], lw["w_q"], lw["kv_norm"], lw["w_uk"], lw["w_uv"], cosq, sinq, cosk, sink]
    aliases = {}
    n_alias = 0
    if cache_out is not None:
        layer, depth, seq, bufs = cache_out
        nb_seq = tm // seq
        for rank in (MLA_KV_RANK, MLA_ROPE_DIM):
            out_specs.append(pl.BlockSpec((nb_seq, 1, seq, rank), lambda i: (i, layer, 0, 0)))
            out_shape.append(jax.ShapeDtypeStruct((t // seq, depth, seq, rank), F32))
        if bufs is not None:
            n_alias = len(bufs)
            for j, buf in enumerate(bufs):
                aliases[len(args)] = len(out_shape) - n_alias + j
                in_specs.append(pl.BlockSpec(memory_space=pl.ANY))
                args.append(buf)
    return pl.pallas_call(
        functools.partial(_inproj_kernel, emit_cache=cache_out is not None, n_alias=n_alias),
        grid=(nb,),
        in_specs=in_specs,
        out_specs=out_specs,
        out_shape=out_shape,
        input_output_aliases=aliases,
        compiler_params=_cparams("arbitrary"),
        name="inproj",
    )(*args)


def _kvcache_kernel(ckv_ref, kr_ref, wuk_ref, wuv_ref, k_ref, vt_ref):
    ckvb = ckv_ref[...].astype(BF16)
    knp = _dot(ckvb, wuk_ref[...])
    kr = kr_ref[...]
    for hd in range(MLA_HEADS):
        sl = slice(hd * HEAD_PAD, (hd + 1) * HEAD_PAD)
        k_ref[:, sl] = (knp[:, sl] + kr).astype(BF16)
    vt_ref[...] = _dot_nt(wuv_ref[...], ckvb).astype(BF16)


def _kvcache(ckv, kr_tile, lw):
    n = ckv.shape[0]
    tm = 512
    return pl.pallas_call(
        _kvcache_kernel,
        grid=(n // tm,),
        in_specs=[
            pl.BlockSpec((tm, MLA_KV_RANK), lambda i: (i, 0)),
            pl.BlockSpec((tm, LANE), lambda i: (i, 0)),
            pl.BlockSpec((MLA_KV_RANK, MLA_HEADS * HEAD_PAD), lambda i: (0, 0)),
            pl.BlockSpec((MLA_WIDTH, MLA_KV_RANK), lambda i: (0, 0)),
        ],
        out_specs=[
            pl.BlockSpec((tm, MLA_HEADS * HEAD_PAD), lambda i: (i, 0)),
            pl.BlockSpec((MLA_WIDTH, tm), lambda i: (0, i)),
        ],
        out_shape=[
            jax.ShapeDtypeStruct((n, MLA_HEADS * HEAD_PAD), BF16),
            jax.ShapeDtypeStruct((MLA_WIDTH, n), BF16),
        ],
        compiler_params=_cparams("arbitrary"),
        name="kvcache",
    )(ckv, kr_tile, lw["w_uk"], lw["w_uv"])


def _head_expand_matrix():
    r = lax.broadcasted_iota(jnp.int32, (LANE, 2 * SSD_WIDTH), 0)
    c = lax.broadcasted_iota(jnp.int32, (LANE, 2 * SSD_WIDTH), 1)
    return jnp.where(c // SSD_HEAD_DIM == r, 1.0, 0.0).astype(BF16)


def _expand_heads(v, emat):
    hi = v.astype(BF16)
    mid = (v - hi.astype(F32)).astype(BF16)
    return _dot(hi, emat) + _dot(mid, emat)


def _prefix_rows(dta, tril):
    return sum(_dot(tril, p) for p in _split3(dta))


def _chunk_masks(q):
    r_i = lax.broadcasted_iota(jnp.int32, (q, q), 0)
    c_i = lax.broadcasted_iota(jnp.int32, (q, q), 1)
    return r_i >= c_i, r_i <= c_i


def _ssd_state_kernel(*refs, cps, nc, has_h0, has_sink):
    it = iter(refs)
    xbc_ref, prev_ref, next_ref, dt_ref, cw_ref, cb_ref, alog_row_ref = (next(it) for _ in range(7))
    h0_ref = next(it) if has_h0 else None
    if has_sink:
        next(it)
    xcb_ref, hsf_ref, sb_ref, dec_ref, hfin_ref, st_ref = (next(it) for _ in range(6))

    q = SSD_CHUNK
    rows = nc * q
    pos = (pl.program_id(0) * nc) % cps
    seq_first = pos == 0
    seq_last = pos + nc == cps

    prev = jnp.where(seq_first, 0.0, prev_ref[...])
    nxt = jnp.where(seq_last, 0.0, next_ref[...])
    ext = jnp.concatenate([prev, xbc_ref[...], nxt], axis=0)
    cw = cw_ref[...]
    acc = cb_ref[...] + ext[SUBLANE - 2:SUBLANE - 2 + rows] * cw[0:1]
    for k in range(1, SSD_CONV):
        o = SUBLANE - 2 + k
        acc = acc + ext[o:o + rows] * cw[k:k + 1]
    xcb_all = _silu(acc).astype(BF16)
    xcb_ref[...] = xcb_all

    lower, _ = _chunk_masks(q)
    tril = jnp.where(lower, 1.0, 0.0).astype(BF16)
    a_row = -jnp.exp(alog_row_ref[...])
    lane = lax.broadcasted_iota(jnp.int32, (q, LANE), 1)
    lane_t = lax.broadcasted_iota(jnp.int32, (2 * SUBLANE, LANE), 1)
    emat = _head_expand_matrix()
    gw = SSD_WIDTH // SSD_GROUPS

    @pl.when(seq_first)
    def _():
        if has_h0:
            st_ref[...] = h0_ref[0]
        else:
            st_ref[...] = jnp.zeros_like(st_ref)

    chunk_dec, chunk_states = [], []
    for c in range(nc):
        sl = slice(c * q, (c + 1) * q)
        xcb = xcb_all[sl]
        xs = xcb[:, :SSD_WIDTH].astype(F32)
        dt = dt_ref[sl, :]
        dta = dt * a_row
        la = _prefix_rows(dta, tril)
        tot = la[q - 1:q, :]
        w = jnp.exp(jnp.where(lane < SSD_HEADS, tot - la, la - dta)) * dt
        w = jnp.where(lane < 2 * SSD_HEADS, w, 0.0)
        wexp = _expand_heads(w, emat)
        etot = jnp.where(lane_t < 2 * SSD_HEADS, jnp.exp(jnp.broadcast_to(tot, (2 * SUBLANE, LANE))), 0.0)
        dec = _expand_heads(etot, emat)[:SUBLANE]
        dec_ref[c] = dec
        bmb = xcb[:, SSD_WIDTH:SSD_WIDTH + SSD_GROUPS * SSD_STATE]
        states = []
        for d in range(2):
            xw = (xs * wexp[:, d * SSD_WIDTH:(d + 1) * SSD_WIDTH]).astype(BF16)
            parts = [_dot_tn(bmb[:, grp * SSD_STATE:(grp + 1) * SSD_STATE], xw[:, grp * gw:(grp + 1) * gw])
                     for grp in range(SSD_GROUPS)]
            states.append(jnp.concatenate(parts, axis=1))
        sb_ref[c] = states[1]
        chunk_dec.append(dec[0:1, :SSD_WIDTH])
        chunk_states.append(states[0])

    hs = st_ref[...]
    for c in range(nc):
        hsf_ref[c] = hs.astype(BF16)
        hs = hs * chunk_dec[c] + chunk_states[c]
    st_ref[...] = hs

    @pl.when(seq_last)
    def _():
        hfin_ref[0, 0] = hs.T


def _ssd_out_kernel(*refs, cps, nc, has_h0, has_sink, nsteps):
    it = iter(refs)
    (xcb_ref, dt_ref, dtt_ref, z_ref, hsf_ref, sb_ref, dec_ref,
     alog_row_ref, alog_col_ref, dskip_ref, gn_ref) = (next(it) for _ in range(11))
    h0_ref = next(it) if has_h0 else None
    if has_sink:
        next(it)
    y_ref, hfin_ref, st_ref = next(it), next(it), next(it)

    q = SSD_CHUNK
    pos = ((nsteps - 1 - pl.program_id(0)) * nc) % cps
    seq_first = pos == 0
    seq_last = pos + nc == cps
    log2e = math.log2(math.e)

    lower, upper = _chunk_masks(q)
    tril = jnp.where(lower, 1.0, 0.0).astype(BF16)
    triu = jnp.where(upper, 1.0, 0.0).astype(BF16)
    a_row = -jnp.exp(alog_row_ref[...])
    a_col = -jnp.exp(alog_col_ref[...])
    lane = lax.broadcasted_iota(jnp.int32, (q, LANE), 1)
    rowi = lax.broadcasted_iota(jnp.int32, (2 * SUBLANE, q), 0)
    emat = _head_expand_matrix()
    rep = SSD_HEADS // SSD_GROUPS
    gw = SSD_WIDTH // SSD_GROUPS
    neg = jnp.float32(-jnp.inf)

    @pl.when(seq_last)
    def _():
        if has_h0:
            st_ref[...] = h0_ref[0]
        else:
            st_ref[...] = jnp.zeros_like(st_ref)

    def chunk_terms(c):
        sl = slice(c * q, (c + 1) * q)
        xcb = xcb_ref[sl, :]
        xsb = xcb[:, :SSD_WIDTH]
        bmb = xcb[:, SSD_WIDTH:SSD_WIDTH + SSD_GROUPS * SSD_STATE]
        cmb = xcb[:, SSD_WIDTH + SSD_GROUPS * SSD_STATE:]
        dt = dt_ref[sl, :]
        dtt = dtt_ref[:, sl]
        dta = dt * a_row
        dtat = dtt * a_col
        la = _prefix_rows(dta, tril)
        tot = la[q - 1:q, :]
        lcol = jnp.where(lane < SSD_HEADS, la, tot - la + dta)
        lat = sum(_dot(p, triu) for p in _split3(dtat))
        tott = lat[:, q - 1:q]
        lrow = jnp.where(rowi < SSD_HEADS, lat, tott - lat + dtat)
        lcol2 = lcol * log2e
        lrow2 = (lrow - jnp.log(dtt)) * log2e
        ecol = jnp.where(lane < 2 * SSD_HEADS, jnp.exp(lcol), 0.0)
        eexp = _expand_heads(ecol, emat)

        cbs = []
        for grp in range(SSD_GROUPS):
            cg = cmb[:, grp * SSD_STATE:(grp + 1) * SSD_STATE]
            bg = bmb[:, grp * SSD_STATE:(grp + 1) * SSD_STATE]
            cbs.append(_dot_nt(cg, bg))
        tiles = []
        for pair in range(SSD_HEADS // 2):
            xpair = xsb[:, pair * LANE:(pair + 1) * LANE]
            res = []
            for hd in (2 * pair, 2 * pair + 1):
                jf, jb = hd, SSD_HEADS + hd
                ef = jnp.exp2(jnp.where(lower, lcol2[:, jf:jf + 1] - lrow2[jf:jf + 1, :], neg))
                eb = jnp.exp2(jnp.where(upper, lcol2[:, jb:jb + 1] - lrow2[jb:jb + 1, :], neg))
                mm = (cbs[hd // rep] * (ef + eb)).astype(BF16)
                res.append(_dot(mm, xpair))
            tiles.append(jnp.where(lane < SSD_HEAD_DIM, res[0], res[1]))
        y = jnp.concatenate(tiles, axis=1)
        hsf = hsf_ref[c]
        parts = [_dot(cmb[:, grp * SSD_STATE:(grp + 1) * SSD_STATE], hsf[:, grp * gw:(grp + 1) * gw])
                 for grp in range(SSD_GROUPS)]
        y = y + jnp.concatenate(parts, axis=1) * eexp[:, :SSD_WIDTH]
        y = y + dskip_ref[...] * xsb.astype(F32)
        return y, cmb, eexp[:, SSD_WIDTH:], _silu(z_ref[sl, :])

    terms = [chunk_terms(c) for c in range(nc)]

    hb = st_ref[...]
    for c in reversed(range(nc)):
        y, cmb, eexp_b, gate = terms[c]
        hsb = hb.astype(BF16)
        parts = [_dot(cmb[:, grp * SSD_STATE:(grp + 1) * SSD_STATE], hsb[:, grp * gw:(grp + 1) * gw])
                 for grp in range(SSD_GROUPS)]
        y = (y + jnp.concatenate(parts, axis=1) * eexp_b) * gate
        y_ref[c * q:(c + 1) * q, :] = _rms_rows(y, gn_ref[...]).astype(BF16)
        hb = hb * dec_ref[c][0:1, SSD_WIDTH:] + sb_ref[c]
    st_ref[...] = hb

    @pl.when(seq_first)
    def _():
        hfin_ref[0, 0] = hb.T


def _ssd(xbc, dt, dtt, z, lw, cps, h0=None, sink=(0, 1, None)):
    slot0, nslots, sink_buf = sink
    t = xbc.shape[0]
    q = SSD_CHUNK
    nc = min(SSD_STEP_CHUNKS, cps)
    assert cps % nc == 0
    rows = nc * q
    nchunks = t // q
    nsteps = nchunks // nc
    spq = cps // nc
    nseq = nchunks // cps
    hb = rows // SUBLANE
    n8 = t // SUBLANE
    has_h0 = h0 is not None
    const = lambda i: (0, 0)
    st_block = (1, SSD_STATE, SSD_WIDTH)
    ch_block = (nc, SSD_STATE, SSD_WIDTH)
    fin_block = (1, 1, SSD_WIDTH, SSD_STATE)
    fin_shape = jax.ShapeDtypeStruct((nseq, nslots, SSD_WIDTH, SSD_STATE), F32)
    any_spec = pl.BlockSpec(memory_space=pl.ANY)
    dec_block = (nc, SUBLANE, 2 * SSD_WIDTH)

    in_specs = [
        pl.BlockSpec((rows, SSD_XBC), lambda i: (i, 0)),
        pl.BlockSpec((SUBLANE, SSD_XBC), lambda i: (jnp.maximum(i * hb - 1, 0), 0)),
        pl.BlockSpec((SUBLANE, SSD_XBC), lambda i: (jnp.minimum((i + 1) * hb, n8 - 1), 0)),
        pl.BlockSpec((rows, LANE), lambda i: (i, 0)),
        pl.BlockSpec((SSD_CONV, SSD_XBC), const),
        pl.BlockSpec((1, SSD_XBC), const),
        pl.BlockSpec((1, LANE), const),
    ]
    args = [xbc, xbc, xbc, dt, lw["conv_w"], lw["conv_b"], lw["alog_row"]]
    if has_h0:
        in_specs.append(pl.BlockSpec(st_block, lambda i: (i // spq, 0, 0)))
        args.append(h0[0])
    aliases = {}
    if sink_buf is not None:
        aliases[len(args)] = 4
        in_specs.append(any_spec)
        args.append(sink_buf)
    xcb, hsf, sb, dec, fin = pl.pallas_call(
        functools.partial(_ssd_state_kernel, cps=cps, nc=nc, has_h0=has_h0, has_sink=sink_buf is not None),
        grid=(nsteps,),
        in_specs=in_specs,
        out_specs=[
            pl.BlockSpec((rows, SSD_XBC), lambda i: (i, 0)),
            pl.BlockSpec(ch_block, lambda i: (i, 0, 0)),
            pl.BlockSpec(ch_block, lambda i: (i, 0, 0)),
            pl.BlockSpec(dec_block, lambda i: (i, 0, 0)),
            pl.BlockSpec(fin_block, lambda i: (i // spq, slot0, 0, 0)),
        ],
        out_shape=[
            jax.ShapeDtypeStruct((t, SSD_XBC), BF16),
            jax.ShapeDtypeStruct((nchunks, SSD_STATE, SSD_WIDTH), BF16),
            jax.ShapeDtypeStruct((nchunks, SSD_STATE, SSD_WIDTH), F32),
            jax.ShapeDtypeStruct((nchunks, SUBLANE, 2 * SSD_WIDTH), F32),
            fin_shape,
        ],
        input_output_aliases=aliases,
        scratch_shapes=[pltpu.VMEM((SSD_STATE, SSD_WIDTH), F32)],
        compiler_params=_cparams("arbitrary"),
        name="ssd_state",
    )(*args)

    gi = lambda i: nsteps - 1 - i
    in_specs = [
        pl.BlockSpec((rows, SSD_XBC), lambda i: (gi(i), 0)),
        pl.BlockSpec((rows, LANE), lambda i: (gi(i), 0)),
        pl.BlockSpec((2 * SSD_HEADS, rows), lambda i: (0, gi(i))),
        pl.BlockSpec((rows, SSD_WIDTH), lambda i: (gi(i), 0)),
        pl.BlockSpec(ch_block, lambda i: (gi(i), 0, 0)),
        pl.BlockSpec(ch_block, lambda i: (gi(i), 0, 0)),
        pl.BlockSpec(dec_block, lambda i: (gi(i), 0, 0)),
        pl.BlockSpec((1, LANE), const),
        pl.BlockSpec((2 * SSD_HEADS, 1), const),
        pl.BlockSpec((1, SSD_WIDTH), const),
        pl.BlockSpec((1, SSD_WIDTH), const),
    ]
    args = [xcb, dt, dtt, z, hsf, sb, dec, lw["alog_row"], lw["alog_col"], lw["dskip_row"], lw["ssd_norm"]]
    if has_h0:
        in_specs.append(pl.BlockSpec(st_block, lambda i: (gi(i) // spq, 0, 0)))
        args.append(h0[1])
    in_specs.append(any_spec)
    args.append(fin)
    y, fin = pl.pallas_call(
        functools.partial(_ssd_out_kernel, cps=cps, nc=nc, has_h0=has_h0, has_sink=True, nsteps=nsteps),
        grid=(nsteps,),
        in_specs=in_specs,
        out_specs=[
            pl.BlockSpec((rows, SSD_WIDTH), lambda i: (gi(i), 0)),
            pl.BlockSpec(fin_block, lambda i: (gi(i) // spq, slot0 + 1, 0, 0)),
        ],
        out_shape=[jax.ShapeDtypeStruct((t, SSD_WIDTH), BF16), fin_shape],
        input_output_aliases={len(args) - 1: 1},
        scratch_shapes=[pltpu.VMEM((SSD_STATE, SSD_WIDTH), F32)],
        compiler_params=_cparams("arbitrary"),
        name="ssd_out",
    )(*args)
    return y, fin


def _state_to_kernel_layout(h):
    n = h.shape[0]
    return h.transpose(0, 3, 1, 2).reshape(n, SSD_STATE, SSD_WIDTH)


def _attn_kernel(*refs, heads, has_cache, nseq_step):
    if has_cache:
        qt_ref, k_ref, vt_ref, kc_ref, vct_ref = refs[:5]
    else:
        qt_ref, k_ref, vt_ref = refs[:3]
    n_in = 5 if has_cache else 3
    o_ref = refs[n_in]
    scratch = refs[n_in + 1:]
    tq = qt_ref.shape[1] // nseq_step
    lk = k_ref.shape[0] // nseq_step
    kb = min(ATTN_KEY_BLOCK, lk)
    ones = jnp.ones((ONES_ROWS, kb), BF16)

    for sq in range(nseq_step):
        s_refs = scratch[2 * sq:2 * sq + 2]
        qcols = slice(sq * tq, (sq + 1) * tq)
        blocks = [(k_ref, vt_ref, sq * lk + i * kb) for i in range(lk // kb)]
        if has_cache:
            assert nseq_step == 1
            lc = kc_ref.shape[0]
            assert min(ATTN_KEY_BLOCK, lc) == kb
            blocks += [(kc_ref, vct_ref, i * kb) for i in range(lc // kb)]
        nblk = len(blocks)

        def score_block(hd, i, m, blocks=blocks, s_refs=s_refs, qcols=qcols):
            kr, _, off = blocks[i]
            q = qt_ref[hd * HEAD_PAD:(hd + 1) * HEAD_PAD, qcols]
            s = _dot(kr[off:off + kb, hd * HEAD_PAD:(hd + 1) * HEAD_PAD], q)
            s_refs[hd % 2][i * kb:(i + 1) * kb, :] = s
            bm = jnp.max(s, axis=0, keepdims=True)
            return bm if m is None else jnp.maximum(m, bm)

        def value_block(hd, i, m, acc, blocks=blocks, s_refs=s_refs):
            _, vr, off = blocks[i]
            p = jnp.exp2((s_refs[hd % 2][i * kb:(i + 1) * kb, :] - m).astype(BF16))
            v = vr[hd * MLA_V_DIM:(hd + 1) * MLA_V_DIM, off:off + kb]
            part = _dot(jnp.concatenate([v, ones], axis=0), p)
            return part if acc is None else acc + part

        m_cur = None
        for i in range(nblk):
            m_cur = score_block(0, i, m_cur)
        for hd in range(heads):
            m_next, acc = None, None
            for i in range(nblk):
                if nblk == 1 and hd + 1 < heads:
                    m_next = score_block(hd + 1, i, m_next)
                acc = value_block(hd, i, m_cur, acc)
                if nblk > 1 and hd + 1 < heads:
                    m_next = score_block(hd + 1, i, m_next)
            vs = slice(hd * MLA_V_DIM, (hd + 1) * MLA_V_DIM)
            o_ref[vs, qcols] = acc[:MLA_V_DIM] / acc[MLA_V_DIM:MLA_V_DIM + 1]
            m_cur = m_next


def _attention(qt, k, vt, seq_len, heads_per_step, cache=None):
    t = k.shape[0]
    nseq = t // seq_len
    tq = min(ATTN_Q_TILE, seq_len)
    nq = seq_len // tq
    g = heads_per_step
    nss = ATTN_SEQS_PER_STEP if (nq == 1 and cache is None and nseq % ATTN_SEQS_PER_STEP == 0) else 1
    in_specs = [
        pl.BlockSpec((g * HEAD_PAD, nss * tq), lambda s, h, j: (h, s * nq + j)),
        pl.BlockSpec((nss * seq_len, g * HEAD_PAD), lambda s, h, j: (s, h)),
        pl.BlockSpec((g * MLA_V_DIM, nss * seq_len), lambda s, h, j: (h, s)),
    ]
    args = [qt, k, vt]
    n_keys = seq_len
    if cache is not None:
        kc, vct = cache
        past = kc.shape[0] // nseq
        n_keys += past
        in_specs += [
            pl.BlockSpec((past, g * HEAD_PAD), lambda s, h, j: (s, h)),
            pl.BlockSpec((g * MLA_V_DIM, past), lambda s, h, j: (h, s)),
        ]
        args += [kc, vct]
    kern = functools.partial(_attn_kernel, heads=g, has_cache=cache is not None, nseq_step=nss)
    return pl.pallas_call(
        kern,
        grid=(nseq // nss, MLA_HEADS // g, nq),
        in_specs=in_specs,
        out_specs=pl.BlockSpec((g * MLA_V_DIM, nss * tq), lambda s, h, j: (h, s * nq + j)),
        out_shape=jax.ShapeDtypeStruct((MLA_WIDTH, t), F32),
        scratch_shapes=[pltpu.VMEM((n_keys, tq), F32) for _ in range(2 * nss)],
        compiler_params=_cparams("arbitrary", "arbitrary", "arbitrary"),
        name="attention",
    )(*args)


def _outproj_kernel(*refs, has_router):
    if has_router:
        (y_ref, ot_ref, x_ref, mod_ref, wa_ref, wb_ref, gm_ref, gpost_ref, gpre2_ref, rt_ref,
         x1_ref, h2_ref, comb_ref) = refs
    else:
        (y_ref, ot_ref, x_ref, mod_ref, wa_ref, wb_ref, gm_ref, gpost_ref, gpre2_ref,
         x1_ref, h2_ref) = refs
    mod = mod_ref[0]
    gate1 = mod[:, 2 * D_MODEL:3 * D_MODEL]
    shift2 = mod[:, 3 * D_MODEL:4 * D_MODEL]
    scale2 = mod[:, 4 * D_MODEL:5 * D_MODEL]
    ot = ot_ref[...]
    ms = jnp.mean(ot * ot, axis=0, keepdims=True)
    on = (ot * lax.rsqrt(ms + EPS) * gm_ref[...]).astype(BF16)
    y = _dot(y_ref[...], wa_ref[...]) + _dot_tn(on, wb_ref[...])
    x1 = x_ref[...] + gate1 * _rms_rows(y, gpost_ref[...])
    x1_ref[...] = x1
    h2 = _rms_rows(x1, gpre2_ref[...]) * (1.0 + scale2) + shift2
    h2_ref[...] = h2.astype(BF16)
    if has_router:
        hh, hm, _ = _split3(h2)
        rh, rm, _ = _split3(rt_ref[...])
        logits = _dot(hh, rh) + (_dot(hm, rh) + _dot(hh, rm))
        lane = lax.broadcasted_iota(jnp.int32, logits.shape, 1).astype(F32)
        neg = jnp.float32(-jnp.inf)
        lg = jnp.where(lane < N_EXPERTS, logits, neg)
        m1 = jnp.max(lg, axis=-1, keepdims=True)
        i1 = jnp.min(jnp.where(lg == m1, lane, float(LANE)), axis=-1, keepdims=True)
        lg2 = jnp.where(lane == i1, neg, lg)
        m2 = jnp.max(lg2, axis=-1, keepdims=True)
        i2 = jnp.min(jnp.where(lg2 == m2, lane, float(LANE)), axis=-1, keepdims=True)
        e2 = jnp.exp(m2 - m1)
        w1 = 1.0 / (1.0 + e2)
        w2 = e2 / (1.0 + e2)
        comb_ref[...] = jnp.where(lane == i1, w1, 0.0) + jnp.where(lane == i2, w2, 0.0)


def _outproj(yssd, ot, x, mod3, mod_row_fn, lw, router=None):
    t = x.shape[0]
    tm = ROW_TILE
    const = lambda i: (0, 0)
    row = lambda i: (i, 0)
    in_specs = [
        pl.BlockSpec((tm, SSD_WIDTH), row),
        pl.BlockSpec((MLA_WIDTH, tm), lambda i: (0, i)),
        pl.BlockSpec((tm, D_MODEL), row),
        pl.BlockSpec((1, 1, N_MOD * D_MODEL), lambda i: (mod_row_fn(ROW_TILE)(i), 0, 0)),
        pl.BlockSpec((SSD_WIDTH, D_MODEL), const),
        pl.BlockSpec((MLA_WIDTH, D_MODEL), const),
        pl.BlockSpec((MLA_WIDTH, 1), const),
        pl.BlockSpec((1, D_MODEL), const),
        pl.BlockSpec((1, D_MODEL), const),
    ]
    args = [yssd, ot, x, mod3, lw["w_out_a"], lw["w_out_b"], lw["mla_norm_col"],
            lw["g_post1"], lw["g_pre2"]]
    out_specs = [pl.BlockSpec((tm, D_MODEL), row), pl.BlockSpec((tm, D_MODEL), row)]
    out_shape = [jax.ShapeDtypeStruct((t, D_MODEL), F32), jax.ShapeDtypeStruct((t, D_MODEL), BF16)]
    if router is not None:
        in_specs.append(pl.BlockSpec((D_MODEL, LANE), const))
        args.append(router)
        out_specs.append(pl.BlockSpec((tm, LANE), row))
        out_shape.append(jax.ShapeDtypeStruct((t, LANE), F32))
    return pl.pallas_call(
        functools.partial(_outproj_kernel, has_router=router is not None),
        grid=(t // tm,),
        in_specs=in_specs,
        out_specs=out_specs,
        out_shape=out_shape,
        compiler_params=_cparams("arbitrary"),
        name="outproj",
    )(*args)


def _ffn_kernel(*refs, has_comb, nslab):
    if has_comb:
        h_ref, x_ref, comb_ref, mod_ref, wgu_ref, wd_ref, gpost_ref, o_ref, acc_ref = refs
    else:
        h_ref, x_ref, mod_ref, wgu_ref, wd_ref, gpost_ref, o_ref, acc_ref = refs
    e = pl.program_id(1)
    h = h_ref[...]
    f = wd_ref.shape[1]
    gu = _dot(h, wgu_ref[0])
    hid = _silu(gu[:, :f]) * gu[:, f:]
    if has_comb:
        comb = comb_ref[...]
        lane = lax.broadcasted_iota(jnp.int32, comb.shape, 1)
        wcol = jnp.sum(jnp.where(lane == e, comb, 0.0), axis=-1, keepdims=True)
        hid = hid * wcol
    part = _dot(hid.astype(BF16), wd_ref[0])

    @pl.when(e == 0)
    def _():
        acc_ref[...] = part

    @pl.when(e > 0)
    def _():
        acc_ref[...] += part

    @pl.when(e == nslab - 1)
    def _():
        gate2 = mod_ref[0][:, 5 * D_MODEL:6 * D_MODEL]
        o_ref[...] = x_ref[...] + gate2 * _rms_rows(acc_ref[...], gpost_ref[...])


def _ffn(h2, x1, mod3, mod_row_fn, wgu, wd, gpost, comb=None):
    t = x1.shape[0]
    tm = FFN_ROW_TILE
    nslab, f, _ = wd.shape
    row = lambda i, e: (i, 0)
    in_specs = [pl.BlockSpec((tm, D_MODEL), row), pl.BlockSpec((tm, D_MODEL), row)]
    args = [h2, x1]
    if comb is not None:
        in_specs.append(pl.BlockSpec((tm, LANE), row))
        args.append(comb)
    in_specs += [
        pl.BlockSpec((1, 1, N_MOD * D_MODEL), lambda i, e: (mod_row_fn(FFN_ROW_TILE)(i), 0, 0)),
        pl.BlockSpec((1, D_MODEL, 2 * f), lambda i, e: (e, 0, 0)),
        pl.BlockSpec((1, f, D_MODEL), lambda i, e: (e, 0, 0)),
        pl.BlockSpec((1, D_MODEL), lambda i, e: (0, 0)),
    ]
    args += [mod3, wgu, wd, gpost]
    return pl.pallas_call(
        functools.partial(_ffn_kernel, has_comb=comb is not None, nslab=nslab),
        grid=(t // tm, nslab),
        in_specs=in_specs,
        out_specs=pl.BlockSpec((tm, D_MODEL), row),
        out_shape=jax.ShapeDtypeStruct((t, D_MODEL), F32),
        scratch_shapes=[pltpu.VMEM((tm, D_MODEL), F32)],
        compiler_params=_cparams("arbitrary", "arbitrary"),
        name="ffn",
    )(*args)


def _moe_kernel(h_ref, x_ref, comb_ref, mod_ref, wg_ref, wu_ref, wd_ref, gpost_ref, o_ref,
                acc_ref, rank_ref, rank_t_ref, comb_t_ref, *, nexp, sub):
    e = pl.program_id(1)
    tm = h_ref.shape[0]
    caps = MOE_CAPS
    cmax = caps[-1]

    @pl.when(e == 0)
    def _():
        acc_ref[...] = jnp.zeros_like(acc_ref)
        r_i = lax.broadcasted_iota(jnp.int32, (sub, sub), 0)
        c_i = lax.broadcasted_iota(jnp.int32, (sub, sub), 1)
        strict = jnp.where(r_i > c_i, 1.0, 0.0).astype(BF16)
        for s in range(tm // sub):
            rows = slice(s * sub, (s + 1) * sub)
            comb = comb_ref[rows, :]
            rank = _dot(strict, jnp.where(comb > 0.0, 1.0, 0.0).astype(BF16))
            rank_ref[rows, :] = rank
            rank_t_ref[:, rows] = rank.T
            comb_t_ref[:, rows] = comb.T

    def expert_pass(rows, wcol, rcol, wrow, rrow, base, cap):
        capl = -(-cap // LANE) * LANE
        slot_l = lax.broadcasted_iota(jnp.int32, (sub, capl), 1).astype(F32)
        slot_s = lax.broadcasted_iota(jnp.int32, (cap, sub), 0).astype(F32)
        gather = jnp.where(((rrow - base) == slot_s) & (wrow > 0.0), 1.0, 0.0).astype(BF16)
        scatter = jnp.where(((rcol - base) == slot_l) & (wcol > 0.0) & (slot_l < float(cap)),
                            1.0, 0.0).astype(BF16)
        xg = _dot(gather, h_ref[rows, :]).astype(BF16)
        hid = _silu(_dot(xg, wg_ref[0])) * _dot(xg, wu_ref[0])
        y = _dot(hid.astype(BF16), wd_ref[0]).astype(BF16)
        if capl > cap:
            y = jnp.concatenate([y, jnp.zeros((capl - cap, y.shape[1]), BF16)], axis=0)
        acc_ref[rows, :] += wcol * _dot(scatter, y)

    lane = lax.broadcasted_iota(jnp.int32, (sub, LANE), 1)

    def sub_tile(s, carry):
        rows = pl.ds(pl.multiple_of(s * sub, sub), sub)
        pick = lane == e
        wcol = jnp.sum(jnp.where(pick, comb_ref[rows, :], 0.0), axis=-1, keepdims=True)
        rcol = jnp.sum(jnp.where(pick, rank_ref[rows, :], 0.0), axis=-1, keepdims=True)
        wrow = comb_t_ref[pl.ds(e, 1), rows]
        rrow = rank_t_ref[pl.ds(e, 1), rows]
        count = jnp.max(jnp.where(wrow > 0.0, rrow + 1.0, 0.0))
        npass = ((count + (cmax - 1.0)) * (1.0 / cmax)).astype(jnp.int32)
        nfull = jnp.maximum(npass - 1, 0)

        def full_pass(k, c):
            expert_pass(rows, wcol, rcol, wrow, rrow, (k * cmax).astype(F32), cmax)
            return c

        lax.fori_loop(0, nfull, full_pass, 0)
        base = (nfull * cmax).astype(F32)
        left = count - base
        lo = 0
        for cap in caps:
            @pl.when(jnp.logical_and(left > float(lo), left <= float(cap)))
            def _(cap=cap):
                expert_pass(rows, wcol, rcol, wrow, rrow, base, cap)
            lo = cap
        return carry

    lax.fori_loop(0, tm // sub, sub_tile, 0)

    @pl.when(e == nexp - 1)
    def _():
        gate2 = mod_ref[0][:, 5 * D_MODEL:6 * D_MODEL]
        o_ref[...] = x_ref[...] + gate2 * _rms_rows(acc_ref[...], gpost_ref[...])


def _moe(h2, x1, comb, mod3, mod_row_fn, wg, wu, wd, gpost):
    t = x1.shape[0]
    tm = min(MOE_ROW_TILE, t)
    sub = min(MOE_SUB_TILE, tm)
    nexp, f, _ = wd.shape
    row = lambda i, e: (i, 0)
    return pl.pallas_call(
        functools.partial(_moe_kernel, nexp=nexp, sub=sub),
        grid=(t // tm, nexp),
        in_specs=[
            pl.BlockSpec((tm, D_MODEL), row),
            pl.BlockSpec((tm, D_MODEL), row),
            pl.BlockSpec((tm, LANE), row),
            pl.BlockSpec((1, 1, N_MOD * D_MODEL), lambda i, e: (mod_row_fn(tm)(i), 0, 0)),
            pl.BlockSpec((1, D_MODEL, f), lambda i, e: (e, 0, 0)),
            pl.BlockSpec((1, D_MODEL, f), lambda i, e: (e, 0, 0)),
            pl.BlockSpec((1, f, D_MODEL), lambda i, e: (e, 0, 0)),
            pl.BlockSpec((1, D_MODEL), lambda i, e: (0, 0)),
        ],
        out_specs=pl.BlockSpec((tm, D_MODEL), row),
        out_shape=jax.ShapeDtypeStruct((t, D_MODEL), F32),
        scratch_shapes=[pltpu.VMEM((tm, D_MODEL), F32), pltpu.VMEM((tm, LANE), F32),
                        pltpu.VMEM((LANE, tm), F32), pltpu.VMEM((LANE, tm), F32)],
        compiler_params=_cparams("arbitrary", "arbitrary"),
        name="moe",
    )(h2, x1, comb, mod3, wg, wu, wd, gpost)


def _prep_layer(i, p):
    w_in = p["w_in"][i]
    s1 = SSD_WIDTH
    s2 = s1 + SSD_XBC
    s3 = s2 + 2 * SSD_HEADS
    s4 = s3 + MLA_Q_RANK
    s5 = s4 + MLA_KV_RANK
    w_z, w_xbc, w_dt, w_cq, w_ckv, w_kr = (w_in[:, :s1], w_in[:, s1:s2], w_in[:, s2:s3],
                                             w_in[:, s3:s4], w_in[:, s4:s5], w_in[:, s5:])
    zc = lambda n: jnp.zeros((D_MODEL, n), F32)
    tile_a = jnp.concatenate([w_dt, zc(ROPE_LANE0 - 2 * SSD_HEADS), w_kr,
                              zc(LANE - ROPE_LANE0 - MLA_ROPE_DIM)], axis=1)
    w_in_pad = jnp.concatenate([w_z, w_xbc, w_cq, w_ckv, tile_a], axis=1).astype(BF16)

    w_uq = p["w_uq"][i].reshape(MLA_Q_RANK, MLA_HEADS, MLA_NOPE_DIM + MLA_ROPE_DIM)
    q_nope, q_rope = w_uq[..., :MLA_NOPE_DIM], w_uq[..., MLA_NOPE_DIM:]
    zq = lambda n: jnp.zeros((MLA_Q_RANK, MLA_HEADS, n), F32)
    pad = HEAD_PAD - MLA_NOPE_DIM - MLA_ROPE_DIM
    w_q = jnp.concatenate([q_nope, q_rope, zq(pad)], axis=-1).reshape(MLA_Q_RANK, -1)

    w_ukv = p["w_ukv"][i].reshape(MLA_KV_RANK, MLA_HEADS, MLA_NOPE_DIM + MLA_V_DIM)
    k_nope, v_w = w_ukv[..., :MLA_NOPE_DIM], w_ukv[..., MLA_NOPE_DIM:]
    w_uk = jnp.concatenate([k_nope, jnp.zeros((MLA_KV_RANK, MLA_HEADS, HEAD_PAD - MLA_NOPE_DIM), F32)],
                           axis=-1).reshape(MLA_KV_RANK, -1)
    w_uv = v_w.reshape(MLA_KV_RANK, MLA_WIDTH)

    dtb = p["dt_bias"][i].reshape(2 * SSD_HEADS)
    alog = p["a_log"][i].reshape(2 * SSD_HEADS)
    padl = lambda v: jnp.pad(v, (0, LANE - v.shape[0])).reshape(1, LANE)
    w_out = p["w_out"][i]
    return {
        "g_pre1": p["norm_pre_mix"][i].reshape(1, D_MODEL),
        "g_post1": p["norm_post_mix"][i].reshape(1, D_MODEL),
        "g_pre2": p["norm_pre_ffn"][i].reshape(1, D_MODEL),
        "g_post2": p["norm_post_ffn"][i].reshape(1, D_MODEL),
        "w_in": w_in_pad,
        "w_dt": w_dt.T.astype(BF16),
        "dtb_row": padl(dtb),
        "dtb_col": dtb.reshape(-1, 1),
        "alog_row": padl(alog),
        "alog_col": alog.reshape(-1, 1),
        "q_norm": p["q_norm"][i].reshape(1, -1),
        "w_q": w_q.T.astype(BF16),
        "kv_norm": p["kv_norm"][i].reshape(1, -1),
        "w_uk": w_uk.astype(BF16),
        "w_uv": w_uv.T.astype(BF16),
        "conv_w": p["conv_w"][i],
        "conv_b": p["conv_b"][i].reshape(1, -1),
        "dskip_row": jnp.repeat(p["d_skip"][i], SSD_HEAD_DIM).reshape(1, -1),
        "ssd_norm": p["ssd_norm"][i].reshape(1, -1),
        "mla_norm_col": p["mla_norm"][i].reshape(-1, 1),
        "w_out_a": w_out[:SSD_WIDTH].astype(BF16),
        "w_out_b": w_out[SSD_WIDTH:].astype(BF16),
    }


def _rope_tables(n_tokens):
    rows = n_tokens // GRID_W
    row = np.repeat(np.arange(rows, dtype=np.float32), GRID_W)
    col = np.tile(np.arange(GRID_W, dtype=np.float32), rows)
    half = MLA_ROPE_DIM // 2
    inv = (np.float32(ROPE_THETA) ** (-np.arange(0, half, 2, dtype=np.float32) / np.float32(half))).astype(np.float32)
    ar = row[:, None] * inv[None, :]
    ac = col[:, None] * inv[None, :]
    ang = np.concatenate([ar, ar, ac, ac], axis=-1).astype(np.float32)
    return jnp.asarray(np.cos(ang), F32), jnp.asarray(np.sin(ang), F32)


def _attn_tables(cos, sin, n):
    scale = (MLA_NOPE_DIM + MLA_ROPE_DIM) ** -0.5 * math.log2(math.e)
    pad = HEAD_PAD - ROPE_LANE0 - MLA_ROPE_DIM
    cosk = jnp.concatenate([jnp.zeros((n, ROPE_LANE0), F32), cos, jnp.zeros((n, pad), F32)], axis=1)
    sink = jnp.concatenate([jnp.zeros((n, ROPE_LANE0), F32), sin, jnp.zeros((n, pad), F32)], axis=1)
    cosq = jnp.concatenate([jnp.ones((n, ROPE_LANE0), F32), cos, jnp.zeros((n, pad), F32)], axis=1)
    return (cosq * scale).T, (sink * scale).T, cosk, sink


def kernel(x_prompt, x_sample, cache_ckv, cache_krope, state_ssm, c, c_ctx, w_mod, b_mod, norm_pre_mix, norm_post_mix, norm_pre_ffn, norm_post_ffn, w_in, conv_w, conv_b, dt_bias, a_log, d_skip, ssd_norm, q_norm, w_uq, kv_norm, w_ukv, mla_norm, w_out, ffn_w_gate, ffn_w_up, ffn_w_down, moe_router, moe_w_gate, moe_w_up, moe_w_down):
    params = dict(w_in=w_in, conv_w=conv_w, conv_b=conv_b, dt_bias=dt_bias, a_log=a_log, d_skip=d_skip,
                  ssd_norm=ssd_norm, q_norm=q_norm, w_uq=w_uq, kv_norm=kv_norm, w_ukv=w_ukv,
                  mla_norm=mla_norm, w_out=w_out, norm_pre_mix=norm_pre_mix, norm_post_mix=norm_post_mix,
                  norm_pre_ffn=norm_pre_ffn, norm_post_ffn=norm_post_ffn)
    batch, seq, d = x_prompt.shape
    dec_batch, dec_seq, _ = x_sample.shape
    depth = w_in.shape[0]
    past = cache_ckv.shape[2]
    tm = ROW_TILE

    cvec = jnp.concatenate([c_ctx[None, :], c, jnp.zeros((SUBLANE - 1 - dec_batch, d), F32)], axis=0)
    mod = _modulation(cvec, w_mod, b_mod)

    ones = jnp.ones((tm, MLA_ROPE_DIM), F32)
    tabs_ctx = _attn_tables(ones, jnp.zeros_like(ones), tm)
    cos, sin = _rope_tables(dec_seq)
    tabs_lat = _attn_tables(cos, sin, dec_seq)
    lat_blocks = dec_seq // tm

    xp = x_prompt.reshape(batch * seq, d)
    xs = x_sample.reshape(dec_batch * dec_seq, d)
    cache_bufs, ssm_buf = None, None
    for i in range(depth):
        lw = _prep_layer(i, params)
        mod3 = mod[i].reshape(SUBLANE, 1, N_MOD * d)
        j = i // 2
        if i % 2 == 0:
            f = ffn_w_gate.shape[2] // 2
            wgu = jnp.stack([jnp.concatenate([ffn_w_gate[j][:, s * f:(s + 1) * f], ffn_w_up[j][:, s * f:(s + 1) * f]],
                                             axis=1) for s in range(2)], axis=0).astype(BF16)
            wd = ffn_w_down[j].reshape(2, f, d).astype(BF16)
            router = None
        else:
            wgu = (moe_w_gate[j].astype(BF16), moe_w_up[j].astype(BF16))
            wd = moe_w_down[j].astype(BF16)
            router = jnp.pad(moe_router[j], ((0, 0), (0, LANE - N_EXPERTS)))

        def run(x, row_fn, tabs, tab_blocks, seq_len, heads_per_step, ctx):
            cache_out = (i, depth, seq_len, cache_bufs) if ctx is None else None
            z, xbc, dt, dtt, qt, k, vt, *caches = _inproj(x, mod3, row_fn, lw, tabs, tab_blocks, cache_out)
            cps = seq_len // SSD_CHUNK
            h0 = None
            if ctx is not None:
                h0 = (_state_to_kernel_layout(ctx[2][:, 0]), _state_to_kernel_layout(ctx[2][:, 1]))
            sink = (2 * i, 2 * depth, ssm_buf) if ctx is None else (0, 2, None)
            yssd, fin = _ssd(xbc, dt, dtt, z, lw, cps, h0=h0, sink=sink)
            cache = None
            if ctx is not None:
                kr_tile = jnp.pad(ctx[1].reshape(-1, MLA_ROPE_DIM),
                                  ((0, 0), (ROPE_LANE0, HEAD_PAD - ROPE_LANE0 - MLA_ROPE_DIM)))
                cache = _kvcache(ctx[0].reshape(-1, MLA_KV_RANK), kr_tile, lw)
            ot = _attention(qt, k, vt, seq_len, heads_per_step, cache=cache)
            outs = _outproj(yssd, ot, x, mod3, row_fn, lw, router=router)
            x1, h2 = outs[0], outs[1]
            comb = outs[2] if router is not None else None
            if comb is None:
                x2 = _ffn(h2, x1, mod3, row_fn, wgu, wd, lw["g_post2"])
            else:
                x2 = _moe(h2, x1, comb, mod3, row_fn, *wgu, wd, lw["g_post2"])
            return x2, caches, fin

        xp, cache_bufs, ssm_buf = run(xp, lambda tile: (lambda b: 0), tabs_ctx, 1, seq, MLA_HEADS, None)
        xs, _, _ = run(xs, lambda tile: (lambda b: 1 + (b * tile) // dec_seq), tabs_lat, lat_blocks, dec_seq, MLA_HEADS,
                             (cache_ckv[:, i], cache_krope[:, i], state_ssm[:, i]))
    return (xp.reshape(batch, seq, d), xs.reshape(dec_batch, dec_seq, d),
            cache_bufs[0], cache_bufs[1],
            ssm_buf.reshape(batch, depth, 2, SSD_HEADS, SSD_HEAD_DIM, SSD_STATE))
```

```python
import functools
import math

import jax
import jax.numpy as jnp
import numpy as np
from jax import lax
from jax.experimental import pallas as pl
from jax.experimental.pallas import tpu as pltpu

F32 = jnp.float32
BF16 = jnp.bfloat16

D_MODEL = 1024
GRID_W = 64
SSD_WIDTH = 512
SSD_HEAD_DIM = 64
SSD_HEADS = 8
SSD_GROUPS = 2
SSD_STATE = 64
SSD_CONV = 5
SSD_CHUNK = 128
SSD_STEP_CHUNKS = 4
SSD_XBC = SSD_WIDTH + 2 * SSD_GROUPS * SSD_STATE
MLA_WIDTH = 512
MLA_V_DIM = 64
MLA_HEADS = 8
MLA_NOPE_DIM = 64
MLA_ROPE_DIM = 32
MLA_Q_RANK = 384
MLA_KV_RANK = 256
ROPE_THETA = 10000.0
N_EXPERTS = 8
N_MOD = 6
EPS = 1e-6

LANE = 128
SUBLANE = 8
HEAD_PAD = 128
ONES_ROWS = 16
ROPE_LANE0 = MLA_NOPE_DIM
C_Z = 0
C_XBC = C_Z + SSD_WIDTH
C_CQ = C_XBC + SSD_XBC
C_CKV = C_CQ + MLA_Q_RANK
C_TA = C_CKV + MLA_KV_RANK
IN_PAD = C_TA + LANE
ROT_GROUP = MLA_ROPE_DIM // 4

VMEM_LIMIT = 56 * 1024 * 1024

ROW_TILE = 512
ATTN_Q_TILE = 256
ATTN_KEY_BLOCK = 512
ATTN_SEQS_PER_STEP = 2
FFN_ROW_TILE = 512
MOE_ROW_TILE = 1024
MOE_SUB_TILE = 512
MOE_CAPS = (128, 160, 192, 224, 256)

NT_DIMS = (((1,), (1,)), ((), ()))
TN_DIMS = (((0,), (0,)), ((), ()))


def _cparams(*sem):
    return pltpu.CompilerParams(dimension_semantics=sem, vmem_limit_bytes=VMEM_LIMIT)


def _silu(x):
    return x / (1.0 + jnp.exp(-x))


def _softplus(x):
    return jnp.maximum(x, 0.0) + jnp.log(1.0 + jnp.exp(-jnp.abs(x)))


def _rms_rows(x, g):
    ms = jnp.mean(x * x, axis=-1, keepdims=True)
    return x * lax.rsqrt(ms + EPS) * g


def _dot(a, b):
    return jnp.dot(a, b, preferred_element_type=F32)


def _dot_nt(a, b):
    return lax.dot_general(a, b, NT_DIMS, preferred_element_type=F32)


def _dot_tn(a, b):
    return lax.dot_general(a, b, TN_DIMS, preferred_element_type=F32)


def _split3(x):
    hi = x.astype(BF16)
    r1 = x - hi.astype(F32)
    mid = r1.astype(BF16)
    lo = (r1 - mid.astype(F32)).astype(BF16)
    return hi, mid, lo


def _mod_kernel(c_ref, w_ref, b_ref, o_ref):
    s = _silu(c_ref[...]).astype(BF16)
    o_ref[0] = _dot(s, w_ref[0].astype(BF16)) + b_ref[0]


def _modulation(cvec, w_mod, b_mod):
    depth, d, n = w_mod.shape
    tn = 1536
    return pl.pallas_call(
        _mod_kernel,
        grid=(depth, n // tn),
        in_specs=[
            pl.BlockSpec((SUBLANE, d), lambda l, j: (0, 0)),
            pl.BlockSpec((1, d, tn), lambda l, j: (l, 0, j)),
            pl.BlockSpec((1, 1, tn), lambda l, j: (l, 0, j)),
        ],
        out_specs=pl.BlockSpec((1, SUBLANE, tn), lambda l, j: (l, 0, j)),
        out_shape=jax.ShapeDtypeStruct((depth, SUBLANE, n), F32),
        compiler_params=_cparams("arbitrary", "arbitrary"),
        name="modulation",
    )(cvec, w_mod, b_mod.reshape(depth, 1, n))


def _inproj_kernel(*refs, emit_cache, n_alias):
    (x_ref, mod_ref, gpre_ref, win_ref, wdt_ref, dtb_row_ref, dtb_col_ref, qn_ref, wq_ref, kvn_ref, wuk_ref,
     wuv_ref, cosq_ref, sinq_ref, cosk_ref, sink_ref) = refs[:16]
    outs = refs[16 + n_alias:]
    z_ref, xbc_ref, dt_ref, dtt_ref, qt_ref, k_ref, vt_ref = outs[:7]
    mod = mod_ref[0]
    shift = mod[:, 0:D_MODEL]
    scale = mod[:, D_MODEL:2 * D_MODEL]
    h = _rms_rows(x_ref[...], gpre_ref[...]) * (1.0 + scale) + shift
    hb = h.astype(BF16)
    proj = _dot(hb, win_ref[...])
    z_ref[...] = proj[:, C_Z:C_XBC]
    xbc_ref[...] = proj[:, C_XBC:C_CQ]
    cqn = _rms_rows(proj[:, C_CQ:C_CKV], qn_ref[...]).astype(BF16)
    ckvn = _rms_rows(proj[:, C_CKV:C_TA], kvn_ref[...])
    if emit_cache:
        ckvn_ref, kr_ref = outs[7:]
        nb, _, sq, _ = ckvn_ref.shape
        ckvn_ref[:, 0] = ckvn.reshape(nb, sq, MLA_KV_RANK)
    ckvb = ckvn.astype(BF16)
    ta = proj[:, C_TA:IN_PAD]
    dt_ref[...] = _softplus(ta + dtb_row_ref[...])
    if emit_cache:
        kr_ref[:, 0] = ta[:, ROPE_LANE0:ROPE_LANE0 + MLA_ROPE_DIM].reshape(nb, sq, MLA_ROPE_DIM)
    lane = lax.broadcasted_iota(jnp.int32, ta.shape, 1)
    first = (lane // ROT_GROUP) % 2 == 0
    rot = jnp.where(first, -pltpu.roll(ta, LANE - ROT_GROUP, 1), pltpu.roll(ta, ROT_GROUP, 1))
    kr_rot = ta * cosk_ref[...] + rot * sink_ref[...]
    knp = _dot(ckvb, wuk_ref[...])
    for hd in range(MLA_HEADS):
        sl = slice(hd * HEAD_PAD, (hd + 1) * HEAD_PAD)
        k_ref[:, sl] = (knp[:, sl] + kr_rot).astype(BF16)
    vt_ref[...] = _dot_nt(wuv_ref[...], ckvb).astype(BF16)
    qt = _dot_nt(wq_ref[...], cqn)
    cosq = cosq_ref[...]
    sinq = sinq_ref[...]
    g, r0 = ROT_GROUP, ROPE_LANE0
    for hd in range(MLA_HEADS):
        blk = qt[hd * HEAD_PAD:(hd + 1) * HEAD_PAD, :]
        rot = jnp.concatenate([blk[:r0], -blk[r0 + g:r0 + 2 * g], blk[r0:r0 + g], -blk[r0 + 3 * g:r0 + 4 * g],
                               blk[r0 + 2 * g:r0 + 3 * g], blk[r0 + 4 * g:]], axis=0)
        qt_ref[hd * HEAD_PAD:(hd + 1) * HEAD_PAD, :] = (blk * cosq + rot * sinq).astype(BF16)
    dtt_ref[...] = _softplus(_dot_nt(wdt_ref[...], hb) + dtb_col_ref[...])


def _inproj(x, mod3, mod_row_fn, lw, tabs, tab_blocks, cache_out=None):
    t = x.shape[0]
    tm = ROW_TILE
    nb = t // tm
    cosq, sinq, cosk, sink = tabs
    ntab = tab_blocks
    const = lambda i: (0, 0)
    row = lambda i: (i, 0)
    col = lambda i: (0, i)
    in_specs = [
        pl.BlockSpec((tm, D_MODEL), row),
        pl.BlockSpec((1, 1, N_MOD * D_MODEL), lambda i: (mod_row_fn(ROW_TILE)(i), 0, 0)),
        pl.BlockSpec((1, D_MODEL), const),
        pl.BlockSpec((D_MODEL, IN_PAD), const),
        pl.BlockSpec((2 * SSD_HEADS, D_MODEL), const),
        pl.BlockSpec((1, LANE), const),
        pl.BlockSpec((2 * SSD_HEADS, 1), const),
        pl.BlockSpec((1, MLA_Q_RANK), const),
        pl.BlockSpec((MLA_HEADS * HEAD_PAD, MLA_Q_RANK), const),
        pl.BlockSpec((1, MLA_KV_RANK), const),
        pl.BlockSpec((MLA_KV_RANK, MLA_HEADS * HEAD_PAD), const),
        pl.BlockSpec((MLA_WIDTH, MLA_KV_RANK), const),
        pl.BlockSpec((HEAD_PAD, tm), lambda i: (0, i % ntab)),
        pl.BlockSpec((HEAD_PAD, tm), lambda i: (0, i % ntab)),
        pl.BlockSpec((tm, LANE), lambda i: (i % ntab, 0)),
        pl.BlockSpec((tm, LANE), lambda i: (i % ntab, 0)),
    ]
    out_specs = [
        pl.BlockSpec((tm, SSD_WIDTH), row),
        pl.BlockSpec((tm, SSD_XBC), row),
        pl.BlockSpec((tm, LANE), row),
        pl.BlockSpec((2 * SSD_HEADS, tm), col),
        pl.BlockSpec((MLA_HEADS * HEAD_PAD, tm), col),
        pl.BlockSpec((tm, MLA_HEADS * HEAD_PAD), row),
        pl.BlockSpec((MLA_WIDTH, tm), col),
    ]
    out_shape = [
        jax.ShapeDtypeStruct((t, SSD_WIDTH), F32),
        jax.ShapeDtypeStruct((t, SSD_XBC), F32),
        jax.ShapeDtypeStruct((t, LANE), F32),
        jax.ShapeDtypeStruct((2 * SSD_HEADS, t), F32),
        jax.ShapeDtypeStruct((MLA_HEADS * HEAD_PAD, t), BF16),
        jax.ShapeDtypeStruct((t, MLA_HEADS * HEAD_PAD), BF16),
        jax.ShapeDtypeStruct((MLA_WIDTH, t), BF16),
    ]
    args = [x, mod3, lw["g_pre1"], lw["w_in"], lw["w_dt"], lw["dtb_row"], lw["dtb_col"],
            lw["q_norm"], lw["w_q"], lw["kv_norm"], lw["w_uk"], lw["w_uv"], cosq, sinq, cosk, sink]
    aliases = {}
    n_alias = 0
    if cache_out is not None:
        layer, depth, seq, bufs = cache_out
        nb_seq = tm // seq
        for rank in (MLA_KV_RANK, MLA_ROPE_DIM):
            out_specs.append(pl.BlockSpec((nb_seq, 1, seq, rank), lambda i: (i, layer, 0, 0)))
            out_shape.append(jax.ShapeDtypeStruct((t // seq, depth, seq, rank), F32))
        if bufs is not None:
            n_alias = len(bufs)
            for j, buf in enumerate(bufs):
                aliases[len(args)] = len(out_shape) - n_alias + j
                in_specs.append(pl.BlockSpec(memory_space=pl.ANY))
                args.append(buf)
    return pl.pallas_call(
        functools.partial(_inproj_kernel, emit_cache=cache_out is not None, n_alias=n_alias),
        grid=(nb,),
        in_specs=in_specs,
        out_specs=out_specs,
        out_shape=out_shape,
        input_output_aliases=aliases,
        compiler_params=_cparams("arbitrary"),
        name="inproj",
    )(*args)


def _kvcache_kernel(ckv_ref, kr_ref, wuk_ref, wuv_ref, k_ref, vt_ref):
    ckvb = ckv_ref[...].astype(BF16)
    knp = _dot(ckvb, wuk_ref[...])
    kr = kr_ref[...]
    for hd in range(MLA_HEADS):
        sl = slice(hd * HEAD_PAD, (hd + 1) * HEAD_PAD)
        k_ref[:, sl] = (knp[:, sl] + kr).astype(BF16)
    vt_ref[...] = _dot_nt(wuv_ref[...], ckvb).astype(BF16)


def _kvcache(ckv, kr_tile, lw):
    n = ckv.shape[0]
    tm = 512
    return pl.pallas_call(
        _kvcache_kernel,
        grid=(n // tm,),
        in_specs=[
            pl.BlockSpec((tm, MLA_KV_RANK), lambda i: (i, 0)),
            pl.BlockSpec((tm, LANE), lambda i: (i, 0)),
            pl.BlockSpec((MLA_KV_RANK, MLA_HEADS * HEAD_PAD), lambda i: (0, 0)),
            pl.BlockSpec((MLA_WIDTH, MLA_KV_RANK), lambda i: (0, 0)),
        ],
        out_specs=[
            pl.BlockSpec((tm, MLA_HEADS * HEAD_PAD), lambda i: (i, 0)),
            pl.BlockSpec((MLA_WIDTH, tm), lambda i: (0, i)),
        ],
        out_shape=[
            jax.ShapeDtypeStruct((n, MLA_HEADS * HEAD_PAD), BF16),
            jax.ShapeDtypeStruct((MLA_WIDTH, n), BF16),
        ],
        compiler_params=_cparams("arbitrary"),
        name="kvcache",
    )(ckv, kr_tile, lw["w_uk"], lw["w_uv"])


def _head_expand_matrix():
    r = lax.broadcasted_iota(jnp.int32, (LANE, 2 * SSD_WIDTH), 0)
    c = lax.broadcasted_iota(jnp.int32, (LANE, 2 * SSD_WIDTH), 1)
    return jnp.where(c // SSD_HEAD_DIM == r, 1.0, 0.0).astype(BF16)


def _expand_heads(v, emat):
    hi = v.astype(BF16)
    mid = (v - hi.astype(F32)).astype(BF16)
    return _dot(hi, emat) + _dot(mid, emat)


def _prefix_rows(dta, tril):
    return sum(_dot(tril, p) for p in _split3(dta))


def _chunk_masks(q):
    r_i = lax.broadcasted_iota(jnp.int32, (q, q), 0)
    c_i = lax.broadcasted_iota(jnp.int32, (q, q), 1)
    return r_i >= c_i, r_i <= c_i


def _ssd_state_kernel(*refs, cps, nc, nsq, has_h0, has_sink):
    it = iter(refs)
    xbc_ref = next(it)
    prev_ref, next_ref = (next(it), next(it)) if nsq == 0 else (None, None)
    dt_ref, cw_ref, cb_ref, alog_row_ref = (next(it) for _ in range(4))
    h0_ref = next(it) if has_h0 else None
    if has_sink:
        next(it)
    xcb_ref, hsf_ref, sb_ref, dec_ref, hfin_ref, st_ref = (next(it) for _ in range(6))

    q = SSD_CHUNK
    rows = nc * q
    cw = cw_ref[...]

    def conv_silu(ext, n):
        acc = cb_ref[...] + ext[SUBLANE - 2:SUBLANE - 2 + n] * cw[0:1]
        for k in range(1, SSD_CONV):
            o = SUBLANE - 2 + k
            acc = acc + ext[o:o + n] * cw[k:k + 1]
        return _silu(acc).astype(BF16)

    if nsq == 0:
        pos = (pl.program_id(0) * nc) % cps
        seq_first = pos == 0
        seq_last = pos + nc == cps
        prev = jnp.where(seq_first, 0.0, prev_ref[...])
        nxt = jnp.where(seq_last, 0.0, next_ref[...])
        xcb_all = conv_silu(jnp.concatenate([prev, xbc_ref[...], nxt], axis=0), rows)
    else:
        ln = cps * q
        pad = jnp.zeros((SUBLANE, SSD_XBC), F32)
        xcb_all = jnp.concatenate(
            [conv_silu(jnp.concatenate([pad, xbc_ref[sq * ln:(sq + 1) * ln, :], pad], axis=0), ln)
             for sq in range(nsq)], axis=0)
    xcb_ref[...] = xcb_all

    lower, _ = _chunk_masks(q)
    tril = jnp.where(lower, 1.0, 0.0).astype(BF16)
    a_row = -jnp.exp(alog_row_ref[...])
    lane = lax.broadcasted_iota(jnp.int32, (q, LANE), 1)
    lane_t = lax.broadcasted_iota(jnp.int32, (2 * SUBLANE, LANE), 1)
    emat = _head_expand_matrix()
    gw = SSD_WIDTH // SSD_GROUPS

    if nsq == 0:
        @pl.when(seq_first)
        def _():
            if has_h0:
                st_ref[...] = h0_ref[0]
            else:
                st_ref[...] = jnp.zeros_like(st_ref)

    chunk_dec, chunk_states = [], []
    for c in range(nc):
        sl = slice(c * q, (c + 1) * q)
        xcb = xcb_all[sl]
        xs = xcb[:, :SSD_WIDTH].astype(F32)
        dt = dt_ref[sl, :]
        dta = dt * a_row
        la = _prefix_rows(dta, tril)
        tot = la[q - 1:q, :]
        w = jnp.exp(jnp.where(lane < SSD_HEADS, tot - la, la - dta)) * dt
        w = jnp.where(lane < 2 * SSD_HEADS, w, 0.0)
        wexp = _expand_heads(w, emat)
        etot = jnp.where(lane_t < 2 * SSD_HEADS, jnp.exp(jnp.broadcast_to(tot, (2 * SUBLANE, LANE))), 0.0)
        dec = _expand_heads(etot, emat)[:SUBLANE]
        dec_ref[c] = dec
        bmb = xcb[:, SSD_WIDTH:SSD_WIDTH + SSD_GROUPS * SSD_STATE]
        states = []
        for d in range(2):
            xw = (xs * wexp[:, d * SSD_WIDTH:(d + 1) * SSD_WIDTH]).astype(BF16)
            parts = [_dot_tn(bmb[:, grp * SSD_STATE:(grp + 1) * SSD_STATE], xw[:, grp * gw:(grp + 1) * gw])
                     for grp in range(SSD_GROUPS)]
            states.append(jnp.concatenate(parts, axis=1))
        sb_ref[c] = states[1]
        chunk_dec.append(dec[0:1, :SSD_WIDTH])
        chunk_states.append(states[0])

    if nsq == 0:
        hs = st_ref[...]
        for c in range(nc):
            hsf_ref[c] = hs.astype(BF16)
            hs = hs * chunk_dec[c] + chunk_states[c]
        st_ref[...] = hs

        @pl.when(seq_last)
        def _():
            hfin_ref[0, 0] = hs.T
    else:
        for sq in range(nsq):
            hs = h0_ref[sq] if has_h0 else jnp.zeros(st_ref.shape, F32)
            for c in range(sq * cps, (sq + 1) * cps):
                hsf_ref[c] = hs.astype(BF16)
                hs = hs * chunk_dec[c] + chunk_states[c]
            hfin_ref[sq, 0] = hs.T


def _ssd_out_kernel(*refs, cps, nc, nsq, has_h0, has_sink, nsteps):
    it = iter(refs)
    (xcb_ref, dt_ref, dtt_ref, z_ref, hsf_ref, sb_ref, dec_ref,
     alog_row_ref, alog_col_ref, dskip_ref, gn_ref) = (next(it) for _ in range(11))
    h0_ref = next(it) if has_h0 else None
    if has_sink:
        next(it)
    y_ref, hfin_ref, st_ref = next(it), next(it), next(it)

    q = SSD_CHUNK
    pos = ((nsteps - 1 - pl.program_id(0)) * nc) % cps
    seq_first = pos == 0
    seq_last = pos + nc == cps
    log2e = math.log2(math.e)

    lower, upper = _chunk_masks(q)
    tril = jnp.where(lower, 1.0, 0.0).astype(BF16)
    triu = jnp.where(upper, 1.0, 0.0).astype(BF16)
    a_row = -jnp.exp(alog_row_ref[...])
    a_col = -jnp.exp(alog_col_ref[...])
    lane = lax.broadcasted_iota(jnp.int32, (q, LANE), 1)
    rowi = lax.broadcasted_iota(jnp.int32, (2 * SUBLANE, q), 0)
    emat = _head_expand_matrix()
    rep = SSD_HEADS // SSD_GROUPS
    gw = SSD_WIDTH // SSD_GROUPS
    neg = jnp.float32(-jnp.inf)

    if nsq == 0:
        @pl.when(seq_last)
        def _():
            if has_h0:
                st_ref[...] = h0_ref[0]
            else:
                st_ref[...] = jnp.zeros_like(st_ref)

    def chunk_terms(c):
        sl = slice(c * q, (c + 1) * q)
        xcb = xcb_ref[sl, :]
        xsb = xcb[:, :SSD_WIDTH]
        bmb = xcb[:, SSD_WIDTH:SSD_WIDTH + SSD_GROUPS * SSD_STATE]
        cmb = xcb[:, SSD_WIDTH + SSD_GROUPS * SSD_STATE:]
        dt = dt_ref[sl, :]
        dtt = dtt_ref[:, sl]
        dta = dt * a_row
        dtat = dtt * a_col
        la = _prefix_rows(dta, tril)
        tot = la[q - 1:q, :]
        lcol = jnp.where(lane < SSD_HEADS, la, tot - la + dta)
        lat = sum(_dot(p, triu) for p in _split3(dtat))
        tott = lat[:, q - 1:q]
        lrow = jnp.where(rowi < SSD_HEADS, lat, tott - lat + dtat)
        lcol2 = lcol * log2e
        lrow2 = (lrow - jnp.log(dtt)) * log2e
        ecol = jnp.where(lane < 2 * SSD_HEADS, jnp.exp(lcol), 0.0)
        eexp = _expand_heads(ecol, emat)

        cbs = []
        for grp in range(SSD_GROUPS):
            cg = cmb[:, grp * SSD_STATE:(grp + 1) * SSD_STATE]
            bg = bmb[:, grp * SSD_STATE:(grp + 1) * SSD_STATE]
            cbs.append(_dot_nt(cg, bg))
        tiles = []
        for pair in range(SSD_HEADS // 2):
            xpair = xsb[:, pair * LANE:(pair + 1) * LANE]
            res = []
            for hd in (2 * pair, 2 * pair + 1):
                jf, jb = hd, SSD_HEADS + hd
                ef = jnp.exp2(jnp.where(lower, lcol2[:, jf:jf + 1] - lrow2[jf:jf + 1, :], neg))
                eb = jnp.exp2(jnp.where(upper, lcol2[:, jb:jb + 1] - lrow2[jb:jb + 1, :], neg))
                mm = (cbs[hd // rep] * (ef + eb)).astype(BF16)
                res.append(_dot(mm, xpair))
            tiles.append(jnp.where(lane < SSD_HEAD_DIM, res[0], res[1]))
        y = jnp.concatenate(tiles, axis=1)
        hsf = hsf_ref[c]
        parts = [_dot(cmb[:, grp * SSD_STATE:(grp + 1) * SSD_STATE], hsf[:, grp * gw:(grp + 1) * gw])
                 for grp in range(SSD_GROUPS)]
        y = y + jnp.concatenate(parts, axis=1) * eexp[:, :SSD_WIDTH]
        y = y + dskip_ref[...] * xsb.astype(F32)
        return y, cmb, eexp[:, SSD_WIDTH:], _silu(z_ref[sl, :])

    terms = [chunk_terms(c) for c in range(nc)]

    def finish_chunk(c, hb):
        y, cmb, eexp_b, gate = terms[c]
        hsb = hb.astype(BF16)
        parts = [_dot(cmb[:, grp * SSD_STATE:(grp + 1) * SSD_STATE], hsb[:, grp * gw:(grp + 1) * gw])
                 for grp in range(SSD_GROUPS)]
        y = (y + jnp.concatenate(parts, axis=1) * eexp_b) * gate
        y_ref[c * q:(c + 1) * q, :] = _rms_rows(y, gn_ref[...]).astype(BF16)
        return hb * dec_ref[c][0:1, SSD_WIDTH:] + sb_ref[c]

    if nsq == 0:
        hb = st_ref[...]
        for c in reversed(range(nc)):
            hb = finish_chunk(c, hb)
        st_ref[...] = hb

        @pl.when(seq_first)
        def _():
            hfin_ref[0, 0] = hb.T
    else:
        for sq in range(nsq):
            hb = h0_ref[sq] if has_h0 else jnp.zeros(st_ref.shape, F32)
            for c in reversed(range(sq * cps, (sq + 1) * cps)):
                hb = finish_chunk(c, hb)
            hfin_ref[sq, 0] = hb.T


def _ssd(xbc, dt, dtt, z, lw, cps, h0=None, sink=(0, 2, None)):
    slot0, nslots, sink_buf = sink
    t = xbc.shape[0]
    q = SSD_CHUNK
    nchunks = t // q
    nseq = nchunks // cps
    nc = SSD_STEP_CHUNKS
    if cps >= nc:
        assert cps % nc == 0
        nsq, spq, sps = 0, cps // nc, 1
    else:
        assert nc % cps == 0 and nseq % (nc // cps) == 0
        nsq, spq = nc // cps, 1
        sps = nsq
    rows = nc * q
    nsteps = nchunks // nc
    hb = rows // SUBLANE
    n8 = t // SUBLANE
    has_h0 = h0 is not None
    const = lambda i: (0, 0)
    st_block = (sps, SSD_STATE, SSD_WIDTH)
    ch_block = (nc, SSD_STATE, SSD_WIDTH)
    fin_block = (sps, 1, SSD_WIDTH, SSD_STATE)
    fin_shape = jax.ShapeDtypeStruct((nseq, nslots, SSD_WIDTH, SSD_STATE), F32)
    any_spec = pl.BlockSpec(memory_space=pl.ANY)
    dec_block = (nc, SUBLANE, 2 * SSD_WIDTH)

    in_specs = [pl.BlockSpec((rows, SSD_XBC), lambda i: (i, 0))]
    args = [xbc]
    if nsq == 0:
        in_specs += [
            pl.BlockSpec((SUBLANE, SSD_XBC), lambda i: (jnp.maximum(i * hb - 1, 0), 0)),
            pl.BlockSpec((SUBLANE, SSD_XBC), lambda i: (jnp.minimum((i + 1) * hb, n8 - 1), 0)),
        ]
        args += [xbc, xbc]
    in_specs += [
        pl.BlockSpec((rows, LANE), lambda i: (i, 0)),
        pl.BlockSpec((SSD_CONV, SSD_XBC), const),
        pl.BlockSpec((1, SSD_XBC), const),
        pl.BlockSpec((1, LANE), const),
    ]
    args += [dt, lw["conv_w"], lw["conv_b"], lw["alog_row"]]
    if has_h0:
        in_specs.append(pl.BlockSpec(st_block, lambda i: (i // spq, 0, 0)))
        args.append(h0[0])
    aliases = {}
    if sink_buf is not None:
        aliases[len(args)] = 4
        in_specs.append(any_spec)
        args.append(sink_buf)
    xcb, hsf, sb, dec, fin = pl.pallas_call(
        functools.partial(_ssd_state_kernel, cps=cps, nc=nc, nsq=nsq, has_h0=has_h0, has_sink=sink_buf is not None),
        grid=(nsteps,),
        in_specs=in_specs,
        out_specs=[
            pl.BlockSpec((rows, SSD_XBC), lambda i: (i, 0)),
            pl.BlockSpec(ch_block, lambda i: (i, 0, 0)),
            pl.BlockSpec(ch_block, lambda i: (i, 0, 0)),
            pl.BlockSpec(dec_block, lambda i: (i, 0, 0)),
            pl.BlockSpec(fin_block, lambda i: (i // spq, slot0, 0, 0)),
        ],
        out_shape=[
            jax.ShapeDtypeStruct((t, SSD_XBC), BF16),
            jax.ShapeDtypeStruct((nchunks, SSD_STATE, SSD_WIDTH), BF16),
            jax.ShapeDtypeStruct((nchunks, SSD_STATE, SSD_WIDTH), F32),
            jax.ShapeDtypeStruct((nchunks, SUBLANE, 2 * SSD_WIDTH), F32),
            fin_shape,
        ],
        input_output_aliases=aliases,
        scratch_shapes=[pltpu.VMEM((SSD_STATE, SSD_WIDTH), F32)],
        compiler_params=_cparams("arbitrary"),
        name="ssd_state",
    )(*args)

    gi = lambda i: nsteps - 1 - i
    in_specs = [
        pl.BlockSpec((rows, SSD_XBC), lambda i: (gi(i), 0)),
        pl.BlockSpec((rows, LANE), lambda i: (gi(i), 0)),
        pl.BlockSpec((2 * SSD_HEADS, rows), lambda i: (0, gi(i))),
        pl.BlockSpec((rows, SSD_WIDTH), lambda i: (gi(i), 0)),
        pl.BlockSpec(ch_block, lambda i: (gi(i), 0, 0)),
        pl.BlockSpec(ch_block, lambda i: (gi(i), 0, 0)),
        pl.BlockSpec(dec_block, lambda i: (gi(i), 0, 0)),
        pl.BlockSpec((1, LANE), const),
        pl.BlockSpec((2 * SSD_HEADS, 1), const),
        pl.BlockSpec((1, SSD_WIDTH), const),
        pl.BlockSpec((1, SSD_WIDTH), const),
    ]
    args = [xcb, dt, dtt, z, hsf, sb, dec, lw["alog_row"], lw["alog_col"], lw["dskip_row"], lw["ssd_norm"]]
    if has_h0:
        in_specs.append(pl.BlockSpec(st_block, lambda i: (gi(i) // spq, 0, 0)))
        args.append(h0[1])
    in_specs.append(any_spec)
    args.append(fin)
    y, fin = pl.pallas_call(
        functools.partial(_ssd_out_kernel, cps=cps, nc=nc, nsq=nsq, has_h0=has_h0, has_sink=True, nsteps=nsteps),
        grid=(nsteps,),
        in_specs=in_specs,
        out_specs=[
            pl.BlockSpec((rows, SSD_WIDTH), lambda i: (gi(i), 0)),
            pl.BlockSpec(fin_block, lambda i: (gi(i) // spq, slot0 + 1, 0, 0)),
        ],
        out_shape=[jax.ShapeDtypeStruct((t, SSD_WIDTH), BF16), fin_shape],
        input_output_aliases={len(args) - 1: 1},
        scratch_shapes=[pltpu.VMEM((SSD_STATE, SSD_WIDTH), F32)],
        compiler_params=_cparams("arbitrary"),
        name="ssd_out",
    )(*args)
    return y, fin


def _state_to_kernel_layout(h):
    n = h.shape[0]
    return h.transpose(0, 3, 1, 2).reshape(n, SSD_STATE, SSD_WIDTH)


def _attn_kernel(*refs, heads, has_cache, nseq_step):
    if has_cache:
        qt_ref, k_ref, vt_ref, kc_ref, vct_ref = refs[:5]
    else:
        qt_ref, k_ref, vt_ref = refs[:3]
    n_in = 5 if has_cache else 3
    o_ref = refs[n_in]
    scratch = refs[n_in + 1:]
    tq = qt_ref.shape[1] // nseq_step
    lk = k_ref.shape[0] // nseq_step
    kb = min(ATTN_KEY_BLOCK, lk)
    ones = jnp.ones((ONES_ROWS, kb), BF16)

    for sq in range(nseq_step):
        s_refs = scratch[2 * sq:2 * sq + 2]
        qcols = slice(sq * tq, (sq + 1) * tq)
        blocks = [(k_ref, vt_ref, sq * lk + i * kb) for i in range(lk // kb)]
        if has_cache:
            assert nseq_step == 1
            lc = kc_ref.shape[0]
            assert min(ATTN_KEY_BLOCK, lc) == kb
            blocks += [(kc_ref, vct_ref, i * kb) for i in range(lc // kb)]
        nblk = len(blocks)

        def score_block(hd, i, m, blocks=blocks, s_refs=s_refs, qcols=qcols):
            kr, _, off = blocks[i]
            q = qt_ref[hd * HEAD_PAD:(hd + 1) * HEAD_PAD, qcols]
            s = _dot(kr[off:off + kb, hd * HEAD_PAD:(hd + 1) * HEAD_PAD], q)
            s_refs[hd % 2][i * kb:(i + 1) * kb, :] = s
            bm = jnp.max(s, axis=0, keepdims=True)
            return bm if m is None else jnp.maximum(m, bm)

        def value_block(hd, i, m, acc, blocks=blocks, s_refs=s_refs):
            _, vr, off = blocks[i]
            p = jnp.exp2((s_refs[hd % 2][i * kb:(i + 1) * kb, :] - m).astype(BF16))
            v = vr[hd * MLA_V_DIM:(hd + 1) * MLA_V_DIM, off:off + kb]
            part = _dot(jnp.concatenate([v, ones], axis=0), p)
            return part if acc is None else acc + part

        m_cur = None
        for i in range(nblk):
            m_cur = score_block(0, i, m_cur)
        for hd in range(heads):
            m_next, acc = None, None
            for i in range(nblk):
                if nblk == 1 and hd + 1 < heads:
                    m_next = score_block(hd + 1, i, m_next)
                acc = value_block(hd, i, m_cur, acc)
                if nblk > 1 and hd + 1 < heads:
                    m_next = score_block(hd + 1, i, m_next)
            vs = slice(hd * MLA_V_DIM, (hd + 1) * MLA_V_DIM)
            o_ref[vs, qcols] = acc[:MLA_V_DIM] / acc[MLA_V_DIM:MLA_V_DIM + 1]
            m_cur = m_next


def _attention(qt, k, vt, seq_len, heads_per_step, cache=None):
    t = k.shape[0]
    nseq = t // seq_len
    tq = min(ATTN_Q_TILE, seq_len)
    nq = seq_len // tq
    g = heads_per_step
    nss = ATTN_SEQS_PER_STEP if (nq == 1 and cache is None and nseq % ATTN_SEQS_PER_STEP == 0) else 1
    in_specs = [
        pl.BlockSpec((g * HEAD_PAD, nss * tq), lambda s, h, j: (h, s * nq + j)),
        pl.BlockSpec((nss * seq_len, g * HEAD_PAD), lambda s, h, j: (s, h)),
        pl.BlockSpec((g * MLA_V_DIM, nss * seq_len), lambda s, h, j: (h, s)),
    ]
    args = [qt, k, vt]
    n_keys = seq_len
    if cache is not None:
        kc, vct = cache
        past = kc.shape[0] // nseq
        n_keys += past
        in_specs += [
            pl.BlockSpec((past, g * HEAD_PAD), lambda s, h, j: (s, h)),
            pl.BlockSpec((g * MLA_V_DIM, past), lambda s, h, j: (h, s)),
        ]
        args += [kc, vct]
    kern = functools.partial(_attn_kernel, heads=g, has_cache=cache is not None, nseq_step=nss)
    return pl.pallas_call(
        kern,
        grid=(nseq // nss, MLA_HEADS // g, nq),
        in_specs=in_specs,
        out_specs=pl.BlockSpec((g * MLA_V_DIM, nss * tq), lambda s, h, j: (h, s * nq + j)),
        out_shape=jax.ShapeDtypeStruct((MLA_WIDTH, t), F32),
        scratch_shapes=[pltpu.VMEM((n_keys, tq), F32) for _ in range(2 * nss)],
        compiler_params=_cparams("arbitrary", "arbitrary", "arbitrary"),
        name="attention",
    )(*args)


def _outproj_kernel(*refs, has_router):
    if has_router:
        (y_ref, ot_ref, x_ref, mod_ref, wa_ref, wb_ref, gm_ref, gpost_ref, gpre2_ref, rt_ref,
         x1_ref, h2_ref, comb_ref) = refs
    else:
        (y_ref, ot_ref, x_ref, mod_ref, wa_ref, wb_ref, gm_ref, gpost_ref, gpre2_ref,
         x1_ref, h2_ref) = refs
    mod = mod_ref[0]
    gate1 = mod[:, 2 * D_MODEL:3 * D_MODEL]
    shift2 = mod[:, 3 * D_MODEL:4 * D_MODEL]
    scale2 = mod[:, 4 * D_MODEL:5 * D_MODEL]
    ot = ot_ref[...]
    ms = jnp.mean(ot * ot, axis=0, keepdims=True)
    on = (ot * lax.rsqrt(ms + EPS) * gm_ref[...]).astype(BF16)
    y = _dot(y_ref[...], wa_ref[...]) + _dot_tn(on, wb_ref[...])
    x1 = x_ref[...] + gate1 * _rms_rows(y, gpost_ref[...])
    x1_ref[...] = x1
    h2 = _rms_rows(x1, gpre2_ref[...]) * (1.0 + scale2) + shift2
    h2_ref[...] = h2.astype(BF16)
    if has_router:
        hh, hm, _ = _split3(h2)
        rh, rm, _ = _split3(rt_ref[...])
        logits = _dot(hh, rh) + (_dot(hm, rh) + _dot(hh, rm))
        lane = lax.broadcasted_iota(jnp.int32, logits.shape, 1).astype(F32)
        neg = jnp.float32(-jnp.inf)
        lg = jnp.where(lane < N_EXPERTS, logits, neg)
        m1 = jnp.max(lg, axis=-1, keepdims=True)
        i1 = jnp.min(jnp.where(lg == m1, lane, float(LANE)), axis=-1, keepdims=True)
        lg2 = jnp.where(lane == i1, neg, lg)
        m2 = jnp.max(lg2, axis=-1, keepdims=True)
        i2 = jnp.min(jnp.where(lg2 == m2, lane, float(LANE)), axis=-1, keepdims=True)
        e2 = jnp.exp(m2 - m1)
        w1 = 1.0 / (1.0 + e2)
        w2 = e2 / (1.0 + e2)
        comb_ref[...] = jnp.where(lane == i1, w1, 0.0) + jnp.where(lane == i2, w2, 0.0)


def _outproj(yssd, ot, x, mod3, mod_row_fn, lw, router=None):
    t = x.shape[0]
    tm = ROW_TILE
    const = lambda i: (0, 0)
    row = lambda i: (i, 0)
    in_specs = [
        pl.BlockSpec((tm, SSD_WIDTH), row),
        pl.BlockSpec((MLA_WIDTH, tm), lambda i: (0, i)),
        pl.BlockSpec((tm, D_MODEL), row),
        pl.BlockSpec((1, 1, N_MOD * D_MODEL), lambda i: (mod_row_fn(ROW_TILE)(i), 0, 0)),
        pl.BlockSpec((SSD_WIDTH, D_MODEL), const),
        pl.BlockSpec((MLA_WIDTH, D_MODEL), const),
        pl.BlockSpec((MLA_WIDTH, 1), const),
        pl.BlockSpec((1, D_MODEL), const),
        pl.BlockSpec((1, D_MODEL), const),
    ]
    args = [yssd, ot, x, mod3, lw["w_out_a"], lw["w_out_b"], lw["mla_norm_col"],
            lw["g_post1"], lw["g_pre2"]]
    out_specs = [pl.BlockSpec((tm, D_MODEL), row), pl.BlockSpec((tm, D_MODEL), row)]
    out_shape = [jax.ShapeDtypeStruct((t, D_MODEL), F32), jax.ShapeDtypeStruct((t, D_MODEL), BF16)]
    if router is not None:
        in_specs.append(pl.BlockSpec((D_MODEL, LANE), const))
        args.append(router)
        out_specs.append(pl.BlockSpec((tm, LANE), row))
        out_shape.append(jax.ShapeDtypeStruct((t, LANE), F32))
    return pl.pallas_call(
        functools.partial(_outproj_kernel, has_router=router is not None),
        grid=(t // tm,),
        in_specs=in_specs,
        out_specs=out_specs,
        out_shape=out_shape,
        compiler_params=_cparams("arbitrary"),
        name="outproj",
    )(*args)


def _ffn_kernel(*refs, has_comb, nslab):
    if has_comb:
        h_ref, x_ref, comb_ref, mod_ref, wgu_ref, wd_ref, gpost_ref, o_ref, acc_ref = refs
    else:
        h_ref, x_ref, mod_ref, wgu_ref, wd_ref, gpost_ref, o_ref, acc_ref = refs
    e = pl.program_id(1)
    h = h_ref[...]
    f = wd_ref.shape[1]
    gu = _dot(h, wgu_ref[0])
    hid = _silu(gu[:, :f]) * gu[:, f:]
    if has_comb:
        comb = comb_ref[...]
        lane = lax.broadcasted_iota(jnp.int32, comb.shape, 1)
        wcol = jnp.sum(jnp.where(lane == e, comb, 0.0), axis=-1, keepdims=True)
        hid = hid * wcol
    part = _dot(hid.astype(BF16), wd_ref[0])

    @pl.when(e == 0)
    def _():
        acc_ref[...] = part

    @pl.when(e > 0)
    def _():
        acc_ref[...] += part

    @pl.when(e == nslab - 1)
    def _():
        gate2 = mod_ref[0][:, 5 * D_MODEL:6 * D_MODEL]
        o_ref[...] = x_ref[...] + gate2 * _rms_rows(acc_ref[...], gpost_ref[...])


def _ffn(h2, x1, mod3, mod_row_fn, wgu, wd, gpost, comb=None):
    t = x1.shape[0]
    tm = FFN_ROW_TILE
    nslab, f, _ = wd.shape
    row = lambda i, e: (i, 0)
    in_specs = [pl.BlockSpec((tm, D_MODEL), row), pl.BlockSpec((tm, D_MODEL), row)]
    args = [h2, x1]
    if comb is not None:
        in_specs.append(pl.BlockSpec((tm, LANE), row))
        args.append(comb)
    in_specs += [
        pl.BlockSpec((1, 1, N_MOD * D_MODEL), lambda i, e: (mod_row_fn(FFN_ROW_TILE)(i), 0, 0)),
        pl.BlockSpec((1, D_MODEL, 2 * f), lambda i, e: (e, 0, 0)),
        pl.BlockSpec((1, f, D_MODEL), lambda i, e: (e, 0, 0)),
        pl.BlockSpec((1, D_MODEL), lambda i, e: (0, 0)),
    ]
    args += [mod3, wgu, wd, gpost]
    return pl.pallas_call(
        functools.partial(_ffn_kernel, has_comb=comb is not None, nslab=nslab),
        grid=(t // tm, nslab),
        in_specs=in_specs,
        out_specs=pl.BlockSpec((tm, D_MODEL), row),
        out_shape=jax.ShapeDtypeStruct((t, D_MODEL), F32),
        scratch_shapes=[pltpu.VMEM((tm, D_MODEL), F32)],
        compiler_params=_cparams("arbitrary", "arbitrary"),
        name="ffn",
    )(*args)


def _moe_kernel(h_ref, x_ref, comb_ref, mod_ref, wg_ref, wu_ref, wd_ref, gpost_ref, o_ref,
                acc_ref, rank_ref, rank_t_ref, comb_t_ref, *, nexp, sub):
    e = pl.program_id(1)
    tm = h_ref.shape[0]
    caps = MOE_CAPS
    cmax = caps[-1]

    @pl.when(e == 0)
    def _():
        acc_ref[...] = jnp.zeros_like(acc_ref)
        r_i = lax.broadcasted_iota(jnp.int32, (sub, sub), 0)
        c_i = lax.broadcasted_iota(jnp.int32, (sub, sub), 1)
        strict = jnp.where(r_i > c_i, 1.0, 0.0).astype(BF16)
        for s in range(tm // sub):
            rows = slice(s * sub, (s + 1) * sub)
            comb = comb_ref[rows, :]
            rank = _dot(strict, jnp.where(comb > 0.0, 1.0, 0.0).astype(BF16))
            rank_ref[rows, :] = rank
            rank_t_ref[:, rows] = rank.T
            comb_t_ref[:, rows] = comb.T

    def expert_pass(rows, wcol, rcol, wrow, rrow, base, cap):
        capl = -(-cap // LANE) * LANE
        slot_l = lax.broadcasted_iota(jnp.int32, (sub, capl), 1).astype(F32)
        slot_s = lax.broadcasted_iota(jnp.int32, (cap, sub), 0).astype(F32)
        gather = jnp.where(((rrow - base) == slot_s) & (wrow > 0.0), 1.0, 0.0).astype(BF16)
        scatter = jnp.where(((rcol - base) == slot_l) & (wcol > 0.0) & (slot_l < float(cap)),
                            1.0, 0.0).astype(BF16)
        xg = _dot(gather, h_ref[rows, :]).astype(BF16)
        hid = _silu(_dot(xg, wg_ref[0])) * _dot(xg, wu_ref[0])
        y = _dot(hid.astype(BF16), wd_ref[0]).astype(BF16)
        if capl > cap:
            y = jnp.concatenate([y, jnp.zeros((capl - cap, y.shape[1]), BF16)], axis=0)
        acc_ref[rows, :] += wcol * _dot(scatter, y)

    lane = lax.broadcasted_iota(jnp.int32, (sub, LANE), 1)

    def sub_tile(s, carry):
        rows = pl.ds(pl.multiple_of(s * sub, sub), sub)
        pick = lane == e
        wcol = jnp.sum(jnp.where(pick, comb_ref[rows, :], 0.0), axis=-1, keepdims=True)
        rcol = jnp.sum(jnp.where(pick, rank_ref[rows, :], 0.0), axis=-1, keepdims=True)
        wrow = comb_t_ref[pl.ds(e, 1), rows]
        rrow = rank_t_ref[pl.ds(e, 1), rows]
        count = jnp.max(jnp.where(wrow > 0.0, rrow + 1.0, 0.0))
        npass = ((count + (cmax - 1.0)) * (1.0 / cmax)).astype(jnp.int32)
        nfull = jnp.maximum(npass - 1, 0)

        def full_pass(k, c):
            expert_pass(rows, wcol, rcol, wrow, rrow, (k * cmax).astype(F32), cmax)
            return c

        lax.fori_loop(0, nfull, full_pass, 0)
        base = (nfull * cmax).astype(F32)
        left = count - base
        lo = 0
        for cap in caps:
            @pl.when(jnp.logical_and(left > float(lo), left <= float(cap)))
            def _(cap=cap):
                expert_pass(rows, wcol, rcol, wrow, rrow, base, cap)
            lo = cap
        return carry

    lax.fori_loop(0, tm // sub, sub_tile, 0)

    @pl.when(e == nexp - 1)
    def _():
        gate2 = mod_ref[0][:, 5 * D_MODEL:6 * D_MODEL]
        o_ref[...] = x_ref[...] + gate2 * _rms_rows(acc_ref[...], gpost_ref[...])


def _moe(h2, x1, comb, mod3, mod_row_fn, wg, wu, wd, gpost):
    t = x1.shape[0]
    tm = min(MOE_ROW_TILE, t)
    sub = min(MOE_SUB_TILE, tm)
    nexp, f, _ = wd.shape
    row = lambda i, e: (i, 0)
    return pl.pallas_call(
        functools.partial(_moe_kernel, nexp=nexp, sub=sub),
        grid=(t // tm, nexp),
        in_specs=[
            pl.BlockSpec((tm, D_MODEL), row),
            pl.BlockSpec((tm, D_MODEL), row),
            pl.BlockSpec((tm, LANE), row),
            pl.BlockSpec((1, 1, N_MOD * D_MODEL), lambda i, e: (mod_row_fn(tm)(i), 0, 0)),
            pl.BlockSpec((1, D_MODEL, f), lambda i, e: (e, 0, 0)),
            pl.BlockSpec((1, D_MODEL, f), lambda i, e: (e, 0, 0)),
            pl.BlockSpec((1, f, D_MODEL), lambda i, e: (e, 0, 0)),
            pl.BlockSpec((1, D_MODEL), lambda i, e: (0, 0)),
        ],
        out_specs=pl.BlockSpec((tm, D_MODEL), row),
        out_shape=jax.ShapeDtypeStruct((t, D_MODEL), F32),
        scratch_shapes=[pltpu.VMEM((tm, D_MODEL), F32), pltpu.VMEM((tm, LANE), F32),
                        pltpu.VMEM((LANE, tm), F32), pltpu.VMEM((LANE, tm), F32)],
        compiler_params=_cparams("arbitrary", "arbitrary"),
        name="moe",
    )(h2, x1, comb, mod3, wg, wu, wd, gpost)


def _prep_layer(i, p):
    w_in = p["w_in"][i]
    s1 = SSD_WIDTH
    s2 = s1 + SSD_XBC
    s3 = s2 + 2 * SSD_HEADS
    s4 = s3 + MLA_Q_RANK
    s5 = s4 + MLA_KV_RANK
    w_z, w_xbc, w_dt, w_cq, w_ckv, w_kr = (w_in[:, :s1], w_in[:, s1:s2], w_in[:, s2:s3],
                                             w_in[:, s3:s4], w_in[:, s4:s5], w_in[:, s5:])
    zc = lambda n: jnp.zeros((D_MODEL, n), F32)
    tile_a = jnp.concatenate([w_dt, zc(ROPE_LANE0 - 2 * SSD_HEADS), w_kr,
                              zc(LANE - ROPE_LANE0 - MLA_ROPE_DIM)], axis=1)
    w_in_pad = jnp.concatenate([w_z, w_xbc, w_cq, w_ckv, tile_a], axis=1).astype(BF16)

    w_uq = p["w_uq"][i].reshape(MLA_Q_RANK, MLA_HEADS, MLA_NOPE_DIM + MLA_ROPE_DIM)
    q_nope, q_rope = w_uq[..., :MLA_NOPE_DIM], w_uq[..., MLA_NOPE_DIM:]
    zq = lambda n: jnp.zeros((MLA_Q_RANK, MLA_HEADS, n), F32)
    pad = HEAD_PAD - MLA_NOPE_DIM - MLA_ROPE_DIM
    w_q = jnp.concatenate([q_nope, q_rope, zq(pad)], axis=-1).reshape(MLA_Q_RANK, -1)

    w_ukv = p["w_ukv"][i].reshape(MLA_KV_RANK, MLA_HEADS, MLA_NOPE_DIM + MLA_V_DIM)
    k_nope, v_w = w_ukv[..., :MLA_NOPE_DIM], w_ukv[..., MLA_NOPE_DIM:]
    w_uk = jnp.concatenate([k_nope, jnp.zeros((MLA_KV_RANK, MLA_HEADS, HEAD_PAD - MLA_NOPE_DIM), F32)],
                           axis=-1).reshape(MLA_KV_RANK, -1)
    w_uv = v_w.reshape(MLA_KV_RANK, MLA_WIDTH)

    dtb = p["dt_bias"][i].reshape(2 * SSD_HEADS)
    alog = p["a_log"][i].reshape(2 * SSD_HEADS)
    padl = lambda v: jnp.pad(v, (0, LANE - v.shape[0])).reshape(1, LANE)
    w_out = p["w_out"][i]
    return {
        "g_pre1": p["norm_pre_mix"][i].reshape(1, D_MODEL),
        "g_post1": p["norm_post_mix"][i].reshape(1, D_MODEL),
        "g_pre2": p["norm_pre_ffn"][i].reshape(1, D_MODEL),
        "g_post2": p["norm_post_ffn"][i].reshape(1, D_MODEL),
        "w_in": w_in_pad,
        "w_dt": w_dt.T.astype(BF16),
        "dtb_row": padl(dtb),
        "dtb_col": dtb.reshape(-1, 1),
        "alog_row": padl(alog),
        "alog_col": alog.reshape(-1, 1),
        "q_norm": p["q_norm"][i].reshape(1, -1),
        "w_q": w_q.T.astype(BF16),
        "kv_norm": p["kv_norm"][i].reshape(1, -1),
        "w_uk": w_uk.astype(BF16),
        "w_uv": w_uv.T.astype(BF16),
        "conv_w": p["conv_w"][i],
        "conv_b": p["conv_b"][i].reshape(1, -1),
        "dskip_row": jnp.repeat(p["d_skip"][i], SSD_HEAD_DIM).reshape(1, -1),
        "ssd_norm": p["ssd_norm"][i].reshape(1, -1),
        "mla_norm_col": p["mla_norm"][i].reshape(-1, 1),
        "w_out_a": w_out[:SSD_WIDTH].astype(BF16),
        "w_out_b": w_out[SSD_WIDTH:].astype(BF16),
    }


def _rope_tables(n_tokens):
    rows = n_tokens // GRID_W
    row = np.repeat(np.arange(rows, dtype=np.float32), GRID_W)
    col = np.tile(np.arange(GRID_W, dtype=np.float32), rows)
    half = MLA_ROPE_DIM // 2
    inv = (np.float32(ROPE_THETA) ** (-np.arange(0, half, 2, dtype=np.float32) / np.float32(half))).astype(np.float32)
    ar = row[:, None] * inv[None, :]
    ac = col[:, None] * inv[None, :]
    ang = np.concatenate([ar, ar, ac, ac], axis=-1).astype(np.float32)
    return jnp.asarray(np.cos(ang), F32), jnp.asarray(np.sin(ang), F32)


def _attn_tables(cos, sin, n):
    scale = (MLA_NOPE_DIM + MLA_ROPE_DIM) ** -0.5 * math.log2(math.e)
    pad = HEAD_PAD - ROPE_LANE0 - MLA_ROPE_DIM
    cosk = jnp.concatenate([jnp.zeros((n, ROPE_LANE0), F32), cos, jnp.zeros((n, pad), F32)], axis=1)
    sink = jnp.concatenate([jnp.zeros((n, ROPE_LANE0), F32), sin, jnp.zeros((n, pad), F32)], axis=1)
    cosq = jnp.concatenate([jnp.ones((n, ROPE_LANE0), F32), cos, jnp.zeros((n, pad), F32)], axis=1)
    return (cosq * scale).T, (sink * scale).T, cosk, sink


def kernel(x_prompt, x_sample, cache_ckv, cache_krope, state_ssm, c, c_ctx, w_mod, b_mod, norm_pre_mix, norm_post_mix, norm_pre_ffn, norm_post_ffn, w_in, conv_w, conv_b, dt_bias, a_log, d_skip, ssd_norm, q_norm, w_uq, kv_norm, w_ukv, mla_norm, w_out, ffn_w_gate, ffn_w_up, ffn_w_down, moe_router, moe_w_gate, moe_w_up, moe_w_down):
    params = dict(w_in=w_in, conv_w=conv_w, conv_b=conv_b, dt_bias=dt_bias, a_log=a_log, d_skip=d_skip,
                  ssd_norm=ssd_norm, q_norm=q_norm, w_uq=w_uq, kv_norm=kv_norm, w_ukv=w_ukv,
                  mla_norm=mla_norm, w_out=w_out, norm_pre_mix=norm_pre_mix, norm_post_mix=norm_post_mix,
                  norm_pre_ffn=norm_pre_ffn, norm_post_ffn=norm_post_ffn)
    batch, seq, d = x_prompt.shape
    dec_batch, dec_seq, _ = x_sample.shape
    depth = w_in.shape[0]
    past = cache_ckv.shape[2]
    tm = ROW_TILE

    cvec = jnp.concatenate([c_ctx[None, :], c, jnp.zeros((SUBLANE - 1 - dec_batch, d), F32)], axis=0)
    mod = _modulation(cvec, w_mod, b_mod)

    ones = jnp.ones((tm, MLA_ROPE_DIM), F32)
    tabs_ctx = _attn_tables(ones, jnp.zeros_like(ones), tm)
    cos, sin = _rope_tables(dec_seq)
    tabs_lat = _attn_tables(cos, sin, dec_seq)
    lat_blocks = dec_seq // tm

    xp = x_prompt.reshape(batch * seq, d)
    xs = x_sample.reshape(dec_batch * dec_seq, d)
    cache_bufs, ssm_buf = None, None
    for i in range(depth):
        lw = _prep_layer(i, params)
        mod3 = mod[i].reshape(SUBLANE, 1, N_MOD * d)
        j = i // 2
        if i % 2 == 0:
            f = ffn_w_gate.shape[2] // 2
            wgu = jnp.stack([jnp.concatenate([ffn_w_gate[j][:, s * f:(s + 1) * f], ffn_w_up[j][:, s * f:(s + 1) * f]],
                                             axis=1) for s in range(2)], axis=0).astype(BF16)
            wd = ffn_w_down[j].reshape(2, f, d).astype(BF16)
            router = None
        else:
            wgu = (moe_w_gate[j].astype(BF16), moe_w_up[j].astype(BF16))
            wd = moe_w_down[j].astype(BF16)
            router = jnp.pad(moe_router[j], ((0, 0), (0, LANE - N_EXPERTS)))

        def run(x, row_fn, tabs, tab_blocks, seq_len, heads_per_step, ctx):
            cache_out = (i, depth, seq_len, cache_bufs) if ctx is None else None
            z, xbc, dt, dtt, qt, k, vt, *caches = _inproj(x, mod3, row_fn, lw, tabs, tab_blocks, cache_out)
            cps = seq_len // SSD_CHUNK
            h0 = None
            if ctx is not None:
                h0 = (_state_to_kernel_layout(ctx[2][:, 0]), _state_to_kernel_layout(ctx[2][:, 1]))
            sink = (2 * i, 2 * depth, ssm_buf) if ctx is None else (0, 2, None)
            yssd, fin = _ssd(xbc, dt, dtt, z, lw, cps, h0=h0, sink=sink)
            cache = None
            if ctx is not None:
                kr_tile = jnp.pad(ctx[1].reshape(-1, MLA_ROPE_DIM),
                                  ((0, 0), (ROPE_LANE0, HEAD_PAD - ROPE_LANE0 - MLA_ROPE_DIM)))
                cache = _kvcache(ctx[0].reshape(-1, MLA_KV_RANK), kr_tile, lw)
            ot = _attention(qt, k, vt, seq_len, heads_per_step, cache=cache)
            outs = _outproj(yssd, ot, x, mod3, row_fn, lw, router=router)
            x1, h2 = outs[0], outs[1]
            comb = outs[2] if router is not None else None
            if comb is None:
                x2 = _ffn(h2, x1, mod3, row_fn, wgu, wd, lw["g_post2"])
            else:
                x2 = _moe(h2, x1, comb, mod3, row_fn, *wgu, wd, lw["g_post2"])
            return x2, caches, fin

        xp, cache_bufs, ssm_buf = run(xp, lambda tile: (lambda b: 0), tabs_ctx, 1, seq, MLA_HEADS, None)
        xs, _, _ = run(xs, lambda tile: (lambda b: 1 + (b * tile) // dec_seq), tabs_lat, lat_blocks, dec_seq, MLA_HEADS,
                             (cache_ckv[:, i], cache_krope[:, i], state_ssm[:, i]))
    return (xp.reshape(batch, seq, d), xs.reshape(dec_batch, dec_seq, d),
            cache_bufs[0], cache_bufs[1],
            ssm_buf.reshape(batch, depth, 2, SSD_HEADS, SSD_HEAD_DIM, SSD_STATE))
```

```python
import functools
import math

import jax
import jax.numpy as jnp
import numpy as np
from jax import lax
from jax.experimental import pallas as pl
from jax.experimental.pallas import tpu as pltpu

F32 = jnp.float32
BF16 = jnp.bfloat16

D_MODEL = 1024
GRID_W = 64
SSD_WIDTH = 512
SSD_HEAD_DIM = 64
SSD_HEADS = 8
SSD_GROUPS = 2
SSD_STATE = 64
SSD_CONV = 5
SSD_CHUNK = 128
SSD_STEP_CHUNKS = 8
SSD_XBC = SSD_WIDTH + 2 * SSD_GROUPS * SSD_STATE
MLA_WIDTH = 512
MLA_V_DIM = 64
MLA_HEADS = 8
MLA_NOPE_DIM = 64
MLA_ROPE_DIM = 32
MLA_Q_RANK = 384
MLA_KV_RANK = 256
ROPE_THETA = 10000.0
N_EXPERTS = 8
N_MOD = 6
EPS = 1e-6

LANE = 128
SUBLANE = 8
HEAD_PAD = 128
ONES_ROWS = 16
ROPE_LANE0 = MLA_NOPE_DIM
C_Z = 0
C_XBC = C_Z + SSD_WIDTH
C_CQ = C_XBC + SSD_XBC
C_CKV = C_CQ + MLA_Q_RANK
C_TA = C_CKV + MLA_KV_RANK
IN_PAD = C_TA + LANE
ROT_GROUP = MLA_ROPE_DIM // 4

VMEM_LIMIT = 56 * 1024 * 1024

ROW_TILE = 1024
ATTN_Q_TILE = 256
ATTN_KEY_BLOCK = 512
ATTN_SEQS_PER_STEP = 4
FFN_ROW_TILE = 512
MOE_ROW_TILE = 1024
MOE_SUB_TILE = 512
MOE_CAPS = (128, 160, 192, 224, 256)

NT_DIMS = (((1,), (1,)), ((), ()))
TN_DIMS = (((0,), (0,)), ((), ()))


def _cparams(*sem):
    return pltpu.CompilerParams(dimension_semantics=sem, vmem_limit_bytes=VMEM_LIMIT)


def _silu(x):
    return x / (1.0 + jnp.exp(-x))


def _softplus(x):
    return jnp.maximum(x, 0.0) + jnp.log(1.0 + jnp.exp(-jnp.abs(x)))


def _rms_rows(x, g):
    ms = jnp.mean(x * x, axis=-1, keepdims=True)
    return x * lax.rsqrt(ms + EPS) * g


def _dot(a, b):
    return jnp.dot(a, b, preferred_element_type=F32)


def _dot_nt(a, b):
    return lax.dot_general(a, b, NT_DIMS, preferred_element_type=F32)


def _dot_tn(a, b):
    return lax.dot_general(a, b, TN_DIMS, preferred_element_type=F32)


def _split3(x):
    hi = x.astype(BF16)
    r1 = x - hi.astype(F32)
    mid = r1.astype(BF16)
    lo = (r1 - mid.astype(F32)).astype(BF16)
    return hi, mid, lo


def _mod_kernel(c_ref, w_ref, b_ref, o_ref):
    s = _silu(c_ref[...]).astype(BF16)
    o_ref[0] = _dot(s, w_ref[0].astype(BF16)) + b_ref[0]


def _modulation(cvec, w_mod, b_mod):
    depth, d, n = w_mod.shape
    tn = 1536
    return pl.pallas_call(
        _mod_kernel,
        grid=(depth, n // tn),
        in_specs=[
            pl.BlockSpec((SUBLANE, d), lambda l, j: (0, 0)),
            pl.BlockSpec((1, d, tn), lambda l, j: (l, 0, j)),
            pl.BlockSpec((1, 1, tn), lambda l, j: (l, 0, j)),
        ],
        out_specs=pl.BlockSpec((1, SUBLANE, tn), lambda l, j: (l, 0, j)),
        out_shape=jax.ShapeDtypeStruct((depth, SUBLANE, n), F32),
        compiler_params=_cparams("arbitrary", "arbitrary"),
        name="modulation",
    )(cvec, w_mod, b_mod.reshape(depth, 1, n))


def _inproj_kernel(*refs, emit_cache, n_alias):
    (x_ref, mod_ref, gpre_ref, win_ref, wdt_ref, dtb_row_ref, dtb_col_ref, qn_ref, wq_ref, kvn_ref, wuk_ref,
     wuv_ref, cosq_ref, sinq_ref, cosk_ref, sink_ref) = refs[:16]
    outs = refs[16 + n_alias:]
    z_ref, xbc_ref, dt_ref, dtt_ref, qt_ref, k_ref, vt_ref = outs[:7]
    mod = mod_ref[0]
    shift = mod[:, 0:D_MODEL]
    scale = mod[:, D_MODEL:2 * D_MODEL]
    h = _rms_rows(x_ref[...], gpre_ref[...]) * (1.0 + scale) + shift
    hb = h.astype(BF16)
    proj = _dot(hb, win_ref[...])
    z_ref[...] = proj[:, C_Z:C_XBC]
    xbc_ref[...] = proj[:, C_XBC:C_CQ]
    cqn = _rms_rows(proj[:, C_CQ:C_CKV], qn_ref[...]).astype(BF16)
    ckvn = _rms_rows(proj[:, C_CKV:C_TA], kvn_ref[...])
    if emit_cache:
        ckvn_ref, kr_ref = outs[7:]
        nb, _, sq, _ = ckvn_ref.shape
        ckvn_ref[:, 0] = ckvn.reshape(nb, sq, MLA_KV_RANK)
    ckvb = ckvn.astype(BF16)
    ta = proj[:, C_TA:IN_PAD]
    dt_ref[...] = _softplus(ta + dtb_row_ref[...])
    if emit_cache:
        kr_ref[:, 0] = ta[:, ROPE_LANE0:ROPE_LANE0 + MLA_ROPE_DIM].reshape(nb, sq, MLA_ROPE_DIM)
    lane = lax.broadcasted_iota(jnp.int32, ta.shape, 1)
    first = (lane // ROT_GROUP) % 2 == 0
    rot = jnp.where(first, -pltpu.roll(ta, LANE - ROT_GROUP, 1), pltpu.roll(ta, ROT_GROUP, 1))
    kr_rot = ta * cosk_ref[...] + rot * sink_ref[...]
    knp = _dot(ckvb, wuk_ref[...])
    for hd in range(MLA_HEADS):
        sl = slice(hd * HEAD_PAD, (hd + 1) * HEAD_PAD)
        k_ref[:, sl] = (knp[:, sl] + kr_rot).astype(BF16)
    vt_ref[...] = _dot_nt(wuv_ref[...], ckvb).astype(BF16)
    qt = _dot_nt(wq_ref[...], cqn)
    cosq = cosq_ref[...]
    sinq = sinq_ref[...]
    g, r0 = ROT_GROUP, ROPE_LANE0
    for hd in range(MLA_HEADS):
        blk = qt[hd * HEAD_PAD:(hd + 1) * HEAD_PAD, :]
        rot = jnp.concatenate([blk[:r0], -blk[r0 + g:r0 + 2 * g], blk[r0:r0 + g], -blk[r0 + 3 * g:r0 + 4 * g],
                               blk[r0 + 2 * g:r0 + 3 * g], blk[r0 + 4 * g:]], axis=0)
        qt_ref[hd * HEAD_PAD:(hd + 1) * HEAD_PAD, :] = (blk * cosq + rot * sinq).astype(BF16)
    dtt_ref[...] = _softplus(_dot_nt(wdt_ref[...], hb) + dtb_col_ref[...])


def _inproj(x, mod3, mod_row_fn, lw, tabs, tab_blocks, cache_out=None):
    t = x.shape[0]
    tm = ROW_TILE
    nb = t // tm
    cosq, sinq, cosk, sink = tabs
    ntab = tab_blocks
    const = lambda i: (0, 0)
    row = lambda i: (i, 0)
    col = lambda i: (0, i)
    in_specs = [
        pl.BlockSpec((tm, D_MODEL), row),
        pl.BlockSpec((1, 1, N_MOD * D_MODEL), lambda i: (mod_row_fn(ROW_TILE)(i), 0, 0)),
        pl.BlockSpec((1, D_MODEL), const),
        pl.BlockSpec((D_MODEL, IN_PAD), const),
        pl.BlockSpec((2 * SSD_HEADS, D_MODEL), const),
        pl.BlockSpec((1, LANE), const),
        pl.BlockSpec((2 * SSD_HEADS, 1), const),
        pl.BlockSpec((1, MLA_Q_RANK), const),
        pl.BlockSpec((MLA_HEADS * HEAD_PAD, MLA_Q_RANK), const),
        pl.BlockSpec((1, MLA_KV_RANK), const),
        pl.BlockSpec((MLA_KV_RANK, MLA_HEADS * HEAD_PAD), const),
        pl.BlockSpec((MLA_WIDTH, MLA_KV_RANK), const),
        pl.BlockSpec((HEAD_PAD, tm), lambda i: (0, i % ntab)),
        pl.BlockSpec((HEAD_PAD, tm), lambda i: (0, i % ntab)),
        pl.BlockSpec((tm, LANE), lambda i: (i % ntab, 0)),
        pl.BlockSpec((tm, LANE), lambda i: (i % ntab, 0)),
    ]
    out_specs = [
        pl.BlockSpec((tm, SSD_WIDTH), row),
        pl.BlockSpec((tm, SSD_XBC), row),
        pl.BlockSpec((tm, LANE), row),
        pl.BlockSpec((2 * SSD_HEADS, tm), col),
        pl.BlockSpec((MLA_HEADS * HEAD_PAD, tm), col),
        pl.BlockSpec((tm, MLA_HEADS * HEAD_PAD), row),
        pl.BlockSpec((MLA_WIDTH, tm), col),
    ]
    out_shape = [
        jax.ShapeDtypeStruct((t, SSD_WIDTH), F32),
        jax.ShapeDtypeStruct((t, SSD_XBC), F32),
        jax.ShapeDtypeStruct((t, LANE), F32),
        jax.ShapeDtypeStruct((2 * SSD_HEADS, t), F32),
        jax.ShapeDtypeStruct((MLA_HEADS * HEAD_PAD, t), BF16),
        jax.ShapeDtypeStruct((t, MLA_HEADS * HEAD_PAD), BF16),
        jax.ShapeDtypeStruct((MLA_WIDTH, t), BF16),
    ]
    args = [x, mod3, lw["g_pre1"], lw["w_in"], lw["w_dt"], lw["dtb_row"], lw["dtb_col"],
            lw["q_norm"], lw["w_q"], lw["kv_norm"], lw["w_uk"], lw["w_uv"], cosq, sinq, cosk, sink]
    aliases = {}
    n_alias = 0
    if cache_out is not None:
        layer, depth, seq, bufs = cache_out
        nb_seq = tm // seq
        for rank in (MLA_KV_RANK, MLA_ROPE_DIM):
            out_specs.append(pl.BlockSpec((nb_seq, 1, seq, rank), lambda i: (i, layer, 0, 0)))
            out_shape.append(jax.ShapeDtypeStruct((t // seq, depth, seq, rank), F32))
        if bufs is not None:
            n_alias = len(bufs)
            for j, buf in enumerate(bufs):
                aliases[len(args)] = len(out_shape) - n_alias + j
                in_specs.append(pl.BlockSpec(memory_space=pl.ANY))
                args.append(buf)
    return pl.pallas_call(
        functools.partial(_inproj_kernel, emit_cache=cache_out is not None, n_alias=n_alias),
        grid=(nb,),
        in_specs=in_specs,
        out_specs=out_specs,
        out_shape=out_shape,
        input_output_aliases=aliases,
        compiler_params=_cparams("arbitrary"),
        name="inproj",
    )(*args)


def _kvcache_kernel(ckv_ref, kr_ref, wuk_ref, wuv_ref, k_ref, vt_ref):
    ckvb = ckv_ref[...].astype(BF16)
    knp = _dot(ckvb, wuk_ref[...])
    kr = kr_ref[...]
    for hd in range(MLA_HEADS):
        sl = slice(hd * HEAD_PAD, (hd + 1) * HEAD_PAD)
        k_ref[:, sl] = (knp[:, sl] + kr).astype(BF16)
    vt_ref[...] = _dot_nt(wuv_ref[...], ckvb).astype(BF16)


def _kvcache(ckv, kr_tile, lw):
    n = ckv.shape[0]
    tm = 512
    return pl.pallas_call(
        _kvcache_kernel,
        grid=(n // tm,),
        in_specs=[
            pl.BlockSpec((tm, MLA_KV_RANK), lambda i: (i, 0)),
            pl.BlockSpec((tm, LANE), lambda i: (i, 0)),
            pl.BlockSpec((MLA_KV_RANK, MLA_HEADS * HEAD_PAD), lambda i: (0, 0)),
            pl.BlockSpec((MLA_WIDTH, MLA_KV_RANK), lambda i: (0, 0)),
        ],
        out_specs=[
            pl.BlockSpec((tm, MLA_HEADS * HEAD_PAD), lambda i: (i, 0)),
            pl.BlockSpec((MLA_WIDTH, tm), lambda i: (0, i)),
        ],
        out_shape=[
            jax.ShapeDtypeStruct((n, MLA_HEADS * HEAD_PAD), BF16),
            jax.ShapeDtypeStruct((MLA_WIDTH, n), BF16),
        ],
        compiler_params=_cparams("arbitrary"),
        name="kvcache",
    )(ckv, kr_tile, lw["w_uk"], lw["w_uv"])


def _head_expand_matrix():
    r = lax.broadcasted_iota(jnp.int32, (LANE, 2 * SSD_WIDTH), 0)
    c = lax.broadcasted_iota(jnp.int32, (LANE, 2 * SSD_WIDTH), 1)
    return jnp.where(c // SSD_HEAD_DIM == r, 1.0, 0.0).astype(BF16)


def _expand_heads(v, emat):
    hi = v.astype(BF16)
    mid = (v - hi.astype(F32)).astype(BF16)
    return _dot(hi, emat) + _dot(mid, emat)


def _prefix_rows(dta, tril):
    return sum(_dot(tril, p) for p in _split3(dta))


def _chunk_masks(q):
    r_i = lax.broadcasted_iota(jnp.int32, (q, q), 0)
    c_i = lax.broadcasted_iota(jnp.int32, (q, q), 1)
    return r_i >= c_i, r_i <= c_i


def _ssd_state_kernel(*refs, cps, nc, nsq, has_h0, has_sink):
    it = iter(refs)
    xbc_ref = next(it)
    prev_ref, next_ref = (next(it), next(it)) if nsq == 0 else (None, None)
    dt_ref, cw_ref, cb_ref, alog_row_ref = (next(it) for _ in range(4))
    h0_ref = next(it) if has_h0 else None
    if has_sink:
        next(it)
    xcb_ref, hsf_ref, sb_ref, dec_ref, hfin_ref, st_ref = (next(it) for _ in range(6))

    q = SSD_CHUNK
    rows = nc * q
    cw = cw_ref[...]

    def conv_silu(ext, n):
        acc = cb_ref[...] + ext[SUBLANE - 2:SUBLANE - 2 + n] * cw[0:1]
        for k in range(1, SSD_CONV):
            o = SUBLANE - 2 + k
            acc = acc + ext[o:o + n] * cw[k:k + 1]
        return _silu(acc).astype(BF16)

    if nsq == 0:
        pos = (pl.program_id(0) * nc) % cps
        seq_first = pos == 0
        seq_last = pos + nc == cps
        prev = jnp.where(seq_first, 0.0, prev_ref[...])
        nxt = jnp.where(seq_last, 0.0, next_ref[...])
        xcb_all = conv_silu(jnp.concatenate([prev, xbc_ref[...], nxt], axis=0), rows)
    else:
        ln = cps * q
        pad = jnp.zeros((SUBLANE, SSD_XBC), F32)
        xcb_all = jnp.concatenate(
            [conv_silu(jnp.concatenate([pad, xbc_ref[sq * ln:(sq + 1) * ln, :], pad], axis=0), ln)
             for sq in range(nsq)], axis=0)
    xcb_ref[...] = xcb_all

    lower, _ = _chunk_masks(q)
    tril = jnp.where(lower, 1.0, 0.0).astype(BF16)
    a_row = -jnp.exp(alog_row_ref[...])
    lane = lax.broadcasted_iota(jnp.int32, (q, LANE), 1)
    lane_t = lax.broadcasted_iota(jnp.int32, (2 * SUBLANE, LANE), 1)
    emat = _head_expand_matrix()
    gw = SSD_WIDTH // SSD_GROUPS

    if nsq == 0:
        @pl.when(seq_first)
        def _():
            if has_h0:
                st_ref[...] = h0_ref[0]
            else:
                st_ref[...] = jnp.zeros_like(st_ref)

    chunk_dec, chunk_states = [], []
    for c in range(nc):
        sl = slice(c * q, (c + 1) * q)
        xcb = xcb_all[sl]
        xs = xcb[:, :SSD_WIDTH].astype(F32)
        dt = dt_ref[sl, :]
        dta = dt * a_row
        la = _prefix_rows(dta, tril)
        tot = la[q - 1:q, :]
        w = jnp.exp(jnp.where(lane < SSD_HEADS, tot - la, la - dta)) * dt
        w = jnp.where(lane < 2 * SSD_HEADS, w, 0.0)
        wexp = _expand_heads(w, emat)
        etot = jnp.where(lane_t < 2 * SSD_HEADS, jnp.exp(jnp.broadcast_to(tot, (2 * SUBLANE, LANE))), 0.0)
        dec = _expand_heads(etot, emat)[:SUBLANE]
        dec_ref[c] = dec
        bmb = xcb[:, SSD_WIDTH:SSD_WIDTH + SSD_GROUPS * SSD_STATE]
        states = []
        for d in range(2):
            xw = (xs * wexp[:, d * SSD_WIDTH:(d + 1) * SSD_WIDTH]).astype(BF16)
            parts = [_dot_tn(bmb[:, grp * SSD_STATE:(grp + 1) * SSD_STATE], xw[:, grp * gw:(grp + 1) * gw])
                     for grp in range(SSD_GROUPS)]
            states.append(jnp.concatenate(parts, axis=1))
        sb_ref[c] = states[1]
        chunk_dec.append(dec[0:1, :SSD_WIDTH])
        chunk_states.append(states[0])

    if nsq == 0:
        hs = st_ref[...]
        for c in range(nc):
            hsf_ref[c] = hs.astype(BF16)
            hs = hs * chunk_dec[c] + chunk_states[c]
        st_ref[...] = hs

        @pl.when(seq_last)
        def _():
            hfin_ref[0, 0] = hs.T
    else:
        for sq in range(nsq):
            hs = h0_ref[sq] if has_h0 else jnp.zeros(st_ref.shape, F32)
            for c in range(sq * cps, (sq + 1) * cps):
                hsf_ref[c] = hs.astype(BF16)
                hs = hs * chunk_dec[c] + chunk_states[c]
            hfin_ref[sq, 0] = hs.T


def _ssd_out_kernel(*refs, cps, nc, nsq, has_h0, has_sink, nsteps):
    it = iter(refs)
    (xcb_ref, dt_ref, dtt_ref, z_ref, hsf_ref, sb_ref, dec_ref,
     alog_row_ref, alog_col_ref, dskip_ref, gn_ref) = (next(it) for _ in range(11))
    h0_ref = next(it) if has_h0 else None
    if has_sink:
        next(it)
    y_ref, hfin_ref, st_ref = next(it), next(it), next(it)

    q = SSD_CHUNK
    pos = ((nsteps - 1 - pl.program_id(0)) * nc) % cps
    seq_first = pos == 0
    seq_last = pos + nc == cps
    log2e = math.log2(math.e)

    lower, upper = _chunk_masks(q)
    tril = jnp.where(lower, 1.0, 0.0).astype(BF16)
    triu = jnp.where(upper, 1.0, 0.0).astype(BF16)
    a_row = -jnp.exp(alog_row_ref[...])
    a_col = -jnp.exp(alog_col_ref[...])
    lane = lax.broadcasted_iota(jnp.int32, (q, LANE), 1)
    rowi = lax.broadcasted_iota(jnp.int32, (2 * SUBLANE, q), 0)
    emat = _head_expand_matrix()
    rep = SSD_HEADS // SSD_GROUPS
    gw = SSD_WIDTH // SSD_GROUPS
    neg = jnp.float32(-jnp.inf)

    if nsq == 0:
        @pl.when(seq_last)
        def _():
            if has_h0:
                st_ref[...] = h0_ref[0]
            else:
                st_ref[...] = jnp.zeros_like(st_ref)

    def chunk_terms(c):
        sl = slice(c * q, (c + 1) * q)
        xcb = xcb_ref[sl, :]
        xsb = xcb[:, :SSD_WIDTH]
        bmb = xcb[:, SSD_WIDTH:SSD_WIDTH + SSD_GROUPS * SSD_STATE]
        cmb = xcb[:, SSD_WIDTH + SSD_GROUPS * SSD_STATE:]
        dt = dt_ref[sl, :]
        dtt = dtt_ref[:, sl]
        dta = dt * a_row
        dtat = dtt * a_col
        la = _prefix_rows(dta, tril)
        tot = la[q - 1:q, :]
        lcol = jnp.where(lane < SSD_HEADS, la, tot - la + dta)
        lat = sum(_dot(p, triu) for p in _split3(dtat))
        tott = lat[:, q - 1:q]
        lrow = jnp.where(rowi < SSD_HEADS, lat, tott - lat + dtat)
        lcol2 = lcol * log2e
        lrow2 = (lrow - jnp.log(dtt)) * log2e
        ecol = jnp.where(lane < 2 * SSD_HEADS, jnp.exp(lcol), 0.0)
        eexp = _expand_heads(ecol, emat)

        cbs = []
        for grp in range(SSD_GROUPS):
            cg = cmb[:, grp * SSD_STATE:(grp + 1) * SSD_STATE]
            bg = bmb[:, grp * SSD_STATE:(grp + 1) * SSD_STATE]
            cbs.append(_dot_nt(cg, bg))
        tiles = []
        for pair in range(SSD_HEADS // 2):
            xpair = xsb[:, pair * LANE:(pair + 1) * LANE]
            res = []
            for hd in (2 * pair, 2 * pair + 1):
                jf, jb = hd, SSD_HEADS + hd
                ef = jnp.exp2(jnp.where(lower, lcol2[:, jf:jf + 1] - lrow2[jf:jf + 1, :], neg))
                eb = jnp.exp2(jnp.where(upper, lcol2[:, jb:jb + 1] - lrow2[jb:jb + 1, :], neg))
                mm = (cbs[hd // rep] * (ef + eb)).astype(BF16)
                res.append(_dot(mm, xpair))
            tiles.append(jnp.where(lane < SSD_HEAD_DIM, res[0], res[1]))
        y = jnp.concatenate(tiles, axis=1)
        hsf = hsf_ref[c]
        parts = [_dot(cmb[:, grp * SSD_STATE:(grp + 1) * SSD_STATE], hsf[:, grp * gw:(grp + 1) * gw])
                 for grp in range(SSD_GROUPS)]
        y = y + jnp.concatenate(parts, axis=1) * eexp[:, :SSD_WIDTH]
        y = y + dskip_ref[...] * xsb.astype(F32)
        return y, cmb, eexp[:, SSD_WIDTH:], _silu(z_ref[sl, :])

    terms = [chunk_terms(c) for c in range(nc)]

    def finish_chunk(c, hb):
        y, cmb, eexp_b, gate = terms[c]
        hsb = hb.astype(BF16)
        parts = [_dot(cmb[:, grp * SSD_STATE:(grp + 1) * SSD_STATE], hsb[:, grp * gw:(grp + 1) * gw])
                 for grp in range(SSD_GROUPS)]
        y = (y + jnp.concatenate(parts, axis=1) * eexp_b) * gate
        y_ref[c * q:(c + 1) * q, :] = _rms_rows(y, gn_ref[...]).astype(BF16)
        return hb * dec_ref[c][0:1, SSD_WIDTH:] + sb_ref[c]

    if nsq == 0:
        hb = st_ref[...]
        for c in reversed(range(nc)):
            hb = finish_chunk(c, hb)
        st_ref[...] = hb

        @pl.when(seq_first)
        def _():
            hfin_ref[0, 0] = hb.T
    else:
        for sq in range(nsq):
            hb = h0_ref[sq] if has_h0 else jnp.zeros(st_ref.shape, F32)
            for c in reversed(range(sq * cps, (sq + 1) * cps)):
                hb = finish_chunk(c, hb)
            hfin_ref[sq, 0] = hb.T


def _ssd(xbc, dt, dtt, z, lw, cps, h0=None, sink=(0, 2, None)):
    slot0, nslots, sink_buf = sink
    t = xbc.shape[0]
    q = SSD_CHUNK
    nchunks = t // q
    nseq = nchunks // cps
    nc = SSD_STEP_CHUNKS
    if cps >= nc:
        assert cps % nc == 0
        nsq, spq, sps = 0, cps // nc, 1
    else:
        assert nc % cps == 0 and nseq % (nc // cps) == 0
        nsq, spq = nc // cps, 1
        sps = nsq
    rows = nc * q
    nsteps = nchunks // nc
    hb = rows // SUBLANE
    n8 = t // SUBLANE
    has_h0 = h0 is not None
    const = lambda i: (0, 0)
    st_block = (sps, SSD_STATE, SSD_WIDTH)
    ch_block = (nc, SSD_STATE, SSD_WIDTH)
    fin_block = (sps, 1, SSD_WIDTH, SSD_STATE)
    fin_shape = jax.ShapeDtypeStruct((nseq, nslots, SSD_WIDTH, SSD_STATE), F32)
    any_spec = pl.BlockSpec(memory_space=pl.ANY)
    dec_block = (nc, SUBLANE, 2 * SSD_WIDTH)

    in_specs = [pl.BlockSpec((rows, SSD_XBC), lambda i: (i, 0))]
    args = [xbc]
    if nsq == 0:
        in_specs += [
            pl.BlockSpec((SUBLANE, SSD_XBC), lambda i: (jnp.maximum(i * hb - 1, 0), 0)),
            pl.BlockSpec((SUBLANE, SSD_XBC), lambda i: (jnp.minimum((i + 1) * hb, n8 - 1), 0)),
        ]
        args += [xbc, xbc]
    in_specs += [
        pl.BlockSpec((rows, LANE), lambda i: (i, 0)),
        pl.BlockSpec((SSD_CONV, SSD_XBC), const),
        pl.BlockSpec((1, SSD_XBC), const),
        pl.BlockSpec((1, LANE), const),
    ]
    args += [dt, lw["conv_w"], lw["conv_b"], lw["alog_row"]]
    if has_h0:
        in_specs.append(pl.BlockSpec(st_block, lambda i: (i // spq, 0, 0)))
        args.append(h0[0])
    aliases = {}
    if sink_buf is not None:
        aliases[len(args)] = 4
        in_specs.append(any_spec)
        args.append(sink_buf)
    xcb, hsf, sb, dec, fin = pl.pallas_call(
        functools.partial(_ssd_state_kernel, cps=cps, nc=nc, nsq=nsq, has_h0=has_h0, has_sink=sink_buf is not None),
        grid=(nsteps,),
        in_specs=in_specs,
        out_specs=[
            pl.BlockSpec((rows, SSD_XBC), lambda i: (i, 0)),
            pl.BlockSpec(ch_block, lambda i: (i, 0, 0)),
            pl.BlockSpec(ch_block, lambda i: (i, 0, 0)),
            pl.BlockSpec(dec_block, lambda i: (i, 0, 0)),
            pl.BlockSpec(fin_block, lambda i: (i // spq, slot0, 0, 0)),
        ],
        out_shape=[
            jax.ShapeDtypeStruct((t, SSD_XBC), BF16),
            jax.ShapeDtypeStruct((nchunks, SSD_STATE, SSD_WIDTH), BF16),
            jax.ShapeDtypeStruct((nchunks, SSD_STATE, SSD_WIDTH), F32),
            jax.ShapeDtypeStruct((nchunks, SUBLANE, 2 * SSD_WIDTH), F32),
            fin_shape,
        ],
        input_output_aliases=aliases,
        scratch_shapes=[pltpu.VMEM((SSD_STATE, SSD_WIDTH), F32)],
        compiler_params=_cparams("arbitrary"),
        name="ssd_state",
    )(*args)

    gi = lambda i: nsteps - 1 - i
    in_specs = [
        pl.BlockSpec((rows, SSD_XBC), lambda i: (gi(i), 0)),
        pl.BlockSpec((rows, LANE), lambda i: (gi(i), 0)),
        pl.BlockSpec((2 * SSD_HEADS, rows), lambda i: (0, gi(i))),
        pl.BlockSpec((rows, SSD_WIDTH), lambda i: (gi(i), 0)),
        pl.BlockSpec(ch_block, lambda i: (gi(i), 0, 0)),
        pl.BlockSpec(ch_block, lambda i: (gi(i), 0, 0)),
        pl.BlockSpec(dec_block, lambda i: (gi(i), 0, 0)),
        pl.BlockSpec((1, LANE), const),
        pl.BlockSpec((2 * SSD_HEADS, 1), const),
        pl.BlockSpec((1, SSD_WIDTH), const),
        pl.BlockSpec((1, SSD_WIDTH), const),
    ]
    args = [xcb, dt, dtt, z, hsf, sb, dec, lw["alog_row"], lw["alog_col"], lw["dskip_row"], lw["ssd_norm"]]
    if has_h0:
        in_specs.append(pl.BlockSpec(st_block, lambda i: (gi(i) // spq, 0, 0)))
        args.append(h0[1])
    in_specs.append(any_spec)
    args.append(fin)
    y, fin = pl.pallas_call(
        functools.partial(_ssd_out_kernel, cps=cps, nc=nc, nsq=nsq, has_h0=has_h0, has_sink=True, nsteps=nsteps),
        grid=(nsteps,),
        in_specs=in_specs,
        out_specs=[
            pl.BlockSpec((rows, SSD_WIDTH), lambda i: (gi(i), 0)),
            pl.BlockSpec(fin_block, lambda i: (gi(i) // spq, slot0 + 1, 0, 0)),
        ],
        out_shape=[jax.ShapeDtypeStruct((t, SSD_WIDTH), BF16), fin_shape],
        input_output_aliases={len(args) - 1: 1},
        scratch_shapes=[pltpu.VMEM((SSD_STATE, SSD_WIDTH), F32)],
        compiler_params=_cparams("arbitrary"),
        name="ssd_out",
    )(*args)
    return y, fin


def _state_to_kernel_layout(h):
    n = h.shape[0]
    return h.transpose(0, 3, 1, 2).reshape(n, SSD_STATE, SSD_WIDTH)


def _attn_kernel(*refs, heads, has_cache, nseq_step):
    if has_cache:
        qt_ref, k_ref, vt_ref, kc_ref, vct_ref = refs[:5]
    else:
        qt_ref, k_ref, vt_ref = refs[:3]
    n_in = 5 if has_cache else 3
    o_ref = refs[n_in]
    scratch = refs[n_in + 1:]
    tq = qt_ref.shape[1] // nseq_step
    lk = k_ref.shape[0] // nseq_step
    kb = min(ATTN_KEY_BLOCK, lk)
    ones = jnp.ones((ONES_ROWS, kb), BF16)

    for sq in range(nseq_step):
        s_refs = scratch[2 * sq:2 * sq + 2]
        qcols = slice(sq * tq, (sq + 1) * tq)
        blocks = [(k_ref, vt_ref, sq * lk + i * kb) for i in range(lk // kb)]
        if has_cache:
            assert nseq_step == 1
            lc = kc_ref.shape[0]
            assert min(ATTN_KEY_BLOCK, lc) == kb
            blocks += [(kc_ref, vct_ref, i * kb) for i in range(lc // kb)]
        nblk = len(blocks)

        def score_block(hd, i, m, blocks=blocks, s_refs=s_refs, qcols=qcols):
            kr, _, off = blocks[i]
            q = qt_ref[hd * HEAD_PAD:(hd + 1) * HEAD_PAD, qcols]
            s = _dot(kr[off:off + kb, hd * HEAD_PAD:(hd + 1) * HEAD_PAD], q)
            s_refs[hd % 2][i * kb:(i + 1) * kb, :] = s
            bm = jnp.max(s, axis=0, keepdims=True)
            return bm if m is None else jnp.maximum(m, bm)

        def value_block(hd, i, m, acc, blocks=blocks, s_refs=s_refs):
            _, vr, off = blocks[i]
            p = jnp.exp2((s_refs[hd % 2][i * kb:(i + 1) * kb, :] - m).astype(BF16))
            v = vr[hd * MLA_V_DIM:(hd + 1) * MLA_V_DIM, off:off + kb]
            part = _dot(jnp.concatenate([v, ones], axis=0), p)
            return part if acc is None else acc + part

        m_cur = None
        for i in range(nblk):
            m_cur = score_block(0, i, m_cur)
        for hd in range(heads):
            m_next, acc = None, None
            for i in range(nblk):
                if nblk == 1 and hd + 1 < heads:
                    m_next = score_block(hd + 1, i, m_next)
                acc = value_block(hd, i, m_cur, acc)
                if nblk > 1 and hd + 1 < heads:
                    m_next = score_block(hd + 1, i, m_next)
            vs = slice(hd * MLA_V_DIM, (hd + 1) * MLA_V_DIM)
            o_ref[vs, qcols] = acc[:MLA_V_DIM] / acc[MLA_V_DIM:MLA_V_DIM + 1]
            m_cur = m_next


def _attention(qt, k, vt, seq_len, heads_per_step, cache=None):
    t = k.shape[0]
    nseq = t // seq_len
    tq = min(ATTN_Q_TILE, seq_len)
    nq = seq_len // tq
    g = heads_per_step
    nss = ATTN_SEQS_PER_STEP if (nq == 1 and cache is None and nseq % ATTN_SEQS_PER_STEP == 0) else 1
    in_specs = [
        pl.BlockSpec((g * HEAD_PAD, nss * tq), lambda s, h, j: (h, s * nq + j)),
        pl.BlockSpec((nss * seq_len, g * HEAD_PAD), lambda s, h, j: (s, h)),
        pl.BlockSpec((g * MLA_V_DIM, nss * seq_len), lambda s, h, j: (h, s)),
    ]
    args = [qt, k, vt]
    n_keys = seq_len
    if cache is not None:
        kc, vct = cache
        past = kc.shape[0] // nseq
        n_keys += past
        in_specs += [
            pl.BlockSpec((past, g * HEAD_PAD), lambda s, h, j: (s, h)),
            pl.BlockSpec((g * MLA_V_DIM, past), lambda s, h, j: (h, s)),
        ]
        args += [kc, vct]
    kern = functools.partial(_attn_kernel, heads=g, has_cache=cache is not None, nseq_step=nss)
    return pl.pallas_call(
        kern,
        grid=(nseq // nss, MLA_HEADS // g, nq),
        in_specs=in_specs,
        out_specs=pl.BlockSpec((g * MLA_V_DIM, nss * tq), lambda s, h, j: (h, s * nq + j)),
        out_shape=jax.ShapeDtypeStruct((MLA_WIDTH, t), F32),
        scratch_shapes=[pltpu.VMEM((n_keys, tq), F32) for _ in range(2 * nss)],
        compiler_params=_cparams("arbitrary", "arbitrary", "arbitrary"),
        name="attention",
    )(*args)


def _outproj_kernel(*refs, has_router):
    if has_router:
        (y_ref, ot_ref, x_ref, mod_ref, wa_ref, wb_ref, gm_ref, gpost_ref, gpre2_ref, rt_ref,
         x1_ref, h2_ref, comb_ref) = refs
    else:
        (y_ref, ot_ref, x_ref, mod_ref, wa_ref, wb_ref, gm_ref, gpost_ref, gpre2_ref,
         x1_ref, h2_ref) = refs
    mod = mod_ref[0]
    gate1 = mod[:, 2 * D_MODEL:3 * D_MODEL]
    shift2 = mod[:, 3 * D_MODEL:4 * D_MODEL]
    scale2 = mod[:, 4 * D_MODEL:5 * D_MODEL]
    ot = ot_ref[...]
    ms = jnp.mean(ot * ot, axis=0, keepdims=True)
    on = (ot * lax.rsqrt(ms + EPS) * gm_ref[...]).astype(BF16)
    y = _dot(y_ref[...], wa_ref[...]) + _dot_tn(on, wb_ref[...])
    x1 = x_ref[...] + gate1 * _rms_rows(y, gpost_ref[...])
    x1_ref[...] = x1
    h2 = _rms_rows(x1, gpre2_ref[...]) * (1.0 + scale2) + shift2
    h2_ref[...] = h2.astype(BF16)
    if has_router:
        hh, hm, _ = _split3(h2)
        rh, rm, _ = _split3(rt_ref[...])
        logits = _dot(hh, rh) + (_dot(hm, rh) + _dot(hh, rm))
        lane = lax.broadcasted_iota(jnp.int32, logits.shape, 1).astype(F32)
        neg = jnp.float32(-jnp.inf)
        lg = jnp.where(lane < N_EXPERTS, logits, neg)
        m1 = jnp.max(lg, axis=-1, keepdims=True)
        i1 = jnp.min(jnp.where(lg == m1, lane, float(LANE)), axis=-1, keepdims=True)
        lg2 = jnp.where(lane == i1, neg, lg)
        m2 = jnp.max(lg2, axis=-1, keepdims=True)
        i2 = jnp.min(jnp.where(lg2 == m2, lane, float(LANE)), axis=-1, keepdims=True)
        e2 = jnp.exp(m2 - m1)
        w1 = 1.0 / (1.0 + e2)
        w2 = e2 / (1.0 + e2)
        comb_ref[...] = jnp.where(lane == i1, w1, 0.0) + jnp.where(lane == i2, w2, 0.0)


def _outproj(yssd, ot, x, mod3, mod_row_fn, lw, router=None):
    t = x.shape[0]
    tm = ROW_TILE
    const = lambda i: (0, 0)
    row = lambda i: (i, 0)
    in_specs = [
        pl.BlockSpec((tm, SSD_WIDTH), row),
        pl.BlockSpec((MLA_WIDTH, tm), lambda i: (0, i)),
        pl.BlockSpec((tm, D_MODEL), row),
        pl.BlockSpec((1, 1, N_MOD * D_MODEL), lambda i: (mod_row_fn(ROW_TILE)(i), 0, 0)),
        pl.BlockSpec((SSD_WIDTH, D_MODEL), const),
        pl.BlockSpec((MLA_WIDTH, D_MODEL), const),
        pl.BlockSpec((MLA_WIDTH, 1), const),
        pl.BlockSpec((1, D_MODEL), const),
        pl.BlockSpec((1, D_MODEL), const),
    ]
    args = [yssd, ot, x, mod3, lw["w_out_a"], lw["w_out_b"], lw["mla_norm_col"],
            lw["g_post1"], lw["g_pre2"]]
    out_specs = [pl.BlockSpec((tm, D_MODEL), row), pl.BlockSpec((tm, D_MODEL), row)]
    out_shape = [jax.ShapeDtypeStruct((t, D_MODEL), F32), jax.ShapeDtypeStruct((t, D_MODEL), BF16)]
    if router is not None:
        in_specs.append(pl.BlockSpec((D_MODEL, LANE), const))
        args.append(router)
        out_specs.append(pl.BlockSpec((tm, LANE), row))
        out_shape.append(jax.ShapeDtypeStruct((t, LANE), F32))
    return pl.pallas_call(
        functools.partial(_outproj_kernel, has_router=router is not None),
        grid=(t // tm,),
        in_specs=in_specs,
        out_specs=out_specs,
        out_shape=out_shape,
        compiler_params=_cparams("arbitrary"),
        name="outproj",
    )(*args)


def _ffn_kernel(*refs, has_comb, nslab):
    if has_comb:
        h_ref, x_ref, comb_ref, mod_ref, wgu_ref, wd_ref, gpost_ref, o_ref, acc_ref = refs
    else:
        h_ref, x_ref, mod_ref, wgu_ref, wd_ref, gpost_ref, o_ref, acc_ref = refs
    e = pl.program_id(1)
    h = h_ref[...]
    f = wd_ref.shape[1]
    gu = _dot(h, wgu_ref[0])
    hid = _silu(gu[:, :f]) * gu[:, f:]
    if has_comb:
        comb = comb_ref[...]
        lane = lax.broadcasted_iota(jnp.int32, comb.shape, 1)
        wcol = jnp.sum(jnp.where(lane == e, comb, 0.0), axis=-1, keepdims=True)
        hid = hid * wcol
    part = _dot(hid.astype(BF16), wd_ref[0])

    @pl.when(e == 0)
    def _():
        acc_ref[...] = part

    @pl.when(e > 0)
    def _():
        acc_ref[...] += part

    @pl.when(e == nslab - 1)
    def _():
        gate2 = mod_ref[0][:, 5 * D_MODEL:6 * D_MODEL]
        o_ref[...] = x_ref[...] + gate2 * _rms_rows(acc_ref[...], gpost_ref[...])


def _ffn(h2, x1, mod3, mod_row_fn, wgu, wd, gpost, comb=None):
    t = x1.shape[0]
    tm = FFN_ROW_TILE
    nslab, f, _ = wd.shape
    row = lambda i, e: (i, 0)
    in_specs = [pl.BlockSpec((tm, D_MODEL), row), pl.BlockSpec((tm, D_MODEL), row)]
    args = [h2, x1]
    if comb is not None:
        in_specs.append(pl.BlockSpec((tm, LANE), row))
        args.append(comb)
    in_specs += [
        pl.BlockSpec((1, 1, N_MOD * D_MODEL), lambda i, e: (mod_row_fn(FFN_ROW_TILE)(i), 0, 0)),
        pl.BlockSpec((1, D_MODEL, 2 * f), lambda i, e: (e, 0, 0)),
        pl.BlockSpec((1, f, D_MODEL), lambda i, e: (e, 0, 0)),
        pl.BlockSpec((1, D_MODEL), lambda i, e: (0, 0)),
    ]
    args += [mod3, wgu, wd, gpost]
    return pl.pallas_call(
        functools.partial(_ffn_kernel, has_comb=comb is not None, nslab=nslab),
        grid=(t // tm, nslab),
        in_specs=in_specs,
        out_specs=pl.BlockSpec((tm, D_MODEL), row),
        out_shape=jax.ShapeDtypeStruct((t, D_MODEL), F32),
        scratch_shapes=[pltpu.VMEM((tm, D_MODEL), F32)],
        compiler_params=_cparams("arbitrary", "arbitrary"),
        name="ffn",
    )(*args)


def _moe_kernel(h_ref, x_ref, comb_ref, mod_ref, wg_ref, wu_ref, wd_ref, gpost_ref, o_ref,
                acc_ref, rank_ref, rank_t_ref, comb_t_ref, *, nexp, sub):
    e = pl.program_id(1)
    tm = h_ref.shape[0]
    caps = MOE_CAPS
    cmax = caps[-1]

    @pl.when(e == 0)
    def _():
        acc_ref[...] = jnp.zeros_like(acc_ref)
        r_i = lax.broadcasted_iota(jnp.int32, (sub, sub), 0)
        c_i = lax.broadcasted_iota(jnp.int32, (sub, sub), 1)
        strict = jnp.where(r_i > c_i, 1.0, 0.0).astype(BF16)
        for s in range(tm // sub):
            rows = slice(s * sub, (s + 1) * sub)
            comb = comb_ref[rows, :]
            rank = _dot(strict, jnp.where(comb > 0.0, 1.0, 0.0).astype(BF16))
            rank_ref[rows, :] = rank
            rank_t_ref[:, rows] = rank.T
            comb_t_ref[:, rows] = comb.T

    def expert_pass(rows, wcol, rcol, wrow, rrow, base, cap):
        capl = -(-cap // LANE) * LANE
        slot_l = lax.broadcasted_iota(jnp.int32, (sub, capl), 1).astype(F32)
        slot_s = lax.broadcasted_iota(jnp.int32, (cap, sub), 0).astype(F32)
        gather = jnp.where(((rrow - base) == slot_s) & (wrow > 0.0), 1.0, 0.0).astype(BF16)
        scatter = jnp.where(((rcol - base) == slot_l) & (wcol > 0.0) & (slot_l < float(cap)),
                            1.0, 0.0).astype(BF16)
        xg = _dot(gather, h_ref[rows, :]).astype(BF16)
        hid = _silu(_dot(xg, wg_ref[0])) * _dot(xg, wu_ref[0])
        y = _dot(hid.astype(BF16), wd_ref[0]).astype(BF16)
        if capl > cap:
            y = jnp.concatenate([y, jnp.zeros((capl - cap, y.shape[1]), BF16)], axis=0)
        acc_ref[rows, :] += wcol * _dot(scatter, y)

    lane = lax.broadcasted_iota(jnp.int32, (sub, LANE), 1)

    def sub_tile(s, carry):
        rows = pl.ds(pl.multiple_of(s * sub, sub), sub)
        pick = lane == e
        wcol = jnp.sum(jnp.where(pick, comb_ref[rows, :], 0.0), axis=-1, keepdims=True)
        rcol = jnp.sum(jnp.where(pick, rank_ref[rows, :], 0.0), axis=-1, keepdims=True)
        wrow = comb_t_ref[pl.ds(e, 1), rows]
        rrow = rank_t_ref[pl.ds(e, 1), rows]
        count = jnp.max(jnp.where(wrow > 0.0, rrow + 1.0, 0.0))
        npass = ((count + (cmax - 1.0)) * (1.0 / cmax)).astype(jnp.int32)
        nfull = jnp.maximum(npass - 1, 0)

        def full_pass(k, c):
            expert_pass(rows, wcol, rcol, wrow, rrow, (k * cmax).astype(F32), cmax)
            return c

        lax.fori_loop(0, nfull, full_pass, 0)
        base = (nfull * cmax).astype(F32)
        left = count - base
        lo = 0
        for cap in caps:
            @pl.when(jnp.logical_and(left > float(lo), left <= float(cap)))
            def _(cap=cap):
                expert_pass(rows, wcol, rcol, wrow, rrow, base, cap)
            lo = cap
        return carry

    lax.fori_loop(0, tm // sub, sub_tile, 0)

    @pl.when(e == nexp - 1)
    def _():
        gate2 = mod_ref[0][:, 5 * D_MODEL:6 * D_MODEL]
        o_ref[...] = x_ref[...] + gate2 * _rms_rows(acc_ref[...], gpost_ref[...])


def _moe(h2, x1, comb, mod3, mod_row_fn, wg, wu, wd, gpost):
    t = x1.shape[0]
    tm = min(MOE_ROW_TILE, t)
    sub = min(MOE_SUB_TILE, tm)
    nexp, f, _ = wd.shape
    row = lambda i, e: (i, 0)
    return pl.pallas_call(
        functools.partial(_moe_kernel, nexp=nexp, sub=sub),
        grid=(t // tm, nexp),
        in_specs=[
            pl.BlockSpec((tm, D_MODEL), row),
            pl.BlockSpec((tm, D_MODEL), row),
            pl.BlockSpec((tm, LANE), row),
            pl.BlockSpec((1, 1, N_MOD * D_MODEL), lambda i, e: (mod_row_fn(tm)(i), 0, 0)),
            pl.BlockSpec((1, D_MODEL, f), lambda i, e: (e, 0, 0)),
            pl.BlockSpec((1, D_MODEL, f), lambda i, e: (e, 0, 0)),
            pl.BlockSpec((1, f, D_MODEL), lambda i, e: (e, 0, 0)),
            pl.BlockSpec((1, D_MODEL), lambda i, e: (0, 0)),
        ],
        out_specs=pl.BlockSpec((tm, D_MODEL), row),
        out_shape=jax.ShapeDtypeStruct((t, D_MODEL), F32),
        scratch_shapes=[pltpu.VMEM((tm, D_MODEL), F32), pltpu.VMEM((tm, LANE), F32),
                        pltpu.VMEM((LANE, tm), F32), pltpu.VMEM((LANE, tm), F32)],
        compiler_params=_cparams("arbitrary", "arbitrary"),
        name="moe",
    )(h2, x1, comb, mod3, wg, wu, wd, gpost)


def _prep_layer(i, p):
    w_in = p["w_in"][i]
    s1 = SSD_WIDTH
    s2 = s1 + SSD_XBC
    s3 = s2 + 2 * SSD_HEADS
    s4 = s3 + MLA_Q_RANK
    s5 = s4 + MLA_KV_RANK
    w_z, w_xbc, w_dt, w_cq, w_ckv, w_kr = (w_in[:, :s1], w_in[:, s1:s2], w_in[:, s2:s3],
                                             w_in[:, s3:s4], w_in[:, s4:s5], w_in[:, s5:])
    zc = lambda n: jnp.zeros((D_MODEL, n), F32)
    tile_a = jnp.concatenate([w_dt, zc(ROPE_LANE0 - 2 * SSD_HEADS), w_kr,
                              zc(LANE - ROPE_LANE0 - MLA_ROPE_DIM)], axis=1)
    w_in_pad = jnp.concatenate([w_z, w_xbc, w_cq, w_ckv, tile_a], axis=1).astype(BF16)

    w_uq = p["w_uq"][i].reshape(MLA_Q_RANK, MLA_HEADS, MLA_NOPE_DIM + MLA_ROPE_DIM)
    q_nope, q_rope = w_uq[..., :MLA_NOPE_DIM], w_uq[..., MLA_NOPE_DIM:]
    zq = lambda n: jnp.zeros((MLA_Q_RANK, MLA_HEADS, n), F32)
    pad = HEAD_PAD - MLA_NOPE_DIM - MLA_ROPE_DIM
    w_q = jnp.concatenate([q_nope, q_rope, zq(pad)], axis=-1).reshape(MLA_Q_RANK, -1)

    w_ukv = p["w_ukv"][i].reshape(MLA_KV_RANK, MLA_HEADS, MLA_NOPE_DIM + MLA_V_DIM)
    k_nope, v_w = w_ukv[..., :MLA_NOPE_DIM], w_ukv[..., MLA_NOPE_DIM:]
    w_uk = jnp.concatenate([k_nope, jnp.zeros((MLA_KV_RANK, MLA_HEADS, HEAD_PAD - MLA_NOPE_DIM), F32)],
                           axis=-1).reshape(MLA_KV_RANK, -1)
    w_uv = v_w.reshape(MLA_KV_RANK, MLA_WIDTH)

    dtb = p["dt_bias"][i].reshape(2 * SSD_HEADS)
    alog = p["a_log"][i].reshape(2 * SSD_HEADS)
    padl = lambda v: jnp.pad(v, (0, LANE - v.shape[0])).reshape(1, LANE)
    w_out = p["w_out"][i]
    return {
        "g_pre1": p["norm_pre_mix"][i].reshape(1, D_MODEL),
        "g_post1": p["norm_post_mix"][i].reshape(1, D_MODEL),
        "g_pre2": p["norm_pre_ffn"][i].reshape(1, D_MODEL),
        "g_post2": p["norm_post_ffn"][i].reshape(1, D_MODEL),
        "w_in": w_in_pad,
        "w_dt": w_dt.T.astype(BF16),
        "dtb_row": padl(dtb),
        "dtb_col": dtb.reshape(-1, 1),
        "alog_row": padl(alog),
        "alog_col": alog.reshape(-1, 1),
        "q_norm": p["q_norm"][i].reshape(1, -1),
        "w_q": w_q.T.astype(BF16),
        "kv_norm": p["kv_norm"][i].reshape(1, -1),
        "w_uk": w_uk.astype(BF16),
        "w_uv": w_uv.T.astype(BF16),
        "conv_w": p["conv_w"][i],
        "conv_b": p["conv_b"][i].reshape(1, -1),
        "dskip_row": jnp.repeat(p["d_skip"][i], SSD_HEAD_DIM).reshape(1, -1),
        "ssd_norm": p["ssd_norm"][i].reshape(1, -1),
        "mla_norm_col": p["mla_norm"][i].reshape(-1, 1),
        "w_out_a": w_out[:SSD_WIDTH].astype(BF16),
        "w_out_b": w_out[SSD_WIDTH:].astype(BF16),
    }


def _rope_tables(n_tokens):
    rows = n_tokens // GRID_W
    row = np.repeat(np.arange(rows, dtype=np.float32), GRID_W)
    col = np.tile(np.arange(GRID_W, dtype=np.float32), rows)
    half = MLA_ROPE_DIM // 2
    inv = (np.float32(ROPE_THETA) ** (-np.arange(0, half, 2, dtype=np.float32) / np.float32(half))).astype(np.float32)
    ar = row[:, None] * inv[None, :]
    ac = col[:, None] * inv[None, :]
    ang = np.concatenate([ar, ar, ac, ac], axis=-1).astype(np.float32)
    return jnp.asarray(np.cos(ang), F32), jnp.asarray(np.sin(ang), F32)


def _attn_tables(cos, sin, n):
    scale = (MLA_NOPE_DIM + MLA_ROPE_DIM) ** -0.5 * math.log2(math.e)
    pad = HEAD_PAD - ROPE_LANE0 - MLA_ROPE_DIM
    cosk = jnp.concatenate([jnp.zeros((n, ROPE_LANE0), F32), cos, jnp.zeros((n, pad), F32)], axis=1)
    sink = jnp.concatenate([jnp.zeros((n, ROPE_LANE0), F32), sin, jnp.zeros((n, pad), F32)], axis=1)
    cosq = jnp.concatenate([jnp.ones((n, ROPE_LANE0), F32), cos, jnp.zeros((n, pad), F32)], axis=1)
    return (cosq * scale).T, (sink * scale).T, cosk, sink


def kernel(x_prompt, x_sample, cache_ckv, cache_krope, state_ssm, c, c_ctx, w_mod, b_mod, norm_pre_mix, norm_post_mix, norm_pre_ffn, norm_post_ffn, w_in, conv_w, conv_b, dt_bias, a_log, d_skip, ssd_norm, q_norm, w_uq, kv_norm, w_ukv, mla_norm, w_out, ffn_w_gate, ffn_w_up, ffn_w_down, moe_router, moe_w_gate, moe_w_up, moe_w_down):
    params = dict(w_in=w_in, conv_w=conv_w, conv_b=conv_b, dt_bias=dt_bias, a_log=a_log, d_skip=d_skip,
                  ssd_norm=ssd_norm, q_norm=q_norm, w_uq=w_uq, kv_norm=kv_norm, w_ukv=w_ukv,
                  mla_norm=mla_norm, w_out=w_out, norm_pre_mix=norm_pre_mix, norm_post_mix=norm_post_mix,
                  norm_pre_ffn=norm_pre_ffn, norm_post_ffn=norm_post_ffn)
    batch, seq, d = x_prompt.shape
    dec_batch, dec_seq, _ = x_sample.shape
    depth = w_in.shape[0]
    past = cache_ckv.shape[2]
    tm = ROW_TILE

    cvec = jnp.concatenate([c_ctx[None, :], c, jnp.zeros((SUBLANE - 1 - dec_batch, d), F32)], axis=0)
    mod = _modulation(cvec, w_mod, b_mod)

    ones = jnp.ones((tm, MLA_ROPE_DIM), F32)
    tabs_ctx = _attn_tables(ones, jnp.zeros_like(ones), tm)
    cos, sin = _rope_tables(dec_seq)
    tabs_lat = _attn_tables(cos, sin, dec_seq)
    lat_blocks = dec_seq // tm

    xp = x_prompt.reshape(batch * seq, d)
    xs = x_sample.reshape(dec_batch * dec_seq, d)
    cache_bufs, ssm_buf = None, None
    for i in range(depth):
        lw = _prep_layer(i, params)
        mod3 = mod[i].reshape(SUBLANE, 1, N_MOD * d)
        j = i // 2
        if i % 2 == 0:
            f = ffn_w_gate.shape[2] // 2
            wgu = jnp.stack([jnp.concatenate([ffn_w_gate[j][:, s * f:(s + 1) * f], ffn_w_up[j][:, s * f:(s + 1) * f]],
                                             axis=1) for s in range(2)], axis=0).astype(BF16)
            wd = ffn_w_down[j].reshape(2, f, d).astype(BF16)
            router = None
        else:
            wgu = (moe_w_gate[j].astype(BF16), moe_w_up[j].astype(BF16))
            wd = moe_w_down[j].astype(BF16)
            router = jnp.pad(moe_router[j], ((0, 0), (0, LANE - N_EXPERTS)))

        def run(x, row_fn, tabs, tab_blocks, seq_len, heads_per_step, ctx):
            cache_out = (i, depth, seq_len, cache_bufs) if ctx is None else None
            z, xbc, dt, dtt, qt, k, vt, *caches = _inproj(x, mod3, row_fn, lw, tabs, tab_blocks, cache_out)
            cps = seq_len // SSD_CHUNK
            h0 = None
            if ctx is not None:
                h0 = (_state_to_kernel_layout(ctx[2][:, 0]), _state_to_kernel_layout(ctx[2][:, 1]))
            sink = (2 * i, 2 * depth, ssm_buf) if ctx is None else (0, 2, None)
            yssd, fin = _ssd(xbc, dt, dtt, z, lw, cps, h0=h0, sink=sink)
            cache = None
            if ctx is not None:
                kr_tile = jnp.pad(ctx[1].reshape(-1, MLA_ROPE_DIM),
                                  ((0, 0), (ROPE_LANE0, HEAD_PAD - ROPE_LANE0 - MLA_ROPE_DIM)))
                cache = _kvcache(ctx[0].reshape(-1, MLA_KV_RANK), kr_tile, lw)
            ot = _attention(qt, k, vt, seq_len, heads_per_step, cache=cache)
            outs = _outproj(yssd, ot, x, mod3, row_fn, lw, router=router)
            x1, h2 = outs[0], outs[1]
            comb = outs[2] if router is not None else None
            if comb is None:
                x2 = _ffn(h2, x1, mod3, row_fn, wgu, wd, lw["g_post2"])
            else:
                x2 = _moe(h2, x1, comb, mod3, row_fn, *wgu, wd, lw["g_post2"])
            return x2, caches, fin

        xp, cache_bufs, ssm_buf = run(xp, lambda tile: (lambda b: 0), tabs_ctx, 1, seq, MLA_HEADS, None)
        xs, _, _ = run(xs, lambda tile: (lambda b: 1 + (b * tile) // dec_seq), tabs_lat, lat_blocks, dec_seq, MLA_HEADS,
                             (cache_ckv[:, i], cache_krope[:, i], state_ssm[:, i]))
    return (xp.reshape(batch, seq, d), xs.reshape(dec_batch, dec_seq, d),
            cache_bufs[0], cache_bufs[1],
            ssm_buf.reshape(batch, depth, 2, SSD_HEADS, SSD_HEAD_DIM, SSD_STATE))
```

```python
import functools
import math

import jax
import jax.numpy as jnp
import numpy as np
from jax import lax
from jax.experimental import pallas as pl
from jax.experimental.pallas import tpu as pltpu

F32 = jnp.float32
BF16 = jnp.bfloat16

D_MODEL = 1024
GRID_W = 64
SSD_WIDTH = 512
SSD_HEAD_DIM = 64
SSD_HEADS = 8
SSD_GROUPS = 2
SSD_STATE = 64
SSD_CONV = 5
SSD_CHUNK = 128
SSD_STEP_CHUNKS = 8
SSD_XBC = SSD_WIDTH + 2 * SSD_GROUPS * SSD_STATE
MLA_WIDTH = 512
MLA_V_DIM = 64
MLA_HEADS = 8
MLA_NOPE_DIM = 64
MLA_ROPE_DIM = 32
MLA_Q_RANK = 384
MLA_KV_RANK = 256
ROPE_THETA = 10000.0
N_EXPERTS = 8
N_MOD = 6
EPS = 1e-6

LANE = 128
SUBLANE = 8
HEAD_PAD = 128
ONES_ROWS = 16
ROPE_LANE0 = MLA_NOPE_DIM
C_Z = 0
C_XBC = C_Z + SSD_WIDTH
C_CQ = C_XBC + SSD_XBC
C_CKV = C_CQ + MLA_Q_RANK
C_TA = C_CKV + MLA_KV_RANK
IN_PAD = C_TA + LANE
ROT_GROUP = MLA_ROPE_DIM // 4

VMEM_LIMIT = 56 * 1024 * 1024

ROW_TILE = 1024
ATTN_Q_TILE = 256
ATTN_KEY_BLOCK = 512
ATTN_SEQS_PER_STEP = 4
FFN_ROW_TILE = 512
MOE_ROW_TILE = 1024
MOE_SUB_TILE = 512
MOE_CAPS = (128, 160, 192, 224, 256)
MOD_COL_TILE = 1536
KVCACHE_ROW_TILE = 512

NT_DIMS = (((1,), (1,)), ((), ()))
TN_DIMS = (((0,), (0,)), ((), ()))


def _cparams(*sem):
    return pltpu.CompilerParams(dimension_semantics=sem, vmem_limit_bytes=VMEM_LIMIT)


def _silu(x):
    return x / (1.0 + jnp.exp(-x))


def _softplus(x):
    return jnp.maximum(x, 0.0) + jnp.log(1.0 + jnp.exp(-jnp.abs(x)))


def _rms_rows(x, g):
    ms = jnp.mean(x * x, axis=-1, keepdims=True)
    return x * lax.rsqrt(ms + EPS) * g


def _dot(a, b):
    return jnp.dot(a, b, preferred_element_type=F32)


def _dot_nt(a, b):
    return lax.dot_general(a, b, NT_DIMS, preferred_element_type=F32)


def _dot_tn(a, b):
    return lax.dot_general(a, b, TN_DIMS, preferred_element_type=F32)


def _split3(x):
    hi = x.astype(BF16)
    r1 = x - hi.astype(F32)
    mid = r1.astype(BF16)
    lo = (r1 - mid.astype(F32)).astype(BF16)
    return hi, mid, lo


def _mod_kernel(c_ref, w_ref, b_ref, o_ref):
    s = _silu(c_ref[...]).astype(BF16)
    o_ref[0] = _dot(s, w_ref[0].astype(BF16)) + b_ref[0]


def _modulation(cvec, w_mod, b_mod):
    depth, d, n = w_mod.shape
    tn = MOD_COL_TILE
    return pl.pallas_call(
        _mod_kernel,
        grid=(depth, n // tn),
        in_specs=[
            pl.BlockSpec((SUBLANE, d), lambda l, j: (0, 0)),
            pl.BlockSpec((1, d, tn), lambda l, j: (l, 0, j)),
            pl.BlockSpec((1, 1, tn), lambda l, j: (l, 0, j)),
        ],
        out_specs=pl.BlockSpec((1, SUBLANE, tn), lambda l, j: (l, 0, j)),
        out_shape=jax.ShapeDtypeStruct((depth, SUBLANE, n), F32),
        compiler_params=_cparams("arbitrary", "arbitrary"),
        name="modulation",
    )(cvec, w_mod, b_mod.reshape(depth, 1, n))


def _inproj_kernel(*refs, emit_cache, n_alias):
    (x_ref, mod_ref, gpre_ref, win_ref, wdt_ref, dtb_row_ref, dtb_col_ref, qn_ref, wq_ref, kvn_ref, wuk_ref,
     wuv_ref, cosq_ref, sinq_ref, cosk_ref, sink_ref) = refs[:16]
    outs = refs[16 + n_alias:]
    z_ref, xbc_ref, dt_ref, dtt_ref, qt_ref, k_ref, vt_ref = outs[:7]
    mod = mod_ref[0]
    shift = mod[:, 0:D_MODEL]
    scale = mod[:, D_MODEL:2 * D_MODEL]
    h = _rms_rows(x_ref[...], gpre_ref[...]) * (1.0 + scale) + shift
    hb = h.astype(BF16)
    proj = _dot(hb, win_ref[...])
    z_ref[...] = proj[:, C_Z:C_XBC]
    xbc_ref[...] = proj[:, C_XBC:C_CQ]
    cqn = _rms_rows(proj[:, C_CQ:C_CKV], qn_ref[...]).astype(BF16)
    ckvn = _rms_rows(proj[:, C_CKV:C_TA], kvn_ref[...])
    if emit_cache:
        ckvn_ref, kr_ref = outs[7:]
        nb, _, sq, _ = ckvn_ref.shape
        ckvn_ref[:, 0] = ckvn.reshape(nb, sq, MLA_KV_RANK)
    ckvb = ckvn.astype(BF16)
    ta = proj[:, C_TA:IN_PAD]
    dt_ref[...] = _softplus(ta + dtb_row_ref[...])
    if emit_cache:
        kr_ref[:, 0] = ta[:, ROPE_LANE0:ROPE_LANE0 + MLA_ROPE_DIM].reshape(nb, sq, MLA_ROPE_DIM)
    lane = lax.broadcasted_iota(jnp.int32, ta.shape, 1)
    first = (lane // ROT_GROUP) % 2 == 0
    rot = jnp.where(first, -pltpu.roll(ta, LANE - ROT_GROUP, 1), pltpu.roll(ta, ROT_GROUP, 1))
    kr_rot = ta * cosk_ref[...] + rot * sink_ref[...]
    knp = _dot(ckvb, wuk_ref[...])
    for hd in range(MLA_HEADS):
        sl = slice(hd * HEAD_PAD, (hd + 1) * HEAD_PAD)
        k_ref[:, sl] = (knp[:, sl] + kr_rot).astype(BF16)
    vt_ref[...] = _dot_nt(wuv_ref[...], ckvb).astype(BF16)
    qt = _dot_nt(wq_ref[...], cqn)
    cosq = cosq_ref[...]
    sinq = sinq_ref[...]
    g, r0 = ROT_GROUP, ROPE_LANE0
    for hd in range(MLA_HEADS):
        blk = qt[hd * HEAD_PAD:(hd + 1) * HEAD_PAD, :]
        rot = jnp.concatenate([blk[:r0], -blk[r0 + g:r0 + 2 * g], blk[r0:r0 + g], -blk[r0 + 3 * g:r0 + 4 * g],
                               blk[r0 + 2 * g:r0 + 3 * g], blk[r0 + 4 * g:]], axis=0)
        qt_ref[hd * HEAD_PAD:(hd + 1) * HEAD_PAD, :] = (blk * cosq + rot * sinq).astype(BF16)
    dtt_ref[...] = _softplus(_dot_nt(wdt_ref[...], hb) + dtb_col_ref[...])


def _inproj(x, mod3, mod_row_fn, lw, tabs, tab_blocks, cache_out=None):
    t = x.shape[0]
    tm = ROW_TILE
    nb = t // tm
    cosq, sinq, cosk, sink = tabs
    ntab = tab_blocks
    const = lambda i: (0, 0)
    row = lambda i: (i, 0)
    col = lambda i: (0, i)
    in_specs = [
        pl.BlockSpec((tm, D_MODEL), row),
        pl.BlockSpec((1, 1, N_MOD * D_MODEL), lambda i: (mod_row_fn(ROW_TILE)(i), 0, 0)),
        pl.BlockSpec((1, D_MODEL), const),
        pl.BlockSpec((D_MODEL, IN_PAD), const),
        pl.BlockSpec((2 * SSD_HEADS, D_MODEL), const),
        pl.BlockSpec((1, LANE), const),
        pl.BlockSpec((2 * SSD_HEADS, 1), const),
        pl.BlockSpec((1, MLA_Q_RANK), const),
        pl.BlockSpec((MLA_HEADS * HEAD_PAD, MLA_Q_RANK), const),
        pl.BlockSpec((1, MLA_KV_RANK), const),
        pl.BlockSpec((MLA_KV_RANK, MLA_HEADS * HEAD_PAD), const),
        pl.BlockSpec((MLA_WIDTH, MLA_KV_RANK), const),
        pl.BlockSpec((HEAD_PAD, tm), lambda i: (0, i % ntab)),
        pl.BlockSpec((HEAD_PAD, tm), lambda i: (0, i % ntab)),
        pl.BlockSpec((tm, LANE), lambda i: (i % ntab, 0)),
        pl.BlockSpec((tm, LANE), lambda i: (i % ntab, 0)),
    ]
    out_specs = [
        pl.BlockSpec((tm, SSD_WIDTH), row),
        pl.BlockSpec((tm, SSD_XBC), row),
        pl.BlockSpec((tm, LANE), row),
        pl.BlockSpec((2 * SSD_HEADS, tm), col),
        pl.BlockSpec((MLA_HEADS * HEAD_PAD, tm), col),
        pl.BlockSpec((tm, MLA_HEADS * HEAD_PAD), row),
        pl.BlockSpec((MLA_WIDTH, tm), col),
    ]
    out_shape = [
        jax.ShapeDtypeStruct((t, SSD_WIDTH), F32),
        jax.ShapeDtypeStruct((t, SSD_XBC), F32),
        jax.ShapeDtypeStruct((t, LANE), F32),
        jax.ShapeDtypeStruct((2 * SSD_HEADS, t), F32),
        jax.ShapeDtypeStruct((MLA_HEADS * HEAD_PAD, t), BF16),
        jax.ShapeDtypeStruct((t, MLA_HEADS * HEAD_PAD), BF16),
        jax.ShapeDtypeStruct((MLA_WIDTH, t), BF16),
    ]
    args = [x, mod3, lw["g_pre1"], lw["w_in"], lw["w_dt"], lw["dtb_row"], lw["dtb_col"],
            lw["q_norm"], lw["w_q"], lw["kv_norm"], lw["w_uk"], lw["w_uv"], cosq, sinq, cosk, sink]
    aliases = {}
    n_alias = 0
    if cache_out is not None:
        layer, depth, seq, bufs = cache_out
        nb_seq = tm // seq
        for rank in (MLA_KV_RANK, MLA_ROPE_DIM):
            out_specs.append(pl.BlockSpec((nb_seq, 1, seq, rank), lambda i: (i, layer, 0, 0)))
            out_shape.append(jax.ShapeDtypeStruct((t // seq, depth, seq, rank), F32))
        if bufs is not None:
            n_alias = len(bufs)
            for j, buf in enumerate(bufs):
                aliases[len(args)] = len(out_shape) - n_alias + j
                in_specs.append(pl.BlockSpec(memory_space=pl.ANY))
                args.append(buf)
    return pl.pallas_call(
        functools.partial(_inproj_kernel, emit_cache=cache_out is not None, n_alias=n_alias),
        grid=(nb,),
        in_specs=in_specs,
        out_specs=out_specs,
        out_shape=out_shape,
        input_output_aliases=aliases,
        compiler_params=_cparams("arbitrary"),
        name="inproj",
    )(*args)


def _kvcache_kernel(ckv_ref, kr_ref, wuk_ref, wuv_ref, k_ref, vt_ref):
    ckvb = ckv_ref[...].astype(BF16)
    knp = _dot(ckvb, wuk_ref[...])
    kr = kr_ref[...]
    for hd in range(MLA_HEADS):
        sl = slice(hd * HEAD_PAD, (hd + 1) * HEAD_PAD)
        k_ref[:, sl] = (knp[:, sl] + kr).astype(BF16)
    vt_ref[...] = _dot_nt(wuv_ref[...], ckvb).astype(BF16)


def _kvcache(ckv, kr_tile, lw):
    n = ckv.shape[0]
    tm = KVCACHE_ROW_TILE
    return pl.pallas_call(
        _kvcache_kernel,
        grid=(n // tm,),
        in_specs=[
            pl.BlockSpec((tm, MLA_KV_RANK), lambda i: (i, 0)),
            pl.BlockSpec((tm, LANE), lambda i: (i, 0)),
            pl.BlockSpec((MLA_KV_RANK, MLA_HEADS * HEAD_PAD), lambda i: (0, 0)),
            pl.BlockSpec((MLA_WIDTH, MLA_KV_RANK), lambda i: (0, 0)),
        ],
        out_specs=[
            pl.BlockSpec((tm, MLA_HEADS * HEAD_PAD), lambda i: (i, 0)),
            pl.BlockSpec((MLA_WIDTH, tm), lambda i: (0, i)),
        ],
        out_shape=[
            jax.ShapeDtypeStruct((n, MLA_HEADS * HEAD_PAD), BF16),
            jax.ShapeDtypeStruct((MLA_WIDTH, n), BF16),
        ],
        compiler_params=_cparams("arbitrary"),
        name="kvcache",
    )(ckv, kr_tile, lw["w_uk"], lw["w_uv"])


def _head_expand_matrix():
    r = lax.broadcasted_iota(jnp.int32, (LANE, 2 * SSD_WIDTH), 0)
    c = lax.broadcasted_iota(jnp.int32, (LANE, 2 * SSD_WIDTH), 1)
    return jnp.where(c // SSD_HEAD_DIM == r, 1.0, 0.0).astype(BF16)


def _expand_heads(v, emat):
    hi = v.astype(BF16)
    mid = (v - hi.astype(F32)).astype(BF16)
    return _dot(hi, emat) + _dot(mid, emat)


def _prefix_rows(dta, tril):
    return sum(_dot(tril, p) for p in _split3(dta))


def _chunk_masks(q):
    r_i = lax.broadcasted_iota(jnp.int32, (q, q), 0)
    c_i = lax.broadcasted_iota(jnp.int32, (q, q), 1)
    return r_i >= c_i, r_i <= c_i


def _ssd_state_kernel(*refs, cps, nc, nsq, has_h0, has_sink):
    it = iter(refs)
    xbc_ref = next(it)
    prev_ref, next_ref = (next(it), next(it)) if nsq == 0 else (None, None)
    dt_ref, cw_ref, cb_ref, alog_row_ref = (next(it) for _ in range(4))
    h0_ref = next(it) if has_h0 else None
    if has_sink:
        next(it)
    xcb_ref, hsf_ref, sb_ref, dec_ref, hfin_ref, st_ref = (next(it) for _ in range(6))

    q = SSD_CHUNK
    rows = nc * q
    cw = cw_ref[...]

    def conv_silu(ext, n):
        acc = cb_ref[...] + ext[SUBLANE - 2:SUBLANE - 2 + n] * cw[0:1]
        for k in range(1, SSD_CONV):
            o = SUBLANE - 2 + k
            acc = acc + ext[o:o + n] * cw[k:k + 1]
        return _silu(acc).astype(BF16)

    if nsq == 0:
        pos = (pl.program_id(0) * nc) % cps
        seq_first = pos == 0
        seq_last = pos + nc == cps
        prev = jnp.where(seq_first, 0.0, prev_ref[...])
        nxt = jnp.where(seq_last, 0.0, next_ref[...])
        xcb_all = conv_silu(jnp.concatenate([prev, xbc_ref[...], nxt], axis=0), rows)
    else:
        ln = cps * q
        pad = jnp.zeros((SUBLANE, SSD_XBC), F32)
        xcb_all = jnp.concatenate(
            [conv_silu(jnp.concatenate([pad, xbc_ref[sq * ln:(sq + 1) * ln, :], pad], axis=0), ln)
             for sq in range(nsq)], axis=0)
    xcb_ref[...] = xcb_all

    lower, _ = _chunk_masks(q)
    tril = jnp.where(lower, 1.0, 0.0).astype(BF16)
    a_row = -jnp.exp(alog_row_ref[...])
    lane = lax.broadcasted_iota(jnp.int32, (q, LANE), 1)
    lane_t = lax.broadcasted_iota(jnp.int32, (2 * SUBLANE, LANE), 1)
    emat = _head_expand_matrix()
    gw = SSD_WIDTH // SSD_GROUPS

    if nsq == 0:
        @pl.when(seq_first)
        def _():
            if has_h0:
                st_ref[...] = h0_ref[0]
            else:
                st_ref[...] = jnp.zeros_like(st_ref)

    chunk_dec, chunk_states = [], []
    for c in range(nc):
        sl = slice(c * q, (c + 1) * q)
        xcb = xcb_all[sl]
        xs = xcb[:, :SSD_WIDTH].astype(F32)
        dt = dt_ref[sl, :]
        dta = dt * a_row
        la = _prefix_rows(dta, tril)
        tot = la[q - 1:q, :]
        w = jnp.exp(jnp.where(lane < SSD_HEADS, tot - la, la - dta)) * dt
        w = jnp.where(lane < 2 * SSD_HEADS, w, 0.0)
        wexp = _expand_heads(w, emat)
        etot = jnp.where(lane_t < 2 * SSD_HEADS, jnp.exp(jnp.broadcast_to(tot, (2 * SUBLANE, LANE))), 0.0)
        dec = _expand_heads(etot, emat)[:SUBLANE]
        dec_ref[c] = dec
        bmb = xcb[:, SSD_WIDTH:SSD_WIDTH + SSD_GROUPS * SSD_STATE]
        states = []
        for d in range(2):
            xw = (xs * wexp[:, d * SSD_WIDTH:(d + 1) * SSD_WIDTH]).astype(BF16)
            parts = [_dot_tn(bmb[:, grp * SSD_STATE:(grp + 1) * SSD_STATE], xw[:, grp * gw:(grp + 1) * gw])
                     for grp in range(SSD_GROUPS)]
            states.append(jnp.concatenate(parts, axis=1))
        sb_ref[c] = states[1]
        chunk_dec.append(dec[0:1, :SSD_WIDTH])
        chunk_states.append(states[0])

    if nsq == 0:
        hs = st_ref[...]
        for c in range(nc):
            hsf_ref[c] = hs.astype(BF16)
            hs = hs * chunk_dec[c] + chunk_states[c]
        st_ref[...] = hs

        @pl.when(seq_last)
        def _():
            hfin_ref[0, 0] = hs.T
    else:
        for sq in range(nsq):
            hs = h0_ref[sq] if has_h0 else jnp.zeros(st_ref.shape, F32)
            for c in range(sq * cps, (sq + 1) * cps):
                hsf_ref[c] = hs.astype(BF16)
                hs = hs * chunk_dec[c] + chunk_states[c]
            hfin_ref[sq, 0] = hs.T


def _ssd_out_kernel(*refs, cps, nc, nsq, has_h0, has_sink, nsteps):
    it = iter(refs)
    (xcb_ref, dt_ref, dtt_ref, z_ref, hsf_ref, sb_ref, dec_ref,
     alog_row_ref, alog_col_ref, dskip_ref, gn_ref) = (next(it) for _ in range(11))
    h0_ref = next(it) if has_h0 else None
    if has_sink:
        next(it)
    y_ref, hfin_ref, st_ref = next(it), next(it), next(it)

    q = SSD_CHUNK
    pos = ((nsteps - 1 - pl.program_id(0)) * nc) % cps
    seq_first = pos == 0
    seq_last = pos + nc == cps
    log2e = math.log2(math.e)

    lower, upper = _chunk_masks(q)
    tril = jnp.where(lower, 1.0, 0.0).astype(BF16)
    triu = jnp.where(upper, 1.0, 0.0).astype(BF16)
    a_row = -jnp.exp(alog_row_ref[...])
    a_col = -jnp.exp(alog_col_ref[...])
    lane = lax.broadcasted_iota(jnp.int32, (q, LANE), 1)
    rowi = lax.broadcasted_iota(jnp.int32, (2 * SUBLANE, q), 0)
    emat = _head_expand_matrix()
    rep = SSD_HEADS // SSD_GROUPS
    gw = SSD_WIDTH // SSD_GROUPS
    neg = jnp.float32(-jnp.inf)

    if nsq == 0:
        @pl.when(seq_last)
        def _():
            if has_h0:
                st_ref[...] = h0_ref[0]
            else:
                st_ref[...] = jnp.zeros_like(st_ref)

    def chunk_terms(c):
        sl = slice(c * q, (c + 1) * q)
        xcb = xcb_ref[sl, :]
        xsb = xcb[:, :SSD_WIDTH]
        bmb = xcb[:, SSD_WIDTH:SSD_WIDTH + SSD_GROUPS * SSD_STATE]
        cmb = xcb[:, SSD_WIDTH + SSD_GROUPS * SSD_STATE:]
        dt = dt_ref[sl, :]
        dtt = dtt_ref[:, sl]
        dta = dt * a_row
        dtat = dtt * a_col
        la = _prefix_rows(dta, tril)
        tot = la[q - 1:q, :]
        lcol = jnp.where(lane < SSD_HEADS, la, tot - la + dta)
        lat = sum(_dot(p, triu) for p in _split3(dtat))
        tott = lat[:, q - 1:q]
        lrow = jnp.where(rowi < SSD_HEADS, lat, tott - lat + dtat)
        lcol2 = lcol * log2e
        lrow2 = (lrow - jnp.log(dtt)) * log2e
        ecol = jnp.where(lane < 2 * SSD_HEADS, jnp.exp(lcol), 0.0)
        eexp = _expand_heads(ecol, emat)

        cbs = []
        for grp in range(SSD_GROUPS):
            cg = cmb[:, grp * SSD_STATE:(grp + 1) * SSD_STATE]
            bg = bmb[:, grp * SSD_STATE:(grp + 1) * SSD_STATE]
            cbs.append(_dot_nt(cg, bg))
        tiles = []
        for pair in range(SSD_HEADS // 2):
            xpair = xsb[:, pair * LANE:(pair + 1) * LANE]
            res = []
            for hd in (2 * pair, 2 * pair + 1):
                jf, jb = hd, SSD_HEADS + hd
                ef = jnp.exp2(jnp.where(lower, lcol2[:, jf:jf + 1] - lrow2[jf:jf + 1, :], neg))
                eb = jnp.exp2(jnp.where(upper, lcol2[:, jb:jb + 1] - lrow2[jb:jb + 1, :], neg))
                mm = (cbs[hd // rep] * (ef + eb)).astype(BF16)
                res.append(_dot(mm, xpair))
            tiles.append(jnp.where(lane < SSD_HEAD_DIM, res[0], res[1]))
        y = jnp.concatenate(tiles, axis=1)
        hsf = hsf_ref[c]
        parts = [_dot(cmb[:, grp * SSD_STATE:(grp + 1) * SSD_STATE], hsf[:, grp * gw:(grp + 1) * gw])
                 for grp in range(SSD_GROUPS)]
        y = y + jnp.concatenate(parts, axis=1) * eexp[:, :SSD_WIDTH]
        y = y + dskip_ref[...] * xsb.astype(F32)
        return y, cmb, eexp[:, SSD_WIDTH:], _silu(z_ref[sl, :])

    terms = [chunk_terms(c) for c in range(nc)]

    def finish_chunk(c, hb):
        y, cmb, eexp_b, gate = terms[c]
        hsb = hb.astype(BF16)
        parts = [_dot(cmb[:, grp * SSD_STATE:(grp + 1) * SSD_STATE], hsb[:, grp * gw:(grp + 1) * gw])
                 for grp in range(SSD_GROUPS)]
        y = (y + jnp.concatenate(parts, axis=1) * eexp_b) * gate
        y_ref[c * q:(c + 1) * q, :] = _rms_rows(y, gn_ref[...]).astype(BF16)
        return hb * dec_ref[c][0:1, SSD_WIDTH:] + sb_ref[c]

    if nsq == 0:
        hb = st_ref[...]
        for c in reversed(range(nc)):
            hb = finish_chunk(c, hb)
        st_ref[...] = hb

        @pl.when(seq_first)
        def _():
            hfin_ref[0, 0] = hb.T
    else:
        for sq in range(nsq):
            hb = h0_ref[sq] if has_h0 else jnp.zeros(st_ref.shape, F32)
            for c in reversed(range(sq * cps, (sq + 1) * cps)):
                hb = finish_chunk(c, hb)
            hfin_ref[sq, 0] = hb.T


def _ssd(xbc, dt, dtt, z, lw, cps, h0=None, sink=(0, 2, None)):
    slot0, nslots, sink_buf = sink
    t = xbc.shape[0]
    q = SSD_CHUNK
    nchunks = t // q
    nseq = nchunks // cps
    nc = SSD_STEP_CHUNKS
    if cps >= nc:
        assert cps % nc == 0
        nsq, spq, sps = 0, cps // nc, 1
    else:
        assert nc % cps == 0 and nseq % (nc // cps) == 0
        nsq, spq = nc // cps, 1
        sps = nsq
    rows = nc * q
    nsteps = nchunks // nc
    hb = rows // SUBLANE
    n8 = t // SUBLANE
    has_h0 = h0 is not None
    const = lambda i: (0, 0)
    st_block = (sps, SSD_STATE, SSD_WIDTH)
    ch_block = (nc, SSD_STATE, SSD_WIDTH)
    fin_block = (sps, 1, SSD_WIDTH, SSD_STATE)
    fin_shape = jax.ShapeDtypeStruct((nseq, nslots, SSD_WIDTH, SSD_STATE), F32)
    any_spec = pl.BlockSpec(memory_space=pl.ANY)
    dec_block = (nc, SUBLANE, 2 * SSD_WIDTH)

    in_specs = [pl.BlockSpec((rows, SSD_XBC), lambda i: (i, 0))]
    args = [xbc]
    if nsq == 0:
        in_specs += [
            pl.BlockSpec((SUBLANE, SSD_XBC), lambda i: (jnp.maximum(i * hb - 1, 0), 0)),
            pl.BlockSpec((SUBLANE, SSD_XBC), lambda i: (jnp.minimum((i + 1) * hb, n8 - 1), 0)),
        ]
        args += [xbc, xbc]
    in_specs += [
        pl.BlockSpec((rows, LANE), lambda i: (i, 0)),
        pl.BlockSpec((SSD_CONV, SSD_XBC), const),
        pl.BlockSpec((1, SSD_XBC), const),
        pl.BlockSpec((1, LANE), const),
    ]
    args += [dt, lw["conv_w"], lw["conv_b"], lw["alog_row"]]
    if has_h0:
        in_specs.append(pl.BlockSpec(st_block, lambda i: (i // spq, 0, 0)))
        args.append(h0[0])
    aliases = {}
    if sink_buf is not None:
        aliases[len(args)] = 4
        in_specs.append(any_spec)
        args.append(sink_buf)
    xcb, hsf, sb, dec, fin = pl.pallas_call(
        functools.partial(_ssd_state_kernel, cps=cps, nc=nc, nsq=nsq, has_h0=has_h0, has_sink=sink_buf is not None),
        grid=(nsteps,),
        in_specs=in_specs,
        out_specs=[
            pl.BlockSpec((rows, SSD_XBC), lambda i: (i, 0)),
            pl.BlockSpec(ch_block, lambda i: (i, 0, 0)),
            pl.BlockSpec(ch_block, lambda i: (i, 0, 0)),
            pl.BlockSpec(dec_block, lambda i: (i, 0, 0)),
            pl.BlockSpec(fin_block, lambda i: (i // spq, slot0, 0, 0)),
        ],
        out_shape=[
            jax.ShapeDtypeStruct((t, SSD_XBC), BF16),
            jax.ShapeDtypeStruct((nchunks, SSD_STATE, SSD_WIDTH), BF16),
            jax.ShapeDtypeStruct((nchunks, SSD_STATE, SSD_WIDTH), F32),
            jax.ShapeDtypeStruct((nchunks, SUBLANE, 2 * SSD_WIDTH), F32),
            fin_shape,
        ],
        input_output_aliases=aliases,
        scratch_shapes=[pltpu.VMEM((SSD_STATE, SSD_WIDTH), F32)],
        compiler_params=_cparams("arbitrary"),
        name="ssd_state",
    )(*args)

    gi = lambda i: nsteps - 1 - i
    in_specs = [
        pl.BlockSpec((rows, SSD_XBC), lambda i: (gi(i), 0)),
        pl.BlockSpec((rows, LANE), lambda i: (gi(i), 0)),
        pl.BlockSpec((2 * SSD_HEADS, rows), lambda i: (0, gi(i))),
        pl.BlockSpec((rows, SSD_WIDTH), lambda i: (gi(i), 0)),
        pl.BlockSpec(ch_block, lambda i: (gi(i), 0, 0)),
        pl.BlockSpec(ch_block, lambda i: (gi(i), 0, 0)),
        pl.BlockSpec(dec_block, lambda i: (gi(i), 0, 0)),
        pl.BlockSpec((1, LANE), const),
        pl.BlockSpec((2 * SSD_HEADS, 1), const),
        pl.BlockSpec((1, SSD_WIDTH), const),
        pl.BlockSpec((1, SSD_WIDTH), const),
    ]
    args = [xcb, dt, dtt, z, hsf, sb, dec, lw["alog_row"], lw["alog_col"], lw["dskip_row"], lw["ssd_norm"]]
    if has_h0:
        in_specs.append(pl.BlockSpec(st_block, lambda i: (gi(i) // spq, 0, 0)))
        args.append(h0[1])
    in_specs.append(any_spec)
    args.append(fin)
    y, fin = pl.pallas_call(
        functools.partial(_ssd_out_kernel, cps=cps, nc=nc, nsq=nsq, has_h0=has_h0, has_sink=True, nsteps=nsteps),
        grid=(nsteps,),
        in_specs=in_specs,
        out_specs=[
            pl.BlockSpec((rows, SSD_WIDTH), lambda i: (gi(i), 0)),
            pl.BlockSpec(fin_block, lambda i: (gi(i) // spq, slot0 + 1, 0, 0)),
        ],
        out_shape=[jax.ShapeDtypeStruct((t, SSD_WIDTH), BF16), fin_shape],
        input_output_aliases={len(args) - 1: 1},
        scratch_shapes=[pltpu.VMEM((SSD_STATE, SSD_WIDTH), F32)],
        compiler_params=_cparams("arbitrary"),
        name="ssd_out",
    )(*args)
    return y, fin


def _state_to_kernel_layout(h):
    n = h.shape[0]
    return h.transpose(0, 3, 1, 2).reshape(n, SSD_STATE, SSD_WIDTH)


def _attn_kernel(*refs, heads, has_cache, nseq_step):
    if has_cache:
        qt_ref, k_ref, vt_ref, kc_ref, vct_ref = refs[:5]
    else:
        qt_ref, k_ref, vt_ref = refs[:3]
    n_in = 5 if has_cache else 3
    o_ref = refs[n_in]
    scratch = refs[n_in + 1:]
    tq = qt_ref.shape[1] // nseq_step
    lk = k_ref.shape[0] // nseq_step
    kb = min(ATTN_KEY_BLOCK, lk)
    ones = jnp.ones((ONES_ROWS, kb), BF16)

    for sq in range(nseq_step):
        s_refs = scratch[2 * sq:2 * sq + 2]
        qcols = slice(sq * tq, (sq + 1) * tq)
        blocks = [(k_ref, vt_ref, sq * lk + i * kb) for i in range(lk // kb)]
        if has_cache:
            assert nseq_step == 1
            lc = kc_ref.shape[0]
            assert min(ATTN_KEY_BLOCK, lc) == kb
            blocks += [(kc_ref, vct_ref, i * kb) for i in range(lc // kb)]
        nblk = len(blocks)

        def score_block(hd, i, m, blocks=blocks, s_refs=s_refs, qcols=qcols):
            kr, _, off = blocks[i]
            q = qt_ref[hd * HEAD_PAD:(hd + 1) * HEAD_PAD, qcols]
            s = _dot(kr[off:off + kb, hd * HEAD_PAD:(hd + 1) * HEAD_PAD], q)
            s_refs[hd % 2][i * kb:(i + 1) * kb, :] = s
            bm = jnp.max(s, axis=0, keepdims=True)
            return bm if m is None else jnp.maximum(m, bm)

        def value_block(hd, i, m, acc, blocks=blocks, s_refs=s_refs):
            _, vr, off = blocks[i]
            p = jnp.exp2((s_refs[hd % 2][i * kb:(i + 1) * kb, :] - m).astype(BF16))
            v = vr[hd * MLA_V_DIM:(hd + 1) * MLA_V_DIM, off:off + kb]
            part = _dot(jnp.concatenate([v, ones], axis=0), p)
            return part if acc is None else acc + part

        m_cur = None
        for i in range(nblk):
            m_cur = score_block(0, i, m_cur)
        for hd in range(heads):
            m_next, acc = None, None
            for i in range(nblk):
                if nblk == 1 and hd + 1 < heads:
                    m_next = score_block(hd + 1, i, m_next)
                acc = value_block(hd, i, m_cur, acc)
                if nblk > 1 and hd + 1 < heads:
                    m_next = score_block(hd + 1, i, m_next)
            vs = slice(hd * MLA_V_DIM, (hd + 1) * MLA_V_DIM)
            o_ref[vs, qcols] = acc[:MLA_V_DIM] / acc[MLA_V_DIM:MLA_V_DIM + 1]
            m_cur = m_next


def _attention(qt, k, vt, seq_len, heads_per_step, cache=None):
    t = k.shape[0]
    nseq = t // seq_len
    tq = min(ATTN_Q_TILE, seq_len)
    nq = seq_len // tq
    g = heads_per_step
    nss = ATTN_SEQS_PER_STEP if (nq == 1 and cache is None and nseq % ATTN_SEQS_PER_STEP == 0) else 1
    in_specs = [
        pl.BlockSpec((g * HEAD_PAD, nss * tq), lambda s, h, j: (h, s * nq + j)),
        pl.BlockSpec((nss * seq_len, g * HEAD_PAD), lambda s, h, j: (s, h)),
        pl.BlockSpec((g * MLA_V_DIM, nss * seq_len), lambda s, h, j: (h, s)),
    ]
    args = [qt, k, vt]
    n_keys = seq_len
    if cache is not None:
        kc, vct = cache
        past = kc.shape[0] // nseq
        n_keys += past
        in_specs += [
            pl.BlockSpec((past, g * HEAD_PAD), lambda s, h, j: (s, h)),
            pl.BlockSpec((g * MLA_V_DIM, past), lambda s, h, j: (h, s)),
        ]
        args += [kc, vct]
    kern = functools.partial(_attn_kernel, heads=g, has_cache=cache is not None, nseq_step=nss)
    return pl.pallas_call(
        kern,
        grid=(nseq // nss, MLA_HEADS // g, nq),
        in_specs=in_specs,
        out_specs=pl.BlockSpec((g * MLA_V_DIM, nss * tq), lambda s, h, j: (h, s * nq + j)),
        out_shape=jax.ShapeDtypeStruct((MLA_WIDTH, t), F32),
        scratch_shapes=[pltpu.VMEM((n_keys, tq), F32) for _ in range(2 * nss)],
        compiler_params=_cparams("arbitrary", "arbitrary", "arbitrary"),
        name="attention",
    )(*args)


def _outproj_kernel(*refs, has_router):
    if has_router:
        (y_ref, ot_ref, x_ref, mod_ref, wa_ref, wb_ref, gm_ref, gpost_ref, gpre2_ref, rt_ref,
         x1_ref, h2_ref, comb_ref) = refs
    else:
        (y_ref, ot_ref, x_ref, mod_ref, wa_ref, wb_ref, gm_ref, gpost_ref, gpre2_ref,
         x1_ref, h2_ref) = refs
    mod = mod_ref[0]
    gate1 = mod[:, 2 * D_MODEL:3 * D_MODEL]
    shift2 = mod[:, 3 * D_MODEL:4 * D_MODEL]
    scale2 = mod[:, 4 * D_MODEL:5 * D_MODEL]
    ot = ot_ref[...]
    ms = jnp.mean(ot * ot, axis=0, keepdims=True)
    on = (ot * lax.rsqrt(ms + EPS) * gm_ref[...]).astype(BF16)
    y = _dot(y_ref[...], wa_ref[...]) + _dot_tn(on, wb_ref[...])
    x1 = x_ref[...] + gate1 * _rms_rows(y, gpost_ref[...])
    x1_ref[...] = x1
    h2 = _rms_rows(x1, gpre2_ref[...]) * (1.0 + scale2) + shift2
    h2_ref[...] = h2.astype(BF16)
    if has_router:
        hh, hm, _ = _split3(h2)
        rh, rm, _ = _split3(rt_ref[...])
        logits = _dot(hh, rh) + (_dot(hm, rh) + _dot(hh, rm))
        lane = lax.broadcasted_iota(jnp.int32, logits.shape, 1).astype(F32)
        neg = jnp.float32(-jnp.inf)
        lg = jnp.where(lane < N_EXPERTS, logits, neg)
        m1 = jnp.max(lg, axis=-1, keepdims=True)
        i1 = jnp.min(jnp.where(lg == m1, lane, float(LANE)), axis=-1, keepdims=True)
        lg2 = jnp.where(lane == i1, neg, lg)
        m2 = jnp.max(lg2, axis=-1, keepdims=True)
        i2 = jnp.min(jnp.where(lg2 == m2, lane, float(LANE)), axis=-1, keepdims=True)
        e2 = jnp.exp(m2 - m1)
        w1 = 1.0 / (1.0 + e2)
        w2 = e2 / (1.0 + e2)
        comb_ref[...] = jnp.where(lane == i1, w1, 0.0) + jnp.where(lane == i2, w2, 0.0)


def _outproj(yssd, ot, x, mod3, mod_row_fn, lw, router=None):
    t = x.shape[0]
    tm = ROW_TILE
    const = lambda i: (0, 0)
    row = lambda i: (i, 0)
    in_specs = [
        pl.BlockSpec((tm, SSD_WIDTH), row),
        pl.BlockSpec((MLA_WIDTH, tm), lambda i: (0, i)),
        pl.BlockSpec((tm, D_MODEL), row),
        pl.BlockSpec((1, 1, N_MOD * D_MODEL), lambda i: (mod_row_fn(ROW_TILE)(i), 0, 0)),
        pl.BlockSpec((SSD_WIDTH, D_MODEL), const),
        pl.BlockSpec((MLA_WIDTH, D_MODEL), const),
        pl.BlockSpec((MLA_WIDTH, 1), const),
        pl.BlockSpec((1, D_MODEL), const),
        pl.BlockSpec((1, D_MODEL), const),
    ]
    args = [yssd, ot, x, mod3, lw["w_out_a"], lw["w_out_b"], lw["mla_norm_col"],
            lw["g_post1"], lw["g_pre2"]]
    out_specs = [pl.BlockSpec((tm, D_MODEL), row), pl.BlockSpec((tm, D_MODEL), row)]
    out_shape = [jax.ShapeDtypeStruct((t, D_MODEL), F32), jax.ShapeDtypeStruct((t, D_MODEL), BF16)]
    if router is not None:
        in_specs.append(pl.BlockSpec((D_MODEL, LANE), const))
        args.append(router)
        out_specs.append(pl.BlockSpec((tm, LANE), row))
        out_shape.append(jax.ShapeDtypeStruct((t, LANE), F32))
    return pl.pallas_call(
        functools.partial(_outproj_kernel, has_router=router is not None),
        grid=(t // tm,),
        in_specs=in_specs,
        out_specs=out_specs,
        out_shape=out_shape,
        compiler_params=_cparams("arbitrary"),
        name="outproj",
    )(*args)


def _ffn_kernel(h_ref, x_ref, mod_ref, wgu_ref, wd_ref, gpost_ref, o_ref, acc_ref, *, nslab):
    e = pl.program_id(1)
    h = h_ref[...]
    f = wd_ref.shape[1]
    gu = _dot(h, wgu_ref[0])
    hid = _silu(gu[:, :f]) * gu[:, f:]
    part = _dot(hid.astype(BF16), wd_ref[0])

    @pl.when(e == 0)
    def _():
        acc_ref[...] = part

    @pl.when(e > 0)
    def _():
        acc_ref[...] += part

    @pl.when(e == nslab - 1)
    def _():
        gate2 = mod_ref[0][:, 5 * D_MODEL:6 * D_MODEL]
        o_ref[...] = x_ref[...] + gate2 * _rms_rows(acc_ref[...], gpost_ref[...])


def _ffn(h2, x1, mod3, mod_row_fn, wgu, wd, gpost):
    t = x1.shape[0]
    tm = FFN_ROW_TILE
    nslab, f, _ = wd.shape
    row = lambda i, e: (i, 0)
    in_specs = [
        pl.BlockSpec((tm, D_MODEL), row),
        pl.BlockSpec((tm, D_MODEL), row),
        pl.BlockSpec((1, 1, N_MOD * D_MODEL), lambda i, e: (mod_row_fn(tm)(i), 0, 0)),
        pl.BlockSpec((1, D_MODEL, 2 * f), lambda i, e: (e, 0, 0)),
        pl.BlockSpec((1, f, D_MODEL), lambda i, e: (e, 0, 0)),
        pl.BlockSpec((1, D_MODEL), lambda i, e: (0, 0)),
    ]
    args = [h2, x1, mod3, wgu, wd, gpost]
    return pl.pallas_call(
        functools.partial(_ffn_kernel, nslab=nslab),
        grid=(t // tm, nslab),
        in_specs=in_specs,
        out_specs=pl.BlockSpec((tm, D_MODEL), row),
        out_shape=jax.ShapeDtypeStruct((t, D_MODEL), F32),
        scratch_shapes=[pltpu.VMEM((tm, D_MODEL), F32)],
        compiler_params=_cparams("arbitrary", "arbitrary"),
        name="ffn",
    )(*args)


def _moe_kernel(h_ref, x_ref, comb_ref, mod_ref, wg_ref, wu_ref, wd_ref, gpost_ref, o_ref,
                acc_ref, rank_ref, rank_t_ref, comb_t_ref, *, nexp, sub):
    e = pl.program_id(1)
    tm = h_ref.shape[0]
    caps = MOE_CAPS
    cmax = caps[-1]

    @pl.when(e == 0)
    def _():
        acc_ref[...] = jnp.zeros_like(acc_ref)
        r_i = lax.broadcasted_iota(jnp.int32, (sub, sub), 0)
        c_i = lax.broadcasted_iota(jnp.int32, (sub, sub), 1)
        strict = jnp.where(r_i > c_i, 1.0, 0.0).astype(BF16)
        for s in range(tm // sub):
            rows = slice(s * sub, (s + 1) * sub)
            comb = comb_ref[rows, :]
            rank = _dot(strict, jnp.where(comb > 0.0, 1.0, 0.0).astype(BF16))
            rank_ref[rows, :] = rank
            rank_t_ref[:, rows] = rank.T
            comb_t_ref[:, rows] = comb.T

    def expert_pass(rows, wcol, rcol, wrow, rrow, base, cap):
        capl = -(-cap // LANE) * LANE
        slot_l = lax.broadcasted_iota(jnp.int32, (sub, capl), 1).astype(F32)
        slot_s = lax.broadcasted_iota(jnp.int32, (cap, sub), 0).astype(F32)
        gather = jnp.where(((rrow - base) == slot_s) & (wrow > 0.0), 1.0, 0.0).astype(BF16)
        scatter = jnp.where(((rcol - base) == slot_l) & (wcol > 0.0) & (slot_l < float(cap)),
                            1.0, 0.0).astype(BF16)
        xg = _dot(gather, h_ref[rows, :]).astype(BF16)
        hid = _silu(_dot(xg, wg_ref[0])) * _dot(xg, wu_ref[0])
        y = _dot(hid.astype(BF16), wd_ref[0]).astype(BF16)
        if capl > cap:
            y = jnp.concatenate([y, jnp.zeros((capl - cap, y.shape[1]), BF16)], axis=0)
        acc_ref[rows, :] += wcol * _dot(scatter, y)

    lane = lax.broadcasted_iota(jnp.int32, (sub, LANE), 1)

    def sub_tile(s, carry):
        rows = pl.ds(pl.multiple_of(s * sub, sub), sub)
        pick = lane == e
        wcol = jnp.sum(jnp.where(pick, comb_ref[rows, :], 0.0), axis=-1, keepdims=True)
        rcol = jnp.sum(jnp.where(pick, rank_ref[rows, :], 0.0), axis=-1, keepdims=True)
        wrow = comb_t_ref[pl.ds(e, 1), rows]
        rrow = rank_t_ref[pl.ds(e, 1), rows]
        count = jnp.max(jnp.where(wrow > 0.0, rrow + 1.0, 0.0))
        npass = ((count + (cmax - 1.0)) * (1.0 / cmax)).astype(jnp.int32)
        nfull = jnp.maximum(npass - 1, 0)

        def full_pass(k, c):
            expert_pass(rows, wcol, rcol, wrow, rrow, (k * cmax).astype(F32), cmax)
            return c

        lax.fori_loop(0, nfull, full_pass, 0)
        base = (nfull * cmax).astype(F32)
        left = count - base
        lo = 0
        for cap in caps:
            @pl.when(jnp.logical_and(left > float(lo), left <= float(cap)))
            def _(cap=cap):
                expert_pass(rows, wcol, rcol, wrow, rrow, base, cap)
            lo = cap
        return carry

    lax.fori_loop(0, tm // sub, sub_tile, 0)

    @pl.when(e == nexp - 1)
    def _():
        gate2 = mod_ref[0][:, 5 * D_MODEL:6 * D_MODEL]
        o_ref[...] = x_ref[...] + gate2 * _rms_rows(acc_ref[...], gpost_ref[...])


def _moe(h2, x1, comb, mod3, mod_row_fn, wg, wu, wd, gpost):
    t = x1.shape[0]
    tm = min(MOE_ROW_TILE, t)
    sub = min(MOE_SUB_TILE, tm)
    nexp, f, _ = wd.shape
    row = lambda i, e: (i, 0)
    return pl.pallas_call(
        functools.partial(_moe_kernel, nexp=nexp, sub=sub),
        grid=(t // tm, nexp),
        in_specs=[
            pl.BlockSpec((tm, D_MODEL), row),
            pl.BlockSpec((tm, D_MODEL), row),
            pl.BlockSpec((tm, LANE), row),
            pl.BlockSpec((1, 1, N_MOD * D_MODEL), lambda i, e: (mod_row_fn(tm)(i), 0, 0)),
            pl.BlockSpec((1, D_MODEL, f), lambda i, e: (e, 0, 0)),
            pl.BlockSpec((1, D_MODEL, f), lambda i, e: (e, 0, 0)),
            pl.BlockSpec((1, f, D_MODEL), lambda i, e: (e, 0, 0)),
            pl.BlockSpec((1, D_MODEL), lambda i, e: (0, 0)),
        ],
        out_specs=pl.BlockSpec((tm, D_MODEL), row),
        out_shape=jax.ShapeDtypeStruct((t, D_MODEL), F32),
        scratch_shapes=[pltpu.VMEM((tm, D_MODEL), F32), pltpu.VMEM((tm, LANE), F32),
                        pltpu.VMEM((LANE, tm), F32), pltpu.VMEM((LANE, tm), F32)],
        compiler_params=_cparams("arbitrary", "arbitrary"),
        name="moe",
    )(h2, x1, comb, mod3, wg, wu, wd, gpost)


def _prep_layer(i, p):
    w_in = p["w_in"][i]
    s1 = SSD_WIDTH
    s2 = s1 + SSD_XBC
    s3 = s2 + 2 * SSD_HEADS
    s4 = s3 + MLA_Q_RANK
    s5 = s4 + MLA_KV_RANK
    w_z, w_xbc, w_dt, w_cq, w_ckv, w_kr = (w_in[:, :s1], w_in[:, s1:s2], w_in[:, s2:s3],
                                             w_in[:, s3:s4], w_in[:, s4:s5], w_in[:, s5:])
    zc = lambda n: jnp.zeros((D_MODEL, n), F32)
    tile_a = jnp.concatenate([w_dt, zc(ROPE_LANE0 - 2 * SSD_HEADS), w_kr,
                              zc(LANE - ROPE_LANE0 - MLA_ROPE_DIM)], axis=1)
    w_in_pad = jnp.concatenate([w_z, w_xbc, w_cq, w_ckv, tile_a], axis=1).astype(BF16)

    w_uq = p["w_uq"][i].reshape(MLA_Q_RANK, MLA_HEADS, MLA_NOPE_DIM + MLA_ROPE_DIM)
    q_nope, q_rope = w_uq[..., :MLA_NOPE_DIM], w_uq[..., MLA_NOPE_DIM:]
    zq = lambda n: jnp.zeros((MLA_Q_RANK, MLA_HEADS, n), F32)
    pad = HEAD_PAD - MLA_NOPE_DIM - MLA_ROPE_DIM
    w_q = jnp.concatenate([q_nope, q_rope, zq(pad)], axis=-1).reshape(MLA_Q_RANK, -1)

    w_ukv = p["w_ukv"][i].reshape(MLA_KV_RANK, MLA_HEADS, MLA_NOPE_DIM + MLA_V_DIM)
    k_nope, v_w = w_ukv[..., :MLA_NOPE_DIM], w_ukv[..., MLA_NOPE_DIM:]
    w_uk = jnp.concatenate([k_nope, jnp.zeros((MLA_KV_RANK, MLA_HEADS, HEAD_PAD - MLA_NOPE_DIM), F32)],
                           axis=-1).reshape(MLA_KV_RANK, -1)
    w_uv = v_w.reshape(MLA_KV_RANK, MLA_WIDTH)

    dtb = p["dt_bias"][i].reshape(2 * SSD_HEADS)
    alog = p["a_log"][i].reshape(2 * SSD_HEADS)
    padl = lambda v: jnp.pad(v, (0, LANE - v.shape[0])).reshape(1, LANE)
    w_out = p["w_out"][i]
    return {
        "g_pre1": p["norm_pre_mix"][i].reshape(1, D_MODEL),
        "g_post1": p["norm_post_mix"][i].reshape(1, D_MODEL),
        "g_pre2": p["norm_pre_ffn"][i].reshape(1, D_MODEL),
        "g_post2": p["norm_post_ffn"][i].reshape(1, D_MODEL),
        "w_in": w_in_pad,
        "w_dt": w_dt.T.astype(BF16),
        "dtb_row": padl(dtb),
        "dtb_col": dtb.reshape(-1, 1),
        "alog_row": padl(alog),
        "alog_col": alog.reshape(-1, 1),
        "q_norm": p["q_norm"][i].reshape(1, -1),
        "w_q": w_q.T.astype(BF16),
        "kv_norm": p["kv_norm"][i].reshape(1, -1),
        "w_uk": w_uk.astype(BF16),
        "w_uv": w_uv.T.astype(BF16),
        "conv_w": p["conv_w"][i],
        "conv_b": p["conv_b"][i].reshape(1, -1),
        "dskip_row": jnp.repeat(p["d_skip"][i], SSD_HEAD_DIM).reshape(1, -1),
        "ssd_norm": p["ssd_norm"][i].reshape(1, -1),
        "mla_norm_col": p["mla_norm"][i].reshape(-1, 1),
        "w_out_a": w_out[:SSD_WIDTH].astype(BF16),
        "w_out_b": w_out[SSD_WIDTH:].astype(BF16),
    }


def _rope_tables(n_tokens):
    rows = n_tokens // GRID_W
    row = np.repeat(np.arange(rows, dtype=np.float32), GRID_W)
    col = np.tile(np.arange(GRID_W, dtype=np.float32), rows)
    half = MLA_ROPE_DIM // 2
    inv = (np.float32(ROPE_THETA) ** (-np.arange(0, half, 2, dtype=np.float32) / np.float32(half))).astype(np.float32)
    ar = row[:, None] * inv[None, :]
    ac = col[:, None] * inv[None, :]
    ang = np.concatenate([ar, ar, ac, ac], axis=-1).astype(np.float32)
    return jnp.asarray(np.cos(ang), F32), jnp.asarray(np.sin(ang), F32)


def _attn_tables(cos, sin, n):
    scale = (MLA_NOPE_DIM + MLA_ROPE_DIM) ** -0.5 * math.log2(math.e)
    pad = HEAD_PAD - ROPE_LANE0 - MLA_ROPE_DIM
    cosk = jnp.concatenate([jnp.zeros((n, ROPE_LANE0), F32), cos, jnp.zeros((n, pad), F32)], axis=1)
    sink = jnp.concatenate([jnp.zeros((n, ROPE_LANE0), F32), sin, jnp.zeros((n, pad), F32)], axis=1)
    cosq = jnp.concatenate([jnp.ones((n, ROPE_LANE0), F32), cos, jnp.zeros((n, pad), F32)], axis=1)
    return (cosq * scale).T, (sink * scale).T, cosk, sink


def kernel(x_prompt, x_sample, cache_ckv, cache_krope, state_ssm, c, c_ctx, w_mod, b_mod, norm_pre_mix, norm_post_mix, norm_pre_ffn, norm_post_ffn, w_in, conv_w, conv_b, dt_bias, a_log, d_skip, ssd_norm, q_norm, w_uq, kv_norm, w_ukv, mla_norm, w_out, ffn_w_gate, ffn_w_up, ffn_w_down, moe_router, moe_w_gate, moe_w_up, moe_w_down):
    params = dict(w_in=w_in, conv_w=conv_w, conv_b=conv_b, dt_bias=dt_bias, a_log=a_log, d_skip=d_skip,
                  ssd_norm=ssd_norm, q_norm=q_norm, w_uq=w_uq, kv_norm=kv_norm, w_ukv=w_ukv,
                  mla_norm=mla_norm, w_out=w_out, norm_pre_mix=norm_pre_mix, norm_post_mix=norm_post_mix,
                  norm_pre_ffn=norm_pre_ffn, norm_post_ffn=norm_post_ffn)
    batch, seq, d = x_prompt.shape
    dec_batch, dec_seq, _ = x_sample.shape
    depth = w_in.shape[0]
    past = cache_ckv.shape[2]
    tm = ROW_TILE

    cvec = jnp.concatenate([c_ctx[None, :], c, jnp.zeros((SUBLANE - 1 - dec_batch, d), F32)], axis=0)
    mod = _modulation(cvec, w_mod, b_mod)

    ones = jnp.ones((tm, MLA_ROPE_DIM), F32)
    tabs_ctx = _attn_tables(ones, jnp.zeros_like(ones), tm)
    cos, sin = _rope_tables(dec_seq)
    tabs_lat = _attn_tables(cos, sin, dec_seq)
    lat_blocks = dec_seq // tm

    xp = x_prompt.reshape(batch * seq, d)
    xs = x_sample.reshape(dec_batch * dec_seq, d)
    cache_bufs, ssm_buf = None, None
    for i in range(depth):
        lw = _prep_layer(i, params)
        mod3 = mod[i].reshape(SUBLANE, 1, N_MOD * d)
        j = i // 2
        if i % 2 == 0:
            f = ffn_w_gate.shape[2] // 2
            wgu = jnp.stack([jnp.concatenate([ffn_w_gate[j][:, s * f:(s + 1) * f], ffn_w_up[j][:, s * f:(s + 1) * f]],
                                             axis=1) for s in range(2)], axis=0).astype(BF16)
            wd = ffn_w_down[j].reshape(2, f, d).astype(BF16)
            router = None
        else:
            wgu = (moe_w_gate[j].astype(BF16), moe_w_up[j].astype(BF16))
            wd = moe_w_down[j].astype(BF16)
            router = jnp.pad(moe_router[j], ((0, 0), (0, LANE - N_EXPERTS)))

        def run(x, row_fn, tabs, tab_blocks, seq_len, heads_per_step, ctx):
            cache_out = (i, depth, seq_len, cache_bufs) if ctx is None else None
            z, xbc, dt, dtt, qt, k, vt, *caches = _inproj(x, mod3, row_fn, lw, tabs, tab_blocks, cache_out)
            cps = seq_len // SSD_CHUNK
            h0 = None
            if ctx is not None:
                h0 = (_state_to_kernel_layout(ctx[2][:, 0]), _state_to_kernel_layout(ctx[2][:, 1]))
            sink = (2 * i, 2 * depth, ssm_buf) if ctx is None else (0, 2, None)
            yssd, fin = _ssd(xbc, dt, dtt, z, lw, cps, h0=h0, sink=sink)
            cache = None
            if ctx is not None:
                kr_tile = jnp.pad(ctx[1].reshape(-1, MLA_ROPE_DIM),
                                  ((0, 0), (ROPE_LANE0, HEAD_PAD - ROPE_LANE0 - MLA_ROPE_DIM)))
                cache = _kvcache(ctx[0].reshape(-1, MLA_KV_RANK), kr_tile, lw)
            ot = _attention(qt, k, vt, seq_len, heads_per_step, cache=cache)
            outs = _outproj(yssd, ot, x, mod3, row_fn, lw, router=router)
            x1, h2 = outs[0], outs[1]
            comb = outs[2] if router is not None else None
            if comb is None:
                x2 = _ffn(h2, x1, mod3, row_fn, wgu, wd, lw["g_post2"])
            else:
                x2 = _moe(h2, x1, comb, mod3, row_fn, *wgu, wd, lw["g_post2"])
            return x2, caches, fin

        xp, cache_bufs, ssm_buf = run(xp, lambda tile: (lambda b: 0), tabs_ctx, 1, seq, MLA_HEADS, None)
        xs, _, _ = run(xs, lambda tile: (lambda b: 1 + (b * tile) // dec_seq), tabs_lat, lat_blocks, dec_seq, MLA_HEADS,
                             (cache_ckv[:, i], cache_krope[:, i], state_ssm[:, i]))
    return (xp.reshape(batch, seq, d), xs.reshape(dec_batch, dec_seq, d),
            cache_bufs[0], cache_bufs[1],
            ssm_buf.reshape(batch, depth, 2, SSD_HEADS, SSD_HEAD_DIM, SSD_STATE))
```

```python
import functools
import math

import jax
import jax.numpy as jnp
import numpy as np
from jax import lax
from jax.experimental import pallas as pl
from jax.experimental.pallas import tpu as pltpu

F32 = jnp.float32
BF16 = jnp.bfloat16

D_MODEL = 1024
GRID_W = 64
SSD_WIDTH = 512
SSD_HEAD_DIM = 64
SSD_HEADS = 8
SSD_GROUPS = 2
SSD_STATE = 64
SSD_CONV = 5
SSD_CHUNK = 128
SSD_STEP_CHUNKS = 8
SSD_XBC = SSD_WIDTH + 2 * SSD_GROUPS * SSD_STATE
MLA_WIDTH = 512
MLA_V_DIM = 64
MLA_HEADS = 8
MLA_NOPE_DIM = 64
MLA_ROPE_DIM = 32
MLA_Q_RANK = 384
MLA_KV_RANK = 256
ROPE_THETA = 10000.0
N_EXPERTS = 8
N_MOD = 6
EPS = 1e-6

LANE = 128
SUBLANE = 8
HEAD_PAD = 128
ONES_ROWS = 16
ROPE_LANE0 = MLA_NOPE_DIM
C_Z = 0
C_XBC = C_Z + SSD_WIDTH
C_CQ = C_XBC + SSD_XBC
C_CKV = C_CQ + MLA_Q_RANK
C_TA = C_CKV + MLA_KV_RANK
IN_PAD = C_TA + LANE
ROT_GROUP = MLA_ROPE_DIM // 4

VMEM_LIMIT = 56 * 1024 * 1024

ROW_TILE = 1024
ATTN_Q_TILE = 256
ATTN_KEY_BLOCK = 512
ATTN_SEQS_PER_STEP = 4
FFN_ROW_TILE = 512
MOE_ROW_TILE = 1024
MOE_SUB_TILE = 512
MOE_CAPS = (128, 160, 192, 224, 256)
MOE_SPARE_SLOTS = 2
MOD_COL_TILE = 1536
KVCACHE_ROW_TILE = 512

NT_DIMS = (((1,), (1,)), ((), ()))
TN_DIMS = (((0,), (0,)), ((), ()))


def _cparams(*sem):
    return pltpu.CompilerParams(dimension_semantics=sem, vmem_limit_bytes=VMEM_LIMIT)


def _silu(x):
    return x / (1.0 + jnp.exp(-x))


def _softplus(x):
    return jnp.maximum(x, 0.0) + jnp.log(1.0 + jnp.exp(-jnp.abs(x)))


def _rms_rows(x, g):
    ms = jnp.mean(x * x, axis=-1, keepdims=True)
    return x * lax.rsqrt(ms + EPS) * g


def _dot(a, b):
    return jnp.dot(a, b, preferred_element_type=F32)


def _dot_nt(a, b):
    return lax.dot_general(a, b, NT_DIMS, preferred_element_type=F32)


def _dot_tn(a, b):
    return lax.dot_general(a, b, TN_DIMS, preferred_element_type=F32)


def _split3(x):
    hi = x.astype(BF16)
    r1 = x - hi.astype(F32)
    mid = r1.astype(BF16)
    lo = (r1 - mid.astype(F32)).astype(BF16)
    return hi, mid, lo


def _mod_kernel(c_ref, w_ref, b_ref, o_ref):
    s = _silu(c_ref[...]).astype(BF16)
    o_ref[0] = _dot(s, w_ref[0].astype(BF16)) + b_ref[0]


def _modulation(cvec, w_mod, b_mod):
    depth, d, n = w_mod.shape
    tn = MOD_COL_TILE
    return pl.pallas_call(
        _mod_kernel,
        grid=(depth, n // tn),
        in_specs=[
            pl.BlockSpec((SUBLANE, d), lambda l, j: (0, 0)),
            pl.BlockSpec((1, d, tn), lambda l, j: (l, 0, j)),
            pl.BlockSpec((1, 1, tn), lambda l, j: (l, 0, j)),
        ],
        out_specs=pl.BlockSpec((1, SUBLANE, tn), lambda l, j: (l, 0, j)),
        out_shape=jax.ShapeDtypeStruct((depth, SUBLANE, n), F32),
        compiler_params=_cparams("arbitrary", "arbitrary"),
        name="modulation",
    )(cvec, w_mod, b_mod.reshape(depth, 1, n))


def _inproj_kernel(*refs, emit_cache, n_alias):
    (x_ref, mod_ref, gpre_ref, win_ref, wdt_ref, dtb_row_ref, dtb_col_ref, qn_ref, wq_ref, kvn_ref, wuk_ref,
     wuv_ref, cosq_ref, sinq_ref, cosk_ref, sink_ref) = refs[:16]
    outs = refs[16 + n_alias:]
    z_ref, xbc_ref, dt_ref, dtt_ref, qt_ref, k_ref, vt_ref = outs[:7]
    mod = mod_ref[0]
    shift = mod[:, 0:D_MODEL]
    scale = mod[:, D_MODEL:2 * D_MODEL]
    h = _rms_rows(x_ref[...], gpre_ref[...]) * (1.0 + scale) + shift
    hb = h.astype(BF16)
    proj = _dot(hb, win_ref[...])
    z_ref[...] = proj[:, C_Z:C_XBC]
    xbc_ref[...] = proj[:, C_XBC:C_CQ]
    cqn = _rms_rows(proj[:, C_CQ:C_CKV], qn_ref[...]).astype(BF16)
    ckvn = _rms_rows(proj[:, C_CKV:C_TA], kvn_ref[...])
    if emit_cache:
        ckvn_ref, kr_ref = outs[7:]
        nb, _, sq, _ = ckvn_ref.shape
        ckvn_ref[:, 0] = ckvn.reshape(nb, sq, MLA_KV_RANK)
    ckvb = ckvn.astype(BF16)
    ta = proj[:, C_TA:IN_PAD]
    dt_ref[...] = _softplus(ta + dtb_row_ref[...])
    if emit_cache:
        kr_ref[:, 0] = ta[:, ROPE_LANE0:ROPE_LANE0 + MLA_ROPE_DIM].reshape(nb, sq, MLA_ROPE_DIM)
    lane = lax.broadcasted_iota(jnp.int32, ta.shape, 1)
    first = (lane // ROT_GROUP) % 2 == 0
    rot = jnp.where(first, -pltpu.roll(ta, LANE - ROT_GROUP, 1), pltpu.roll(ta, ROT_GROUP, 1))
    kr_rot = ta * cosk_ref[...] + rot * sink_ref[...]
    knp = _dot(ckvb, wuk_ref[...])
    for hd in range(MLA_HEADS):
        sl = slice(hd * HEAD_PAD, (hd + 1) * HEAD_PAD)
        k_ref[:, sl] = (knp[:, sl] + kr_rot).astype(BF16)
    vt_ref[...] = _dot_nt(wuv_ref[...], ckvb).astype(BF16)
    qt = _dot_nt(wq_ref[...], cqn)
    cosq = cosq_ref[...]
    sinq = sinq_ref[...]
    g, r0 = ROT_GROUP, ROPE_LANE0
    for hd in range(MLA_HEADS):
        blk = qt[hd * HEAD_PAD:(hd + 1) * HEAD_PAD, :]
        rot = jnp.concatenate([blk[:r0], -blk[r0 + g:r0 + 2 * g], blk[r0:r0 + g], -blk[r0 + 3 * g:r0 + 4 * g],
                               blk[r0 + 2 * g:r0 + 3 * g], blk[r0 + 4 * g:]], axis=0)
        qt_ref[hd * HEAD_PAD:(hd + 1) * HEAD_PAD, :] = (blk * cosq + rot * sinq).astype(BF16)
    dtt_ref[...] = _softplus(_dot_nt(wdt_ref[...], hb) + dtb_col_ref[...])


def _inproj(x, mod3, mod_row_fn, lw, tabs, tab_blocks, cache_out=None):
    t = x.shape[0]
    tm = ROW_TILE
    nb = t // tm
    cosq, sinq, cosk, sink = tabs
    ntab = tab_blocks
    const = lambda i: (0, 0)
    row = lambda i: (i, 0)
    col = lambda i: (0, i)
    in_specs = [
        pl.BlockSpec((tm, D_MODEL), row),
        pl.BlockSpec((1, 1, N_MOD * D_MODEL), lambda i: (mod_row_fn(ROW_TILE)(i), 0, 0)),
        pl.BlockSpec((1, D_MODEL), const),
        pl.BlockSpec((D_MODEL, IN_PAD), const),
        pl.BlockSpec((2 * SSD_HEADS, D_MODEL), const),
        pl.BlockSpec((1, LANE), const),
        pl.BlockSpec((2 * SSD_HEADS, 1), const),
        pl.BlockSpec((1, MLA_Q_RANK), const),
        pl.BlockSpec((MLA_HEADS * HEAD_PAD, MLA_Q_RANK), const),
        pl.BlockSpec((1, MLA_KV_RANK), const),
        pl.BlockSpec((MLA_KV_RANK, MLA_HEADS * HEAD_PAD), const),
        pl.BlockSpec((MLA_WIDTH, MLA_KV_RANK), const),
        pl.BlockSpec((HEAD_PAD, tm), lambda i: (0, i % ntab)),
        pl.BlockSpec((HEAD_PAD, tm), lambda i: (0, i % ntab)),
        pl.BlockSpec((tm, LANE), lambda i: (i % ntab, 0)),
        pl.BlockSpec((tm, LANE), lambda i: (i % ntab, 0)),
    ]
    out_specs = [
        pl.BlockSpec((tm, SSD_WIDTH), row),
        pl.BlockSpec((tm, SSD_XBC), row),
        pl.BlockSpec((tm, LANE), row),
        pl.BlockSpec((2 * SSD_HEADS, tm), col),
        pl.BlockSpec((MLA_HEADS * HEAD_PAD, tm), col),
        pl.BlockSpec((tm, MLA_HEADS * HEAD_PAD), row),
        pl.BlockSpec((MLA_WIDTH, tm), col),
    ]
    out_shape = [
        jax.ShapeDtypeStruct((t, SSD_WIDTH), F32),
        jax.ShapeDtypeStruct((t, SSD_XBC), F32),
        jax.ShapeDtypeStruct((t, LANE), F32),
        jax.ShapeDtypeStruct((2 * SSD_HEADS, t), F32),
        jax.ShapeDtypeStruct((MLA_HEADS * HEAD_PAD, t), BF16),
        jax.ShapeDtypeStruct((t, MLA_HEADS * HEAD_PAD), BF16),
        jax.ShapeDtypeStruct((MLA_WIDTH, t), BF16),
    ]
    args = [x, mod3, lw["g_pre1"], lw["w_in"], lw["w_dt"], lw["dtb_row"], lw["dtb_col"],
            lw["q_norm"], lw["w_q"], lw["kv_norm"], lw["w_uk"], lw["w_uv"], cosq, sinq, cosk, sink]
    aliases = {}
    n_alias = 0
    if cache_out is not None:
        layer, depth, seq, bufs = cache_out
        nb_seq = tm // seq
        for rank in (MLA_KV_RANK, MLA_ROPE_DIM):
            out_specs.append(pl.BlockSpec((nb_seq, 1, seq, rank), lambda i: (i, layer, 0, 0)))
            out_shape.append(jax.ShapeDtypeStruct((t // seq, depth, seq, rank), F32))
        if bufs is not None:
            n_alias = len(bufs)
            for j, buf in enumerate(bufs):
                aliases[len(args)] = len(out_shape) - n_alias + j
                in_specs.append(pl.BlockSpec(memory_space=pl.ANY))
                args.append(buf)
    return pl.pallas_call(
        functools.partial(_inproj_kernel, emit_cache=cache_out is not None, n_alias=n_alias),
        grid=(nb,),
        in_specs=in_specs,
        out_specs=out_specs,
        out_shape=out_shape,
        input_output_aliases=aliases,
        compiler_params=_cparams("arbitrary"),
        name="inproj",
    )(*args)


def _kvcache_kernel(ckv_ref, kr_ref, wuk_ref, wuv_ref, k_ref, vt_ref):
    ckvb = ckv_ref[...].astype(BF16)
    knp = _dot(ckvb, wuk_ref[...])
    kr = kr_ref[...]
    for hd in range(MLA_HEADS):
        sl = slice(hd * HEAD_PAD, (hd + 1) * HEAD_PAD)
        k_ref[:, sl] = (knp[:, sl] + kr).astype(BF16)
    vt_ref[...] = _dot_nt(wuv_ref[...], ckvb).astype(BF16)


def _kvcache(ckv, kr_tile, lw):
    n = ckv.shape[0]
    tm = KVCACHE_ROW_TILE
    return pl.pallas_call(
        _kvcache_kernel,
        grid=(n // tm,),
        in_specs=[
            pl.BlockSpec((tm, MLA_KV_RANK), lambda i: (i, 0)),
            pl.BlockSpec((tm, LANE), lambda i: (i, 0)),
            pl.BlockSpec((MLA_KV_RANK, MLA_HEADS * HEAD_PAD), lambda i: (0, 0)),
            pl.BlockSpec((MLA_WIDTH, MLA_KV_RANK), lambda i: (0, 0)),
        ],
        out_specs=[
            pl.BlockSpec((tm, MLA_HEADS * HEAD_PAD), lambda i: (i, 0)),
            pl.BlockSpec((MLA_WIDTH, tm), lambda i: (0, i)),
        ],
        out_shape=[
            jax.ShapeDtypeStruct((n, MLA_HEADS * HEAD_PAD), BF16),
            jax.ShapeDtypeStruct((MLA_WIDTH, n), BF16),
        ],
        compiler_params=_cparams("arbitrary"),
        name="kvcache",
    )(ckv, kr_tile, lw["w_uk"], lw["w_uv"])


def _head_expand_matrix():
    r = lax.broadcasted_iota(jnp.int32, (LANE, 2 * SSD_WIDTH), 0)
    c = lax.broadcasted_iota(jnp.int32, (LANE, 2 * SSD_WIDTH), 1)
    return jnp.where(c // SSD_HEAD_DIM == r, 1.0, 0.0).astype(BF16)


def _expand_heads(v, emat):
    hi = v.astype(BF16)
    mid = (v - hi.astype(F32)).astype(BF16)
    return _dot(hi, emat) + _dot(mid, emat)


def _prefix_rows(dta, tril):
    return sum(_dot(tril, p) for p in _split3(dta))


def _chunk_masks(q):
    r_i = lax.broadcasted_iota(jnp.int32, (q, q), 0)
    c_i = lax.broadcasted_iota(jnp.int32, (q, q), 1)
    return r_i >= c_i, r_i <= c_i


def _ssd_state_kernel(*refs, cps, nc, nsq, has_h0, has_sink):
    it = iter(refs)
    xbc_ref = next(it)
    prev_ref, next_ref = (next(it), next(it)) if nsq == 0 else (None, None)
    dt_ref, cw_ref, cb_ref, alog_row_ref = (next(it) for _ in range(4))
    h0_ref = next(it) if has_h0 else None
    if has_sink:
        next(it)
    xcb_ref, hsf_ref, sb_ref, dec_ref, hfin_ref, st_ref = (next(it) for _ in range(6))

    q = SSD_CHUNK
    rows = nc * q
    cw = cw_ref[...]

    def conv_silu(ext, n):
        acc = cb_ref[...] + ext[SUBLANE - 2:SUBLANE - 2 + n] * cw[0:1]
        for k in range(1, SSD_CONV):
            o = SUBLANE - 2 + k
            acc = acc + ext[o:o + n] * cw[k:k + 1]
        return _silu(acc).astype(BF16)

    if nsq == 0:
        pos = (pl.program_id(0) * nc) % cps
        seq_first = pos == 0
        seq_last = pos + nc == cps
        prev = jnp.where(seq_first, 0.0, prev_ref[...])
        nxt = jnp.where(seq_last, 0.0, next_ref[...])
        xcb_all = conv_silu(jnp.concatenate([prev, xbc_ref[...], nxt], axis=0), rows)
    else:
        ln = cps * q
        pad = jnp.zeros((SUBLANE, SSD_XBC), F32)
        xcb_all = jnp.concatenate(
            [conv_silu(jnp.concatenate([pad, xbc_ref[sq * ln:(sq + 1) * ln, :], pad], axis=0), ln)
             for sq in range(nsq)], axis=0)
    xcb_ref[...] = xcb_all

    lower, _ = _chunk_masks(q)
    tril = jnp.where(lower, 1.0, 0.0).astype(BF16)
    a_row = -jnp.exp(alog_row_ref[...])
    lane = lax.broadcasted_iota(jnp.int32, (q, LANE), 1)
    lane_t = lax.broadcasted_iota(jnp.int32, (2 * SUBLANE, LANE), 1)
    emat = _head_expand_matrix()
    gw = SSD_WIDTH // SSD_GROUPS

    if nsq == 0:
        @pl.when(seq_first)
        def _():
            if has_h0:
                st_ref[...] = h0_ref[0]
            else:
                st_ref[...] = jnp.zeros_like(st_ref)

    chunk_dec, chunk_states = [], []
    for c in range(nc):
        sl = slice(c * q, (c + 1) * q)
        xcb = xcb_all[sl]
        xs = xcb[:, :SSD_WIDTH].astype(F32)
        dt = dt_ref[sl, :]
        dta = dt * a_row
        la = _prefix_rows(dta, tril)
        tot = la[q - 1:q, :]
        w = jnp.exp(jnp.where(lane < SSD_HEADS, tot - la, la - dta)) * dt
        w = jnp.where(lane < 2 * SSD_HEADS, w, 0.0)
        wexp = _expand_heads(w, emat)
        etot = jnp.where(lane_t < 2 * SSD_HEADS, jnp.exp(jnp.broadcast_to(tot, (2 * SUBLANE, LANE))), 0.0)
        dec = _expand_heads(etot, emat)[:SUBLANE]
        dec_ref[c] = dec
        bmb = xcb[:, SSD_WIDTH:SSD_WIDTH + SSD_GROUPS * SSD_STATE]
        states = []
        for d in range(2):
            xw = (xs * wexp[:, d * SSD_WIDTH:(d + 1) * SSD_WIDTH]).astype(BF16)
            parts = [_dot_tn(bmb[:, grp * SSD_STATE:(grp + 1) * SSD_STATE], xw[:, grp * gw:(grp + 1) * gw])
                     for grp in range(SSD_GROUPS)]
            states.append(jnp.concatenate(parts, axis=1))
        sb_ref[c] = states[1]
        chunk_dec.append(dec[0:1, :SSD_WIDTH])
        chunk_states.append(states[0])

    if nsq == 0:
        hs = st_ref[...]
        for c in range(nc):
            hsf_ref[c] = hs.astype(BF16)
            hs = hs * chunk_dec[c] + chunk_states[c]
        st_ref[...] = hs

        @pl.when(seq_last)
        def _():
            hfin_ref[0, 0] = hs.T
    else:
        for sq in range(nsq):
            hs = h0_ref[sq] if has_h0 else jnp.zeros(st_ref.shape, F32)
            for c in range(sq * cps, (sq + 1) * cps):
                hsf_ref[c] = hs.astype(BF16)
                hs = hs * chunk_dec[c] + chunk_states[c]
            hfin_ref[sq, 0] = hs.T


def _ssd_out_kernel(*refs, cps, nc, nsq, has_h0, has_sink, nsteps):
    it = iter(refs)
    (xcb_ref, dt_ref, dtt_ref, z_ref, hsf_ref, sb_ref, dec_ref,
     alog_row_ref, alog_col_ref, dskip_ref, gn_ref) = (next(it) for _ in range(11))
    h0_ref = next(it) if has_h0 else None
    if has_sink:
        next(it)
    y_ref, hfin_ref, st_ref = next(it), next(it), next(it)

    q = SSD_CHUNK
    pos = ((nsteps - 1 - pl.program_id(0)) * nc) % cps
    seq_first = pos == 0
    seq_last = pos + nc == cps
    log2e = math.log2(math.e)

    lower, upper = _chunk_masks(q)
    tril = jnp.where(lower, 1.0, 0.0).astype(BF16)
    triu = jnp.where(upper, 1.0, 0.0).astype(BF16)
    a_row = -jnp.exp(alog_row_ref[...])
    a_col = -jnp.exp(alog_col_ref[...])
    lane = lax.broadcasted_iota(jnp.int32, (q, LANE), 1)
    rowi = lax.broadcasted_iota(jnp.int32, (2 * SUBLANE, q), 0)
    emat = _head_expand_matrix()
    rep = SSD_HEADS // SSD_GROUPS
    gw = SSD_WIDTH // SSD_GROUPS
    neg = jnp.float32(-jnp.inf)

    if nsq == 0:
        @pl.when(seq_last)
        def _():
            if has_h0:
                st_ref[...] = h0_ref[0]
            else:
                st_ref[...] = jnp.zeros_like(st_ref)

    def chunk_terms(c):
        sl = slice(c * q, (c + 1) * q)
        xcb = xcb_ref[sl, :]
        xsb = xcb[:, :SSD_WIDTH]
        bmb = xcb[:, SSD_WIDTH:SSD_WIDTH + SSD_GROUPS * SSD_STATE]
        cmb = xcb[:, SSD_WIDTH + SSD_GROUPS * SSD_STATE:]
        dt = dt_ref[sl, :]
        dtt = dtt_ref[:, sl]
        dta = dt * a_row
        dtat = dtt * a_col
        la = _prefix_rows(dta, tril)
        tot = la[q - 1:q, :]
        lcol = jnp.where(lane < SSD_HEADS, la, tot - la + dta)
        lat = sum(_dot(p, triu) for p in _split3(dtat))
        tott = lat[:, q - 1:q]
        lrow = jnp.where(rowi < SSD_HEADS, lat, tott - lat + dtat)
        lcol2 = lcol * log2e
        lrow2 = (lrow - jnp.log(dtt)) * log2e
        ecol = jnp.where(lane < 2 * SSD_HEADS, jnp.exp(lcol), 0.0)
        eexp = _expand_heads(ecol, emat)

        cbs = []
        for grp in range(SSD_GROUPS):
            cg = cmb[:, grp * SSD_STATE:(grp + 1) * SSD_STATE]
            bg = bmb[:, grp * SSD_STATE:(grp + 1) * SSD_STATE]
            cbs.append(_dot_nt(cg, bg))
        tiles = []
        for pair in range(SSD_HEADS // 2):
            xpair = xsb[:, pair * LANE:(pair + 1) * LANE]
            res = []
            for hd in (2 * pair, 2 * pair + 1):
                jf, jb = hd, SSD_HEADS + hd
                ef = jnp.exp2(jnp.where(lower, lcol2[:, jf:jf + 1] - lrow2[jf:jf + 1, :], neg))
                eb = jnp.exp2(jnp.where(upper, lcol2[:, jb:jb + 1] - lrow2[jb:jb + 1, :], neg))
                mm = (cbs[hd // rep] * (ef + eb)).astype(BF16)
                res.append(_dot(mm, xpair))
            tiles.append(jnp.where(lane < SSD_HEAD_DIM, res[0], res[1]))
        y = jnp.concatenate(tiles, axis=1)
        hsf = hsf_ref[c]
        parts = [_dot(cmb[:, grp * SSD_STATE:(grp + 1) * SSD_STATE], hsf[:, grp * gw:(grp + 1) * gw])
                 for grp in range(SSD_GROUPS)]
        y = y + jnp.concatenate(parts, axis=1) * eexp[:, :SSD_WIDTH]
        y = y + dskip_ref[...] * xsb.astype(F32)
        return y, cmb, eexp[:, SSD_WIDTH:], _silu(z_ref[sl, :])

    terms = [chunk_terms(c) for c in range(nc)]

    def finish_chunk(c, hb):
        y, cmb, eexp_b, gate = terms[c]
        hsb = hb.astype(BF16)
        parts = [_dot(cmb[:, grp * SSD_STATE:(grp + 1) * SSD_STATE], hsb[:, grp * gw:(grp + 1) * gw])
                 for grp in range(SSD_GROUPS)]
        y = (y + jnp.concatenate(parts, axis=1) * eexp_b) * gate
        y_ref[c * q:(c + 1) * q, :] = _rms_rows(y, gn_ref[...]).astype(BF16)
        return hb * dec_ref[c][0:1, SSD_WIDTH:] + sb_ref[c]

    if nsq == 0:
        hb = st_ref[...]
        for c in reversed(range(nc)):
            hb = finish_chunk(c, hb)
        st_ref[...] = hb

        @pl.when(seq_first)
        def _():
            hfin_ref[0, 0] = hb.T
    else:
        for sq in range(nsq):
            hb = h0_ref[sq] if has_h0 else jnp.zeros(st_ref.shape, F32)
            for c in reversed(range(sq * cps, (sq + 1) * cps)):
                hb = finish_chunk(c, hb)
            hfin_ref[sq, 0] = hb.T


def _ssd(xbc, dt, dtt, z, lw, cps, h0=None, sink=(0, 2, None)):
    slot0, nslots, sink_buf = sink
    t = xbc.shape[0]
    q = SSD_CHUNK
    nchunks = t // q
    nseq = nchunks // cps
    nc = SSD_STEP_CHUNKS
    if cps >= nc:
        assert cps % nc == 0
        nsq, spq, sps = 0, cps // nc, 1
    else:
        assert nc % cps == 0 and nseq % (nc // cps) == 0
        nsq, spq = nc // cps, 1
        sps = nsq
    rows = nc * q
    nsteps = nchunks // nc
    hb = rows // SUBLANE
    n8 = t // SUBLANE
    has_h0 = h0 is not None
    const = lambda i: (0, 0)
    st_block = (sps, SSD_STATE, SSD_WIDTH)
    ch_block = (nc, SSD_STATE, SSD_WIDTH)
    fin_block = (sps, 1, SSD_WIDTH, SSD_STATE)
    fin_shape = jax.ShapeDtypeStruct((nseq, nslots, SSD_WIDTH, SSD_STATE), F32)
    any_spec = pl.BlockSpec(memory_space=pl.ANY)
    dec_block = (nc, SUBLANE, 2 * SSD_WIDTH)

    in_specs = [pl.BlockSpec((rows, SSD_XBC), lambda i: (i, 0))]
    args = [xbc]
    if nsq == 0:
        in_specs += [
            pl.BlockSpec((SUBLANE, SSD_XBC), lambda i: (jnp.maximum(i * hb - 1, 0), 0)),
            pl.BlockSpec((SUBLANE, SSD_XBC), lambda i: (jnp.minimum((i + 1) * hb, n8 - 1), 0)),
        ]
        args += [xbc, xbc]
    in_specs += [
        pl.BlockSpec((rows, LANE), lambda i: (i, 0)),
        pl.BlockSpec((SSD_CONV, SSD_XBC), const),
        pl.BlockSpec((1, SSD_XBC), const),
        pl.BlockSpec((1, LANE), const),
    ]
    args += [dt, lw["conv_w"], lw["conv_b"], lw["alog_row"]]
    if has_h0:
        in_specs.append(pl.BlockSpec(st_block, lambda i: (i // spq, 0, 0)))
        args.append(h0[0])
    aliases = {}
    if sink_buf is not None:
        aliases[len(args)] = 4
        in_specs.append(any_spec)
        args.append(sink_buf)
    xcb, hsf, sb, dec, fin = pl.pallas_call(
        functools.partial(_ssd_state_kernel, cps=cps, nc=nc, nsq=nsq, has_h0=has_h0, has_sink=sink_buf is not None),
        grid=(nsteps,),
        in_specs=in_specs,
        out_specs=[
            pl.BlockSpec((rows, SSD_XBC), lambda i: (i, 0)),
            pl.BlockSpec(ch_block, lambda i: (i, 0, 0)),
            pl.BlockSpec(ch_block, lambda i: (i, 0, 0)),
            pl.BlockSpec(dec_block, lambda i: (i, 0, 0)),
            pl.BlockSpec(fin_block, lambda i: (i // spq, slot0, 0, 0)),
        ],
        out_shape=[
            jax.ShapeDtypeStruct((t, SSD_XBC), BF16),
            jax.ShapeDtypeStruct((nchunks, SSD_STATE, SSD_WIDTH), BF16),
            jax.ShapeDtypeStruct((nchunks, SSD_STATE, SSD_WIDTH), F32),
            jax.ShapeDtypeStruct((nchunks, SUBLANE, 2 * SSD_WIDTH), F32),
            fin_shape,
        ],
        input_output_aliases=aliases,
        scratch_shapes=[pltpu.VMEM((SSD_STATE, SSD_WIDTH), F32)],
        compiler_params=_cparams("arbitrary"),
        name="ssd_state",
    )(*args)

    gi = lambda i: nsteps - 1 - i
    in_specs = [
        pl.BlockSpec((rows, SSD_XBC), lambda i: (gi(i), 0)),
        pl.BlockSpec((rows, LANE), lambda i: (gi(i), 0)),
        pl.BlockSpec((2 * SSD_HEADS, rows), lambda i: (0, gi(i))),
        pl.BlockSpec((rows, SSD_WIDTH), lambda i: (gi(i), 0)),
        pl.BlockSpec(ch_block, lambda i: (gi(i), 0, 0)),
        pl.BlockSpec(ch_block, lambda i: (gi(i), 0, 0)),
        pl.BlockSpec(dec_block, lambda i: (gi(i), 0, 0)),
        pl.BlockSpec((1, LANE), const),
        pl.BlockSpec((2 * SSD_HEADS, 1), const),
        pl.BlockSpec((1, SSD_WIDTH), const),
        pl.BlockSpec((1, SSD_WIDTH), const),
    ]
    args = [xcb, dt, dtt, z, hsf, sb, dec, lw["alog_row"], lw["alog_col"], lw["dskip_row"], lw["ssd_norm"]]
    if has_h0:
        in_specs.append(pl.BlockSpec(st_block, lambda i: (gi(i) // spq, 0, 0)))
        args.append(h0[1])
    in_specs.append(any_spec)
    args.append(fin)
    y, fin = pl.pallas_call(
        functools.partial(_ssd_out_kernel, cps=cps, nc=nc, nsq=nsq, has_h0=has_h0, has_sink=True, nsteps=nsteps),
        grid=(nsteps,),
        in_specs=in_specs,
        out_specs=[
            pl.BlockSpec((rows, SSD_WIDTH), lambda i: (gi(i), 0)),
            pl.BlockSpec(fin_block, lambda i: (gi(i) // spq, slot0 + 1, 0, 0)),
        ],
        out_shape=[jax.ShapeDtypeStruct((t, SSD_WIDTH), BF16), fin_shape],
        input_output_aliases={len(args) - 1: 1},
        scratch_shapes=[pltpu.VMEM((SSD_STATE, SSD_WIDTH), F32)],
        compiler_params=_cparams("arbitrary"),
        name="ssd_out",
    )(*args)
    return y, fin


def _state_to_kernel_layout(h):
    n = h.shape[0]
    return h.transpose(0, 3, 1, 2).reshape(n, SSD_STATE, SSD_WIDTH)


def _attn_kernel(*refs, heads, has_cache, nseq_step):
    if has_cache:
        qt_ref, k_ref, vt_ref, kc_ref, vct_ref = refs[:5]
    else:
        qt_ref, k_ref, vt_ref = refs[:3]
    n_in = 5 if has_cache else 3
    o_ref = refs[n_in]
    scratch = refs[n_in + 1:]
    tq = qt_ref.shape[1] // nseq_step
    lk = k_ref.shape[0] // nseq_step
    kb = min(ATTN_KEY_BLOCK, lk)
    ones = jnp.ones((ONES_ROWS, kb), BF16)

    for sq in range(nseq_step):
        s_refs = scratch[2 * sq:2 * sq + 2]
        qcols = slice(sq * tq, (sq + 1) * tq)
        blocks = [(k_ref, vt_ref, sq * lk + i * kb) for i in range(lk // kb)]
        if has_cache:
            assert nseq_step == 1
            lc = kc_ref.shape[0]
            assert min(ATTN_KEY_BLOCK, lc) == kb
            blocks += [(kc_ref, vct_ref, i * kb) for i in range(lc // kb)]
        nblk = len(blocks)

        def score_block(hd, i, m, blocks=blocks, s_refs=s_refs, qcols=qcols):
            kr, _, off = blocks[i]
            q = qt_ref[hd * HEAD_PAD:(hd + 1) * HEAD_PAD, qcols]
            s = _dot(kr[off:off + kb, hd * HEAD_PAD:(hd + 1) * HEAD_PAD], q)
            s_refs[hd % 2][i * kb:(i + 1) * kb, :] = s
            bm = jnp.max(s, axis=0, keepdims=True)
            return bm if m is None else jnp.maximum(m, bm)

        def value_block(hd, i, m, acc, blocks=blocks, s_refs=s_refs):
            _, vr, off = blocks[i]
            p = jnp.exp2((s_refs[hd % 2][i * kb:(i + 1) * kb, :] - m).astype(BF16))
            v = vr[hd * MLA_V_DIM:(hd + 1) * MLA_V_DIM, off:off + kb]
            part = _dot(jnp.concatenate([v, ones], axis=0), p)
            return part if acc is None else acc + part

        m_cur = None
        for i in range(nblk):
            m_cur = score_block(0, i, m_cur)
        for hd in range(heads):
            m_next, acc = None, None
            for i in range(nblk):
                if nblk == 1 and hd + 1 < heads:
                    m_next = score_block(hd + 1, i, m_next)
                acc = value_block(hd, i, m_cur, acc)
                if nblk > 1 and hd + 1 < heads:
                    m_next = score_block(hd + 1, i, m_next)
            vs = slice(hd * MLA_V_DIM, (hd + 1) * MLA_V_DIM)
            o_ref[vs, qcols] = acc[:MLA_V_DIM] / acc[MLA_V_DIM:MLA_V_DIM + 1]
            m_cur = m_next


def _attention(qt, k, vt, seq_len, heads_per_step, cache=None):
    t = k.shape[0]
    nseq = t // seq_len
    tq = min(ATTN_Q_TILE, seq_len)
    nq = seq_len // tq
    g = heads_per_step
    nss = ATTN_SEQS_PER_STEP if (nq == 1 and cache is None and nseq % ATTN_SEQS_PER_STEP == 0) else 1
    in_specs = [
        pl.BlockSpec((g * HEAD_PAD, nss * tq), lambda s, h, j: (h, s * nq + j)),
        pl.BlockSpec((nss * seq_len, g * HEAD_PAD), lambda s, h, j: (s, h)),
        pl.BlockSpec((g * MLA_V_DIM, nss * seq_len), lambda s, h, j: (h, s)),
    ]
    args = [qt, k, vt]
    n_keys = seq_len
    if cache is not None:
        kc, vct = cache
        past = kc.shape[0] // nseq
        n_keys += past
        in_specs += [
            pl.BlockSpec((past, g * HEAD_PAD), lambda s, h, j: (s, h)),
            pl.BlockSpec((g * MLA_V_DIM, past), lambda s, h, j: (h, s)),
        ]
        args += [kc, vct]
    kern = functools.partial(_attn_kernel, heads=g, has_cache=cache is not None, nseq_step=nss)
    return pl.pallas_call(
        kern,
        grid=(nseq // nss, MLA_HEADS // g, nq),
        in_specs=in_specs,
        out_specs=pl.BlockSpec((g * MLA_V_DIM, nss * tq), lambda s, h, j: (h, s * nq + j)),
        out_shape=jax.ShapeDtypeStruct((MLA_WIDTH, t), F32),
        scratch_shapes=[pltpu.VMEM((n_keys, tq), F32) for _ in range(2 * nss)],
        compiler_params=_cparams("arbitrary", "arbitrary", "arbitrary"),
        name="attention",
    )(*args)


def _outproj_kernel(*refs, has_router):
    if has_router:
        (y_ref, ot_ref, x_ref, mod_ref, wa_ref, wb_ref, gm_ref, gpost_ref, gpre2_ref, rt_ref,
         x1_ref, h2_ref, comb_ref) = refs
    else:
        (y_ref, ot_ref, x_ref, mod_ref, wa_ref, wb_ref, gm_ref, gpost_ref, gpre2_ref,
         x1_ref, h2_ref) = refs
    mod = mod_ref[0]
    gate1 = mod[:, 2 * D_MODEL:3 * D_MODEL]
    shift2 = mod[:, 3 * D_MODEL:4 * D_MODEL]
    scale2 = mod[:, 4 * D_MODEL:5 * D_MODEL]
    ot = ot_ref[...]
    ms = jnp.mean(ot * ot, axis=0, keepdims=True)
    on = (ot * lax.rsqrt(ms + EPS) * gm_ref[...]).astype(BF16)
    y = _dot(y_ref[...], wa_ref[...]) + _dot_tn(on, wb_ref[...])
    x1 = x_ref[...] + gate1 * _rms_rows(y, gpost_ref[...])
    x1_ref[...] = x1
    h2 = _rms_rows(x1, gpre2_ref[...]) * (1.0 + scale2) + shift2
    h2_ref[...] = h2.astype(BF16)
    if has_router:
        hh, hm, _ = _split3(h2)
        rh, rm, _ = _split3(rt_ref[...])
        logits = _dot(hh, rh) + (_dot(hm, rh) + _dot(hh, rm))
        lane = lax.broadcasted_iota(jnp.int32, logits.shape, 1).astype(F32)
        neg = jnp.float32(-jnp.inf)
        lg = jnp.where(lane < N_EXPERTS, logits, neg)
        m1 = jnp.max(lg, axis=-1, keepdims=True)
        i1 = jnp.min(jnp.where(lg == m1, lane, float(LANE)), axis=-1, keepdims=True)
        lg2 = jnp.where(lane == i1, neg, lg)
        m2 = jnp.max(lg2, axis=-1, keepdims=True)
        i2 = jnp.min(jnp.where(lg2 == m2, lane, float(LANE)), axis=-1, keepdims=True)
        e2 = jnp.exp(m2 - m1)
        w1 = 1.0 / (1.0 + e2)
        w2 = e2 / (1.0 + e2)
        comb_ref[...] = jnp.where(lane == i1, w1, 0.0) + jnp.where(lane == i2, w2, 0.0)


def _outproj(yssd, ot, x, mod3, mod_row_fn, lw, router=None):
    t = x.shape[0]
    tm = ROW_TILE
    const = lambda i: (0, 0)
    row = lambda i: (i, 0)
    in_specs = [
        pl.BlockSpec((tm, SSD_WIDTH), row),
        pl.BlockSpec((MLA_WIDTH, tm), lambda i: (0, i)),
        pl.BlockSpec((tm, D_MODEL), row),
        pl.BlockSpec((1, 1, N_MOD * D_MODEL), lambda i: (mod_row_fn(ROW_TILE)(i), 0, 0)),
        pl.BlockSpec((SSD_WIDTH, D_MODEL), const),
        pl.BlockSpec((MLA_WIDTH, D_MODEL), const),
        pl.BlockSpec((MLA_WIDTH, 1), const),
        pl.BlockSpec((1, D_MODEL), const),
        pl.BlockSpec((1, D_MODEL), const),
    ]
    args = [yssd, ot, x, mod3, lw["w_out_a"], lw["w_out_b"], lw["mla_norm_col"],
            lw["g_post1"], lw["g_pre2"]]
    out_specs = [pl.BlockSpec((tm, D_MODEL), row), pl.BlockSpec((tm, D_MODEL), row)]
    out_shape = [jax.ShapeDtypeStruct((t, D_MODEL), F32), jax.ShapeDtypeStruct((t, D_MODEL), BF16)]
    if router is not None:
        in_specs.append(pl.BlockSpec((D_MODEL, LANE), const))
        args.append(router)
        out_specs.append(pl.BlockSpec((tm, LANE), row))
        out_shape.append(jax.ShapeDtypeStruct((t, LANE), F32))
    return pl.pallas_call(
        functools.partial(_outproj_kernel, has_router=router is not None),
        grid=(t // tm,),
        in_specs=in_specs,
        out_specs=out_specs,
        out_shape=out_shape,
        compiler_params=_cparams("arbitrary"),
        name="outproj",
    )(*args)


def _ffn_kernel(h_ref, x_ref, mod_ref, wgu_ref, wd_ref, gpost_ref, o_ref, acc_ref, *, nslab):
    e = pl.program_id(1)
    h = h_ref[...]
    f = wd_ref.shape[1]
    gu = _dot(h, wgu_ref[0])
    hid = _silu(gu[:, :f]) * gu[:, f:]
    part = _dot(hid.astype(BF16), wd_ref[0])

    @pl.when(e == 0)
    def _():
        acc_ref[...] = part

    @pl.when(e > 0)
    def _():
        acc_ref[...] += part

    @pl.when(e == nslab - 1)
    def _():
        gate2 = mod_ref[0][:, 5 * D_MODEL:6 * D_MODEL]
        o_ref[...] = x_ref[...] + gate2 * _rms_rows(acc_ref[...], gpost_ref[...])


def _ffn(h2, x1, mod3, mod_row_fn, wgu, wd, gpost):
    t = x1.shape[0]
    tm = FFN_ROW_TILE
    nslab, f, _ = wd.shape
    row = lambda i, e: (i, 0)
    in_specs = [
        pl.BlockSpec((tm, D_MODEL), row),
        pl.BlockSpec((tm, D_MODEL), row),
        pl.BlockSpec((1, 1, N_MOD * D_MODEL), lambda i, e: (mod_row_fn(tm)(i), 0, 0)),
        pl.BlockSpec((1, D_MODEL, 2 * f), lambda i, e: (e, 0, 0)),
        pl.BlockSpec((1, f, D_MODEL), lambda i, e: (e, 0, 0)),
        pl.BlockSpec((1, D_MODEL), lambda i, e: (0, 0)),
    ]
    args = [h2, x1, mod3, wgu, wd, gpost]
    return pl.pallas_call(
        functools.partial(_ffn_kernel, nslab=nslab),
        grid=(t // tm, nslab),
        in_specs=in_specs,
        out_specs=pl.BlockSpec((tm, D_MODEL), row),
        out_shape=jax.ShapeDtypeStruct((t, D_MODEL), F32),
        scratch_shapes=[pltpu.VMEM((tm, D_MODEL), F32)],
        compiler_params=_cparams("arbitrary", "arbitrary"),
        name="ffn",
    )(*args)


def _moe_kernel(h_ref, x_ref, comb_ref, mod_ref, wg_ref, wu_ref, wd_ref, gpost_ref, o_ref,
                acc_ref, rank_ref, rank_t_ref, comb_t_ref, *, nexp, sub):
    e = pl.program_id(1)
    tm = h_ref.shape[0]
    caps = MOE_CAPS
    cmax = caps[-1]

    @pl.when(e == 0)
    def _():
        acc_ref[...] = jnp.zeros_like(acc_ref)
        r_i = lax.broadcasted_iota(jnp.int32, (sub, sub), 0)
        c_i = lax.broadcasted_iota(jnp.int32, (sub, sub), 1)
        strict = jnp.where(r_i > c_i, 1.0, 0.0).astype(BF16)
        for s in range(tm // sub):
            rows = slice(s * sub, (s + 1) * sub)
            comb = comb_ref[rows, :]
            rank = _dot(strict, jnp.where(comb > 0.0, 1.0, 0.0).astype(BF16))
            rank_ref[rows, :] = rank
            rank_t_ref[:, rows] = rank.T
            comb_t_ref[:, rows] = comb.T

    def expert_pass(rows, wcol, rcol, wrow, rrow, base, cap):
        capl = -(-cap // LANE) * LANE
        slot_l = lax.broadcasted_iota(jnp.int32, (sub, capl), 1).astype(F32)
        slot_s = lax.broadcasted_iota(jnp.int32, (cap, sub), 0).astype(F32)
        gather = jnp.where(((rrow - base) == slot_s) & (wrow > 0.0), 1.0, 0.0).astype(BF16)
        scatter = jnp.where(((rcol - base) == slot_l) & (wcol > 0.0) & (slot_l < float(cap)),
                            1.0, 0.0).astype(BF16)
        xg = _dot(gather, h_ref[rows, :]).astype(BF16)
        hid = _silu(_dot(xg, wg_ref[0])) * _dot(xg, wu_ref[0])
        y = _dot(hid.astype(BF16), wd_ref[0]).astype(BF16)
        if capl > cap:
            y = jnp.concatenate([y, jnp.zeros((capl - cap, y.shape[1]), BF16)], axis=0)
        acc_ref[rows, :] += wcol * _dot(scatter, y)

    lane = lax.broadcasted_iota(jnp.int32, (sub, LANE), 1)

    def sub_tile(s, carry):
        rows = pl.ds(pl.multiple_of(s * sub, sub), sub)
        pick = lane == e
        wcol = jnp.sum(jnp.where(pick, comb_ref[rows, :], 0.0), axis=-1, keepdims=True)
        rcol = jnp.sum(jnp.where(pick, rank_ref[rows, :], 0.0), axis=-1, keepdims=True)
        wrow = comb_t_ref[pl.ds(e, 1), rows]
        rrow = rank_t_ref[pl.ds(e, 1), rows]
        count = jnp.max(jnp.where(wrow > 0.0, rrow + 1.0, 0.0))
        want = count + float(MOE_SPARE_SLOTS)
        npass = ((want + (cmax - 1.0)) * (1.0 / cmax)).astype(jnp.int32)
        nfull = npass - 1

        def full_pass(k, c):
            expert_pass(rows, wcol, rcol, wrow, rrow, (k * cmax).astype(F32), cmax)
            return c

        lax.fori_loop(0, nfull, full_pass, 0)
        base = (nfull * cmax).astype(F32)
        left = want - base
        lo = 0
        for cap in caps:
            @pl.when(jnp.logical_and(left > float(lo), left <= float(cap)))
            def _(cap=cap):
                expert_pass(rows, wcol, rcol, wrow, rrow, base, cap)
            lo = cap
        return carry

    lax.fori_loop(0, tm // sub, sub_tile, 0)

    @pl.when(e == nexp - 1)
    def _():
        gate2 = mod_ref[0][:, 5 * D_MODEL:6 * D_MODEL]
        o_ref[...] = x_ref[...] + gate2 * _rms_rows(acc_ref[...], gpost_ref[...])


def _moe(h2, x1, comb, mod3, mod_row_fn, wg, wu, wd, gpost):
    t = x1.shape[0]
    tm = min(MOE_ROW_TILE, t)
    sub = min(MOE_SUB_TILE, tm)
    nexp, f, _ = wd.shape
    row = lambda i, e: (i, 0)
    return pl.pallas_call(
        functools.partial(_moe_kernel, nexp=nexp, sub=sub),
        grid=(t // tm, nexp),
        in_specs=[
            pl.BlockSpec((tm, D_MODEL), row),
            pl.BlockSpec((tm, D_MODEL), row),
            pl.BlockSpec((tm, LANE), row),
            pl.BlockSpec((1, 1, N_MOD * D_MODEL), lambda i, e: (mod_row_fn(tm)(i), 0, 0)),
            pl.BlockSpec((1, D_MODEL, f), lambda i, e: (e, 0, 0)),
            pl.BlockSpec((1, D_MODEL, f), lambda i, e: (e, 0, 0)),
            pl.BlockSpec((1, f, D_MODEL), lambda i, e: (e, 0, 0)),
            pl.BlockSpec((1, D_MODEL), lambda i, e: (0, 0)),
        ],
        out_specs=pl.BlockSpec((tm, D_MODEL), row),
        out_shape=jax.ShapeDtypeStruct((t, D_MODEL), F32),
        scratch_shapes=[pltpu.VMEM((tm, D_MODEL), F32), pltpu.VMEM((tm, LANE), F32),
                        pltpu.VMEM((LANE, tm), F32), pltpu.VMEM((LANE, tm), F32)],
        compiler_params=_cparams("arbitrary", "arbitrary"),
        name="moe",
    )(h2, x1, comb, mod3, wg, wu, wd, gpost)


def _prep_layer(i, p):
    w_in = p["w_in"][i]
    s1 = SSD_WIDTH
    s2 = s1 + SSD_XBC
    s3 = s2 + 2 * SSD_HEADS
    s4 = s3 + MLA_Q_RANK
    s5 = s4 + MLA_KV_RANK
    w_z, w_xbc, w_dt, w_cq, w_ckv, w_kr = (w_in[:, :s1], w_in[:, s1:s2], w_in[:, s2:s3],
                                             w_in[:, s3:s4], w_in[:, s4:s5], w_in[:, s5:])
    zc = lambda n: jnp.zeros((D_MODEL, n), F32)
    tile_a = jnp.concatenate([w_dt, zc(ROPE_LANE0 - 2 * SSD_HEADS), w_kr,
                              zc(LANE - ROPE_LANE0 - MLA_ROPE_DIM)], axis=1)
    w_in_pad = jnp.concatenate([w_z, w_xbc, w_cq, w_ckv, tile_a], axis=1).astype(BF16)

    w_uq = p["w_uq"][i].reshape(MLA_Q_RANK, MLA_HEADS, MLA_NOPE_DIM + MLA_ROPE_DIM)
    q_nope, q_rope = w_uq[..., :MLA_NOPE_DIM], w_uq[..., MLA_NOPE_DIM:]
    zq = lambda n: jnp.zeros((MLA_Q_RANK, MLA_HEADS, n), F32)
    pad = HEAD_PAD - MLA_NOPE_DIM - MLA_ROPE_DIM
    w_q = jnp.concatenate([q_nope, q_rope, zq(pad)], axis=-1).reshape(MLA_Q_RANK, -1)

    w_ukv = p["w_ukv"][i].reshape(MLA_KV_RANK, MLA_HEADS, MLA_NOPE_DIM + MLA_V_DIM)
    k_nope, v_w = w_ukv[..., :MLA_NOPE_DIM], w_ukv[..., MLA_NOPE_DIM:]
    w_uk = jnp.concatenate([k_nope, jnp.zeros((MLA_KV_RANK, MLA_HEADS, HEAD_PAD - MLA_NOPE_DIM), F32)],
                           axis=-1).reshape(MLA_KV_RANK, -1)
    w_uv = v_w.reshape(MLA_KV_RANK, MLA_WIDTH)

    dtb = p["dt_bias"][i].reshape(2 * SSD_HEADS)
    alog = p["a_log"][i].reshape(2 * SSD_HEADS)
    padl = lambda v: jnp.pad(v, (0, LANE - v.shape[0])).reshape(1, LANE)
    w_out = p["w_out"][i]
    return {
        "g_pre1": p["norm_pre_mix"][i].reshape(1, D_MODEL),
        "g_post1": p["norm_post_mix"][i].reshape(1, D_MODEL),
        "g_pre2": p["norm_pre_ffn"][i].reshape(1, D_MODEL),
        "g_post2": p["norm_post_ffn"][i].reshape(1, D_MODEL),
        "w_in": w_in_pad,
        "w_dt": w_dt.T.astype(BF16),
        "dtb_row": padl(dtb),
        "dtb_col": dtb.reshape(-1, 1),
        "alog_row": padl(alog),
        "alog_col": alog.reshape(-1, 1),
        "q_norm": p["q_norm"][i].reshape(1, -1),
        "w_q": w_q.T.astype(BF16),
        "kv_norm": p["kv_norm"][i].reshape(1, -1),
        "w_uk": w_uk.astype(BF16),
        "w_uv": w_uv.T.astype(BF16),
        "conv_w": p["conv_w"][i],
        "conv_b": p["conv_b"][i].reshape(1, -1),
        "dskip_row": jnp.repeat(p["d_skip"][i], SSD_HEAD_DIM).reshape(1, -1),
        "ssd_norm": p["ssd_norm"][i].reshape(1, -1),
        "mla_norm_col": p["mla_norm"][i].reshape(-1, 1),
        "w_out_a": w_out[:SSD_WIDTH].astype(BF16),
        "w_out_b": w_out[SSD_WIDTH:].astype(BF16),
    }


def _rope_tables(n_tokens):
    rows = n_tokens // GRID_W
    row = np.repeat(np.arange(rows, dtype=np.float32), GRID_W)
    col = np.tile(np.arange(GRID_W, dtype=np.float32), rows)
    half = MLA_ROPE_DIM // 2
    inv = (np.float32(ROPE_THETA) ** (-np.arange(0, half, 2, dtype=np.float32) / np.float32(half))).astype(np.float32)
    ar = row[:, None] * inv[None, :]
    ac = col[:, None] * inv[None, :]
    ang = np.concatenate([ar, ar, ac, ac], axis=-1).astype(np.float32)
    return jnp.asarray(np.cos(ang), F32), jnp.asarray(np.sin(ang), F32)


def _attn_tables(cos, sin, n):
    scale = (MLA_NOPE_DIM + MLA_ROPE_DIM) ** -0.5 * math.log2(math.e)
    pad = HEAD_PAD - ROPE_LANE0 - MLA_ROPE_DIM
    cosk = jnp.concatenate([jnp.zeros((n, ROPE_LANE0), F32), cos, jnp.zeros((n, pad), F32)], axis=1)
    sink = jnp.concatenate([jnp.zeros((n, ROPE_LANE0), F32), sin, jnp.zeros((n, pad), F32)], axis=1)
    cosq = jnp.concatenate([jnp.ones((n, ROPE_LANE0), F32), cos, jnp.zeros((n, pad), F32)], axis=1)
    return (cosq * scale).T, (sink * scale).T, cosk, sink


def kernel(x_prompt, x_sample, cache_ckv, cache_krope, state_ssm, c, c_ctx, w_mod, b_mod, norm_pre_mix, norm_post_mix, norm_pre_ffn, norm_post_ffn, w_in, conv_w, conv_b, dt_bias, a_log, d_skip, ssd_norm, q_norm, w_uq, kv_norm, w_ukv, mla_norm, w_out, ffn_w_gate, ffn_w_up, ffn_w_down, moe_router, moe_w_gate, moe_w_up, moe_w_down):
    params = dict(w_in=w_in, conv_w=conv_w, conv_b=conv_b, dt_bias=dt_bias, a_log=a_log, d_skip=d_skip,
                  ssd_norm=ssd_norm, q_norm=q_norm, w_uq=w_uq, kv_norm=kv_norm, w_ukv=w_ukv,
                  mla_norm=mla_norm, w_out=w_out, norm_pre_mix=norm_pre_mix, norm_post_mix=norm_post_mix,
                  norm_pre_ffn=norm_pre_ffn, norm_post_ffn=norm_post_ffn)
    batch, seq, d = x_prompt.shape
    dec_batch, dec_seq, _ = x_sample.shape
    depth = w_in.shape[0]
    past = cache_ckv.shape[2]
    tm = ROW_TILE

    cvec = jnp.concatenate([c_ctx[None, :], c, jnp.zeros((SUBLANE - 1 - dec_batch, d), F32)], axis=0)
    mod = _modulation(cvec, w_mod, b_mod)

    ones = jnp.ones((tm, MLA_ROPE_DIM), F32)
    tabs_ctx = _attn_tables(ones, jnp.zeros_like(ones), tm)
    cos, sin = _rope_tables(dec_seq)
    tabs_lat = _attn_tables(cos, sin, dec_seq)
    lat_blocks = dec_seq // tm

    xp = x_prompt.reshape(batch * seq, d)
    xs = x_sample.reshape(dec_batch * dec_seq, d)
    cache_bufs, ssm_buf = None, None
    for i in range(depth):
        lw = _prep_layer(i, params)
        mod3 = mod[i].reshape(SUBLANE, 1, N_MOD * d)
        j = i // 2
        if i % 2 == 0:
            f = ffn_w_gate.shape[2] // 2
            wgu = jnp.stack([jnp.concatenate([ffn_w_gate[j][:, s * f:(s + 1) * f], ffn_w_up[j][:, s * f:(s + 1) * f]],
                                             axis=1) for s in range(2)], axis=0).astype(BF16)
            wd = ffn_w_down[j].reshape(2, f, d).astype(BF16)
            router = None
        else:
            wgu = (moe_w_gate[j].astype(BF16), moe_w_up[j].astype(BF16))
            wd = moe_w_down[j].astype(BF16)
            router = jnp.pad(moe_router[j], ((0, 0), (0, LANE - N_EXPERTS)))

        def run(x, row_fn, tabs, tab_blocks, seq_len, heads_per_step, ctx):
            cache_out = (i, depth, seq_len, cache_bufs) if ctx is None else None
            z, xbc, dt, dtt, qt, k, vt, *caches = _inproj(x, mod3, row_fn, lw, tabs, tab_blocks, cache_out)
            cps = seq_len // SSD_CHUNK
            h0 = None
            if ctx is not None:
                h0 = (_state_to_kernel_layout(ctx[2][:, 0]), _state_to_kernel_layout(ctx[2][:, 1]))
            sink = (2 * i, 2 * depth, ssm_buf) if ctx is None else (0, 2, None)
            yssd, fin = _ssd(xbc, dt, dtt, z, lw, cps, h0=h0, sink=sink)
            cache = None
            if ctx is not None:
                kr_tile = jnp.pad(ctx[1].reshape(-1, MLA_ROPE_DIM),
                                  ((0, 0), (ROPE_LANE0, HEAD_PAD - ROPE_LANE0 - MLA_ROPE_DIM)))
                cache = _kvcache(ctx[0].reshape(-1, MLA_KV_RANK), kr_tile, lw)
            ot = _attention(qt, k, vt, seq_len, heads_per_step, cache=cache)
            outs = _outproj(yssd, ot, x, mod3, row_fn, lw, router=router)
            x1, h2 = outs[0], outs[1]
            comb = outs[2] if router is not None else None
            if comb is None:
                x2 = _ffn(h2, x1, mod3, row_fn, wgu, wd, lw["g_post2"])
            else:
                x2 = _moe(h2, x1, comb, mod3, row_fn, *wgu, wd, lw["g_post2"])
            return x2, caches, fin

        xp, cache_bufs, ssm_buf = run(xp, lambda tile: (lambda b: 0), tabs_ctx, 1, seq, MLA_HEADS, None)
        xs, _, _ = run(xs, lambda tile: (lambda b: 1 + (b * tile) // dec_seq), tabs_lat, lat_blocks, dec_seq, MLA_HEADS,
                             (cache_ckv[:, i], cache_krope[:, i], state_ssm[:, i]))
    return (xp.reshape(batch, seq, d), xs.reshape(dec_batch, dec_seq, d),
            cache_bufs[0], cache_bufs[1],
            ssm_buf.reshape(batch, depth, 2, SSD_HEADS, SSD_HEAD_DIM, SSD_STATE))
```

```python
import functools
import math

import jax
import jax.numpy as jnp
import numpy as np
from jax import lax
from jax.experimental import pallas as pl
from jax.experimental.pallas import tpu as pltpu

F32 = jnp.float32
BF16 = jnp.bfloat16

D_MODEL = 1024
GRID_W = 64
SSD_WIDTH = 512
SSD_HEAD_DIM = 64
SSD_HEADS = 8
SSD_GROUPS = 2
SSD_STATE = 64
SSD_CONV = 5
SSD_CHUNK = 128
SSD_STEP_CHUNKS = 8
SSD_XBC = SSD_WIDTH + 2 * SSD_GROUPS * SSD_STATE
MLA_WIDTH = 512
MLA_V_DIM = 64
MLA_HEADS = 8
MLA_NOPE_DIM = 64
MLA_ROPE_DIM = 32
MLA_Q_RANK = 384
MLA_KV_RANK = 256
ROPE_THETA = 10000.0
N_EXPERTS = 8
N_MOD = 6
EPS = 1e-6

LANE = 128
SUBLANE = 8
HEAD_PAD = 128
ONES_ROWS = 16
ROPE_LANE0 = MLA_NOPE_DIM
C_Z = 0
C_XBC = C_Z + SSD_WIDTH
C_CQ = C_XBC + SSD_XBC
C_CKV = C_CQ + MLA_Q_RANK
C_TA = C_CKV + MLA_KV_RANK
IN_PAD = C_TA + LANE
ROT_GROUP = MLA_ROPE_DIM // 4

VMEM_LIMIT = 56 * 1024 * 1024

ROW_TILE = 1024
ATTN_Q_TILE = 256
ATTN_KEY_BLOCK = 512
ATTN_SEQS_PER_STEP = 4
FFN_ROW_TILE = 512
FFN_SLABS = 1
MOE_ROW_TILE = 1024
MOE_SUB_TILE = 512
MOE_CAPS = (128, 160, 192, 224, 256)
MOE_SPARE_SLOTS = 2
MOD_COL_TILE = 1536
KVCACHE_ROW_TILE = 512

NT_DIMS = (((1,), (1,)), ((), ()))
TN_DIMS = (((0,), (0,)), ((), ()))


def _cparams(*sem):
    return pltpu.CompilerParams(dimension_semantics=sem, vmem_limit_bytes=VMEM_LIMIT)


def _silu(x):
    return x / (1.0 + jnp.exp(-x))


def _softplus(x):
    return jnp.maximum(x, 0.0) + jnp.log(1.0 + jnp.exp(-jnp.abs(x)))


def _rms_rows(x, g):
    ms = jnp.mean(x * x, axis=-1, keepdims=True)
    return x * lax.rsqrt(ms + EPS) * g


def _dot(a, b):
    return jnp.dot(a, b, preferred_element_type=F32)


def _dot_nt(a, b):
    return lax.dot_general(a, b, NT_DIMS, preferred_element_type=F32)


def _dot_tn(a, b):
    return lax.dot_general(a, b, TN_DIMS, preferred_element_type=F32)


def _split3(x):
    hi = x.astype(BF16)
    r1 = x - hi.astype(F32)
    mid = r1.astype(BF16)
    lo = (r1 - mid.astype(F32)).astype(BF16)
    return hi, mid, lo


def _mod_kernel(c_ref, w_ref, b_ref, o_ref):
    s = _silu(c_ref[...]).astype(BF16)
    o_ref[0] = _dot(s, w_ref[0].astype(BF16)) + b_ref[0]


def _modulation(cvec, w_mod, b_mod):
    depth, d, n = w_mod.shape
    tn = MOD_COL_TILE
    return pl.pallas_call(
        _mod_kernel,
        grid=(depth, n // tn),
        in_specs=[
            pl.BlockSpec((SUBLANE, d), lambda l, j: (0, 0)),
            pl.BlockSpec((1, d, tn), lambda l, j: (l, 0, j)),
            pl.BlockSpec((1, 1, tn), lambda l, j: (l, 0, j)),
        ],
        out_specs=pl.BlockSpec((1, SUBLANE, tn), lambda l, j: (l, 0, j)),
        out_shape=jax.ShapeDtypeStruct((depth, SUBLANE, n), F32),
        compiler_params=_cparams("arbitrary", "arbitrary"),
        name="modulation",
    )(cvec, w_mod, b_mod.reshape(depth, 1, n))


def _inproj_kernel(*refs, emit_cache, n_alias):
    (x_ref, mod_ref, gpre_ref, win_ref, wdt_ref, dtb_row_ref, dtb_col_ref, qn_ref, wq_ref, kvn_ref, wuk_ref,
     wuv_ref, cosq_ref, sinq_ref, cosk_ref, sink_ref) = refs[:16]
    outs = refs[16 + n_alias:]
    z_ref, xbc_ref, dt_ref, dtt_ref, qt_ref, k_ref, vt_ref = outs[:7]
    mod = mod_ref[0]
    shift = mod[:, 0:D_MODEL]
    scale = mod[:, D_MODEL:2 * D_MODEL]
    h = _rms_rows(x_ref[...], gpre_ref[...]) * (1.0 + scale) + shift
    hb = h.astype(BF16)
    proj = _dot(hb, win_ref[...])
    z_ref[...] = proj[:, C_Z:C_XBC]
    xbc_ref[...] = proj[:, C_XBC:C_CQ]
    cqn = _rms_rows(proj[:, C_CQ:C_CKV], qn_ref[...]).astype(BF16)
    ckvn = _rms_rows(proj[:, C_CKV:C_TA], kvn_ref[...])
    if emit_cache:
        ckvn_ref, kr_ref = outs[7:]
        nb, _, sq, _ = ckvn_ref.shape
        ckvn_ref[:, 0] = ckvn.reshape(nb, sq, MLA_KV_RANK)
    ckvb = ckvn.astype(BF16)
    ta = proj[:, C_TA:IN_PAD]
    dt_ref[...] = _softplus(ta + dtb_row_ref[...])
    if emit_cache:
        kr_ref[:, 0] = ta[:, ROPE_LANE0:ROPE_LANE0 + MLA_ROPE_DIM].reshape(nb, sq, MLA_ROPE_DIM)
    lane = lax.broadcasted_iota(jnp.int32, ta.shape, 1)
    first = (lane // ROT_GROUP) % 2 == 0
    rot = jnp.where(first, -pltpu.roll(ta, LANE - ROT_GROUP, 1), pltpu.roll(ta, ROT_GROUP, 1))
    kr_rot = ta * cosk_ref[...] + rot * sink_ref[...]
    knp = _dot(ckvb, wuk_ref[...])
    for hd in range(MLA_HEADS):
        sl = slice(hd * HEAD_PAD, (hd + 1) * HEAD_PAD)
        k_ref[:, sl] = (knp[:, sl] + kr_rot).astype(BF16)
    vt_ref[...] = _dot_nt(wuv_ref[...], ckvb).astype(BF16)
    qt = _dot_nt(wq_ref[...], cqn)
    cosq = cosq_ref[...]
    sinq = sinq_ref[...]
    g, r0 = ROT_GROUP, ROPE_LANE0
    for hd in range(MLA_HEADS):
        blk = qt[hd * HEAD_PAD:(hd + 1) * HEAD_PAD, :]
        rot = jnp.concatenate([blk[:r0], -blk[r0 + g:r0 + 2 * g], blk[r0:r0 + g], -blk[r0 + 3 * g:r0 + 4 * g],
                               blk[r0 + 2 * g:r0 + 3 * g], blk[r0 + 4 * g:]], axis=0)
        qt_ref[hd * HEAD_PAD:(hd + 1) * HEAD_PAD, :] = (blk * cosq + rot * sinq).astype(BF16)
    dtt_ref[...] = _softplus(_dot_nt(wdt_ref[...], hb) + dtb_col_ref[...])


def _inproj(x, mod3, mod_row_fn, lw, tabs, tab_blocks, cache_out=None):
    t = x.shape[0]
    tm = ROW_TILE
    nb = t // tm
    cosq, sinq, cosk, sink = tabs
    ntab = tab_blocks
    const = lambda i: (0, 0)
    row = lambda i: (i, 0)
    col = lambda i: (0, i)
    in_specs = [
        pl.BlockSpec((tm, D_MODEL), row),
        pl.BlockSpec((1, 1, N_MOD * D_MODEL), lambda i: (mod_row_fn(ROW_TILE)(i), 0, 0)),
        pl.BlockSpec((1, D_MODEL), const),
        pl.BlockSpec((D_MODEL, IN_PAD), const),
        pl.BlockSpec((2 * SSD_HEADS, D_MODEL), const),
        pl.BlockSpec((1, LANE), const),
        pl.BlockSpec((2 * SSD_HEADS, 1), const),
        pl.BlockSpec((1, MLA_Q_RANK), const),
        pl.BlockSpec((MLA_HEADS * HEAD_PAD, MLA_Q_RANK), const),
        pl.BlockSpec((1, MLA_KV_RANK), const),
        pl.BlockSpec((MLA_KV_RANK, MLA_HEADS * HEAD_PAD), const),
        pl.BlockSpec((MLA_WIDTH, MLA_KV_RANK), const),
        pl.BlockSpec((HEAD_PAD, tm), lambda i: (0, i % ntab)),
        pl.BlockSpec((HEAD_PAD, tm), lambda i: (0, i % ntab)),
        pl.BlockSpec((tm, LANE), lambda i: (i % ntab, 0)),
        pl.BlockSpec((tm, LANE), lambda i: (i % ntab, 0)),
    ]
    out_specs = [
        pl.BlockSpec((tm, SSD_WIDTH), row),
        pl.BlockSpec((tm, SSD_XBC), row),
        pl.BlockSpec((tm, LANE), row),
        pl.BlockSpec((2 * SSD_HEADS, tm), col),
        pl.BlockSpec((MLA_HEADS * HEAD_PAD, tm), col),
        pl.BlockSpec((tm, MLA_HEADS * HEAD_PAD), row),
        pl.BlockSpec((MLA_WIDTH, tm), col),
    ]
    out_shape = [
        jax.ShapeDtypeStruct((t, SSD_WIDTH), F32),
        jax.ShapeDtypeStruct((t, SSD_XBC), F32),
        jax.ShapeDtypeStruct((t, LANE), F32),
        jax.ShapeDtypeStruct((2 * SSD_HEADS, t), F32),
        jax.ShapeDtypeStruct((MLA_HEADS * HEAD_PAD, t), BF16),
        jax.ShapeDtypeStruct((t, MLA_HEADS * HEAD_PAD), BF16),
        jax.ShapeDtypeStruct((MLA_WIDTH, t), BF16),
    ]
    args = [x, mod3, lw["g_pre1"], lw["w_in"], lw["w_dt"], lw["dtb_row"], lw["dtb_col"],
            lw["q_norm"], lw["w_q"], lw["kv_norm"], lw["w_uk"], lw["w_uv"], cosq, sinq, cosk, sink]
    aliases = {}
    n_alias = 0
    if cache_out is not None:
        layer, depth, seq, bufs = cache_out
        nb_seq = tm // seq
        for rank in (MLA_KV_RANK, MLA_ROPE_DIM):
            out_specs.append(pl.BlockSpec((nb_seq, 1, seq, rank), lambda i: (i, layer, 0, 0)))
            out_shape.append(jax.ShapeDtypeStruct((t // seq, depth, seq, rank), F32))
        if bufs is not None:
            n_alias = len(bufs)
            for j, buf in enumerate(bufs):
                aliases[len(args)] = len(out_shape) - n_alias + j
                in_specs.append(pl.BlockSpec(memory_space=pl.ANY))
                args.append(buf)
    return pl.pallas_call(
        functools.partial(_inproj_kernel, emit_cache=cache_out is not None, n_alias=n_alias),
        grid=(nb,),
        in_specs=in_specs,
        out_specs=out_specs,
        out_shape=out_shape,
        input_output_aliases=aliases,
        compiler_params=_cparams("arbitrary"),
        name="inproj",
    )(*args)


def _kvcache_kernel(ckv_ref, kr_ref, wuk_ref, wuv_ref, k_ref, vt_ref):
    ckvb = ckv_ref[...].astype(BF16)
    knp = _dot(ckvb, wuk_ref[...])
    kr = kr_ref[...]
    for hd in range(MLA_HEADS):
        sl = slice(hd * HEAD_PAD, (hd + 1) * HEAD_PAD)
        k_ref[:, sl] = (knp[:, sl] + kr).astype(BF16)
    vt_ref[...] = _dot_nt(wuv_ref[...], ckvb).astype(BF16)


def _kvcache(ckv, kr_tile, lw):
    n = ckv.shape[0]
    tm = KVCACHE_ROW_TILE
    return pl.pallas_call(
        _kvcache_kernel,
        grid=(n // tm,),
        in_specs=[
            pl.BlockSpec((tm, MLA_KV_RANK), lambda i: (i, 0)),
            pl.BlockSpec((tm, LANE), lambda i: (i, 0)),
            pl.BlockSpec((MLA_KV_RANK, MLA_HEADS * HEAD_PAD), lambda i: (0, 0)),
            pl.BlockSpec((MLA_WIDTH, MLA_KV_RANK), lambda i: (0, 0)),
        ],
        out_specs=[
            pl.BlockSpec((tm, MLA_HEADS * HEAD_PAD), lambda i: (i, 0)),
            pl.BlockSpec((MLA_WIDTH, tm), lambda i: (0, i)),
        ],
        out_shape=[
            jax.ShapeDtypeStruct((n, MLA_HEADS * HEAD_PAD), BF16),
            jax.ShapeDtypeStruct((MLA_WIDTH, n), BF16),
        ],
        compiler_params=_cparams("arbitrary"),
        name="kvcache",
    )(ckv, kr_tile, lw["w_uk"], lw["w_uv"])


def _head_expand_matrix():
    r = lax.broadcasted_iota(jnp.int32, (LANE, 2 * SSD_WIDTH), 0)
    c = lax.broadcasted_iota(jnp.int32, (LANE, 2 * SSD_WIDTH), 1)
    return jnp.where(c // SSD_HEAD_DIM == r, 1.0, 0.0).astype(BF16)


def _expand_heads(v, emat):
    hi = v.astype(BF16)
    mid = (v - hi.astype(F32)).astype(BF16)
    return _dot(hi, emat) + _dot(mid, emat)


def _prefix_rows(dta, tril):
    return sum(_dot(tril, p) for p in _split3(dta))


def _chunk_masks(q):
    r_i = lax.broadcasted_iota(jnp.int32, (q, q), 0)
    c_i = lax.broadcasted_iota(jnp.int32, (q, q), 1)
    return r_i >= c_i, r_i <= c_i


def _ssd_state_kernel(*refs, cps, nc, nsq, has_h0, has_sink):
    it = iter(refs)
    xbc_ref = next(it)
    prev_ref, next_ref = (next(it), next(it)) if nsq == 0 else (None, None)
    dt_ref, cw_ref, cb_ref, alog_row_ref = (next(it) for _ in range(4))
    h0_ref = next(it) if has_h0 else None
    if has_sink:
        next(it)
    xcb_ref, hsf_ref, sb_ref, dec_ref, hfin_ref, st_ref = (next(it) for _ in range(6))

    q = SSD_CHUNK
    rows = nc * q
    cw = cw_ref[...]

    def conv_silu(ext, n):
        acc = cb_ref[...] + ext[SUBLANE - 2:SUBLANE - 2 + n] * cw[0:1]
        for k in range(1, SSD_CONV):
            o = SUBLANE - 2 + k
            acc = acc + ext[o:o + n] * cw[k:k + 1]
        return _silu(acc).astype(BF16)

    if nsq == 0:
        pos = (pl.program_id(0) * nc) % cps
        seq_first = pos == 0
        seq_last = pos + nc == cps
        prev = jnp.where(seq_first, 0.0, prev_ref[...])
        nxt = jnp.where(seq_last, 0.0, next_ref[...])
        xcb_all = conv_silu(jnp.concatenate([prev, xbc_ref[...], nxt], axis=0), rows)
    else:
        ln = cps * q
        pad = jnp.zeros((SUBLANE, SSD_XBC), F32)
        xcb_all = jnp.concatenate(
            [conv_silu(jnp.concatenate([pad, xbc_ref[sq * ln:(sq + 1) * ln, :], pad], axis=0), ln)
             for sq in range(nsq)], axis=0)
    xcb_ref[...] = xcb_all

    lower, _ = _chunk_masks(q)
    tril = jnp.where(lower, 1.0, 0.0).astype(BF16)
    a_row = -jnp.exp(alog_row_ref[...])
    lane = lax.broadcasted_iota(jnp.int32, (q, LANE), 1)
    lane_t = lax.broadcasted_iota(jnp.int32, (2 * SUBLANE, LANE), 1)
    emat = _head_expand_matrix()
    gw = SSD_WIDTH // SSD_GROUPS

    if nsq == 0:
        @pl.when(seq_first)
        def _():
            if has_h0:
                st_ref[...] = h0_ref[0]
            else:
                st_ref[...] = jnp.zeros_like(st_ref)

    chunk_dec, chunk_states = [], []
    for c in range(nc):
        sl = slice(c * q, (c + 1) * q)
        xcb = xcb_all[sl]
        xs = xcb[:, :SSD_WIDTH].astype(F32)
        dt = dt_ref[sl, :]
        dta = dt * a_row
        la = _prefix_rows(dta, tril)
        tot = la[q - 1:q, :]
        w = jnp.exp(jnp.where(lane < SSD_HEADS, tot - la, la - dta)) * dt
        w = jnp.where(lane < 2 * SSD_HEADS, w, 0.0)
        wexp = _expand_heads(w, emat)
        etot = jnp.where(lane_t < 2 * SSD_HEADS, jnp.exp(jnp.broadcast_to(tot, (2 * SUBLANE, LANE))), 0.0)
        dec = _expand_heads(etot, emat)[:SUBLANE]
        dec_ref[c] = dec
        bmb = xcb[:, SSD_WIDTH:SSD_WIDTH + SSD_GROUPS * SSD_STATE]
        states = []
        for d in range(2):
            xw = (xs * wexp[:, d * SSD_WIDTH:(d + 1) * SSD_WIDTH]).astype(BF16)
            parts = [_dot_tn(bmb[:, grp * SSD_STATE:(grp + 1) * SSD_STATE], xw[:, grp * gw:(grp + 1) * gw])
                     for grp in range(SSD_GROUPS)]
            states.append(jnp.concatenate(parts, axis=1))
        sb_ref[c] = states[1]
        chunk_dec.append(dec[0:1, :SSD_WIDTH])
        chunk_states.append(states[0])

    if nsq == 0:
        hs = st_ref[...]
        for c in range(nc):
            hsf_ref[c] = hs.astype(BF16)
            hs = hs * chunk_dec[c] + chunk_states[c]
        st_ref[...] = hs

        @pl.when(seq_last)
        def _():
            hfin_ref[0, 0] = hs.T
    else:
        for sq in range(nsq):
            hs = h0_ref[sq] if has_h0 else jnp.zeros(st_ref.shape, F32)
            for c in range(sq * cps, (sq + 1) * cps):
                hsf_ref[c] = hs.astype(BF16)
                hs = hs * chunk_dec[c] + chunk_states[c]
            hfin_ref[sq, 0] = hs.T


def _ssd_out_kernel(*refs, cps, nc, nsq, has_h0, has_sink, nsteps):
    it = iter(refs)
    (xcb_ref, dt_ref, dtt_ref, z_ref, hsf_ref, sb_ref, dec_ref,
     alog_row_ref, alog_col_ref, dskip_ref, gn_ref) = (next(it) for _ in range(11))
    h0_ref = next(it) if has_h0 else None
    if has_sink:
        next(it)
    y_ref, hfin_ref, st_ref = next(it), next(it), next(it)

    q = SSD_CHUNK
    pos = ((nsteps - 1 - pl.program_id(0)) * nc) % cps
    seq_first = pos == 0
    seq_last = pos + nc == cps
    log2e = math.log2(math.e)

    lower, upper = _chunk_masks(q)
    tril = jnp.where(lower, 1.0, 0.0).astype(BF16)
    triu = jnp.where(upper, 1.0, 0.0).astype(BF16)
    a_row = -jnp.exp(alog_row_ref[...])
    a_col = -jnp.exp(alog_col_ref[...])
    lane = lax.broadcasted_iota(jnp.int32, (q, LANE), 1)
    rowi = lax.broadcasted_iota(jnp.int32, (2 * SUBLANE, q), 0)
    emat = _head_expand_matrix()
    rep = SSD_HEADS // SSD_GROUPS
    gw = SSD_WIDTH // SSD_GROUPS
    neg = jnp.float32(-jnp.inf)

    if nsq == 0:
        @pl.when(seq_last)
        def _():
            if has_h0:
                st_ref[...] = h0_ref[0]
            else:
                st_ref[...] = jnp.zeros_like(st_ref)

    def chunk_terms(c):
        sl = slice(c * q, (c + 1) * q)
        xcb = xcb_ref[sl, :]
        xsb = xcb[:, :SSD_WIDTH]
        bmb = xcb[:, SSD_WIDTH:SSD_WIDTH + SSD_GROUPS * SSD_STATE]
        cmb = xcb[:, SSD_WIDTH + SSD_GROUPS * SSD_STATE:]
        dt = dt_ref[sl, :]
        dtt = dtt_ref[:, sl]
        dta = dt * a_row
        dtat = dtt * a_col
        la = _prefix_rows(dta, tril)
        tot = la[q - 1:q, :]
        lcol = jnp.where(lane < SSD_HEADS, la, tot - la + dta)
        lat = sum(_dot(p, triu) for p in _split3(dtat))
        tott = lat[:, q - 1:q]
        lrow = jnp.where(rowi < SSD_HEADS, lat, tott - lat + dtat)
        lcol2 = lcol * log2e
        lrow2 = (lrow - jnp.log(dtt)) * log2e
        ecol = jnp.where(lane < 2 * SSD_HEADS, jnp.exp(lcol), 0.0)
        eexp = _expand_heads(ecol, emat)

        cbs = []
        for grp in range(SSD_GROUPS):
            cg = cmb[:, grp * SSD_STATE:(grp + 1) * SSD_STATE]
            bg = bmb[:, grp * SSD_STATE:(grp + 1) * SSD_STATE]
            cbs.append(_dot_nt(cg, bg))
        tiles = []
        for pair in range(SSD_HEADS // 2):
            xpair = xsb[:, pair * LANE:(pair + 1) * LANE]
            res = []
            for hd in (2 * pair, 2 * pair + 1):
                jf, jb = hd, SSD_HEADS + hd
                ef = jnp.exp2(jnp.where(lower, lcol2[:, jf:jf + 1] - lrow2[jf:jf + 1, :], neg))
                eb = jnp.exp2(jnp.where(upper, lcol2[:, jb:jb + 1] - lrow2[jb:jb + 1, :], neg))
                mm = (cbs[hd // rep] * (ef + eb)).astype(BF16)
                res.append(_dot(mm, xpair))
            tiles.append(jnp.where(lane < SSD_HEAD_DIM, res[0], res[1]))
        y = jnp.concatenate(tiles, axis=1)
        hsf = hsf_ref[c]
        parts = [_dot(cmb[:, grp * SSD_STATE:(grp + 1) * SSD_STATE], hsf[:, grp * gw:(grp + 1) * gw])
                 for grp in range(SSD_GROUPS)]
        y = y + jnp.concatenate(parts, axis=1) * eexp[:, :SSD_WIDTH]
        y = y + dskip_ref[...] * xsb.astype(F32)
        return y, cmb, eexp[:, SSD_WIDTH:], _silu(z_ref[sl, :])

    terms = [chunk_terms(c) for c in range(nc)]

    def finish_chunk(c, hb):
        y, cmb, eexp_b, gate = terms[c]
        hsb = hb.astype(BF16)
        parts = [_dot(cmb[:, grp * SSD_STATE:(grp + 1) * SSD_STATE], hsb[:, grp * gw:(grp + 1) * gw])
                 for grp in range(SSD_GROUPS)]
        y = (y + jnp.concatenate(parts, axis=1) * eexp_b) * gate
        y_ref[c * q:(c + 1) * q, :] = _rms_rows(y, gn_ref[...]).astype(BF16)
        return hb * dec_ref[c][0:1, SSD_WIDTH:] + sb_ref[c]

    if nsq == 0:
        hb = st_ref[...]
        for c in reversed(range(nc)):
            hb = finish_chunk(c, hb)
        st_ref[...] = hb

        @pl.when(seq_first)
        def _():
            hfin_ref[0, 0] = hb.T
    else:
        for sq in range(nsq):
            hb = h0_ref[sq] if has_h0 else jnp.zeros(st_ref.shape, F32)
            for c in reversed(range(sq * cps, (sq + 1) * cps)):
                hb = finish_chunk(c, hb)
            hfin_ref[sq, 0] = hb.T


def _ssd(xbc, dt, dtt, z, lw, cps, h0=None, sink=(0, 2, None)):
    slot0, nslots, sink_buf = sink
    t = xbc.shape[0]
    q = SSD_CHUNK
    nchunks = t // q
    nseq = nchunks // cps
    nc = SSD_STEP_CHUNKS
    if cps >= nc:
        assert cps % nc == 0
        nsq, spq, sps = 0, cps // nc, 1
    else:
        assert nc % cps == 0 and nseq % (nc // cps) == 0
        nsq, spq = nc // cps, 1
        sps = nsq
    rows = nc * q
    nsteps = nchunks // nc
    hb = rows // SUBLANE
    n8 = t // SUBLANE
    has_h0 = h0 is not None
    const = lambda i: (0, 0)
    st_block = (sps, SSD_STATE, SSD_WIDTH)
    ch_block = (nc, SSD_STATE, SSD_WIDTH)
    fin_block = (sps, 1, SSD_WIDTH, SSD_STATE)
    fin_shape = jax.ShapeDtypeStruct((nseq, nslots, SSD_WIDTH, SSD_STATE), F32)
    any_spec = pl.BlockSpec(memory_space=pl.ANY)
    dec_block = (nc, SUBLANE, 2 * SSD_WIDTH)

    in_specs = [pl.BlockSpec((rows, SSD_XBC), lambda i: (i, 0))]
    args = [xbc]
    if nsq == 0:
        in_specs += [
            pl.BlockSpec((SUBLANE, SSD_XBC), lambda i: (jnp.maximum(i * hb - 1, 0), 0)),
            pl.BlockSpec((SUBLANE, SSD_XBC), lambda i: (jnp.minimum((i + 1) * hb, n8 - 1), 0)),
        ]
        args += [xbc, xbc]
    in_specs += [
        pl.BlockSpec((rows, LANE), lambda i: (i, 0)),
        pl.BlockSpec((SSD_CONV, SSD_XBC), const),
        pl.BlockSpec((1, SSD_XBC), const),
        pl.BlockSpec((1, LANE), const),
    ]
    args += [dt, lw["conv_w"], lw["conv_b"], lw["alog_row"]]
    if has_h0:
        in_specs.append(pl.BlockSpec(st_block, lambda i: (i // spq, 0, 0)))
        args.append(h0[0])
    aliases = {}
    if sink_buf is not None:
        aliases[len(args)] = 4
        in_specs.append(any_spec)
        args.append(sink_buf)
    xcb, hsf, sb, dec, fin = pl.pallas_call(
        functools.partial(_ssd_state_kernel, cps=cps, nc=nc, nsq=nsq, has_h0=has_h0, has_sink=sink_buf is not None),
        grid=(nsteps,),
        in_specs=in_specs,
        out_specs=[
            pl.BlockSpec((rows, SSD_XBC), lambda i: (i, 0)),
            pl.BlockSpec(ch_block, lambda i: (i, 0, 0)),
            pl.BlockSpec(ch_block, lambda i: (i, 0, 0)),
            pl.BlockSpec(dec_block, lambda i: (i, 0, 0)),
            pl.BlockSpec(fin_block, lambda i: (i // spq, slot0, 0, 0)),
        ],
        out_shape=[
            jax.ShapeDtypeStruct((t, SSD_XBC), BF16),
            jax.ShapeDtypeStruct((nchunks, SSD_STATE, SSD_WIDTH), BF16),
            jax.ShapeDtypeStruct((nchunks, SSD_STATE, SSD_WIDTH), F32),
            jax.ShapeDtypeStruct((nchunks, SUBLANE, 2 * SSD_WIDTH), F32),
            fin_shape,
        ],
        input_output_aliases=aliases,
        scratch_shapes=[pltpu.VMEM((SSD_STATE, SSD_WIDTH), F32)],
        compiler_params=_cparams("arbitrary"),
        name="ssd_state",
    )(*args)

    gi = lambda i: nsteps - 1 - i
    in_specs = [
        pl.BlockSpec((rows, SSD_XBC), lambda i: (gi(i), 0)),
        pl.BlockSpec((rows, LANE), lambda i: (gi(i), 0)),
        pl.BlockSpec((2 * SSD_HEADS, rows), lambda i: (0, gi(i))),
        pl.BlockSpec((rows, SSD_WIDTH), lambda i: (gi(i), 0)),
        pl.BlockSpec(ch_block, lambda i: (gi(i), 0, 0)),
        pl.BlockSpec(ch_block, lambda i: (gi(i), 0, 0)),
        pl.BlockSpec(dec_block, lambda i: (gi(i), 0, 0)),
        pl.BlockSpec((1, LANE), const),
        pl.BlockSpec((2 * SSD_HEADS, 1), const),
        pl.BlockSpec((1, SSD_WIDTH), const),
        pl.BlockSpec((1, SSD_WIDTH), const),
    ]
    args = [xcb, dt, dtt, z, hsf, sb, dec, lw["alog_row"], lw["alog_col"], lw["dskip_row"], lw["ssd_norm"]]
    if has_h0:
        in_specs.append(pl.BlockSpec(st_block, lambda i: (gi(i) // spq, 0, 0)))
        args.append(h0[1])
    in_specs.append(any_spec)
    args.append(fin)
    y, fin = pl.pallas_call(
        functools.partial(_ssd_out_kernel, cps=cps, nc=nc, nsq=nsq, has_h0=has_h0, has_sink=True, nsteps=nsteps),
        grid=(nsteps,),
        in_specs=in_specs,
        out_specs=[
            pl.BlockSpec((rows, SSD_WIDTH), lambda i: (gi(i), 0)),
            pl.BlockSpec(fin_block, lambda i: (gi(i) // spq, slot0 + 1, 0, 0)),
        ],
        out_shape=[jax.ShapeDtypeStruct((t, SSD_WIDTH), BF16), fin_shape],
        input_output_aliases={len(args) - 1: 1},
        scratch_shapes=[pltpu.VMEM((SSD_STATE, SSD_WIDTH), F32)],
        compiler_params=_cparams("arbitrary"),
        name="ssd_out",
    )(*args)
    return y, fin


def _state_to_kernel_layout(h):
    n = h.shape[0]
    return h.transpose(0, 3, 1, 2).reshape(n, SSD_STATE, SSD_WIDTH)


def _attn_kernel(*refs, heads, has_cache, nseq_step):
    if has_cache:
        qt_ref, k_ref, vt_ref, kc_ref, vct_ref = refs[:5]
    else:
        qt_ref, k_ref, vt_ref = refs[:3]
    n_in = 5 if has_cache else 3
    o_ref = refs[n_in]
    scratch = refs[n_in + 1:]
    tq = qt_ref.shape[1] // nseq_step
    lk = k_ref.shape[0] // nseq_step
    kb = min(ATTN_KEY_BLOCK, lk)
    ones = jnp.ones((ONES_ROWS, kb), BF16)

    for sq in range(nseq_step):
        s_refs = scratch[2 * sq:2 * sq + 2]
        qcols = slice(sq * tq, (sq + 1) * tq)
        blocks = [(k_ref, vt_ref, sq * lk + i * kb) for i in range(lk // kb)]
        if has_cache:
            assert nseq_step == 1
            lc = kc_ref.shape[0]
            assert min(ATTN_KEY_BLOCK, lc) == kb
            blocks += [(kc_ref, vct_ref, i * kb) for i in range(lc // kb)]
        nblk = len(blocks)

        def score_block(hd, i, m, blocks=blocks, s_refs=s_refs, qcols=qcols):
            kr, _, off = blocks[i]
            q = qt_ref[hd * HEAD_PAD:(hd + 1) * HEAD_PAD, qcols]
            s = _dot(kr[off:off + kb, hd * HEAD_PAD:(hd + 1) * HEAD_PAD], q)
            s_refs[hd % 2][i * kb:(i + 1) * kb, :] = s
            bm = jnp.max(s, axis=0, keepdims=True)
            return bm if m is None else jnp.maximum(m, bm)

        def value_block(hd, i, m, acc, blocks=blocks, s_refs=s_refs):
            _, vr, off = blocks[i]
            p = jnp.exp2((s_refs[hd % 2][i * kb:(i + 1) * kb, :] - m).astype(BF16))
            v = vr[hd * MLA_V_DIM:(hd + 1) * MLA_V_DIM, off:off + kb]
            part = _dot(jnp.concatenate([v, ones], axis=0), p)
            return part if acc is None else acc + part

        m_cur = None
        for i in range(nblk):
            m_cur = score_block(0, i, m_cur)
        for hd in range(heads):
            m_next, acc = None, None
            for i in range(nblk):
                if nblk == 1 and hd + 1 < heads:
                    m_next = score_block(hd + 1, i, m_next)
                acc = value_block(hd, i, m_cur, acc)
                if nblk > 1 and hd + 1 < heads:
                    m_next = score_block(hd + 1, i, m_next)
            vs = slice(hd * MLA_V_DIM, (hd + 1) * MLA_V_DIM)
            o_ref[vs, qcols] = acc[:MLA_V_DIM] / acc[MLA_V_DIM:MLA_V_DIM + 1]
            m_cur = m_next


def _attention(qt, k, vt, seq_len, heads_per_step, cache=None):
    t = k.shape[0]
    nseq = t // seq_len
    tq = min(ATTN_Q_TILE, seq_len)
    nq = seq_len // tq
    g = heads_per_step
    nss = ATTN_SEQS_PER_STEP if (nq == 1 and cache is None and nseq % ATTN_SEQS_PER_STEP == 0) else 1
    in_specs = [
        pl.BlockSpec((g * HEAD_PAD, nss * tq), lambda s, h, j: (h, s * nq + j)),
        pl.BlockSpec((nss * seq_len, g * HEAD_PAD), lambda s, h, j: (s, h)),
        pl.BlockSpec((g * MLA_V_DIM, nss * seq_len), lambda s, h, j: (h, s)),
    ]
    args = [qt, k, vt]
    n_keys = seq_len
    if cache is not None:
        kc, vct = cache
        past = kc.shape[0] // nseq
        n_keys += past
        in_specs += [
            pl.BlockSpec((past, g * HEAD_PAD), lambda s, h, j: (s, h)),
            pl.BlockSpec((g * MLA_V_DIM, past), lambda s, h, j: (h, s)),
        ]
        args += [kc, vct]
    kern = functools.partial(_attn_kernel, heads=g, has_cache=cache is not None, nseq_step=nss)
    return pl.pallas_call(
        kern,
        grid=(nseq // nss, MLA_HEADS // g, nq),
        in_specs=in_specs,
        out_specs=pl.BlockSpec((g * MLA_V_DIM, nss * tq), lambda s, h, j: (h, s * nq + j)),
        out_shape=jax.ShapeDtypeStruct((MLA_WIDTH, t), F32),
        scratch_shapes=[pltpu.VMEM((n_keys, tq), F32) for _ in range(2 * nss)],
        compiler_params=_cparams("arbitrary", "arbitrary", "arbitrary"),
        name="attention",
    )(*args)


def _outproj_kernel(*refs, has_router):
    if has_router:
        (y_ref, ot_ref, x_ref, mod_ref, wa_ref, wb_ref, gm_ref, gpost_ref, gpre2_ref, rt_ref,
         x1_ref, h2_ref, comb_ref) = refs
    else:
        (y_ref, ot_ref, x_ref, mod_ref, wa_ref, wb_ref, gm_ref, gpost_ref, gpre2_ref,
         x1_ref, h2_ref) = refs
    mod = mod_ref[0]
    gate1 = mod[:, 2 * D_MODEL:3 * D_MODEL]
    shift2 = mod[:, 3 * D_MODEL:4 * D_MODEL]
    scale2 = mod[:, 4 * D_MODEL:5 * D_MODEL]
    ot = ot_ref[...]
    ms = jnp.mean(ot * ot, axis=0, keepdims=True)
    on = (ot * lax.rsqrt(ms + EPS) * gm_ref[...]).astype(BF16)
    y = _dot(y_ref[...], wa_ref[...]) + _dot_tn(on, wb_ref[...])
    x1 = x_ref[...] + gate1 * _rms_rows(y, gpost_ref[...])
    x1_ref[...] = x1
    h2 = _rms_rows(x1, gpre2_ref[...]) * (1.0 + scale2) + shift2
    h2_ref[...] = h2.astype(BF16)
    if has_router:
        hh, hm, _ = _split3(h2)
        rh, rm, _ = _split3(rt_ref[...])
        logits = _dot(hh, rh) + (_dot(hm, rh) + _dot(hh, rm))
        lane = lax.broadcasted_iota(jnp.int32, logits.shape, 1).astype(F32)
        neg = jnp.float32(-jnp.inf)
        lg = jnp.where(lane < N_EXPERTS, logits, neg)
        m1 = jnp.max(lg, axis=-1, keepdims=True)
        i1 = jnp.min(jnp.where(lg == m1, lane, float(LANE)), axis=-1, keepdims=True)
        lg2 = jnp.where(lane == i1, neg, lg)
        m2 = jnp.max(lg2, axis=-1, keepdims=True)
        i2 = jnp.min(jnp.where(lg2 == m2, lane, float(LANE)), axis=-1, keepdims=True)
        e2 = jnp.exp(m2 - m1)
        w1 = 1.0 / (1.0 + e2)
        w2 = e2 / (1.0 + e2)
        comb_ref[...] = jnp.where(lane == i1, w1, 0.0) + jnp.where(lane == i2, w2, 0.0)


def _outproj(yssd, ot, x, mod3, mod_row_fn, lw, router=None):
    t = x.shape[0]
    tm = ROW_TILE
    const = lambda i: (0, 0)
    row = lambda i: (i, 0)
    in_specs = [
        pl.BlockSpec((tm, SSD_WIDTH), row),
        pl.BlockSpec((MLA_WIDTH, tm), lambda i: (0, i)),
        pl.BlockSpec((tm, D_MODEL), row),
        pl.BlockSpec((1, 1, N_MOD * D_MODEL), lambda i: (mod_row_fn(ROW_TILE)(i), 0, 0)),
        pl.BlockSpec((SSD_WIDTH, D_MODEL), const),
        pl.BlockSpec((MLA_WIDTH, D_MODEL), const),
        pl.BlockSpec((MLA_WIDTH, 1), const),
        pl.BlockSpec((1, D_MODEL), const),
        pl.BlockSpec((1, D_MODEL), const),
    ]
    args = [yssd, ot, x, mod3, lw["w_out_a"], lw["w_out_b"], lw["mla_norm_col"],
            lw["g_post1"], lw["g_pre2"]]
    out_specs = [pl.BlockSpec((tm, D_MODEL), row), pl.BlockSpec((tm, D_MODEL), row)]
    out_shape = [jax.ShapeDtypeStruct((t, D_MODEL), F32), jax.ShapeDtypeStruct((t, D_MODEL), BF16)]
    if router is not None:
        in_specs.append(pl.BlockSpec((D_MODEL, LANE), const))
        args.append(router)
        out_specs.append(pl.BlockSpec((tm, LANE), row))
        out_shape.append(jax.ShapeDtypeStruct((t, LANE), F32))
    return pl.pallas_call(
        functools.partial(_outproj_kernel, has_router=router is not None),
        grid=(t // tm,),
        in_specs=in_specs,
        out_specs=out_specs,
        out_shape=out_shape,
        compiler_params=_cparams("arbitrary"),
        name="outproj",
    )(*args)


def _ffn_kernel(h_ref, x_ref, mod_ref, wgu_ref, wd_ref, gpost_ref, o_ref, acc_ref, *, nslab):
    e = pl.program_id(1)
    h = h_ref[...]
    f = wd_ref.shape[1]
    gu = _dot(h, wgu_ref[0])
    hid = _silu(gu[:, :f]) * gu[:, f:]
    part = _dot(hid.astype(BF16), wd_ref[0])

    @pl.when(e == 0)
    def _():
        acc_ref[...] = part

    @pl.when(e > 0)
    def _():
        acc_ref[...] += part

    @pl.when(e == nslab - 1)
    def _():
        gate2 = mod_ref[0][:, 5 * D_MODEL:6 * D_MODEL]
        o_ref[...] = x_ref[...] + gate2 * _rms_rows(acc_ref[...], gpost_ref[...])


def _ffn(h2, x1, mod3, mod_row_fn, wgu, wd, gpost):
    t = x1.shape[0]
    tm = FFN_ROW_TILE
    nslab, f, _ = wd.shape
    row = lambda i, e: (i, 0)
    in_specs = [
        pl.BlockSpec((tm, D_MODEL), row),
        pl.BlockSpec((tm, D_MODEL), row),
        pl.BlockSpec((1, 1, N_MOD * D_MODEL), lambda i, e: (mod_row_fn(tm)(i), 0, 0)),
        pl.BlockSpec((1, D_MODEL, 2 * f), lambda i, e: (e, 0, 0)),
        pl.BlockSpec((1, f, D_MODEL), lambda i, e: (e, 0, 0)),
        pl.BlockSpec((1, D_MODEL), lambda i, e: (0, 0)),
    ]
    args = [h2, x1, mod3, wgu, wd, gpost]
    return pl.pallas_call(
        functools.partial(_ffn_kernel, nslab=nslab),
        grid=(t // tm, nslab),
        in_specs=in_specs,
        out_specs=pl.BlockSpec((tm, D_MODEL), row),
        out_shape=jax.ShapeDtypeStruct((t, D_MODEL), F32),
        scratch_shapes=[pltpu.VMEM((tm, D_MODEL), F32)],
        compiler_params=_cparams("arbitrary", "arbitrary"),
        name="ffn",
    )(*args)


def _moe_kernel(h_ref, x_ref, comb_ref, mod_ref, wg_ref, wu_ref, wd_ref, gpost_ref, o_ref,
                acc_ref, rank_ref, rank_t_ref, comb_t_ref, *, nexp, sub):
    e = pl.program_id(1)
    tm = h_ref.shape[0]
    caps = MOE_CAPS
    cmax = caps[-1]

    @pl.when(e == 0)
    def _():
        acc_ref[...] = jnp.zeros_like(acc_ref)
        r_i = lax.broadcasted_iota(jnp.int32, (sub, sub), 0)
        c_i = lax.broadcasted_iota(jnp.int32, (sub, sub), 1)
        strict = jnp.where(r_i > c_i, 1.0, 0.0).astype(BF16)
        for s in range(tm // sub):
            rows = slice(s * sub, (s + 1) * sub)
            comb = comb_ref[rows, :]
            rank = _dot(strict, jnp.where(comb > 0.0, 1.0, 0.0).astype(BF16))
            rank_ref[rows, :] = rank
            rank_t_ref[:, rows] = rank.T
            comb_t_ref[:, rows] = comb.T

    def expert_pass(rows, wcol, rcol, wrow, rrow, base, cap):
        capl = -(-cap // LANE) * LANE
        slot_l = lax.broadcasted_iota(jnp.int32, (sub, capl), 1).astype(F32)
        slot_s = lax.broadcasted_iota(jnp.int32, (cap, sub), 0).astype(F32)
        gather = jnp.where(((rrow - base) == slot_s) & (wrow > 0.0), 1.0, 0.0).astype(BF16)
        scatter = jnp.where(((rcol - base) == slot_l) & (wcol > 0.0) & (slot_l < float(cap)),
                            1.0, 0.0).astype(BF16)
        xg = _dot(gather, h_ref[rows, :]).astype(BF16)
        hid = _silu(_dot(xg, wg_ref[0])) * _dot(xg, wu_ref[0])
        y = _dot(hid.astype(BF16), wd_ref[0]).astype(BF16)
        if capl > cap:
            y = jnp.concatenate([y, jnp.zeros((capl - cap, y.shape[1]), BF16)], axis=0)
        acc_ref[rows, :] += wcol * _dot(scatter, y)

    lane = lax.broadcasted_iota(jnp.int32, (sub, LANE), 1)

    def sub_tile(s, carry):
        rows = pl.ds(pl.multiple_of(s * sub, sub), sub)
        pick = lane == e
        wcol = jnp.sum(jnp.where(pick, comb_ref[rows, :], 0.0), axis=-1, keepdims=True)
        rcol = jnp.sum(jnp.where(pick, rank_ref[rows, :], 0.0), axis=-1, keepdims=True)
        wrow = comb_t_ref[pl.ds(e, 1), rows]
        rrow = rank_t_ref[pl.ds(e, 1), rows]
        count = jnp.max(jnp.where(wrow > 0.0, rrow + 1.0, 0.0))
        want = count + float(MOE_SPARE_SLOTS)
        npass = ((want + (cmax - 1.0)) * (1.0 / cmax)).astype(jnp.int32)
        nfull = npass - 1

        def full_pass(k, c):
            expert_pass(rows, wcol, rcol, wrow, rrow, (k * cmax).astype(F32), cmax)
            return c

        lax.fori_loop(0, nfull, full_pass, 0)
        base = (nfull * cmax).astype(F32)
        left = want - base
        lo = 0
        for cap in caps:
            @pl.when(jnp.logical_and(left > float(lo), left <= float(cap)))
            def _(cap=cap):
                expert_pass(rows, wcol, rcol, wrow, rrow, base, cap)
            lo = cap
        return carry

    lax.fori_loop(0, tm // sub, sub_tile, 0)

    @pl.when(e == nexp - 1)
    def _():
        gate2 = mod_ref[0][:, 5 * D_MODEL:6 * D_MODEL]
        o_ref[...] = x_ref[...] + gate2 * _rms_rows(acc_ref[...], gpost_ref[...])


def _moe(h2, x1, comb, mod3, mod_row_fn, wg, wu, wd, gpost):
    t = x1.shape[0]
    tm = min(MOE_ROW_TILE, t)
    sub = min(MOE_SUB_TILE, tm)
    nexp, f, _ = wd.shape
    row = lambda i, e: (i, 0)
    return pl.pallas_call(
        functools.partial(_moe_kernel, nexp=nexp, sub=sub),
        grid=(t // tm, nexp),
        in_specs=[
            pl.BlockSpec((tm, D_MODEL), row),
            pl.BlockSpec((tm, D_MODEL), row),
            pl.BlockSpec((tm, LANE), row),
            pl.BlockSpec((1, 1, N_MOD * D_MODEL), lambda i, e: (mod_row_fn(tm)(i), 0, 0)),
            pl.BlockSpec((1, D_MODEL, f), lambda i, e: (e, 0, 0)),
            pl.BlockSpec((1, D_MODEL, f), lambda i, e: (e, 0, 0)),
            pl.BlockSpec((1, f, D_MODEL), lambda i, e: (e, 0, 0)),
            pl.BlockSpec((1, D_MODEL), lambda i, e: (0, 0)),
        ],
        out_specs=pl.BlockSpec((tm, D_MODEL), row),
        out_shape=jax.ShapeDtypeStruct((t, D_MODEL), F32),
        scratch_shapes=[pltpu.VMEM((tm, D_MODEL), F32), pltpu.VMEM((tm, LANE), F32),
                        pltpu.VMEM((LANE, tm), F32), pltpu.VMEM((LANE, tm), F32)],
        compiler_params=_cparams("arbitrary", "arbitrary"),
        name="moe",
    )(h2, x1, comb, mod3, wg, wu, wd, gpost)


def _prep_layer(i, p):
    w_in = p["w_in"][i]
    s1 = SSD_WIDTH
    s2 = s1 + SSD_XBC
    s3 = s2 + 2 * SSD_HEADS
    s4 = s3 + MLA_Q_RANK
    s5 = s4 + MLA_KV_RANK
    w_z, w_xbc, w_dt, w_cq, w_ckv, w_kr = (w_in[:, :s1], w_in[:, s1:s2], w_in[:, s2:s3],
                                             w_in[:, s3:s4], w_in[:, s4:s5], w_in[:, s5:])
    zc = lambda n: jnp.zeros((D_MODEL, n), F32)
    tile_a = jnp.concatenate([w_dt, zc(ROPE_LANE0 - 2 * SSD_HEADS), w_kr,
                              zc(LANE - ROPE_LANE0 - MLA_ROPE_DIM)], axis=1)
    w_in_pad = jnp.concatenate([w_z, w_xbc, w_cq, w_ckv, tile_a], axis=1).astype(BF16)

    w_uq = p["w_uq"][i].reshape(MLA_Q_RANK, MLA_HEADS, MLA_NOPE_DIM + MLA_ROPE_DIM)
    q_nope, q_rope = w_uq[..., :MLA_NOPE_DIM], w_uq[..., MLA_NOPE_DIM:]
    zq = lambda n: jnp.zeros((MLA_Q_RANK, MLA_HEADS, n), F32)
    pad = HEAD_PAD - MLA_NOPE_DIM - MLA_ROPE_DIM
    w_q = jnp.concatenate([q_nope, q_rope, zq(pad)], axis=-1).reshape(MLA_Q_RANK, -1)

    w_ukv = p["w_ukv"][i].reshape(MLA_KV_RANK, MLA_HEADS, MLA_NOPE_DIM + MLA_V_DIM)
    k_nope, v_w = w_ukv[..., :MLA_NOPE_DIM], w_ukv[..., MLA_NOPE_DIM:]
    w_uk = jnp.concatenate([k_nope, jnp.zeros((MLA_KV_RANK, MLA_HEADS, HEAD_PAD - MLA_NOPE_DIM), F32)],
                           axis=-1).reshape(MLA_KV_RANK, -1)
    w_uv = v_w.reshape(MLA_KV_RANK, MLA_WIDTH)

    dtb = p["dt_bias"][i].reshape(2 * SSD_HEADS)
    alog = p["a_log"][i].reshape(2 * SSD_HEADS)
    padl = lambda v: jnp.pad(v, (0, LANE - v.shape[0])).reshape(1, LANE)
    w_out = p["w_out"][i]
    return {
        "g_pre1": p["norm_pre_mix"][i].reshape(1, D_MODEL),
        "g_post1": p["norm_post_mix"][i].reshape(1, D_MODEL),
        "g_pre2": p["norm_pre_ffn"][i].reshape(1, D_MODEL),
        "g_post2": p["norm_post_ffn"][i].reshape(1, D_MODEL),
        "w_in": w_in_pad,
        "w_dt": w_dt.T.astype(BF16),
        "dtb_row": padl(dtb),
        "dtb_col": dtb.reshape(-1, 1),
        "alog_row": padl(alog),
        "alog_col": alog.reshape(-1, 1),
        "q_norm": p["q_norm"][i].reshape(1, -1),
        "w_q": w_q.T.astype(BF16),
        "kv_norm": p["kv_norm"][i].reshape(1, -1),
        "w_uk": w_uk.astype(BF16),
        "w_uv": w_uv.T.astype(BF16),
        "conv_w": p["conv_w"][i],
        "conv_b": p["conv_b"][i].reshape(1, -1),
        "dskip_row": jnp.repeat(p["d_skip"][i], SSD_HEAD_DIM).reshape(1, -1),
        "ssd_norm": p["ssd_norm"][i].reshape(1, -1),
        "mla_norm_col": p["mla_norm"][i].reshape(-1, 1),
        "w_out_a": w_out[:SSD_WIDTH].astype(BF16),
        "w_out_b": w_out[SSD_WIDTH:].astype(BF16),
    }


def _rope_tables(n_tokens):
    rows = n_tokens // GRID_W
    row = np.repeat(np.arange(rows, dtype=np.float32), GRID_W)
    col = np.tile(np.arange(GRID_W, dtype=np.float32), rows)
    half = MLA_ROPE_DIM // 2
    inv = (np.float32(ROPE_THETA) ** (-np.arange(0, half, 2, dtype=np.float32) / np.float32(half))).astype(np.float32)
    ar = row[:, None] * inv[None, :]
    ac = col[:, None] * inv[None, :]
    ang = np.concatenate([ar, ar, ac, ac], axis=-1).astype(np.float32)
    return jnp.asarray(np.cos(ang), F32), jnp.asarray(np.sin(ang), F32)


def _attn_tables(cos, sin, n):
    scale = (MLA_NOPE_DIM + MLA_ROPE_DIM) ** -0.5 * math.log2(math.e)
    pad = HEAD_PAD - ROPE_LANE0 - MLA_ROPE_DIM
    cosk = jnp.concatenate([jnp.zeros((n, ROPE_LANE0), F32), cos, jnp.zeros((n, pad), F32)], axis=1)
    sink = jnp.concatenate([jnp.zeros((n, ROPE_LANE0), F32), sin, jnp.zeros((n, pad), F32)], axis=1)
    cosq = jnp.concatenate([jnp.ones((n, ROPE_LANE0), F32), cos, jnp.zeros((n, pad), F32)], axis=1)
    return (cosq * scale).T, (sink * scale).T, cosk, sink


def kernel(x_prompt, x_sample, cache_ckv, cache_krope, state_ssm, c, c_ctx, w_mod, b_mod, norm_pre_mix, norm_post_mix, norm_pre_ffn, norm_post_ffn, w_in, conv_w, conv_b, dt_bias, a_log, d_skip, ssd_norm, q_norm, w_uq, kv_norm, w_ukv, mla_norm, w_out, ffn_w_gate, ffn_w_up, ffn_w_down, moe_router, moe_w_gate, moe_w_up, moe_w_down):
    params = dict(w_in=w_in, conv_w=conv_w, conv_b=conv_b, dt_bias=dt_bias, a_log=a_log, d_skip=d_skip,
                  ssd_norm=ssd_norm, q_norm=q_norm, w_uq=w_uq, kv_norm=kv_norm, w_ukv=w_ukv,
                  mla_norm=mla_norm, w_out=w_out, norm_pre_mix=norm_pre_mix, norm_post_mix=norm_post_mix,
                  norm_pre_ffn=norm_pre_ffn, norm_post_ffn=norm_post_ffn)
    batch, seq, d = x_prompt.shape
    dec_batch, dec_seq, _ = x_sample.shape
    depth = w_in.shape[0]
    past = cache_ckv.shape[2]
    tm = ROW_TILE

    cvec = jnp.concatenate([c_ctx[None, :], c, jnp.zeros((SUBLANE - 1 - dec_batch, d), F32)], axis=0)
    mod = _modulation(cvec, w_mod, b_mod)

    ones = jnp.ones((tm, MLA_ROPE_DIM), F32)
    tabs_ctx = _attn_tables(ones, jnp.zeros_like(ones), tm)
    cos, sin = _rope_tables(dec_seq)
    tabs_lat = _attn_tables(cos, sin, dec_seq)
    lat_blocks = dec_seq // tm

    xp = x_prompt.reshape(batch * seq, d)
    xs = x_sample.reshape(dec_batch * dec_seq, d)
    cache_bufs, ssm_buf = None, None
    for i in range(depth):
        lw = _prep_layer(i, params)
        mod3 = mod[i].reshape(SUBLANE, 1, N_MOD * d)
        j = i // 2
        if i % 2 == 0:
            f = ffn_w_gate.shape[2] // FFN_SLABS
            wgu = jnp.stack([jnp.concatenate([ffn_w_gate[j][:, s * f:(s + 1) * f], ffn_w_up[j][:, s * f:(s + 1) * f]],
                                             axis=1) for s in range(FFN_SLABS)], axis=0).astype(BF16)
            wd = ffn_w_down[j].reshape(FFN_SLABS, f, d).astype(BF16)
            router = None
        else:
            wgu = (moe_w_gate[j].astype(BF16), moe_w_up[j].astype(BF16))
            wd = moe_w_down[j].astype(BF16)
            router = jnp.pad(moe_router[j], ((0, 0), (0, LANE - N_EXPERTS)))

        def run(x, row_fn, tabs, tab_blocks, seq_len, heads_per_step, ctx):
            cache_out = (i, depth, seq_len, cache_bufs) if ctx is None else None
            z, xbc, dt, dtt, qt, k, vt, *caches = _inproj(x, mod3, row_fn, lw, tabs, tab_blocks, cache_out)
            cps = seq_len // SSD_CHUNK
            h0 = None
            if ctx is not None:
                h0 = (_state_to_kernel_layout(ctx[2][:, 0]), _state_to_kernel_layout(ctx[2][:, 1]))
            sink = (2 * i, 2 * depth, ssm_buf) if ctx is None else (0, 2, None)
            yssd, fin = _ssd(xbc, dt, dtt, z, lw, cps, h0=h0, sink=sink)
            cache = None
            if ctx is not None:
                kr_tile = jnp.pad(ctx[1].reshape(-1, MLA_ROPE_DIM),
                                  ((0, 0), (ROPE_LANE0, HEAD_PAD - ROPE_LANE0 - MLA_ROPE_DIM)))
                cache = _kvcache(ctx[0].reshape(-1, MLA_KV_RANK), kr_tile, lw)
            ot = _attention(qt, k, vt, seq_len, heads_per_step, cache=cache)
            outs = _outproj(yssd, ot, x, mod3, row_fn, lw, router=router)
            x1, h2 = outs[0], outs[1]
            comb = outs[2] if router is not None else None
            if comb is None:
                x2 = _ffn(h2, x1, mod3, row_fn, wgu, wd, lw["g_post2"])
            else:
                x2 = _moe(h2, x1, comb, mod3, row_fn, *wgu, wd, lw["g_post2"])
            return x2, caches, fin

        xp, cache_bufs, ssm_buf = run(xp, lambda tile: (lambda b: 0), tabs_ctx, 1, seq, MLA_HEADS, None)
        xs, _, _ = run(xs, lambda tile: (lambda b: 1 + (b * tile) // dec_seq), tabs_lat, lat_blocks, dec_seq, MLA_HEADS,
                             (cache_ckv[:, i], cache_krope[:, i], state_ssm[:, i]))
    return (xp.reshape(batch, seq, d), xs.reshape(dec_batch, dec_seq, d),
            cache_bufs[0], cache_bufs[1],
            ssm_buf.reshape(batch, depth, 2, SSD_HEADS, SSD_HEAD_DIM, SSD_STATE))
```

```python
import functools
import math

import jax
import jax.numpy as jnp
import numpy as np
from jax import lax
from jax.experimental import pallas as pl
from jax.experimental.pallas import tpu as pltpu

F32 = jnp.float32
BF16 = jnp.bfloat16

D_MODEL = 1024
GRID_W = 64
SSD_WIDTH = 512
SSD_HEAD_DIM = 64
SSD_HEADS = 8
SSD_GROUPS = 2
SSD_STATE = 64
SSD_CONV = 5
SSD_CHUNK = 128
SSD_STEP_CHUNKS = 8
SSD_XBC = SSD_WIDTH + 2 * SSD_GROUPS * SSD_STATE
MLA_WIDTH = 512
MLA_V_DIM = 64
MLA_HEADS = 8
MLA_NOPE_DIM = 64
MLA_ROPE_DIM = 32
MLA_Q_RANK = 384
MLA_KV_RANK = 256
ROPE_THETA = 10000.0
N_EXPERTS = 8
N_MOD = 6
EPS = 1e-6

LANE = 128
SUBLANE = 8
HEAD_PAD = 128
ONES_ROWS = 16
ROPE_LANE0 = MLA_NOPE_DIM
C_Z = 0
C_XBC = C_Z + SSD_WIDTH
C_CQ = C_XBC + SSD_XBC
C_CKV = C_CQ + MLA_Q_RANK
C_TA = C_CKV + MLA_KV_RANK
IN_PAD = C_TA + LANE
ROT_GROUP = MLA_ROPE_DIM // 4

VMEM_LIMIT = 56 * 1024 * 1024

ROW_TILE = 1024
ATTN_Q_TILE = 256
ATTN_KEY_BLOCK = 512
ATTN_QTILES_PER_STEP = 2
ATTN_SEQS_PER_STEP = 4
FFN_ROW_TILE = 512
FFN_SLABS = 1
MOE_ROW_TILE = 1024
MOE_SUB_TILE = 512
MOE_CAPS = (128, 160, 192, 224, 256)
MOE_SPARE_SLOTS = 2
MOD_COL_TILE = 1536
KVCACHE_ROW_TILE = 512

NT_DIMS = (((1,), (1,)), ((), ()))
TN_DIMS = (((0,), (0,)), ((), ()))


def _cparams(*sem):
    return pltpu.CompilerParams(dimension_semantics=sem, vmem_limit_bytes=VMEM_LIMIT)


def _silu(x):
    return x / (1.0 + jnp.exp(-x))


def _softplus(x):
    return jnp.maximum(x, 0.0) + jnp.log(1.0 + jnp.exp(-jnp.abs(x)))


def _rms_rows(x, g):
    ms = jnp.mean(x * x, axis=-1, keepdims=True)
    return x * lax.rsqrt(ms + EPS) * g


def _dot(a, b):
    return jnp.dot(a, b, preferred_element_type=F32)


def _dot_nt(a, b):
    return lax.dot_general(a, b, NT_DIMS, preferred_element_type=F32)


def _dot_tn(a, b):
    return lax.dot_general(a, b, TN_DIMS, preferred_element_type=F32)


def _split3(x):
    hi = x.astype(BF16)
    r1 = x - hi.astype(F32)
    mid = r1.astype(BF16)
    lo = (r1 - mid.astype(F32)).astype(BF16)
    return hi, mid, lo


def _mod_kernel(c_ref, w_ref, b_ref, o_ref):
    s = _silu(c_ref[...]).astype(BF16)
    o_ref[0] = _dot(s, w_ref[0].astype(BF16)) + b_ref[0]


def _modulation(cvec, w_mod, b_mod):
    depth, d, n = w_mod.shape
    tn = MOD_COL_TILE
    return pl.pallas_call(
        _mod_kernel,
        grid=(depth, n // tn),
        in_specs=[
            pl.BlockSpec((SUBLANE, d), lambda l, j: (0, 0)),
            pl.BlockSpec((1, d, tn), lambda l, j: (l, 0, j)),
            pl.BlockSpec((1, 1, tn), lambda l, j: (l, 0, j)),
        ],
        out_specs=pl.BlockSpec((1, SUBLANE, tn), lambda l, j: (l, 0, j)),
        out_shape=jax.ShapeDtypeStruct((depth, SUBLANE, n), F32),
        compiler_params=_cparams("arbitrary", "arbitrary"),
        name="modulation",
    )(cvec, w_mod, b_mod.reshape(depth, 1, n))


def _inproj_kernel(*refs, emit_cache, n_alias, cache_slot):
    (x_ref, mod_ref, gpre_ref, win_ref, wdt_ref, dtb_row_ref, dtb_col_ref, qn_ref, wq_ref, kvn_ref, wuk_ref,
     wuv_ref, cosq_ref, sinq_ref, cosk_ref, sink_ref) = refs[:16]
    outs = refs[16 + n_alias:]
    z_ref, xbc_ref, dt_ref, dtt_ref, qt_ref, k_ref, vt_ref = outs[:7]
    mod = mod_ref[0]
    shift = mod[:, 0:D_MODEL]
    scale = mod[:, D_MODEL:2 * D_MODEL]
    h = _rms_rows(x_ref[...], gpre_ref[...]) * (1.0 + scale) + shift
    hb = h.astype(BF16)
    proj = _dot(hb, win_ref[...])
    z_ref[...] = proj[:, C_Z:C_XBC]
    xbc_ref[...] = proj[:, C_XBC:C_CQ]
    cqn = _rms_rows(proj[:, C_CQ:C_CKV], qn_ref[...]).astype(BF16)
    ckvn = _rms_rows(proj[:, C_CKV:C_TA], kvn_ref[...])
    if emit_cache:
        ckvn_ref, kr_ref = outs[7:]
        nb, nslot, sq, _ = ckvn_ref.shape
        if nslot > 1:
            ckvn_ref[...] = jnp.zeros_like(ckvn_ref)
            kr_ref[...] = jnp.zeros_like(kr_ref)
        ckvn_ref[:, cache_slot] = ckvn.reshape(nb, sq, MLA_KV_RANK)
    ckvb = ckvn.astype(BF16)
    ta = proj[:, C_TA:IN_PAD]
    dt_ref[...] = _softplus(ta + dtb_row_ref[...])
    if emit_cache:
        kr_ref[:, cache_slot] = ta[:, ROPE_LANE0:ROPE_LANE0 + MLA_ROPE_DIM].reshape(nb, sq, MLA_ROPE_DIM)
    lane = lax.broadcasted_iota(jnp.int32, ta.shape, 1)
    first = (lane // ROT_GROUP) % 2 == 0
    rot = jnp.where(first, -pltpu.roll(ta, LANE - ROT_GROUP, 1), pltpu.roll(ta, ROT_GROUP, 1))
    kr_rot = ta * cosk_ref[...] + rot * sink_ref[...]
    knp = _dot(ckvb, wuk_ref[...])
    for hd in range(MLA_HEADS):
        sl = slice(hd * HEAD_PAD, (hd + 1) * HEAD_PAD)
        k_ref[:, sl] = (knp[:, sl] + kr_rot).astype(BF16)
    vt_ref[...] = _dot_nt(wuv_ref[...], ckvb).astype(BF16)
    qt = _dot_nt(wq_ref[...], cqn)
    cosq = cosq_ref[...]
    sinq = sinq_ref[...]
    g, r0 = ROT_GROUP, ROPE_LANE0
    for hd in range(MLA_HEADS):
        blk = qt[hd * HEAD_PAD:(hd + 1) * HEAD_PAD, :]
        rot = jnp.concatenate([blk[:r0], -blk[r0 + g:r0 + 2 * g], blk[r0:r0 + g], -blk[r0 + 3 * g:r0 + 4 * g],
                               blk[r0 + 2 * g:r0 + 3 * g], blk[r0 + 4 * g:]], axis=0)
        qt_ref[hd * HEAD_PAD:(hd + 1) * HEAD_PAD, :] = (blk * cosq + rot * sinq).astype(BF16)
    dtt_ref[...] = _softplus(_dot_nt(wdt_ref[...], hb) + dtb_col_ref[...])


def _inproj(x, mod3, mod_row_fn, lw, tabs, tab_blocks, cache_out=None):
    t = x.shape[0]
    tm = ROW_TILE
    nb = t // tm
    cosq, sinq, cosk, sink = tabs
    ntab = tab_blocks
    const = lambda i: (0, 0)
    row = lambda i: (i, 0)
    col = lambda i: (0, i)
    in_specs = [
        pl.BlockSpec((tm, D_MODEL), row),
        pl.BlockSpec((1, 1, N_MOD * D_MODEL), lambda i: (mod_row_fn(ROW_TILE)(i), 0, 0)),
        pl.BlockSpec((1, D_MODEL), const),
        pl.BlockSpec((D_MODEL, IN_PAD), const),
        pl.BlockSpec((2 * SSD_HEADS, D_MODEL), const),
        pl.BlockSpec((1, LANE), const),
        pl.BlockSpec((2 * SSD_HEADS, 1), const),
        pl.BlockSpec((1, MLA_Q_RANK), const),
        pl.BlockSpec((MLA_HEADS * HEAD_PAD, MLA_Q_RANK), const),
        pl.BlockSpec((1, MLA_KV_RANK), const),
        pl.BlockSpec((MLA_KV_RANK, MLA_HEADS * HEAD_PAD), const),
        pl.BlockSpec((MLA_WIDTH, MLA_KV_RANK), const),
        pl.BlockSpec((HEAD_PAD, tm), lambda i: (0, i % ntab)),
        pl.BlockSpec((HEAD_PAD, tm), lambda i: (0, i % ntab)),
        pl.BlockSpec((tm, LANE), lambda i: (i % ntab, 0)),
        pl.BlockSpec((tm, LANE), lambda i: (i % ntab, 0)),
    ]
    out_specs = [
        pl.BlockSpec((tm, SSD_WIDTH), row),
        pl.BlockSpec((tm, SSD_XBC), row),
        pl.BlockSpec((tm, LANE), row),
        pl.BlockSpec((2 * SSD_HEADS, tm), col),
        pl.BlockSpec((MLA_HEADS * HEAD_PAD, tm), col),
        pl.BlockSpec((tm, MLA_HEADS * HEAD_PAD), row),
        pl.BlockSpec((MLA_WIDTH, tm), col),
    ]
    out_shape = [
        jax.ShapeDtypeStruct((t, SSD_WIDTH), F32),
        jax.ShapeDtypeStruct((t, SSD_XBC), F32),
        jax.ShapeDtypeStruct((t, LANE), F32),
        jax.ShapeDtypeStruct((2 * SSD_HEADS, t), F32),
        jax.ShapeDtypeStruct((MLA_HEADS * HEAD_PAD, t), BF16),
        jax.ShapeDtypeStruct((t, MLA_HEADS * HEAD_PAD), BF16),
        jax.ShapeDtypeStruct((MLA_WIDTH, t), BF16),
    ]
    args = [x, mod3, lw["g_pre1"], lw["w_in"], lw["w_dt"], lw["dtb_row"], lw["dtb_col"],
            lw["q_norm"], lw["w_q"], lw["kv_norm"], lw["w_uk"], lw["w_uv"], cosq, sinq, cosk, sink]
    aliases = {}
    n_alias = 0
    cache_slot = 0
    if cache_out is not None:
        layer, depth, seq, bufs = cache_out
        nb_seq = tm // seq
        whole = bufs is None
        cache_slot = layer if whole else 0
        for rank in (MLA_KV_RANK, MLA_ROPE_DIM):
            out_specs.append(pl.BlockSpec((nb_seq, depth if whole else 1, seq, rank),
                                          lambda i: (i, 0 if whole else layer, 0, 0)))
            out_shape.append(jax.ShapeDtypeStruct((t // seq, depth, seq, rank), F32))
        if bufs is not None:
            n_alias = len(bufs)
            for j, buf in enumerate(bufs):
                aliases[len(args)] = len(out_shape) - n_alias + j
                in_specs.append(pl.BlockSpec(memory_space=pl.ANY))
                args.append(buf)
    return pl.pallas_call(
        functools.partial(_inproj_kernel, emit_cache=cache_out is not None, n_alias=n_alias, cache_slot=cache_slot),
        grid=(nb,),
        in_specs=in_specs,
        out_specs=out_specs,
        out_shape=out_shape,
        input_output_aliases=aliases,
        compiler_params=_cparams("arbitrary"),
        name="inproj",
    )(*args)


def _kvcache_kernel(ckv_ref, kr_ref, wuk_ref, wuv_ref, k_ref, vt_ref):
    ckvb = ckv_ref[...].astype(BF16)
    knp = _dot(ckvb, wuk_ref[...])
    kr = kr_ref[...]
    for hd in range(MLA_HEADS):
        sl = slice(hd * HEAD_PAD, (hd + 1) * HEAD_PAD)
        k_ref[:, sl] = (knp[:, sl] + kr).astype(BF16)
    vt_ref[...] = _dot_nt(wuv_ref[...], ckvb).astype(BF16)


def _kvcache(ckv, kr_tile, lw):
    n = ckv.shape[0]
    tm = KVCACHE_ROW_TILE
    return pl.pallas_call(
        _kvcache_kernel,
        grid=(n // tm,),
        in_specs=[
            pl.BlockSpec((tm, MLA_KV_RANK), lambda i: (i, 0)),
            pl.BlockSpec((tm, LANE), lambda i: (i, 0)),
            pl.BlockSpec((MLA_KV_RANK, MLA_HEADS * HEAD_PAD), lambda i: (0, 0)),
            pl.BlockSpec((MLA_WIDTH, MLA_KV_RANK), lambda i: (0, 0)),
        ],
        out_specs=[
            pl.BlockSpec((tm, MLA_HEADS * HEAD_PAD), lambda i: (i, 0)),
            pl.BlockSpec((MLA_WIDTH, tm), lambda i: (0, i)),
        ],
        out_shape=[
            jax.ShapeDtypeStruct((n, MLA_HEADS * HEAD_PAD), BF16),
            jax.ShapeDtypeStruct((MLA_WIDTH, n), BF16),
        ],
        compiler_params=_cparams("arbitrary"),
        name="kvcache",
    )(ckv, kr_tile, lw["w_uk"], lw["w_uv"])


def _head_expand_matrix():
    r = lax.broadcasted_iota(jnp.int32, (LANE, 2 * SSD_WIDTH), 0)
    c = lax.broadcasted_iota(jnp.int32, (LANE, 2 * SSD_WIDTH), 1)
    return jnp.where(c // SSD_HEAD_DIM == r, 1.0, 0.0).astype(BF16)


def _expand_heads(v, emat):
    hi = v.astype(BF16)
    mid = (v - hi.astype(F32)).astype(BF16)
    return _dot(hi, emat) + _dot(mid, emat)


def _prefix_rows(dta, tril):
    return sum(_dot(tril, p) for p in _split3(dta))


def _chunk_masks(q):
    r_i = lax.broadcasted_iota(jnp.int32, (q, q), 0)
    c_i = lax.broadcasted_iota(jnp.int32, (q, q), 1)
    return r_i >= c_i, r_i <= c_i


def _ssd_state_kernel(*refs, cps, nc, nsq, has_h0, has_sink, fin_slot):
    it = iter(refs)
    xbc_ref = next(it)
    prev_ref, next_ref = (next(it), next(it)) if nsq == 0 else (None, None)
    dt_ref, cw_ref, cb_ref, alog_row_ref = (next(it) for _ in range(4))
    h0_ref = next(it) if has_h0 else None
    if has_sink:
        next(it)
    xcb_ref, hsf_ref, sb_ref, dec_ref, hfin_ref, st_ref = (next(it) for _ in range(6))

    q = SSD_CHUNK
    rows = nc * q
    cw = cw_ref[...]

    def conv_silu(ext, n):
        acc = cb_ref[...] + ext[SUBLANE - 2:SUBLANE - 2 + n] * cw[0:1]
        for k in range(1, SSD_CONV):
            o = SUBLANE - 2 + k
            acc = acc + ext[o:o + n] * cw[k:k + 1]
        return _silu(acc).astype(BF16)

    if nsq == 0:
        pos = (pl.program_id(0) * nc) % cps
        seq_first = pos == 0
        seq_last = pos + nc == cps
        prev = jnp.where(seq_first, 0.0, prev_ref[...])
        nxt = jnp.where(seq_last, 0.0, next_ref[...])
        xcb_all = conv_silu(jnp.concatenate([prev, xbc_ref[...], nxt], axis=0), rows)
    else:
        ln = cps * q
        pad = jnp.zeros((SUBLANE, SSD_XBC), F32)
        xcb_all = jnp.concatenate(
            [conv_silu(jnp.concatenate([pad, xbc_ref[sq * ln:(sq + 1) * ln, :], pad], axis=0), ln)
             for sq in range(nsq)], axis=0)
    xcb_ref[...] = xcb_all

    lower, _ = _chunk_masks(q)
    tril = jnp.where(lower, 1.0, 0.0).astype(BF16)
    a_row = -jnp.exp(alog_row_ref[...])
    lane = lax.broadcasted_iota(jnp.int32, (q, LANE), 1)
    lane_t = lax.broadcasted_iota(jnp.int32, (2 * SUBLANE, LANE), 1)
    emat = _head_expand_matrix()
    gw = SSD_WIDTH // SSD_GROUPS

    if nsq == 0:
        @pl.when(seq_first)
        def _():
            if has_h0:
                st_ref[...] = h0_ref[0]
            else:
                st_ref[...] = jnp.zeros_like(st_ref)

    chunk_dec, chunk_states = [], []
    for c in range(nc):
        sl = slice(c * q, (c + 1) * q)
        xcb = xcb_all[sl]
        xs = xcb[:, :SSD_WIDTH].astype(F32)
        dt = dt_ref[sl, :]
        dta = dt * a_row
        la = _prefix_rows(dta, tril)
        tot = la[q - 1:q, :]
        w = jnp.exp(jnp.where(lane < SSD_HEADS, tot - la, la - dta)) * dt
        w = jnp.where(lane < 2 * SSD_HEADS, w, 0.0)
        wexp = _expand_heads(w, emat)
        etot = jnp.where(lane_t < 2 * SSD_HEADS, jnp.exp(jnp.broadcast_to(tot, (2 * SUBLANE, LANE))), 0.0)
        dec = _expand_heads(etot, emat)[:SUBLANE]
        dec_ref[c] = dec
        bmb = xcb[:, SSD_WIDTH:SSD_WIDTH + SSD_GROUPS * SSD_STATE]
        states = []
        for d in range(2):
            xw = (xs * wexp[:, d * SSD_WIDTH:(d + 1) * SSD_WIDTH]).astype(BF16)
            parts = [_dot_tn(bmb[:, grp * SSD_STATE:(grp + 1) * SSD_STATE], xw[:, grp * gw:(grp + 1) * gw])
                     for grp in range(SSD_GROUPS)]
            states.append(jnp.concatenate(parts, axis=1))
        sb_ref[c] = states[1]
        chunk_dec.append(dec[0:1, :SSD_WIDTH])
        chunk_states.append(states[0])

    if nsq == 0:
        hs = st_ref[...]
        for c in range(nc):
            hsf_ref[c] = hs.astype(BF16)
            hs = hs * chunk_dec[c] + chunk_states[c]
        st_ref[...] = hs

        @pl.when(seq_last)
        def _():
            if not has_sink:
                hfin_ref[...] = jnp.zeros_like(hfin_ref)
            hfin_ref[0, fin_slot] = hs.T
    else:
        for sq in range(nsq):
            hs = h0_ref[sq] if has_h0 else jnp.zeros(st_ref.shape, F32)
            for c in range(sq * cps, (sq + 1) * cps):
                hsf_ref[c] = hs.astype(BF16)
                hs = hs * chunk_dec[c] + chunk_states[c]
            if not has_sink:
                hfin_ref[sq] = jnp.zeros(hfin_ref.shape[1:], F32)
            hfin_ref[sq, fin_slot] = hs.T


def _ssd_out_kernel(*refs, cps, nc, nsq, has_h0, has_sink, nsteps):
    it = iter(refs)
    (xcb_ref, dt_ref, dtt_ref, z_ref, hsf_ref, sb_ref, dec_ref,
     alog_row_ref, alog_col_ref, dskip_ref, gn_ref) = (next(it) for _ in range(11))
    h0_ref = next(it) if has_h0 else None
    if has_sink:
        next(it)
    y_ref, hfin_ref, st_ref = next(it), next(it), next(it)

    q = SSD_CHUNK
    pos = ((nsteps - 1 - pl.program_id(0)) * nc) % cps
    seq_first = pos == 0
    seq_last = pos + nc == cps
    log2e = math.log2(math.e)

    lower, upper = _chunk_masks(q)
    tril = jnp.where(lower, 1.0, 0.0).astype(BF16)
    triu = jnp.where(upper, 1.0, 0.0).astype(BF16)
    a_row = -jnp.exp(alog_row_ref[...])
    a_col = -jnp.exp(alog_col_ref[...])
    lane = lax.broadcasted_iota(jnp.int32, (q, LANE), 1)
    rowi = lax.broadcasted_iota(jnp.int32, (2 * SUBLANE, q), 0)
    emat = _head_expand_matrix()
    rep = SSD_HEADS // SSD_GROUPS
    gw = SSD_WIDTH // SSD_GROUPS
    neg = jnp.float32(-jnp.inf)

    if nsq == 0:
        @pl.when(seq_last)
        def _():
            if has_h0:
                st_ref[...] = h0_ref[0]
            else:
                st_ref[...] = jnp.zeros_like(st_ref)

    def chunk_terms(c):
        sl = slice(c * q, (c + 1) * q)
        xcb = xcb_ref[sl, :]
        xsb = xcb[:, :SSD_WIDTH]
        bmb = xcb[:, SSD_WIDTH:SSD_WIDTH + SSD_GROUPS * SSD_STATE]
        cmb = xcb[:, SSD_WIDTH + SSD_GROUPS * SSD_STATE:]
        dt = dt_ref[sl, :]
        dtt = dtt_ref[:, sl]
        dta = dt * a_row
        dtat = dtt * a_col
        la = _prefix_rows(dta, tril)
        tot = la[q - 1:q, :]
        lcol = jnp.where(lane < SSD_HEADS, la, tot - la + dta)
        lat = sum(_dot(p, triu) for p in _split3(dtat))
        tott = lat[:, q - 1:q]
        lrow = jnp.where(rowi < SSD_HEADS, lat, tott - lat + dtat)
        lcol2 = lcol * log2e
        lrow2 = (lrow - jnp.log(dtt)) * log2e
        ecol = jnp.where(lane < 2 * SSD_HEADS, jnp.exp(lcol), 0.0)
        eexp = _expand_heads(ecol, emat)

        cbs = []
        for grp in range(SSD_GROUPS):
            cg = cmb[:, grp * SSD_STATE:(grp + 1) * SSD_STATE]
            bg = bmb[:, grp * SSD_STATE:(grp + 1) * SSD_STATE]
            cbs.append(_dot_nt(cg, bg))
        tiles = []
        for pair in range(SSD_HEADS // 2):
            xpair = xsb[:, pair * LANE:(pair + 1) * LANE]
            res = []
            for hd in (2 * pair, 2 * pair + 1):
                jf, jb = hd, SSD_HEADS + hd
                ef = jnp.exp2(jnp.where(lower, lcol2[:, jf:jf + 1] - lrow2[jf:jf + 1, :], neg))
                eb = jnp.exp2(jnp.where(upper, lcol2[:, jb:jb + 1] - lrow2[jb:jb + 1, :], neg))
                mm = (cbs[hd // rep] * (ef + eb)).astype(BF16)
                res.append(_dot(mm, xpair))
            tiles.append(jnp.where(lane < SSD_HEAD_DIM, res[0], res[1]))
        y = jnp.concatenate(tiles, axis=1)
        hsf = hsf_ref[c]
        parts = [_dot(cmb[:, grp * SSD_STATE:(grp + 1) * SSD_STATE], hsf[:, grp * gw:(grp + 1) * gw])
                 for grp in range(SSD_GROUPS)]
        y = y + jnp.concatenate(parts, axis=1) * eexp[:, :SSD_WIDTH]
        y = y + dskip_ref[...] * xsb.astype(F32)
        return y, cmb, eexp[:, SSD_WIDTH:], _silu(z_ref[sl, :])

    terms = [chunk_terms(c) for c in range(nc)]

    def finish_chunk(c, hb):
        y, cmb, eexp_b, gate = terms[c]
        hsb = hb.astype(BF16)
        parts = [_dot(cmb[:, grp * SSD_STATE:(grp + 1) * SSD_STATE], hsb[:, grp * gw:(grp + 1) * gw])
                 for grp in range(SSD_GROUPS)]
        y = (y + jnp.concatenate(parts, axis=1) * eexp_b) * gate
        y_ref[c * q:(c + 1) * q, :] = _rms_rows(y, gn_ref[...]).astype(BF16)
        return hb * dec_ref[c][0:1, SSD_WIDTH:] + sb_ref[c]

    if nsq == 0:
        hb = st_ref[...]
        for c in reversed(range(nc)):
            hb = finish_chunk(c, hb)
        st_ref[...] = hb

        @pl.when(seq_first)
        def _():
            hfin_ref[0, 0] = hb.T
    else:
        for sq in range(nsq):
            hb = h0_ref[sq] if has_h0 else jnp.zeros(st_ref.shape, F32)
            for c in reversed(range(sq * cps, (sq + 1) * cps)):
                hb = finish_chunk(c, hb)
            hfin_ref[sq, 0] = hb.T


def _ssd(xbc, dt, dtt, z, lw, cps, h0=None, sink=(0, 2, None)):
    slot0, nslots, sink_buf = sink
    t = xbc.shape[0]
    q = SSD_CHUNK
    nchunks = t // q
    nseq = nchunks // cps
    nc = SSD_STEP_CHUNKS
    if cps >= nc:
        assert cps % nc == 0
        nsq, spq, sps = 0, cps // nc, 1
    else:
        assert nc % cps == 0 and nseq % (nc // cps) == 0
        nsq, spq = nc // cps, 1
        sps = nsq
    rows = nc * q
    nsteps = nchunks // nc
    hb = rows // SUBLANE
    n8 = t // SUBLANE
    has_h0 = h0 is not None
    const = lambda i: (0, 0)
    st_block = (sps, SSD_STATE, SSD_WIDTH)
    ch_block = (nc, SSD_STATE, SSD_WIDTH)
    fin_block = (sps, 1, SSD_WIDTH, SSD_STATE)
    fin_shape = jax.ShapeDtypeStruct((nseq, nslots, SSD_WIDTH, SSD_STATE), F32)
    any_spec = pl.BlockSpec(memory_space=pl.ANY)
    dec_block = (nc, SUBLANE, 2 * SSD_WIDTH)

    in_specs = [pl.BlockSpec((rows, SSD_XBC), lambda i: (i, 0))]
    args = [xbc]
    if nsq == 0:
        in_specs += [
            pl.BlockSpec((SUBLANE, SSD_XBC), lambda i: (jnp.maximum(i * hb - 1, 0), 0)),
            pl.BlockSpec((SUBLANE, SSD_XBC), lambda i: (jnp.minimum((i + 1) * hb, n8 - 1), 0)),
        ]
        args += [xbc, xbc]
    in_specs += [
        pl.BlockSpec((rows, LANE), lambda i: (i, 0)),
        pl.BlockSpec((SSD_CONV, SSD_XBC), const),
        pl.BlockSpec((1, SSD_XBC), const),
        pl.BlockSpec((1, LANE), const),
    ]
    args += [dt, lw["conv_w"], lw["conv_b"], lw["alog_row"]]
    if has_h0:
        in_specs.append(pl.BlockSpec(st_block, lambda i: (i // spq, 0, 0)))
        args.append(h0[0])
    aliases = {}
    if sink_buf is not None:
        aliases[len(args)] = 4
        in_specs.append(any_spec)
        args.append(sink_buf)
    xcb, hsf, sb, dec, fin = pl.pallas_call(
        functools.partial(_ssd_state_kernel, cps=cps, nc=nc, nsq=nsq, has_h0=has_h0, has_sink=sink_buf is not None,
                          fin_slot=0 if sink_buf is not None else slot0),
        grid=(nsteps,),
        in_specs=in_specs,
        out_specs=[
            pl.BlockSpec((rows, SSD_XBC), lambda i: (i, 0)),
            pl.BlockSpec(ch_block, lambda i: (i, 0, 0)),
            pl.BlockSpec(ch_block, lambda i: (i, 0, 0)),
            pl.BlockSpec(dec_block, lambda i: (i, 0, 0)),
            pl.BlockSpec(fin_block if sink_buf is not None else (sps, nslots, SSD_WIDTH, SSD_STATE),
                         lambda i: (i // spq, slot0 if sink_buf is not None else 0, 0, 0)),
        ],
        out_shape=[
            jax.ShapeDtypeStruct((t, SSD_XBC), BF16),
            jax.ShapeDtypeStruct((nchunks, SSD_STATE, SSD_WIDTH), BF16),
            jax.ShapeDtypeStruct((nchunks, SSD_STATE, SSD_WIDTH), F32),
            jax.ShapeDtypeStruct((nchunks, SUBLANE, 2 * SSD_WIDTH), F32),
            fin_shape,
        ],
        input_output_aliases=aliases,
        scratch_shapes=[pltpu.VMEM((SSD_STATE, SSD_WIDTH), F32)],
        compiler_params=_cparams("arbitrary"),
        name="ssd_state",
    )(*args)

    gi = lambda i: nsteps - 1 - i
    in_specs = [
        pl.BlockSpec((rows, SSD_XBC), lambda i: (gi(i), 0)),
        pl.BlockSpec((rows, LANE), lambda i: (gi(i), 0)),
        pl.BlockSpec((2 * SSD_HEADS, rows), lambda i: (0, gi(i))),
        pl.BlockSpec((rows, SSD_WIDTH), lambda i: (gi(i), 0)),
        pl.BlockSpec(ch_block, lambda i: (gi(i), 0, 0)),
        pl.BlockSpec(ch_block, lambda i: (gi(i), 0, 0)),
        pl.BlockSpec(dec_block, lambda i: (gi(i), 0, 0)),
        pl.BlockSpec((1, LANE), const),
        pl.BlockSpec((2 * SSD_HEADS, 1), const),
        pl.BlockSpec((1, SSD_WIDTH), const),
        pl.BlockSpec((1, SSD_WIDTH), const),
    ]
    args = [xcb, dt, dtt, z, hsf, sb, dec, lw["alog_row"], lw["alog_col"], lw["dskip_row"], lw["ssd_norm"]]
    if has_h0:
        in_specs.append(pl.BlockSpec(st_block, lambda i: (gi(i) // spq, 0, 0)))
        args.append(h0[1])
    in_specs.append(any_spec)
    args.append(fin)
    y, fin = pl.pallas_call(
        functools.partial(_ssd_out_kernel, cps=cps, nc=nc, nsq=nsq, has_h0=has_h0, has_sink=True, nsteps=nsteps),
        grid=(nsteps,),
        in_specs=in_specs,
        out_specs=[
            pl.BlockSpec((rows, SSD_WIDTH), lambda i: (gi(i), 0)),
            pl.BlockSpec(fin_block, lambda i: (gi(i) // spq, slot0 + 1, 0, 0)),
        ],
        out_shape=[jax.ShapeDtypeStruct((t, SSD_WIDTH), BF16), fin_shape],
        input_output_aliases={len(args) - 1: 1},
        scratch_shapes=[pltpu.VMEM((SSD_STATE, SSD_WIDTH), F32)],
        compiler_params=_cparams("arbitrary"),
        name="ssd_out",
    )(*args)
    return y, fin


def _state_to_kernel_layout(h):
    n = h.shape[0]
    return h.transpose(0, 3, 1, 2).reshape(n, SSD_STATE, SSD_WIDTH)


def _attn_kernel(*refs, heads, has_cache, nseq_step, shared_keys):
    if has_cache:
        qt_ref, k_ref, vt_ref, kc_ref, vct_ref = refs[:5]
    else:
        qt_ref, k_ref, vt_ref = refs[:3]
    n_in = 5 if has_cache else 3
    o_ref = refs[n_in]
    scratch = refs[n_in + 1:]
    tq = qt_ref.shape[1] // nseq_step
    lk = k_ref.shape[0] if shared_keys else k_ref.shape[0] // nseq_step
    kb = min(ATTN_KEY_BLOCK, lk)
    ones = jnp.ones((ONES_ROWS, kb), BF16)

    for sq in range(nseq_step):
        s_refs = scratch[2 * sq:2 * sq + 2]
        qcols = slice(sq * tq, (sq + 1) * tq)
        blocks = [(k_ref, vt_ref, (0 if shared_keys else sq * lk) + i * kb) for i in range(lk // kb)]
        if has_cache:
            assert nseq_step == 1 or shared_keys
            lc = kc_ref.shape[0]
            assert min(ATTN_KEY_BLOCK, lc) == kb
            blocks += [(kc_ref, vct_ref, i * kb) for i in range(lc // kb)]
        nblk = len(blocks)

        def score_block(hd, i, m, blocks=blocks, s_refs=s_refs, qcols=qcols):
            kr, _, off = blocks[i]
            q = qt_ref[hd * HEAD_PAD:(hd + 1) * HEAD_PAD, qcols]
            s = _dot(kr[off:off + kb, hd * HEAD_PAD:(hd + 1) * HEAD_PAD], q)
            s_refs[hd % 2][i * kb:(i + 1) * kb, :] = s
            bm = jnp.max(s, axis=0, keepdims=True)
            return bm if m is None else jnp.maximum(m, bm)

        def value_block(hd, i, m, acc, blocks=blocks, s_refs=s_refs):
            _, vr, off = blocks[i]
            p = jnp.exp2((s_refs[hd % 2][i * kb:(i + 1) * kb, :] - m).astype(BF16))
            v = vr[hd * MLA_V_DIM:(hd + 1) * MLA_V_DIM, off:off + kb]
            part = _dot(jnp.concatenate([v, ones], axis=0), p)
            return part if acc is None else acc + part

        m_cur = None
        for i in range(nblk):
            m_cur = score_block(0, i, m_cur)
        for hd in range(heads):
            m_next, acc = None, None
            for i in range(nblk):
                if nblk == 1 and hd + 1 < heads:
                    m_next = score_block(hd + 1, i, m_next)
                acc = value_block(hd, i, m_cur, acc)
                if nblk > 1 and hd + 1 < heads:
                    m_next = score_block(hd + 1, i, m_next)
            vs = slice(hd * MLA_V_DIM, (hd + 1) * MLA_V_DIM)
            o_ref[vs, qcols] = acc[:MLA_V_DIM] / acc[MLA_V_DIM:MLA_V_DIM + 1]
            m_cur = m_next


def _attention(qt, k, vt, seq_len, heads_per_step, cache=None):
    t = k.shape[0]
    nseq = t // seq_len
    tq = min(ATTN_Q_TILE, seq_len)
    nq = seq_len // tq
    g = heads_per_step
    nss = ATTN_SEQS_PER_STEP if (nq == 1 and cache is None and nseq % ATTN_SEQS_PER_STEP == 0) else 1
    shared = nq > 1 and nq % ATTN_QTILES_PER_STEP == 0
    if shared:
        nss, nq = ATTN_QTILES_PER_STEP, nq // ATTN_QTILES_PER_STEP
    kseqs = 1 if shared else nss
    in_specs = [
        pl.BlockSpec((g * HEAD_PAD, nss * tq), lambda s, h, j: (h, s * nq + j)),
        pl.BlockSpec((kseqs * seq_len, g * HEAD_PAD), lambda s, h, j: (s, h)),
        pl.BlockSpec((g * MLA_V_DIM, kseqs * seq_len), lambda s, h, j: (h, s)),
    ]
    args = [qt, k, vt]
    n_keys = seq_len
    if cache is not None:
        kc, vct = cache
        past = kc.shape[0] // nseq
        n_keys += past
        in_specs += [
            pl.BlockSpec((past, g * HEAD_PAD), lambda s, h, j: (s, h)),
            pl.BlockSpec((g * MLA_V_DIM, past), lambda s, h, j: (h, s)),
        ]
        args += [kc, vct]
    kern = functools.partial(_attn_kernel, heads=g, has_cache=cache is not None, nseq_step=nss, shared_keys=shared)
    return pl.pallas_call(
        kern,
        grid=(nseq // kseqs, MLA_HEADS // g, nq),
        in_specs=in_specs,
        out_specs=pl.BlockSpec((g * MLA_V_DIM, nss * tq), lambda s, h, j: (h, s * nq + j)),
        out_shape=jax.ShapeDtypeStruct((MLA_WIDTH, t), F32),
        scratch_shapes=[pltpu.VMEM((n_keys, tq), F32) for _ in range(2 * nss)],
        compiler_params=_cparams("arbitrary", "arbitrary", "arbitrary"),
        name="attention",
    )(*args)


def _outproj_kernel(*refs, has_router):
    if has_router:
        (y_ref, ot_ref, x_ref, mod_ref, wa_ref, wb_ref, gm_ref, gpost_ref, gpre2_ref, rt_ref,
         x1_ref, h2_ref, comb_ref) = refs
    else:
        (y_ref, ot_ref, x_ref, mod_ref, wa_ref, wb_ref, gm_ref, gpost_ref, gpre2_ref,
         x1_ref, h2_ref) = refs
    mod = mod_ref[0]
    gate1 = mod[:, 2 * D_MODEL:3 * D_MODEL]
    shift2 = mod[:, 3 * D_MODEL:4 * D_MODEL]
    scale2 = mod[:, 4 * D_MODEL:5 * D_MODEL]
    ot = ot_ref[...]
    ms = jnp.mean(ot * ot, axis=0, keepdims=True)
    on = (ot * lax.rsqrt(ms + EPS) * gm_ref[...]).astype(BF16)
    y = _dot(y_ref[...], wa_ref[...]) + _dot_tn(on, wb_ref[...])
    x1 = x_ref[...] + gate1 * _rms_rows(y, gpost_ref[...])
    x1_ref[...] = x1
    h2 = _rms_rows(x1, gpre2_ref[...]) * (1.0 + scale2) + shift2
    h2_ref[...] = h2.astype(BF16)
    if has_router:
        hh, hm, _ = _split3(h2)
        rh, rm, _ = _split3(rt_ref[...])
        logits = _dot(hh, rh) + (_dot(hm, rh) + _dot(hh, rm))
        lane = lax.broadcasted_iota(jnp.int32, logits.shape, 1).astype(F32)
        neg = jnp.float32(-jnp.inf)
        lg = jnp.where(lane < N_EXPERTS, logits, neg)
        m1 = jnp.max(lg, axis=-1, keepdims=True)
        i1 = jnp.min(jnp.where(lg == m1, lane, float(LANE)), axis=-1, keepdims=True)
        lg2 = jnp.where(lane == i1, neg, lg)
        m2 = jnp.max(lg2, axis=-1, keepdims=True)
        i2 = jnp.min(jnp.where(lg2 == m2, lane, float(LANE)), axis=-1, keepdims=True)
        e2 = jnp.exp(m2 - m1)
        w1 = 1.0 / (1.0 + e2)
        w2 = e2 / (1.0 + e2)
        comb_ref[...] = jnp.where(lane == i1, w1, 0.0) + jnp.where(lane == i2, w2, 0.0)


def _outproj(yssd, ot, x, mod3, mod_row_fn, lw, router=None):
    t = x.shape[0]
    tm = ROW_TILE
    const = lambda i: (0, 0)
    row = lambda i: (i, 0)
    in_specs = [
        pl.BlockSpec((tm, SSD_WIDTH), row),
        pl.BlockSpec((MLA_WIDTH, tm), lambda i: (0, i)),
        pl.BlockSpec((tm, D_MODEL), row),
        pl.BlockSpec((1, 1, N_MOD * D_MODEL), lambda i: (mod_row_fn(ROW_TILE)(i), 0, 0)),
        pl.BlockSpec((SSD_WIDTH, D_MODEL), const),
        pl.BlockSpec((MLA_WIDTH, D_MODEL), const),
        pl.BlockSpec((MLA_WIDTH, 1), const),
        pl.BlockSpec((1, D_MODEL), const),
        pl.BlockSpec((1, D_MODEL), const),
    ]
    args = [yssd, ot, x, mod3, lw["w_out_a"], lw["w_out_b"], lw["mla_norm_col"],
            lw["g_post1"], lw["g_pre2"]]
    out_specs = [pl.BlockSpec((tm, D_MODEL), row), pl.BlockSpec((tm, D_MODEL), row)]
    out_shape = [jax.ShapeDtypeStruct((t, D_MODEL), F32), jax.ShapeDtypeStruct((t, D_MODEL), BF16)]
    if router is not None:
        in_specs.append(pl.BlockSpec((D_MODEL, LANE), const))
        args.append(router)
        out_specs.append(pl.BlockSpec((tm, LANE), row))
        out_shape.append(jax.ShapeDtypeStruct((t, LANE), F32))
    return pl.pallas_call(
        functools.partial(_outproj_kernel, has_router=router is not None),
        grid=(t // tm,),
        in_specs=in_specs,
        out_specs=out_specs,
        out_shape=out_shape,
        compiler_params=_cparams("arbitrary"),
        name="outproj",
    )(*args)


def _ffn_kernel(h_ref, x_ref, mod_ref, wgu_ref, wd_ref, gpost_ref, o_ref, acc_ref, *, nslab):
    e = pl.program_id(1)
    h = h_ref[...]
    f = wd_ref.shape[1]
    gu = _dot(h, wgu_ref[0])
    hid = _silu(gu[:, :f]) * gu[:, f:]
    part = _dot(hid.astype(BF16), wd_ref[0])

    @pl.when(e == 0)
    def _():
        acc_ref[...] = part

    @pl.when(e > 0)
    def _():
        acc_ref[...] += part

    @pl.when(e == nslab - 1)
    def _():
        gate2 = mod_ref[0][:, 5 * D_MODEL:6 * D_MODEL]
        o_ref[...] = x_ref[...] + gate2 * _rms_rows(acc_ref[...], gpost_ref[...])


def _ffn(h2, x1, mod3, mod_row_fn, wgu, wd, gpost):
    t = x1.shape[0]
    tm = FFN_ROW_TILE
    nslab, f, _ = wd.shape
    row = lambda i, e: (i, 0)
    in_specs = [
        pl.BlockSpec((tm, D_MODEL), row),
        pl.BlockSpec((tm, D_MODEL), row),
        pl.BlockSpec((1, 1, N_MOD * D_MODEL), lambda i, e: (mod_row_fn(tm)(i), 0, 0)),
        pl.BlockSpec((1, D_MODEL, 2 * f), lambda i, e: (e, 0, 0)),
        pl.BlockSpec((1, f, D_MODEL), lambda i, e: (e, 0, 0)),
        pl.BlockSpec((1, D_MODEL), lambda i, e: (0, 0)),
    ]
    args = [h2, x1, mod3, wgu, wd, gpost]
    return pl.pallas_call(
        functools.partial(_ffn_kernel, nslab=nslab),
        grid=(t // tm, nslab),
        in_specs=in_specs,
        out_specs=pl.BlockSpec((tm, D_MODEL), row),
        out_shape=jax.ShapeDtypeStruct((t, D_MODEL), F32),
        scratch_shapes=[pltpu.VMEM((tm, D_MODEL), F32)],
        compiler_params=_cparams("arbitrary", "arbitrary"),
        name="ffn",
    )(*args)


def _moe_kernel(h_ref, x_ref, comb_ref, mod_ref, wg_ref, wu_ref, wd_ref, gpost_ref, o_ref,
                acc_ref, rank_ref, rank_t_ref, comb_t_ref, *, nexp, sub):
    e = pl.program_id(1)
    tm = h_ref.shape[0]
    caps = MOE_CAPS
    cmax = caps[-1]

    @pl.when(e == 0)
    def _():
        acc_ref[...] = jnp.zeros_like(acc_ref)
        r_i = lax.broadcasted_iota(jnp.int32, (sub, sub), 0)
        c_i = lax.broadcasted_iota(jnp.int32, (sub, sub), 1)
        strict = jnp.where(r_i > c_i, 1.0, 0.0).astype(BF16)
        for s in range(tm // sub):
            rows = slice(s * sub, (s + 1) * sub)
            comb = comb_ref[rows, :]
            rank = _dot(strict, jnp.where(comb > 0.0, 1.0, 0.0).astype(BF16))
            rank_ref[rows, :] = rank
            rank_t_ref[:, rows] = rank.T
            comb_t_ref[:, rows] = comb.T

    def expert_pass(rows, wcol, rcol, wrow, rrow, base, cap):
        capl = -(-cap // LANE) * LANE
        slot_l = lax.broadcasted_iota(jnp.int32, (sub, capl), 1).astype(F32)
        slot_s = lax.broadcasted_iota(jnp.int32, (cap, sub), 0).astype(F32)
        gather = jnp.where(((rrow - base) == slot_s) & (wrow > 0.0), 1.0, 0.0).astype(BF16)
        scatter = jnp.where(((rcol - base) == slot_l) & (wcol > 0.0) & (slot_l < float(cap)),
                            1.0, 0.0).astype(BF16)
        xg = _dot(gather, h_ref[rows, :]).astype(BF16)
        hid = _silu(_dot(xg, wg_ref[0])) * _dot(xg, wu_ref[0])
        y = _dot(hid.astype(BF16), wd_ref[0]).astype(BF16)
        if capl > cap:
            y = jnp.concatenate([y, jnp.zeros((capl - cap, y.shape[1]), BF16)], axis=0)
        acc_ref[rows, :] += wcol * _dot(scatter, y)

    lane = lax.broadcasted_iota(jnp.int32, (sub, LANE), 1)

    def sub_tile(s, carry):
        rows = pl.ds(pl.multiple_of(s * sub, sub), sub)
        pick = lane == e
        wcol = jnp.sum(jnp.where(pick, comb_ref[rows, :], 0.0), axis=-1, keepdims=True)
        rcol = jnp.sum(jnp.where(pick, rank_ref[rows, :], 0.0), axis=-1, keepdims=True)
        wrow = comb_t_ref[pl.ds(e, 1), rows]
        rrow = rank_t_ref[pl.ds(e, 1), rows]
        count = jnp.max(jnp.where(wrow > 0.0, rrow + 1.0, 0.0))
        want = count + float(MOE_SPARE_SLOTS)
        npass = ((want + (cmax - 1.0)) * (1.0 / cmax)).astype(jnp.int32)
        nfull = npass - 1

        def full_pass(k, c):
            expert_pass(rows, wcol, rcol, wrow, rrow, (k * cmax).astype(F32), cmax)
            return c

        lax.fori_loop(0, nfull, full_pass, 0)
        base = (nfull * cmax).astype(F32)
        left = want - base
        lo = 0
        for cap in caps:
            @pl.when(jnp.logical_and(left > float(lo), left <= float(cap)))
            def _(cap=cap):
                expert_pass(rows, wcol, rcol, wrow, rrow, base, cap)
            lo = cap
        return carry

    lax.fori_loop(0, tm // sub, sub_tile, 0)

    @pl.when(e == nexp - 1)
    def _():
        gate2 = mod_ref[0][:, 5 * D_MODEL:6 * D_MODEL]
        o_ref[...] = x_ref[...] + gate2 * _rms_rows(acc_ref[...], gpost_ref[...])


def _moe(h2, x1, comb, mod3, mod_row_fn, wg, wu, wd, gpost):
    t = x1.shape[0]
    tm = min(MOE_ROW_TILE, t)
    sub = min(MOE_SUB_TILE, tm)
    nexp, f, _ = wd.shape
    row = lambda i, e: (i, 0)
    return pl.pallas_call(
        functools.partial(_moe_kernel, nexp=nexp, sub=sub),
        grid=(t // tm, nexp),
        in_specs=[
            pl.BlockSpec((tm, D_MODEL), row),
            pl.BlockSpec((tm, D_MODEL), row),
            pl.BlockSpec((tm, LANE), row),
            pl.BlockSpec((1, 1, N_MOD * D_MODEL), lambda i, e: (mod_row_fn(tm)(i), 0, 0)),
            pl.BlockSpec((1, D_MODEL, f), lambda i, e: (e, 0, 0)),
            pl.BlockSpec((1, D_MODEL, f), lambda i, e: (e, 0, 0)),
            pl.BlockSpec((1, f, D_MODEL), lambda i, e: (e, 0, 0)),
            pl.BlockSpec((1, D_MODEL), lambda i, e: (0, 0)),
        ],
        out_specs=pl.BlockSpec((tm, D_MODEL), row),
        out_shape=jax.ShapeDtypeStruct((t, D_MODEL), F32),
        scratch_shapes=[pltpu.VMEM((tm, D_MODEL), F32), pltpu.VMEM((tm, LANE), F32),
                        pltpu.VMEM((LANE, tm), F32), pltpu.VMEM((LANE, tm), F32)],
        compiler_params=_cparams("arbitrary", "arbitrary"),
        name="moe",
    )(h2, x1, comb, mod3, wg, wu, wd, gpost)


def _prep_layer(i, p):
    w_in = p["w_in"][i]
    s1 = SSD_WIDTH
    s2 = s1 + SSD_XBC
    s3 = s2 + 2 * SSD_HEADS
    s4 = s3 + MLA_Q_RANK
    s5 = s4 + MLA_KV_RANK
    w_z, w_xbc, w_dt, w_cq, w_ckv, w_kr = (w_in[:, :s1], w_in[:, s1:s2], w_in[:, s2:s3],
                                             w_in[:, s3:s4], w_in[:, s4:s5], w_in[:, s5:])
    zc = lambda n: jnp.zeros((D_MODEL, n), F32)
    tile_a = jnp.concatenate([w_dt, zc(ROPE_LANE0 - 2 * SSD_HEADS), w_kr,
                              zc(LANE - ROPE_LANE0 - MLA_ROPE_DIM)], axis=1)
    w_in_pad = jnp.concatenate([w_z, w_xbc, w_cq, w_ckv, tile_a], axis=1).astype(BF16)

    w_uq = p["w_uq"][i].reshape(MLA_Q_RANK, MLA_HEADS, MLA_NOPE_DIM + MLA_ROPE_DIM)
    q_nope, q_rope = w_uq[..., :MLA_NOPE_DIM], w_uq[..., MLA_NOPE_DIM:]
    zq = lambda n: jnp.zeros((MLA_Q_RANK, MLA_HEADS, n), F32)
    pad = HEAD_PAD - MLA_NOPE_DIM - MLA_ROPE_DIM
    w_q = jnp.concatenate([q_nope, q_rope, zq(pad)], axis=-1).reshape(MLA_Q_RANK, -1)

    w_ukv = p["w_ukv"][i].reshape(MLA_KV_RANK, MLA_HEADS, MLA_NOPE_DIM + MLA_V_DIM)
    k_nope, v_w = w_ukv[..., :MLA_NOPE_DIM], w_ukv[..., MLA_NOPE_DIM:]
    w_uk = jnp.concatenate([k_nope, jnp.zeros((MLA_KV_RANK, MLA_HEADS, HEAD_PAD - MLA_NOPE_DIM), F32)],
                           axis=-1).reshape(MLA_KV_RANK, -1)
    w_uv = v_w.reshape(MLA_KV_RANK, MLA_WIDTH)

    dtb = p["dt_bias"][i].reshape(2 * SSD_HEADS)
    alog = p["a_log"][i].reshape(2 * SSD_HEADS)
    padl = lambda v: jnp.pad(v, (0, LANE - v.shape[0])).reshape(1, LANE)
    w_out = p["w_out"][i]
    return {
        "g_pre1": p["norm_pre_mix"][i].reshape(1, D_MODEL),
        "g_post1": p["norm_post_mix"][i].reshape(1, D_MODEL),
        "g_pre2": p["norm_pre_ffn"][i].reshape(1, D_MODEL),
        "g_post2": p["norm_post_ffn"][i].reshape(1, D_MODEL),
        "w_in": w_in_pad,
        "w_dt": w_dt.T.astype(BF16),
        "dtb_row": padl(dtb),
        "dtb_col": dtb.reshape(-1, 1),
        "alog_row": padl(alog),
        "alog_col": alog.reshape(-1, 1),
        "q_norm": p["q_norm"][i].reshape(1, -1),
        "w_q": w_q.T.astype(BF16),
        "kv_norm": p["kv_norm"][i].reshape(1, -1),
        "w_uk": w_uk.astype(BF16),
        "w_uv": w_uv.T.astype(BF16),
        "conv_w": p["conv_w"][i],
        "conv_b": p["conv_b"][i].reshape(1, -1),
        "dskip_row": jnp.repeat(p["d_skip"][i], SSD_HEAD_DIM).reshape(1, -1),
        "ssd_norm": p["ssd_norm"][i].reshape(1, -1),
        "mla_norm_col": p["mla_norm"][i].reshape(-1, 1),
        "w_out_a": w_out[:SSD_WIDTH].astype(BF16),
        "w_out_b": w_out[SSD_WIDTH:].astype(BF16),
    }


def _rope_tables(n_tokens):
    rows = n_tokens // GRID_W
    row = np.repeat(np.arange(rows, dtype=np.float32), GRID_W)
    col = np.tile(np.arange(GRID_W, dtype=np.float32), rows)
    half = MLA_ROPE_DIM // 2
    inv = (np.float32(ROPE_THETA) ** (-np.arange(0, half, 2, dtype=np.float32) / np.float32(half))).astype(np.float32)
    ar = row[:, None] * inv[None, :]
    ac = col[:, None] * inv[None, :]
    ang = np.concatenate([ar, ar, ac, ac], axis=-1).astype(np.float32)
    return jnp.asarray(np.cos(ang), F32), jnp.asarray(np.sin(ang), F32)


def _attn_tables(cos, sin, n):
    scale = (MLA_NOPE_DIM + MLA_ROPE_DIM) ** -0.5 * math.log2(math.e)
    pad = HEAD_PAD - ROPE_LANE0 - MLA_ROPE_DIM
    cosk = jnp.concatenate([jnp.zeros((n, ROPE_LANE0), F32), cos, jnp.zeros((n, pad), F32)], axis=1)
    sink = jnp.concatenate([jnp.zeros((n, ROPE_LANE0), F32), sin, jnp.zeros((n, pad), F32)], axis=1)
    cosq = jnp.concatenate([jnp.ones((n, ROPE_LANE0), F32), cos, jnp.zeros((n, pad), F32)], axis=1)
    return (cosq * scale).T, (sink * scale).T, cosk, sink


def kernel(x_prompt, x_sample, cache_ckv, cache_krope, state_ssm, c, c_ctx, w_mod, b_mod, norm_pre_mix, norm_post_mix, norm_pre_ffn, norm_post_ffn, w_in, conv_w, conv_b, dt_bias, a_log, d_skip, ssd_norm, q_norm, w_uq, kv_norm, w_ukv, mla_norm, w_out, ffn_w_gate, ffn_w_up, ffn_w_down, moe_router, moe_w_gate, moe_w_up, moe_w_down):
    params = dict(w_in=w_in, conv_w=conv_w, conv_b=conv_b, dt_bias=dt_bias, a_log=a_log, d_skip=d_skip,
                  ssd_norm=ssd_norm, q_norm=q_norm, w_uq=w_uq, kv_norm=kv_norm, w_ukv=w_ukv,
                  mla_norm=mla_norm, w_out=w_out, norm_pre_mix=norm_pre_mix, norm_post_mix=norm_post_mix,
                  norm_pre_ffn=norm_pre_ffn, norm_post_ffn=norm_post_ffn)
    batch, seq, d = x_prompt.shape
    dec_batch, dec_seq, _ = x_sample.shape
    depth = w_in.shape[0]
    past = cache_ckv.shape[2]
    tm = ROW_TILE

    cvec = jnp.concatenate([c_ctx[None, :], c, jnp.zeros((SUBLANE - 1 - dec_batch, d), F32)], axis=0)
    mod = _modulation(cvec, w_mod, b_mod)

    ones = jnp.ones((tm, MLA_ROPE_DIM), F32)
    tabs_ctx = _attn_tables(ones, jnp.zeros_like(ones), tm)
    cos, sin = _rope_tables(dec_seq)
    tabs_lat = _attn_tables(cos, sin, dec_seq)
    lat_blocks = dec_seq // tm

    xp = x_prompt.reshape(batch * seq, d)
    xs = x_sample.reshape(dec_batch * dec_seq, d)
    cache_bufs, ssm_buf = None, None
    for i in range(depth):
        lw = _prep_layer(i, params)
        mod3 = mod[i].reshape(SUBLANE, 1, N_MOD * d)
        j = i // 2
        if i % 2 == 0:
            f = ffn_w_gate.shape[2] // FFN_SLABS
            wgu = jnp.stack([jnp.concatenate([ffn_w_gate[j][:, s * f:(s + 1) * f], ffn_w_up[j][:, s * f:(s + 1) * f]],
                                             axis=1) for s in range(FFN_SLABS)], axis=0).astype(BF16)
            wd = ffn_w_down[j].reshape(FFN_SLABS, f, d).astype(BF16)
            router = None
        else:
            wgu = (moe_w_gate[j].astype(BF16), moe_w_up[j].astype(BF16))
            wd = moe_w_down[j].astype(BF16)
            router = jnp.pad(moe_router[j], ((0, 0), (0, LANE - N_EXPERTS)))

        def run(x, row_fn, tabs, tab_blocks, seq_len, heads_per_step, ctx):
            cache_out = (i, depth, seq_len, cache_bufs) if ctx is None else None
            z, xbc, dt, dtt, qt, k, vt, *caches = _inproj(x, mod3, row_fn, lw, tabs, tab_blocks, cache_out)
            cps = seq_len // SSD_CHUNK
            h0 = None
            if ctx is not None:
                h0 = (_state_to_kernel_layout(ctx[2][:, 0]), _state_to_kernel_layout(ctx[2][:, 1]))
            sink = (2 * i, 2 * depth, ssm_buf) if ctx is None else (0, 2, None)
            yssd, fin = _ssd(xbc, dt, dtt, z, lw, cps, h0=h0, sink=sink)
            cache = None
            if ctx is not None:
                kr_tile = jnp.pad(ctx[1].reshape(-1, MLA_ROPE_DIM),
                                  ((0, 0), (ROPE_LANE0, HEAD_PAD - ROPE_LANE0 - MLA_ROPE_DIM)))
                cache = _kvcache(ctx[0].reshape(-1, MLA_KV_RANK), kr_tile, lw)
            ot = _attention(qt, k, vt, seq_len, heads_per_step, cache=cache)
            outs = _outproj(yssd, ot, x, mod3, row_fn, lw, router=router)
            x1, h2 = outs[0], outs[1]
            comb = outs[2] if router is not None else None
            if comb is None:
                x2 = _ffn(h2, x1, mod3, row_fn, wgu, wd, lw["g_post2"])
            else:
                x2 = _moe(h2, x1, comb, mod3, row_fn, *wgu, wd, lw["g_post2"])
            return x2, caches, fin

        xp, cache_bufs, ssm_buf = run(xp, lambda tile: (lambda b: 0), tabs_ctx, 1, seq, MLA_HEADS, None)
        xs, _, _ = run(xs, lambda tile: (lambda b: 1 + (b * tile) // dec_seq), tabs_lat, lat_blocks, dec_seq, MLA_HEADS,
                             (cache_ckv[:, i], cache_krope[:, i], state_ssm[:, i]))
    return (xp.reshape(batch, seq, d), xs.reshape(dec_batch, dec_seq, d),
            cache_bufs[0], cache_bufs[1],
            ssm_buf.reshape(batch, depth, 2, SSD_HEADS, SSD_HEAD_DIM, SSD_STATE))
```

```python
import functools
import math

import jax
import jax.numpy as jnp
import numpy as np
from jax import lax
from jax.experimental import pallas as pl
from jax.experimental.pallas import tpu as pltpu

F32 = jnp.float32
BF16 = jnp.bfloat16

D_MODEL = 1024
GRID_W = 64
SSD_WIDTH = 512
SSD_HEAD_DIM = 64
SSD_HEADS = 8
SSD_GROUPS = 2
SSD_STATE = 64
SSD_CONV = 5
SSD_CHUNK = 128
SSD_STEP_CHUNKS = 8
SSD_XBC = SSD_WIDTH + 2 * SSD_GROUPS * SSD_STATE
MLA_WIDTH = 512
MLA_V_DIM = 64
MLA_HEADS = 8
MLA_NOPE_DIM = 64
MLA_ROPE_DIM = 32
MLA_Q_RANK = 384
MLA_KV_RANK = 256
ROPE_THETA = 10000.0
N_EXPERTS = 8
N_MOD = 6
EPS = 1e-6

LANE = 128
SUBLANE = 8
HEAD_PAD = 128
ONES_ROWS = 16
ROPE_LANE0 = MLA_NOPE_DIM
C_Z = 0
C_XBC = C_Z + SSD_WIDTH
C_CQ = C_XBC + SSD_XBC
C_CKV = C_CQ + MLA_Q_RANK
C_TA = C_CKV + MLA_KV_RANK
IN_PAD = C_TA + LANE
ROT_GROUP = MLA_ROPE_DIM // 4

VMEM_LIMIT = 56 * 1024 * 1024

ROW_TILE = 1024
ATTN_Q_TILE = 256
ATTN_KEY_BLOCK = 512
ATTN_QTILES_PER_STEP = 2
ATTN_SEQS_PER_STEP = 4
FFN_ROW_TILE = 512
FFN_SLABS = 1
MOE_ROW_TILE = 1024
MOE_SUB_TILE = 512
MOE_CAPS = (128, 160, 192, 224, 256, 288, 320)
MOE_SPARE_SLOTS = 2
MOD_COL_TILE = 1536
KVCACHE_ROW_TILE = 512

NT_DIMS = (((1,), (1,)), ((), ()))
TN_DIMS = (((0,), (0,)), ((), ()))


def _cparams(*sem):
    return pltpu.CompilerParams(dimension_semantics=sem, vmem_limit_bytes=VMEM_LIMIT)


def _silu(x):
    return x / (1.0 + jnp.exp(-x))


def _softplus(x):
    return jnp.maximum(x, 0.0) + jnp.log(1.0 + jnp.exp(-jnp.abs(x)))


def _rms_rows(x, g):
    ms = jnp.mean(x * x, axis=-1, keepdims=True)
    return x * lax.rsqrt(ms + EPS) * g


def _dot(a, b):
    return jnp.dot(a, b, preferred_element_type=F32)


def _dot_nt(a, b):
    return lax.dot_general(a, b, NT_DIMS, preferred_element_type=F32)


def _dot_tn(a, b):
    return lax.dot_general(a, b, TN_DIMS, preferred_element_type=F32)


def _split3(x):
    hi = x.astype(BF16)
    r1 = x - hi.astype(F32)
    mid = r1.astype(BF16)
    lo = (r1 - mid.astype(F32)).astype(BF16)
    return hi, mid, lo


def _mod_kernel(c_ref, w_ref, b_ref, o_ref):
    s = _silu(c_ref[...]).astype(BF16)
    o_ref[0] = _dot(s, w_ref[0].astype(BF16)) + b_ref[0]


def _modulation(cvec, w_mod, b_mod):
    depth, d, n = w_mod.shape
    tn = MOD_COL_TILE
    return pl.pallas_call(
        _mod_kernel,
        grid=(depth, n // tn),
        in_specs=[
            pl.BlockSpec((SUBLANE, d), lambda l, j: (0, 0)),
            pl.BlockSpec((1, d, tn), lambda l, j: (l, 0, j)),
            pl.BlockSpec((1, 1, tn), lambda l, j: (l, 0, j)),
        ],
        out_specs=pl.BlockSpec((1, SUBLANE, tn), lambda l, j: (l, 0, j)),
        out_shape=jax.ShapeDtypeStruct((depth, SUBLANE, n), F32),
        compiler_params=_cparams("arbitrary", "arbitrary"),
        name="modulation",
    )(cvec, w_mod, b_mod.reshape(depth, 1, n))


def _inproj_kernel(*refs, emit_cache, n_alias, cache_slot):
    (x_ref, mod_ref, gpre_ref, win_ref, wdt_ref, dtb_row_ref, dtb_col_ref, qn_ref, wq_ref, kvn_ref, wuk_ref,
     wuv_ref, cosq_ref, sinq_ref, cosk_ref, sink_ref) = refs[:16]
    outs = refs[16 + n_alias:]
    z_ref, xbc_ref, dt_ref, dtt_ref, qt_ref, k_ref, vt_ref = outs[:7]
    mod = mod_ref[0]
    shift = mod[:, 0:D_MODEL]
    scale = mod[:, D_MODEL:2 * D_MODEL]
    h = _rms_rows(x_ref[...], gpre_ref[...]) * (1.0 + scale) + shift
    hb = h.astype(BF16)
    proj = _dot(hb, win_ref[...])
    z_ref[...] = proj[:, C_Z:C_XBC]
    xbc_ref[...] = proj[:, C_XBC:C_CQ]
    cqn = _rms_rows(proj[:, C_CQ:C_CKV], qn_ref[...]).astype(BF16)
    ckvn = _rms_rows(proj[:, C_CKV:C_TA], kvn_ref[...])
    if emit_cache:
        ckvn_ref, kr_ref = outs[7:]
        nb, nslot, sq, _ = ckvn_ref.shape
        if nslot > 1:
            ckvn_ref[...] = jnp.zeros_like(ckvn_ref)
            kr_ref[...] = jnp.zeros_like(kr_ref)
        ckvn_ref[:, cache_slot] = ckvn.reshape(nb, sq, MLA_KV_RANK)
    ckvb = ckvn.astype(BF16)
    ta = proj[:, C_TA:IN_PAD]
    dt_ref[...] = _softplus(ta + dtb_row_ref[...])
    if emit_cache:
        kr_ref[:, cache_slot] = ta[:, ROPE_LANE0:ROPE_LANE0 + MLA_ROPE_DIM].reshape(nb, sq, MLA_ROPE_DIM)
    lane = lax.broadcasted_iota(jnp.int32, ta.shape, 1)
    first = (lane // ROT_GROUP) % 2 == 0
    rot = jnp.where(first, -pltpu.roll(ta, LANE - ROT_GROUP, 1), pltpu.roll(ta, ROT_GROUP, 1))
    kr_rot = ta * cosk_ref[...] + rot * sink_ref[...]
    knp = _dot(ckvb, wuk_ref[...])
    for hd in range(MLA_HEADS):
        sl = slice(hd * HEAD_PAD, (hd + 1) * HEAD_PAD)
        k_ref[:, sl] = (knp[:, sl] + kr_rot).astype(BF16)
    vt_ref[...] = _dot_nt(wuv_ref[...], ckvb).astype(BF16)
    qt = _dot_nt(wq_ref[...], cqn)
    cosq = cosq_ref[...]
    sinq = sinq_ref[...]
    g, r0 = ROT_GROUP, ROPE_LANE0
    for hd in range(MLA_HEADS):
        blk = qt[hd * HEAD_PAD:(hd + 1) * HEAD_PAD, :]
        rot = jnp.concatenate([blk[:r0], -blk[r0 + g:r0 + 2 * g], blk[r0:r0 + g], -blk[r0 + 3 * g:r0 + 4 * g],
                               blk[r0 + 2 * g:r0 + 3 * g], blk[r0 + 4 * g:]], axis=0)
        qt_ref[hd * HEAD_PAD:(hd + 1) * HEAD_PAD, :] = (blk * cosq + rot * sinq).astype(BF16)
    dtt_ref[...] = _softplus(_dot_nt(wdt_ref[...], hb) + dtb_col_ref[...])


def _inproj(x, mod3, mod_row_fn, lw, tabs, tab_blocks, cache_out=None):
    t = x.shape[0]
    tm = ROW_TILE
    nb = t // tm
    cosq, sinq, cosk, sink = tabs
    ntab = tab_blocks
    const = lambda i: (0, 0)
    row = lambda i: (i, 0)
    col = lambda i: (0, i)
    in_specs = [
        pl.BlockSpec((tm, D_MODEL), row),
        pl.BlockSpec((1, 1, N_MOD * D_MODEL), lambda i: (mod_row_fn(ROW_TILE)(i), 0, 0)),
        pl.BlockSpec((1, D_MODEL), const),
        pl.BlockSpec((D_MODEL, IN_PAD), const),
        pl.BlockSpec((2 * SSD_HEADS, D_MODEL), const),
        pl.BlockSpec((1, LANE), const),
        pl.BlockSpec((2 * SSD_HEADS, 1), const),
        pl.BlockSpec((1, MLA_Q_RANK), const),
        pl.BlockSpec((MLA_HEADS * HEAD_PAD, MLA_Q_RANK), const),
        pl.BlockSpec((1, MLA_KV_RANK), const),
        pl.BlockSpec((MLA_KV_RANK, MLA_HEADS * HEAD_PAD), const),
        pl.BlockSpec((MLA_WIDTH, MLA_KV_RANK), const),
        pl.BlockSpec((HEAD_PAD, tm), lambda i: (0, i % ntab)),
        pl.BlockSpec((HEAD_PAD, tm), lambda i: (0, i % ntab)),
        pl.BlockSpec((tm, LANE), lambda i: (i % ntab, 0)),
        pl.BlockSpec((tm, LANE), lambda i: (i % ntab, 0)),
    ]
    out_specs = [
        pl.BlockSpec((tm, SSD_WIDTH), row),
        pl.BlockSpec((tm, SSD_XBC), row),
        pl.BlockSpec((tm, LANE), row),
        pl.BlockSpec((2 * SSD_HEADS, tm), col),
        pl.BlockSpec((MLA_HEADS * HEAD_PAD, tm), col),
        pl.BlockSpec((tm, MLA_HEADS * HEAD_PAD), row),
        pl.BlockSpec((MLA_WIDTH, tm), col),
    ]
    out_shape = [
        jax.ShapeDtypeStruct((t, SSD_WIDTH), F32),
        jax.ShapeDtypeStruct((t, SSD_XBC), F32),
        jax.ShapeDtypeStruct((t, LANE), F32),
        jax.ShapeDtypeStruct((2 * SSD_HEADS, t), F32),
        jax.ShapeDtypeStruct((MLA_HEADS * HEAD_PAD, t), BF16),
        jax.ShapeDtypeStruct((t, MLA_HEADS * HEAD_PAD), BF16),
        jax.ShapeDtypeStruct((MLA_WIDTH, t), BF16),
    ]
    args = [x, mod3, lw["g_pre1"], lw["w_in"], lw["w_dt"], lw["dtb_row"], lw["dtb_col"],
            lw["q_norm"], lw["w_q"], lw["kv_norm"], lw["w_uk"], lw["w_uv"], cosq, sinq, cosk, sink]
    aliases = {}
    n_alias = 0
    cache_slot = 0
    if cache_out is not None:
        layer, depth, seq, bufs = cache_out
        nb_seq = tm // seq
        whole = bufs is None
        cache_slot = layer if whole else 0
        for rank in (MLA_KV_RANK, MLA_ROPE_DIM):
            out_specs.append(pl.BlockSpec((nb_seq, depth if whole else 1, seq, rank),
                                          lambda i: (i, 0 if whole else layer, 0, 0)))
            out_shape.append(jax.ShapeDtypeStruct((t // seq, depth, seq, rank), F32))
        if bufs is not None:
            n_alias = len(bufs)
            for j, buf in enumerate(bufs):
                aliases[len(args)] = len(out_shape) - n_alias + j
                in_specs.append(pl.BlockSpec(memory_space=pl.ANY))
                args.append(buf)
    return pl.pallas_call(
        functools.partial(_inproj_kernel, emit_cache=cache_out is not None, n_alias=n_alias, cache_slot=cache_slot),
        grid=(nb,),
        in_specs=in_specs,
        out_specs=out_specs,
        out_shape=out_shape,
        input_output_aliases=aliases,
        compiler_params=_cparams("arbitrary"),
        name="inproj",
    )(*args)


def _kvcache_kernel(ckv_ref, kr_ref, wuk_ref, wuv_ref, k_ref, vt_ref):
    ckvb = ckv_ref[...].astype(BF16)
    knp = _dot(ckvb, wuk_ref[...])
    kr = kr_ref[...]
    for hd in range(MLA_HEADS):
        sl = slice(hd * HEAD_PAD, (hd + 1) * HEAD_PAD)
        k_ref[:, sl] = (knp[:, sl] + kr).astype(BF16)
    vt_ref[...] = _dot_nt(wuv_ref[...], ckvb).astype(BF16)


def _kvcache(ckv, kr_tile, lw):
    n = ckv.shape[0]
    tm = KVCACHE_ROW_TILE
    return pl.pallas_call(
        _kvcache_kernel,
        grid=(n // tm,),
        in_specs=[
            pl.BlockSpec((tm, MLA_KV_RANK), lambda i: (i, 0)),
            pl.BlockSpec((tm, LANE), lambda i: (i, 0)),
            pl.BlockSpec((MLA_KV_RANK, MLA_HEADS * HEAD_PAD), lambda i: (0, 0)),
            pl.BlockSpec((MLA_WIDTH, MLA_KV_RANK), lambda i: (0, 0)),
        ],
        out_specs=[
            pl.BlockSpec((tm, MLA_HEADS * HEAD_PAD), lambda i: (i, 0)),
            pl.BlockSpec((MLA_WIDTH, tm), lambda i: (0, i)),
        ],
        out_shape=[
            jax.ShapeDtypeStruct((n, MLA_HEADS * HEAD_PAD), BF16),
            jax.ShapeDtypeStruct((MLA_WIDTH, n), BF16),
        ],
        compiler_params=_cparams("arbitrary"),
        name="kvcache",
    )(ckv, kr_tile, lw["w_uk"], lw["w_uv"])


def _head_expand_matrix():
    r = lax.broadcasted_iota(jnp.int32, (LANE, 2 * SSD_WIDTH), 0)
    c = lax.broadcasted_iota(jnp.int32, (LANE, 2 * SSD_WIDTH), 1)
    return jnp.where(c // SSD_HEAD_DIM == r, 1.0, 0.0).astype(BF16)


def _expand_heads(v, emat):
    hi = v.astype(BF16)
    mid = (v - hi.astype(F32)).astype(BF16)
    return _dot(hi, emat) + _dot(mid, emat)


def _prefix_rows(dta, tril):
    return sum(_dot(tril, p) for p in _split3(dta))


def _chunk_masks(q):
    r_i = lax.broadcasted_iota(jnp.int32, (q, q), 0)
    c_i = lax.broadcasted_iota(jnp.int32, (q, q), 1)
    return r_i >= c_i, r_i <= c_i


def _ssd_state_kernel(*refs, cps, nc, nsq, has_h0, has_sink, fin_slot):
    it = iter(refs)
    xbc_ref = next(it)
    prev_ref, next_ref = (next(it), next(it)) if nsq == 0 else (None, None)
    dt_ref, cw_ref, cb_ref, alog_row_ref = (next(it) for _ in range(4))
    h0_ref = next(it) if has_h0 else None
    if has_sink:
        next(it)
    xcb_ref, hsf_ref, sb_ref, dec_ref, hfin_ref, st_ref = (next(it) for _ in range(6))

    q = SSD_CHUNK
    rows = nc * q
    cw = cw_ref[...]

    def conv_silu(ext, n):
        acc = cb_ref[...] + ext[SUBLANE - 2:SUBLANE - 2 + n] * cw[0:1]
        for k in range(1, SSD_CONV):
            o = SUBLANE - 2 + k
            acc = acc + ext[o:o + n] * cw[k:k + 1]
        return _silu(acc).astype(BF16)

    if nsq == 0:
        pos = (pl.program_id(0) * nc) % cps
        seq_first = pos == 0
        seq_last = pos + nc == cps
        prev = jnp.where(seq_first, 0.0, prev_ref[...])
        nxt = jnp.where(seq_last, 0.0, next_ref[...])
        xcb_all = conv_silu(jnp.concatenate([prev, xbc_ref[...], nxt], axis=0), rows)
    else:
        ln = cps * q
        pad = jnp.zeros((SUBLANE, SSD_XBC), F32)
        xcb_all = jnp.concatenate(
            [conv_silu(jnp.concatenate([pad, xbc_ref[sq * ln:(sq + 1) * ln, :], pad], axis=0), ln)
             for sq in range(nsq)], axis=0)
    xcb_ref[...] = xcb_all

    lower, _ = _chunk_masks(q)
    tril = jnp.where(lower, 1.0, 0.0).astype(BF16)
    a_row = -jnp.exp(alog_row_ref[...])
    lane = lax.broadcasted_iota(jnp.int32, (q, LANE), 1)
    lane_t = lax.broadcasted_iota(jnp.int32, (2 * SUBLANE, LANE), 1)
    emat = _head_expand_matrix()
    gw = SSD_WIDTH // SSD_GROUPS

    if nsq == 0:
        @pl.when(seq_first)
        def _():
            if has_h0:
                st_ref[...] = h0_ref[0]
            else:
                st_ref[...] = jnp.zeros_like(st_ref)

    chunk_dec, chunk_states = [], []
    for c in range(nc):
        sl = slice(c * q, (c + 1) * q)
        xcb = xcb_all[sl]
        xs = xcb[:, :SSD_WIDTH].astype(F32)
        dt = dt_ref[sl, :]
        dta = dt * a_row
        la = _prefix_rows(dta, tril)
        tot = la[q - 1:q, :]
        w = jnp.exp(jnp.where(lane < SSD_HEADS, tot - la, la - dta)) * dt
        w = jnp.where(lane < 2 * SSD_HEADS, w, 0.0)
        wexp = _expand_heads(w, emat)
        etot = jnp.where(lane_t < 2 * SSD_HEADS, jnp.exp(jnp.broadcast_to(tot, (2 * SUBLANE, LANE))), 0.0)
        dec = _expand_heads(etot, emat)[:SUBLANE]
        dec_ref[c] = dec
        bmb = xcb[:, SSD_WIDTH:SSD_WIDTH + SSD_GROUPS * SSD_STATE]
        states = []
        for d in range(2):
            xw = (xs * wexp[:, d * SSD_WIDTH:(d + 1) * SSD_WIDTH]).astype(BF16)
            parts = [_dot_tn(bmb[:, grp * SSD_STATE:(grp + 1) * SSD_STATE], xw[:, grp * gw:(grp + 1) * gw])
                     for grp in range(SSD_GROUPS)]
            states.append(jnp.concatenate(parts, axis=1))
        sb_ref[c] = states[1]
        chunk_dec.append(dec[0:1, :SSD_WIDTH])
        chunk_states.append(states[0])

    if nsq == 0:
        hs = st_ref[...]
        for c in range(nc):
            hsf_ref[c] = hs.astype(BF16)
            hs = hs * chunk_dec[c] + chunk_states[c]
        st_ref[...] = hs

        @pl.when(seq_last)
        def _():
            if not has_sink:
                hfin_ref[...] = jnp.zeros_like(hfin_ref)
            hfin_ref[0, fin_slot] = hs.T
    else:
        for sq in range(nsq):
            hs = h0_ref[sq] if has_h0 else jnp.zeros(st_ref.shape, F32)
            for c in range(sq * cps, (sq + 1) * cps):
                hsf_ref[c] = hs.astype(BF16)
                hs = hs * chunk_dec[c] + chunk_states[c]
            if not has_sink:
                hfin_ref[sq] = jnp.zeros(hfin_ref.shape[1:], F32)
            hfin_ref[sq, fin_slot] = hs.T


def _ssd_out_kernel(*refs, cps, nc, nsq, has_h0, has_sink, nsteps):
    it = iter(refs)
    (xcb_ref, dt_ref, dtt_ref, z_ref, hsf_ref, sb_ref, dec_ref,
     alog_row_ref, alog_col_ref, dskip_ref, gn_ref) = (next(it) for _ in range(11))
    h0_ref = next(it) if has_h0 else None
    if has_sink:
        next(it)
    y_ref, hfin_ref, st_ref = next(it), next(it), next(it)

    q = SSD_CHUNK
    pos = ((nsteps - 1 - pl.program_id(0)) * nc) % cps
    seq_first = pos == 0
    seq_last = pos + nc == cps
    log2e = math.log2(math.e)

    lower, upper = _chunk_masks(q)
    tril = jnp.where(lower, 1.0, 0.0).astype(BF16)
    triu = jnp.where(upper, 1.0, 0.0).astype(BF16)
    a_row = -jnp.exp(alog_row_ref[...])
    a_col = -jnp.exp(alog_col_ref[...])
    lane = lax.broadcasted_iota(jnp.int32, (q, LANE), 1)
    rowi = lax.broadcasted_iota(jnp.int32, (2 * SUBLANE, q), 0)
    emat = _head_expand_matrix()
    rep = SSD_HEADS // SSD_GROUPS
    gw = SSD_WIDTH // SSD_GROUPS
    neg = jnp.float32(-jnp.inf)

    if nsq == 0:
        @pl.when(seq_last)
        def _():
            if has_h0:
                st_ref[...] = h0_ref[0]
            else:
                st_ref[...] = jnp.zeros_like(st_ref)

    def chunk_terms(c):
        sl = slice(c * q, (c + 1) * q)
        xcb = xcb_ref[sl, :]
        xsb = xcb[:, :SSD_WIDTH]
        bmb = xcb[:, SSD_WIDTH:SSD_WIDTH + SSD_GROUPS * SSD_STATE]
        cmb = xcb[:, SSD_WIDTH + SSD_GROUPS * SSD_STATE:]
        dt = dt_ref[sl, :]
        dtt = dtt_ref[:, sl]
        dta = dt * a_row
        dtat = dtt * a_col
        la = _prefix_rows(dta, tril)
        tot = la[q - 1:q, :]
        lcol = jnp.where(lane < SSD_HEADS, la, tot - la + dta)
        lat = sum(_dot(p, triu) for p in _split3(dtat))
        tott = lat[:, q - 1:q]
        lrow = jnp.where(rowi < SSD_HEADS, lat, tott - lat + dtat)
        lcol2 = lcol * log2e
        lrow2 = (lrow - jnp.log(dtt)) * log2e
        ecol = jnp.where(lane < 2 * SSD_HEADS, jnp.exp(lcol), 0.0)
        eexp = _expand_heads(ecol, emat)

        cbs = []
        for grp in range(SSD_GROUPS):
            cg = cmb[:, grp * SSD_STATE:(grp + 1) * SSD_STATE]
            bg = bmb[:, grp * SSD_STATE:(grp + 1) * SSD_STATE]
            cbs.append(_dot_nt(cg, bg))
        tiles = []
        for pair in range(SSD_HEADS // 2):
            xpair = xsb[:, pair * LANE:(pair + 1) * LANE]
            res = []
            for hd in (2 * pair, 2 * pair + 1):
                jf, jb = hd, SSD_HEADS + hd
                ef = jnp.exp2(jnp.where(lower, lcol2[:, jf:jf + 1] - lrow2[jf:jf + 1, :], neg))
                eb = jnp.exp2(jnp.where(upper, lcol2[:, jb:jb + 1] - lrow2[jb:jb + 1, :], neg))
                mm = (cbs[hd // rep] * (ef + eb)).astype(BF16)
                res.append(_dot(mm, xpair))
            tiles.append(jnp.where(lane < SSD_HEAD_DIM, res[0], res[1]))
        y = jnp.concatenate(tiles, axis=1)
        hsf = hsf_ref[c]
        parts = [_dot(cmb[:, grp * SSD_STATE:(grp + 1) * SSD_STATE], hsf[:, grp * gw:(grp + 1) * gw])
                 for grp in range(SSD_GROUPS)]
        y = y + jnp.concatenate(parts, axis=1) * eexp[:, :SSD_WIDTH]
        y = y + dskip_ref[...] * xsb.astype(F32)
        return y, cmb, eexp[:, SSD_WIDTH:], _silu(z_ref[sl, :])

    terms = [chunk_terms(c) for c in range(nc)]

    def finish_chunk(c, hb):
        y, cmb, eexp_b, gate = terms[c]
        hsb = hb.astype(BF16)
        parts = [_dot(cmb[:, grp * SSD_STATE:(grp + 1) * SSD_STATE], hsb[:, grp * gw:(grp + 1) * gw])
                 for grp in range(SSD_GROUPS)]
        y = (y + jnp.concatenate(parts, axis=1) * eexp_b) * gate
        y_ref[c * q:(c + 1) * q, :] = _rms_rows(y, gn_ref[...]).astype(BF16)
        return hb * dec_ref[c][0:1, SSD_WIDTH:] + sb_ref[c]

    if nsq == 0:
        hb = st_ref[...]
        for c in reversed(range(nc)):
            hb = finish_chunk(c, hb)
        st_ref[...] = hb

        @pl.when(seq_first)
        def _():
            hfin_ref[0, 0] = hb.T
    else:
        for sq in range(nsq):
            hb = h0_ref[sq] if has_h0 else jnp.zeros(st_ref.shape, F32)
            for c in reversed(range(sq * cps, (sq + 1) * cps)):
                hb = finish_chunk(c, hb)
            hfin_ref[sq, 0] = hb.T


def _ssd(xbc, dt, dtt, z, lw, cps, h0=None, sink=(0, 2, None)):
    slot0, nslots, sink_buf = sink
    t = xbc.shape[0]
    q = SSD_CHUNK
    nchunks = t // q
    nseq = nchunks // cps
    nc = SSD_STEP_CHUNKS
    if cps >= nc:
        assert cps % nc == 0
        nsq, spq, sps = 0, cps // nc, 1
    else:
        assert nc % cps == 0 and nseq % (nc // cps) == 0
        nsq, spq = nc // cps, 1
        sps = nsq
    rows = nc * q
    nsteps = nchunks // nc
    hb = rows // SUBLANE
    n8 = t // SUBLANE
    has_h0 = h0 is not None
    const = lambda i: (0, 0)
    st_block = (sps, SSD_STATE, SSD_WIDTH)
    ch_block = (nc, SSD_STATE, SSD_WIDTH)
    fin_block = (sps, 1, SSD_WIDTH, SSD_STATE)
    fin_shape = jax.ShapeDtypeStruct((nseq, nslots, SSD_WIDTH, SSD_STATE), F32)
    any_spec = pl.BlockSpec(memory_space=pl.ANY)
    dec_block = (nc, SUBLANE, 2 * SSD_WIDTH)

    in_specs = [pl.BlockSpec((rows, SSD_XBC), lambda i: (i, 0))]
    args = [xbc]
    if nsq == 0:
        in_specs += [
            pl.BlockSpec((SUBLANE, SSD_XBC), lambda i: (jnp.maximum(i * hb - 1, 0), 0)),
            pl.BlockSpec((SUBLANE, SSD_XBC), lambda i: (jnp.minimum((i + 1) * hb, n8 - 1), 0)),
        ]
        args += [xbc, xbc]
    in_specs += [
        pl.BlockSpec((rows, LANE), lambda i: (i, 0)),
        pl.BlockSpec((SSD_CONV, SSD_XBC), const),
        pl.BlockSpec((1, SSD_XBC), const),
        pl.BlockSpec((1, LANE), const),
    ]
    args += [dt, lw["conv_w"], lw["conv_b"], lw["alog_row"]]
    if has_h0:
        in_specs.append(pl.BlockSpec(st_block, lambda i: (i // spq, 0, 0)))
        args.append(h0[0])
    aliases = {}
    if sink_buf is not None:
        aliases[len(args)] = 4
        in_specs.append(any_spec)
        args.append(sink_buf)
    xcb, hsf, sb, dec, fin = pl.pallas_call(
        functools.partial(_ssd_state_kernel, cps=cps, nc=nc, nsq=nsq, has_h0=has_h0, has_sink=sink_buf is not None,
                          fin_slot=0 if sink_buf is not None else slot0),
        grid=(nsteps,),
        in_specs=in_specs,
        out_specs=[
            pl.BlockSpec((rows, SSD_XBC), lambda i: (i, 0)),
            pl.BlockSpec(ch_block, lambda i: (i, 0, 0)),
            pl.BlockSpec(ch_block, lambda i: (i, 0, 0)),
            pl.BlockSpec(dec_block, lambda i: (i, 0, 0)),
            pl.BlockSpec(fin_block if sink_buf is not None else (sps, nslots, SSD_WIDTH, SSD_STATE),
                         lambda i: (i // spq, slot0 if sink_buf is not None else 0, 0, 0)),
        ],
        out_shape=[
            jax.ShapeDtypeStruct((t, SSD_XBC), BF16),
            jax.ShapeDtypeStruct((nchunks, SSD_STATE, SSD_WIDTH), BF16),
            jax.ShapeDtypeStruct((nchunks, SSD_STATE, SSD_WIDTH), F32),
            jax.ShapeDtypeStruct((nchunks, SUBLANE, 2 * SSD_WIDTH), F32),
            fin_shape,
        ],
        input_output_aliases=aliases,
        scratch_shapes=[pltpu.VMEM((SSD_STATE, SSD_WIDTH), F32)],
        compiler_params=_cparams("arbitrary"),
        name="ssd_state",
    )(*args)

    gi = lambda i: nsteps - 1 - i
    in_specs = [
        pl.BlockSpec((rows, SSD_XBC), lambda i: (gi(i), 0)),
        pl.BlockSpec((rows, LANE), lambda i: (gi(i), 0)),
        pl.BlockSpec((2 * SSD_HEADS, rows), lambda i: (0, gi(i))),
        pl.BlockSpec((rows, SSD_WIDTH), lambda i: (gi(i), 0)),
        pl.BlockSpec(ch_block, lambda i: (gi(i), 0, 0)),
        pl.BlockSpec(ch_block, lambda i: (gi(i), 0, 0)),
        pl.BlockSpec(dec_block, lambda i: (gi(i), 0, 0)),
        pl.BlockSpec((1, LANE), const),
        pl.BlockSpec((2 * SSD_HEADS, 1), const),
        pl.BlockSpec((1, SSD_WIDTH), const),
        pl.BlockSpec((1, SSD_WIDTH), const),
    ]
    args = [xcb, dt, dtt, z, hsf, sb, dec, lw["alog_row"], lw["alog_col"], lw["dskip_row"], lw["ssd_norm"]]
    if has_h0:
        in_specs.append(pl.BlockSpec(st_block, lambda i: (gi(i) // spq, 0, 0)))
        args.append(h0[1])
    in_specs.append(any_spec)
    args.append(fin)
    y, fin = pl.pallas_call(
        functools.partial(_ssd_out_kernel, cps=cps, nc=nc, nsq=nsq, has_h0=has_h0, has_sink=True, nsteps=nsteps),
        grid=(nsteps,),
        in_specs=in_specs,
        out_specs=[
            pl.BlockSpec((rows, SSD_WIDTH), lambda i: (gi(i), 0)),
            pl.BlockSpec(fin_block, lambda i: (gi(i) // spq, slot0 + 1, 0, 0)),
        ],
        out_shape=[jax.ShapeDtypeStruct((t, SSD_WIDTH), BF16), fin_shape],
        input_output_aliases={len(args) - 1: 1},
        scratch_shapes=[pltpu.VMEM((SSD_STATE, SSD_WIDTH), F32)],
        compiler_params=_cparams("arbitrary"),
        name="ssd_out",
    )(*args)
    return y, fin


def _state_to_kernel_layout(h):
    n = h.shape[0]
    return h.transpose(0, 3, 1, 2).reshape(n, SSD_STATE, SSD_WIDTH)


def _attn_kernel(*refs, heads, has_cache, nseq_step, shared_keys):
    if has_cache:
        qt_ref, k_ref, vt_ref, kc_ref, vct_ref = refs[:5]
    else:
        qt_ref, k_ref, vt_ref = refs[:3]
    n_in = 5 if has_cache else 3
    o_ref = refs[n_in]
    scratch = refs[n_in + 1:]
    tq = qt_ref.shape[1] // nseq_step
    lk = k_ref.shape[0] if shared_keys else k_ref.shape[0] // nseq_step
    kb = min(ATTN_KEY_BLOCK, lk)
    ones = jnp.ones((ONES_ROWS, kb), BF16)

    for sq in range(nseq_step):
        s_refs = scratch[2 * sq:2 * sq + 2]
        qcols = slice(sq * tq, (sq + 1) * tq)
        blocks = [(k_ref, vt_ref, (0 if shared_keys else sq * lk) + i * kb) for i in range(lk // kb)]
        if has_cache:
            assert nseq_step == 1 or shared_keys
            lc = kc_ref.shape[0]
            assert min(ATTN_KEY_BLOCK, lc) == kb
            blocks += [(kc_ref, vct_ref, i * kb) for i in range(lc // kb)]
        nblk = len(blocks)

        def score_block(hd, i, m, blocks=blocks, s_refs=s_refs, qcols=qcols):
            kr, _, off = blocks[i]
            q = qt_ref[hd * HEAD_PAD:(hd + 1) * HEAD_PAD, qcols]
            s = _dot(kr[off:off + kb, hd * HEAD_PAD:(hd + 1) * HEAD_PAD], q)
            s_refs[hd % 2][i * kb:(i + 1) * kb, :] = s
            bm = jnp.max(s, axis=0, keepdims=True)
            return bm if m is None else jnp.maximum(m, bm)

        def value_block(hd, i, m, acc, blocks=blocks, s_refs=s_refs):
            _, vr, off = blocks[i]
            p = jnp.exp2((s_refs[hd % 2][i * kb:(i + 1) * kb, :] - m).astype(BF16))
            v = vr[hd * MLA_V_DIM:(hd + 1) * MLA_V_DIM, off:off + kb]
            part = _dot(jnp.concatenate([v, ones], axis=0), p)
            return part if acc is None else acc + part

        m_cur = None
        for i in range(nblk):
            m_cur = score_block(0, i, m_cur)
        for hd in range(heads):
            m_next, acc = None, None
            for i in range(nblk):
                if nblk == 1 and hd + 1 < heads:
                    m_next = score_block(hd + 1, i, m_next)
                acc = value_block(hd, i, m_cur, acc)
                if nblk > 1 and hd + 1 < heads:
                    m_next = score_block(hd + 1, i, m_next)
            vs = slice(hd * MLA_V_DIM, (hd + 1) * MLA_V_DIM)
            o_ref[vs, qcols] = acc[:MLA_V_DIM] / acc[MLA_V_DIM:MLA_V_DIM + 1]
            m_cur = m_next


def _attention(qt, k, vt, seq_len, heads_per_step, cache=None):
    t = k.shape[0]
    nseq = t // seq_len
    tq = min(ATTN_Q_TILE, seq_len)
    nq = seq_len // tq
    g = heads_per_step
    nss = ATTN_SEQS_PER_STEP if (nq == 1 and cache is None and nseq % ATTN_SEQS_PER_STEP == 0) else 1
    shared = nq > 1 and nq % ATTN_QTILES_PER_STEP == 0
    if shared:
        nss, nq = ATTN_QTILES_PER_STEP, nq // ATTN_QTILES_PER_STEP
    kseqs = 1 if shared else nss
    in_specs = [
        pl.BlockSpec((g * HEAD_PAD, nss * tq), lambda s, h, j: (h, s * nq + j)),
        pl.BlockSpec((kseqs * seq_len, g * HEAD_PAD), lambda s, h, j: (s, h)),
        pl.BlockSpec((g * MLA_V_DIM, kseqs * seq_len), lambda s, h, j: (h, s)),
    ]
    args = [qt, k, vt]
    n_keys = seq_len
    if cache is not None:
        kc, vct = cache
        past = kc.shape[0] // nseq
        n_keys += past
        in_specs += [
            pl.BlockSpec((past, g * HEAD_PAD), lambda s, h, j: (s, h)),
            pl.BlockSpec((g * MLA_V_DIM, past), lambda s, h, j: (h, s)),
        ]
        args += [kc, vct]
    kern = functools.partial(_attn_kernel, heads=g, has_cache=cache is not None, nseq_step=nss, shared_keys=shared)
    return pl.pallas_call(
        kern,
        grid=(nseq // kseqs, MLA_HEADS // g, nq),
        in_specs=in_specs,
        out_specs=pl.BlockSpec((g * MLA_V_DIM, nss * tq), lambda s, h, j: (h, s * nq + j)),
        out_shape=jax.ShapeDtypeStruct((MLA_WIDTH, t), F32),
        scratch_shapes=[pltpu.VMEM((n_keys, tq), F32) for _ in range(2 * nss)],
        compiler_params=_cparams("arbitrary", "arbitrary", "arbitrary"),
        name="attention",
    )(*args)


def _outproj_kernel(*refs, has_router):
    if has_router:
        (y_ref, ot_ref, x_ref, mod_ref, wa_ref, wb_ref, gm_ref, gpost_ref, gpre2_ref, rt_ref,
         x1_ref, h2_ref, comb_ref) = refs
    else:
        (y_ref, ot_ref, x_ref, mod_ref, wa_ref, wb_ref, gm_ref, gpost_ref, gpre2_ref,
         x1_ref, h2_ref) = refs
    mod = mod_ref[0]
    gate1 = mod[:, 2 * D_MODEL:3 * D_MODEL]
    shift2 = mod[:, 3 * D_MODEL:4 * D_MODEL]
    scale2 = mod[:, 4 * D_MODEL:5 * D_MODEL]
    ot = ot_ref[...]
    ms = jnp.mean(ot * ot, axis=0, keepdims=True)
    on = (ot * lax.rsqrt(ms + EPS) * gm_ref[...]).astype(BF16)
    y = _dot(y_ref[...], wa_ref[...]) + _dot_tn(on, wb_ref[...])
    x1 = x_ref[...] + gate1 * _rms_rows(y, gpost_ref[...])
    x1_ref[...] = x1
    h2 = _rms_rows(x1, gpre2_ref[...]) * (1.0 + scale2) + shift2
    h2_ref[...] = h2.astype(BF16)
    if has_router:
        hh, hm, _ = _split3(h2)
        rh, rm, _ = _split3(rt_ref[...])
        logits = _dot(hh, rh) + (_dot(hm, rh) + _dot(hh, rm))
        lane = lax.broadcasted_iota(jnp.int32, logits.shape, 1).astype(F32)
        neg = jnp.float32(-jnp.inf)
        lg = jnp.where(lane < N_EXPERTS, logits, neg)
        m1 = jnp.max(lg, axis=-1, keepdims=True)
        i1 = jnp.min(jnp.where(lg == m1, lane, float(LANE)), axis=-1, keepdims=True)
        lg2 = jnp.where(lane == i1, neg, lg)
        m2 = jnp.max(lg2, axis=-1, keepdims=True)
        i2 = jnp.min(jnp.where(lg2 == m2, lane, float(LANE)), axis=-1, keepdims=True)
        e2 = jnp.exp(m2 - m1)
        w1 = 1.0 / (1.0 + e2)
        w2 = e2 / (1.0 + e2)
        comb_ref[...] = jnp.where(lane == i1, w1, 0.0) + jnp.where(lane == i2, w2, 0.0)


def _outproj(yssd, ot, x, mod3, mod_row_fn, lw, router=None):
    t = x.shape[0]
    tm = ROW_TILE
    const = lambda i: (0, 0)
    row = lambda i: (i, 0)
    in_specs = [
        pl.BlockSpec((tm, SSD_WIDTH), row),
        pl.BlockSpec((MLA_WIDTH, tm), lambda i: (0, i)),
        pl.BlockSpec((tm, D_MODEL), row),
        pl.BlockSpec((1, 1, N_MOD * D_MODEL), lambda i: (mod_row_fn(ROW_TILE)(i), 0, 0)),
        pl.BlockSpec((SSD_WIDTH, D_MODEL), const),
        pl.BlockSpec((MLA_WIDTH, D_MODEL), const),
        pl.BlockSpec((MLA_WIDTH, 1), const),
        pl.BlockSpec((1, D_MODEL), const),
        pl.BlockSpec((1, D_MODEL), const),
    ]
    args = [yssd, ot, x, mod3, lw["w_out_a"], lw["w_out_b"], lw["mla_norm_col"],
            lw["g_post1"], lw["g_pre2"]]
    out_specs = [pl.BlockSpec((tm, D_MODEL), row), pl.BlockSpec((tm, D_MODEL), row)]
    out_shape = [jax.ShapeDtypeStruct((t, D_MODEL), F32), jax.ShapeDtypeStruct((t, D_MODEL), BF16)]
    if router is not None:
        in_specs.append(pl.BlockSpec((D_MODEL, LANE), const))
        args.append(router)
        out_specs.append(pl.BlockSpec((tm, LANE), row))
        out_shape.append(jax.ShapeDtypeStruct((t, LANE), F32))
    return pl.pallas_call(
        functools.partial(_outproj_kernel, has_router=router is not None),
        grid=(t // tm,),
        in_specs=in_specs,
        out_specs=out_specs,
        out_shape=out_shape,
        compiler_params=_cparams("arbitrary"),
        name="outproj",
    )(*args)


def _ffn_kernel(h_ref, x_ref, mod_ref, wgu_ref, wd_ref, gpost_ref, o_ref, acc_ref, *, nslab):
    e = pl.program_id(1)
    h = h_ref[...]
    f = wd_ref.shape[1]
    gu = _dot(h, wgu_ref[0])
    hid = _silu(gu[:, :f]) * gu[:, f:]
    part = _dot(hid.astype(BF16), wd_ref[0])

    @pl.when(e == 0)
    def _():
        acc_ref[...] = part

    @pl.when(e > 0)
    def _():
        acc_ref[...] += part

    @pl.when(e == nslab - 1)
    def _():
        gate2 = mod_ref[0][:, 5 * D_MODEL:6 * D_MODEL]
        o_ref[...] = x_ref[...] + gate2 * _rms_rows(acc_ref[...], gpost_ref[...])


def _ffn(h2, x1, mod3, mod_row_fn, wgu, wd, gpost):
    t = x1.shape[0]
    tm = FFN_ROW_TILE
    nslab, f, _ = wd.shape
    row = lambda i, e: (i, 0)
    in_specs = [
        pl.BlockSpec((tm, D_MODEL), row),
        pl.BlockSpec((tm, D_MODEL), row),
        pl.BlockSpec((1, 1, N_MOD * D_MODEL), lambda i, e: (mod_row_fn(tm)(i), 0, 0)),
        pl.BlockSpec((1, D_MODEL, 2 * f), lambda i, e: (e, 0, 0)),
        pl.BlockSpec((1, f, D_MODEL), lambda i, e: (e, 0, 0)),
        pl.BlockSpec((1, D_MODEL), lambda i, e: (0, 0)),
    ]
    args = [h2, x1, mod3, wgu, wd, gpost]
    return pl.pallas_call(
        functools.partial(_ffn_kernel, nslab=nslab),
        grid=(t // tm, nslab),
        in_specs=in_specs,
        out_specs=pl.BlockSpec((tm, D_MODEL), row),
        out_shape=jax.ShapeDtypeStruct((t, D_MODEL), F32),
        scratch_shapes=[pltpu.VMEM((tm, D_MODEL), F32)],
        compiler_params=_cparams("arbitrary", "arbitrary"),
        name="ffn",
    )(*args)


def _moe_kernel(h_ref, x_ref, comb_ref, mod_ref, wg_ref, wu_ref, wd_ref, gpost_ref, o_ref,
                acc_ref, rank_ref, rank_t_ref, comb_t_ref, *, nexp, sub):
    e = pl.program_id(1)
    tm = h_ref.shape[0]
    caps = MOE_CAPS
    cmax = caps[-1]

    @pl.when(e == 0)
    def _():
        acc_ref[...] = jnp.zeros_like(acc_ref)
        r_i = lax.broadcasted_iota(jnp.int32, (sub, sub), 0)
        c_i = lax.broadcasted_iota(jnp.int32, (sub, sub), 1)
        strict = jnp.where(r_i > c_i, 1.0, 0.0).astype(BF16)
        for s in range(tm // sub):
            rows = slice(s * sub, (s + 1) * sub)
            comb = comb_ref[rows, :]
            rank = _dot(strict, jnp.where(comb > 0.0, 1.0, 0.0).astype(BF16))
            rank_ref[rows, :] = rank
            rank_t_ref[:, rows] = rank.T
            comb_t_ref[:, rows] = comb.T

    def expert_pass(rows, wcol, rcol, wrow, rrow, base, cap):
        capl = -(-cap // LANE) * LANE
        slot_l = lax.broadcasted_iota(jnp.int32, (sub, capl), 1).astype(F32)
        slot_s = lax.broadcasted_iota(jnp.int32, (cap, sub), 0).astype(F32)
        gather = jnp.where(((rrow - base) == slot_s) & (wrow > 0.0), 1.0, 0.0).astype(BF16)
        scatter = jnp.where(((rcol - base) == slot_l) & (wcol > 0.0) & (slot_l < float(cap)),
                            1.0, 0.0).astype(BF16)
        xg = _dot(gather, h_ref[rows, :]).astype(BF16)
        hid = _silu(_dot(xg, wg_ref[0])) * _dot(xg, wu_ref[0])
        y = _dot(hid.astype(BF16), wd_ref[0]).astype(BF16)
        if capl > cap:
            y = jnp.concatenate([y, jnp.zeros((capl - cap, y.shape[1]), BF16)], axis=0)
        acc_ref[rows, :] += wcol * _dot(scatter, y)

    lane = lax.broadcasted_iota(jnp.int32, (sub, LANE), 1)

    def sub_tile(s, carry):
        rows = pl.ds(pl.multiple_of(s * sub, sub), sub)
        pick = lane == e
        wcol = jnp.sum(jnp.where(pick, comb_ref[rows, :], 0.0), axis=-1, keepdims=True)
        rcol = jnp.sum(jnp.where(pick, rank_ref[rows, :], 0.0), axis=-1, keepdims=True)
        wrow = comb_t_ref[pl.ds(e, 1), rows]
        rrow = rank_t_ref[pl.ds(e, 1), rows]
        count = jnp.max(jnp.where(wrow > 0.0, rrow + 1.0, 0.0))
        want = count + float(MOE_SPARE_SLOTS)
        npass = ((want + (cmax - 1.0)) * (1.0 / cmax)).astype(jnp.int32)
        nfull = npass - 1

        def full_pass(k, c):
            expert_pass(rows, wcol, rcol, wrow, rrow, (k * cmax).astype(F32), cmax)
            return c

        lax.fori_loop(0, nfull, full_pass, 0)
        base = (nfull * cmax).astype(F32)
        left = want - base
        lo = 0
        for cap in caps:
            @pl.when(jnp.logical_and(left > float(lo), left <= float(cap)))
            def _(cap=cap):
                expert_pass(rows, wcol, rcol, wrow, rrow, base, cap)
            lo = cap
        return carry

    lax.fori_loop(0, tm // sub, sub_tile, 0)

    @pl.when(e == nexp - 1)
    def _():
        gate2 = mod_ref[0][:, 5 * D_MODEL:6 * D_MODEL]
        o_ref[...] = x_ref[...] + gate2 * _rms_rows(acc_ref[...], gpost_ref[...])


def _moe(h2, x1, comb, mod3, mod_row_fn, wg, wu, wd, gpost):
    t = x1.shape[0]
    tm = min(MOE_ROW_TILE, t)
    sub = min(MOE_SUB_TILE, tm)
    nexp, f, _ = wd.shape
    row = lambda i, e: (i, 0)
    return pl.pallas_call(
        functools.partial(_moe_kernel, nexp=nexp, sub=sub),
        grid=(t // tm, nexp),
        in_specs=[
            pl.BlockSpec((tm, D_MODEL), row),
            pl.BlockSpec((tm, D_MODEL), row),
            pl.BlockSpec((tm, LANE), row),
            pl.BlockSpec((1, 1, N_MOD * D_MODEL), lambda i, e: (mod_row_fn(tm)(i), 0, 0)),
            pl.BlockSpec((1, D_MODEL, f), lambda i, e: (e, 0, 0)),
            pl.BlockSpec((1, D_MODEL, f), lambda i, e: (e, 0, 0)),
            pl.BlockSpec((1, f, D_MODEL), lambda i, e: (e, 0, 0)),
            pl.BlockSpec((1, D_MODEL), lambda i, e: (0, 0)),
        ],
        out_specs=pl.BlockSpec((tm, D_MODEL), row),
        out_shape=jax.ShapeDtypeStruct((t, D_MODEL), F32),
        scratch_shapes=[pltpu.VMEM((tm, D_MODEL), F32), pltpu.VMEM((tm, LANE), F32),
                        pltpu.VMEM((LANE, tm), F32), pltpu.VMEM((LANE, tm), F32)],
        compiler_params=_cparams("arbitrary", "arbitrary"),
        name="moe",
    )(h2, x1, comb, mod3, wg, wu, wd, gpost)


def _prep_layer(i, p):
    w_in = p["w_in"][i]
    s1 = SSD_WIDTH
    s2 = s1 + SSD_XBC
    s3 = s2 + 2 * SSD_HEADS
    s4 = s3 + MLA_Q_RANK
    s5 = s4 + MLA_KV_RANK
    w_z, w_xbc, w_dt, w_cq, w_ckv, w_kr = (w_in[:, :s1], w_in[:, s1:s2], w_in[:, s2:s3],
                                             w_in[:, s3:s4], w_in[:, s4:s5], w_in[:, s5:])
    zc = lambda n: jnp.zeros((D_MODEL, n), F32)
    tile_a = jnp.concatenate([w_dt, zc(ROPE_LANE0 - 2 * SSD_HEADS), w_kr,
                              zc(LANE - ROPE_LANE0 - MLA_ROPE_DIM)], axis=1)
    w_in_pad = jnp.concatenate([w_z, w_xbc, w_cq, w_ckv, tile_a], axis=1).astype(BF16)

    w_uq = p["w_uq"][i].reshape(MLA_Q_RANK, MLA_HEADS, MLA_NOPE_DIM + MLA_ROPE_DIM)
    q_nope, q_rope = w_uq[..., :MLA_NOPE_DIM], w_uq[..., MLA_NOPE_DIM:]
    zq = lambda n: jnp.zeros((MLA_Q_RANK, MLA_HEADS, n), F32)
    pad = HEAD_PAD - MLA_NOPE_DIM - MLA_ROPE_DIM
    w_q = jnp.concatenate([q_nope, q_rope, zq(pad)], axis=-1).reshape(MLA_Q_RANK, -1)

    w_ukv = p["w_ukv"][i].reshape(MLA_KV_RANK, MLA_HEADS, MLA_NOPE_DIM + MLA_V_DIM)
    k_nope, v_w = w_ukv[..., :MLA_NOPE_DIM], w_ukv[..., MLA_NOPE_DIM:]
    w_uk = jnp.concatenate([k_nope, jnp.zeros((MLA_KV_RANK, MLA_HEADS, HEAD_PAD - MLA_NOPE_DIM), F32)],
                           axis=-1).reshape(MLA_KV_RANK, -1)
    w_uv = v_w.reshape(MLA_KV_RANK, MLA_WIDTH)

    dtb = p["dt_bias"][i].reshape(2 * SSD_HEADS)
    alog = p["a_log"][i].reshape(2 * SSD_HEADS)
    padl = lambda v: jnp.pad(v, (0, LANE - v.shape[0])).reshape(1, LANE)
    w_out = p["w_out"][i]
    return {
        "g_pre1": p["norm_pre_mix"][i].reshape(1, D_MODEL),
        "g_post1": p["norm_post_mix"][i].reshape(1, D_MODEL),
        "g_pre2": p["norm_pre_ffn"][i].reshape(1, D_MODEL),
        "g_post2": p["norm_post_ffn"][i].reshape(1, D_MODEL),
        "w_in": w_in_pad,
        "w_dt": w_dt.T.astype(BF16),
        "dtb_row": padl(dtb),
        "dtb_col": dtb.reshape(-1, 1),
        "alog_row": padl(alog),
        "alog_col": alog.reshape(-1, 1),
        "q_norm": p["q_norm"][i].reshape(1, -1),
        "w_q": w_q.T.astype(BF16),
        "kv_norm": p["kv_norm"][i].reshape(1, -1),
        "w_uk": w_uk.astype(BF16),
        "w_uv": w_uv.T.astype(BF16),
        "conv_w": p["conv_w"][i],
        "conv_b": p["conv_b"][i].reshape(1, -1),
        "dskip_row": jnp.repeat(p["d_skip"][i], SSD_HEAD_DIM).reshape(1, -1),
        "ssd_norm": p["ssd_norm"][i].reshape(1, -1),
        "mla_norm_col": p["mla_norm"][i].reshape(-1, 1),
        "w_out_a": w_out[:SSD_WIDTH].astype(BF16),
        "w_out_b": w_out[SSD_WIDTH:].astype(BF16),
    }


def _rope_tables(n_tokens):
    rows = n_tokens // GRID_W
    row = np.repeat(np.arange(rows, dtype=np.float32), GRID_W)
    col = np.tile(np.arange(GRID_W, dtype=np.float32), rows)
    half = MLA_ROPE_DIM // 2
    inv = (np.float32(ROPE_THETA) ** (-np.arange(0, half, 2, dtype=np.float32) / np.float32(half))).astype(np.float32)
    ar = row[:, None] * inv[None, :]
    ac = col[:, None] * inv[None, :]
    ang = np.concatenate([ar, ar, ac, ac], axis=-1).astype(np.float32)
    return jnp.asarray(np.cos(ang), F32), jnp.asarray(np.sin(ang), F32)


def _attn_tables(cos, sin, n):
    scale = (MLA_NOPE_DIM + MLA_ROPE_DIM) ** -0.5 * math.log2(math.e)
    pad = HEAD_PAD - ROPE_LANE0 - MLA_ROPE_DIM
    cosk = jnp.concatenate([jnp.zeros((n, ROPE_LANE0), F32), cos, jnp.zeros((n, pad), F32)], axis=1)
    sink = jnp.concatenate([jnp.zeros((n, ROPE_LANE0), F32), sin, jnp.zeros((n, pad), F32)], axis=1)
    cosq = jnp.concatenate([jnp.ones((n, ROPE_LANE0), F32), cos, jnp.zeros((n, pad), F32)], axis=1)
    return (cosq * scale).T, (sink * scale).T, cosk, sink


def kernel(x_prompt, x_sample, cache_ckv, cache_krope, state_ssm, c, c_ctx, w_mod, b_mod, norm_pre_mix, norm_post_mix, norm_pre_ffn, norm_post_ffn, w_in, conv_w, conv_b, dt_bias, a_log, d_skip, ssd_norm, q_norm, w_uq, kv_norm, w_ukv, mla_norm, w_out, ffn_w_gate, ffn_w_up, ffn_w_down, moe_router, moe_w_gate, moe_w_up, moe_w_down):
    params = dict(w_in=w_in, conv_w=conv_w, conv_b=conv_b, dt_bias=dt_bias, a_log=a_log, d_skip=d_skip,
                  ssd_norm=ssd_norm, q_norm=q_norm, w_uq=w_uq, kv_norm=kv_norm, w_ukv=w_ukv,
                  mla_norm=mla_norm, w_out=w_out, norm_pre_mix=norm_pre_mix, norm_post_mix=norm_post_mix,
                  norm_pre_ffn=norm_pre_ffn, norm_post_ffn=norm_post_ffn)
    batch, seq, d = x_prompt.shape
    dec_batch, dec_seq, _ = x_sample.shape
    depth = w_in.shape[0]
    past = cache_ckv.shape[2]
    tm = ROW_TILE

    cvec = jnp.concatenate([c_ctx[None, :], c, jnp.zeros((SUBLANE - 1 - dec_batch, d), F32)], axis=0)
    mod = _modulation(cvec, w_mod, b_mod)

    ones = jnp.ones((tm, MLA_ROPE_DIM), F32)
    tabs_ctx = _attn_tables(ones, jnp.zeros_like(ones), tm)
    cos, sin = _rope_tables(dec_seq)
    tabs_lat = _attn_tables(cos, sin, dec_seq)
    lat_blocks = dec_seq // tm

    xp = x_prompt.reshape(batch * seq, d)
    xs = x_sample.reshape(dec_batch * dec_seq, d)
    cache_bufs, ssm_buf = None, None
    for i in range(depth):
        lw = _prep_layer(i, params)
        mod3 = mod[i].reshape(SUBLANE, 1, N_MOD * d)
        j = i // 2
        if i % 2 == 0:
            f = ffn_w_gate.shape[2] // FFN_SLABS
            wgu = jnp.stack([jnp.concatenate([ffn_w_gate[j][:, s * f:(s + 1) * f], ffn_w_up[j][:, s * f:(s + 1) * f]],
                                             axis=1) for s in range(FFN_SLABS)], axis=0).astype(BF16)
            wd = ffn_w_down[j].reshape(FFN_SLABS, f, d).astype(BF16)
            router = None
        else:
            wgu = (moe_w_gate[j].astype(BF16), moe_w_up[j].astype(BF16))
            wd = moe_w_down[j].astype(BF16)
            router = jnp.pad(moe_router[j], ((0, 0), (0, LANE - N_EXPERTS)))

        def run(x, row_fn, tabs, tab_blocks, seq_len, heads_per_step, ctx):
            cache_out = (i, depth, seq_len, cache_bufs) if ctx is None else None
            z, xbc, dt, dtt, qt, k, vt, *caches = _inproj(x, mod3, row_fn, lw, tabs, tab_blocks, cache_out)
            cps = seq_len // SSD_CHUNK
            h0 = None
            if ctx is not None:
                h0 = (_state_to_kernel_layout(ctx[2][:, 0]), _state_to_kernel_layout(ctx[2][:, 1]))
            sink = (2 * i, 2 * depth, ssm_buf) if ctx is None else (0, 2, None)
            yssd, fin = _ssd(xbc, dt, dtt, z, lw, cps, h0=h0, sink=sink)
            cache = None
            if ctx is not None:
                kr_tile = jnp.pad(ctx[1].reshape(-1, MLA_ROPE_DIM),
                                  ((0, 0), (ROPE_LANE0, HEAD_PAD - ROPE_LANE0 - MLA_ROPE_DIM)))
                cache = _kvcache(ctx[0].reshape(-1, MLA_KV_RANK), kr_tile, lw)
            ot = _attention(qt, k, vt, seq_len, heads_per_step, cache=cache)
            outs = _outproj(yssd, ot, x, mod3, row_fn, lw, router=router)
            x1, h2 = outs[0], outs[1]
            comb = outs[2] if router is not None else None
            if comb is None:
                x2 = _ffn(h2, x1, mod3, row_fn, wgu, wd, lw["g_post2"])
            else:
                x2 = _moe(h2, x1, comb, mod3, row_fn, *wgu, wd, lw["g_post2"])
            return x2, caches, fin

        xp, cache_bufs, ssm_buf = run(xp, lambda tile: (lambda b: 0), tabs_ctx, 1, seq, MLA_HEADS, None)
        xs, _, _ = run(xs, lambda tile: (lambda b: 1 + (b * tile) // dec_seq), tabs_lat, lat_blocks, dec_seq, MLA_HEADS,
                             (cache_ckv[:, i], cache_krope[:, i], state_ssm[:, i]))
    return (xp.reshape(batch, seq, d), xs.reshape(dec_batch, dec_seq, d),
            cache_bufs[0], cache_bufs[1],
            ssm_buf.reshape(batch, depth, 2, SSD_HEADS, SSD_HEAD_DIM, SSD_STATE))
```
